```python
import math
import jax, jax.numpy as jnp
from jax import lax
import numpy as np

D_MODEL = 1024
BATCH = 2
SEQ = 8192
DEPTH = 1
DEC_BATCH = 128
DEC_SEQ = 8
PAST_LEN = 16384
PAGE_SIZE = 128

DN_HEADS = 4
DN_DK = 128
DN_DV = 128
DN_CONV = 4
DN_CHUNK = 64
SWA_HEADS = 8
SWA_KV_HEADS = 2
SWA_GROUP = SWA_HEADS // SWA_KV_HEADS
SWA_HD = 64
WINDOW = 128
SWA_BLOCK = 128
N_BUCKETS = 32
MAX_DISTANCE = 128
D_FF = 2816
FFN_CONV = 3
EPS = 1e-6
NEG_INF = -1e30

DN_QK = DN_HEADS * DN_DK
DN_V = DN_HEADS * DN_DV
DN_CONV_CH = 2 * DN_QK + DN_V
SWA_Q = SWA_HEADS * SWA_HD
SWA_KV = SWA_KV_HEADS * SWA_HD
D_MIX = DN_V + SWA_Q
IN_SPLITS = (DN_CONV_CH, DN_V, DN_HEADS, DN_HEADS, SWA_Q, SWA_KV, SWA_KV)
IN_COLS = sum(IN_SPLITS)

kernel_name = 'hymba_gdn_swa_convffn_step'


def _split_cols(t, sizes):
    idx = [int(i) for i in np.cumsum(sizes)[:-1]]
    return jnp.split(t, idx, axis=-1)


def _rmsnorm(x, w):
    xf = x.astype(jnp.float32)
    y = xf * lax.rsqrt(jnp.mean(xf * xf, axis=-1, keepdims=True) + EPS)
    return (y * w.astype(jnp.float32)).astype(x.dtype)


def _l2norm(t):
    t = t.astype(jnp.float32)
    return t * lax.rsqrt(jnp.sum(t * t, axis=-1, keepdims=True) + EPS)


def _causal_dwconv(x, buf, w):
    width = w.shape[0]
    L = x.shape[1]
    xp = jnp.concatenate([buf.astype(x.dtype), x], axis=1)
    out = xp[:, 0:L] * w[0]
    for j in range(1, width):
        out = out + xp[:, j:j + L] * w[j]
    return out, xp[:, L:]


def _t5_bucket(dist):
    d = jnp.maximum(dist, 0)
    exact = N_BUCKETS // 2
    logv = jnp.log(jnp.maximum(d, 1).astype(jnp.float32) / exact) / math.log(MAX_DISTANCE / exact)
    large = jnp.minimum(exact + (logv * (N_BUCKETS - exact)).astype(jnp.int32), N_BUCKETS - 1)
    return jnp.where(d < exact, d, large)


def _gated_delta_chunked(q, k, v, g, beta, S0):
    B, L, H, dk = q.shape
    dv = v.shape[-1]
    C = min(DN_CHUNK, L)
    pad = (-L) % C
    if pad:
        pw = lambda t: jnp.pad(t, [(0, 0), (0, pad)] + [(0, 0)] * (t.ndim - 2))
        q, k, v, g, beta = pw(q), pw(k), pw(v), pw(g), pw(beta)
    NC = (L + pad) // C

    def chunks(t):
        return jnp.moveaxis(t.reshape((B, NC, C) + t.shape[2:]), 3, 1)

    qc, kc, vc, gc, bc = chunks(q), chunks(k), chunks(v), chunks(g), chunks(beta)
    G = jnp.cumsum(gc, axis=-1)
    incl = jnp.tril(jnp.ones((C, C), bool))
    strict = jnp.tril(jnp.ones((C, C), bool), -1)
    decay = jnp.where(incl, jnp.exp(jnp.where(incl, G[..., :, None] - G[..., None, :], 0.0)), 0.0)
    kk = jnp.einsum('bhnid,bhnjd->bhnij', kc, kc)
    A = jnp.where(strict, bc[..., :, None] * kk * decay, 0.0)
    eye = jnp.eye(C, dtype=jnp.float32)
    T = lax.linalg.triangular_solve(A + eye, jnp.broadcast_to(eye, A.shape),
                                    left_side=True, lower=True, unit_diagonal=True)
    eG = jnp.exp(G)
    w_v = jnp.einsum('bhnij,bhnjd->bhnid', T, vc * bc[..., None])
    w_k = jnp.einsum('bhnij,bhnjd->bhnid', T, kc * (bc * eG)[..., None])
    qk = jnp.einsum('bhnid,bhnjd->bhnij', qc, kc) * decay
    q_dec = qc * eG[..., None]
    k_tail = kc * jnp.exp(G[..., -1:] - G)[..., None]
    c_dec = jnp.exp(G[..., -1])
    xs = tuple(jnp.moveaxis(t, 2, 0) for t in (w_v, w_k, qk, q_dec, k_tail, c_dec))

    def step(S, inp):
        wv, wk, qk_i, qd, kt, cd = inp
        u = wv - jnp.einsum('bhik,bhkv->bhiv', wk, S)
        o = jnp.einsum('bhik,bhkv->bhiv', qd, S) + jnp.einsum('bhij,bhjv->bhiv', qk_i, u)
        S = S * cd[..., None, None] + jnp.einsum('bhik,bhiv->bhkv', kt, u)
        return S, o

    S_fin, o = lax.scan(step, S0, xs)
    o = jnp.moveaxis(jnp.moveaxis(o, 0, 2), 1, 3).reshape(B, NC * C, H, dv)[:, :L]
    return o, S_fin


def _deltanet_mixer(qkv, z, b_raw, a_raw, conv_buf, S0, conv_w, A_log, dt_bias, norm_w):
    B, L, _ = qkv.shape
    c_out, new_buf = _causal_dwconv(qkv, conv_buf, conv_w)
    c_out = jax.nn.silu(c_out)
    q, k, v = jnp.split(c_out, [DN_QK, 2 * DN_QK], axis=-1)
    q = _l2norm(q.reshape(B, L, DN_HEADS, DN_DK)) * (DN_DK ** -0.5)
    k = _l2norm(k.reshape(B, L, DN_HEADS, DN_DK))
    v = v.reshape(B, L, DN_HEADS, DN_DV).astype(jnp.float32)
    beta = jax.nn.sigmoid(b_raw.astype(jnp.float32))
    g = -jnp.exp(A_log.astype(jnp.float32)) * jax.nn.softplus(a_raw.astype(jnp.float32) + dt_bias.astype(jnp.float32))
    o, S = _gated_delta_chunked(q, k, v, g, beta, S0.astype(jnp.float32))
    o = _rmsnorm(o, norm_w) * jax.nn.silu(z.reshape(B, L, DN_HEADS, DN_DV).astype(jnp.float32))
    return o.reshape(B, L, DN_V).astype(qkv.dtype), new_buf, S.astype(S0.dtype)


def _swa_attend(q, k, v, q_pos, k_pos, sinks, rel_bias):
    s = jnp.einsum('bnihgd,bnjhd->bnhgij', q, k).astype(jnp.float32) * (SWA_HD ** -0.5)
    dist = q_pos[:, :, None] - k_pos[:, None, :]
    valid = (dist >= 0) & (dist < WINDOW) & (k_pos[:, None, :] >= 0)
    nb, lq, lk = dist.shape
    bias = rel_bias.astype(jnp.float32)[_t5_bucket(dist)]
    bias = jnp.transpose(bias.reshape(nb, lq, lk, SWA_KV_HEADS, SWA_GROUP), (0, 3, 4, 1, 2))
    s = jnp.where(valid[:, None, None], s + bias, NEG_INF)
    sink = sinks.astype(jnp.float32).reshape(SWA_KV_HEADS, SWA_GROUP)[:, :, None, None]
    m = jnp.maximum(jnp.max(s, axis=-1, keepdims=True), sink)
    p = jnp.exp(s - m)
    p = p / (jnp.sum(p, axis=-1, keepdims=True) + jnp.exp(sink - m))
    return jnp.einsum('bnhgij,bnjhd->bnihgd', p.astype(v.dtype), v)


def _swa_prompt(q, k, v, sinks, rel_bias):
    B, L = q.shape[:2]
    NB = L // SWA_BLOCK
    qb = q.reshape(B, NB, SWA_BLOCK, SWA_KV_HEADS, SWA_GROUP, SWA_HD)
    kb = k.reshape(B, NB, SWA_BLOCK, SWA_KV_HEADS, SWA_HD)
    vb = v.reshape(B, NB, SWA_BLOCK, SWA_KV_HEADS, SWA_HD)

    def with_prev(t):
        prev = jnp.concatenate([jnp.zeros_like(t[:, :1]), t[:, :-1]], axis=1)
        return jnp.concatenate([prev, t], axis=2)

    blk = jnp.arange(NB, dtype=jnp.int32)[:, None] * SWA_BLOCK
    q_pos = blk + jnp.arange(SWA_BLOCK, dtype=jnp.int32)[None]
    k_pos = blk + jnp.arange(-SWA_BLOCK, SWA_BLOCK, dtype=jnp.int32)[None]
    o = _swa_attend(qb, with_prev(kb), with_prev(vb), q_pos, k_pos, sinks, rel_bias)
    n_buf = min(WINDOW, L)
    return o.reshape(B, L, SWA_Q), k[:, L - n_buf:], v[:, L - n_buf:]


def _swa_sample(q, k, v, k_buf, v_buf, sinks, rel_bias):
    B, L = q.shape[:2]
    n_buf = k_buf.shape[1]
    keys = jnp.concatenate([k_buf.astype(k.dtype), k], axis=1)
    vals = jnp.concatenate([v_buf.astype(v.dtype), v], axis=1)
    q_pos = (PAST_LEN + jnp.arange(L, dtype=jnp.int32))[None]
    k_pos = (PAST_LEN - n_buf + jnp.arange(n_buf + L, dtype=jnp.int32))[None]
    o = _swa_attend(q[:, None], keys[:, None], vals[:, None], q_pos, k_pos, sinks, rel_bias)
    return o.reshape(B, L, SWA_Q), keys[:, L:], vals[:, L:]


def _trunk(x, c, states, layer_w, rel_bias, final_norm_w):
    (w_ada, b_ada, norm_mix_w, w_in, dn_conv_w, dn_A_log, dn_dt_bias, dn_norm_w,
     swa_sinks, w_out, norm_ffn_w, ffn_w_up, ffn_conv_w, ffn_conv_b, ffn_w_down) = layer_w
    B, L, _ = x.shape
    is_sample = states is not None
    outs = ([], [], [], [], [])
    for i in range(DEPTH):
        mod = jnp.einsum('bd,de->be', jax.nn.silu(c), w_ada[i]) + b_ada[i]
        sh1, sc1, g1, sh2, sc2, g2 = jnp.split(mod[:, None, :], 6, axis=-1)
        h = _rmsnorm(x, norm_mix_w[i]) * (1 + sc1) + sh1
        proj = jnp.einsum('bld,de->ble', h, w_in[i])
        qkv, z, b_raw, a_raw, sq, sk, sv = _split_cols(proj, IN_SPLITS)
        if is_sample:
            dn_buf, S0, k_buf, v_buf, f_buf = (s[i] for s in states)
        else:
            dn_buf = jnp.zeros((B, DN_CONV - 1, DN_CONV_CH), x.dtype)
            S0 = jnp.zeros((B, DN_HEADS, DN_DK, DN_DV), jnp.float32)
            f_buf = jnp.zeros((B, FFN_CONV - 1, 2 * D_FF), x.dtype)
        o_dn, n_dn_buf, n_S = _deltanet_mixer(qkv, z, b_raw, a_raw, dn_buf, S0, dn_conv_w[i],
                                              dn_A_log[i], dn_dt_bias[i], dn_norm_w[i])
        sq = sq.reshape(B, L, SWA_KV_HEADS, SWA_GROUP, SWA_HD)
        sk = sk.reshape(B, L, SWA_KV_HEADS, SWA_HD)
        sv = sv.reshape(B, L, SWA_KV_HEADS, SWA_HD)
        if is_sample:
            o_swa, n_k, n_v = _swa_sample(sq, sk, sv, k_buf, v_buf, swa_sinks[i], rel_bias)
        else:
            o_swa, n_k, n_v = _swa_prompt(sq, sk, sv, swa_sinks[i], rel_bias)
        mix = jnp.concatenate([o_dn, o_swa.astype(o_dn.dtype)], axis=-1)
        x = x + g1 * jnp.einsum('ble,ed->bld', mix, w_out[i])
        h = _rmsnorm(x, norm_ffn_w[i]) * (1 + sc2) + sh2
        u = jnp.einsum('bld,df->blf', h, ffn_w_up[i])
        u_c, n_f_buf = _causal_dwconv(u, f_buf, ffn_conv_w[i])
        gate, up = jnp.split(u_c + ffn_conv_b[i], 2, axis=-1)
        x = x + g2 * jnp.einsum('blf,fd->bld', jax.nn.silu(gate) * up, ffn_w_down[i])
        for lst, val in zip(outs, (n_dn_buf, n_S, n_k, n_v, n_f_buf)):
            lst.append(val)
    y = _rmsnorm(x, final_norm_w)
    return y, [jnp.stack(lst) for lst in outs]


def setup_inputs(seed: int = 0) -> dict:
    key = jax.random.key(seed)
    ks = jax.random.split(key, 32)
    f32 = jnp.float32
    nrm = lambda k, shape, s: jax.random.normal(k, shape, f32) * s
    n_buf = min(WINDOW, PAST_LEN)
    dt = jnp.exp(jax.random.uniform(ks[20], (DEPTH, DN_HEADS), f32, math.log(1e-3), math.log(1e-1)))
    return {
        'x_prompt': nrm(ks[0], (BATCH, SEQ, D_MODEL), 1.0),
        'x_sample': nrm(ks[1], (DEC_BATCH, DEC_SEQ, D_MODEL), 1.0),
        'state_dn_conv': nrm(ks[2], (DEPTH, DEC_BATCH, DN_CONV - 1, DN_CONV_CH), 1.0),
        'state_dn_ssm': nrm(ks[3], (DEPTH, DEC_BATCH, DN_HEADS, DN_DK, DN_DV), DN_DK ** -0.5),
        'cache_swa_k': nrm(ks[4], (DEPTH, DEC_BATCH, n_buf, SWA_KV_HEADS, SWA_HD), 1.0),
        'cache_swa_v': nrm(ks[5], (DEPTH, DEC_BATCH, n_buf, SWA_KV_HEADS, SWA_HD), 1.0),
        'state_ffn_conv': nrm(ks[6], (DEPTH, DEC_BATCH, FFN_CONV - 1, 2 * D_FF), 1.0),
        'c_prompt': nrm(ks[7], (BATCH, D_MODEL), 1.0),
        'c_sample': nrm(ks[8], (DEC_BATCH, D_MODEL), 1.0),
        'rel_bias': nrm(ks[9], (N_BUCKETS, SWA_HEADS), 0.5),
        'final_norm_w': 1.0 + nrm(ks[10], (D_MODEL,), 0.05),
        'w_ada': nrm(ks[11], (DEPTH, D_MODEL, 6 * D_MODEL), 0.5 * D_MODEL ** -0.5),
        'b_ada': nrm(ks[12], (DEPTH, 6 * D_MODEL), 0.02),
        'norm_mix_w': 1.0 + nrm(ks[13], (DEPTH, D_MODEL), 0.05),
        'w_in': nrm(ks[14], (DEPTH, D_MODEL, IN_COLS), D_MODEL ** -0.5),
        'dn_conv_w': nrm(ks[15], (DEPTH, DN_CONV, DN_CONV_CH), DN_CONV ** -0.5),
        'dn_A_log': jnp.log(jax.random.uniform(ks[16], (DEPTH, DN_HEADS), f32, 1.0, 16.0)),
        'dn_dt_bias': dt + jnp.log(-jnp.expm1(-dt)),
        'dn_norm_w': 1.0 + nrm(ks[17], (DEPTH, DN_DV), 0.05),
        'swa_sinks': nrm(ks[18], (DEPTH, SWA_HEADS), 0.5),
        'w_out': nrm(ks[19], (DEPTH, D_MIX, D_MODEL), D_MIX ** -0.5),
        'norm_ffn_w': 1.0 + nrm(ks[21], (DEPTH, D_MODEL), 0.05),
        'ffn_w_up': nrm(ks[22], (DEPTH, D_MODEL, 2 * D_FF), D_MODEL ** -0.5),
        'ffn_conv_w': nrm(ks[23], (DEPTH, FFN_CONV, 2 * D_FF), FFN_CONV ** -0.5),
        'ffn_conv_b': nrm(ks[24], (DEPTH, 2 * D_FF), 0.02),
        'ffn_w_down': nrm(ks[25], (DEPTH, D_FF, D_MODEL), D_FF ** -0.5),
    }


def reference(x_prompt, x_sample, state_dn_conv, state_dn_ssm, cache_swa_k, cache_swa_v, state_ffn_conv,
              c_prompt, c_sample, rel_bias, final_norm_w, w_ada, b_ada, norm_mix_w, w_in, dn_conv_w,
              dn_A_log, dn_dt_bias, dn_norm_w, swa_sinks, w_out, norm_ffn_w, ffn_w_up, ffn_conv_w,
              ffn_conv_b, ffn_w_down):
    layer_w = (w_ada, b_ada, norm_mix_w, w_in, dn_conv_w, dn_A_log, dn_dt_bias, dn_norm_w,
               swa_sinks, w_out, norm_ffn_w, ffn_w_up, ffn_conv_w, ffn_conv_b, ffn_w_down)
    y_prompt, (p_dn_conv, p_dn_ssm, p_swa_k, p_swa_v, p_ffn_conv) = _trunk(
        x_prompt, c_prompt, None, layer_w, rel_bias, final_norm_w)
    sample_states = (state_dn_conv, state_dn_ssm, cache_swa_k, cache_swa_v, state_ffn_conv)
    y_sample, (s_dn_conv, s_dn_ssm, s_swa_k, s_swa_v, s_ffn_conv) = _trunk(
        x_sample, c_sample, sample_states, layer_w, rel_bias, final_norm_w)
    return (y_prompt, y_sample, p_dn_conv, s_dn_conv, p_dn_ssm, s_dn_ssm, p_swa_k, s_swa_k,
            p_swa_v, s_swa_v, p_ffn_conv, s_ffn_conv)
```

```python
import functools
import math

import numpy as np
import jax
import jax.numpy as jnp
from jax import lax
from jax.experimental import pallas as pl
from jax.experimental.pallas import tpu as pltpu

F32 = jnp.float32
BF16 = jnp.bfloat16

D_MODEL = 1024
PAST_LEN = 16384
DN_HEADS = 4
DN_DK = 128
DN_DV = 128
DN_CONV = 4
SWA_HEADS = 8
SWA_KV_HEADS = 2
SWA_GROUP = SWA_HEADS // SWA_KV_HEADS
SWA_HD = 64
WINDOW = 128
N_BUCKETS = 32
MAX_DISTANCE = 128
D_FF = 2816
FFN_CONV = 3
EPS = 1e-6
NEG_INF = -1e30

DN_QK = DN_HEADS * DN_DK
DN_V = DN_HEADS * DN_DV
DN_CONV_CH = 2 * DN_QK + DN_V
SWA_Q = SWA_HEADS * SWA_HD
SWA_KV = SWA_KV_HEADS * SWA_HD
BA_PAD = 128
UNIT = 128
FFN_CHUNK = 256
VMEM_LIMIT = 56 * 1024 * 1024


def _cparams(sem):
    return pltpu.CompilerParams(dimension_semantics=sem, vmem_limit_bytes=VMEM_LIMIT)


def _bf(x):
    return x.astype(BF16)


def _dot(a, b):
    return jnp.dot(_bf(a), _bf(b), preferred_element_type=F32)


def _dot_nt(a, b):
    return lax.dot_general(_bf(a), _bf(b), (((1,), (1,)), ((), ())), preferred_element_type=F32)


def _split2(x):
    hi = _bf(x)
    lo = _bf(x - hi.astype(F32))
    return hi, lo


def _dot_x3(a, b):
    ah, al = _split2(a)
    bh, bl = _split2(b)
    d = functools.partial(jnp.dot, preferred_element_type=F32)
    return d(ah, bh) + (d(al, bh) + d(ah, bl))


def _dot_mask(m, x):
    hi = _bf(x)
    r = x - hi.astype(F32)
    mid = _bf(r)
    lo = _bf(r - mid.astype(F32))
    d = functools.partial(jnp.dot, preferred_element_type=F32)
    return d(m, hi) + (d(m, mid) + d(m, lo))


def _sigmoid(x):
    return 1.0 / (1.0 + jnp.exp(-x))


def _silu(x):
    return x * _sigmoid(x)


def _softplus(x):
    return jnp.maximum(x, 0.0) + jnp.log1p(jnp.exp(-jnp.abs(x)))


def _rms(x, w):
    ms = jnp.mean(x * x, axis=-1, keepdims=True)
    return x * lax.rsqrt(ms + EPS) * w


def _l2norm(t):
    return t * lax.rsqrt(jnp.sum(t * t, axis=-1, keepdims=True) + EPS)


def _rows(m3, nb, lt):
    return jnp.broadcast_to(m3, (nb, lt, m3.shape[-1])).reshape(nb * lt, m3.shape[-1])


def _causal_conv(x, prev, w, nb, lt):
    width = w.shape[0]
    rows, ch = x.shape
    tmod = lax.broadcasted_iota(jnp.int32, (rows, 1), 0) & (lt - 1)
    out = x * w[width - 1:width, :]
    for j in range(1, width):
        sh = pltpu.roll(x, j, axis=0)
        for t in range(j):
            p = width - 1 - j + t
            sh = jnp.where(tmod == t, _rows(prev[:, p:p + 1, :], nb, lt), sh)
        out = out + sh * w[width - 1 - j:width - j, :]
    return out


def _mod_kernel(c_ref, w_ref, b_ref, o_ref):
    o_ref[...] = _dot_x3(_silu(c_ref[...]), w_ref[...]) + b_ref[...]


def _mod_call(c_all, w_ada, b_ada):
    rows = c_all.shape[0]
    cols = w_ada.shape[1]
    tile = 512
    return pl.pallas_call(
        _mod_kernel,
        grid=(cols // tile,),
        in_specs=[pl.BlockSpec((rows, D_MODEL), lambda j: (0, 0)),
                  pl.BlockSpec((D_MODEL, tile), lambda j: (0, j)),
                  pl.BlockSpec((1, tile), lambda j: (0, j))],
        out_specs=pl.BlockSpec((rows, tile), lambda j: (0, j)),
        out_shape=jax.ShapeDtypeStruct((rows, cols), F32),
        compiler_params=_cparams(("arbitrary",)),
        name="adaln_mod",
    )(c_all, w_ada, b_ada)


IN_SPLIT = (DN_CONV_CH, DN_V, BA_PAD, SWA_Q, SWA_KV, SWA_KV)
IN_COLS_PAD = sum(IN_SPLIT)


def _in_kernel(x_ref, mod_ref, nw_ref, w_ref, *out_refs, nb, lt):
    x = x_ref[...]
    ms = jnp.mean(x * x, axis=-1, keepdims=True)
    y = x * lax.rsqrt(ms + EPS) * nw_ref[...]
    h = y * (1.0 + mod_ref[:, 1:2, :]) + mod_ref[:, 0:1, :]
    h = _bf(h.reshape(nb * lt, D_MODEL))
    off = 0
    for o_ref, n in zip(out_refs, IN_SPLIT):
        o_ref[...] = jnp.dot(h, w_ref[:, off:off + n], preferred_element_type=F32)
        off += n


def _in_call(x, mod3, norm_w, w_cat, nb, lt):
    bsz, seq, _ = x.shape
    nt = seq // lt
    rows = nb * lt
    n_tok = bsz * seq
    row_map = lambda i, j: (i * nt + j, 0)
    return pl.pallas_call(
        functools.partial(_in_kernel, nb=nb, lt=lt),
        grid=(bsz // nb, nt),
        in_specs=[pl.BlockSpec((nb, lt, D_MODEL), lambda i, j: (i, j, 0)),
                  pl.BlockSpec((nb, 6, D_MODEL), lambda i, j: (i, 0, 0)),
                  pl.BlockSpec((1, D_MODEL), lambda i, j: (0, 0)),
                  pl.BlockSpec((D_MODEL, IN_COLS_PAD), lambda i, j: (0, 0))],
        out_specs=[pl.BlockSpec((rows, n), row_map) for n in IN_SPLIT],
        out_shape=[jax.ShapeDtypeStruct((n_tok, n), F32) for n in IN_SPLIT],
        compiler_params=_cparams(("arbitrary", "arbitrary")),
        name="norm_in_proj",
    )(x, mod3, norm_w, w_cat)


INV_BASE_SHIFT = 3


def _unit_lower_inverse(a_mat, ri, ci, chunk_shift):
    def blocks(s):
        return (ri >> s) == (ci >> s)

    base = min(INV_BASE_SHIFT, chunk_shift)
    n = jnp.where(blocks(base), -a_mat, 0.0)
    t = jnp.where(ri == ci, 1.0, n)
    x = n
    for lvl in range(1, base):
        x = _dot_x3(x, x)
        t = t + _dot_x3(t, x)
    for s in range(base, chunk_shift):
        e = jnp.where(blocks(s + 1) & jnp.logical_not(blocks(s)), a_mat, 0.0)
        t = t - _dot_x3(t, _dot_x3(e, t))
    return t


def _dn_kernel(*refs, nb, chunk, carry):
    if carry:
        (qkv_ref, z_ref, ba_ref, cw_ref, alog_ref, dt_ref, nw_ref, o_ref, s_ref, cst_ref) = refs
        s0_ref = s_ref

        @pl.when(pl.program_id(1) == 0)
        def _():
            cst_ref[...] = jnp.zeros_like(cst_ref)
            s_ref[...] = jnp.zeros_like(s_ref)
    else:
        (qkv_ref, z_ref, ba_ref, cw_ref, alog_ref, dt_ref, nw_ref, cst_ref, s0_ref, o_ref, s_ref) = refs

    ri = lax.broadcasted_iota(jnp.int32, (UNIT, UNIT), 0)
    ci = lax.broadcasted_iota(jnp.int32, (UNIT, UNIT), 1)
    shift = int(math.log2(chunk))
    same = (ri >> shift) == (ci >> shift)
    incl = same & (ri >= ci)
    strict = same & (ri > ci)

    prev = cst_ref[...]
    x = qkv_ref[...]
    cw = cw_ref[...]

    ba = ba_ref[...]
    beta_full = _sigmoid(ba)
    g_full = -jnp.exp(alog_ref[...]) * _softplus(ba + dt_ref[...])
    masks = jnp.concatenate([jnp.where(incl, 1.0, 0.0), jnp.where(same, 1.0, 0.0)], axis=0).astype(BF16)
    gsum = _dot_mask(masks, g_full)
    g_cum = gsum[:UNIT]
    g_tot = gsum[UNIT:]
    g_cum_t = g_cum.T

    def head_cols(base, h):
        lo = base + h * DN_DK
        return _silu(_causal_conv(x[:, lo:lo + DN_DK], prev[:, :, lo:lo + DN_DK], cw[:, lo:lo + DN_DK], nb, chunk))

    for h in range(DN_HEADS):
        q = _l2norm(head_cols(0, h)) * (DN_DK ** -0.5)
        k = _l2norm(head_cols(DN_QK, h))
        v = head_cols(2 * DN_QK, h)
        gc = g_cum[:, DN_HEADS + h:DN_HEADS + h + 1]
        gr = g_cum_t[DN_HEADS + h:DN_HEADS + h + 1, :]
        gt = g_tot[:, DN_HEADS + h:DN_HEADS + h + 1]
        bc = beta_full[:, h:h + 1]
        decay = jnp.where(incl, jnp.exp(jnp.where(incl, gc - gr, 0.0)), 0.0)
        e_g = jnp.exp(gc)
        kq = _dot_nt(jnp.concatenate([k, q], axis=0), k)
        kk = kq[:UNIT]
        qk = kq[UNIT:] * decay
        a_mat = jnp.where(strict, bc * kk * decay, 0.0)
        t_inv = _unit_lower_inverse(a_mat, ri, ci, shift)
        wvk = _dot(t_inv, jnp.concatenate([v * bc, k * (bc * e_g)], axis=1))
        w_v = wvk[:, :DN_DV]
        w_k = wvk[:, DN_DV:]
        q_dec = q * e_g
        k_tail_t = (k * jnp.exp(gt - gc)).T
        c_dec = jnp.exp(gt)

        if carry:
            s_old = s_ref[0, h]
            r = _dot(jnp.concatenate([w_k, q_dec], axis=0), s_old)
            u = w_v - r[:UNIT]
            o = r[UNIT:] + _dot(qk, u)
            s_ref[0, h] = s_old * c_dec[0:1, :] + _dot(k_tail_t, u)
        else:
            us, qs = [], []
            for s in range(nb):
                lo = s * chunk
                lhs = jnp.concatenate([w_k[lo:lo + chunk], q_dec[lo:lo + chunk]], axis=0)
                r = jnp.dot(lhs, s0_ref[s, h], preferred_element_type=F32)
                us.append(w_v[lo:lo + chunk] - r[:chunk])
                qs.append(r[chunk:])
            u = jnp.concatenate(us, axis=0)
            o = jnp.concatenate(qs, axis=0) + _dot(qk, u)
            for s in range(nb):
                lo = s * chunk
                upd = jnp.dot(k_tail_t[:, lo:lo + chunk], u[lo:lo + chunk], preferred_element_type=F32)
                s_ref[s, h] = s0_ref[s, h] * c_dec[lo:lo + 1, :] + upd

        zz = z_ref[:, h * DN_DV:(h + 1) * DN_DV]
        o_ref[:, h * DN_DV:(h + 1) * DN_DV] = _rms(o, nw_ref[...]) * _silu(zz)

    if carry:
        cst_ref[0] = x[UNIT - (DN_CONV - 1):, :]


def _dn_call(qkv, z, ba, conv_w, alog_row, dt_row, norm_w, bsz, seq, state=None):
    carry = state is None
    nb = 1 if carry else UNIT // seq
    chunk = UNIT // nb
    nt = seq // chunk
    n_tok = bsz * seq
    row_map = lambda i, j: (i * nt + j, 0)
    const = lambda i, j: (0, 0)
    in_specs = [pl.BlockSpec((UNIT, DN_CONV_CH), row_map),
                pl.BlockSpec((UNIT, DN_V), row_map),
                pl.BlockSpec((UNIT, BA_PAD), row_map),
                pl.BlockSpec((DN_CONV, DN_CONV_CH), const),
                pl.BlockSpec((1, BA_PAD), const),
                pl.BlockSpec((1, BA_PAD), const),
                pl.BlockSpec((1, DN_DV), const)]
    args = [qkv, z, ba, conv_w, alog_row, dt_row, norm_w]
    scratch = []
    if carry:
        scratch = [pltpu.VMEM((1, DN_CONV - 1, DN_CONV_CH), F32)]
    else:
        conv_state, s0 = state
        in_specs += [pl.BlockSpec((nb, DN_CONV - 1, DN_CONV_CH), lambda i, j: (i, 0, 0)),
                     pl.BlockSpec((nb, DN_HEADS, DN_DK, DN_DV), lambda i, j: (i, 0, 0, 0))]
        args += [conv_state, s0]
    return pl.pallas_call(
        functools.partial(_dn_kernel, nb=nb, chunk=chunk, carry=carry),
        grid=(bsz // nb, nt),
        in_specs=in_specs,
        out_specs=[pl.BlockSpec((UNIT, DN_V), row_map),
                   pl.BlockSpec((nb, DN_HEADS, DN_DK, DN_DV), lambda i, j: (i, 0, 0, 0))],
        out_shape=[jax.ShapeDtypeStruct((n_tok, DN_V), F32),
                   jax.ShapeDtypeStruct((bsz, DN_HEADS, DN_DK, DN_DV), F32)],
        scratch_shapes=scratch,
        compiler_params=_cparams(("arbitrary", "arbitrary")),
        name="gated_deltanet_carry" if carry else "gated_deltanet_state",
    )(*args)


def _bucket_table():
    i = np.arange(WINDOW, dtype=np.int64)[:, None]
    j = np.arange(2 * WINDOW, dtype=np.int64)[None, :]
    d = np.maximum(i + WINDOW - j, 0)
    exact = N_BUCKETS // 2
    logv = (np.log(np.maximum(d, 1).astype(np.float32) / np.float32(exact)).astype(np.float32)
            / np.float32(math.log(MAX_DISTANCE / exact)))
    large = np.minimum(exact + (logv * np.float32(N_BUCKETS - exact)).astype(np.int32), N_BUCKETS - 1)
    return np.where(d < exact, d, large).astype(np.int32)


def _bias_kernel(rb_ref, bucket_ref, o_ref):
    bucket = bucket_ref[...]
    for h in range(SWA_HEADS):
        acc = jnp.zeros(bucket.shape, F32)
        for b in range(N_BUCKETS):
            acc = jnp.where(bucket == b, rb_ref[b, h], acc)
        o_ref[h] = acc


def _bias_call(rel_bias):
    bucket = jnp.asarray(_bucket_table())
    return pl.pallas_call(
        _bias_kernel,
        in_specs=[pl.BlockSpec(memory_space=pltpu.SMEM),
                  pl.BlockSpec((WINDOW, 2 * WINDOW), lambda: (0, 0))],
        out_specs=pl.BlockSpec((SWA_HEADS, WINDOW, 2 * WINDOW), lambda: (0, 0, 0)),
        out_shape=jax.ShapeDtypeStruct((SWA_HEADS, WINDOW, 2 * WINDOW), F32),
        name="swa_rel_bias_table",
    )(rel_bias, bucket)


def _softmax_sink(s, sink):
    m = jnp.maximum(jnp.max(s, axis=-1, keepdims=True), sink)
    p = jnp.exp(s - m)
    return p / (jnp.sum(p, axis=-1, keepdims=True) + jnp.exp(sink - m))


def _swa_prompt_kernel(sink_ref, q_ref, kp_ref, kc_ref, vp_ref, vc_ref, bias_ref, o_ref):
    blk = pl.program_id(1)
    ri = lax.broadcasted_iota(jnp.int32, (WINDOW, 2 * WINDOW), 0)
    ci = lax.broadcasted_iota(jnp.int32, (WINDOW, 2 * WINDOW), 1)
    dist = ri + WINDOW - ci
    valid = (dist >= 0) & (dist < WINDOW) & ((ci >= WINDOW) | (blk > 0))
    q = q_ref[...] * (SWA_HD ** -0.5)
    keys = jnp.concatenate([kp_ref[...], kc_ref[...]], axis=0)
    vals = jnp.concatenate([vp_ref[...], vc_ref[...]], axis=0)
    outs = []
    for hk in range(SWA_KV_HEADS):
        k_h = keys[:, hk * SWA_HD:(hk + 1) * SWA_HD]
        v_h = vals[:, hk * SWA_HD:(hk + 1) * SWA_HD]
        for g in range(SWA_GROUP):
            head = hk * SWA_GROUP + g
            s = _dot_nt(q[:, head * SWA_HD:(head + 1) * SWA_HD], k_h)
            s = jnp.where(valid, s + bias_ref[head], NEG_INF)
            outs.append(_dot(_softmax_sink(s, sink_ref[head]), v_h))
    o_ref[...] = jnp.concatenate(outs, axis=1)


def _swa_prompt_call(sinks, sq, sk, sv, bias, bsz, seq):
    nblk = seq // WINDOW
    cur = lambda b, i: (b * nblk + i, 0)
    prv = lambda b, i: (b * nblk + jnp.maximum(i - 1, 0), 0)
    return pl.pallas_call(
        _swa_prompt_kernel,
        grid=(bsz, nblk),
        in_specs=[pl.BlockSpec(memory_space=pltpu.SMEM),
                  pl.BlockSpec((WINDOW, SWA_Q), cur),
                  pl.BlockSpec((WINDOW, SWA_KV), prv),
                  pl.BlockSpec((WINDOW, SWA_KV), cur),
                  pl.BlockSpec((WINDOW, SWA_KV), prv),
                  pl.BlockSpec((WINDOW, SWA_KV), cur),
                  pl.BlockSpec((SWA_HEADS, WINDOW, 2 * WINDOW), lambda b, i: (0, 0, 0))],
        out_specs=pl.BlockSpec((WINDOW, SWA_Q), cur),
        out_shape=jax.ShapeDtypeStruct((bsz * seq, SWA_Q), F32),
        compiler_params=_cparams(("arbitrary", "arbitrary")),
        name="swa_banded",
    )(sinks, sq, sk, sk, sv, sv, bias)


def _swa_sample_kernel(sink_ref, q_ref, kn_ref, vn_ref, kc_ref, vc_ref, bias_ref, o_ref, ko_ref, vo_ref, *, nb, lt):
    ri = lax.broadcasted_iota(jnp.int32, (lt, 2 * WINDOW), 0)
    ci = lax.broadcasted_iota(jnp.int32, (lt, 2 * WINDOW), 1)
    dist = ri + WINDOW - ci
    valid1 = (dist >= 0) & (dist < WINDOW)
    valid = jnp.concatenate([valid1] * SWA_GROUP, axis=0)
    q_all = q_ref[...] * (SWA_HD ** -0.5)
    kn_all = kn_ref[...]
    vn_all = vn_ref[...]
    pad = jnp.zeros((WINDOW - lt, SWA_HD), F32)
    rows_out = []
    for s in range(nb):
        lo = s * lt
        kc = kc_ref[s]
        vc = vc_ref[s]
        kn = kn_all[lo:lo + lt]
        vn = vn_all[lo:lo + lt]
        ko_ref[s] = jnp.concatenate([kc[lt:], kn], axis=0)
        vo_ref[s] = jnp.concatenate([vc[lt:], vn], axis=0)
        q = q_all[lo:lo + lt]
        pieces = []
        for hk in range(SWA_KV_HEADS):
            cols = slice(hk * SWA_HD, (hk + 1) * SWA_HD)
            k_h = jnp.concatenate([kc[:, cols], kn[:, cols], pad], axis=0)
            v_h = jnp.concatenate([vc[:, cols], vn[:, cols], pad], axis=0)
            heads = range(hk * SWA_GROUP, (hk + 1) * SWA_GROUP)
            q4 = jnp.concatenate([q[:, h * SWA_HD:(h + 1) * SWA_HD] for h in heads], axis=0)
            b4 = jnp.concatenate([bias_ref[h, 0:lt, :] for h in heads], axis=0)
            sink4 = jnp.concatenate([jnp.full((lt, 1), sink_ref[h], F32) for h in heads], axis=0)
            sc = lax.dot_general(q4, k_h, (((1,), (1,)), ((), ())), preferred_element_type=F32)
            sc = jnp.where(valid, sc + b4, NEG_INF)
            o4 = jnp.dot(_softmax_sink(sc, sink4), v_h, preferred_element_type=F32)
            pieces += [o4[g * lt:(g + 1) * lt] for g in range(SWA_GROUP)]
        rows_out.append(jnp.concatenate(pieces, axis=1))
    o_ref[...] = jnp.concatenate(rows_out, axis=0)


def _swa_sample_call(sinks, sq, sk, sv, cache_k, cache_v, bias, bsz, seq):
    nb = UNIT // seq
    rows = lambda i: (i, 0)
    seqs = lambda i: (i, 0, 0)
    return pl.pallas_call(
        functools.partial(_swa_sample_kernel, nb=nb, lt=seq),
        grid=(bsz // nb,),
        in_specs=[pl.BlockSpec(memory_space=pltpu.SMEM),
                  pl.BlockSpec((UNIT, SWA_Q), rows),
                  pl.BlockSpec((UNIT, SWA_KV), rows),
                  pl.BlockSpec((UNIT, SWA_KV), rows),
                  pl.BlockSpec((nb, WINDOW, SWA_KV), seqs),
                  pl.BlockSpec((nb, WINDOW, SWA_KV), seqs),
                  pl.BlockSpec((SWA_HEADS, WINDOW, 2 * WINDOW), lambda i: (0, 0, 0))],
        out_specs=[pl.BlockSpec((UNIT, SWA_Q), rows),
                   pl.BlockSpec((nb, WINDOW, SWA_KV), seqs),
                   pl.BlockSpec((nb, WINDOW, SWA_KV), seqs)],
        out_shape=[jax.ShapeDtypeStruct((bsz * seq, SWA_Q), F32),
                   jax.ShapeDtypeStruct((bsz, WINDOW, SWA_KV), F32),
                   jax.ShapeDtypeStruct((bsz, WINDOW, SWA_KV), F32)],
        compiler_params=_cparams(("arbitrary",)),
        name="swa_cached",
    )(sinks, sq, sk, sv, cache_k, cache_v, bias)


def _post_kernel(*refs, nb, lt, carry):
    if carry:
        (odn_ref, oswa_ref, x_ref, mod_ref, wout_ref, nfw_ref, wup_ref, cw_ref, cb_ref, wdn_ref, fnw_ref,
         y_ref, fbuf_ref, h_scr, x1_scr, acc_scr) = refs
        prev_ref = fbuf_ref

        @pl.when(pl.program_id(1) == 0)
        def _():
            fbuf_ref[...] = jnp.zeros_like(fbuf_ref)
    else:
        (odn_ref, oswa_ref, x_ref, mod_ref, wout_ref, nfw_ref, wup_ref, cw_ref, cb_ref, wdn_ref, fnw_ref,
         prev_ref, y_ref, fbuf_ref, h_scr, x1_scr, acc_scr) = refs
    rows = nb * lt
    attn = (jnp.dot(_bf(odn_ref[...]), wout_ref[0:DN_V, :], preferred_element_type=F32)
            + jnp.dot(_bf(oswa_ref[...]), wout_ref[DN_V:, :], preferred_element_type=F32))
    x1 = x_ref[...].reshape(rows, D_MODEL) + _rows(mod_ref[:, 2:3, :], nb, lt) * attn
    x1_scr[...] = x1
    h = _rms(x1, nfw_ref[...]) * (1.0 + _rows(mod_ref[:, 4:5, :], nb, lt)) + _rows(mod_ref[:, 3:4, :], nb, lt)
    h_scr[...] = _bf(h)
    acc_scr[...] = jnp.zeros_like(acc_scr)
    for c in range(D_FF // FFN_CHUNK):
        halves = []
        for base in (0, D_FF):
            cols = slice(base + c * FFN_CHUNK, base + (c + 1) * FFN_CHUNK)
            u = jnp.dot(h_scr[...], wup_ref[:, cols], preferred_element_type=F32)
            prev = prev_ref[:, :, cols]
            halves.append(_causal_conv(u, prev, cw_ref[:, cols], nb, lt) + cb_ref[:, cols])
            fbuf_ref[:, :, cols] = u.reshape(nb, lt, FFN_CHUNK)[:, lt - (FFN_CONV - 1):, :]
        act = _silu(halves[0]) * halves[1]
        acc_scr[...] += jnp.dot(_bf(act), wdn_ref[c * FFN_CHUNK:(c + 1) * FFN_CHUNK, :],
                                preferred_element_type=F32)
    x2 = x1_scr[...] + _rows(mod_ref[:, 5:6, :], nb, lt) * acc_scr[...]
    y_ref[...] = _rms(x2, fnw_ref[...]).reshape(nb, lt, D_MODEL)


def _post_call(o_dn, o_swa, x, mod3, w_out, norm_ffn_w, w_up, conv_w, conv_b, w_down, final_w, nb, lt, state=None):
    carry = state is None
    bsz, seq, _ = x.shape
    nt = seq // lt
    rows = nb * lt
    row_map = lambda i, j: (i * nt + j, 0)
    const = lambda i, j: (0, 0)
    in_specs = [pl.BlockSpec((rows, DN_V), row_map),
                pl.BlockSpec((rows, SWA_Q), row_map),
                pl.BlockSpec((nb, lt, D_MODEL), lambda i, j: (i, j, 0)),
                pl.BlockSpec((nb, 6, D_MODEL), lambda i, j: (i, 0, 0)),
                pl.BlockSpec((D_MODEL, D_MODEL), const),
                pl.BlockSpec((1, D_MODEL), const),
                pl.BlockSpec((D_MODEL, 2 * D_FF), const),
                pl.BlockSpec((FFN_CONV, 2 * D_FF), const),
                pl.BlockSpec((1, 2 * D_FF), const),
                pl.BlockSpec((D_FF, D_MODEL), const),
                pl.BlockSpec((1, D_MODEL), const)]
    args = [o_dn, o_swa, x, mod3, w_out, norm_ffn_w, w_up, conv_w, conv_b, w_down, final_w]
    if not carry:
        in_specs.append(pl.BlockSpec((nb, FFN_CONV - 1, 2 * D_FF), lambda i, j: (i, 0, 0)))
        args.append(state)
    return pl.pallas_call(
        functools.partial(_post_kernel, nb=nb, lt=lt, carry=carry),
        grid=(bsz // nb, nt),
        in_specs=in_specs,
        out_specs=[pl.BlockSpec((nb, lt, D_MODEL), lambda i, j: (i, j, 0)),
                   pl.BlockSpec((nb, FFN_CONV - 1, 2 * D_FF), lambda i, j: (i, 0, 0))],
        out_shape=[jax.ShapeDtypeStruct((bsz, seq, D_MODEL), F32),
                   jax.ShapeDtypeStruct((bsz, FFN_CONV - 1, 2 * D_FF), F32)],
        scratch_shapes=[pltpu.VMEM((rows, D_MODEL), BF16),
                        pltpu.VMEM((rows, D_MODEL), F32),
                        pltpu.VMEM((rows, D_MODEL), F32)],
        compiler_params=_cparams(("arbitrary", "arbitrary")),
        name="out_proj_convffn_carry" if carry else "out_proj_convffn_state",
    )(*args)


def _pad_row(vec, offset):
    return jnp.zeros((1, BA_PAD), F32).at[0, offset:offset + vec.shape[0]].set(vec)


def kernel(x_prompt, x_sample, state_dn_conv, state_dn_ssm, cache_swa_k, cache_swa_v, state_ffn_conv, c_prompt, c_sample, rel_bias, final_norm_w, w_ada, b_ada, norm_mix_w, w_in, dn_conv_w, dn_A_log, dn_dt_bias, dn_norm_w, swa_sinks, w_out, norm_ffn_w, ffn_w_up, ffn_conv_w, ffn_conv_b, ffn_w_down):
    bp, lp, _ = x_prompt.shape
    bs, ls, _ = x_sample.shape
    layer = 0

    w_i = w_in[layer]
    n_ba = 2 * DN_HEADS
    ba_lo = DN_CONV_CH + DN_V
    w_cat = jnp.concatenate(
        [w_i[:, :ba_lo], jnp.pad(w_i[:, ba_lo:ba_lo + n_ba], ((0, 0), (0, BA_PAD - n_ba))), w_i[:, ba_lo + n_ba:]],
        axis=1).astype(BF16)
    w_out_b = w_out[layer].astype(BF16)
    w_up_b = ffn_w_up[layer].astype(BF16)
    w_dn_b = ffn_w_down[layer].astype(BF16)
    alog_row = _pad_row(dn_A_log[layer], DN_HEADS)
    dt_row = _pad_row(dn_dt_bias[layer], DN_HEADS)
    row = lambda v: v.reshape(1, -1)

    n_c = bp + bs
    n_c_pad = -(-n_c // 8) * 8
    c_all = jnp.pad(jnp.concatenate([c_prompt, c_sample], axis=0), ((0, n_c_pad - n_c), (0, 0)))
    mod = _mod_call(c_all, w_ada[layer], row(b_ada[layer]))
    mod_p = mod[:bp].reshape(bp, 6, D_MODEL)
    mod_s = mod[bp:n_c].reshape(bs, 6, D_MODEL)

    bias = _bias_call(rel_bias)
    sinks = swa_sinks[layer]

    def mixer_in(x, mod3, nb, lt):
        return _in_call(x, mod3, row(norm_mix_w[layer]), w_cat, nb, lt)

    def post(o_dn, o_swa, x, mod3, nb, lt, state=None):
        return _post_call(o_dn, o_swa, x, mod3, w_out_b, row(norm_ffn_w[layer]), w_up_b, ffn_conv_w[layer],
                          row(ffn_conv_b[layer]), w_dn_b, row(final_norm_w), nb, lt, state)

    dn_args = (dn_conv_w[layer], alog_row, dt_row, row(dn_norm_w[layer]))

    qkv, z, ba, sq, sk, sv = mixer_in(x_prompt, mod_p, 1, 512)
    o_dn, p_ssm = _dn_call(qkv, z, ba, *dn_args, bp, lp)
    o_swa = _swa_prompt_call(sinks, sq, sk, sv, bias, bp, lp)
    y_prompt, p_fbuf = post(o_dn, o_swa, x_prompt, mod_p, 1, 256)
    p_dn_conv = qkv.reshape(bp, lp, DN_CONV_CH)[:, lp - (DN_CONV - 1):]
    p_swa_k = sk.reshape(bp, lp, SWA_KV_HEADS, SWA_HD)[:, lp - WINDOW:]
    p_swa_v = sv.reshape(bp, lp, SWA_KV_HEADS, SWA_HD)[:, lp - WINDOW:]

    qkv_s, z_s, ba_s, sq_s, sk_s, sv_s = mixer_in(x_sample, mod_s, 64, ls)
    o_dn_s, s_ssm = _dn_call(qkv_s, z_s, ba_s, *dn_args, bs, ls, state=(state_dn_conv[layer], state_dn_ssm[layer]))
    o_swa_s, s_k, s_v = _swa_sample_call(sinks, sq_s, sk_s, sv_s,
                                         cache_swa_k[layer].reshape(bs, WINDOW, SWA_KV),
                                         cache_swa_v[layer].reshape(bs, WINDOW, SWA_KV), bias, bs, ls)
    y_sample, s_fbuf = post(o_dn_s, o_swa_s, x_sample, mod_s, 32, ls, state=state_ffn_conv[layer])
    s_dn_conv = qkv_s.reshape(bs, ls, DN_CONV_CH)[:, ls - (DN_CONV - 1):]

    return (y_prompt, y_sample, p_dn_conv[None], s_dn_conv[None], p_ssm[None], s_ssm[None],
            p_swa_k[None], s_k.reshape(bs, WINDOW, SWA_KV_HEADS, SWA_HD)[None],
            p_swa_v[None], s_v.reshape(bs, WINDOW, SWA_KV_HEADS, SWA_HD)[None],
            p_fbuf[None], s_fbuf[None])
```

```python
import functools
import math

import numpy as np
import jax
import jax.numpy as jnp
from jax import lax
from jax.experimental import pallas as pl
from jax.experimental.pallas import tpu as pltpu

F32 = jnp.float32
BF16 = jnp.bfloat16

D_MODEL = 1024
PAST_LEN = 16384
DN_HEADS = 4
DN_DK = 128
DN_DV = 128
DN_CONV = 4
SWA_HEADS = 8
SWA_KV_HEADS = 2
SWA_GROUP = SWA_HEADS // SWA_KV_HEADS
SWA_HD = 64
WINDOW = 128
N_BUCKETS = 32
MAX_DISTANCE = 128
D_FF = 2816
FFN_CONV = 3
EPS = 1e-6
NEG_INF = -1e30

DN_QK = DN_HEADS * DN_DK
DN_V = DN_HEADS * DN_DV
DN_CONV_CH = 2 * DN_QK + DN_V
SWA_Q = SWA_HEADS * SWA_HD
SWA_KV = SWA_KV_HEADS * SWA_HD
BA_PAD = 128
UNIT = 128
FFN_CHUNK = 256
VMEM_LIMIT = 56 * 1024 * 1024


def _cparams(sem):
    return pltpu.CompilerParams(dimension_semantics=sem, vmem_limit_bytes=VMEM_LIMIT)


def _bf(x):
    return x.astype(BF16)


def _dot(a, b):
    return jnp.dot(_bf(a), _bf(b), preferred_element_type=F32)


def _dot_nt(a, b):
    return lax.dot_general(_bf(a), _bf(b), (((1,), (1,)), ((), ())), preferred_element_type=F32)


def _split2(x):
    hi = _bf(x)
    lo = _bf(x - hi.astype(F32))
    return hi, lo


def _dot_x3(a, b):
    ah, al = _split2(a)
    bh, bl = _split2(b)
    d = functools.partial(jnp.dot, preferred_element_type=F32)
    return d(ah, bh) + (d(al, bh) + d(ah, bl))


def _dot_mask(m, x):
    hi = _bf(x)
    r = x - hi.astype(F32)
    mid = _bf(r)
    lo = _bf(r - mid.astype(F32))
    d = functools.partial(jnp.dot, preferred_element_type=F32)
    return d(m, hi) + (d(m, mid) + d(m, lo))


def _sigmoid(x):
    return 1.0 / (1.0 + jnp.exp(-x))


def _silu(x):
    return x * _sigmoid(x)


def _softplus(x):
    return jnp.maximum(x, 0.0) + jnp.log1p(jnp.exp(-jnp.abs(x)))


def _rms(x, w):
    ms = jnp.mean(x * x, axis=-1, keepdims=True)
    return x * lax.rsqrt(ms + EPS) * w


def _l2norm(t):
    return t * lax.rsqrt(jnp.sum(t * t, axis=-1, keepdims=True) + EPS)


def _rows(m3, nb, lt):
    return jnp.broadcast_to(m3, (nb, lt, m3.shape[-1])).reshape(nb * lt, m3.shape[-1])


def _causal_conv(x, prev, w, nb, lt):
    width = w.shape[0]
    rows, ch = x.shape
    tmod = lax.broadcasted_iota(jnp.int32, (rows, 1), 0) & (lt - 1)
    out = x * w[width - 1:width, :]
    for j in range(1, width):
        sh = pltpu.roll(x, j, axis=0)
        for t in range(j):
            p = width - 1 - j + t
            sh = jnp.where(tmod == t, _rows(prev[:, p:p + 1, :], nb, lt), sh)
        out = out + sh * w[width - 1 - j:width - j, :]
    return out


def _mod_kernel(c_ref, w_ref, b_ref, o_ref):
    o_ref[...] = _dot_x3(_silu(c_ref[...]), w_ref[...]) + b_ref[...]


def _mod_call(c_all, w_ada, b_ada):
    rows = c_all.shape[0]
    cols = w_ada.shape[1]
    tile = 512
    return pl.pallas_call(
        _mod_kernel,
        grid=(cols // tile,),
        in_specs=[pl.BlockSpec((rows, D_MODEL), lambda j: (0, 0)),
                  pl.BlockSpec((D_MODEL, tile), lambda j: (0, j)),
                  pl.BlockSpec((1, tile), lambda j: (0, j))],
        out_specs=pl.BlockSpec((rows, tile), lambda j: (0, j)),
        out_shape=jax.ShapeDtypeStruct((rows, cols), F32),
        compiler_params=_cparams(("arbitrary",)),
        name="adaln_mod",
    )(c_all, w_ada, b_ada)


IN_SPLIT = (DN_CONV_CH, DN_V, BA_PAD, SWA_Q, SWA_KV, SWA_KV)
IN_COLS_PAD = sum(IN_SPLIT)


def _in_kernel(x_ref, mod_ref, nw_ref, w_ref, *out_refs, nb, lt):
    x = x_ref[...]
    ms = jnp.mean(x * x, axis=-1, keepdims=True)
    y = x * lax.rsqrt(ms + EPS) * nw_ref[...]
    h = y * (1.0 + mod_ref[:, 1:2, :]) + mod_ref[:, 0:1, :]
    h = _bf(h.reshape(nb * lt, D_MODEL))
    off = 0
    for o_ref, n in zip(out_refs, IN_SPLIT):
        o_ref[...] = jnp.dot(h, w_ref[:, off:off + n], preferred_element_type=F32)
        off += n


def _in_call(x, mod3, norm_w, w_cat, nb, lt):
    bsz, seq, _ = x.shape
    nt = seq // lt
    rows = nb * lt
    n_tok = bsz * seq
    row_map = lambda i, j: (i * nt + j, 0)
    return pl.pallas_call(
        functools.partial(_in_kernel, nb=nb, lt=lt),
        grid=(bsz // nb, nt),
        in_specs=[pl.BlockSpec((nb, lt, D_MODEL), lambda i, j: (i, j, 0)),
                  pl.BlockSpec((nb, 6, D_MODEL), lambda i, j: (i, 0, 0)),
                  pl.BlockSpec((1, D_MODEL), lambda i, j: (0, 0)),
                  pl.BlockSpec((D_MODEL, IN_COLS_PAD), lambda i, j: (0, 0))],
        out_specs=[pl.BlockSpec((rows, n), row_map) for n in IN_SPLIT],
        out_shape=[jax.ShapeDtypeStruct((n_tok, n), F32) for n in IN_SPLIT],
        compiler_params=_cparams(("arbitrary", "arbitrary")),
        name="norm_in_proj",
    )(x, mod3, norm_w, w_cat)


INV_BASE_SHIFT = 3
DN_UNITS_CARRY = 2
DN_UNITS_STATE = 1


def _unit_lower_inverses(a_mats, ri, ci, chunk_shift):
    def blocks(s):
        return (ri >> s) == (ci >> s)

    base = min(INV_BASE_SHIFT, chunk_shift)
    xs = [jnp.where(blocks(base), -a, 0.0) for a in a_mats]
    ts = [jnp.where(ri == ci, 1.0, n) for n in xs]
    for lvl in range(1, base):
        xs = [_dot_x3(x, x) for x in xs]
        ts = [t + _dot_x3(t, x) for t, x in zip(ts, xs)]
    for s in range(base, chunk_shift):
        sel = blocks(s + 1) & jnp.logical_not(blocks(s))
        ets = [_dot_x3(jnp.where(sel, a, 0.0), t) for a, t in zip(a_mats, ts)]
        ts = [t - _dot_x3(t, et) for t, et in zip(ts, ets)]
    return ts


def _dn_kernel(*refs, nb, chunk, carry, units):
    if carry:
        (qkv_ref, z_ref, ba_ref, cw_ref, alog_ref, dt_ref, nw_ref, o_ref, s_ref, cst_ref) = refs
        s0_ref = s_ref

        @pl.when(pl.program_id(1) == 0)
        def _():
            cst_ref[...] = jnp.zeros_like(cst_ref)
            s_ref[...] = jnp.zeros_like(s_ref)
    else:
        (qkv_ref, z_ref, ba_ref, cw_ref, alog_ref, dt_ref, nw_ref, cst_ref, s0_ref, o_ref, s_ref) = refs

    rows = units * UNIT
    seq_rows = rows // nb
    ri = lax.broadcasted_iota(jnp.int32, (UNIT, UNIT), 0)
    ci = lax.broadcasted_iota(jnp.int32, (UNIT, UNIT), 1)
    shift = int(math.log2(chunk))
    same = (ri >> shift) == (ci >> shift)
    incl = same & (ri >= ci)
    strict = same & (ri > ci)

    prev = cst_ref[...]
    x = qkv_ref[...]
    cw = cw_ref[...]

    ba = ba_ref[...]
    beta_full = _sigmoid(ba)
    g_full = -jnp.exp(alog_ref[...]) * _softplus(ba + dt_ref[...])
    masks = jnp.concatenate([jnp.where(incl, 1.0, 0.0), jnp.where(same, 1.0, 0.0)], axis=0).astype(BF16)
    gsums = [_dot_mask(masks, g_full[u * UNIT:(u + 1) * UNIT]) for u in range(units)]
    g_cum = [g[:UNIT] for g in gsums]
    g_tot = [g[UNIT:] for g in gsums]
    g_cum_t = [g.T for g in g_cum]

    def head_cols(base, h):
        lo = base + h * DN_DK
        return _silu(_causal_conv(x[:, lo:lo + DN_DK], prev[:, :, lo:lo + DN_DK], cw[:, lo:lo + DN_DK],
                                  nb, seq_rows))

    q_h = [_l2norm(head_cols(0, h)) * (DN_DK ** -0.5) for h in range(DN_HEADS)]
    k_h = [_l2norm(head_cols(DN_QK, h)) for h in range(DN_HEADS)]
    v_h = [head_cols(2 * DN_QK, h) for h in range(DN_HEADS)]

    probs = [(u, h) for u in range(units) for h in range(DN_HEADS)]
    n_p = len(probs)
    rs = lambda u: slice(u * UNIT, (u + 1) * UNIT)
    q = [q_h[h][rs(u)] for u, h in probs]
    k = [k_h[h][rs(u)] for u, h in probs]
    v = [v_h[h][rs(u)] for u, h in probs]
    gc = [g_cum[u][:, DN_HEADS + h:DN_HEADS + h + 1] for u, h in probs]
    gr = [g_cum_t[u][DN_HEADS + h:DN_HEADS + h + 1, :] for u, h in probs]
    gt = [g_tot[u][:, DN_HEADS + h:DN_HEADS + h + 1] for u, h in probs]
    bc = [beta_full[rs(u), h:h + 1] for u, h in probs]
    decay = [jnp.where(incl, jnp.exp(jnp.where(incl, gc[p] - gr[p], 0.0)), 0.0) for p in range(n_p)]
    e_g = [jnp.exp(gc[p]) for p in range(n_p)]
    kq = [_dot_nt(jnp.concatenate([k[p], q[p]], axis=0), k[p]) for p in range(n_p)]
    qk = [kq[p][UNIT:] * decay[p] for p in range(n_p)]
    a_mats = [jnp.where(strict, bc[p] * kq[p][:UNIT] * decay[p], 0.0) for p in range(n_p)]
    t_inv = _unit_lower_inverses(a_mats, ri, ci, shift)
    wvk = [_dot(t_inv[p], jnp.concatenate([v[p] * bc[p], k[p] * (bc[p] * e_g[p])], axis=1)) for p in range(n_p)]
    w_v = [w[:, :DN_DV] for w in wvk]
    w_k = [w[:, DN_DV:] for w in wvk]
    q_dec = [q[p] * e_g[p] for p in range(n_p)]
    k_tail_t = [(k[p] * jnp.exp(gt[p] - gc[p])).T for p in range(n_p)]
    c_dec = [jnp.exp(gt[p]) for p in range(n_p)]

    outs = {}
    if carry:
        state = [s_ref[0, h] for h in range(DN_HEADS)]
        for u in range(units):
            ps = [u * DN_HEADS + h for h in range(DN_HEADS)]
            r = [_dot(jnp.concatenate([w_k[p], q_dec[p]], axis=0), state[h]) for h, p in enumerate(ps)]
            uu = [w_v[p] - r[h][:UNIT] for h, p in enumerate(ps)]
            for h, p in enumerate(ps):
                outs[p] = r[h][UNIT:] + _dot(qk[p], uu[h])
            state = [state[h] * c_dec[p][0:1, :] + _dot(k_tail_t[p], uu[h]) for h, p in enumerate(ps)]
        for h in range(DN_HEADS):
            s_ref[0, h] = state[h]
    else:
        per_unit = nb // units
        for p, (u, h) in enumerate(probs):
            us, qs = [], []
            for s in range(per_unit):
                lo = s * chunk
                lhs = jnp.concatenate([w_k[p][lo:lo + chunk], q_dec[p][lo:lo + chunk]], axis=0)
                r = jnp.dot(lhs, s0_ref[u * per_unit + s, h], preferred_element_type=F32)
                us.append(w_v[p][lo:lo + chunk] - r[:chunk])
                qs.append(r[chunk:])
            uu = jnp.concatenate(us, axis=0)
            outs[p] = jnp.concatenate(qs, axis=0) + _dot(qk[p], uu)
            for s in range(per_unit):
                lo = s * chunk
                upd = jnp.dot(k_tail_t[p][:, lo:lo + chunk], uu[lo:lo + chunk], preferred_element_type=F32)
                s_ref[u * per_unit + s, h] = s0_ref[u * per_unit + s, h] * c_dec[p][lo:lo + 1, :] + upd

    for p, (u, h) in enumerate(probs):
        zz = z_ref[rs(u), h * DN_DV:(h + 1) * DN_DV]
        o_ref[rs(u), h * DN_DV:(h + 1) * DN_DV] = _rms(outs[p], nw_ref[...]) * _silu(zz)

    if carry:
        cst_ref[0] = x[rows - (DN_CONV - 1):, :]


def _dn_call(qkv, z, ba, conv_w, alog_row, dt_row, norm_w, bsz, seq, state=None):
    carry = state is None
    units = DN_UNITS_CARRY if carry else DN_UNITS_STATE
    rows = units * UNIT
    nb = 1 if carry else rows // seq
    chunk = UNIT if carry else seq
    nt = seq * nb // rows
    n_tok = bsz * seq
    row_map = lambda i, j: (i * nt + j, 0)
    const = lambda i, j: (0, 0)
    in_specs = [pl.BlockSpec((rows, DN_CONV_CH), row_map),
                pl.BlockSpec((rows, DN_V), row_map),
                pl.BlockSpec((rows, BA_PAD), row_map),
                pl.BlockSpec((DN_CONV, DN_CONV_CH), const),
                pl.BlockSpec((1, BA_PAD), const),
                pl.BlockSpec((1, BA_PAD), const),
                pl.BlockSpec((1, DN_DV), const)]
    args = [qkv, z, ba, conv_w, alog_row, dt_row, norm_w]
    scratch = []
    if carry:
        scratch = [pltpu.VMEM((1, DN_CONV - 1, DN_CONV_CH), F32)]
    else:
        conv_state, s0 = state
        in_specs += [pl.BlockSpec((nb, DN_CONV - 1, DN_CONV_CH), lambda i, j: (i, 0, 0)),
                     pl.BlockSpec((nb, DN_HEADS, DN_DK, DN_DV), lambda i, j: (i, 0, 0, 0))]
        args += [conv_state, s0]
    return pl.pallas_call(
        functools.partial(_dn_kernel, nb=nb, chunk=chunk, carry=carry, units=units),
        grid=(bsz // nb, nt),
        in_specs=in_specs,
        out_specs=[pl.BlockSpec((rows, DN_V), row_map),
                   pl.BlockSpec((nb, DN_HEADS, DN_DK, DN_DV), lambda i, j: (i, 0, 0, 0))],
        out_shape=[jax.ShapeDtypeStruct((n_tok, DN_V), F32),
                   jax.ShapeDtypeStruct((bsz, DN_HEADS, DN_DK, DN_DV), F32)],
        scratch_shapes=scratch,
        compiler_params=_cparams(("arbitrary", "arbitrary")),
        name="gated_deltanet_carry" if carry else "gated_deltanet_state",
    )(*args)


def _bucket_table():
    i = np.arange(WINDOW, dtype=np.int64)[:, None]
    j = np.arange(2 * WINDOW, dtype=np.int64)[None, :]
    d = np.maximum(i + WINDOW - j, 0)
    exact = N_BUCKETS // 2
    logv = (np.log(np.maximum(d, 1).astype(np.float32) / np.float32(exact)).astype(np.float32)
            / np.float32(math.log(MAX_DISTANCE / exact)))
    large = np.minimum(exact + (logv * np.float32(N_BUCKETS - exact)).astype(np.int32), N_BUCKETS - 1)
    return np.where(d < exact, d, large).astype(np.int32)


def _bias_kernel(rb_ref, bucket_ref, o_ref):
    bucket = bucket_ref[...]
    for h in range(SWA_HEADS):
        acc = jnp.zeros(bucket.shape, F32)
        for b in range(N_BUCKETS):
            acc = jnp.where(bucket == b, rb_ref[b, h], acc)
        o_ref[h] = acc


def _bias_call(rel_bias):
    bucket = jnp.asarray(_bucket_table())
    return pl.pallas_call(
        _bias_kernel,
        in_specs=[pl.BlockSpec(memory_space=pltpu.SMEM),
                  pl.BlockSpec((WINDOW, 2 * WINDOW), lambda: (0, 0))],
        out_specs=pl.BlockSpec((SWA_HEADS, WINDOW, 2 * WINDOW), lambda: (0, 0, 0)),
        out_shape=jax.ShapeDtypeStruct((SWA_HEADS, WINDOW, 2 * WINDOW), F32),
        name="swa_rel_bias_table",
    )(rel_bias, bucket)


def _softmax_sink(s, sink):
    m = jnp.maximum(jnp.max(s, axis=-1, keepdims=True), sink)
    p = jnp.exp(s - m)
    return p / (jnp.sum(p, axis=-1, keepdims=True) + jnp.exp(sink - m))


def _swa_prompt_kernel(sink_ref, q_ref, kp_ref, kc_ref, vp_ref, vc_ref, bias_ref, o_ref):
    blk = pl.program_id(1)
    ri = lax.broadcasted_iota(jnp.int32, (WINDOW, 2 * WINDOW), 0)
    ci = lax.broadcasted_iota(jnp.int32, (WINDOW, 2 * WINDOW), 1)
    dist = ri + WINDOW - ci
    valid = (dist >= 0) & (dist < WINDOW) & ((ci >= WINDOW) | (blk > 0))
    q = q_ref[...] * (SWA_HD ** -0.5)
    keys = jnp.concatenate([kp_ref[...], kc_ref[...]], axis=0)
    vals = jnp.concatenate([vp_ref[...], vc_ref[...]], axis=0)
    outs = []
    for hk in range(SWA_KV_HEADS):
        k_h = keys[:, hk * SWA_HD:(hk + 1) * SWA_HD]
        v_h = vals[:, hk * SWA_HD:(hk + 1) * SWA_HD]
        for g in range(SWA_GROUP):
            head = hk * SWA_GROUP + g
            s = _dot_nt(q[:, head * SWA_HD:(head + 1) * SWA_HD], k_h)
            s = jnp.where(valid, s + bias_ref[head], NEG_INF)
            outs.append(_dot(_softmax_sink(s, sink_ref[head]), v_h))
    o_ref[...] = jnp.concatenate(outs, axis=1)


def _swa_prompt_call(sinks, sq, sk, sv, bias, bsz, seq):
    nblk = seq // WINDOW
    cur = lambda b, i: (b * nblk + i, 0)
    prv = lambda b, i: (b * nblk + jnp.maximum(i - 1, 0), 0)
    return pl.pallas_call(
        _swa_prompt_kernel,
        grid=(bsz, nblk),
        in_specs=[pl.BlockSpec(memory_space=pltpu.SMEM),
                  pl.BlockSpec((WINDOW, SWA_Q), cur),
                  pl.BlockSpec((WINDOW, SWA_KV), prv),
                  pl.BlockSpec((WINDOW, SWA_KV), cur),
                  pl.BlockSpec((WINDOW, SWA_KV), prv),
                  pl.BlockSpec((WINDOW, SWA_KV), cur),
                  pl.BlockSpec((SWA_HEADS, WINDOW, 2 * WINDOW), lambda b, i: (0, 0, 0))],
        out_specs=pl.BlockSpec((WINDOW, SWA_Q), cur),
        out_shape=jax.ShapeDtypeStruct((bsz * seq, SWA_Q), F32),
        compiler_params=_cparams(("arbitrary", "arbitrary")),
        name="swa_banded",
    )(sinks, sq, sk, sk, sv, sv, bias)


def _swa_sample_kernel(sink_ref, q_ref, kn_ref, vn_ref, kc_ref, vc_ref, bias_ref, o_ref, ko_ref, vo_ref, *, nb, lt):
    ri = lax.broadcasted_iota(jnp.int32, (lt, 2 * WINDOW), 0)
    ci = lax.broadcasted_iota(jnp.int32, (lt, 2 * WINDOW), 1)
    dist = ri + WINDOW - ci
    valid1 = (dist >= 0) & (dist < WINDOW)
    valid = jnp.concatenate([valid1] * SWA_GROUP, axis=0)
    q_all = q_ref[...] * (SWA_HD ** -0.5)
    kn_all = kn_ref[...]
    vn_all = vn_ref[...]
    pad = jnp.zeros((WINDOW - lt, SWA_HD), F32)
    rows_out = []
    for s in range(nb):
        lo = s * lt
        kc = kc_ref[s]
        vc = vc_ref[s]
        kn = kn_all[lo:lo + lt]
        vn = vn_all[lo:lo + lt]
        ko_ref[s] = jnp.concatenate([kc[lt:], kn], axis=0)
        vo_ref[s] = jnp.concatenate([vc[lt:], vn], axis=0)
        q = q_all[lo:lo + lt]
        pieces = []
        for hk in range(SWA_KV_HEADS):
            cols = slice(hk * SWA_HD, (hk + 1) * SWA_HD)
            k_h = jnp.concatenate([kc[:, cols], kn[:, cols], pad], axis=0)
            v_h = jnp.concatenate([vc[:, cols], vn[:, cols], pad], axis=0)
            heads = range(hk * SWA_GROUP, (hk + 1) * SWA_GROUP)
            q4 = jnp.concatenate([q[:, h * SWA_HD:(h + 1) * SWA_HD] for h in heads], axis=0)
            b4 = jnp.concatenate([bias_ref[h, 0:lt, :] for h in heads], axis=0)
            sink4 = jnp.concatenate([jnp.full((lt, 1), sink_ref[h], F32) for h in heads], axis=0)
            sc = lax.dot_general(q4, k_h, (((1,), (1,)), ((), ())), preferred_element_type=F32)
            sc = jnp.where(valid, sc + b4, NEG_INF)
            o4 = jnp.dot(_softmax_sink(sc, sink4), v_h, preferred_element_type=F32)
            pieces += [o4[g * lt:(g + 1) * lt] for g in range(SWA_GROUP)]
        rows_out.append(jnp.concatenate(pieces, axis=1))
    o_ref[...] = jnp.concatenate(rows_out, axis=0)


def _swa_sample_call(sinks, sq, sk, sv, cache_k, cache_v, bias, bsz, seq):
    nb = UNIT // seq
    rows = lambda i: (i, 0)
    seqs = lambda i: (i, 0, 0)
    return pl.pallas_call(
        functools.partial(_swa_sample_kernel, nb=nb, lt=seq),
        grid=(bsz // nb,),
        in_specs=[pl.BlockSpec(memory_space=pltpu.SMEM),
                  pl.BlockSpec((UNIT, SWA_Q), rows),
                  pl.BlockSpec((UNIT, SWA_KV), rows),
                  pl.BlockSpec((UNIT, SWA_KV), rows),
                  pl.BlockSpec((nb, WINDOW, SWA_KV), seqs),
                  pl.BlockSpec((nb, WINDOW, SWA_KV), seqs),
                  pl.BlockSpec((SWA_HEADS, WINDOW, 2 * WINDOW), lambda i: (0, 0, 0))],
        out_specs=[pl.BlockSpec((UNIT, SWA_Q), rows),
                   pl.BlockSpec((nb, WINDOW, SWA_KV), seqs),
                   pl.BlockSpec((nb, WINDOW, SWA_KV), seqs)],
        out_shape=[jax.ShapeDtypeStruct((bsz * seq, SWA_Q), F32),
                   jax.ShapeDtypeStruct((bsz, WINDOW, SWA_KV), F32),
                   jax.ShapeDtypeStruct((bsz, WINDOW, SWA_KV), F32)],
        compiler_params=_cparams(("arbitrary",)),
        name="swa_cached",
    )(sinks, sq, sk, sv, cache_k, cache_v, bias)


def _post_kernel(*refs, nb, lt, carry):
    if carry:
        (odn_ref, oswa_ref, x_ref, mod_ref, wout_ref, nfw_ref, wup_ref, cw_ref, cb_ref, wdn_ref, fnw_ref,
         y_ref, fbuf_ref, h_scr, x1_scr, acc_scr) = refs
        prev_ref = fbuf_ref

        @pl.when(pl.program_id(1) == 0)
        def _():
            fbuf_ref[...] = jnp.zeros_like(fbuf_ref)
    else:
        (odn_ref, oswa_ref, x_ref, mod_ref, wout_ref, nfw_ref, wup_ref, cw_ref, cb_ref, wdn_ref, fnw_ref,
         prev_ref, y_ref, fbuf_ref, h_scr, x1_scr, acc_scr) = refs
    rows = nb * lt
    attn = (jnp.dot(_bf(odn_ref[...]), wout_ref[0:DN_V, :], preferred_element_type=F32)
            + jnp.dot(_bf(oswa_ref[...]), wout_ref[DN_V:, :], preferred_element_type=F32))
    x1 = x_ref[...].reshape(rows, D_MODEL) + _rows(mod_ref[:, 2:3, :], nb, lt) * attn
    x1_scr[...] = x1
    h = _rms(x1, nfw_ref[...]) * (1.0 + _rows(mod_ref[:, 4:5, :], nb, lt)) + _rows(mod_ref[:, 3:4, :], nb, lt)
    h_scr[...] = _bf(h)
    acc_scr[...] = jnp.zeros_like(acc_scr)
    for c in range(D_FF // FFN_CHUNK):
        halves = []
        for base in (0, D_FF):
            cols = slice(base + c * FFN_CHUNK, base + (c + 1) * FFN_CHUNK)
            u = jnp.dot(h_scr[...], wup_ref[:, cols], preferred_element_type=F32)
            prev = prev_ref[:, :, cols]
            halves.append(_causal_conv(u, prev, cw_ref[:, cols], nb, lt) + cb_ref[:, cols])
            fbuf_ref[:, :, cols] = u.reshape(nb, lt, FFN_CHUNK)[:, lt - (FFN_CONV - 1):, :]
        act = _silu(halves[0]) * halves[1]
        acc_scr[...] += jnp.dot(_bf(act), wdn_ref[c * FFN_CHUNK:(c + 1) * FFN_CHUNK, :],
                                preferred_element_type=F32)
    x2 = x1_scr[...] + _rows(mod_ref[:, 5:6, :], nb, lt) * acc_scr[...]
    y_ref[...] = _rms(x2, fnw_ref[...]).reshape(nb, lt, D_MODEL)


def _post_call(o_dn, o_swa, x, mod3, w_out, norm_ffn_w, w_up, conv_w, conv_b, w_down, final_w, nb, lt, state=None):
    carry = state is None
    bsz, seq, _ = x.shape
    nt = seq // lt
    rows = nb * lt
    row_map = lambda i, j: (i * nt + j, 0)
    const = lambda i, j: (0, 0)
    in_specs = [pl.BlockSpec((rows, DN_V), row_map),
                pl.BlockSpec((rows, SWA_Q), row_map),
                pl.BlockSpec((nb, lt, D_MODEL), lambda i, j: (i, j, 0)),
                pl.BlockSpec((nb, 6, D_MODEL), lambda i, j: (i, 0, 0)),
                pl.BlockSpec((D_MODEL, D_MODEL), const),
                pl.BlockSpec((1, D_MODEL), const),
                pl.BlockSpec((D_MODEL, 2 * D_FF), const),
                pl.BlockSpec((FFN_CONV, 2 * D_FF), const),
                pl.BlockSpec((1, 2 * D_FF), const),
                pl.BlockSpec((D_FF, D_MODEL), const),
                pl.BlockSpec((1, D_MODEL), const)]
    args = [o_dn, o_swa, x, mod3, w_out, norm_ffn_w, w_up, conv_w, conv_b, w_down, final_w]
    if not carry:
        in_specs.append(pl.BlockSpec((nb, FFN_CONV - 1, 2 * D_FF), lambda i, j: (i, 0, 0)))
        args.append(state)
    return pl.pallas_call(
        functools.partial(_post_kernel, nb=nb, lt=lt, carry=carry),
        grid=(bsz // nb, nt),
        in_specs=in_specs,
        out_specs=[pl.BlockSpec((nb, lt, D_MODEL), lambda i, j: (i, j, 0)),
                   pl.BlockSpec((nb, FFN_CONV - 1, 2 * D_FF), lambda i, j: (i, 0, 0))],
        out_shape=[jax.ShapeDtypeStruct((bsz, seq, D_MODEL), F32),
                   jax.ShapeDtypeStruct((bsz, FFN_CONV - 1, 2 * D_FF), F32)],
        scratch_shapes=[pltpu.VMEM((rows, D_MODEL), BF16),
                        pltpu.VMEM((rows, D_MODEL), F32),
                        pltpu.VMEM((rows, D_MODEL), F32)],
        compiler_params=_cparams(("arbitrary", "arbitrary")),
        name="out_proj_convffn_carry" if carry else "out_proj_convffn_state",
    )(*args)


def _pad_row(vec, offset):
    return jnp.zeros((1, BA_PAD), F32).at[0, offset:offset + vec.shape[0]].set(vec)


def kernel(x_prompt, x_sample, state_dn_conv, state_dn_ssm, cache_swa_k, cache_swa_v, state_ffn_conv, c_prompt, c_sample, rel_bias, final_norm_w, w_ada, b_ada, norm_mix_w, w_in, dn_conv_w, dn_A_log, dn_dt_bias, dn_norm_w, swa_sinks, w_out, norm_ffn_w, ffn_w_up, ffn_conv_w, ffn_conv_b, ffn_w_down):
    bp, lp, _ = x_prompt.shape
    bs, ls, _ = x_sample.shape
    layer = 0

    w_i = w_in[layer]
    n_ba = 2 * DN_HEADS
    ba_lo = DN_CONV_CH + DN_V
    w_cat = jnp.concatenate(
        [w_i[:, :ba_lo], jnp.pad(w_i[:, ba_lo:ba_lo + n_ba], ((0, 0), (0, BA_PAD - n_ba))), w_i[:, ba_lo + n_ba:]],
        axis=1).astype(BF16)
    w_out_b = w_out[layer].astype(BF16)
    w_up_b = ffn_w_up[layer].astype(BF16)
    w_dn_b = ffn_w_down[layer].astype(BF16)
    alog_row = _pad_row(dn_A_log[layer], DN_HEADS)
    dt_row = _pad_row(dn_dt_bias[layer], DN_HEADS)
    row = lambda v: v.reshape(1, -1)

    n_c = bp + bs
    n_c_pad = -(-n_c // 8) * 8
    c_all = jnp.pad(jnp.concatenate([c_prompt, c_sample], axis=0), ((0, n_c_pad - n_c), (0, 0)))
    mod = _mod_call(c_all, w_ada[layer], row(b_ada[layer]))
    mod_p = mod[:bp].reshape(bp, 6, D_MODEL)
    mod_s = mod[bp:n_c].reshape(bs, 6, D_MODEL)

    bias = _bias_call(rel_bias)
    sinks = swa_sinks[layer]

    def mixer_in(x, mod3, nb, lt):
        return _in_call(x, mod3, row(norm_mix_w[layer]), w_cat, nb, lt)

    def post(o_dn, o_swa, x, mod3, nb, lt, state=None):
        return _post_call(o_dn, o_swa, x, mod3, w_out_b, row(norm_ffn_w[layer]), w_up_b, ffn_conv_w[layer],
                          row(ffn_conv_b[layer]), w_dn_b, row(final_norm_w), nb, lt, state)

    dn_args = (dn_conv_w[layer], alog_row, dt_row, row(dn_norm_w[layer]))

    qkv, z, ba, sq, sk, sv = mixer_in(x_prompt, mod_p, 1, 512)
    o_dn, p_ssm = _dn_call(qkv, z, ba, *dn_args, bp, lp)
    o_swa = _swa_prompt_call(sinks, sq, sk, sv, bias, bp, lp)
    y_prompt, p_fbuf = post(o_dn, o_swa, x_prompt, mod_p, 1, 256)
    p_dn_conv = qkv.reshape(bp, lp, DN_CONV_CH)[:, lp - (DN_CONV - 1):]
    p_swa_k = sk.reshape(bp, lp, SWA_KV_HEADS, SWA_HD)[:, lp - WINDOW:]
    p_swa_v = sv.reshape(bp, lp, SWA_KV_HEADS, SWA_HD)[:, lp - WINDOW:]

    qkv_s, z_s, ba_s, sq_s, sk_s, sv_s = mixer_in(x_sample, mod_s, 64, ls)
    o_dn_s, s_ssm = _dn_call(qkv_s, z_s, ba_s, *dn_args, bs, ls, state=(state_dn_conv[layer], state_dn_ssm[layer]))
    o_swa_s, s_k, s_v = _swa_sample_call(sinks, sq_s, sk_s, sv_s,
                                         cache_swa_k[layer].reshape(bs, WINDOW, SWA_KV),
                                         cache_swa_v[layer].reshape(bs, WINDOW, SWA_KV), bias, bs, ls)
    y_sample, s_fbuf = post(o_dn_s, o_swa_s, x_sample, mod_s, 32, ls, state=state_ffn_conv[layer])
    s_dn_conv = qkv_s.reshape(bs, ls, DN_CONV_CH)[:, ls - (DN_CONV - 1):]

    return (y_prompt, y_sample, p_dn_conv[None], s_dn_conv[None], p_ssm[None], s_ssm[None],
            p_swa_k[None], s_k.reshape(bs, WINDOW, SWA_KV_HEADS, SWA_HD)[None],
            p_swa_v[None], s_v.reshape(bs, WINDOW, SWA_KV_HEADS, SWA_HD)[None],
            p_fbuf[None], s_fbuf[None])
```

```python
import functools
import math

import numpy as np
import jax
import jax.numpy as jnp
from jax import lax
from jax.experimental import pallas as pl
from jax.experimental.pallas import tpu as pltpu

F32 = jnp.float32
BF16 = jnp.bfloat16

D_MODEL = 1024
PAST_LEN = 16384
DN_HEADS = 4
DN_DK = 128
DN_DV = 128
DN_CONV = 4
SWA_HEADS = 8
SWA_KV_HEADS = 2
SWA_GROUP = SWA_HEADS // SWA_KV_HEADS
SWA_HD = 64
WINDOW = 128
N_BUCKETS = 32
MAX_DISTANCE = 128
D_FF = 2816
FFN_CONV = 3
EPS = 1e-6
NEG_INF = -1e30

DN_QK = DN_HEADS * DN_DK
DN_V = DN_HEADS * DN_DV
DN_CONV_CH = 2 * DN_QK + DN_V
SWA_Q = SWA_HEADS * SWA_HD
SWA_KV = SWA_KV_HEADS * SWA_HD
BA_PAD = 128
SUBLANES = 8
UNIT = 128
FFN_CHUNK = 256
VMEM_LIMIT = 56 * 1024 * 1024


def _cparams(sem):
    return pltpu.CompilerParams(dimension_semantics=sem, vmem_limit_bytes=VMEM_LIMIT)


def _bf(x):
    return x.astype(BF16)


def _dot(a, b):
    return jnp.dot(_bf(a), _bf(b), preferred_element_type=F32)


def _dot_nt(a, b):
    return lax.dot_general(_bf(a), _bf(b), (((1,), (1,)), ((), ())), preferred_element_type=F32)


def _split2(x):
    hi = _bf(x)
    lo = _bf(x - hi.astype(F32))
    return hi, lo


def _dot_x3(a, b):
    ah, al = _split2(a)
    bh, bl = _split2(b)
    d = functools.partial(jnp.dot, preferred_element_type=F32)
    return d(ah, bh) + (d(al, bh) + d(ah, bl))


def _dot_mask(m, x):
    hi = _bf(x)
    r = x - hi.astype(F32)
    mid = _bf(r)
    lo = _bf(r - mid.astype(F32))
    d = functools.partial(jnp.dot, preferred_element_type=F32)
    return d(m, hi) + (d(m, mid) + d(m, lo))


def _sigmoid(x):
    return 1.0 / (1.0 + jnp.exp(-x))


def _silu(x):
    return x * _sigmoid(x)


def _softplus(x):
    return jnp.maximum(x, 0.0) + jnp.log1p(jnp.exp(-jnp.abs(x)))


def _rms(x, w):
    ms = jnp.mean(x * x, axis=-1, keepdims=True)
    return x * lax.rsqrt(ms + EPS) * w


def _l2norm(t):
    return t * lax.rsqrt(jnp.sum(t * t, axis=-1, keepdims=True) + EPS)


def _rows(m3, nb, lt):
    return jnp.broadcast_to(m3, (nb, lt, m3.shape[-1])).reshape(nb * lt, m3.shape[-1])


def _causal_conv(x, prev, w, nb, lt):
    width = w.shape[0]
    rows, ch = x.shape
    head = SUBLANES if (nb == 1 and lt > SUBLANES) else rows
    lh = min(lt, head)
    tmod = lax.broadcasted_iota(jnp.int32, (head, 1), 0) & (lh - 1)
    out = x * w[width - 1:width, :]
    for j in range(1, width):
        sh = pltpu.roll(x, j, axis=0)
        top = sh[:head]
        for t in range(j):
            p = width - 1 - j + t
            top = jnp.where(tmod == t, _rows(prev[:, p:p + 1, :], nb, lh), top)
        sh = top if head == rows else jnp.concatenate([top, sh[head:]], axis=0)
        out = out + sh * w[width - 1 - j:width - j, :]
    return out


def _mod_kernel(c_ref, w_ref, b_ref, o_ref):
    o_ref[...] = _dot_x3(_silu(c_ref[...]), w_ref[...]) + b_ref[...]


def _mod_call(c_all, w_ada, b_ada):
    rows = c_all.shape[0]
    cols = w_ada.shape[1]
    tile = 512
    return pl.pallas_call(
        _mod_kernel,
        grid=(cols // tile,),
        in_specs=[pl.BlockSpec((rows, D_MODEL), lambda j: (0, 0)),
                  pl.BlockSpec((D_MODEL, tile), lambda j: (0, j)),
                  pl.BlockSpec((1, tile), lambda j: (0, j))],
        out_specs=pl.BlockSpec((rows, tile), lambda j: (0, j)),
        out_shape=jax.ShapeDtypeStruct((rows, cols), F32),
        compiler_params=_cparams(("arbitrary",)),
        name="adaln_mod",
    )(c_all, w_ada, b_ada)


IN_SPLIT = (DN_CONV_CH, DN_V, BA_PAD, SWA_Q, SWA_KV, SWA_KV)
IN_COLS_PAD = sum(IN_SPLIT)


def _in_kernel(x_ref, mod_ref, nw_ref, w_ref, *out_refs, nb, lt):
    x = x_ref[...]
    ms = jnp.mean(x * x, axis=-1, keepdims=True)
    y = x * lax.rsqrt(ms + EPS) * nw_ref[...]
    h = y * (1.0 + mod_ref[:, 1:2, :]) + mod_ref[:, 0:1, :]
    h = _bf(h.reshape(nb * lt, D_MODEL))
    off = 0
    for o_ref, n in zip(out_refs, IN_SPLIT):
        o_ref[...] = jnp.dot(h, w_ref[:, off:off + n], preferred_element_type=F32)
        off += n


def _in_call(x, mod3, norm_w, w_cat, nb, lt):
    bsz, seq, _ = x.shape
    nt = seq // lt
    rows = nb * lt
    n_tok = bsz * seq
    row_map = lambda i, j: (i * nt + j, 0)
    return pl.pallas_call(
        functools.partial(_in_kernel, nb=nb, lt=lt),
        grid=(bsz // nb, nt),
        in_specs=[pl.BlockSpec((nb, lt, D_MODEL), lambda i, j: (i, j, 0)),
                  pl.BlockSpec((nb, 6, D_MODEL), lambda i, j: (i, 0, 0)),
                  pl.BlockSpec((1, D_MODEL), lambda i, j: (0, 0)),
                  pl.BlockSpec((D_MODEL, IN_COLS_PAD), lambda i, j: (0, 0))],
        out_specs=[pl.BlockSpec((rows, n), row_map) for n in IN_SPLIT],
        out_shape=[jax.ShapeDtypeStruct((n_tok, n), F32) for n in IN_SPLIT],
        compiler_params=_cparams(("arbitrary", "arbitrary")),
        name="norm_in_proj",
    )(x, mod3, norm_w, w_cat)


INV_BASE_SHIFT = 3
DN_UNITS_CARRY = 2
DN_UNITS_STATE = 1


def _unit_lower_inverses(a_mats, ri, ci, chunk_shift):
    def blocks(s):
        return (ri >> s) == (ci >> s)

    base = min(INV_BASE_SHIFT, chunk_shift)
    xs = [jnp.where(blocks(base), -a, 0.0) for a in a_mats]
    ts = [jnp.where(ri == ci, 1.0, n) for n in xs]
    for lvl in range(1, base):
        xs = [_dot_x3(x, x) for x in xs]
        ts = [t + _dot_x3(t, x) for t, x in zip(ts, xs)]
    for s in range(base, chunk_shift):
        sel = blocks(s + 1) & jnp.logical_not(blocks(s))
        ets = [_dot_x3(jnp.where(sel, a, 0.0), t) for a, t in zip(a_mats, ts)]
        ts = [t - _dot_x3(t, et) for t, et in zip(ts, ets)]
    return ts


def _dn_kernel(*refs, nb, chunk, carry, units):
    if carry:
        (qkv_ref, z_ref, ba_ref, cw_ref, alog_ref, dt_ref, nw_ref, o_ref, s_ref, cst_ref) = refs
        s0_ref = s_ref

        @pl.when(pl.program_id(1) == 0)
        def _():
            cst_ref[...] = jnp.zeros_like(cst_ref)
            s_ref[...] = jnp.zeros_like(s_ref)
    else:
        (qkv_ref, z_ref, ba_ref, cw_ref, alog_ref, dt_ref, nw_ref, cst_ref, s0_ref, o_ref, s_ref) = refs

    rows = units * UNIT
    seq_rows = rows // nb
    ri = lax.broadcasted_iota(jnp.int32, (UNIT, UNIT), 0)
    ci = lax.broadcasted_iota(jnp.int32, (UNIT, UNIT), 1)
    shift = int(math.log2(chunk))
    same = (ri >> shift) == (ci >> shift)
    incl = same & (ri >= ci)
    strict = same & (ri > ci)

    prev = cst_ref[...]
    x = qkv_ref[...]
    cw = cw_ref[...]

    ba = ba_ref[...]
    beta_full = _sigmoid(ba)
    g_full = -jnp.exp(alog_ref[...]) * _softplus(ba + dt_ref[...])
    masks = jnp.concatenate([jnp.where(incl, 1.0, 0.0), jnp.where(same, 1.0, 0.0)], axis=0).astype(BF16)
    gsums = [_dot_mask(masks, g_full[u * UNIT:(u + 1) * UNIT]) for u in range(units)]
    g_cum = [g[:UNIT] for g in gsums]
    g_tot = [g[UNIT:] for g in gsums]
    g_cum_t = [g.T for g in g_cum]

    def head_cols(base, h):
        lo = base + h * DN_DK
        return _silu(_causal_conv(x[:, lo:lo + DN_DK], prev[:, :, lo:lo + DN_DK], cw[:, lo:lo + DN_DK],
                                  nb, seq_rows))

    q_h = [_l2norm(head_cols(0, h)) * (DN_DK ** -0.5) for h in range(DN_HEADS)]
    k_h = [_l2norm(head_cols(DN_QK, h)) for h in range(DN_HEADS)]
    v_h = [head_cols(2 * DN_QK, h) for h in range(DN_HEADS)]

    probs = [(u, h) for u in range(units) for h in range(DN_HEADS)]
    n_p = len(probs)
    rs = lambda u: slice(u * UNIT, (u + 1) * UNIT)
    q = [q_h[h][rs(u)] for u, h in probs]
    k = [k_h[h][rs(u)] for u, h in probs]
    v = [v_h[h][rs(u)] for u, h in probs]
    gc = [g_cum[u][:, DN_HEADS + h:DN_HEADS + h + 1] for u, h in probs]
    gr = [g_cum_t[u][DN_HEADS + h:DN_HEADS + h + 1, :] for u, h in probs]
    gt = [g_tot[u][:, DN_HEADS + h:DN_HEADS + h + 1] for u, h in probs]
    bc = [beta_full[rs(u), h:h + 1] for u, h in probs]
    decay = [jnp.where(incl, jnp.exp(jnp.where(incl, gc[p] - gr[p], 0.0)), 0.0) for p in range(n_p)]
    e_g = [jnp.exp(gc[p]) for p in range(n_p)]
    kq = [_dot_nt(jnp.concatenate([k[p], q[p]], axis=0), k[p]) for p in range(n_p)]
    qk = [kq[p][UNIT:] * decay[p] for p in range(n_p)]
    a_mats = [jnp.where(strict, bc[p] * kq[p][:UNIT] * decay[p], 0.0) for p in range(n_p)]
    t_inv = _unit_lower_inverses(a_mats, ri, ci, shift)
    wvk = [_dot(t_inv[p], jnp.concatenate([v[p] * bc[p], k[p] * (bc[p] * e_g[p])], axis=1)) for p in range(n_p)]
    w_v = [w[:, :DN_DV] for w in wvk]
    w_k = [w[:, DN_DV:] for w in wvk]
    q_dec = [q[p] * e_g[p] for p in range(n_p)]
    k_tail_t = [(k[p] * jnp.exp(gt[p] - gc[p])).T for p in range(n_p)]
    c_dec = [jnp.exp(gt[p]) for p in range(n_p)]

    outs = {}
    if carry:
        state = [s_ref[0, h] for h in range(DN_HEADS)]
        for u in range(units):
            ps = [u * DN_HEADS + h for h in range(DN_HEADS)]
            r = [_dot(jnp.concatenate([w_k[p], q_dec[p]], axis=0), state[h]) for h, p in enumerate(ps)]
            uu = [w_v[p] - r[h][:UNIT] for h, p in enumerate(ps)]
            for h, p in enumerate(ps):
                outs[p] = r[h][UNIT:] + _dot(qk[p], uu[h])
            state = [state[h] * c_dec[p][0:1, :] + _dot(k_tail_t[p], uu[h]) for h, p in enumerate(ps)]
        for h in range(DN_HEADS):
            s_ref[0, h] = state[h]
    else:
        per_unit = nb // units
        for p, (u, h) in enumerate(probs):
            us, qs = [], []
            for s in range(per_unit):
                lo = s * chunk
                lhs = jnp.concatenate([w_k[p][lo:lo + chunk], q_dec[p][lo:lo + chunk]], axis=0)
                r = jnp.dot(lhs, s0_ref[u * per_unit + s, h], preferred_element_type=F32)
                us.append(w_v[p][lo:lo + chunk] - r[:chunk])
                qs.append(r[chunk:])
            uu = jnp.concatenate(us, axis=0)
            outs[p] = jnp.concatenate(qs, axis=0) + _dot(qk[p], uu)
            for s in range(per_unit):
                lo = s * chunk
                upd = jnp.dot(k_tail_t[p][:, lo:lo + chunk], uu[lo:lo + chunk], preferred_element_type=F32)
                s_ref[u * per_unit + s, h] = s0_ref[u * per_unit + s, h] * c_dec[p][lo:lo + 1, :] + upd

    for p, (u, h) in enumerate(probs):
        zz = z_ref[rs(u), h * DN_DV:(h + 1) * DN_DV]
        o_ref[rs(u), h * DN_DV:(h + 1) * DN_DV] = _rms(outs[p], nw_ref[...]) * _silu(zz)

    if carry:
        cst_ref[0] = x[rows - (DN_CONV - 1):, :]


def _dn_call(qkv, z, ba, conv_w, alog_row, dt_row, norm_w, bsz, seq, state=None):
    carry = state is None
    units = DN_UNITS_CARRY if carry else DN_UNITS_STATE
    rows = units * UNIT
    nb = 1 if carry else rows // seq
    chunk = UNIT if carry else seq
    nt = seq * nb // rows
    n_tok = bsz * seq
    row_map = lambda i, j: (i * nt + j, 0)
    const = lambda i, j: (0, 0)
    in_specs = [pl.BlockSpec((rows, DN_CONV_CH), row_map),
                pl.BlockSpec((rows, DN_V), row_map),
                pl.BlockSpec((rows, BA_PAD), row_map),
                pl.BlockSpec((DN_CONV, DN_CONV_CH), const),
                pl.BlockSpec((1, BA_PAD), const),
                pl.BlockSpec((1, BA_PAD), const),
                pl.BlockSpec((1, DN_DV), const)]
    args = [qkv, z, ba, conv_w, alog_row, dt_row, norm_w]
    scratch = []
    if carry:
        scratch = [pltpu.VMEM((1, DN_CONV - 1, DN_CONV_CH), F32)]
    else:
        conv_state, s0 = state
        in_specs += [pl.BlockSpec((nb, DN_CONV - 1, DN_CONV_CH), lambda i, j: (i, 0, 0)),
                     pl.BlockSpec((nb, DN_HEADS, DN_DK, DN_DV), lambda i, j: (i, 0, 0, 0))]
        args += [conv_state, s0]
    return pl.pallas_call(
        functools.partial(_dn_kernel, nb=nb, chunk=chunk, carry=carry, units=units),
        grid=(bsz // nb, nt),
        in_specs=in_specs,
        out_specs=[pl.BlockSpec((rows, DN_V), row_map),
                   pl.BlockSpec((nb, DN_HEADS, DN_DK, DN_DV), lambda i, j: (i, 0, 0, 0))],
        out_shape=[jax.ShapeDtypeStruct((n_tok, DN_V), F32),
                   jax.ShapeDtypeStruct((bsz, DN_HEADS, DN_DK, DN_DV), F32)],
        scratch_shapes=scratch,
        compiler_params=_cparams(("arbitrary", "arbitrary")),
        name="gated_deltanet_carry" if carry else "gated_deltanet_state",
    )(*args)


def _bucket_table():
    i = np.arange(WINDOW, dtype=np.int64)[:, None]
    j = np.arange(2 * WINDOW, dtype=np.int64)[None, :]
    d = np.maximum(i + WINDOW - j, 0)
    exact = N_BUCKETS // 2
    logv = (np.log(np.maximum(d, 1).astype(np.float32) / np.float32(exact)).astype(np.float32)
            / np.float32(math.log(MAX_DISTANCE / exact)))
    large = np.minimum(exact + (logv * np.float32(N_BUCKETS - exact)).astype(np.int32), N_BUCKETS - 1)
    return np.where(d < exact, d, large).astype(np.int32)


def _bias_kernel(rb_ref, bucket_ref, o_ref):
    bucket = bucket_ref[...]
    for h in range(SWA_HEADS):
        acc = jnp.zeros(bucket.shape, F32)
        for b in range(N_BUCKETS):
            acc = jnp.where(bucket == b, rb_ref[b, h], acc)
        o_ref[h] = acc


def _bias_call(rel_bias):
    bucket = jnp.asarray(_bucket_table())
    return pl.pallas_call(
        _bias_kernel,
        in_specs=[pl.BlockSpec(memory_space=pltpu.SMEM),
                  pl.BlockSpec((WINDOW, 2 * WINDOW), lambda: (0, 0))],
        out_specs=pl.BlockSpec((SWA_HEADS, WINDOW, 2 * WINDOW), lambda: (0, 0, 0)),
        out_shape=jax.ShapeDtypeStruct((SWA_HEADS, WINDOW, 2 * WINDOW), F32),
        name="swa_rel_bias_table",
    )(rel_bias, bucket)


def _softmax_sink(s, sink):
    m = jnp.maximum(jnp.max(s, axis=-1, keepdims=True), sink)
    p = jnp.exp(s - m)
    return p / (jnp.sum(p, axis=-1, keepdims=True) + jnp.exp(sink - m))


def _swa_prompt_kernel(sink_ref, q_ref, kp_ref, kc_ref, vp_ref, vc_ref, bias_ref, o_ref):
    blk = pl.program_id(1)
    ri = lax.broadcasted_iota(jnp.int32, (WINDOW, 2 * WINDOW), 0)
    ci = lax.broadcasted_iota(jnp.int32, (WINDOW, 2 * WINDOW), 1)
    dist = ri + WINDOW - ci
    valid = (dist >= 0) & (dist < WINDOW) & ((ci >= WINDOW) | (blk > 0))
    q = q_ref[...] * (SWA_HD ** -0.5)
    keys = jnp.concatenate([kp_ref[...], kc_ref[...]], axis=0)
    vals = jnp.concatenate([vp_ref[...], vc_ref[...]], axis=0)
    outs = []
    for hk in range(SWA_KV_HEADS):
        k_h = keys[:, hk * SWA_HD:(hk + 1) * SWA_HD]
        v_h = vals[:, hk * SWA_HD:(hk + 1) * SWA_HD]
        for g in range(SWA_GROUP):
            head = hk * SWA_GROUP + g
            s = _dot_nt(q[:, head * SWA_HD:(head + 1) * SWA_HD], k_h)
            s = jnp.where(valid, s + bias_ref[head], NEG_INF)
            outs.append(_dot(_softmax_sink(s, sink_ref[head]), v_h))
    o_ref[...] = jnp.concatenate(outs, axis=1)


def _swa_prompt_call(sinks, sq, sk, sv, bias, bsz, seq):
    nblk = seq // WINDOW
    cur = lambda b, i: (b * nblk + i, 0)
    prv = lambda b, i: (b * nblk + jnp.maximum(i - 1, 0), 0)
    return pl.pallas_call(
        _swa_prompt_kernel,
        grid=(bsz, nblk),
        in_specs=[pl.BlockSpec(memory_space=pltpu.SMEM),
                  pl.BlockSpec((WINDOW, SWA_Q), cur),
                  pl.BlockSpec((WINDOW, SWA_KV), prv),
                  pl.BlockSpec((WINDOW, SWA_KV), cur),
                  pl.BlockSpec((WINDOW, SWA_KV), prv),
                  pl.BlockSpec((WINDOW, SWA_KV), cur),
                  pl.BlockSpec((SWA_HEADS, WINDOW, 2 * WINDOW), lambda b, i: (0, 0, 0))],
        out_specs=pl.BlockSpec((WINDOW, SWA_Q), cur),
        out_shape=jax.ShapeDtypeStruct((bsz * seq, SWA_Q), F32),
        compiler_params=_cparams(("arbitrary", "arbitrary")),
        name="swa_banded",
    )(sinks, sq, sk, sk, sv, sv, bias)


def _swa_sample_kernel(sink_ref, q_ref, kn_ref, vn_ref, kc_ref, vc_ref, bias_ref, o_ref, ko_ref, vo_ref, *, nb, lt):
    ri = lax.broadcasted_iota(jnp.int32, (lt, 2 * WINDOW), 0)
    ci = lax.broadcasted_iota(jnp.int32, (lt, 2 * WINDOW), 1)
    dist = ri + WINDOW - ci
    valid1 = (dist >= 0) & (dist < WINDOW)
    valid = jnp.concatenate([valid1] * SWA_GROUP, axis=0)
    q_all = q_ref[...] * (SWA_HD ** -0.5)
    kn_all = kn_ref[...]
    vn_all = vn_ref[...]
    pad = jnp.zeros((WINDOW - lt, SWA_HD), F32)
    rows_out = []
    for s in range(nb):
        lo = s * lt
        kc = kc_ref[s]
        vc = vc_ref[s]
        kn = kn_all[lo:lo + lt]
        vn = vn_all[lo:lo + lt]
        ko_ref[s] = jnp.concatenate([kc[lt:], kn], axis=0)
        vo_ref[s] = jnp.concatenate([vc[lt:], vn], axis=0)
        q = q_all[lo:lo + lt]
        pieces = []
        for hk in range(SWA_KV_HEADS):
            cols = slice(hk * SWA_HD, (hk + 1) * SWA_HD)
            k_h = jnp.concatenate([kc[:, cols], kn[:, cols], pad], axis=0)
            v_h = jnp.concatenate([vc[:, cols], vn[:, cols], pad], axis=0)
            heads = range(hk * SWA_GROUP, (hk + 1) * SWA_GROUP)
            q4 = jnp.concatenate([q[:, h * SWA_HD:(h + 1) * SWA_HD] for h in heads], axis=0)
            b4 = jnp.concatenate([bias_ref[h, 0:lt, :] for h in heads], axis=0)
            sink4 = jnp.concatenate([jnp.full((lt, 1), sink_ref[h], F32) for h in heads], axis=0)
            sc = lax.dot_general(q4, k_h, (((1,), (1,)), ((), ())), preferred_element_type=F32)
            sc = jnp.where(valid, sc + b4, NEG_INF)
            o4 = jnp.dot(_softmax_sink(sc, sink4), v_h, preferred_element_type=F32)
            pieces += [o4[g * lt:(g + 1) * lt] for g in range(SWA_GROUP)]
        rows_out.append(jnp.concatenate(pieces, axis=1))
    o_ref[...] = jnp.concatenate(rows_out, axis=0)


def _swa_sample_call(sinks, sq, sk, sv, cache_k, cache_v, bias, bsz, seq):
    nb = UNIT // seq
    rows = lambda i: (i, 0)
    seqs = lambda i: (i, 0, 0)
    return pl.pallas_call(
        functools.partial(_swa_sample_kernel, nb=nb, lt=seq),
        grid=(bsz // nb,),
        in_specs=[pl.BlockSpec(memory_space=pltpu.SMEM),
                  pl.BlockSpec((UNIT, SWA_Q), rows),
                  pl.BlockSpec((UNIT, SWA_KV), rows),
                  pl.BlockSpec((UNIT, SWA_KV), rows),
                  pl.BlockSpec((nb, WINDOW, SWA_KV), seqs),
                  pl.BlockSpec((nb, WINDOW, SWA_KV), seqs),
                  pl.BlockSpec((SWA_HEADS, WINDOW, 2 * WINDOW), lambda i: (0, 0, 0))],
        out_specs=[pl.BlockSpec((UNIT, SWA_Q), rows),
                   pl.BlockSpec((nb, WINDOW, SWA_KV), seqs),
                   pl.BlockSpec((nb, WINDOW, SWA_KV), seqs)],
        out_shape=[jax.ShapeDtypeStruct((bsz * seq, SWA_Q), F32),
                   jax.ShapeDtypeStruct((bsz, WINDOW, SWA_KV), F32),
                   jax.ShapeDtypeStruct((bsz, WINDOW, SWA_KV), F32)],
        compiler_params=_cparams(("arbitrary",)),
        name="swa_cached",
    )(sinks, sq, sk, sv, cache_k, cache_v, bias)


def _post_kernel(*refs, nb, lt, carry):
    if carry:
        (odn_ref, oswa_ref, x_ref, mod_ref, wout_ref, nfw_ref, wup_ref, cw_ref, cb_ref, wdn_ref, fnw_ref,
         y_ref, fbuf_ref, h_scr, x1_scr, acc_scr) = refs
        prev_ref = fbuf_ref

        @pl.when(pl.program_id(1) == 0)
        def _():
            fbuf_ref[...] = jnp.zeros_like(fbuf_ref)
    else:
        (odn_ref, oswa_ref, x_ref, mod_ref, wout_ref, nfw_ref, wup_ref, cw_ref, cb_ref, wdn_ref, fnw_ref,
         prev_ref, y_ref, fbuf_ref, h_scr, x1_scr, acc_scr) = refs
    rows = nb * lt
    attn = (jnp.dot(_bf(odn_ref[...]), wout_ref[0:DN_V, :], preferred_element_type=F32)
            + jnp.dot(_bf(oswa_ref[...]), wout_ref[DN_V:, :], preferred_element_type=F32))
    x1 = x_ref[...].reshape(rows, D_MODEL) + _rows(mod_ref[:, 2:3, :], nb, lt) * attn
    x1_scr[...] = x1
    h = _rms(x1, nfw_ref[...]) * (1.0 + _rows(mod_ref[:, 4:5, :], nb, lt)) + _rows(mod_ref[:, 3:4, :], nb, lt)
    h_scr[...] = _bf(h)
    acc_scr[...] = jnp.zeros_like(acc_scr)
    n_chunks = D_FF // FFN_CHUNK

    def col_slices(c):
        return [slice(base + c * FFN_CHUNK, base + (c + 1) * FFN_CHUNK) for base in (0, D_FF)]

    def up_proj(c):
        return [jnp.dot(h_scr[...], wup_ref[:, cols], preferred_element_type=F32) for cols in col_slices(c)]

    u_next = up_proj(0)
    for c in range(n_chunks):
        u_cur = u_next
        if c + 1 < n_chunks:
            u_next = up_proj(c + 1)
        halves = []
        for u, cols in zip(u_cur, col_slices(c)):
            prev = prev_ref[:, :, cols]
            halves.append(_causal_conv(u, prev, cw_ref[:, cols], nb, lt) + cb_ref[:, cols])
            fbuf_ref[:, :, cols] = u.reshape(nb, lt, FFN_CHUNK)[:, lt - (FFN_CONV - 1):, :]
        act = _silu(halves[0]) * halves[1]
        acc_scr[...] += jnp.dot(_bf(act), wdn_ref[c * FFN_CHUNK:(c + 1) * FFN_CHUNK, :],
                                preferred_element_type=F32)
    x2 = x1_scr[...] + _rows(mod_ref[:, 5:6, :], nb, lt) * acc_scr[...]
    y_ref[...] = _rms(x2, fnw_ref[...]).reshape(nb, lt, D_MODEL)


def _post_call(o_dn, o_swa, x, mod3, w_out, norm_ffn_w, w_up, conv_w, conv_b, w_down, final_w, nb, lt, state=None):
    carry = state is None
    bsz, seq, _ = x.shape
    nt = seq // lt
    rows = nb * lt
    row_map = lambda i, j: (i * nt + j, 0)
    const = lambda i, j: (0, 0)
    in_specs = [pl.BlockSpec((rows, DN_V), row_map),
                pl.BlockSpec((rows, SWA_Q), row_map),
                pl.BlockSpec((nb, lt, D_MODEL), lambda i, j: (i, j, 0)),
                pl.BlockSpec((nb, 6, D_MODEL), lambda i, j: (i, 0, 0)),
                pl.BlockSpec((D_MODEL, D_MODEL), const),
                pl.BlockSpec((1, D_MODEL), const),
                pl.BlockSpec((D_MODEL, 2 * D_FF), const),
                pl.BlockSpec((FFN_CONV, 2 * D_FF), const),
                pl.BlockSpec((1, 2 * D_FF), const),
                pl.BlockSpec((D_FF, D_MODEL), const),
                pl.BlockSpec((1, D_MODEL), const)]
    args = [o_dn, o_swa, x, mod3, w_out, norm_ffn_w, w_up, conv_w, conv_b, w_down, final_w]
    if not carry:
        in_specs.append(pl.BlockSpec((nb, FFN_CONV - 1, 2 * D_FF), lambda i, j: (i, 0, 0)))
        args.append(state)
    return pl.pallas_call(
        functools.partial(_post_kernel, nb=nb, lt=lt, carry=carry),
        grid=(bsz // nb, nt),
        in_specs=in_specs,
        out_specs=[pl.BlockSpec((nb, lt, D_MODEL), lambda i, j: (i, j, 0)),
                   pl.BlockSpec((nb, FFN_CONV - 1, 2 * D_FF), lambda i, j: (i, 0, 0))],
        out_shape=[jax.ShapeDtypeStruct((bsz, seq, D_MODEL), F32),
                   jax.ShapeDtypeStruct((bsz, FFN_CONV - 1, 2 * D_FF), F32)],
        scratch_shapes=[pltpu.VMEM((rows, D_MODEL), BF16),
                        pltpu.VMEM((rows, D_MODEL), F32),
                        pltpu.VMEM((rows, D_MODEL), F32)],
        compiler_params=_cparams(("arbitrary", "arbitrary")),
        name="out_proj_convffn_carry" if carry else "out_proj_convffn_state",
    )(*args)


def _pad_row(vec, offset):
    return jnp.zeros((1, BA_PAD), F32).at[0, offset:offset + vec.shape[0]].set(vec)


def kernel(x_prompt, x_sample, state_dn_conv, state_dn_ssm, cache_swa_k, cache_swa_v, state_ffn_conv, c_prompt, c_sample, rel_bias, final_norm_w, w_ada, b_ada, norm_mix_w, w_in, dn_conv_w, dn_A_log, dn_dt_bias, dn_norm_w, swa_sinks, w_out, norm_ffn_w, ffn_w_up, ffn_conv_w, ffn_conv_b, ffn_w_down):
    bp, lp, _ = x_prompt.shape
    bs, ls, _ = x_sample.shape
    layer = 0

    w_i = w_in[layer]
    n_ba = 2 * DN_HEADS
    ba_lo = DN_CONV_CH + DN_V
    w_cat = jnp.concatenate(
        [w_i[:, :ba_lo], jnp.pad(w_i[:, ba_lo:ba_lo + n_ba], ((0, 0), (0, BA_PAD - n_ba))), w_i[:, ba_lo + n_ba:]],
        axis=1).astype(BF16)
    w_out_b = w_out[layer].astype(BF16)
    w_up_b = ffn_w_up[layer].astype(BF16)
    w_dn_b = ffn_w_down[layer].astype(BF16)
    alog_row = _pad_row(dn_A_log[layer], DN_HEADS)
    dt_row = _pad_row(dn_dt_bias[layer], DN_HEADS)
    row = lambda v: v.reshape(1, -1)

    n_c = bp + bs
    n_c_pad = -(-n_c // 8) * 8
    c_all = jnp.pad(jnp.concatenate([c_prompt, c_sample], axis=0), ((0, n_c_pad - n_c), (0, 0)))
    mod = _mod_call(c_all, w_ada[layer], row(b_ada[layer]))
    mod_p = mod[:bp].reshape(bp, 6, D_MODEL)
    mod_s = mod[bp:n_c].reshape(bs, 6, D_MODEL)

    bias = _bias_call(rel_bias)
    sinks = swa_sinks[layer]

    def mixer_in(x, mod3, nb, lt):
        return _in_call(x, mod3, row(norm_mix_w[layer]), w_cat, nb, lt)

    def post(o_dn, o_swa, x, mod3, nb, lt, state=None):
        return _post_call(o_dn, o_swa, x, mod3, w_out_b, row(norm_ffn_w[layer]), w_up_b, ffn_conv_w[layer],
                          row(ffn_conv_b[layer]), w_dn_b, row(final_norm_w), nb, lt, state)

    dn_args = (dn_conv_w[layer], alog_row, dt_row, row(dn_norm_w[layer]))

    qkv, z, ba, sq, sk, sv = mixer_in(x_prompt, mod_p, 1, 512)
    o_dn, p_ssm = _dn_call(qkv, z, ba, *dn_args, bp, lp)
    o_swa = _swa_prompt_call(sinks, sq, sk, sv, bias, bp, lp)
    y_prompt, p_fbuf = post(o_dn, o_swa, x_prompt, mod_p, 1, 256)
    p_dn_conv = qkv.reshape(bp, lp, DN_CONV_CH)[:, lp - (DN_CONV - 1):]
    p_swa_k = sk.reshape(bp, lp, SWA_KV_HEADS, SWA_HD)[:, lp - WINDOW:]
    p_swa_v = sv.reshape(bp, lp, SWA_KV_HEADS, SWA_HD)[:, lp - WINDOW:]

    qkv_s, z_s, ba_s, sq_s, sk_s, sv_s = mixer_in(x_sample, mod_s, 64, ls)
    o_dn_s, s_ssm = _dn_call(qkv_s, z_s, ba_s, *dn_args, bs, ls, state=(state_dn_conv[layer], state_dn_ssm[layer]))
    o_swa_s, s_k, s_v = _swa_sample_call(sinks, sq_s, sk_s, sv_s,
                                         cache_swa_k[layer].reshape(bs, WINDOW, SWA_KV),
                                         cache_swa_v[layer].reshape(bs, WINDOW, SWA_KV), bias, bs, ls)
    y_sample, s_fbuf = post(o_dn_s, o_swa_s, x_sample, mod_s, 32, ls, state=state_ffn_conv[layer])
    s_dn_conv = qkv_s.reshape(bs, ls, DN_CONV_CH)[:, ls - (DN_CONV - 1):]

    return (y_prompt, y_sample, p_dn_conv[None], s_dn_conv[None], p_ssm[None], s_ssm[None],
            p_swa_k[None], s_k.reshape(bs, WINDOW, SWA_KV_HEADS, SWA_HD)[None],
            p_swa_v[None], s_v.reshape(bs, WINDOW, SWA_KV_HEADS, SWA_HD)[None],
            p_fbuf[None], s_fbuf[None])
```

```python
import functools
import math

import numpy as np
import jax
import jax.numpy as jnp
from jax import lax
from jax.experimental import pallas as pl
from jax.experimental.pallas import tpu as pltpu

F32 = jnp.float32
BF16 = jnp.bfloat16

D_MODEL = 1024
PAST_LEN = 16384
DN_HEADS = 4
DN_DK = 128
DN_DV = 128
DN_CONV = 4
SWA_HEADS = 8
SWA_KV_HEADS = 2
SWA_GROUP = SWA_HEADS // SWA_KV_HEADS
SWA_HD = 64
WINDOW = 128
N_BUCKETS = 32
MAX_DISTANCE = 128
D_FF = 2816
FFN_CONV = 3
EPS = 1e-6
NEG_INF = -1e30

DN_QK = DN_HEADS * DN_DK
DN_V = DN_HEADS * DN_DV
DN_CONV_CH = 2 * DN_QK + DN_V
SWA_Q = SWA_HEADS * SWA_HD
SWA_KV = SWA_KV_HEADS * SWA_HD
BA_PAD = 128
SUBLANES = 8
UNIT = 128
FFN_CHUNK = 256
VMEM_LIMIT = 56 * 1024 * 1024


def _cparams(sem):
    return pltpu.CompilerParams(dimension_semantics=sem, vmem_limit_bytes=VMEM_LIMIT)


def _bf(x):
    return x.astype(BF16)


def _dot(a, b):
    return jnp.dot(_bf(a), _bf(b), preferred_element_type=F32)


def _dot_nt(a, b):
    return lax.dot_general(_bf(a), _bf(b), (((1,), (1,)), ((), ())), preferred_element_type=F32)


def _split2(x):
    hi = _bf(x)
    lo = _bf(x - hi.astype(F32))
    return hi, lo


def _dot_x3(a, b):
    ah, al = _split2(a)
    bh, bl = _split2(b)
    d = functools.partial(jnp.dot, preferred_element_type=F32)
    return d(ah, bh) + (d(al, bh) + d(ah, bl))


def _dot_mask(m, x):
    hi = _bf(x)
    r = x - hi.astype(F32)
    mid = _bf(r)
    lo = _bf(r - mid.astype(F32))
    d = functools.partial(jnp.dot, preferred_element_type=F32)
    return d(m, hi) + (d(m, mid) + d(m, lo))


def _sigmoid(x):
    return 1.0 / (1.0 + jnp.exp(-x))


def _silu(x):
    return x * _sigmoid(x)


def _softplus(x):
    return jnp.maximum(x, 0.0) + jnp.log1p(jnp.exp(-jnp.abs(x)))


def _rms(x, w):
    ms = jnp.mean(x * x, axis=-1, keepdims=True)
    return x * lax.rsqrt(ms + EPS) * w


def _l2norm(t):
    return t * lax.rsqrt(jnp.sum(t * t, axis=-1, keepdims=True) + EPS)


def _rows(m3, nb, lt):
    return jnp.broadcast_to(m3, (nb, lt, m3.shape[-1])).reshape(nb * lt, m3.shape[-1])


def _causal_conv(x, prev, w, nb, lt):
    width = w.shape[0]
    rows, ch = x.shape
    head = SUBLANES if (nb == 1 and lt > SUBLANES) else rows
    lh = min(lt, head)
    tmod = lax.broadcasted_iota(jnp.int32, (head, 1), 0) & (lh - 1)
    out = x * w[width - 1:width, :]
    for j in range(1, width):
        sh = pltpu.roll(x, j, axis=0)
        top = sh[:head]
        for t in range(j):
            p = width - 1 - j + t
            top = jnp.where(tmod == t, _rows(prev[:, p:p + 1, :], nb, lh), top)
        sh = top if head == rows else jnp.concatenate([top, sh[head:]], axis=0)
        out = out + sh * w[width - 1 - j:width - j, :]
    return out


def _mod_kernel(c_ref, w_ref, b_ref, o_ref):
    o_ref[...] = _dot_x3(_silu(c_ref[...]), w_ref[...]) + b_ref[...]


def _mod_call(c_all, w_ada, b_ada):
    rows = c_all.shape[0]
    cols = w_ada.shape[1]
    tile = 512
    return pl.pallas_call(
        _mod_kernel,
        grid=(cols // tile,),
        in_specs=[pl.BlockSpec((rows, D_MODEL), lambda j: (0, 0)),
                  pl.BlockSpec((D_MODEL, tile), lambda j: (0, j)),
                  pl.BlockSpec((1, tile), lambda j: (0, j))],
        out_specs=pl.BlockSpec((rows, tile), lambda j: (0, j)),
        out_shape=jax.ShapeDtypeStruct((rows, cols), F32),
        compiler_params=_cparams(("arbitrary",)),
        name="adaln_mod",
    )(c_all, w_ada, b_ada)


IN_SPLIT = (DN_CONV_CH, DN_V, BA_PAD, SWA_Q, SWA_KV, SWA_KV)
IN_COLS_PAD = sum(IN_SPLIT)


def _in_kernel(x_ref, mod_ref, nw_ref, w_ref, *out_refs, nb, lt):
    x = x_ref[...]
    ms = jnp.mean(x * x, axis=-1, keepdims=True)
    y = x * lax.rsqrt(ms + EPS) * nw_ref[...]
    h = y * (1.0 + mod_ref[:, 1:2, :]) + mod_ref[:, 0:1, :]
    h = _bf(h.reshape(nb * lt, D_MODEL))
    off = 0
    for o_ref, n in zip(out_refs, IN_SPLIT):
        o_ref[...] = jnp.dot(h, w_ref[:, off:off + n], preferred_element_type=F32)
        off += n


def _in_call(x, mod3, norm_w, w_cat, nb, lt):
    bsz, seq, _ = x.shape
    nt = seq // lt
    rows = nb * lt
    n_tok = bsz * seq
    row_map = lambda i, j: (i * nt + j, 0)
    return pl.pallas_call(
        functools.partial(_in_kernel, nb=nb, lt=lt),
        grid=(bsz // nb, nt),
        in_specs=[pl.BlockSpec((nb, lt, D_MODEL), lambda i, j: (i, j, 0)),
                  pl.BlockSpec((nb, 6, D_MODEL), lambda i, j: (i, 0, 0)),
                  pl.BlockSpec((1, D_MODEL), lambda i, j: (0, 0)),
                  pl.BlockSpec((D_MODEL, IN_COLS_PAD), lambda i, j: (0, 0))],
        out_specs=[pl.BlockSpec((rows, n), row_map) for n in IN_SPLIT],
        out_shape=[jax.ShapeDtypeStruct((n_tok, n), F32) for n in IN_SPLIT],
        compiler_params=_cparams(("arbitrary", "arbitrary")),
        name="norm_in_proj",
    )(x, mod3, norm_w, w_cat)


INV_BASE_SHIFT = 2
DN_UNITS_CARRY = 2
SWA_BLOCKS_PER_STEP = 4
DN_UNITS_STATE = 1


def _unit_lower_inverses(a_mats, ri, ci, chunk_shift):
    def blocks(s):
        return (ri >> s) == (ci >> s)

    base = min(INV_BASE_SHIFT, chunk_shift)
    xs = [jnp.where(blocks(base), -a, 0.0) for a in a_mats]
    ts = [jnp.where(ri == ci, 1.0, n) for n in xs]
    for lvl in range(1, base):
        xs = [_dot(x, x) for x in xs]
        ts = [t + _dot(t, x) for t, x in zip(ts, xs)]
    for s in range(base, chunk_shift):
        sel = blocks(s + 1) & jnp.logical_not(blocks(s))
        ets = [_dot(jnp.where(sel, a, 0.0), t) for a, t in zip(a_mats, ts)]
        ts = [t - _dot(t, et) for t, et in zip(ts, ets)]
    return ts


def _dn_kernel(*refs, nb, chunk, carry, units):
    if carry:
        (qkv_ref, z_ref, ba_ref, cw_ref, alog_ref, dt_ref, nw_ref, o_ref, s_ref, cst_ref) = refs
        s0_ref = s_ref

        @pl.when(pl.program_id(1) == 0)
        def _():
            cst_ref[...] = jnp.zeros_like(cst_ref)
            s_ref[...] = jnp.zeros_like(s_ref)
    else:
        (qkv_ref, z_ref, ba_ref, cw_ref, alog_ref, dt_ref, nw_ref, cst_ref, s0_ref, o_ref, s_ref) = refs

    rows = units * UNIT
    seq_rows = rows // nb
    ri = lax.broadcasted_iota(jnp.int32, (UNIT, UNIT), 0)
    ci = lax.broadcasted_iota(jnp.int32, (UNIT, UNIT), 1)
    shift = int(math.log2(chunk))
    same = (ri >> shift) == (ci >> shift)
    incl = same & (ri >= ci)
    strict = same & (ri > ci)

    prev = cst_ref[...]
    x = qkv_ref[...]
    cw = cw_ref[...]

    ba = ba_ref[...]
    beta_full = _sigmoid(ba)
    g_full = -jnp.exp(alog_ref[...]) * _softplus(ba + dt_ref[...])
    masks = jnp.concatenate([jnp.where(incl, 1.0, 0.0), jnp.where(same, 1.0, 0.0)], axis=0).astype(BF16)
    gsums = [_dot_mask(masks, g_full[u * UNIT:(u + 1) * UNIT]) for u in range(units)]
    g_cum = [g[:UNIT] for g in gsums]
    g_tot = [g[UNIT:] for g in gsums]
    g_cum_t = [g.T for g in g_cum]

    def head_cols(base, h):
        lo = base + h * DN_DK
        return _silu(_causal_conv(x[:, lo:lo + DN_DK], prev[:, :, lo:lo + DN_DK], cw[:, lo:lo + DN_DK],
                                  nb, seq_rows))

    q_h = [_l2norm(head_cols(0, h)) * (DN_DK ** -0.5) for h in range(DN_HEADS)]
    k_h = [_l2norm(head_cols(DN_QK, h)) for h in range(DN_HEADS)]
    v_h = [head_cols(2 * DN_QK, h) for h in range(DN_HEADS)]

    probs = [(u, h) for u in range(units) for h in range(DN_HEADS)]
    n_p = len(probs)
    rs = lambda u: slice(u * UNIT, (u + 1) * UNIT)
    q = [q_h[h][rs(u)] for u, h in probs]
    k = [k_h[h][rs(u)] for u, h in probs]
    v = [v_h[h][rs(u)] for u, h in probs]
    gc = [g_cum[u][:, DN_HEADS + h:DN_HEADS + h + 1] for u, h in probs]
    gr = [g_cum_t[u][DN_HEADS + h:DN_HEADS + h + 1, :] for u, h in probs]
    gt = [g_tot[u][:, DN_HEADS + h:DN_HEADS + h + 1] for u, h in probs]
    bc = [beta_full[rs(u), h:h + 1] for u, h in probs]
    decay = [jnp.where(incl, jnp.exp(jnp.where(incl, gc[p] - gr[p], 0.0)), 0.0) for p in range(n_p)]
    e_g = [jnp.exp(gc[p]) for p in range(n_p)]
    kq = [_dot_nt(jnp.concatenate([k[p], q[p]], axis=0), k[p]) for p in range(n_p)]
    qk = [kq[p][UNIT:] * decay[p] for p in range(n_p)]
    a_mats = [jnp.where(strict, bc[p] * kq[p][:UNIT] * decay[p], 0.0) for p in range(n_p)]
    t_inv = _unit_lower_inverses(a_mats, ri, ci, shift)
    wvk = [_dot(t_inv[p], jnp.concatenate([v[p] * bc[p], k[p] * (bc[p] * e_g[p])], axis=1)) for p in range(n_p)]
    w_v = [w[:, :DN_DV] for w in wvk]
    w_k = [w[:, DN_DV:] for w in wvk]
    q_dec = [q[p] * e_g[p] for p in range(n_p)]
    k_tail_t = [(k[p] * jnp.exp(gt[p] - gc[p])).T for p in range(n_p)]
    c_dec = [jnp.exp(gt[p]) for p in range(n_p)]

    outs = {}
    if carry:
        state = [s_ref[0, h] for h in range(DN_HEADS)]
        for u in range(units):
            ps = [u * DN_HEADS + h for h in range(DN_HEADS)]
            r = [_dot(jnp.concatenate([w_k[p], q_dec[p]], axis=0), state[h]) for h, p in enumerate(ps)]
            uu = [w_v[p] - r[h][:UNIT] for h, p in enumerate(ps)]
            for h, p in enumerate(ps):
                outs[p] = r[h][UNIT:] + _dot(qk[p], uu[h])
            state = [state[h] * c_dec[p][0:1, :] + _dot(k_tail_t[p], uu[h]) for h, p in enumerate(ps)]
        for h in range(DN_HEADS):
            s_ref[0, h] = state[h]
    else:
        per_unit = nb // units
        for p, (u, h) in enumerate(probs):
            us, qs = [], []
            for s in range(per_unit):
                lo = s * chunk
                lhs = jnp.concatenate([w_k[p][lo:lo + chunk], q_dec[p][lo:lo + chunk]], axis=0)
                r = jnp.dot(lhs, s0_ref[u * per_unit + s, h], preferred_element_type=F32)
                us.append(w_v[p][lo:lo + chunk] - r[:chunk])
                qs.append(r[chunk:])
            uu = jnp.concatenate(us, axis=0)
            outs[p] = jnp.concatenate(qs, axis=0) + _dot(qk[p], uu)
            for s in range(per_unit):
                lo = s * chunk
                upd = jnp.dot(k_tail_t[p][:, lo:lo + chunk], uu[lo:lo + chunk], preferred_element_type=F32)
                s_ref[u * per_unit + s, h] = s0_ref[u * per_unit + s, h] * c_dec[p][lo:lo + 1, :] + upd

    for p, (u, h) in enumerate(probs):
        zz = z_ref[rs(u), h * DN_DV:(h + 1) * DN_DV]
        o_ref[rs(u), h * DN_DV:(h + 1) * DN_DV] = _rms(outs[p], nw_ref[...]) * _silu(zz)

    if carry:
        cst_ref[0] = x[rows - (DN_CONV - 1):, :]


def _dn_call(qkv, z, ba, conv_w, alog_row, dt_row, norm_w, bsz, seq, state=None):
    carry = state is None
    units = DN_UNITS_CARRY if carry else DN_UNITS_STATE
    rows = units * UNIT
    nb = 1 if carry else rows // seq
    chunk = UNIT if carry else seq
    nt = seq * nb // rows
    n_tok = bsz * seq
    row_map = lambda i, j: (i * nt + j, 0)
    const = lambda i, j: (0, 0)
    in_specs = [pl.BlockSpec((rows, DN_CONV_CH), row_map),
                pl.BlockSpec((rows, DN_V), row_map),
                pl.BlockSpec((rows, BA_PAD), row_map),
                pl.BlockSpec((DN_CONV, DN_CONV_CH), const),
                pl.BlockSpec((1, BA_PAD), const),
                pl.BlockSpec((1, BA_PAD), const),
                pl.BlockSpec((1, DN_DV), const)]
    args = [qkv, z, ba, conv_w, alog_row, dt_row, norm_w]
    scratch = []
    if carry:
        scratch = [pltpu.VMEM((1, DN_CONV - 1, DN_CONV_CH), F32)]
    else:
        conv_state, s0 = state
        in_specs += [pl.BlockSpec((nb, DN_CONV - 1, DN_CONV_CH), lambda i, j: (i, 0, 0)),
                     pl.BlockSpec((nb, DN_HEADS, DN_DK, DN_DV), lambda i, j: (i, 0, 0, 0))]
        args += [conv_state, s0]
    return pl.pallas_call(
        functools.partial(_dn_kernel, nb=nb, chunk=chunk, carry=carry, units=units),
        grid=(bsz // nb, nt),
        in_specs=in_specs,
        out_specs=[pl.BlockSpec((rows, DN_V), row_map),
                   pl.BlockSpec((nb, DN_HEADS, DN_DK, DN_DV), lambda i, j: (i, 0, 0, 0))],
        out_shape=[jax.ShapeDtypeStruct((n_tok, DN_V), F32),
                   jax.ShapeDtypeStruct((bsz, DN_HEADS, DN_DK, DN_DV), F32)],
        scratch_shapes=scratch,
        compiler_params=_cparams(("arbitrary", "arbitrary")),
        name="gated_deltanet_carry" if carry else "gated_deltanet_state",
    )(*args)


def _bucket_table():
    i = np.arange(WINDOW, dtype=np.int64)[:, None]
    j = np.arange(2 * WINDOW, dtype=np.int64)[None, :]
    d = np.maximum(i + WINDOW - j, 0)
    exact = N_BUCKETS // 2
    logv = (np.log(np.maximum(d, 1).astype(np.float32) / np.float32(exact)).astype(np.float32)
            / np.float32(math.log(MAX_DISTANCE / exact)))
    large = np.minimum(exact + (logv * np.float32(N_BUCKETS - exact)).astype(np.int32), N_BUCKETS - 1)
    return np.where(d < exact, d, large).astype(np.int32)


def _bias_kernel(rb_ref, bucket_ref, o_ref):
    bucket = bucket_ref[...]
    ri = lax.broadcasted_iota(jnp.int32, bucket.shape, 0)
    ci = lax.broadcasted_iota(jnp.int32, bucket.shape, 1)
    dist = ri + WINDOW - ci
    valid = (dist >= 0) & (dist < WINDOW)
    for hk in range(SWA_KV_HEADS):
        for par in range(2):
            for st in range(SWA_GROUP // 2):
                head = hk * SWA_GROUP + 2 * st + par
                acc = jnp.zeros(bucket.shape, F32)
                for b in range(N_BUCKETS):
                    acc = jnp.where(bucket == b, rb_ref[b, head], acc)
                gen = jnp.where(valid, acc, NEG_INF)
                o_ref[1, hk, par, st * WINDOW:(st + 1) * WINDOW, :] = gen
                o_ref[0, hk, par, st * WINDOW:(st + 1) * WINDOW, :] = jnp.where(ci >= WINDOW, gen, NEG_INF)


BIAS_SHAPE = (2, SWA_KV_HEADS, 2, (SWA_GROUP // 2) * WINDOW, 2 * WINDOW)


def _bias_call(rel_bias):
    bucket = jnp.asarray(_bucket_table())
    return pl.pallas_call(
        _bias_kernel,
        in_specs=[pl.BlockSpec(memory_space=pltpu.SMEM),
                  pl.BlockSpec((WINDOW, 2 * WINDOW), lambda: (0, 0))],
        out_specs=pl.BlockSpec(BIAS_SHAPE, lambda: (0,) * len(BIAS_SHAPE)),
        out_shape=jax.ShapeDtypeStruct(BIAS_SHAPE, F32),
        name="swa_rel_bias_table",
    )(rel_bias, bucket)


def _softmax_sink(s, sink):
    m = jnp.maximum(jnp.max(s, axis=-1, keepdims=True), sink)
    p = jnp.exp(s - m)
    return p / (jnp.sum(p, axis=-1, keepdims=True) + jnp.exp(sink - m))


def _softmax_sink_parts(s, sink):
    m = jnp.maximum(jnp.max(s, axis=-1, keepdims=True), sink)
    p = jnp.exp(s - m)
    return p, 1.0 / (jnp.sum(p, axis=-1, keepdims=True) + jnp.exp(sink - m))


def _half_lane_variants(full, hk, lo_half):
    rolled = pltpu.roll(full, SWA_HD, axis=1)
    low_src, high_src = (full, rolled) if hk == 0 else (rolled, full)
    return jnp.where(lo_half, low_src, 0.0), jnp.where(lo_half, 0.0, high_src)


def _swa_prompt_kernel(sink_ref, q_ref, kp_ref, kc_ref, vp_ref, vc_ref, bias_ref, o_ref, *, n_blk):
    step = pl.program_id(1)
    lo_half = lax.broadcasted_iota(jnp.int32, (1, 2 * SWA_HD), 1) < SWA_HD
    q = q_ref[...] * (SWA_HD ** -0.5)
    keys = jnp.concatenate([kp_ref[...], kc_ref[...]], axis=0)
    vals = jnp.concatenate([vp_ref[...], vc_ref[...]], axis=0)
    k_var = [_half_lane_variants(keys, hk, lo_half) for hk in range(SWA_KV_HEADS)]
    v_var = [_half_lane_variants(vals, hk, lo_half) for hk in range(SWA_KV_HEADS)]
    n_stack = SWA_GROUP // 2
    sinks = [[jnp.concatenate([jnp.full((WINDOW, 1), sink_ref[hk * SWA_GROUP + 2 * st + par], F32)
                               for st in range(n_stack)], axis=0) for par in range(2)]
             for hk in range(SWA_KV_HEADS)]

    def scores(b):
        rows = slice(b * WINDOW, (b + 1) * WINDOW)
        win = slice(b * WINDOW, (b + 2) * WINDOW)
        variant = jnp.where(step == 0, 0, 1) if b == 0 else 1
        out = []
        for hk in range(SWA_KV_HEADS):
            q2 = jnp.concatenate([q[rows, (hk * n_stack + st) * 2 * SWA_HD:(hk * n_stack + st + 1) * 2 * SWA_HD]
                                  for st in range(n_stack)], axis=0)
            out.append([_dot_nt(q2, k_var[hk][par][win]) + bias_ref[variant, hk, par] for par in range(2)])
        return out

    s_next = scores(0)
    for b in range(n_blk):
        s_cur = s_next
        if b + 1 < n_blk:
            s_next = scores(b + 1)
        rows = slice(b * WINDOW, (b + 1) * WINDOW)
        win = slice(b * WINDOW, (b + 2) * WINDOW)
        for hk in range(SWA_KV_HEADS):
            parts = [_softmax_sink_parts(s_cur[hk][par], sinks[hk][par]) for par in range(2)]
            o2 = _dot(parts[0][0], v_var[hk][0][win]) + _dot(parts[1][0], v_var[hk][1][win])
            o2 = o2 * jnp.where(lo_half, parts[0][1], parts[1][1])
            for st in range(n_stack):
                lo = (hk * n_stack + st) * 2 * SWA_HD
                o_ref[rows, lo:lo + 2 * SWA_HD] = o2[st * WINDOW:(st + 1) * WINDOW]


def _swa_prompt_call(sinks, sq, sk, sv, bias, bsz, seq):
    n_blk = SWA_BLOCKS_PER_STEP
    tile = n_blk * WINDOW
    nt = seq // tile
    cur = lambda b, i: (b * nt + i, 0)
    prv = lambda b, i: (b * nt * n_blk + jnp.maximum(i * n_blk - 1, 0), 0)
    return pl.pallas_call(
        functools.partial(_swa_prompt_kernel, n_blk=n_blk),
        grid=(bsz, nt),
        in_specs=[pl.BlockSpec(memory_space=pltpu.SMEM),
                  pl.BlockSpec((tile, SWA_Q), cur),
                  pl.BlockSpec((WINDOW, SWA_KV), prv),
                  pl.BlockSpec((tile, SWA_KV), cur),
                  pl.BlockSpec((WINDOW, SWA_KV), prv),
                  pl.BlockSpec((tile, SWA_KV), cur),
                  pl.BlockSpec(BIAS_SHAPE, lambda b, i: (0,) * len(BIAS_SHAPE))],
        out_specs=pl.BlockSpec((tile, SWA_Q), cur),
        out_shape=jax.ShapeDtypeStruct((bsz * seq, SWA_Q), F32),
        compiler_params=_cparams(("arbitrary", "arbitrary")),
        name="swa_banded",
    )(sinks, sq, sk, sk, sv, sv, bias)


def _swa_sample_kernel(sink_ref, q_ref, kn_ref, vn_ref, kc_ref, vc_ref, bias_ref, o_ref, ko_ref, vo_ref, *, nb, lt):
    q_all = q_ref[...] * (SWA_HD ** -0.5)
    kn_all = kn_ref[...]
    vn_all = vn_ref[...]
    pad = jnp.zeros((WINDOW - lt, SWA_HD), F32)
    rows_out = []
    for s in range(nb):
        lo = s * lt
        kc = kc_ref[s]
        vc = vc_ref[s]
        kn = kn_all[lo:lo + lt]
        vn = vn_all[lo:lo + lt]
        ko_ref[s] = jnp.concatenate([kc[lt:], kn], axis=0)
        vo_ref[s] = jnp.concatenate([vc[lt:], vn], axis=0)
        q = q_all[lo:lo + lt]
        pieces = []
        for hk in range(SWA_KV_HEADS):
            cols = slice(hk * SWA_HD, (hk + 1) * SWA_HD)
            k_h = jnp.concatenate([kc[:, cols], kn[:, cols], pad], axis=0)
            v_h = jnp.concatenate([vc[:, cols], vn[:, cols], pad], axis=0)
            heads = range(hk * SWA_GROUP, (hk + 1) * SWA_GROUP)
            q4 = jnp.concatenate([q[:, h * SWA_HD:(h + 1) * SWA_HD] for h in heads], axis=0)
            b4 = jnp.concatenate([bias_ref[1, hk, g % 2, (g // 2) * WINDOW:(g // 2) * WINDOW + lt, :]
                                  for g in range(SWA_GROUP)], axis=0)
            sink4 = jnp.concatenate([jnp.full((lt, 1), sink_ref[h], F32) for h in heads], axis=0)
            sc = lax.dot_general(q4, k_h, (((1,), (1,)), ((), ())), preferred_element_type=F32)
            sc = sc + b4
            o4 = jnp.dot(_softmax_sink(sc, sink4), v_h, preferred_element_type=F32)
            pieces += [o4[g * lt:(g + 1) * lt] for g in range(SWA_GROUP)]
        rows_out.append(jnp.concatenate(pieces, axis=1))
    o_ref[...] = jnp.concatenate(rows_out, axis=0)


def _swa_sample_call(sinks, sq, sk, sv, cache_k, cache_v, bias, bsz, seq):
    nb = UNIT // seq
    rows = lambda i: (i, 0)
    seqs = lambda i: (i, 0, 0)
    return pl.pallas_call(
        functools.partial(_swa_sample_kernel, nb=nb, lt=seq),
        grid=(bsz // nb,),
        in_specs=[pl.BlockSpec(memory_space=pltpu.SMEM),
                  pl.BlockSpec((UNIT, SWA_Q), rows),
                  pl.BlockSpec((UNIT, SWA_KV), rows),
                  pl.BlockSpec((UNIT, SWA_KV), rows),
                  pl.BlockSpec((nb, WINDOW, SWA_KV), seqs),
                  pl.BlockSpec((nb, WINDOW, SWA_KV), seqs),
                  pl.BlockSpec(BIAS_SHAPE, lambda i: (0,) * len(BIAS_SHAPE))],
        out_specs=[pl.BlockSpec((UNIT, SWA_Q), rows),
                   pl.BlockSpec((nb, WINDOW, SWA_KV), seqs),
                   pl.BlockSpec((nb, WINDOW, SWA_KV), seqs)],
        out_shape=[jax.ShapeDtypeStruct((bsz * seq, SWA_Q), F32),
                   jax.ShapeDtypeStruct((bsz, WINDOW, SWA_KV), F32),
                   jax.ShapeDtypeStruct((bsz, WINDOW, SWA_KV), F32)],
        compiler_params=_cparams(("arbitrary",)),
        name="swa_cached",
    )(sinks, sq, sk, sv, cache_k, cache_v, bias)


def _post_kernel(*refs, nb, lt, carry):
    if carry:
        (odn_ref, oswa_ref, x_ref, mod_ref, wout_ref, nfw_ref, wup_ref, cw_ref, cb_ref, wdn_ref, fnw_ref,
         y_ref, fbuf_ref, h_scr, x1_scr, acc_scr) = refs
        prev_ref = fbuf_ref

        @pl.when(pl.program_id(1) == 0)
        def _():
            fbuf_ref[...] = jnp.zeros_like(fbuf_ref)
    else:
        (odn_ref, oswa_ref, x_ref, mod_ref, wout_ref, nfw_ref, wup_ref, cw_ref, cb_ref, wdn_ref, fnw_ref,
         prev_ref, y_ref, fbuf_ref, h_scr, x1_scr, acc_scr) = refs
    rows = nb * lt
    attn = (jnp.dot(_bf(odn_ref[...]), wout_ref[0:DN_V, :], preferred_element_type=F32)
            + jnp.dot(_bf(oswa_ref[...]), wout_ref[DN_V:, :], preferred_element_type=F32))
    x1 = x_ref[...].reshape(rows, D_MODEL) + _rows(mod_ref[:, 2:3, :], nb, lt) * attn
    x1_scr[...] = x1
    h = _rms(x1, nfw_ref[...]) * (1.0 + _rows(mod_ref[:, 4:5, :], nb, lt)) + _rows(mod_ref[:, 3:4, :], nb, lt)
    h_scr[...] = _bf(h)
    acc_scr[...] = jnp.zeros_like(acc_scr)
    n_chunks = D_FF // FFN_CHUNK

    def col_slices(c):
        return [slice(base + c * FFN_CHUNK, base + (c + 1) * FFN_CHUNK) for base in (0, D_FF)]

    def up_proj(c):
        return [jnp.dot(h_scr[...], wup_ref[:, cols], preferred_element_type=F32) for cols in col_slices(c)]

    u_next = up_proj(0)
    for c in range(n_chunks):
        u_cur = u_next
        if c + 1 < n_chunks:
            u_next = up_proj(c + 1)
        halves = []
        for u, cols in zip(u_cur, col_slices(c)):
            prev = prev_ref[:, :, cols]
            halves.append(_causal_conv(u, prev, cw_ref[:, cols], nb, lt) + cb_ref[:, cols])
            fbuf_ref[:, :, cols] = u.reshape(nb, lt, FFN_CHUNK)[:, lt - (FFN_CONV - 1):, :]
        act = _silu(halves[0]) * halves[1]
        acc_scr[...] += jnp.dot(_bf(act), wdn_ref[c * FFN_CHUNK:(c + 1) * FFN_CHUNK, :],
                                preferred_element_type=F32)
    x2 = x1_scr[...] + _rows(mod_ref[:, 5:6, :], nb, lt) * acc_scr[...]
    y_ref[...] = _rms(x2, fnw_ref[...]).reshape(nb, lt, D_MODEL)


def _post_call(o_dn, o_swa, x, mod3, w_out, norm_ffn_w, w_up, conv_w, conv_b, w_down, final_w, nb, lt, state=None):
    carry = state is None
    bsz, seq, _ = x.shape
    nt = seq // lt
    rows = nb * lt
    row_map = lambda i, j: (i * nt + j, 0)
    const = lambda i, j: (0, 0)
    in_specs = [pl.BlockSpec((rows, DN_V), row_map),
                pl.BlockSpec((rows, SWA_Q), row_map),
                pl.BlockSpec((nb, lt, D_MODEL), lambda i, j: (i, j, 0)),
                pl.BlockSpec((nb, 6, D_MODEL), lambda i, j: (i, 0, 0)),
                pl.BlockSpec((D_MODEL, D_MODEL), const),
                pl.BlockSpec((1, D_MODEL), const),
                pl.BlockSpec((D_MODEL, 2 * D_FF), const),
                pl.BlockSpec((FFN_CONV, 2 * D_FF), const),
                pl.BlockSpec((1, 2 * D_FF), const),
                pl.BlockSpec((D_FF, D_MODEL), const),
                pl.BlockSpec((1, D_MODEL), const)]
    args = [o_dn, o_swa, x, mod3, w_out, norm_ffn_w, w_up, conv_w, conv_b, w_down, final_w]
    if not carry:
        in_specs.append(pl.BlockSpec((nb, FFN_CONV - 1, 2 * D_FF), lambda i, j: (i, 0, 0)))
        args.append(state)
    return pl.pallas_call(
        functools.partial(_post_kernel, nb=nb, lt=lt, carry=carry),
        grid=(bsz // nb, nt),
        in_specs=in_specs,
        out_specs=[pl.BlockSpec((nb, lt, D_MODEL), lambda i, j: (i, j, 0)),
                   pl.BlockSpec((nb, FFN_CONV - 1, 2 * D_FF), lambda i, j: (i, 0, 0))],
        out_shape=[jax.ShapeDtypeStruct((bsz, seq, D_MODEL), F32),
                   jax.ShapeDtypeStruct((bsz, FFN_CONV - 1, 2 * D_FF), F32)],
        scratch_shapes=[pltpu.VMEM((rows, D_MODEL), BF16),
                        pltpu.VMEM((rows, D_MODEL), F32),
                        pltpu.VMEM((rows, D_MODEL), F32)],
        compiler_params=_cparams(("arbitrary", "arbitrary")),
        name="out_proj_convffn_carry" if carry else "out_proj_convffn_state",
    )(*args)


def _pad_row(vec, offset):
    return jnp.zeros((1, BA_PAD), F32).at[0, offset:offset + vec.shape[0]].set(vec)


def kernel(x_prompt, x_sample, state_dn_conv, state_dn_ssm, cache_swa_k, cache_swa_v, state_ffn_conv, c_prompt, c_sample, rel_bias, final_norm_w, w_ada, b_ada, norm_mix_w, w_in, dn_conv_w, dn_A_log, dn_dt_bias, dn_norm_w, swa_sinks, w_out, norm_ffn_w, ffn_w_up, ffn_conv_w, ffn_conv_b, ffn_w_down):
    bp, lp, _ = x_prompt.shape
    bs, ls, _ = x_sample.shape
    layer = 0

    w_i = w_in[layer]
    n_ba = 2 * DN_HEADS
    ba_lo = DN_CONV_CH + DN_V
    w_cat = jnp.concatenate(
        [w_i[:, :ba_lo], jnp.pad(w_i[:, ba_lo:ba_lo + n_ba], ((0, 0), (0, BA_PAD - n_ba))), w_i[:, ba_lo + n_ba:]],
        axis=1).astype(BF16)
    w_out_b = w_out[layer].astype(BF16)
    w_up_b = ffn_w_up[layer].astype(BF16)
    w_dn_b = ffn_w_down[layer].astype(BF16)
    alog_row = _pad_row(dn_A_log[layer], DN_HEADS)
    dt_row = _pad_row(dn_dt_bias[layer], DN_HEADS)
    row = lambda v: v.reshape(1, -1)

    n_c = bp + bs
    n_c_pad = -(-n_c // 8) * 8
    c_all = jnp.pad(jnp.concatenate([c_prompt, c_sample], axis=0), ((0, n_c_pad - n_c), (0, 0)))
    mod = _mod_call(c_all, w_ada[layer], row(b_ada[layer]))
    mod_p = mod[:bp].reshape(bp, 6, D_MODEL)
    mod_s = mod[bp:n_c].reshape(bs, 6, D_MODEL)

    bias = _bias_call(rel_bias)
    sinks = swa_sinks[layer]

    def mixer_in(x, mod3, nb, lt):
        return _in_call(x, mod3, row(norm_mix_w[layer]), w_cat, nb, lt)

    def post(o_dn, o_swa, x, mod3, nb, lt, state=None):
        return _post_call(o_dn, o_swa, x, mod3, w_out_b, row(norm_ffn_w[layer]), w_up_b, ffn_conv_w[layer],
                          row(ffn_conv_b[layer]), w_dn_b, row(final_norm_w), nb, lt, state)

    dn_args = (dn_conv_w[layer], alog_row, dt_row, row(dn_norm_w[layer]))

    qkv, z, ba, sq, sk, sv = mixer_in(x_prompt, mod_p, 1, 512)
    o_dn, p_ssm = _dn_call(qkv, z, ba, *dn_args, bp, lp)
    o_swa = _swa_prompt_call(sinks, sq, sk, sv, bias, bp, lp)
    y_prompt, p_fbuf = post(o_dn, o_swa, x_prompt, mod_p, 1, 256)
    p_dn_conv = qkv.reshape(bp, lp, DN_CONV_CH)[:, lp - (DN_CONV - 1):]
    p_swa_k = sk.reshape(bp, lp, SWA_KV_HEADS, SWA_HD)[:, lp - WINDOW:]
    p_swa_v = sv.reshape(bp, lp, SWA_KV_HEADS, SWA_HD)[:, lp - WINDOW:]

    qkv_s, z_s, ba_s, sq_s, sk_s, sv_s = mixer_in(x_sample, mod_s, 64, ls)
    o_dn_s, s_ssm = _dn_call(qkv_s, z_s, ba_s, *dn_args, bs, ls, state=(state_dn_conv[layer], state_dn_ssm[layer]))
    o_swa_s, s_k, s_v = _swa_sample_call(sinks, sq_s, sk_s, sv_s,
                                         cache_swa_k[layer].reshape(bs, WINDOW, SWA_KV),
                                         cache_swa_v[layer].reshape(bs, WINDOW, SWA_KV), bias, bs, ls)
    y_sample, s_fbuf = post(o_dn_s, o_swa_s, x_sample, mod_s, 32, ls, state=state_ffn_conv[layer])
    s_dn_conv = qkv_s.reshape(bs, ls, DN_CONV_CH)[:, ls - (DN_CONV - 1):]

    return (y_prompt, y_sample, p_dn_conv[None], s_dn_conv[None], p_ssm[None], s_ssm[None],
            p_swa_k[None], s_k.reshape(bs, WINDOW, SWA_KV_HEADS, SWA_HD)[None],
            p_swa_v[None], s_v.reshape(bs, WINDOW, SWA_KV_HEADS, SWA_HD)[None],
            p_fbuf[None], s_fbuf[None])
```

```python
import functools
import math

import numpy as np
import jax
import jax.numpy as jnp
from jax import lax
from jax.experimental import pallas as pl
from jax.experimental.pallas import tpu as pltpu

F32 = jnp.float32
BF16 = jnp.bfloat16

D_MODEL = 1024
PAST_LEN = 16384
DN_HEADS = 4
DN_DK = 128
DN_DV = 128
DN_CONV = 4
SWA_HEADS = 8
SWA_KV_HEADS = 2
SWA_GROUP = SWA_HEADS // SWA_KV_HEADS
SWA_HD = 64
WINDOW = 128
N_BUCKETS = 32
MAX_DISTANCE = 128
D_FF = 2816
FFN_CONV = 3
EPS = 1e-6
NEG_INF = -1e30

DN_QK = DN_HEADS * DN_DK
DN_V = DN_HEADS * DN_DV
DN_CONV_CH = 2 * DN_QK + DN_V
SWA_Q = SWA_HEADS * SWA_HD
SWA_KV = SWA_KV_HEADS * SWA_HD
BA_PAD = 128
SUBLANES = 8
UNIT = 128
FFN_CHUNK = 256
VMEM_LIMIT = 56 * 1024 * 1024


def _cparams(sem):
    return pltpu.CompilerParams(dimension_semantics=sem, vmem_limit_bytes=VMEM_LIMIT)


def _bf(x):
    return x.astype(BF16)


def _dot(a, b):
    return jnp.dot(_bf(a), _bf(b), preferred_element_type=F32)


def _dot_nt(a, b):
    return lax.dot_general(_bf(a), _bf(b), (((1,), (1,)), ((), ())), preferred_element_type=F32)


def _split2(x):
    hi = _bf(x)
    lo = _bf(x - hi.astype(F32))
    return hi, lo


def _dot_x3(a, b):
    ah, al = _split2(a)
    bh, bl = _split2(b)
    d = functools.partial(jnp.dot, preferred_element_type=F32)
    return d(ah, bh) + (d(al, bh) + d(ah, bl))


def _dot_mask(m, x):
    hi = _bf(x)
    r = x - hi.astype(F32)
    mid = _bf(r)
    lo = _bf(r - mid.astype(F32))
    d = functools.partial(jnp.dot, preferred_element_type=F32)
    return d(m, hi) + (d(m, mid) + d(m, lo))


def _sigmoid(x):
    return 1.0 / (1.0 + jnp.exp(-x))


def _silu(x):
    return x * _sigmoid(x)


def _softplus(x):
    return jnp.maximum(x, 0.0) + jnp.log1p(jnp.exp(-jnp.abs(x)))


def _rms(x, w):
    ms = jnp.mean(x * x, axis=-1, keepdims=True)
    return x * lax.rsqrt(ms + EPS) * w


def _l2norm(t):
    return t * lax.rsqrt(jnp.sum(t * t, axis=-1, keepdims=True) + EPS)


def _rows(m3, nb, lt):
    return jnp.broadcast_to(m3, (nb, lt, m3.shape[-1])).reshape(nb * lt, m3.shape[-1])


def _causal_conv(x, prev, w, nb, lt):
    width = w.shape[0]
    rows, ch = x.shape
    if nb == 1 and lt > SUBLANES:
        tiles = jnp.concatenate([prev, x.reshape(lt // SUBLANES, SUBLANES, ch)], axis=0)
        sub = lax.broadcasted_iota(jnp.int32, (1, SUBLANES, 1), 1)
        out = tiles[1:] * w[width - 1:width, :]
        for j in range(1, width):
            rot = pltpu.roll(tiles, j, axis=1)
            out = out + jnp.where(sub >= j, rot[1:], rot[:-1]) * w[width - 1 - j:width - j, :]
        return out.reshape(rows, ch)
    tmod = lax.broadcasted_iota(jnp.int32, (rows, 1), 0) & (lt - 1)
    out = x * w[width - 1:width, :]
    for j in range(1, width):
        sh = pltpu.roll(x, j, axis=0)
        for t in range(j):
            p = width - 1 - j + t
            sh = jnp.where(tmod == t, _rows(prev[:, p:p + 1, :], nb, lt), sh)
        out = out + sh * w[width - 1 - j:width - j, :]
    return out


def _last_rows(x, nb, lt, n_state):
    ch = x.shape[-1]
    if nb == 1 and lt > SUBLANES:
        return x[lt - SUBLANES:].reshape(1, SUBLANES, ch)
    return x.reshape(nb, lt, ch)[:, lt - n_state:, :]


def _state_rows(nb, lt, n_state):
    return SUBLANES if (nb == 1 and lt > SUBLANES) else n_state


def _mod_kernel(c_ref, w_ref, b_ref, o_ref):
    o_ref[...] = _dot_x3(_silu(c_ref[...]), w_ref[...]) + b_ref[...]


def _mod_call(c_all, w_ada, b_ada):
    rows = c_all.shape[0]
    cols = w_ada.shape[1]
    tile = 512
    return pl.pallas_call(
        _mod_kernel,
        grid=(cols // tile,),
        in_specs=[pl.BlockSpec((rows, D_MODEL), lambda j: (0, 0)),
                  pl.BlockSpec((D_MODEL, tile), lambda j: (0, j)),
                  pl.BlockSpec((1, tile), lambda j: (0, j))],
        out_specs=pl.BlockSpec((rows, tile), lambda j: (0, j)),
        out_shape=jax.ShapeDtypeStruct((rows, cols), F32),
        compiler_params=_cparams(("arbitrary",)),
        name="adaln_mod",
    )(c_all, w_ada, b_ada)


IN_SPLIT = (DN_CONV_CH, DN_V, BA_PAD, SWA_Q, SWA_KV, SWA_KV)
IN_COLS_PAD = sum(IN_SPLIT)


IN_SLAB = 2 * DN_DK


def _in_kernel(*refs, nb, lt, carry):
    if carry:
        (x_ref, mod_ref, nw_ref, w_ref, cw_ref,
         qkv_ref, z_ref, ba_ref, sq_ref, sk_ref, sv_ref, tail_ref, h_scr) = refs
        prev_ref = tail_ref

        @pl.when(pl.program_id(1) == 0)
        def _():
            tail_ref[...] = jnp.zeros_like(tail_ref)
    else:
        (x_ref, mod_ref, nw_ref, w_ref, cw_ref, prev_ref,
         qkv_ref, z_ref, ba_ref, sq_ref, sk_ref, sv_ref, tail_ref, h_scr) = refs
    rows = nb * lt
    x = x_ref[...]
    ms = jnp.mean(x * x, axis=-1, keepdims=True)
    y = x * lax.rsqrt(ms + EPS) * nw_ref[...]
    h = y * (1.0 + mod_ref[:, 1:2, :]) + mod_ref[:, 0:1, :]
    h_scr[...] = _bf(h.reshape(rows, D_MODEL))

    def proj(lo, n):
        return jnp.dot(h_scr[...], w_ref[:, lo:lo + n], preferred_element_type=F32)

    z_lo = DN_CONV_CH
    ba_lo = z_lo + DN_V
    sq_lo = ba_lo + BA_PAD
    kv_lo = sq_lo + SWA_Q

    def plain_proj(i):
        if i < 2:
            z_ref[:, i * IN_SLAB:(i + 1) * IN_SLAB] = proj(z_lo + i * IN_SLAB, IN_SLAB)
        elif i < 4:
            sq_ref[:, (i - 2) * IN_SLAB:(i - 1) * IN_SLAB] = proj(sq_lo + (i - 2) * IN_SLAB, IN_SLAB)
        elif i == 4:
            kv = proj(kv_lo, 2 * SWA_KV)
            sk_ref[...] = kv[:, :SWA_KV]
            sv_ref[...] = kv[:, SWA_KV:]
        else:
            ba_ref[...] = proj(ba_lo, BA_PAD)

    n_slabs = DN_CONV_CH // IN_SLAB
    raw_next = proj(0, IN_SLAB)
    for slab in range(n_slabs):
        raw = raw_next
        plain_proj(slab)
        if slab + 1 < n_slabs:
            raw_next = proj((slab + 1) * IN_SLAB, IN_SLAB)
        cols = slice(slab * IN_SLAB, (slab + 1) * IN_SLAB)
        c = _silu(_causal_conv(raw, prev_ref[:, :, cols], cw_ref[:, cols], nb, lt))
        tail_ref[:, :, cols] = _last_rows(raw, nb, lt, DN_CONV - 1)
        if slab * IN_SLAB < 2 * DN_QK:
            scale = DN_DK ** -0.5 if slab * IN_SLAB < DN_QK else 1.0
            c = jnp.concatenate([_l2norm(c[:, i * DN_DK:(i + 1) * DN_DK]) * scale
                                 for i in range(IN_SLAB // DN_DK)], axis=1)
        qkv_ref[:, cols] = c


def _in_call(x, mod3, norm_w, w_cat, conv_w, nb, lt, state=None):
    carry = state is None
    bsz, seq, _ = x.shape
    nt = seq // lt
    rows = nb * lt
    n_tok = bsz * seq
    n_state = _state_rows(nb, lt, DN_CONV - 1)
    row_map = lambda i, j: (i * nt + j, 0)
    const = lambda i, j: (0, 0)
    per_seq = lambda i, j: (i, 0, 0)
    in_specs = [pl.BlockSpec((nb, lt, D_MODEL), lambda i, j: (i, j, 0)),
                pl.BlockSpec((nb, 6, D_MODEL), per_seq),
                pl.BlockSpec((1, D_MODEL), const),
                pl.BlockSpec((D_MODEL, IN_COLS_PAD), const),
                pl.BlockSpec((DN_CONV, DN_CONV_CH), const)]
    args = [x, mod3, norm_w, w_cat, conv_w]
    if not carry:
        in_specs.append(pl.BlockSpec((nb, DN_CONV - 1, DN_CONV_CH), per_seq))
        args.append(state)
    return pl.pallas_call(
        functools.partial(_in_kernel, nb=nb, lt=lt, carry=carry),
        grid=(bsz // nb, nt),
        in_specs=in_specs,
        out_specs=[pl.BlockSpec((rows, n), row_map) for n in IN_SPLIT]
        + [pl.BlockSpec((nb, n_state, DN_CONV_CH), per_seq)],
        out_shape=[jax.ShapeDtypeStruct((n_tok, n), F32) for n in IN_SPLIT]
        + [jax.ShapeDtypeStruct((bsz, n_state, DN_CONV_CH), F32)],
        scratch_shapes=[pltpu.VMEM((rows, D_MODEL), BF16)],
        compiler_params=_cparams(("arbitrary", "arbitrary")),
        name="norm_in_proj_carry" if carry else "norm_in_proj_state",
    )(*args)


INV_BASE_SHIFT = 2
DN_UNITS_CARRY = 2
SWA_BLOCKS_PER_STEP = 4
DN_UNITS_STATE = 1


def _unit_lower_inverses(a_mats, ri, ci, chunk_shift):
    def blocks(s):
        return (ri >> s) == (ci >> s)

    base = min(INV_BASE_SHIFT, chunk_shift)
    xs = [jnp.where(blocks(base), -a, 0.0) for a in a_mats]
    ts = [jnp.where(ri == ci, 1.0, n) for n in xs]
    for lvl in range(1, base):
        xs = [_dot(x, x) for x in xs]
        ts = [t + _dot(t, x) for t, x in zip(ts, xs)]
    for s in range(base, chunk_shift):
        sel = blocks(s + 1) & jnp.logical_not(blocks(s))
        ets = [_dot(jnp.where(sel, a, 0.0), t) for a, t in zip(a_mats, ts)]
        ts = [t - _dot(t, et) for t, et in zip(ts, ets)]
    return ts


def _dn_kernel(*refs, nb, chunk, carry, units):
    if carry:
        (qkv_ref, z_ref, ba_ref, alog_ref, dt_ref, nw_ref, o_ref, s_ref) = refs
        s0_ref = s_ref

        @pl.when(pl.program_id(1) == 0)
        def _():
            s_ref[...] = jnp.zeros_like(s_ref)
    else:
        (qkv_ref, z_ref, ba_ref, alog_ref, dt_ref, nw_ref, s0_ref, o_ref, s_ref) = refs

    ri = lax.broadcasted_iota(jnp.int32, (UNIT, UNIT), 0)
    ci = lax.broadcasted_iota(jnp.int32, (UNIT, UNIT), 1)
    shift = int(math.log2(chunk))
    same = (ri >> shift) == (ci >> shift)
    incl = same & (ri >= ci)
    strict = same & (ri > ci)

    ba = ba_ref[...]
    beta_full = _sigmoid(ba)
    g_full = -jnp.exp(alog_ref[...]) * _softplus(ba + dt_ref[...])
    masks = jnp.concatenate([jnp.where(incl, 1.0, 0.0), jnp.where(same, 1.0, 0.0)], axis=0).astype(BF16)
    gsums = [_dot_mask(masks, g_full[u * UNIT:(u + 1) * UNIT]) for u in range(units)]
    g_cum = [g[:UNIT] for g in gsums]
    g_tot = [g[UNIT:] for g in gsums]
    g_cum_t = [g.T for g in g_cum]

    probs = [(u, h) for u in range(units) for h in range(DN_HEADS)]
    n_p = len(probs)
    rs = lambda u: slice(u * UNIT, (u + 1) * UNIT)

    def head_cols(base):
        return [qkv_ref[rs(u), base + h * DN_DK:base + (h + 1) * DN_DK] for u, h in probs]

    q = head_cols(0)
    k = head_cols(DN_QK)
    v = head_cols(2 * DN_QK)
    gc = [g_cum[u][:, DN_HEADS + h:DN_HEADS + h + 1] for u, h in probs]
    gr = [g_cum_t[u][DN_HEADS + h:DN_HEADS + h + 1, :] for u, h in probs]
    gt = [g_tot[u][:, DN_HEADS + h:DN_HEADS + h + 1] for u, h in probs]
    bc = [beta_full[rs(u), h:h + 1] for u, h in probs]
    decay = [jnp.where(incl, jnp.exp(jnp.where(incl, gc[p] - gr[p], 0.0)), 0.0) for p in range(n_p)]
    e_g = [jnp.exp(gc[p]) for p in range(n_p)]
    kq = [_dot_nt(jnp.concatenate([k[p], q[p]], axis=0), k[p]) for p in range(n_p)]
    qk = [kq[p][UNIT:] * decay[p] for p in range(n_p)]
    a_mats = [jnp.where(strict, bc[p] * kq[p][:UNIT] * decay[p], 0.0) for p in range(n_p)]
    t_inv = _unit_lower_inverses(a_mats, ri, ci, shift)
    wvk = [_dot(t_inv[p], jnp.concatenate([v[p] * bc[p], k[p] * (bc[p] * e_g[p])], axis=1)) for p in range(n_p)]
    w_v = [w[:, :DN_DV] for w in wvk]
    w_k = [w[:, DN_DV:] for w in wvk]
    q_dec = [q[p] * e_g[p] for p in range(n_p)]
    k_tail_t = [(k[p] * jnp.exp(gt[p] - gc[p])).T for p in range(n_p)]
    c_dec = [jnp.exp(gt[p]) for p in range(n_p)]

    outs = {}
    if carry:
        state = [s_ref[0, h] for h in range(DN_HEADS)]
        for u in range(units):
            ps = [u * DN_HEADS + h for h in range(DN_HEADS)]
            r = [_dot(jnp.concatenate([w_k[p], q_dec[p]], axis=0), state[h]) for h, p in enumerate(ps)]
            uu = [w_v[p] - r[h][:UNIT] for h, p in enumerate(ps)]
            for h, p in enumerate(ps):
                outs[p] = r[h][UNIT:] + _dot(qk[p], uu[h])
            state = [state[h] * c_dec[p][0:1, :] + _dot(k_tail_t[p], uu[h]) for h, p in enumerate(ps)]
        for h in range(DN_HEADS):
            s_ref[0, h] = state[h]
    else:
        per_unit = nb // units
        for p, (u, h) in enumerate(probs):
            us, qs = [], []
            for s in range(per_unit):
                lo = s * chunk
                lhs = jnp.concatenate([w_k[p][lo:lo + chunk], q_dec[p][lo:lo + chunk]], axis=0)
                r = jnp.dot(lhs, s0_ref[u * per_unit + s, h], preferred_element_type=F32)
                us.append(w_v[p][lo:lo + chunk] - r[:chunk])
                qs.append(r[chunk:])
            uu = jnp.concatenate(us, axis=0)
            outs[p] = jnp.concatenate(qs, axis=0) + _dot(qk[p], uu)
            for s in range(per_unit):
                lo = s * chunk
                upd = jnp.dot(k_tail_t[p][:, lo:lo + chunk], uu[lo:lo + chunk], preferred_element_type=F32)
                s_ref[u * per_unit + s, h] = s0_ref[u * per_unit + s, h] * c_dec[p][lo:lo + 1, :] + upd

    for p, (u, h) in enumerate(probs):
        zz = z_ref[rs(u), h * DN_DV:(h + 1) * DN_DV]
        o_ref[rs(u), h * DN_DV:(h + 1) * DN_DV] = _rms(outs[p], nw_ref[...]) * _silu(zz)


def _dn_call(qkv, z, ba, alog_row, dt_row, norm_w, bsz, seq, state=None):
    carry = state is None
    units = DN_UNITS_CARRY if carry else DN_UNITS_STATE
    rows = units * UNIT
    nb = 1 if carry else rows // seq
    chunk = UNIT if carry else seq
    nt = seq * nb // rows
    n_tok = bsz * seq
    row_map = lambda i, j: (i * nt + j, 0)
    const = lambda i, j: (0, 0)
    in_specs = [pl.BlockSpec((rows, DN_CONV_CH), row_map),
                pl.BlockSpec((rows, DN_V), row_map),
                pl.BlockSpec((rows, BA_PAD), row_map),
                pl.BlockSpec((1, BA_PAD), const),
                pl.BlockSpec((1, BA_PAD), const),
                pl.BlockSpec((1, DN_DV), const)]
    args = [qkv, z, ba, alog_row, dt_row, norm_w]
    if not carry:
        in_specs.append(pl.BlockSpec((nb, DN_HEADS, DN_DK, DN_DV), lambda i, j: (i, 0, 0, 0)))
        args.append(state)
    return pl.pallas_call(
        functools.partial(_dn_kernel, nb=nb, chunk=chunk, carry=carry, units=units),
        grid=(bsz // nb, nt),
        in_specs=in_specs,
        out_specs=[pl.BlockSpec((rows, DN_V), row_map),
                   pl.BlockSpec((nb, DN_HEADS, DN_DK, DN_DV), lambda i, j: (i, 0, 0, 0))],
        out_shape=[jax.ShapeDtypeStruct((n_tok, DN_V), F32),
                   jax.ShapeDtypeStruct((bsz, DN_HEADS, DN_DK, DN_DV), F32)],
        compiler_params=_cparams(("arbitrary", "arbitrary")),
        name="gated_deltanet_carry" if carry else "gated_deltanet_state",
    )(*args)


def _bucket_table():
    i = np.arange(WINDOW, dtype=np.int64)[:, None]
    j = np.arange(2 * WINDOW, dtype=np.int64)[None, :]
    d = np.maximum(i + WINDOW - j, 0)
    exact = N_BUCKETS // 2
    logv = (np.log(np.maximum(d, 1).astype(np.float32) / np.float32(exact)).astype(np.float32)
            / np.float32(math.log(MAX_DISTANCE / exact)))
    large = np.minimum(exact + (logv * np.float32(N_BUCKETS - exact)).astype(np.int32), N_BUCKETS - 1)
    return np.where(d < exact, d, large).astype(np.int32)


def _bias_kernel(rb_ref, bucket_ref, o_ref):
    bucket = bucket_ref[...]
    ri = lax.broadcasted_iota(jnp.int32, bucket.shape, 0)
    ci = lax.broadcasted_iota(jnp.int32, bucket.shape, 1)
    dist = ri + WINDOW - ci
    valid = (dist >= 0) & (dist < WINDOW)
    for hk in range(SWA_KV_HEADS):
        for par in range(2):
            for st in range(SWA_GROUP // 2):
                head = hk * SWA_GROUP + 2 * st + par
                acc = jnp.zeros(bucket.shape, F32)
                for b in range(N_BUCKETS):
                    acc = jnp.where(bucket == b, rb_ref[b, head], acc)
                gen = jnp.where(valid, acc, NEG_INF)
                o_ref[1, hk, par, st * WINDOW:(st + 1) * WINDOW, :] = gen
                o_ref[0, hk, par, st * WINDOW:(st + 1) * WINDOW, :] = jnp.where(ci >= WINDOW, gen, NEG_INF)


BIAS_SHAPE = (2, SWA_KV_HEADS, 2, (SWA_GROUP // 2) * WINDOW, 2 * WINDOW)


def _bias_call(rel_bias):
    bucket = jnp.asarray(_bucket_table())
    return pl.pallas_call(
        _bias_kernel,
        in_specs=[pl.BlockSpec(memory_space=pltpu.SMEM),
                  pl.BlockSpec((WINDOW, 2 * WINDOW), lambda: (0, 0))],
        out_specs=pl.BlockSpec(BIAS_SHAPE, lambda: (0,) * len(BIAS_SHAPE)),
        out_shape=jax.ShapeDtypeStruct(BIAS_SHAPE, F32),
        name="swa_rel_bias_table",
    )(rel_bias, bucket)


def _softmax_sink(s, sink):
    m = jnp.maximum(jnp.max(s, axis=-1, keepdims=True), sink)
    p = jnp.exp(s - m)
    return p / (jnp.sum(p, axis=-1, keepdims=True) + jnp.exp(sink - m))


def _softmax_sink_parts(s, sink):
    m = jnp.maximum(jnp.max(s, axis=-1, keepdims=True), sink)
    p = jnp.exp(s - m)
    return p, 1.0 / (jnp.sum(p, axis=-1, keepdims=True) + jnp.exp(sink - m))


def _half_lane_variants(full, hk, lo_half):
    rolled = pltpu.roll(full, SWA_HD, axis=1)
    low_src, high_src = (full, rolled) if hk == 0 else (rolled, full)
    return jnp.where(lo_half, low_src, 0.0), jnp.where(lo_half, 0.0, high_src)


def _swa_prompt_kernel(sink_ref, q_ref, kp_ref, kc_ref, vp_ref, vc_ref, bias_ref, o_ref, *, n_blk):
    step = pl.program_id(1)
    lo_half = lax.broadcasted_iota(jnp.int32, (1, 2 * SWA_HD), 1) < SWA_HD
    q = q_ref[...] * (SWA_HD ** -0.5)
    keys = jnp.concatenate([kp_ref[...], kc_ref[...]], axis=0)
    vals = jnp.concatenate([vp_ref[...], vc_ref[...]], axis=0)
    k_var = [_half_lane_variants(keys, hk, lo_half) for hk in range(SWA_KV_HEADS)]
    v_var = [_half_lane_variants(vals, hk, lo_half) for hk in range(SWA_KV_HEADS)]
    n_stack = SWA_GROUP // 2
    sinks = [[jnp.concatenate([jnp.full((WINDOW, 1), sink_ref[hk * SWA_GROUP + 2 * st + par], F32)
                               for st in range(n_stack)], axis=0) for par in range(2)]
             for hk in range(SWA_KV_HEADS)]

    def scores(b):
        rows = slice(b * WINDOW, (b + 1) * WINDOW)
        win = slice(b * WINDOW, (b + 2) * WINDOW)
        variant = jnp.where(step == 0, 0, 1) if b == 0 else 1
        out = []
        for hk in range(SWA_KV_HEADS):
            q2 = jnp.concatenate([q[rows, (hk * n_stack + st) * 2 * SWA_HD:(hk * n_stack + st + 1) * 2 * SWA_HD]
                                  for st in range(n_stack)], axis=0)
            out.append([_dot_nt(q2, k_var[hk][par][win]) + bias_ref[variant, hk, par] for par in range(2)])
        return out

    s_next = scores(0)
    for b in range(n_blk):
        s_cur = s_next
        if b + 1 < n_blk:
            s_next = scores(b + 1)
        rows = slice(b * WINDOW, (b + 1) * WINDOW)
        win = slice(b * WINDOW, (b + 2) * WINDOW)
        for hk in range(SWA_KV_HEADS):
            parts = [_softmax_sink_parts(s_cur[hk][par], sinks[hk][par]) for par in range(2)]
            o2 = _dot(parts[0][0], v_var[hk][0][win]) + _dot(parts[1][0], v_var[hk][1][win])
            o2 = o2 * jnp.where(lo_half, parts[0][1], parts[1][1])
            for st in range(n_stack):
                lo = (hk * n_stack + st) * 2 * SWA_HD
                o_ref[rows, lo:lo + 2 * SWA_HD] = o2[st * WINDOW:(st + 1) * WINDOW]


def _swa_prompt_call(sinks, sq, sk, sv, bias, bsz, seq):
    n_blk = SWA_BLOCKS_PER_STEP
    tile = n_blk * WINDOW
    nt = seq // tile
    cur = lambda b, i: (b * nt + i, 0)
    prv = lambda b, i: (b * nt * n_blk + jnp.maximum(i * n_blk - 1, 0), 0)
    return pl.pallas_call(
        functools.partial(_swa_prompt_kernel, n_blk=n_blk),
        grid=(bsz, nt),
        in_specs=[pl.BlockSpec(memory_space=pltpu.SMEM),
                  pl.BlockSpec((tile, SWA_Q), cur),
                  pl.BlockSpec((WINDOW, SWA_KV), prv),
                  pl.BlockSpec((tile, SWA_KV), cur),
                  pl.BlockSpec((WINDOW, SWA_KV), prv),
                  pl.BlockSpec((tile, SWA_KV), cur),
                  pl.BlockSpec(BIAS_SHAPE, lambda b, i: (0,) * len(BIAS_SHAPE))],
        out_specs=pl.BlockSpec((tile, SWA_Q), cur),
        out_shape=jax.ShapeDtypeStruct((bsz * seq, SWA_Q), F32),
        compiler_params=_cparams(("arbitrary", "arbitrary")),
        name="swa_banded",
    )(sinks, sq, sk, sk, sv, sv, bias)


def _swa_sample_kernel(sink_ref, q_ref, kn_ref, vn_ref, kc_ref, vc_ref, bias_ref, o_ref, ko_ref, vo_ref, *, nb, lt):
    q_all = q_ref[...] * (SWA_HD ** -0.5)
    kn_all = kn_ref[...]
    vn_all = vn_ref[...]
    pad = jnp.zeros((WINDOW - lt, SWA_HD), F32)
    rows_out = []
    for s in range(nb):
        lo = s * lt
        kc = kc_ref[s]
        vc = vc_ref[s]
        kn = kn_all[lo:lo + lt]
        vn = vn_all[lo:lo + lt]
        ko_ref[s] = jnp.concatenate([kc[lt:], kn], axis=0)
        vo_ref[s] = jnp.concatenate([vc[lt:], vn], axis=0)
        q = q_all[lo:lo + lt]
        pieces = []
        for hk in range(SWA_KV_HEADS):
            cols = slice(hk * SWA_HD, (hk + 1) * SWA_HD)
            k_h = jnp.concatenate([kc[:, cols], kn[:, cols], pad], axis=0)
            v_h = jnp.concatenate([vc[:, cols], vn[:, cols], pad], axis=0)
            heads = range(hk * SWA_GROUP, (hk + 1) * SWA_GROUP)
            q4 = jnp.concatenate([q[:, h * SWA_HD:(h + 1) * SWA_HD] for h in heads], axis=0)
            b4 = jnp.concatenate([bias_ref[1, hk, g % 2, (g // 2) * WINDOW:(g // 2) * WINDOW + lt, :]
                                  for g in range(SWA_GROUP)], axis=0)
            sink4 = jnp.concatenate([jnp.full((lt, 1), sink_ref[h], F32) for h in heads], axis=0)
            sc = lax.dot_general(q4, k_h, (((1,), (1,)), ((), ())), preferred_element_type=F32)
            sc = sc + b4
            o4 = jnp.dot(_softmax_sink(sc, sink4), v_h, preferred_element_type=F32)
            pieces += [o4[g * lt:(g + 1) * lt] for g in range(SWA_GROUP)]
        rows_out.append(jnp.concatenate(pieces, axis=1))
    o_ref[...] = jnp.concatenate(rows_out, axis=0)


def _swa_sample_call(sinks, sq, sk, sv, cache_k, cache_v, bias, bsz, seq):
    nb = UNIT // seq
    rows = lambda i: (i, 0)
    seqs = lambda i: (i, 0, 0)
    return pl.pallas_call(
        functools.partial(_swa_sample_kernel, nb=nb, lt=seq),
        grid=(bsz // nb,),
        in_specs=[pl.BlockSpec(memory_space=pltpu.SMEM),
                  pl.BlockSpec((UNIT, SWA_Q), rows),
                  pl.BlockSpec((UNIT, SWA_KV), rows),
                  pl.BlockSpec((UNIT, SWA_KV), rows),
                  pl.BlockSpec((nb, WINDOW, SWA_KV), seqs),
                  pl.BlockSpec((nb, WINDOW, SWA_KV), seqs),
                  pl.BlockSpec(BIAS_SHAPE, lambda i: (0,) * len(BIAS_SHAPE))],
        out_specs=[pl.BlockSpec((UNIT, SWA_Q), rows),
                   pl.BlockSpec((nb, WINDOW, SWA_KV), seqs),
                   pl.BlockSpec((nb, WINDOW, SWA_KV), seqs)],
        out_shape=[jax.ShapeDtypeStruct((bsz * seq, SWA_Q), F32),
                   jax.ShapeDtypeStruct((bsz, WINDOW, SWA_KV), F32),
                   jax.ShapeDtypeStruct((bsz, WINDOW, SWA_KV), F32)],
        compiler_params=_cparams(("arbitrary",)),
        name="swa_cached",
    )(sinks, sq, sk, sv, cache_k, cache_v, bias)


def _post_kernel(*refs, nb, lt, carry):
    if carry:
        (odn_ref, oswa_ref, x_ref, mod_ref, wout_ref, nfw_ref, wup_ref, cw_ref, cb_ref, wdn_ref, fnw_ref,
         y_ref, fbuf_ref, h_scr, x1_scr, acc_scr) = refs
        prev_ref = fbuf_ref

        @pl.when(pl.program_id(1) == 0)
        def _():
            fbuf_ref[...] = jnp.zeros_like(fbuf_ref)
    else:
        (odn_ref, oswa_ref, x_ref, mod_ref, wout_ref, nfw_ref, wup_ref, cw_ref, cb_ref, wdn_ref, fnw_ref,
         prev_ref, y_ref, fbuf_ref, h_scr, x1_scr, acc_scr) = refs
    rows = nb * lt
    attn = (jnp.dot(_bf(odn_ref[...]), wout_ref[0:DN_V, :], preferred_element_type=F32)
            + jnp.dot(_bf(oswa_ref[...]), wout_ref[DN_V:, :], preferred_element_type=F32))
    x1 = x_ref[...].reshape(rows, D_MODEL) + _rows(mod_ref[:, 2:3, :], nb, lt) * attn
    x1_scr[...] = x1
    h = _rms(x1, nfw_ref[...]) * (1.0 + _rows(mod_ref[:, 4:5, :], nb, lt)) + _rows(mod_ref[:, 3:4, :], nb, lt)
    h_scr[...] = _bf(h)
    acc_scr[...] = jnp.zeros_like(acc_scr)
    n_chunks = D_FF // FFN_CHUNK

    def col_slices(c):
        return [slice(base + c * FFN_CHUNK, base + (c + 1) * FFN_CHUNK) for base in (0, D_FF)]

    def up_proj(c):
        return [jnp.dot(h_scr[...], wup_ref[:, cols], preferred_element_type=F32) for cols in col_slices(c)]

    u_next = up_proj(0)
    for c in range(n_chunks):
        u_cur = u_next
        if c + 1 < n_chunks:
            u_next = up_proj(c + 1)
        halves = []
        for u, cols in zip(u_cur, col_slices(c)):
            prev = prev_ref[:, :, cols]
            halves.append(_causal_conv(u, prev, cw_ref[:, cols], nb, lt) + cb_ref[:, cols])
            fbuf_ref[:, :, cols] = _last_rows(u, nb, lt, FFN_CONV - 1)
        act = _silu(halves[0]) * halves[1]
        acc_scr[...] += jnp.dot(_bf(act), wdn_ref[c * FFN_CHUNK:(c + 1) * FFN_CHUNK, :],
                                preferred_element_type=F32)
    x2 = x1_scr[...] + _rows(mod_ref[:, 5:6, :], nb, lt) * acc_scr[...]
    y_ref[...] = _rms(x2, fnw_ref[...]).reshape(nb, lt, D_MODEL)


def _post_call(o_dn, o_swa, x, mod3, w_out, norm_ffn_w, w_up, conv_w, conv_b, w_down, final_w, nb, lt, state=None):
    carry = state is None
    bsz, seq, _ = x.shape
    nt = seq // lt
    rows = nb * lt
    n_state = _state_rows(nb, lt, FFN_CONV - 1)
    row_map = lambda i, j: (i * nt + j, 0)
    const = lambda i, j: (0, 0)
    in_specs = [pl.BlockSpec((rows, DN_V), row_map),
                pl.BlockSpec((rows, SWA_Q), row_map),
                pl.BlockSpec((nb, lt, D_MODEL), lambda i, j: (i, j, 0)),
                pl.BlockSpec((nb, 6, D_MODEL), lambda i, j: (i, 0, 0)),
                pl.BlockSpec((D_MODEL, D_MODEL), const),
                pl.BlockSpec((1, D_MODEL), const),
                pl.BlockSpec((D_MODEL, 2 * D_FF), const),
                pl.BlockSpec((FFN_CONV, 2 * D_FF), const),
                pl.BlockSpec((1, 2 * D_FF), const),
                pl.BlockSpec((D_FF, D_MODEL), const),
                pl.BlockSpec((1, D_MODEL), const)]
    args = [o_dn, o_swa, x, mod3, w_out, norm_ffn_w, w_up, conv_w, conv_b, w_down, final_w]
    if not carry:
        in_specs.append(pl.BlockSpec((nb, FFN_CONV - 1, 2 * D_FF), lambda i, j: (i, 0, 0)))
        args.append(state)
    return pl.pallas_call(
        functools.partial(_post_kernel, nb=nb, lt=lt, carry=carry),
        grid=(bsz // nb, nt),
        in_specs=in_specs,
        out_specs=[pl.BlockSpec((nb, lt, D_MODEL), lambda i, j: (i, j, 0)),
                   pl.BlockSpec((nb, n_state, 2 * D_FF), lambda i, j: (i, 0, 0))],
        out_shape=[jax.ShapeDtypeStruct((bsz, seq, D_MODEL), F32),
                   jax.ShapeDtypeStruct((bsz, n_state, 2 * D_FF), F32)],
        scratch_shapes=[pltpu.VMEM((rows, D_MODEL), BF16),
                        pltpu.VMEM((rows, D_MODEL), F32),
                        pltpu.VMEM((rows, D_MODEL), F32)],
        compiler_params=_cparams(("arbitrary", "arbitrary")),
        name="out_proj_convffn_carry" if carry else "out_proj_convffn_state",
    )(*args)


def _pad_row(vec, offset):
    return jnp.zeros((1, BA_PAD), F32).at[0, offset:offset + vec.shape[0]].set(vec)


def kernel(x_prompt, x_sample, state_dn_conv, state_dn_ssm, cache_swa_k, cache_swa_v, state_ffn_conv, c_prompt, c_sample, rel_bias, final_norm_w, w_ada, b_ada, norm_mix_w, w_in, dn_conv_w, dn_A_log, dn_dt_bias, dn_norm_w, swa_sinks, w_out, norm_ffn_w, ffn_w_up, ffn_conv_w, ffn_conv_b, ffn_w_down):
    bp, lp, _ = x_prompt.shape
    bs, ls, _ = x_sample.shape
    layer = 0

    w_i = w_in[layer]
    n_ba = 2 * DN_HEADS
    ba_lo = DN_CONV_CH + DN_V
    w_cat = jnp.concatenate(
        [w_i[:, :ba_lo], jnp.pad(w_i[:, ba_lo:ba_lo + n_ba], ((0, 0), (0, BA_PAD - n_ba))), w_i[:, ba_lo + n_ba:]],
        axis=1).astype(BF16)
    w_out_b = w_out[layer].astype(BF16)
    w_up_b = ffn_w_up[layer].astype(BF16)
    w_dn_b = ffn_w_down[layer].astype(BF16)
    alog_row = _pad_row(dn_A_log[layer], DN_HEADS)
    dt_row = _pad_row(dn_dt_bias[layer], DN_HEADS)
    row = lambda v: v.reshape(1, -1)

    n_c = bp + bs
    n_c_pad = -(-n_c // 8) * 8
    c_all = jnp.pad(jnp.concatenate([c_prompt, c_sample], axis=0), ((0, n_c_pad - n_c), (0, 0)))
    mod = _mod_call(c_all, w_ada[layer], row(b_ada[layer]))
    mod_p = mod[:bp].reshape(bp, 6, D_MODEL)
    mod_s = mod[bp:n_c].reshape(bs, 6, D_MODEL)

    bias = _bias_call(rel_bias)
    sinks = swa_sinks[layer]

    def mixer_in(x, mod3, nb, lt, state=None):
        return _in_call(x, mod3, row(norm_mix_w[layer]), w_cat, dn_conv_w[layer], nb, lt, state)

    def post(o_dn, o_swa, x, mod3, nb, lt, state=None):
        return _post_call(o_dn, o_swa, x, mod3, w_out_b, row(norm_ffn_w[layer]), w_up_b, ffn_conv_w[layer],
                          row(ffn_conv_b[layer]), w_dn_b, row(final_norm_w), nb, lt, state)

    dn_args = (alog_row, dt_row, row(dn_norm_w[layer]))

    qkv, z, ba, sq, sk, sv, p_tail = mixer_in(x_prompt, mod_p, 1, 512)
    o_dn, p_ssm = _dn_call(qkv, z, ba, *dn_args, bp, lp)
    o_swa = _swa_prompt_call(sinks, sq, sk, sv, bias, bp, lp)
    y_prompt, p_ffn_tail = post(o_dn, o_swa, x_prompt, mod_p, 1, 256)
    p_dn_conv = p_tail[:, SUBLANES - (DN_CONV - 1):]
    p_fbuf = p_ffn_tail[:, SUBLANES - (FFN_CONV - 1):]
    p_swa_k = sk.reshape(bp, lp, SWA_KV_HEADS, SWA_HD)[:, lp - WINDOW:]
    p_swa_v = sv.reshape(bp, lp, SWA_KV_HEADS, SWA_HD)[:, lp - WINDOW:]

    qkv_s, z_s, ba_s, sq_s, sk_s, sv_s, s_dn_conv = mixer_in(x_sample, mod_s, 64, ls, state=state_dn_conv[layer])
    o_dn_s, s_ssm = _dn_call(qkv_s, z_s, ba_s, *dn_args, bs, ls, state=state_dn_ssm[layer])
    o_swa_s, s_k, s_v = _swa_sample_call(sinks, sq_s, sk_s, sv_s,
                                         cache_swa_k[layer].reshape(bs, WINDOW, SWA_KV),
                                         cache_swa_v[layer].reshape(bs, WINDOW, SWA_KV), bias, bs, ls)
    y_sample, s_fbuf = post(o_dn_s, o_swa_s, x_sample, mod_s, 32, ls, state=state_ffn_conv[layer])

    return (y_prompt, y_sample, p_dn_conv[None], s_dn_conv[None], p_ssm[None], s_ssm[None],
            p_swa_k[None], s_k.reshape(bs, WINDOW, SWA_KV_HEADS, SWA_HD)[None],
            p_swa_v[None], s_v.reshape(bs, WINDOW, SWA_KV_HEADS, SWA_HD)[None],
            p_fbuf[None], s_fbuf[None])
```

```python
import functools
import math

import numpy as np
import jax
import jax.numpy as jnp
from jax import lax
from jax.experimental import pallas as pl
from jax.experimental.pallas import tpu as pltpu

F32 = jnp.float32
BF16 = jnp.bfloat16

D_MODEL = 1024
PAST_LEN = 16384
DN_HEADS = 4
DN_DK = 128
DN_DV = 128
DN_CONV = 4
SWA_HEADS = 8
SWA_KV_HEADS = 2
SWA_GROUP = SWA_HEADS // SWA_KV_HEADS
SWA_HD = 64
WINDOW = 128
N_BUCKETS = 32
MAX_DISTANCE = 128
D_FF = 2816
FFN_CONV = 3
EPS = 1e-6
NEG_INF = -1e30

DN_QK = DN_HEADS * DN_DK
DN_V = DN_HEADS * DN_DV
DN_CONV_CH = 2 * DN_QK + DN_V
SWA_Q = SWA_HEADS * SWA_HD
SWA_KV = SWA_KV_HEADS * SWA_HD
BA_PAD = 128
SUBLANES = 8
UNIT = 128
FFN_CHUNK = 256
VMEM_LIMIT = 56 * 1024 * 1024


def _cparams(sem):
    return pltpu.CompilerParams(dimension_semantics=sem, vmem_limit_bytes=VMEM_LIMIT)


def _bf(x):
    return x.astype(BF16)


def _dot(a, b):
    return jnp.dot(_bf(a), _bf(b), preferred_element_type=F32)


def _dot_nt(a, b):
    return lax.dot_general(_bf(a), _bf(b), (((1,), (1,)), ((), ())), preferred_element_type=F32)


def _split2(x):
    hi = _bf(x)
    lo = _bf(x - hi.astype(F32))
    return hi, lo


def _dot_x3(a, b):
    ah, al = _split2(a)
    bh, bl = _split2(b)
    d = functools.partial(jnp.dot, preferred_element_type=F32)
    return d(ah, bh) + (d(al, bh) + d(ah, bl))


def _dot_mask(m, x):
    hi = _bf(x)
    r = x - hi.astype(F32)
    mid = _bf(r)
    lo = _bf(r - mid.astype(F32))
    d = functools.partial(jnp.dot, preferred_element_type=F32)
    return d(m, hi) + (d(m, mid) + d(m, lo))


def _sigmoid(x):
    return 1.0 / (1.0 + jnp.exp(-x))


def _silu(x):
    return x * _sigmoid(x)


def _softplus(x):
    return jnp.maximum(x, 0.0) + jnp.log1p(jnp.exp(-jnp.abs(x)))


def _rms(x, w):
    ms = jnp.mean(x * x, axis=-1, keepdims=True)
    return x * lax.rsqrt(ms + EPS) * w


def _l2norm(t):
    return t * lax.rsqrt(jnp.sum(t * t, axis=-1, keepdims=True) + EPS)


def _rows(m3, nb, lt):
    return jnp.broadcast_to(m3, (nb, lt, m3.shape[-1])).reshape(nb * lt, m3.shape[-1])


def _causal_conv(x, prev, w, nb, lt):
    width = w.shape[0]
    rows, ch = x.shape
    if nb == 1 and lt > SUBLANES:
        tiles = jnp.concatenate([prev, x.reshape(lt // SUBLANES, SUBLANES, ch)], axis=0)
        sub = lax.broadcasted_iota(jnp.int32, (1, SUBLANES, 1), 1)
        out = tiles[1:] * w[width - 1:width, :]
        for j in range(1, width):
            rot = pltpu.roll(tiles, j, axis=1)
            out = out + jnp.where(sub >= j, rot[1:], rot[:-1]) * w[width - 1 - j:width - j, :]
        return out.reshape(rows, ch)
    tmod = lax.broadcasted_iota(jnp.int32, (rows, 1), 0) & (lt - 1)
    out = x * w[width - 1:width, :]
    for j in range(1, width):
        sh = pltpu.roll(x, j, axis=0)
        for t in range(j):
            p = width - 1 - j + t
            sh = jnp.where(tmod == t, _rows(prev[:, p:p + 1, :], nb, lt), sh)
        out = out + sh * w[width - 1 - j:width - j, :]
    return out


def _last_rows(x, nb, lt, n_state):
    ch = x.shape[-1]
    if nb == 1 and lt > SUBLANES:
        return x[lt - SUBLANES:].reshape(1, SUBLANES, ch)
    return x.reshape(nb, lt, ch)[:, lt - n_state:, :]


def _state_rows(nb, lt, n_state):
    return SUBLANES if (nb == 1 and lt > SUBLANES) else n_state


def _mod_kernel(c_ref, w_ref, b_ref, o_ref):
    o_ref[...] = _dot_x3(_silu(c_ref[...]), w_ref[...]) + b_ref[...]


def _mod_call(c_all, w_ada, b_ada):
    rows = c_all.shape[0]
    cols = w_ada.shape[1]
    tile = 512
    return pl.pallas_call(
        _mod_kernel,
        grid=(cols // tile,),
        in_specs=[pl.BlockSpec((rows, D_MODEL), lambda j: (0, 0)),
                  pl.BlockSpec((D_MODEL, tile), lambda j: (0, j)),
                  pl.BlockSpec((1, tile), lambda j: (0, j))],
        out_specs=pl.BlockSpec((rows, tile), lambda j: (0, j)),
        out_shape=jax.ShapeDtypeStruct((rows, cols), F32),
        compiler_params=_cparams(("arbitrary",)),
        name="adaln_mod",
    )(c_all, w_ada, b_ada)


IN_SPLIT = (DN_CONV_CH, DN_V, BA_PAD, SWA_Q, SWA_KV, SWA_KV)
IN_COLS_PAD = sum(IN_SPLIT)


IN_SLAB = 2 * DN_DK


def _in_kernel(*refs, nb, lt, carry):
    if carry:
        (x_ref, mod_ref, nw_ref, w_ref, cw_ref,
         qkv_ref, z_ref, ba_ref, sq_ref, sk_ref, sv_ref, tail_ref, h_scr) = refs
        prev_ref = tail_ref

        @pl.when(pl.program_id(1) == 0)
        def _():
            tail_ref[...] = jnp.zeros_like(tail_ref)
    else:
        (x_ref, mod_ref, nw_ref, w_ref, cw_ref, prev_ref,
         qkv_ref, z_ref, ba_ref, sq_ref, sk_ref, sv_ref, tail_ref, h_scr) = refs
    rows = nb * lt
    x = x_ref[...]
    ms = jnp.mean(x * x, axis=-1, keepdims=True)
    y = x * lax.rsqrt(ms + EPS) * nw_ref[...]
    h = y * (1.0 + mod_ref[:, 1:2, :]) + mod_ref[:, 0:1, :]
    h_scr[...] = _bf(h.reshape(rows, D_MODEL))

    def proj(lo, n):
        return jnp.dot(h_scr[...], w_ref[:, lo:lo + n], preferred_element_type=F32)

    z_lo = DN_CONV_CH
    ba_lo = z_lo + DN_V
    sq_lo = ba_lo + BA_PAD
    kv_lo = sq_lo + SWA_Q

    def plain_proj(i):
        if i < 2:
            z_ref[:, i * IN_SLAB:(i + 1) * IN_SLAB] = proj(z_lo + i * IN_SLAB, IN_SLAB)
        elif i < 4:
            sq_ref[:, (i - 2) * IN_SLAB:(i - 1) * IN_SLAB] = proj(sq_lo + (i - 2) * IN_SLAB, IN_SLAB)
        elif i == 4:
            kv = proj(kv_lo, 2 * SWA_KV)
            sk_ref[...] = kv[:, :SWA_KV]
            sv_ref[...] = kv[:, SWA_KV:]
        else:
            ba_ref[...] = proj(ba_lo, BA_PAD)

    n_slabs = DN_CONV_CH // IN_SLAB
    raw_next = proj(0, IN_SLAB)
    for slab in range(n_slabs):
        raw = raw_next
        plain_proj(slab)
        if slab + 1 < n_slabs:
            raw_next = proj((slab + 1) * IN_SLAB, IN_SLAB)
        cols = slice(slab * IN_SLAB, (slab + 1) * IN_SLAB)
        c = _silu(_causal_conv(raw, prev_ref[:, :, cols], cw_ref[:, cols], nb, lt))
        tail_ref[:, :, cols] = _last_rows(raw, nb, lt, DN_CONV - 1)
        if slab * IN_SLAB < 2 * DN_QK:
            scale = DN_DK ** -0.5 if slab * IN_SLAB < DN_QK else 1.0
            c = jnp.concatenate([_l2norm(c[:, i * DN_DK:(i + 1) * DN_DK]) * scale
                                 for i in range(IN_SLAB // DN_DK)], axis=1)
        qkv_ref[:, cols] = c


def _in_call(x, mod3, norm_w, w_cat, conv_w, nb, lt, state=None):
    carry = state is None
    bsz, seq, _ = x.shape
    nt = seq // lt
    rows = nb * lt
    n_tok = bsz * seq
    n_state = _state_rows(nb, lt, DN_CONV - 1)
    row_map = lambda i, j: (i * nt + j, 0)
    const = lambda i, j: (0, 0)
    per_seq = lambda i, j: (i, 0, 0)
    in_specs = [pl.BlockSpec((nb, lt, D_MODEL), lambda i, j: (i, j, 0)),
                pl.BlockSpec((nb, 6, D_MODEL), per_seq),
                pl.BlockSpec((1, D_MODEL), const),
                pl.BlockSpec((D_MODEL, IN_COLS_PAD), const),
                pl.BlockSpec((DN_CONV, DN_CONV_CH), const)]
    args = [x, mod3, norm_w, w_cat, conv_w]
    if not carry:
        in_specs.append(pl.BlockSpec((nb, DN_CONV - 1, DN_CONV_CH), per_seq))
        args.append(state)
    return pl.pallas_call(
        functools.partial(_in_kernel, nb=nb, lt=lt, carry=carry),
        grid=(bsz // nb, nt),
        in_specs=in_specs,
        out_specs=[pl.BlockSpec((rows, n), row_map) for n in IN_SPLIT]
        + [pl.BlockSpec((nb, n_state, DN_CONV_CH), per_seq)],
        out_shape=[jax.ShapeDtypeStruct((n_tok, n), F32) for n in IN_SPLIT]
        + [jax.ShapeDtypeStruct((bsz, n_state, DN_CONV_CH), F32)],
        scratch_shapes=[pltpu.VMEM((rows, D_MODEL), BF16)],
        compiler_params=_cparams(("arbitrary", "arbitrary")),
        name="norm_in_proj_carry" if carry else "norm_in_proj_state",
    )(*args)


INV_BASE_SHIFT = 2
DN_UNITS_CARRY = 2
SWA_BLOCKS_PER_STEP = 4
DN_UNITS_STATE = 1


def _unit_lower_inverses(a_mats, ri, ci, chunk_shift):
    def blocks(s):
        return (ri >> s) == (ci >> s)

    base = min(INV_BASE_SHIFT, chunk_shift)
    xs = [jnp.where(blocks(base), -a, 0.0) for a in a_mats]
    ts = [jnp.where(ri == ci, 1.0, n) for n in xs]
    for lvl in range(1, base):
        xs = [_dot(x, x) for x in xs]
        ts = [t + _dot(t, x) for t, x in zip(ts, xs)]
    for s in range(base, chunk_shift):
        sel = blocks(s + 1) & jnp.logical_not(blocks(s))
        ets = [_dot(jnp.where(sel, a, 0.0), t) for a, t in zip(a_mats, ts)]
        ts = [t - _dot(t, et) for t, et in zip(ts, ets)]
    return ts


def _dn_kernel(*refs, nb, chunk, carry, units):
    if carry:
        (qkv_ref, z_ref, ba_ref, alog_ref, dt_ref, nw_ref, o_ref, s_ref) = refs
        s0_ref = s_ref

        @pl.when(pl.program_id(1) == 0)
        def _():
            s_ref[...] = jnp.zeros_like(s_ref)
    else:
        (qkv_ref, z_ref, ba_ref, alog_ref, dt_ref, nw_ref, s0_ref, o_ref, s_ref) = refs

    ri = lax.broadcasted_iota(jnp.int32, (UNIT, UNIT), 0)
    ci = lax.broadcasted_iota(jnp.int32, (UNIT, UNIT), 1)
    shift = int(math.log2(chunk))
    same = (ri >> shift) == (ci >> shift)
    incl = same & (ri >= ci)
    strict = same & (ri > ci)

    ba = ba_ref[...]
    beta_full = _sigmoid(ba)
    g_full = -jnp.exp(alog_ref[...]) * _softplus(ba + dt_ref[...])
    masks = jnp.concatenate([jnp.where(incl, 1.0, 0.0), jnp.where(same, 1.0, 0.0)], axis=0).astype(BF16)
    gsums = [_dot_mask(masks, g_full[u * UNIT:(u + 1) * UNIT]) for u in range(units)]
    g_cum = [g[:UNIT] for g in gsums]
    g_tot = [g[UNIT:] for g in gsums]
    g_cum_t = [g.T for g in g_cum]

    probs = [(u, h) for u in range(units) for h in range(DN_HEADS)]
    n_p = len(probs)
    rs = lambda u: slice(u * UNIT, (u + 1) * UNIT)

    def head_cols(base):
        return [qkv_ref[rs(u), base + h * DN_DK:base + (h + 1) * DN_DK] for u, h in probs]

    q = head_cols(0)
    k = head_cols(DN_QK)
    v = head_cols(2 * DN_QK)
    gc = [g_cum[u][:, DN_HEADS + h:DN_HEADS + h + 1] for u, h in probs]
    gr = [g_cum_t[u][DN_HEADS + h:DN_HEADS + h + 1, :] for u, h in probs]
    gt = [g_tot[u][:, DN_HEADS + h:DN_HEADS + h + 1] for u, h in probs]
    bc = [beta_full[rs(u), h:h + 1] for u, h in probs]
    decay = [jnp.where(incl, jnp.exp(jnp.where(incl, gc[p] - gr[p], 0.0)), 0.0) for p in range(n_p)]
    e_g = [jnp.exp(gc[p]) for p in range(n_p)]
    kq = [_dot_nt(jnp.concatenate([k[p], q[p]], axis=0), k[p]) for p in range(n_p)]
    qk = [kq[p][UNIT:] * decay[p] for p in range(n_p)]
    a_mats = [jnp.where(strict, bc[p] * kq[p][:UNIT] * decay[p], 0.0) for p in range(n_p)]
    t_inv = _unit_lower_inverses(a_mats, ri, ci, shift)
    wvk = [_dot(t_inv[p], jnp.concatenate([v[p] * bc[p], k[p] * (bc[p] * e_g[p])], axis=1)) for p in range(n_p)]
    w_v = [w[:, :DN_DV] for w in wvk]
    w_k = [w[:, DN_DV:] for w in wvk]
    q_dec = [q[p] * e_g[p] for p in range(n_p)]
    k_tail_t = [(k[p] * jnp.exp(gt[p] - gc[p])).T for p in range(n_p)]
    c_dec = [jnp.exp(gt[p]) for p in range(n_p)]

    outs = {}
    if carry:
        state = [s_ref[0, h] for h in range(DN_HEADS)]
        for u in range(units):
            ps = [u * DN_HEADS + h for h in range(DN_HEADS)]
            r = [_dot(jnp.concatenate([w_k[p], q_dec[p]], axis=0), state[h]) for h, p in enumerate(ps)]
            uu = [w_v[p] - r[h][:UNIT] for h, p in enumerate(ps)]
            for h, p in enumerate(ps):
                outs[p] = r[h][UNIT:] + _dot(qk[p], uu[h])
            state = [state[h] * c_dec[p][0:1, :] + _dot(k_tail_t[p], uu[h]) for h, p in enumerate(ps)]
        for h in range(DN_HEADS):
            s_ref[0, h] = state[h]
    else:
        per_unit = nb // units
        for p, (u, h) in enumerate(probs):
            us, qs = [], []
            for s in range(per_unit):
                lo = s * chunk
                lhs = jnp.concatenate([w_k[p][lo:lo + chunk], q_dec[p][lo:lo + chunk]], axis=0)
                r = jnp.dot(lhs, s0_ref[u * per_unit + s, h], preferred_element_type=F32)
                us.append(w_v[p][lo:lo + chunk] - r[:chunk])
                qs.append(r[chunk:])
            uu = jnp.concatenate(us, axis=0)
            outs[p] = jnp.concatenate(qs, axis=0) + _dot(qk[p], uu)
            for s in range(per_unit):
                lo = s * chunk
                upd = jnp.dot(k_tail_t[p][:, lo:lo + chunk], uu[lo:lo + chunk], preferred_element_type=F32)
                s_ref[u * per_unit + s, h] = s0_ref[u * per_unit + s, h] * c_dec[p][lo:lo + 1, :] + upd

    for p, (u, h) in enumerate(probs):
        zz = z_ref[rs(u), h * DN_DV:(h + 1) * DN_DV]
        o_ref[rs(u), h * DN_DV:(h + 1) * DN_DV] = _rms(outs[p], nw_ref[...]) * _silu(zz)


def _dn_call(qkv, z, ba, alog_row, dt_row, norm_w, bsz, seq, state=None):
    carry = state is None
    units = DN_UNITS_CARRY if carry else DN_UNITS_STATE
    rows = units * UNIT
    nb = 1 if carry else rows // seq
    chunk = UNIT if carry else seq
    nt = seq * nb // rows
    n_tok = bsz * seq
    row_map = lambda i, j: (i * nt + j, 0)
    const = lambda i, j: (0, 0)
    in_specs = [pl.BlockSpec((rows, DN_CONV_CH), row_map),
                pl.BlockSpec((rows, DN_V), row_map),
                pl.BlockSpec((rows, BA_PAD), row_map),
                pl.BlockSpec((1, BA_PAD), const),
                pl.BlockSpec((1, BA_PAD), const),
                pl.BlockSpec((1, DN_DV), const)]
    args = [qkv, z, ba, alog_row, dt_row, norm_w]
    if not carry:
        in_specs.append(pl.BlockSpec((nb, DN_HEADS, DN_DK, DN_DV), lambda i, j: (i, 0, 0, 0)))
        args.append(state)
    return pl.pallas_call(
        functools.partial(_dn_kernel, nb=nb, chunk=chunk, carry=carry, units=units),
        grid=(bsz // nb, nt),
        in_specs=in_specs,
        out_specs=[pl.BlockSpec((rows, DN_V), row_map),
                   pl.BlockSpec((nb, DN_HEADS, DN_DK, DN_DV), lambda i, j: (i, 0, 0, 0))],
        out_shape=[jax.ShapeDtypeStruct((n_tok, DN_V), F32),
                   jax.ShapeDtypeStruct((bsz, DN_HEADS, DN_DK, DN_DV), F32)],
        compiler_params=_cparams(("arbitrary", "arbitrary")),
        name="gated_deltanet_carry" if carry else "gated_deltanet_state",
    )(*args)


def _bucket_table():
    i = np.arange(WINDOW, dtype=np.int64)[:, None]
    j = np.arange(2 * WINDOW, dtype=np.int64)[None, :]
    d = np.maximum(i + WINDOW - j, 0)
    exact = N_BUCKETS // 2
    logv = (np.log(np.maximum(d, 1).astype(np.float32) / np.float32(exact)).astype(np.float32)
            / np.float32(math.log(MAX_DISTANCE / exact)))
    large = np.minimum(exact + (logv * np.float32(N_BUCKETS - exact)).astype(np.int32), N_BUCKETS - 1)
    return np.where(d < exact, d, large).astype(np.int32)


def _bias_lookup(rb_ref, bucket, head):
    acc = jnp.zeros(bucket.shape, F32)
    for b in range(N_BUCKETS):
        acc = jnp.where(bucket == b, rb_ref[b, head], acc)
    return acc


def _bias_kernel(rb_ref, bucket_t_ref, bucket_ref, ot_ref, os_ref):
    bucket_t = bucket_t_ref[...]
    kj = lax.broadcasted_iota(jnp.int32, bucket_t.shape, 0)
    qi = lax.broadcasted_iota(jnp.int32, bucket_t.shape, 1)
    dist = qi + WINDOW - kj
    valid = (dist >= 0) & (dist < WINDOW)
    for hk in range(SWA_KV_HEADS):
        for par in range(2):
            for st in range(SWA_GROUP // 2):
                head = hk * SWA_GROUP + 2 * st + par
                gen = jnp.where(valid, _bias_lookup(rb_ref, bucket_t, head), NEG_INF)
                ot_ref[1, hk, par, :, st * WINDOW:(st + 1) * WINDOW] = gen
                ot_ref[0, hk, par, :, st * WINDOW:(st + 1) * WINDOW] = jnp.where(kj >= WINDOW, gen, NEG_INF)
    bucket = bucket_ref[...]
    qi = lax.broadcasted_iota(jnp.int32, bucket.shape, 0)
    kj = lax.broadcasted_iota(jnp.int32, bucket.shape, 1)
    dist = qi + WINDOW - kj
    valid = (dist >= 0) & (dist < WINDOW)
    for head in range(SWA_HEADS):
        os_ref[head] = jnp.where(valid, _bias_lookup(rb_ref, bucket, head), NEG_INF)


BIAS_T_SHAPE = (2, SWA_KV_HEADS, 2, 2 * WINDOW, (SWA_GROUP // 2) * WINDOW)
BIAS_S_SHAPE = (SWA_HEADS, SUBLANES, 2 * WINDOW)


def _bias_call(rel_bias):
    bucket = _bucket_table()
    return pl.pallas_call(
        _bias_kernel,
        in_specs=[pl.BlockSpec(memory_space=pltpu.SMEM),
                  pl.BlockSpec((2 * WINDOW, WINDOW), lambda: (0, 0)),
                  pl.BlockSpec((SUBLANES, 2 * WINDOW), lambda: (0, 0))],
        out_specs=[pl.BlockSpec(BIAS_T_SHAPE, lambda: (0,) * len(BIAS_T_SHAPE)),
                   pl.BlockSpec(BIAS_S_SHAPE, lambda: (0,) * len(BIAS_S_SHAPE))],
        out_shape=[jax.ShapeDtypeStruct(BIAS_T_SHAPE, F32), jax.ShapeDtypeStruct(BIAS_S_SHAPE, F32)],
        name="swa_rel_bias_table",
    )(rel_bias, jnp.asarray(np.ascontiguousarray(bucket.T)), jnp.asarray(bucket[:SUBLANES]))


def _softmax_sink(s, sink):
    m = jnp.maximum(jnp.max(s, axis=-1, keepdims=True), sink)
    p = jnp.exp(s - m)
    return p / (jnp.sum(p, axis=-1, keepdims=True) + jnp.exp(sink - m))


def _softmax_sink_parts(s, sink):
    m = jnp.maximum(jnp.max(s, axis=0, keepdims=True), sink)
    p = jnp.exp(s - m)
    return p, 1.0 / (jnp.sum(p, axis=0, keepdims=True) + jnp.exp(sink - m))


def _half_lane_variants(full, hk, lo_half):
    rolled = pltpu.roll(full, SWA_HD, axis=1)
    low_src, high_src = (full, rolled) if hk == 0 else (rolled, full)
    return jnp.where(lo_half, low_src, 0.0), jnp.where(lo_half, 0.0, high_src)


def _swa_prompt_kernel(sink_ref, q_ref, kp_ref, kc_ref, vp_ref, vc_ref, bias_ref, o_ref, *, n_blk):
    step = pl.program_id(1)
    lo_half = lax.broadcasted_iota(jnp.int32, (1, 2 * SWA_HD), 1) < SWA_HD
    lo_rows = lax.broadcasted_iota(jnp.int32, (2 * SWA_HD, 1), 0) < SWA_HD
    q = _bf(q_ref[...] * (SWA_HD ** -0.5))
    keys = jnp.concatenate([kp_ref[...], kc_ref[...]], axis=0)
    vals = jnp.concatenate([vp_ref[...], vc_ref[...]], axis=0)
    k_var = [[_bf(t) for t in _half_lane_variants(keys, hk, lo_half)] for hk in range(SWA_KV_HEADS)]
    v_var_t = [[_bf(t.T) for t in _half_lane_variants(vals, hk, lo_half)] for hk in range(SWA_KV_HEADS)]
    n_stack = SWA_GROUP // 2
    sinks = [[jnp.concatenate([jnp.full((1, WINDOW), sink_ref[hk * SWA_GROUP + 2 * st + par], F32)
                               for st in range(n_stack)], axis=1) for par in range(2)]
             for hk in range(SWA_KV_HEADS)]
    nt_dims = (((1,), (1,)), ((), ()))

    def scores(b):
        rows = slice(b * WINDOW, (b + 1) * WINDOW)
        win = slice(b * WINDOW, (b + 2) * WINDOW)
        variant = jnp.where(step == 0, 0, 1) if b == 0 else 1
        out = []
        for hk in range(SWA_KV_HEADS):
            q2 = jnp.concatenate([q[rows, (hk * n_stack + st) * 2 * SWA_HD:(hk * n_stack + st + 1) * 2 * SWA_HD]
                                  for st in range(n_stack)], axis=0)
            out.append([lax.dot_general(k_var[hk][par][win], q2, nt_dims, preferred_element_type=F32)
                        + bias_ref[variant, hk, par] for par in range(2)])
        return out

    s_next = scores(0)
    for b in range(n_blk):
        s_cur = s_next
        if b + 1 < n_blk:
            s_next = scores(b + 1)
        rows = slice(b * WINDOW, (b + 1) * WINDOW)
        win = slice(b * WINDOW, (b + 2) * WINDOW)
        for hk in range(SWA_KV_HEADS):
            parts = [_softmax_sink_parts(s_cur[hk][par], sinks[hk][par]) for par in range(2)]
            o_t = (jnp.dot(v_var_t[hk][0][:, win], _bf(parts[0][0]), preferred_element_type=F32)
                   + jnp.dot(v_var_t[hk][1][:, win], _bf(parts[1][0]), preferred_element_type=F32))
            o_t = o_t * jnp.where(lo_rows, parts[0][1], parts[1][1])
            for st in range(n_stack):
                lo = (hk * n_stack + st) * 2 * SWA_HD
                o_ref[rows, lo:lo + 2 * SWA_HD] = o_t[:, st * WINDOW:(st + 1) * WINDOW].T


def _swa_prompt_call(sinks, sq, sk, sv, bias, bsz, seq):
    n_blk = SWA_BLOCKS_PER_STEP
    tile = n_blk * WINDOW
    nt = seq // tile
    cur = lambda b, i: (b * nt + i, 0)
    prv = lambda b, i: (b * nt * n_blk + jnp.maximum(i * n_blk - 1, 0), 0)
    return pl.pallas_call(
        functools.partial(_swa_prompt_kernel, n_blk=n_blk),
        grid=(bsz, nt),
        in_specs=[pl.BlockSpec(memory_space=pltpu.SMEM),
                  pl.BlockSpec((tile, SWA_Q), cur),
                  pl.BlockSpec((WINDOW, SWA_KV), prv),
                  pl.BlockSpec((tile, SWA_KV), cur),
                  pl.BlockSpec((WINDOW, SWA_KV), prv),
                  pl.BlockSpec((tile, SWA_KV), cur),
                  pl.BlockSpec(BIAS_T_SHAPE, lambda b, i: (0,) * len(BIAS_T_SHAPE))],
        out_specs=pl.BlockSpec((tile, SWA_Q), cur),
        out_shape=jax.ShapeDtypeStruct((bsz * seq, SWA_Q), F32),
        compiler_params=_cparams(("arbitrary", "arbitrary")),
        name="swa_banded",
    )(sinks, sq, sk, sk, sv, sv, bias)


def _swa_sample_kernel(sink_ref, q_ref, kn_ref, vn_ref, kc_ref, vc_ref, bias_ref, o_ref, ko_ref, vo_ref, *, nb, lt):
    q_all = q_ref[...] * (SWA_HD ** -0.5)
    kn_all = kn_ref[...]
    vn_all = vn_ref[...]
    pad = jnp.zeros((WINDOW - lt, SWA_HD), F32)
    rows_out = []
    for s in range(nb):
        lo = s * lt
        kc = kc_ref[s]
        vc = vc_ref[s]
        kn = kn_all[lo:lo + lt]
        vn = vn_all[lo:lo + lt]
        ko_ref[s] = jnp.concatenate([kc[lt:], kn], axis=0)
        vo_ref[s] = jnp.concatenate([vc[lt:], vn], axis=0)
        q = q_all[lo:lo + lt]
        pieces = []
        for hk in range(SWA_KV_HEADS):
            cols = slice(hk * SWA_HD, (hk + 1) * SWA_HD)
            k_h = jnp.concatenate([kc[:, cols], kn[:, cols], pad], axis=0)
            v_h = jnp.concatenate([vc[:, cols], vn[:, cols], pad], axis=0)
            heads = range(hk * SWA_GROUP, (hk + 1) * SWA_GROUP)
            q4 = jnp.concatenate([q[:, h * SWA_HD:(h + 1) * SWA_HD] for h in heads], axis=0)
            b4 = jnp.concatenate([bias_ref[h, 0:lt, :] for h in heads], axis=0)
            sink4 = jnp.concatenate([jnp.full((lt, 1), sink_ref[h], F32) for h in heads], axis=0)
            sc = lax.dot_general(q4, k_h, (((1,), (1,)), ((), ())), preferred_element_type=F32)
            sc = sc + b4
            o4 = jnp.dot(_softmax_sink(sc, sink4), v_h, preferred_element_type=F32)
            pieces += [o4[g * lt:(g + 1) * lt] for g in range(SWA_GROUP)]
        rows_out.append(jnp.concatenate(pieces, axis=1))
    o_ref[...] = jnp.concatenate(rows_out, axis=0)


def _swa_sample_call(sinks, sq, sk, sv, cache_k, cache_v, bias, bsz, seq):
    nb = UNIT // seq
    rows = lambda i: (i, 0)
    seqs = lambda i: (i, 0, 0)
    return pl.pallas_call(
        functools.partial(_swa_sample_kernel, nb=nb, lt=seq),
        grid=(bsz // nb,),
        in_specs=[pl.BlockSpec(memory_space=pltpu.SMEM),
                  pl.BlockSpec((UNIT, SWA_Q), rows),
                  pl.BlockSpec((UNIT, SWA_KV), rows),
                  pl.BlockSpec((UNIT, SWA_KV), rows),
                  pl.BlockSpec((nb, WINDOW, SWA_KV), seqs),
                  pl.BlockSpec((nb, WINDOW, SWA_KV), seqs),
                  pl.BlockSpec(BIAS_S_SHAPE, lambda i: (0,) * len(BIAS_S_SHAPE))],
        out_specs=[pl.BlockSpec((UNIT, SWA_Q), rows),
                   pl.BlockSpec((nb, WINDOW, SWA_KV), seqs),
                   pl.BlockSpec((nb, WINDOW, SWA_KV), seqs)],
        out_shape=[jax.ShapeDtypeStruct((bsz * seq, SWA_Q), F32),
                   jax.ShapeDtypeStruct((bsz, WINDOW, SWA_KV), F32),
                   jax.ShapeDtypeStruct((bsz, WINDOW, SWA_KV), F32)],
        compiler_params=_cparams(("arbitrary",)),
        name="swa_cached",
    )(sinks, sq, sk, sv, cache_k, cache_v, bias)


def _post_kernel(*refs, nb, lt, carry):
    if carry:
        (odn_ref, oswa_ref, x_ref, mod_ref, wout_ref, nfw_ref, wup_ref, cw_ref, cb_ref, wdn_ref, fnw_ref,
         y_ref, fbuf_ref, h_scr, x1_scr, acc_scr) = refs
        prev_ref = fbuf_ref

        @pl.when(pl.program_id(1) == 0)
        def _():
            fbuf_ref[...] = jnp.zeros_like(fbuf_ref)
    else:
        (odn_ref, oswa_ref, x_ref, mod_ref, wout_ref, nfw_ref, wup_ref, cw_ref, cb_ref, wdn_ref, fnw_ref,
         prev_ref, y_ref, fbuf_ref, h_scr, x1_scr, acc_scr) = refs
    rows = nb * lt
    attn = (jnp.dot(_bf(odn_ref[...]), wout_ref[0:DN_V, :], preferred_element_type=F32)
            + jnp.dot(_bf(oswa_ref[...]), wout_ref[DN_V:, :], preferred_element_type=F32))
    x1 = x_ref[...].reshape(rows, D_MODEL) + _rows(mod_ref[:, 2:3, :], nb, lt) * attn
    x1_scr[...] = x1
    h = _rms(x1, nfw_ref[...]) * (1.0 + _rows(mod_ref[:, 4:5, :], nb, lt)) + _rows(mod_ref[:, 3:4, :], nb, lt)
    h_scr[...] = _bf(h)
    acc_scr[...] = jnp.zeros_like(acc_scr)
    n_chunks = D_FF // FFN_CHUNK

    def col_slices(c):
        return [slice(base + c * FFN_CHUNK, base + (c + 1) * FFN_CHUNK) for base in (0, D_FF)]

    def up_proj(c):
        return [jnp.dot(h_scr[...], wup_ref[:, cols], preferred_element_type=F32) for cols in col_slices(c)]

    u_next = up_proj(0)
    for c in range(n_chunks):
        u_cur = u_next
        if c + 1 < n_chunks:
            u_next = up_proj(c + 1)
        halves = []
        for u, cols in zip(u_cur, col_slices(c)):
            prev = prev_ref[:, :, cols]
            halves.append(_causal_conv(u, prev, cw_ref[:, cols], nb, lt) + cb_ref[:, cols])
            fbuf_ref[:, :, cols] = _last_rows(u, nb, lt, FFN_CONV - 1)
        act = _silu(halves[0]) * halves[1]
        acc_scr[...] += jnp.dot(_bf(act), wdn_ref[c * FFN_CHUNK:(c + 1) * FFN_CHUNK, :],
                                preferred_element_type=F32)
    x2 = x1_scr[...] + _rows(mod_ref[:, 5:6, :], nb, lt) * acc_scr[...]
    y_ref[...] = _rms(x2, fnw_ref[...]).reshape(nb, lt, D_MODEL)


def _post_call(o_dn, o_swa, x, mod3, w_out, norm_ffn_w, w_up, conv_w, conv_b, w_down, final_w, nb, lt, state=None):
    carry = state is None
    bsz, seq, _ = x.shape
    nt = seq // lt
    rows = nb * lt
    n_state = _state_rows(nb, lt, FFN_CONV - 1)
    row_map = lambda i, j: (i * nt + j, 0)
    const = lambda i, j: (0, 0)
    in_specs = [pl.BlockSpec((rows, DN_V), row_map),
                pl.BlockSpec((rows, SWA_Q), row_map),
                pl.BlockSpec((nb, lt, D_MODEL), lambda i, j: (i, j, 0)),
                pl.BlockSpec((nb, 6, D_MODEL), lambda i, j: (i, 0, 0)),
                pl.BlockSpec((D_MODEL, D_MODEL), const),
                pl.BlockSpec((1, D_MODEL), const),
                pl.BlockSpec((D_MODEL, 2 * D_FF), const),
                pl.BlockSpec((FFN_CONV, 2 * D_FF), const),
                pl.BlockSpec((1, 2 * D_FF), const),
                pl.BlockSpec((D_FF, D_MODEL), const),
                pl.BlockSpec((1, D_MODEL), const)]
    args = [o_dn, o_swa, x, mod3, w_out, norm_ffn_w, w_up, conv_w, conv_b, w_down, final_w]
    if not carry:
        in_specs.append(pl.BlockSpec((nb, FFN_CONV - 1, 2 * D_FF), lambda i, j: (i, 0, 0)))
        args.append(state)
    return pl.pallas_call(
        functools.partial(_post_kernel, nb=nb, lt=lt, carry=carry),
        grid=(bsz // nb, nt),
        in_specs=in_specs,
        out_specs=[pl.BlockSpec((nb, lt, D_MODEL), lambda i, j: (i, j, 0)),
                   pl.BlockSpec((nb, n_state, 2 * D_FF), lambda i, j: (i, 0, 0))],
        out_shape=[jax.ShapeDtypeStruct((bsz, seq, D_MODEL), F32),
                   jax.ShapeDtypeStruct((bsz, n_state, 2 * D_FF), F32)],
        scratch_shapes=[pltpu.VMEM((rows, D_MODEL), BF16),
                        pltpu.VMEM((rows, D_MODEL), F32),
                        pltpu.VMEM((rows, D_MODEL), F32)],
        compiler_params=_cparams(("arbitrary", "arbitrary")),
        name="out_proj_convffn_carry" if carry else "out_proj_convffn_state",
    )(*args)


def _pad_row(vec, offset):
    return jnp.zeros((1, BA_PAD), F32).at[0, offset:offset + vec.shape[0]].set(vec)


def kernel(x_prompt, x_sample, state_dn_conv, state_dn_ssm, cache_swa_k, cache_swa_v, state_ffn_conv, c_prompt, c_sample, rel_bias, final_norm_w, w_ada, b_ada, norm_mix_w, w_in, dn_conv_w, dn_A_log, dn_dt_bias, dn_norm_w, swa_sinks, w_out, norm_ffn_w, ffn_w_up, ffn_conv_w, ffn_conv_b, ffn_w_down):
    bp, lp, _ = x_prompt.shape
    bs, ls, _ = x_sample.shape
    layer = 0

    w_i = w_in[layer]
    n_ba = 2 * DN_HEADS
    ba_lo = DN_CONV_CH + DN_V
    w_cat = jnp.concatenate(
        [w_i[:, :ba_lo], jnp.pad(w_i[:, ba_lo:ba_lo + n_ba], ((0, 0), (0, BA_PAD - n_ba))), w_i[:, ba_lo + n_ba:]],
        axis=1).astype(BF16)
    w_out_b = w_out[layer].astype(BF16)
    w_up_b = ffn_w_up[layer].astype(BF16)
    w_dn_b = ffn_w_down[layer].astype(BF16)
    alog_row = _pad_row(dn_A_log[layer], DN_HEADS)
    dt_row = _pad_row(dn_dt_bias[layer], DN_HEADS)
    row = lambda v: v.reshape(1, -1)

    n_c = bp + bs
    n_c_pad = -(-n_c // 8) * 8
    c_all = jnp.pad(jnp.concatenate([c_prompt, c_sample], axis=0), ((0, n_c_pad - n_c), (0, 0)))
    mod = _mod_call(c_all, w_ada[layer], row(b_ada[layer]))
    mod_p = mod[:bp].reshape(bp, 6, D_MODEL)
    mod_s = mod[bp:n_c].reshape(bs, 6, D_MODEL)

    bias_t, bias_s = _bias_call(rel_bias)
    sinks = swa_sinks[layer]

    def mixer_in(x, mod3, nb, lt, state=None):
        return _in_call(x, mod3, row(norm_mix_w[layer]), w_cat, dn_conv_w[layer], nb, lt, state)

    def post(o_dn, o_swa, x, mod3, nb, lt, state=None):
        return _post_call(o_dn, o_swa, x, mod3, w_out_b, row(norm_ffn_w[layer]), w_up_b, ffn_conv_w[layer],
                          row(ffn_conv_b[layer]), w_dn_b, row(final_norm_w), nb, lt, state)

    dn_args = (alog_row, dt_row, row(dn_norm_w[layer]))

    qkv, z, ba, sq, sk, sv, p_tail = mixer_in(x_prompt, mod_p, 1, 512)
    o_dn, p_ssm = _dn_call(qkv, z, ba, *dn_args, bp, lp)
    o_swa = _swa_prompt_call(sinks, sq, sk, sv, bias_t, bp, lp)
    y_prompt, p_ffn_tail = post(o_dn, o_swa, x_prompt, mod_p, 1, 256)
    p_dn_conv = p_tail[:, SUBLANES - (DN_CONV - 1):]
    p_fbuf = p_ffn_tail[:, SUBLANES - (FFN_CONV - 1):]
    p_swa_k = sk.reshape(bp, lp, SWA_KV_HEADS, SWA_HD)[:, lp - WINDOW:]
    p_swa_v = sv.reshape(bp, lp, SWA_KV_HEADS, SWA_HD)[:, lp - WINDOW:]

    qkv_s, z_s, ba_s, sq_s, sk_s, sv_s, s_dn_conv = mixer_in(x_sample, mod_s, 64, ls, state=state_dn_conv[layer])
    o_dn_s, s_ssm = _dn_call(qkv_s, z_s, ba_s, *dn_args, bs, ls, state=state_dn_ssm[layer])
    o_swa_s, s_k, s_v = _swa_sample_call(sinks, sq_s, sk_s, sv_s,
                                         cache_swa_k[layer].reshape(bs, WINDOW, SWA_KV),
                                         cache_swa_v[layer].reshape(bs, WINDOW, SWA_KV), bias_s, bs, ls)
    y_sample, s_fbuf = post(o_dn_s, o_swa_s, x_sample, mod_s, 32, ls, state=state_ffn_conv[layer])

    return (y_prompt, y_sample, p_dn_conv[None], s_dn_conv[None], p_ssm[None], s_ssm[None],
            p_swa_k[None], s_k.reshape(bs, WINDOW, SWA_KV_HEADS, SWA_HD)[None],
            p_swa_v[None], s_v.reshape(bs, WINDOW, SWA_KV_HEADS, SWA_HD)[None],
            p_fbuf[None], s_fbuf[None])
```

```python
import functools
import math

import numpy as np
import jax
import jax.numpy as jnp
from jax import lax
from jax.experimental import pallas as pl
from jax.experimental.pallas import tpu as pltpu

F32 = jnp.float32
BF16 = jnp.bfloat16

D_MODEL = 1024
PAST_LEN = 16384
DN_HEADS = 4
DN_DK = 128
DN_DV = 128
DN_CONV = 4
SWA_HEADS = 8
SWA_KV_HEADS = 2
SWA_GROUP = SWA_HEADS // SWA_KV_HEADS
SWA_HD = 64
WINDOW = 128
N_BUCKETS = 32
MAX_DISTANCE = 128
D_FF = 2816
FFN_CONV = 3
EPS = 1e-6
NEG_INF = -1e30

DN_QK = DN_HEADS * DN_DK
DN_V = DN_HEADS * DN_DV
DN_CONV_CH = 2 * DN_QK + DN_V
SWA_Q = SWA_HEADS * SWA_HD
SWA_KV = SWA_KV_HEADS * SWA_HD
BA_PAD = 128
SUBLANES = 8
UNIT = 128
FFN_CHUNK = 256
VMEM_LIMIT = 56 * 1024 * 1024
RESIDENT = pl.Buffered(1)


def _cparams(sem):
    return pltpu.CompilerParams(dimension_semantics=sem, vmem_limit_bytes=VMEM_LIMIT)


def _bf(x):
    return x.astype(BF16)


def _dot(a, b):
    return jnp.dot(_bf(a), _bf(b), preferred_element_type=F32)


def _dot_nt(a, b):
    return lax.dot_general(_bf(a), _bf(b), (((1,), (1,)), ((), ())), preferred_element_type=F32)


def _split2(x):
    hi = _bf(x)
    lo = _bf(x - hi.astype(F32))
    return hi, lo


def _dot_x3(a, b):
    ah, al = _split2(a)
    bh, bl = _split2(b)
    d = functools.partial(jnp.dot, preferred_element_type=F32)
    return d(ah, bh) + (d(al, bh) + d(ah, bl))


def _dot_mask(m, x):
    hi = _bf(x)
    r = x - hi.astype(F32)
    mid = _bf(r)
    lo = _bf(r - mid.astype(F32))
    d = functools.partial(jnp.dot, preferred_element_type=F32)
    return d(m, hi) + (d(m, mid) + d(m, lo))


def _sigmoid(x):
    return 1.0 / (1.0 + jnp.exp(-x))


def _silu(x):
    return x * _sigmoid(x)


def _softplus(x):
    return jnp.maximum(x, 0.0) + jnp.log1p(jnp.exp(-jnp.abs(x)))


def _rms(x, w):
    ms = jnp.mean(x * x, axis=-1, keepdims=True)
    return x * lax.rsqrt(ms + EPS) * w


def _l2norm(t):
    return t * lax.rsqrt(jnp.sum(t * t, axis=-1, keepdims=True) + EPS)


def _rows(m3, nb, lt):
    return jnp.broadcast_to(m3, (nb, lt, m3.shape[-1])).reshape(nb * lt, m3.shape[-1])


def _causal_conv(x, prev, w, nb, lt):
    width = w.shape[0]
    rows, ch = x.shape
    if nb == 1 and lt > SUBLANES:
        tiles = jnp.concatenate([prev, x.reshape(lt // SUBLANES, SUBLANES, ch)], axis=0)
        sub = lax.broadcasted_iota(jnp.int32, (1, SUBLANES, 1), 1)
        out = tiles[1:] * w[width - 1:width, :]
        for j in range(1, width):
            rot = pltpu.roll(tiles, j, axis=1)
            out = out + jnp.where(sub >= j, rot[1:], rot[:-1]) * w[width - 1 - j:width - j, :]
        return out.reshape(rows, ch)
    tmod = lax.broadcasted_iota(jnp.int32, (rows, 1), 0) & (lt - 1)
    out = x * w[width - 1:width, :]
    for j in range(1, width):
        sh = pltpu.roll(x, j, axis=0)
        for t in range(j):
            p = width - 1 - j + t
            sh = jnp.where(tmod == t, _rows(prev[:, p:p + 1, :], nb, lt), sh)
        out = out + sh * w[width - 1 - j:width - j, :]
    return out


def _last_rows(x, nb, lt, n_state):
    ch = x.shape[-1]
    if nb == 1 and lt > SUBLANES:
        return x[lt - SUBLANES:].reshape(1, SUBLANES, ch)
    return x.reshape(nb, lt, ch)[:, lt - n_state:, :]


def _state_rows(nb, lt, n_state):
    return SUBLANES if (nb == 1 and lt > SUBLANES) else n_state


def _mod_kernel(c_ref, w_ref, b_ref, o_ref):
    o_ref[...] = _dot_x3(_silu(c_ref[...]), w_ref[...]) + b_ref[...]


def _mod_call(c_all, w_ada, b_ada):
    rows = c_all.shape[0]
    cols = w_ada.shape[1]
    tile = 512
    return pl.pallas_call(
        _mod_kernel,
        grid=(cols // tile,),
        in_specs=[pl.BlockSpec((rows, D_MODEL), lambda j: (0, 0)),
                  pl.BlockSpec((D_MODEL, tile), lambda j: (0, j)),
                  pl.BlockSpec((1, tile), lambda j: (0, j))],
        out_specs=pl.BlockSpec((rows, tile), lambda j: (0, j)),
        out_shape=jax.ShapeDtypeStruct((rows, cols), F32),
        compiler_params=_cparams(("arbitrary",)),
        name="adaln_mod",
    )(c_all, w_ada, b_ada)


IN_SPLIT = (DN_CONV_CH, DN_V, BA_PAD, SWA_Q, SWA_KV, SWA_KV)
IN_COLS_PAD = sum(IN_SPLIT)


IN_SLAB = 2 * DN_DK


def _in_kernel(*refs, nb, lt, carry):
    if carry:
        (x_ref, mod_ref, nw_ref, w_ref, cw_ref,
         qkv_ref, z_ref, ba_ref, sq_ref, sk_ref, sv_ref, tail_ref, h_scr) = refs
        prev_ref = tail_ref

        @pl.when(pl.program_id(1) == 0)
        def _():
            tail_ref[...] = jnp.zeros_like(tail_ref)
    else:
        (x_ref, mod_ref, nw_ref, w_ref, cw_ref, prev_ref,
         qkv_ref, z_ref, ba_ref, sq_ref, sk_ref, sv_ref, tail_ref, h_scr) = refs
    rows = nb * lt
    x = x_ref[...]
    ms = jnp.mean(x * x, axis=-1, keepdims=True)
    y = x * lax.rsqrt(ms + EPS) * nw_ref[...]
    h = y * (1.0 + mod_ref[:, 1:2, :]) + mod_ref[:, 0:1, :]
    h_scr[...] = _bf(h.reshape(rows, D_MODEL))

    def proj(lo, n):
        return jnp.dot(h_scr[...], w_ref[:, lo:lo + n], preferred_element_type=F32)

    z_lo = DN_CONV_CH
    ba_lo = z_lo + DN_V
    sq_lo = ba_lo + BA_PAD
    kv_lo = sq_lo + SWA_Q

    def plain_proj(i):
        if i < 2:
            z_ref[:, i * IN_SLAB:(i + 1) * IN_SLAB] = proj(z_lo + i * IN_SLAB, IN_SLAB)
        elif i < 4:
            sq_ref[:, (i - 2) * IN_SLAB:(i - 1) * IN_SLAB] = proj(sq_lo + (i - 2) * IN_SLAB, IN_SLAB)
        elif i == 4:
            kv = proj(kv_lo, 2 * SWA_KV)
            sk_ref[...] = kv[:, :SWA_KV]
            sv_ref[...] = kv[:, SWA_KV:]
        else:
            ba_ref[...] = proj(ba_lo, BA_PAD)

    n_slabs = DN_CONV_CH // IN_SLAB
    raw_next = proj(0, IN_SLAB)
    for slab in range(n_slabs):
        raw = raw_next
        plain_proj(slab)
        if slab + 1 < n_slabs:
            raw_next = proj((slab + 1) * IN_SLAB, IN_SLAB)
        cols = slice(slab * IN_SLAB, (slab + 1) * IN_SLAB)
        c = _silu(_causal_conv(raw, prev_ref[:, :, cols], cw_ref[:, cols], nb, lt))
        tail_ref[:, :, cols] = _last_rows(raw, nb, lt, DN_CONV - 1)
        if slab * IN_SLAB < 2 * DN_QK:
            scale = DN_DK ** -0.5 if slab * IN_SLAB < DN_QK else 1.0
            c = jnp.concatenate([_l2norm(c[:, i * DN_DK:(i + 1) * DN_DK]) * scale
                                 for i in range(IN_SLAB // DN_DK)], axis=1)
        qkv_ref[:, cols] = c


def _in_call(x, mod3, norm_w, w_cat, conv_w, nb, lt, state=None):
    carry = state is None
    bsz, seq, _ = x.shape
    nt = seq // lt
    rows = nb * lt
    n_tok = bsz * seq
    n_state = _state_rows(nb, lt, DN_CONV - 1)
    row_map = lambda i, j: (i * nt + j, 0)
    const = lambda i, j: (0, 0)
    per_seq = lambda i, j: (i, 0, 0)
    in_specs = [pl.BlockSpec((nb, lt, D_MODEL), lambda i, j: (i, j, 0)),
                pl.BlockSpec((nb, 6, D_MODEL), per_seq),
                pl.BlockSpec((1, D_MODEL), const),
                pl.BlockSpec((D_MODEL, IN_COLS_PAD), const, pipeline_mode=RESIDENT),
                pl.BlockSpec((DN_CONV, DN_CONV_CH), const)]
    args = [x, mod3, norm_w, w_cat, conv_w]
    if not carry:
        in_specs.append(pl.BlockSpec((nb, DN_CONV - 1, DN_CONV_CH), per_seq))
        args.append(state)
    return pl.pallas_call(
        functools.partial(_in_kernel, nb=nb, lt=lt, carry=carry),
        grid=(bsz // nb, nt),
        in_specs=in_specs,
        out_specs=[pl.BlockSpec((rows, n), row_map) for n in IN_SPLIT]
        + [pl.BlockSpec((nb, n_state, DN_CONV_CH), per_seq)],
        out_shape=[jax.ShapeDtypeStruct((n_tok, n), F32) for n in IN_SPLIT]
        + [jax.ShapeDtypeStruct((bsz, n_state, DN_CONV_CH), F32)],
        scratch_shapes=[pltpu.VMEM((rows, D_MODEL), BF16)],
        compiler_params=_cparams(("arbitrary", "arbitrary")),
        name="norm_in_proj_carry" if carry else "norm_in_proj_state",
    )(*args)


INV_BASE_SHIFT = 2
DN_UNITS_CARRY = 4
SWA_BLOCKS_PER_STEP = 4
DN_UNITS_STATE = 1


def _unit_lower_inverses(a_mats, ri, ci, chunk_shift):
    def blocks(s):
        return (ri >> s) == (ci >> s)

    base = min(INV_BASE_SHIFT, chunk_shift)
    xs = [jnp.where(blocks(base), -a, 0.0) for a in a_mats]
    ts = [jnp.where(ri == ci, 1.0, n) for n in xs]
    for lvl in range(1, base):
        xs = [_dot(x, x) for x in xs]
        ts = [t + _dot(t, x) for t, x in zip(ts, xs)]
    for s in range(base, chunk_shift):
        sel = blocks(s + 1) & jnp.logical_not(blocks(s))
        ets = [_dot(jnp.where(sel, a, 0.0), t) for a, t in zip(a_mats, ts)]
        ts = [t - _dot(t, et) for t, et in zip(ts, ets)]
    return ts


def _dn_kernel(*refs, nb, chunk, carry, units):
    if carry:
        (qkv_ref, z_ref, ba_ref, alog_ref, dt_ref, nw_ref, o_ref, s_ref) = refs
        s0_ref = s_ref

        @pl.when(pl.program_id(1) == 0)
        def _():
            s_ref[...] = jnp.zeros_like(s_ref)
    else:
        (qkv_ref, z_ref, ba_ref, alog_ref, dt_ref, nw_ref, s0_ref, o_ref, s_ref) = refs

    ri = lax.broadcasted_iota(jnp.int32, (UNIT, UNIT), 0)
    ci = lax.broadcasted_iota(jnp.int32, (UNIT, UNIT), 1)
    shift = int(math.log2(chunk))
    same = (ri >> shift) == (ci >> shift)
    incl = same & (ri >= ci)
    strict = same & (ri > ci)

    ba = ba_ref[...]
    beta_full = _sigmoid(ba)
    g_full = -jnp.exp(alog_ref[...]) * _softplus(ba + dt_ref[...])
    masks = jnp.concatenate([jnp.where(incl, 1.0, 0.0), jnp.where(same, 1.0, 0.0)], axis=0).astype(BF16)
    gsums = [_dot_mask(masks, g_full[u * UNIT:(u + 1) * UNIT]) for u in range(units)]
    g_cum = [g[:UNIT] for g in gsums]
    g_tot = [g[UNIT:] for g in gsums]
    g_cum_t = [g.T for g in g_cum]

    probs = [(u, h) for u in range(units) for h in range(DN_HEADS)]
    n_p = len(probs)
    rs = lambda u: slice(u * UNIT, (u + 1) * UNIT)

    def head_cols(base):
        return [qkv_ref[rs(u), base + h * DN_DK:base + (h + 1) * DN_DK] for u, h in probs]

    q = head_cols(0)
    k = head_cols(DN_QK)
    v = head_cols(2 * DN_QK)
    gc = [g_cum[u][:, DN_HEADS + h:DN_HEADS + h + 1] for u, h in probs]
    gr = [g_cum_t[u][DN_HEADS + h:DN_HEADS + h + 1, :] for u, h in probs]
    gt = [g_tot[u][:, DN_HEADS + h:DN_HEADS + h + 1] for u, h in probs]
    bc = [beta_full[rs(u), h:h + 1] for u, h in probs]
    decay = [jnp.where(incl, jnp.exp(jnp.where(incl, gc[p] - gr[p], 0.0)), 0.0) for p in range(n_p)]
    e_g = [jnp.exp(gc[p]) for p in range(n_p)]
    kq = [_dot_nt(jnp.concatenate([k[p], q[p]], axis=0), k[p]) for p in range(n_p)]
    qk = [kq[p][UNIT:] * decay[p] for p in range(n_p)]
    a_mats = [jnp.where(strict, bc[p] * kq[p][:UNIT] * decay[p], 0.0) for p in range(n_p)]
    t_inv = _unit_lower_inverses(a_mats, ri, ci, shift)
    wvk = [_dot(t_inv[p], jnp.concatenate([v[p] * bc[p], k[p] * (bc[p] * e_g[p])], axis=1)) for p in range(n_p)]
    w_v = [w[:, :DN_DV] for w in wvk]
    w_k = [w[:, DN_DV:] for w in wvk]
    q_dec = [q[p] * e_g[p] for p in range(n_p)]
    k_tail_t = [(k[p] * jnp.exp(gt[p] - gc[p])).T for p in range(n_p)]
    c_dec = [jnp.exp(gt[p]) for p in range(n_p)]

    outs = {}
    if carry:
        state = [s_ref[0, h] for h in range(DN_HEADS)]
        for u in range(units):
            ps = [u * DN_HEADS + h for h in range(DN_HEADS)]
            r = [_dot(jnp.concatenate([w_k[p], q_dec[p]], axis=0), state[h]) for h, p in enumerate(ps)]
            uu = [w_v[p] - r[h][:UNIT] for h, p in enumerate(ps)]
            for h, p in enumerate(ps):
                outs[p] = r[h][UNIT:] + _dot(qk[p], uu[h])
            state = [state[h] * c_dec[p][0:1, :] + _dot(k_tail_t[p], uu[h]) for h, p in enumerate(ps)]
        for h in range(DN_HEADS):
            s_ref[0, h] = state[h]
    else:
        per_unit = nb // units
        for p, (u, h) in enumerate(probs):
            us, qs = [], []
            for s in range(per_unit):
                lo = s * chunk
                lhs = jnp.concatenate([w_k[p][lo:lo + chunk], q_dec[p][lo:lo + chunk]], axis=0)
                r = jnp.dot(lhs, s0_ref[u * per_unit + s, h], preferred_element_type=F32)
                us.append(w_v[p][lo:lo + chunk] - r[:chunk])
                qs.append(r[chunk:])
            uu = jnp.concatenate(us, axis=0)
            outs[p] = jnp.concatenate(qs, axis=0) + _dot(qk[p], uu)
            for s in range(per_unit):
                lo = s * chunk
                upd = jnp.dot(k_tail_t[p][:, lo:lo + chunk], uu[lo:lo + chunk], preferred_element_type=F32)
                s_ref[u * per_unit + s, h] = s0_ref[u * per_unit + s, h] * c_dec[p][lo:lo + 1, :] + upd

    for p, (u, h) in enumerate(probs):
        zz = z_ref[rs(u), h * DN_DV:(h + 1) * DN_DV]
        o_ref[rs(u), h * DN_DV:(h + 1) * DN_DV] = _rms(outs[p], nw_ref[...]) * _silu(zz)


def _dn_call(qkv, z, ba, alog_row, dt_row, norm_w, bsz, seq, state=None):
    carry = state is None
    units = DN_UNITS_CARRY if carry else DN_UNITS_STATE
    rows = units * UNIT
    nb = 1 if carry else rows // seq
    chunk = UNIT if carry else seq
    nt = seq * nb // rows
    n_tok = bsz * seq
    row_map = lambda i, j: (i * nt + j, 0)
    const = lambda i, j: (0, 0)
    in_specs = [pl.BlockSpec((rows, DN_CONV_CH), row_map),
                pl.BlockSpec((rows, DN_V), row_map),
                pl.BlockSpec((rows, BA_PAD), row_map),
                pl.BlockSpec((1, BA_PAD), const),
                pl.BlockSpec((1, BA_PAD), const),
                pl.BlockSpec((1, DN_DV), const)]
    args = [qkv, z, ba, alog_row, dt_row, norm_w]
    if not carry:
        in_specs.append(pl.BlockSpec((nb, DN_HEADS, DN_DK, DN_DV), lambda i, j: (i, 0, 0, 0)))
        args.append(state)
    return pl.pallas_call(
        functools.partial(_dn_kernel, nb=nb, chunk=chunk, carry=carry, units=units),
        grid=(bsz // nb, nt),
        in_specs=in_specs,
        out_specs=[pl.BlockSpec((rows, DN_V), row_map),
                   pl.BlockSpec((nb, DN_HEADS, DN_DK, DN_DV), lambda i, j: (i, 0, 0, 0))],
        out_shape=[jax.ShapeDtypeStruct((n_tok, DN_V), F32),
                   jax.ShapeDtypeStruct((bsz, DN_HEADS, DN_DK, DN_DV), F32)],
        compiler_params=_cparams(("arbitrary", "arbitrary")),
        name="gated_deltanet_carry" if carry else "gated_deltanet_state",
    )(*args)


def _bucket_table():
    i = np.arange(WINDOW, dtype=np.int64)[:, None]
    j = np.arange(2 * WINDOW, dtype=np.int64)[None, :]
    d = np.maximum(i + WINDOW - j, 0)
    exact = N_BUCKETS // 2
    logv = (np.log(np.maximum(d, 1).astype(np.float32) / np.float32(exact)).astype(np.float32)
            / np.float32(math.log(MAX_DISTANCE / exact)))
    large = np.minimum(exact + (logv * np.float32(N_BUCKETS - exact)).astype(np.int32), N_BUCKETS - 1)
    return np.where(d < exact, d, large).astype(np.int32)


def _bias_lookup(rb_ref, bucket, head):
    acc = jnp.zeros(bucket.shape, F32)
    for b in range(N_BUCKETS):
        acc = jnp.where(bucket == b, rb_ref[b, head], acc)
    return acc


def _bias_kernel(rb_ref, bucket_t_ref, bucket_ref, ot_ref, os_ref):
    bucket_t = bucket_t_ref[...]
    kj = lax.broadcasted_iota(jnp.int32, bucket_t.shape, 0)
    qi = lax.broadcasted_iota(jnp.int32, bucket_t.shape, 1)
    dist = qi + WINDOW - kj
    valid = (dist >= 0) & (dist < WINDOW)
    for hk in range(SWA_KV_HEADS):
        for par in range(2):
            for st in range(SWA_GROUP // 2):
                head = hk * SWA_GROUP + 2 * st + par
                gen = jnp.where(valid, _bias_lookup(rb_ref, bucket_t, head), NEG_INF)
                ot_ref[1, hk, par, :, st * WINDOW:(st + 1) * WINDOW] = gen
                ot_ref[0, hk, par, :, st * WINDOW:(st + 1) * WINDOW] = jnp.where(kj >= WINDOW, gen, NEG_INF)
    bucket = bucket_ref[...]
    qi = lax.broadcasted_iota(jnp.int32, bucket.shape, 0)
    kj = lax.broadcasted_iota(jnp.int32, bucket.shape, 1)
    dist = qi + WINDOW - kj
    valid = (dist >= 0) & (dist < WINDOW)
    for head in range(SWA_HEADS):
        os_ref[head] = jnp.where(valid, _bias_lookup(rb_ref, bucket, head), NEG_INF)


BIAS_T_SHAPE = (2, SWA_KV_HEADS, 2, 2 * WINDOW, (SWA_GROUP // 2) * WINDOW)
BIAS_S_SHAPE = (SWA_HEADS, SUBLANES, 2 * WINDOW)


def _bias_call(rel_bias):
    bucket = _bucket_table()
    return pl.pallas_call(
        _bias_kernel,
        in_specs=[pl.BlockSpec(memory_space=pltpu.SMEM),
                  pl.BlockSpec((2 * WINDOW, WINDOW), lambda: (0, 0)),
                  pl.BlockSpec((SUBLANES, 2 * WINDOW), lambda: (0, 0))],
        out_specs=[pl.BlockSpec(BIAS_T_SHAPE, lambda: (0,) * len(BIAS_T_SHAPE)),
                   pl.BlockSpec(BIAS_S_SHAPE, lambda: (0,) * len(BIAS_S_SHAPE))],
        out_shape=[jax.ShapeDtypeStruct(BIAS_T_SHAPE, F32), jax.ShapeDtypeStruct(BIAS_S_SHAPE, F32)],
        name="swa_rel_bias_table",
    )(rel_bias, jnp.asarray(np.ascontiguousarray(bucket.T)), jnp.asarray(bucket[:SUBLANES]))


def _softmax_sink_parts(s, sink):
    m = jnp.maximum(jnp.max(s, axis=0, keepdims=True), sink)
    p = jnp.exp(s - m)
    return p, 1.0 / (jnp.sum(p, axis=0, keepdims=True) + jnp.exp(sink - m))


def _half_lane_variants(full, hk, lo_half):
    rolled = pltpu.roll(full, SWA_HD, axis=1)
    low_src, high_src = (full, rolled) if hk == 0 else (rolled, full)
    return jnp.where(lo_half, low_src, 0.0), jnp.where(lo_half, 0.0, high_src)


def _swa_prompt_kernel(sink_ref, q_ref, kp_ref, kc_ref, vp_ref, vc_ref, bias_ref, o_ref, *, n_blk):
    step = pl.program_id(1)
    lo_half = lax.broadcasted_iota(jnp.int32, (1, 2 * SWA_HD), 1) < SWA_HD
    lo_rows = lax.broadcasted_iota(jnp.int32, (2 * SWA_HD, 1), 0) < SWA_HD
    q = _bf(q_ref[...] * (SWA_HD ** -0.5))
    keys = jnp.concatenate([kp_ref[...], kc_ref[...]], axis=0)
    vals = jnp.concatenate([vp_ref[...], vc_ref[...]], axis=0)
    k_var = [[_bf(t) for t in _half_lane_variants(keys, hk, lo_half)] for hk in range(SWA_KV_HEADS)]
    v_var_t = [[_bf(t.T) for t in _half_lane_variants(vals, hk, lo_half)] for hk in range(SWA_KV_HEADS)]
    n_stack = SWA_GROUP // 2
    sinks = [[jnp.concatenate([jnp.full((1, WINDOW), sink_ref[hk * SWA_GROUP + 2 * st + par], F32)
                               for st in range(n_stack)], axis=1) for par in range(2)]
             for hk in range(SWA_KV_HEADS)]
    nt_dims = (((1,), (1,)), ((), ()))

    def scores(b):
        rows = slice(b * WINDOW, (b + 1) * WINDOW)
        win = slice(b * WINDOW, (b + 2) * WINDOW)
        variant = jnp.where(step == 0, 0, 1) if b == 0 else 1
        out = []
        for hk in range(SWA_KV_HEADS):
            q2 = jnp.concatenate([q[rows, (hk * n_stack + st) * 2 * SWA_HD:(hk * n_stack + st + 1) * 2 * SWA_HD]
                                  for st in range(n_stack)], axis=0)
            out.append([lax.dot_general(k_var[hk][par][win], q2, nt_dims, preferred_element_type=F32)
                        + bias_ref[variant, hk, par] for par in range(2)])
        return out

    s_next = scores(0)
    for b in range(n_blk):
        s_cur = s_next
        if b + 1 < n_blk:
            s_next = scores(b + 1)
        rows = slice(b * WINDOW, (b + 1) * WINDOW)
        win = slice(b * WINDOW, (b + 2) * WINDOW)
        for hk in range(SWA_KV_HEADS):
            parts = [_softmax_sink_parts(s_cur[hk][par], sinks[hk][par]) for par in range(2)]
            o_t = (jnp.dot(v_var_t[hk][0][:, win], _bf(parts[0][0]), preferred_element_type=F32)
                   + jnp.dot(v_var_t[hk][1][:, win], _bf(parts[1][0]), preferred_element_type=F32))
            o_t = o_t * jnp.where(lo_rows, parts[0][1], parts[1][1])
            for st in range(n_stack):
                lo = (hk * n_stack + st) * 2 * SWA_HD
                o_ref[rows, lo:lo + 2 * SWA_HD] = o_t[:, st * WINDOW:(st + 1) * WINDOW].T


def _swa_prompt_call(sinks, sq, sk, sv, bias, bsz, seq):
    n_blk = SWA_BLOCKS_PER_STEP
    tile = n_blk * WINDOW
    nt = seq // tile
    cur = lambda b, i: (b * nt + i, 0)
    prv = lambda b, i: (b * nt * n_blk + jnp.maximum(i * n_blk - 1, 0), 0)
    return pl.pallas_call(
        functools.partial(_swa_prompt_kernel, n_blk=n_blk),
        grid=(bsz, nt),
        in_specs=[pl.BlockSpec(memory_space=pltpu.SMEM),
                  pl.BlockSpec((tile, SWA_Q), cur),
                  pl.BlockSpec((WINDOW, SWA_KV), prv),
                  pl.BlockSpec((tile, SWA_KV), cur),
                  pl.BlockSpec((WINDOW, SWA_KV), prv),
                  pl.BlockSpec((tile, SWA_KV), cur),
                  pl.BlockSpec(BIAS_T_SHAPE, lambda b, i: (0,) * len(BIAS_T_SHAPE))],
        out_specs=pl.BlockSpec((tile, SWA_Q), cur),
        out_shape=jax.ShapeDtypeStruct((bsz * seq, SWA_Q), F32),
        compiler_params=_cparams(("arbitrary", "arbitrary")),
        name="swa_banded",
    )(sinks, sq, sk, sk, sv, sv, bias)


def _swa_sample_kernel(sink_ref, q_ref, kn_ref, vn_ref, kc_ref, vc_ref, bias_ref, o_ref, ko_ref, vo_ref, *, nb, lt):
    lo_half = lax.broadcasted_iota(jnp.int32, (1, 2 * SWA_HD), 1) < SWA_HD
    q_all = q_ref[...] * (SWA_HD ** -0.5)
    kn_all = kn_ref[...]
    vn_all = vn_ref[...]
    pad = jnp.zeros((WINDOW - lt, SWA_KV), F32)
    bias = jnp.concatenate([bias_ref[h, 0:lt, :] for h in range(SWA_HEADS)], axis=0)
    sink = jnp.concatenate([jnp.full((lt, 1), sink_ref[h], F32) for h in range(SWA_HEADS)], axis=0)
    nt_dims = (((1,), (1,)), ((), ()))

    def to_half(tile, src_half, dst_half):
        return tile if src_half == dst_half else pltpu.roll(tile, SWA_HD, axis=1)

    lhs, keys, vals = [], [], []
    for s in range(nb):
        rows = slice(s * lt, (s + 1) * lt)
        kc, vc, kn, vn = kc_ref[s], vc_ref[s], kn_all[rows], vn_all[rows]
        ko_ref[s] = jnp.concatenate([kc[lt:], kn], axis=0)
        vo_ref[s] = jnp.concatenate([vc[lt:], vn], axis=0)
        keys.append(_bf(jnp.concatenate([kc, kn, pad], axis=0)))
        vals.append(_bf(jnp.concatenate([vc, vn, pad], axis=0)))
        q = q_all[rows]
        tiles = []
        for h in range(SWA_HEADS):
            hk = h // SWA_GROUP
            t = to_half(q[:, (h // 2) * 2 * SWA_HD:(h // 2 + 1) * 2 * SWA_HD], h % 2, hk)
            tiles.append(jnp.where(lo_half if hk == 0 else jnp.logical_not(lo_half), t, 0.0))
        lhs.append(_bf(jnp.concatenate(tiles, axis=0)))
    scores = [lax.dot_general(lhs[s], keys[s], nt_dims, preferred_element_type=F32) + bias for s in range(nb)]
    m = [jnp.maximum(jnp.max(sc, axis=-1, keepdims=True), sink) for sc in scores]
    p = [jnp.exp(sc - mm) for sc, mm in zip(scores, m)]
    rinv = [1.0 / (jnp.sum(pp, axis=-1, keepdims=True) + jnp.exp(sink - mm)) for pp, mm in zip(p, m)]
    res = [jnp.dot(_bf(p[s]), vals[s], preferred_element_type=F32) * rinv[s] for s in range(nb)]
    rows_out = []
    for s in range(nb):
        tiles = []
        for pair in range(SWA_HEADS // 2):
            hk = (2 * pair) // SWA_GROUP
            low = to_half(res[s][(2 * pair) * lt:(2 * pair + 1) * lt], hk, 0)
            high = to_half(res[s][(2 * pair + 1) * lt:(2 * pair + 2) * lt], hk, 1)
            tiles.append(jnp.where(lo_half, low, high))
        rows_out.append(jnp.concatenate(tiles, axis=1))
    o_ref[...] = jnp.concatenate(rows_out, axis=0)


def _swa_sample_call(sinks, sq, sk, sv, cache_k, cache_v, bias, bsz, seq):
    nb = UNIT // seq
    rows = lambda i: (i, 0)
    seqs = lambda i: (i, 0, 0)
    return pl.pallas_call(
        functools.partial(_swa_sample_kernel, nb=nb, lt=seq),
        grid=(bsz // nb,),
        in_specs=[pl.BlockSpec(memory_space=pltpu.SMEM),
                  pl.BlockSpec((UNIT, SWA_Q), rows),
                  pl.BlockSpec((UNIT, SWA_KV), rows),
                  pl.BlockSpec((UNIT, SWA_KV), rows),
                  pl.BlockSpec((nb, WINDOW, SWA_KV), seqs),
                  pl.BlockSpec((nb, WINDOW, SWA_KV), seqs),
                  pl.BlockSpec(BIAS_S_SHAPE, lambda i: (0,) * len(BIAS_S_SHAPE))],
        out_specs=[pl.BlockSpec((UNIT, SWA_Q), rows),
                   pl.BlockSpec((nb, WINDOW, SWA_KV), seqs),
                   pl.BlockSpec((nb, WINDOW, SWA_KV), seqs)],
        out_shape=[jax.ShapeDtypeStruct((bsz * seq, SWA_Q), F32),
                   jax.ShapeDtypeStruct((bsz, WINDOW, SWA_KV), F32),
                   jax.ShapeDtypeStruct((bsz, WINDOW, SWA_KV), F32)],
        compiler_params=_cparams(("arbitrary",)),
        name="swa_cached",
    )(sinks, sq, sk, sv, cache_k, cache_v, bias)


def _post_kernel(*refs, nb, lt, carry):
    if carry:
        (odn_ref, oswa_ref, x_ref, mod_ref, wout_ref, nfw_ref, wup_ref, cw_ref, cb_ref, wdn_ref, fnw_ref,
         y_ref, fbuf_ref, h_scr, x1_scr, acc_scr) = refs
        prev_ref = fbuf_ref

        @pl.when(pl.program_id(1) == 0)
        def _():
            fbuf_ref[...] = jnp.zeros_like(fbuf_ref)
    else:
        (odn_ref, oswa_ref, x_ref, mod_ref, wout_ref, nfw_ref, wup_ref, cw_ref, cb_ref, wdn_ref, fnw_ref,
         prev_ref, y_ref, fbuf_ref, h_scr, x1_scr, acc_scr) = refs
    rows = nb * lt
    attn = (jnp.dot(_bf(odn_ref[...]), wout_ref[0:DN_V, :], preferred_element_type=F32)
            + jnp.dot(_bf(oswa_ref[...]), wout_ref[DN_V:, :], preferred_element_type=F32))
    x1 = x_ref[...].reshape(rows, D_MODEL) + _rows(mod_ref[:, 2:3, :], nb, lt) * attn
    x1_scr[...] = x1
    h = _rms(x1, nfw_ref[...]) * (1.0 + _rows(mod_ref[:, 4:5, :], nb, lt)) + _rows(mod_ref[:, 3:4, :], nb, lt)
    h_scr[...] = _bf(h)
    acc_scr[...] = jnp.zeros_like(acc_scr)
    n_chunks = D_FF // FFN_CHUNK

    def col_slices(c):
        return [slice(base + c * FFN_CHUNK, base + (c + 1) * FFN_CHUNK) for base in (0, D_FF)]

    def up_proj(c):
        return [jnp.dot(h_scr[...], wup_ref[:, cols], preferred_element_type=F32) for cols in col_slices(c)]

    u_next = up_proj(0)
    for c in range(n_chunks):
        u_cur = u_next
        if c + 1 < n_chunks:
            u_next = up_proj(c + 1)
        halves = []
        for u, cols in zip(u_cur, col_slices(c)):
            prev = prev_ref[:, :, cols]
            halves.append(_causal_conv(u, prev, cw_ref[:, cols], nb, lt) + cb_ref[:, cols])
            fbuf_ref[:, :, cols] = _last_rows(u, nb, lt, FFN_CONV - 1)
        act = _silu(halves[0]) * halves[1]
        acc_scr[...] += jnp.dot(_bf(act), wdn_ref[c * FFN_CHUNK:(c + 1) * FFN_CHUNK, :],
                                preferred_element_type=F32)
    x2 = x1_scr[...] + _rows(mod_ref[:, 5:6, :], nb, lt) * acc_scr[...]
    y_ref[...] = _rms(x2, fnw_ref[...]).reshape(nb, lt, D_MODEL)


def _post_call(o_dn, o_swa, x, mod3, w_out, norm_ffn_w, w_up, conv_w, conv_b, w_down, final_w, nb, lt, state=None):
    carry = state is None
    bsz, seq, _ = x.shape
    nt = seq // lt
    rows = nb * lt
    n_state = _state_rows(nb, lt, FFN_CONV - 1)
    row_map = lambda i, j: (i * nt + j, 0)
    const = lambda i, j: (0, 0)
    in_specs = [pl.BlockSpec((rows, DN_V), row_map),
                pl.BlockSpec((rows, SWA_Q), row_map),
                pl.BlockSpec((nb, lt, D_MODEL), lambda i, j: (i, j, 0)),
                pl.BlockSpec((nb, 6, D_MODEL), lambda i, j: (i, 0, 0)),
                pl.BlockSpec((D_MODEL, D_MODEL), const, pipeline_mode=RESIDENT),
                pl.BlockSpec((1, D_MODEL), const),
                pl.BlockSpec((D_MODEL, 2 * D_FF), const, pipeline_mode=RESIDENT),
                pl.BlockSpec((FFN_CONV, 2 * D_FF), const),
                pl.BlockSpec((1, 2 * D_FF), const),
                pl.BlockSpec((D_FF, D_MODEL), const, pipeline_mode=RESIDENT),
                pl.BlockSpec((1, D_MODEL), const)]
    args = [o_dn, o_swa, x, mod3, w_out, norm_ffn_w, w_up, conv_w, conv_b, w_down, final_w]
    if not carry:
        in_specs.append(pl.BlockSpec((nb, FFN_CONV - 1, 2 * D_FF), lambda i, j: (i, 0, 0)))
        args.append(state)
    return pl.pallas_call(
        functools.partial(_post_kernel, nb=nb, lt=lt, carry=carry),
        grid=(bsz // nb, nt),
        in_specs=in_specs,
        out_specs=[pl.BlockSpec((nb, lt, D_MODEL), lambda i, j: (i, j, 0)),
                   pl.BlockSpec((nb, n_state, 2 * D_FF), lambda i, j: (i, 0, 0))],
        out_shape=[jax.ShapeDtypeStruct((bsz, seq, D_MODEL), F32),
                   jax.ShapeDtypeStruct((bsz, n_state, 2 * D_FF), F32)],
        scratch_shapes=[pltpu.VMEM((rows, D_MODEL), BF16),
                        pltpu.VMEM((rows, D_MODEL), F32),
                        pltpu.VMEM((rows, D_MODEL), F32)],
        compiler_params=_cparams(("arbitrary", "arbitrary")),
        name="out_proj_convffn_carry" if carry else "out_proj_convffn_state",
    )(*args)


def _pad_row(vec, offset):
    return jnp.zeros((1, BA_PAD), F32).at[0, offset:offset + vec.shape[0]].set(vec)


def kernel(x_prompt, x_sample, state_dn_conv, state_dn_ssm, cache_swa_k, cache_swa_v, state_ffn_conv, c_prompt, c_sample, rel_bias, final_norm_w, w_ada, b_ada, norm_mix_w, w_in, dn_conv_w, dn_A_log, dn_dt_bias, dn_norm_w, swa_sinks, w_out, norm_ffn_w, ffn_w_up, ffn_conv_w, ffn_conv_b, ffn_w_down):
    bp, lp, _ = x_prompt.shape
    bs, ls, _ = x_sample.shape
    layer = 0

    w_i = w_in[layer]
    n_ba = 2 * DN_HEADS
    ba_lo = DN_CONV_CH + DN_V
    w_cat = jnp.concatenate(
        [w_i[:, :ba_lo], jnp.pad(w_i[:, ba_lo:ba_lo + n_ba], ((0, 0), (0, BA_PAD - n_ba))), w_i[:, ba_lo + n_ba:]],
        axis=1).astype(BF16)
    w_out_b = w_out[layer].astype(BF16)
    w_up_b = ffn_w_up[layer].astype(BF16)
    w_dn_b = ffn_w_down[layer].astype(BF16)
    alog_row = _pad_row(dn_A_log[layer], DN_HEADS)
    dt_row = _pad_row(dn_dt_bias[layer], DN_HEADS)
    row = lambda v: v.reshape(1, -1)

    n_c = bp + bs
    n_c_pad = -(-n_c // 8) * 8
    c_all = jnp.pad(jnp.concatenate([c_prompt, c_sample], axis=0), ((0, n_c_pad - n_c), (0, 0)))
    mod = _mod_call(c_all, w_ada[layer], row(b_ada[layer]))
    mod_p = mod[:bp].reshape(bp, 6, D_MODEL)
    mod_s = mod[bp:n_c].reshape(bs, 6, D_MODEL)

    bias_t, bias_s = _bias_call(rel_bias)
    sinks = swa_sinks[layer]

    def mixer_in(x, mod3, nb, lt, state=None):
        return _in_call(x, mod3, row(norm_mix_w[layer]), w_cat, dn_conv_w[layer], nb, lt, state)

    def post(o_dn, o_swa, x, mod3, nb, lt, state=None):
        return _post_call(o_dn, o_swa, x, mod3, w_out_b, row(norm_ffn_w[layer]), w_up_b, ffn_conv_w[layer],
                          row(ffn_conv_b[layer]), w_dn_b, row(final_norm_w), nb, lt, state)

    dn_args = (alog_row, dt_row, row(dn_norm_w[layer]))

    qkv, z, ba, sq, sk, sv, p_tail = mixer_in(x_prompt, mod_p, 1, 512)
    o_dn, p_ssm = _dn_call(qkv, z, ba, *dn_args, bp, lp)
    o_swa = _swa_prompt_call(sinks, sq, sk, sv, bias_t, bp, lp)
    y_prompt, p_ffn_tail = post(o_dn, o_swa, x_prompt, mod_p, 1, 256)
    p_dn_conv = p_tail[:, SUBLANES - (DN_CONV - 1):]
    p_fbuf = p_ffn_tail[:, SUBLANES - (FFN_CONV - 1):]
    p_swa_k = sk.reshape(bp, lp, SWA_KV_HEADS, SWA_HD)[:, lp - WINDOW:]
    p_swa_v = sv.reshape(bp, lp, SWA_KV_HEADS, SWA_HD)[:, lp - WINDOW:]

    qkv_s, z_s, ba_s, sq_s, sk_s, sv_s, s_dn_conv = mixer_in(x_sample, mod_s, 64, ls, state=state_dn_conv[layer])
    o_dn_s, s_ssm = _dn_call(qkv_s, z_s, ba_s, *dn_args, bs, ls, state=state_dn_ssm[layer])
    o_swa_s, s_k, s_v = _swa_sample_call(sinks, sq_s, sk_s, sv_s,
                                         cache_swa_k[layer].reshape(bs, WINDOW, SWA_KV),
                                         cache_swa_v[layer].reshape(bs, WINDOW, SWA_KV), bias_s, bs, ls)
    y_sample, s_fbuf = post(o_dn_s, o_swa_s, x_sample, mod_s, 32, ls, state=state_ffn_conv[layer])

    return (y_prompt, y_sample, p_dn_conv[None], s_dn_conv[None], p_ssm[None], s_ssm[None],
            p_swa_k[None], s_k.reshape(bs, WINDOW, SWA_KV_HEADS, SWA_HD)[None],
            p_swa_v[None], s_v.reshape(bs, WINDOW, SWA_KV_HEADS, SWA_HD)[None],
            p_fbuf[None], s_fbuf[None])
```

```python
import functools
import math

import numpy as np
import jax
import jax.numpy as jnp
from jax import lax
from jax.experimental import pallas as pl
from jax.experimental.pallas import tpu as pltpu

F32 = jnp.float32
BF16 = jnp.bfloat16

D_MODEL = 1024
PAST_LEN = 16384
DN_HEADS = 4
DN_DK = 128
DN_DV = 128
DN_CONV = 4
SWA_HEADS = 8
SWA_KV_HEADS = 2
SWA_GROUP = SWA_HEADS // SWA_KV_HEADS
SWA_HD = 64
WINDOW = 128
N_BUCKETS = 32
MAX_DISTANCE = 128
D_FF = 2816
FFN_CONV = 3
EPS = 1e-6
NEG_INF = -1e30

DN_QK = DN_HEADS * DN_DK
DN_V = DN_HEADS * DN_DV
DN_CONV_CH = 2 * DN_QK + DN_V
SWA_Q = SWA_HEADS * SWA_HD
SWA_KV = SWA_KV_HEADS * SWA_HD
BA_PAD = 128
SUBLANES = 8
UNIT = 128
FFN_CHUNK = 256
VMEM_LIMIT = 56 * 1024 * 1024
RESIDENT = pl.Buffered(1)


def _cparams(sem):
    return pltpu.CompilerParams(dimension_semantics=sem, vmem_limit_bytes=VMEM_LIMIT)


def _bf(x):
    return x.astype(BF16)


def _dot(a, b):
    return jnp.dot(_bf(a), _bf(b), preferred_element_type=F32)


def _dot_nt(a, b):
    return lax.dot_general(_bf(a), _bf(b), (((1,), (1,)), ((), ())), preferred_element_type=F32)


def _split2(x):
    hi = _bf(x)
    lo = _bf(x - hi.astype(F32))
    return hi, lo


def _dot_x3(a, b):
    ah, al = _split2(a)
    bh, bl = _split2(b)
    d = functools.partial(jnp.dot, preferred_element_type=F32)
    return d(ah, bh) + (d(al, bh) + d(ah, bl))


def _dot_mask(m, x):
    hi = _bf(x)
    r = x - hi.astype(F32)
    mid = _bf(r)
    lo = _bf(r - mid.astype(F32))
    d = functools.partial(jnp.dot, preferred_element_type=F32)
    return d(m, hi) + (d(m, mid) + d(m, lo))


def _sigmoid(x):
    return 1.0 / (1.0 + jnp.exp(-x))


def _silu(x):
    return x * _sigmoid(x)


def _softplus(x):
    return jnp.maximum(x, 0.0) + jnp.log1p(jnp.exp(-jnp.abs(x)))


def _rms(x, w):
    ms = jnp.mean(x * x, axis=-1, keepdims=True)
    return x * lax.rsqrt(ms + EPS) * w


def _l2norm(t):
    return t * lax.rsqrt(jnp.sum(t * t, axis=-1, keepdims=True) + EPS)


def _rows(m3, nb, lt):
    return jnp.broadcast_to(m3, (nb, lt, m3.shape[-1])).reshape(nb * lt, m3.shape[-1])


def _causal_conv(x, prev, w, nb, lt):
    width = w.shape[0]
    rows, ch = x.shape
    if nb == 1 and lt > SUBLANES:
        tiles = jnp.concatenate([prev, x.reshape(lt // SUBLANES, SUBLANES, ch)], axis=0)
        sub = lax.broadcasted_iota(jnp.int32, (1, SUBLANES, 1), 1)
        out = tiles[1:] * w[width - 1:width, :]
        for j in range(1, width):
            rot = pltpu.roll(tiles, j, axis=1)
            out = out + jnp.where(sub >= j, rot[1:], rot[:-1]) * w[width - 1 - j:width - j, :]
        return out.reshape(rows, ch)
    tmod = lax.broadcasted_iota(jnp.int32, (rows, 1), 0) & (lt - 1)
    out = x * w[width - 1:width, :]
    for j in range(1, width):
        sh = pltpu.roll(x, j, axis=0)
        for t in range(j):
            p = width - 1 - j + t
            sh = jnp.where(tmod == t, _rows(prev[:, p:p + 1, :], nb, lt), sh)
        out = out + sh * w[width - 1 - j:width - j, :]
    return out


def _last_rows(x, nb, lt, n_state):
    ch = x.shape[-1]
    if nb == 1 and lt > SUBLANES:
        return x[lt - SUBLANES:].reshape(1, SUBLANES, ch)
    return x.reshape(nb, lt, ch)[:, lt - n_state:, :]


def _state_rows(nb, lt, n_state):
    return SUBLANES if (nb == 1 and lt > SUBLANES) else n_state


MOD_PARTS = 6


def _mod_kernel(c_ref, w_ref, b_ref, os_ref, op_ref, *, n_s, n_p):
    a = _silu(c_ref[...])
    for k in range(MOD_PARTS):
        cols = slice(k * D_MODEL, (k + 1) * D_MODEL)
        res = _dot_x3(a, w_ref[:, cols]) + b_ref[:, cols]
        os_ref[:, k, :] = res[:n_s]
        op_ref[:, k, :] = res[n_s:n_s + n_p]


def _mod_call(c_all, w_ada, b_ada, n_s, n_p):
    rows = c_all.shape[0]
    cols = w_ada.shape[1]
    return pl.pallas_call(
        functools.partial(_mod_kernel, n_s=n_s, n_p=n_p),
        in_specs=[pl.BlockSpec((rows, D_MODEL), lambda: (0, 0)),
                  pl.BlockSpec((D_MODEL, cols), lambda: (0, 0)),
                  pl.BlockSpec((1, cols), lambda: (0, 0))],
        out_specs=[pl.BlockSpec((n_s, MOD_PARTS, D_MODEL), lambda: (0, 0, 0)),
                   pl.BlockSpec((n_p, MOD_PARTS, D_MODEL), lambda: (0, 0, 0))],
        out_shape=[jax.ShapeDtypeStruct((n_s, MOD_PARTS, D_MODEL), F32),
                   jax.ShapeDtypeStruct((n_p, MOD_PARTS, D_MODEL), F32)],
        compiler_params=pltpu.CompilerParams(vmem_limit_bytes=VMEM_LIMIT),
        name="adaln_mod",
    )(c_all, w_ada, b_ada)


IN_SPLIT = (DN_CONV_CH, DN_V, BA_PAD, SWA_Q, SWA_KV, SWA_KV)


W_PREP_ROWS = 256
IN_SLAB = 2 * DN_DK


def _in_kernel(*refs, nb, lt, carry):
    if carry:
        (x_ref, mod_ref, nw_ref, w_ref, cw_ref,
         qkv_ref, z_ref, ba_ref, sq_ref, sk_ref, sv_ref, tail_ref, h_scr, wdn_ref, wba_ref, wsw_ref) = refs
        prev_ref = tail_ref

        @pl.when(pl.program_id(1) == 0)
        def _():
            tail_ref[...] = jnp.zeros_like(tail_ref)
    else:
        (x_ref, mod_ref, nw_ref, w_ref, cw_ref, prev_ref,
         qkv_ref, z_ref, ba_ref, sq_ref, sk_ref, sv_ref, tail_ref, h_scr, wdn_ref, wba_ref, wsw_ref) = refs

    @pl.when((pl.program_id(0) == 0) & (pl.program_id(1) == 0))
    def _():
        ba_lo = DN_CONV_CH + DN_V
        n_ba = 2 * DN_HEADS
        lane = lax.broadcasted_iota(jnp.int32, (1, BA_PAD), 1)
        for r in range(0, D_MODEL, W_PREP_ROWS):
            rr = slice(r, r + W_PREP_ROWS)
            wdn_ref[rr, :] = _bf(w_ref[rr, :ba_lo])
            wba_ref[rr, :] = _bf(jnp.where(lane < n_ba, w_ref[rr, ba_lo:ba_lo + BA_PAD], 0.0))
            wsw_ref[rr, :] = _bf(w_ref[rr, ba_lo + n_ba:])

    rows = nb * lt
    x = x_ref[...]
    ms = jnp.mean(x * x, axis=-1, keepdims=True)
    y = x * lax.rsqrt(ms + EPS) * nw_ref[...]
    h = y * (1.0 + mod_ref[:, 1:2, :]) + mod_ref[:, 0:1, :]
    h_scr[...] = _bf(h.reshape(rows, D_MODEL))

    def proj(w_ref, lo, n):
        return jnp.dot(h_scr[...], w_ref[:, lo:lo + n], preferred_element_type=F32)

    def plain_proj(i):
        if i < 2:
            z_ref[:, i * IN_SLAB:(i + 1) * IN_SLAB] = proj(wdn_ref, DN_CONV_CH + i * IN_SLAB, IN_SLAB)
        elif i < 4:
            sq_ref[:, (i - 2) * IN_SLAB:(i - 1) * IN_SLAB] = proj(wsw_ref, (i - 2) * IN_SLAB, IN_SLAB)
        elif i == 4:
            kv = proj(wsw_ref, SWA_Q, 2 * SWA_KV)
            sk_ref[...] = kv[:, :SWA_KV]
            sv_ref[...] = kv[:, SWA_KV:]
        else:
            ba_ref[...] = proj(wba_ref, 0, BA_PAD)

    n_slabs = DN_CONV_CH // IN_SLAB
    raw_next = proj(wdn_ref, 0, IN_SLAB)
    for slab in range(n_slabs):
        raw = raw_next
        plain_proj(slab)
        if slab + 1 < n_slabs:
            raw_next = proj(wdn_ref, (slab + 1) * IN_SLAB, IN_SLAB)
        cols = slice(slab * IN_SLAB, (slab + 1) * IN_SLAB)
        c = _silu(_causal_conv(raw, prev_ref[:, :, cols], cw_ref[:, cols], nb, lt))
        tail_ref[:, :, cols] = _last_rows(raw, nb, lt, DN_CONV - 1)
        if slab * IN_SLAB < 2 * DN_QK:
            scale = DN_DK ** -0.5 if slab * IN_SLAB < DN_QK else 1.0
            c = jnp.concatenate([_l2norm(c[:, i * DN_DK:(i + 1) * DN_DK]) * scale
                                 for i in range(IN_SLAB // DN_DK)], axis=1)
        qkv_ref[:, cols] = c


def _in_call(x, mod3, norm_w, w_in, conv_w, nb, lt, state=None):
    carry = state is None
    bsz, seq, _ = x.shape
    nt = seq // lt
    rows = nb * lt
    n_tok = bsz * seq
    n_state = _state_rows(nb, lt, DN_CONV - 1)
    row_map = lambda i, j: (i * nt + j, 0)
    const = lambda i, j: (0, 0)
    per_seq = lambda i, j: (i, 0, 0)
    in_specs = [pl.BlockSpec((nb, lt, D_MODEL), lambda i, j: (i, j, 0)),
                pl.BlockSpec((nb, 6, D_MODEL), per_seq),
                pl.BlockSpec((1, D_MODEL), const),
                pl.BlockSpec(w_in.shape, const, pipeline_mode=RESIDENT),
                pl.BlockSpec((DN_CONV, DN_CONV_CH), const)]
    args = [x, mod3, norm_w, w_in, conv_w]
    if not carry:
        in_specs.append(pl.BlockSpec((nb, DN_CONV - 1, DN_CONV_CH), per_seq))
        args.append(state)
    return pl.pallas_call(
        functools.partial(_in_kernel, nb=nb, lt=lt, carry=carry),
        grid=(bsz // nb, nt),
        in_specs=in_specs,
        out_specs=[pl.BlockSpec((rows, n), row_map) for n in IN_SPLIT]
        + [pl.BlockSpec((nb, n_state, DN_CONV_CH), per_seq)],
        out_shape=[jax.ShapeDtypeStruct((n_tok, n), F32) for n in IN_SPLIT]
        + [jax.ShapeDtypeStruct((bsz, n_state, DN_CONV_CH), F32)],
        scratch_shapes=[pltpu.VMEM((rows, D_MODEL), BF16),
                        pltpu.VMEM((D_MODEL, DN_CONV_CH + DN_V), BF16),
                        pltpu.VMEM((D_MODEL, BA_PAD), BF16),
                        pltpu.VMEM((D_MODEL, SWA_Q + 2 * SWA_KV), BF16)],
        compiler_params=_cparams(("arbitrary", "arbitrary")),
        name="norm_in_proj_carry" if carry else "norm_in_proj_state",
    )(*args)


INV_BASE_SHIFT = 2
DN_UNITS_CARRY = 4
SWA_BLOCKS_PER_STEP = 4
DN_UNITS_STATE = 1


def _unit_lower_inverses(a_mats, ri, ci, chunk_shift):
    def blocks(s):
        return (ri >> s) == (ci >> s)

    base = min(INV_BASE_SHIFT, chunk_shift)
    xs = [jnp.where(blocks(base), -a, 0.0) for a in a_mats]
    ts = [jnp.where(ri == ci, 1.0, n) for n in xs]
    for lvl in range(1, base):
        xs = [_dot(x, x) for x in xs]
        ts = [t + _dot(t, x) for t, x in zip(ts, xs)]
    for s in range(base, chunk_shift):
        sel = blocks(s + 1) & jnp.logical_not(blocks(s))
        ets = [_dot(jnp.where(sel, a, 0.0), t) for a, t in zip(a_mats, ts)]
        ts = [t - _dot(t, et) for t, et in zip(ts, ets)]
    return ts


def _dn_kernel(*refs, nb, chunk, carry, units):
    if carry:
        (qkv_ref, z_ref, ba_ref, alog_ref, dt_ref, nw_ref, o_ref, s_ref) = refs
        s0_ref = s_ref

        @pl.when(pl.program_id(1) == 0)
        def _():
            s_ref[...] = jnp.zeros_like(s_ref)
    else:
        (qkv_ref, z_ref, ba_ref, alog_ref, dt_ref, nw_ref, s0_ref, o_ref, s_ref) = refs

    ri = lax.broadcasted_iota(jnp.int32, (UNIT, UNIT), 0)
    ci = lax.broadcasted_iota(jnp.int32, (UNIT, UNIT), 1)
    shift = int(math.log2(chunk))
    same = (ri >> shift) == (ci >> shift)
    incl = same & (ri >= ci)
    strict = same & (ri > ci)

    ba = ba_ref[...]
    beta_full = _sigmoid(ba)
    g_full = -jnp.exp(alog_ref[...]) * _softplus(ba + dt_ref[...])
    masks = jnp.concatenate([jnp.where(incl, 1.0, 0.0), jnp.where(same, 1.0, 0.0)], axis=0).astype(BF16)
    gsums = [_dot_mask(masks, g_full[u * UNIT:(u + 1) * UNIT]) for u in range(units)]
    g_cum = [g[:UNIT] for g in gsums]
    g_tot = [g[UNIT:] for g in gsums]
    g_cum_t = [g.T for g in g_cum]

    probs = [(u, h) for u in range(units) for h in range(DN_HEADS)]
    n_p = len(probs)
    rs = lambda u: slice(u * UNIT, (u + 1) * UNIT)

    def head_cols(base):
        return [qkv_ref[rs(u), base + h * DN_DK:base + (h + 1) * DN_DK] for u, h in probs]

    q = head_cols(0)
    k = head_cols(DN_QK)
    v = head_cols(2 * DN_QK)
    gc = [g_cum[u][:, DN_HEADS + h:DN_HEADS + h + 1] for u, h in probs]
    gr = [g_cum_t[u][DN_HEADS + h:DN_HEADS + h + 1, :] for u, h in probs]
    gt = [g_tot[u][:, DN_HEADS + h:DN_HEADS + h + 1] for u, h in probs]
    bc = [beta_full[rs(u), h:h + 1] for u, h in probs]
    decay = [jnp.where(incl, jnp.exp(jnp.where(incl, gc[p] - gr[p], 0.0)), 0.0) for p in range(n_p)]
    e_g = [jnp.exp(gc[p]) for p in range(n_p)]
    kq = [_dot_nt(jnp.concatenate([k[p], q[p]], axis=0), k[p]) for p in range(n_p)]
    qk = [kq[p][UNIT:] * decay[p] for p in range(n_p)]
    a_mats = [jnp.where(strict, bc[p] * kq[p][:UNIT] * decay[p], 0.0) for p in range(n_p)]
    t_inv = _unit_lower_inverses(a_mats, ri, ci, shift)
    wvk = [_dot(t_inv[p], jnp.concatenate([v[p] * bc[p], k[p] * (bc[p] * e_g[p])], axis=1)) for p in range(n_p)]
    w_v = [w[:, :DN_DV] for w in wvk]
    w_k = [w[:, DN_DV:] for w in wvk]
    q_dec = [q[p] * e_g[p] for p in range(n_p)]
    k_tail_t = [(k[p] * jnp.exp(gt[p] - gc[p])).T for p in range(n_p)]
    c_dec = [jnp.exp(gt[p]) for p in range(n_p)]

    outs = {}
    if carry:
        state = [s_ref[0, h] for h in range(DN_HEADS)]
        for u in range(units):
            ps = [u * DN_HEADS + h for h in range(DN_HEADS)]
            r = [_dot(jnp.concatenate([w_k[p], q_dec[p]], axis=0), state[h]) for h, p in enumerate(ps)]
            uu = [w_v[p] - r[h][:UNIT] for h, p in enumerate(ps)]
            for h, p in enumerate(ps):
                outs[p] = r[h][UNIT:] + _dot(qk[p], uu[h])
            state = [state[h] * c_dec[p][0:1, :] + _dot(k_tail_t[p], uu[h]) for h, p in enumerate(ps)]
        for h in range(DN_HEADS):
            s_ref[0, h] = state[h]
    else:
        per_unit = nb // units
        for p, (u, h) in enumerate(probs):
            us, qs = [], []
            for s in range(per_unit):
                lo = s * chunk
                lhs = jnp.concatenate([w_k[p][lo:lo + chunk], q_dec[p][lo:lo + chunk]], axis=0)
                r = jnp.dot(lhs, s0_ref[u * per_unit + s, h], preferred_element_type=F32)
                us.append(w_v[p][lo:lo + chunk] - r[:chunk])
                qs.append(r[chunk:])
            uu = jnp.concatenate(us, axis=0)
            outs[p] = jnp.concatenate(qs, axis=0) + _dot(qk[p], uu)
            for s in range(per_unit):
                lo = s * chunk
                upd = jnp.dot(k_tail_t[p][:, lo:lo + chunk], uu[lo:lo + chunk], preferred_element_type=F32)
                s_ref[u * per_unit + s, h] = s0_ref[u * per_unit + s, h] * c_dec[p][lo:lo + 1, :] + upd

    for p, (u, h) in enumerate(probs):
        zz = z_ref[rs(u), h * DN_DV:(h + 1) * DN_DV]
        o_ref[rs(u), h * DN_DV:(h + 1) * DN_DV] = _rms(outs[p], nw_ref[...]) * _silu(zz)


def _dn_call(qkv, z, ba, alog_row, dt_row, norm_w, bsz, seq, state=None):
    carry = state is None
    units = DN_UNITS_CARRY if carry else DN_UNITS_STATE
    rows = units * UNIT
    nb = 1 if carry else rows // seq
    chunk = UNIT if carry else seq
    nt = seq * nb // rows
    n_tok = bsz * seq
    row_map = lambda i, j: (i * nt + j, 0)
    const = lambda i, j: (0, 0)
    in_specs = [pl.BlockSpec((rows, DN_CONV_CH), row_map),
                pl.BlockSpec((rows, DN_V), row_map),
                pl.BlockSpec((rows, BA_PAD), row_map),
                pl.BlockSpec((1, BA_PAD), const),
                pl.BlockSpec((1, BA_PAD), const),
                pl.BlockSpec((1, DN_DV), const)]
    args = [qkv, z, ba, alog_row, dt_row, norm_w]
    if not carry:
        in_specs.append(pl.BlockSpec((nb, DN_HEADS, DN_DK, DN_DV), lambda i, j: (i, 0, 0, 0)))
        args.append(state)
    return pl.pallas_call(
        functools.partial(_dn_kernel, nb=nb, chunk=chunk, carry=carry, units=units),
        grid=(bsz // nb, nt),
        in_specs=in_specs,
        out_specs=[pl.BlockSpec((rows, DN_V), row_map),
                   pl.BlockSpec((nb, DN_HEADS, DN_DK, DN_DV), lambda i, j: (i, 0, 0, 0))],
        out_shape=[jax.ShapeDtypeStruct((n_tok, DN_V), F32),
                   jax.ShapeDtypeStruct((bsz, DN_HEADS, DN_DK, DN_DV), F32)],
        compiler_params=_cparams(("arbitrary", "arbitrary")),
        name="gated_deltanet_carry" if carry else "gated_deltanet_state",
    )(*args)


def _bucket_table():
    i = np.arange(WINDOW, dtype=np.int64)[:, None]
    j = np.arange(2 * WINDOW, dtype=np.int64)[None, :]
    d = np.maximum(i + WINDOW - j, 0)
    exact = N_BUCKETS // 2
    logv = (np.log(np.maximum(d, 1).astype(np.float32) / np.float32(exact)).astype(np.float32)
            / np.float32(math.log(MAX_DISTANCE / exact)))
    large = np.minimum(exact + (logv * np.float32(N_BUCKETS - exact)).astype(np.int32), N_BUCKETS - 1)
    return np.where(d < exact, d, large).astype(np.int32)


def _bias_lookup(rb_ref, bucket, head):
    acc = jnp.zeros(bucket.shape, F32)
    for b in range(N_BUCKETS):
        acc = jnp.where(bucket == b, rb_ref[b, head], acc)
    return acc


def _bias_kernel(rb_ref, bucket_t_ref, bucket_ref, ot_ref, os_ref):
    bucket_t = bucket_t_ref[...]
    kj = lax.broadcasted_iota(jnp.int32, bucket_t.shape, 0)
    qi = lax.broadcasted_iota(jnp.int32, bucket_t.shape, 1)
    dist = qi + WINDOW - kj
    valid = (dist >= 0) & (dist < WINDOW)
    for hk in range(SWA_KV_HEADS):
        for par in range(2):
            for st in range(SWA_GROUP // 2):
                head = hk * SWA_GROUP + 2 * st + par
                gen = jnp.where(valid, _bias_lookup(rb_ref, bucket_t, head), NEG_INF)
                ot_ref[1, hk, par, :, st * WINDOW:(st + 1) * WINDOW] = gen
                ot_ref[0, hk, par, :, st * WINDOW:(st + 1) * WINDOW] = jnp.where(kj >= WINDOW, gen, NEG_INF)
    bucket = bucket_ref[...]
    qi = lax.broadcasted_iota(jnp.int32, bucket.shape, 0)
    kj = lax.broadcasted_iota(jnp.int32, bucket.shape, 1)
    dist = qi + WINDOW - kj
    valid = (dist >= 0) & (dist < WINDOW)
    for head in range(SWA_HEADS):
        os_ref[head] = jnp.where(valid, _bias_lookup(rb_ref, bucket, head), NEG_INF)


BIAS_T_SHAPE = (2, SWA_KV_HEADS, 2, 2 * WINDOW, (SWA_GROUP // 2) * WINDOW)
BIAS_S_SHAPE = (SWA_HEADS, SUBLANES, 2 * WINDOW)


def _bias_call(rel_bias):
    bucket = _bucket_table()
    return pl.pallas_call(
        _bias_kernel,
        in_specs=[pl.BlockSpec(memory_space=pltpu.SMEM),
                  pl.BlockSpec((2 * WINDOW, WINDOW), lambda: (0, 0)),
                  pl.BlockSpec((SUBLANES, 2 * WINDOW), lambda: (0, 0))],
        out_specs=[pl.BlockSpec(BIAS_T_SHAPE, lambda: (0,) * len(BIAS_T_SHAPE)),
                   pl.BlockSpec(BIAS_S_SHAPE, lambda: (0,) * len(BIAS_S_SHAPE))],
        out_shape=[jax.ShapeDtypeStruct(BIAS_T_SHAPE, F32), jax.ShapeDtypeStruct(BIAS_S_SHAPE, F32)],
        name="swa_rel_bias_table",
    )(rel_bias, jnp.asarray(np.ascontiguousarray(bucket.T)), jnp.asarray(bucket[:SUBLANES]))


def _softmax_sink_parts(s, sink):
    m = jnp.maximum(jnp.max(s, axis=0, keepdims=True), sink)
    p = jnp.exp(s - m)
    return p, 1.0 / (jnp.sum(p, axis=0, keepdims=True) + jnp.exp(sink - m))


def _half_lane_variants(full, hk, lo_half):
    rolled = pltpu.roll(full, SWA_HD, axis=1)
    low_src, high_src = (full, rolled) if hk == 0 else (rolled, full)
    return jnp.where(lo_half, low_src, 0.0), jnp.where(lo_half, 0.0, high_src)


def _swa_prompt_kernel(sink_ref, q_ref, kp_ref, kc_ref, vp_ref, vc_ref, bias_ref, o_ref, *, n_blk):
    step = pl.program_id(1)
    lo_half = lax.broadcasted_iota(jnp.int32, (1, 2 * SWA_HD), 1) < SWA_HD
    lo_rows = lax.broadcasted_iota(jnp.int32, (2 * SWA_HD, 1), 0) < SWA_HD
    q = _bf(q_ref[...] * (SWA_HD ** -0.5))
    keys = jnp.concatenate([kp_ref[...], kc_ref[...]], axis=0)
    vals = jnp.concatenate([vp_ref[...], vc_ref[...]], axis=0)
    k_var = [[_bf(t) for t in _half_lane_variants(keys, hk, lo_half)] for hk in range(SWA_KV_HEADS)]
    v_var_t = [[_bf(t.T) for t in _half_lane_variants(vals, hk, lo_half)] for hk in range(SWA_KV_HEADS)]
    n_stack = SWA_GROUP // 2
    sinks = [[jnp.concatenate([jnp.full((1, WINDOW), sink_ref[hk * SWA_GROUP + 2 * st + par], F32)
                               for st in range(n_stack)], axis=1) for par in range(2)]
             for hk in range(SWA_KV_HEADS)]
    nt_dims = (((1,), (1,)), ((), ()))

    def scores(b):
        rows = slice(b * WINDOW, (b + 1) * WINDOW)
        win = slice(b * WINDOW, (b + 2) * WINDOW)
        variant = jnp.where(step == 0, 0, 1) if b == 0 else 1
        out = []
        for hk in range(SWA_KV_HEADS):
            q2 = jnp.concatenate([q[rows, (hk * n_stack + st) * 2 * SWA_HD:(hk * n_stack + st + 1) * 2 * SWA_HD]
                                  for st in range(n_stack)], axis=0)
            out.append([lax.dot_general(k_var[hk][par][win], q2, nt_dims, preferred_element_type=F32)
                        + bias_ref[variant, hk, par] for par in range(2)])
        return out

    s_next = scores(0)
    for b in range(n_blk):
        s_cur = s_next
        if b + 1 < n_blk:
            s_next = scores(b + 1)
        rows = slice(b * WINDOW, (b + 1) * WINDOW)
        win = slice(b * WINDOW, (b + 2) * WINDOW)
        for hk in range(SWA_KV_HEADS):
            parts = [_softmax_sink_parts(s_cur[hk][par], sinks[hk][par]) for par in range(2)]
            o_t = (jnp.dot(v_var_t[hk][0][:, win], _bf(parts[0][0]), preferred_element_type=F32)
                   + jnp.dot(v_var_t[hk][1][:, win], _bf(parts[1][0]), preferred_element_type=F32))
            o_t = o_t * jnp.where(lo_rows, parts[0][1], parts[1][1])
            for st in range(n_stack):
                lo = (hk * n_stack + st) * 2 * SWA_HD
                o_ref[rows, lo:lo + 2 * SWA_HD] = o_t[:, st * WINDOW:(st + 1) * WINDOW].T


def _swa_prompt_call(sinks, sq, sk, sv, bias, bsz, seq):
    n_blk = SWA_BLOCKS_PER_STEP
    tile = n_blk * WINDOW
    nt = seq // tile
    cur = lambda b, i: (b * nt + i, 0)
    prv = lambda b, i: (b * nt * n_blk + jnp.maximum(i * n_blk - 1, 0), 0)
    return pl.pallas_call(
        functools.partial(_swa_prompt_kernel, n_blk=n_blk),
        grid=(bsz, nt),
        in_specs=[pl.BlockSpec(memory_space=pltpu.SMEM),
                  pl.BlockSpec((tile, SWA_Q), cur),
                  pl.BlockSpec((WINDOW, SWA_KV), prv),
                  pl.BlockSpec((tile, SWA_KV), cur),
                  pl.BlockSpec((WINDOW, SWA_KV), prv),
                  pl.BlockSpec((tile, SWA_KV), cur),
                  pl.BlockSpec(BIAS_T_SHAPE, lambda b, i: (0,) * len(BIAS_T_SHAPE))],
        out_specs=pl.BlockSpec((tile, SWA_Q), cur),
        out_shape=jax.ShapeDtypeStruct((bsz * seq, SWA_Q), F32),
        compiler_params=_cparams(("arbitrary", "arbitrary")),
        name="swa_banded",
    )(sinks, sq, sk, sk, sv, sv, bias)


def _swa_sample_kernel(sink_ref, q_ref, kn_ref, vn_ref, kc_ref, vc_ref, bias_ref, o_ref, ko_ref, vo_ref, *, nb, lt):
    lo_half = lax.broadcasted_iota(jnp.int32, (1, 2 * SWA_HD), 1) < SWA_HD
    q_all = q_ref[...] * (SWA_HD ** -0.5)
    kn_all = kn_ref[...]
    vn_all = vn_ref[...]
    pad = jnp.zeros((WINDOW - lt, SWA_KV), F32)
    bias = jnp.concatenate([bias_ref[h, 0:lt, :] for h in range(SWA_HEADS)], axis=0)
    sink = jnp.concatenate([jnp.full((lt, 1), sink_ref[h], F32) for h in range(SWA_HEADS)], axis=0)
    nt_dims = (((1,), (1,)), ((), ()))

    def to_half(tile, src_half, dst_half):
        return tile if src_half == dst_half else pltpu.roll(tile, SWA_HD, axis=1)

    lhs, keys, vals = [], [], []
    for s in range(nb):
        rows = slice(s * lt, (s + 1) * lt)
        kc, vc, kn, vn = kc_ref[s], vc_ref[s], kn_all[rows], vn_all[rows]
        ko_ref[s] = jnp.concatenate([kc[lt:], kn], axis=0)
        vo_ref[s] = jnp.concatenate([vc[lt:], vn], axis=0)
        keys.append(_bf(jnp.concatenate([kc, kn, pad], axis=0)))
        vals.append(_bf(jnp.concatenate([vc, vn, pad], axis=0)))
        q = q_all[rows]
        tiles = []
        for h in range(SWA_HEADS):
            hk = h // SWA_GROUP
            t = to_half(q[:, (h // 2) * 2 * SWA_HD:(h // 2 + 1) * 2 * SWA_HD], h % 2, hk)
            tiles.append(jnp.where(lo_half if hk == 0 else jnp.logical_not(lo_half), t, 0.0))
        lhs.append(_bf(jnp.concatenate(tiles, axis=0)))
    scores = [lax.dot_general(lhs[s], keys[s], nt_dims, preferred_element_type=F32) + bias for s in range(nb)]
    m = [jnp.maximum(jnp.max(sc, axis=-1, keepdims=True), sink) for sc in scores]
    p = [jnp.exp(sc - mm) for sc, mm in zip(scores, m)]
    rinv = [1.0 / (jnp.sum(pp, axis=-1, keepdims=True) + jnp.exp(sink - mm)) for pp, mm in zip(p, m)]
    res = [jnp.dot(_bf(p[s]), vals[s], preferred_element_type=F32) * rinv[s] for s in range(nb)]
    rows_out = []
    for s in range(nb):
        tiles = []
        for pair in range(SWA_HEADS // 2):
            hk = (2 * pair) // SWA_GROUP
            low = to_half(res[s][(2 * pair) * lt:(2 * pair + 1) * lt], hk, 0)
            high = to_half(res[s][(2 * pair + 1) * lt:(2 * pair + 2) * lt], hk, 1)
            tiles.append(jnp.where(lo_half, low, high))
        rows_out.append(jnp.concatenate(tiles, axis=1))
    o_ref[...] = jnp.concatenate(rows_out, axis=0)


def _swa_sample_call(sinks, sq, sk, sv, cache_k, cache_v, bias, bsz, seq):
    nb = UNIT // seq
    rows = lambda i: (i, 0)
    seqs = lambda i: (i, 0, 0)
    return pl.pallas_call(
        functools.partial(_swa_sample_kernel, nb=nb, lt=seq),
        grid=(bsz // nb,),
        in_specs=[pl.BlockSpec(memory_space=pltpu.SMEM),
                  pl.BlockSpec((UNIT, SWA_Q), rows),
                  pl.BlockSpec((UNIT, SWA_KV), rows),
                  pl.BlockSpec((UNIT, SWA_KV), rows),
                  pl.BlockSpec((nb, WINDOW, SWA_KV), seqs),
                  pl.BlockSpec((nb, WINDOW, SWA_KV), seqs),
                  pl.BlockSpec(BIAS_S_SHAPE, lambda i: (0,) * len(BIAS_S_SHAPE))],
        out_specs=[pl.BlockSpec((UNIT, SWA_Q), rows),
                   pl.BlockSpec((nb, WINDOW, SWA_KV), seqs),
                   pl.BlockSpec((nb, WINDOW, SWA_KV), seqs)],
        out_shape=[jax.ShapeDtypeStruct((bsz * seq, SWA_Q), F32),
                   jax.ShapeDtypeStruct((bsz, WINDOW, SWA_KV), F32),
                   jax.ShapeDtypeStruct((bsz, WINDOW, SWA_KV), F32)],
        compiler_params=_cparams(("arbitrary",)),
        name="swa_cached",
    )(sinks, sq, sk, sv, cache_k, cache_v, bias)


def _post_kernel(*refs, nb, lt, carry):
    if carry:
        (odn_ref, oswa_ref, x_ref, mod_ref, wout_ref, nfw_ref, wup_ref, cw_ref, cb_ref, wdn_ref, fnw_ref,
         y_ref, fbuf_ref, h_scr, x1_scr, acc_scr) = refs
        prev_ref = fbuf_ref

        @pl.when(pl.program_id(1) == 0)
        def _():
            fbuf_ref[...] = jnp.zeros_like(fbuf_ref)
    else:
        (odn_ref, oswa_ref, x_ref, mod_ref, wout_ref, nfw_ref, wup_ref, cw_ref, cb_ref, wdn_ref, fnw_ref,
         prev_ref, y_ref, fbuf_ref, h_scr, x1_scr, acc_scr) = refs
    rows = nb * lt
    attn = (jnp.dot(_bf(odn_ref[...]), wout_ref[0:DN_V, :], preferred_element_type=F32)
            + jnp.dot(_bf(oswa_ref[...]), wout_ref[DN_V:, :], preferred_element_type=F32))
    x1 = x_ref[...].reshape(rows, D_MODEL) + _rows(mod_ref[:, 2:3, :], nb, lt) * attn
    x1_scr[...] = x1
    h = _rms(x1, nfw_ref[...]) * (1.0 + _rows(mod_ref[:, 4:5, :], nb, lt)) + _rows(mod_ref[:, 3:4, :], nb, lt)
    h_scr[...] = _bf(h)
    acc_scr[...] = jnp.zeros_like(acc_scr)
    n_chunks = D_FF // FFN_CHUNK

    def col_slices(c):
        return [slice(base + c * FFN_CHUNK, base + (c + 1) * FFN_CHUNK) for base in (0, D_FF)]

    def up_proj(c):
        return [jnp.dot(h_scr[...], wup_ref[:, cols], preferred_element_type=F32) for cols in col_slices(c)]

    u_next = up_proj(0)
    for c in range(n_chunks):
        u_cur = u_next
        if c + 1 < n_chunks:
            u_next = up_proj(c + 1)
        halves = []
        for u, cols in zip(u_cur, col_slices(c)):
            prev = prev_ref[:, :, cols]
            halves.append(_causal_conv(u, prev, cw_ref[:, cols], nb, lt) + cb_ref[:, cols])
            fbuf_ref[:, :, cols] = _last_rows(u, nb, lt, FFN_CONV - 1)
        act = _silu(halves[0]) * halves[1]
        acc_scr[...] += jnp.dot(_bf(act), wdn_ref[c * FFN_CHUNK:(c + 1) * FFN_CHUNK, :],
                                preferred_element_type=F32)
    x2 = x1_scr[...] + _rows(mod_ref[:, 5:6, :], nb, lt) * acc_scr[...]
    y_ref[...] = _rms(x2, fnw_ref[...]).reshape(nb, lt, D_MODEL)


def _post_call(o_dn, o_swa, x, mod3, w_out, norm_ffn_w, w_up, conv_w, conv_b, w_down, final_w, nb, lt, state=None):
    carry = state is None
    bsz, seq, _ = x.shape
    nt = seq // lt
    rows = nb * lt
    n_state = _state_rows(nb, lt, FFN_CONV - 1)
    row_map = lambda i, j: (i * nt + j, 0)
    const = lambda i, j: (0, 0)
    in_specs = [pl.BlockSpec((rows, DN_V), row_map),
                pl.BlockSpec((rows, SWA_Q), row_map),
                pl.BlockSpec((nb, lt, D_MODEL), lambda i, j: (i, j, 0)),
                pl.BlockSpec((nb, 6, D_MODEL), lambda i, j: (i, 0, 0)),
                pl.BlockSpec((D_MODEL, D_MODEL), const, pipeline_mode=RESIDENT),
                pl.BlockSpec((1, D_MODEL), const),
                pl.BlockSpec((D_MODEL, 2 * D_FF), const, pipeline_mode=RESIDENT),
                pl.BlockSpec((FFN_CONV, 2 * D_FF), const),
                pl.BlockSpec((1, 2 * D_FF), const),
                pl.BlockSpec((D_FF, D_MODEL), const, pipeline_mode=RESIDENT),
                pl.BlockSpec((1, D_MODEL), const)]
    args = [o_dn, o_swa, x, mod3, w_out, norm_ffn_w, w_up, conv_w, conv_b, w_down, final_w]
    if not carry:
        in_specs.append(pl.BlockSpec((nb, FFN_CONV - 1, 2 * D_FF), lambda i, j: (i, 0, 0)))
        args.append(state)
    return pl.pallas_call(
        functools.partial(_post_kernel, nb=nb, lt=lt, carry=carry),
        grid=(bsz // nb, nt),
        in_specs=in_specs,
        out_specs=[pl.BlockSpec((nb, lt, D_MODEL), lambda i, j: (i, j, 0)),
                   pl.BlockSpec((nb, n_state, 2 * D_FF), lambda i, j: (i, 0, 0))],
        out_shape=[jax.ShapeDtypeStruct((bsz, seq, D_MODEL), F32),
                   jax.ShapeDtypeStruct((bsz, n_state, 2 * D_FF), F32)],
        scratch_shapes=[pltpu.VMEM((rows, D_MODEL), BF16),
                        pltpu.VMEM((rows, D_MODEL), F32),
                        pltpu.VMEM((rows, D_MODEL), F32)],
        compiler_params=_cparams(("arbitrary", "arbitrary")),
        name="out_proj_convffn_carry" if carry else "out_proj_convffn_state",
    )(*args)


def _pad_row(vec, offset):
    return jnp.zeros((1, BA_PAD), F32).at[0, offset:offset + vec.shape[0]].set(vec)


def kernel(x_prompt, x_sample, state_dn_conv, state_dn_ssm, cache_swa_k, cache_swa_v, state_ffn_conv, c_prompt, c_sample, rel_bias, final_norm_w, w_ada, b_ada, norm_mix_w, w_in, dn_conv_w, dn_A_log, dn_dt_bias, dn_norm_w, swa_sinks, w_out, norm_ffn_w, ffn_w_up, ffn_conv_w, ffn_conv_b, ffn_w_down):
    bp, lp, _ = x_prompt.shape
    bs, ls, _ = x_sample.shape
    layer = 0

    w_out_b = w_out[layer].astype(BF16)
    w_up_b = ffn_w_up[layer].astype(BF16)
    w_dn_b = ffn_w_down[layer].astype(BF16)
    alog_row = _pad_row(dn_A_log[layer], DN_HEADS)
    dt_row = _pad_row(dn_dt_bias[layer], DN_HEADS)
    row = lambda v: v.reshape(1, -1)

    n_c = bp + bs
    n_c_pad = -(-n_c // SUBLANES) * SUBLANES
    c_all = jnp.pad(jnp.concatenate([c_sample, c_prompt], axis=0), ((0, n_c_pad - n_c), (0, 0)))
    mod_s, mod_p = _mod_call(c_all, w_ada[layer], row(b_ada[layer]), bs, bp)

    bias_t, bias_s = _bias_call(rel_bias)
    sinks = swa_sinks[layer]

    def mixer_in(x, mod3, nb, lt, state=None):
        return _in_call(x, mod3, row(norm_mix_w[layer]), w_in[layer], dn_conv_w[layer], nb, lt, state)

    def post(o_dn, o_swa, x, mod3, nb, lt, state=None):
        return _post_call(o_dn, o_swa, x, mod3, w_out_b, row(norm_ffn_w[layer]), w_up_b, ffn_conv_w[layer],
                          row(ffn_conv_b[layer]), w_dn_b, row(final_norm_w), nb, lt, state)

    dn_args = (alog_row, dt_row, row(dn_norm_w[layer]))

    qkv, z, ba, sq, sk, sv, p_tail = mixer_in(x_prompt, mod_p, 1, 512)
    o_dn, p_ssm = _dn_call(qkv, z, ba, *dn_args, bp, lp)
    o_swa = _swa_prompt_call(sinks, sq, sk, sv, bias_t, bp, lp)
    y_prompt, p_ffn_tail = post(o_dn, o_swa, x_prompt, mod_p, 1, 256)
    p_dn_conv = p_tail[:, SUBLANES - (DN_CONV - 1):]
    p_fbuf = p_ffn_tail[:, SUBLANES - (FFN_CONV - 1):]
    last_window = lambda t: t.reshape(bp, lp, SWA_KV)[:, lp - WINDOW:].reshape(bp, WINDOW, SWA_KV_HEADS, SWA_HD)
    p_swa_k = last_window(sk)
    p_swa_v = last_window(sv)

    qkv_s, z_s, ba_s, sq_s, sk_s, sv_s, s_dn_conv = mixer_in(x_sample, mod_s, 64, ls, state=state_dn_conv[layer])
    o_dn_s, s_ssm = _dn_call(qkv_s, z_s, ba_s, *dn_args, bs, ls, state=state_dn_ssm[layer])
    o_swa_s, s_k, s_v = _swa_sample_call(sinks, sq_s, sk_s, sv_s,
                                         cache_swa_k[layer].reshape(bs, WINDOW, SWA_KV),
                                         cache_swa_v[layer].reshape(bs, WINDOW, SWA_KV), bias_s, bs, ls)
    y_sample, s_fbuf = post(o_dn_s, o_swa_s, x_sample, mod_s, 32, ls, state=state_ffn_conv[layer])

    return (y_prompt, y_sample, p_dn_conv[None], s_dn_conv[None], p_ssm[None], s_ssm[None],
            p_swa_k[None], s_k.reshape(bs, WINDOW, SWA_KV_HEADS, SWA_HD)[None],
            p_swa_v[None], s_v.reshape(bs, WINDOW, SWA_KV_HEADS, SWA_HD)[None],
            p_fbuf[None], s_fbuf[None])
```

```python
import functools
import math

import numpy as np
import jax
import jax.numpy as jnp
from jax import lax
from jax.experimental import pallas as pl
from jax.experimental.pallas import tpu as pltpu

F32 = jnp.float32
BF16 = jnp.bfloat16

D_MODEL = 1024
PAST_LEN = 16384
DN_HEADS = 4
DN_DK = 128
DN_DV = 128
DN_CONV = 4
SWA_HEADS = 8
SWA_KV_HEADS = 2
SWA_GROUP = SWA_HEADS // SWA_KV_HEADS
SWA_HD = 64
WINDOW = 128
N_BUCKETS = 32
MAX_DISTANCE = 128
D_FF = 2816
FFN_CONV = 3
EPS = 1e-6
NEG_INF = -1e30

DN_QK = DN_HEADS * DN_DK
DN_V = DN_HEADS * DN_DV
DN_CONV_CH = 2 * DN_QK + DN_V
SWA_Q = SWA_HEADS * SWA_HD
SWA_KV = SWA_KV_HEADS * SWA_HD
BA_PAD = 128
SUBLANES = 8
UNIT = 128
FFN_CHUNK = 256
VMEM_LIMIT = 56 * 1024 * 1024
RESIDENT = pl.Buffered(1)


def _cparams(sem):
    return pltpu.CompilerParams(dimension_semantics=sem, vmem_limit_bytes=VMEM_LIMIT)


def _bf(x):
    return x.astype(BF16)


def _dot(a, b):
    return jnp.dot(_bf(a), _bf(b), preferred_element_type=F32)


def _dot_nt(a, b):
    return lax.dot_general(_bf(a), _bf(b), (((1,), (1,)), ((), ())), preferred_element_type=F32)


def _split2(x):
    hi = _bf(x)
    lo = _bf(x - hi.astype(F32))
    return hi, lo


def _dot_x3(a, b):
    ah, al = _split2(a)
    bh, bl = _split2(b)
    d = functools.partial(jnp.dot, preferred_element_type=F32)
    return d(ah, bh) + (d(al, bh) + d(ah, bl))


def _dot_mask(m, x):
    hi = _bf(x)
    r = x - hi.astype(F32)
    mid = _bf(r)
    lo = _bf(r - mid.astype(F32))
    d = functools.partial(jnp.dot, preferred_element_type=F32)
    return d(m, hi) + (d(m, mid) + d(m, lo))


def _sigmoid(x):
    return 1.0 / (1.0 + jnp.exp(-x))


def _silu(x):
    return x * _sigmoid(x)


def _softplus(x):
    return jnp.maximum(x, 0.0) + jnp.log1p(jnp.exp(-jnp.abs(x)))


def _rms(x, w):
    ms = jnp.mean(x * x, axis=-1, keepdims=True)
    return x * lax.rsqrt(ms + EPS) * w


def _l2norm(t):
    return t * lax.rsqrt(jnp.sum(t * t, axis=-1, keepdims=True) + EPS)


def _rows(m3, nb, lt):
    return jnp.broadcast_to(m3, (nb, lt, m3.shape[-1])).reshape(nb * lt, m3.shape[-1])


def _causal_conv(x, prev, w, nb, lt):
    width = w.shape[0]
    rows, ch = x.shape
    if nb == 1 and lt > SUBLANES:
        tiles = jnp.concatenate([prev, x.reshape(lt // SUBLANES, SUBLANES, ch)], axis=0)
        sub = lax.broadcasted_iota(jnp.int32, (1, SUBLANES, 1), 1)
        out = tiles[1:] * w[width - 1:width, :]
        for j in range(1, width):
            rot = pltpu.roll(tiles, j, axis=1)
            out = out + jnp.where(sub >= j, rot[1:], rot[:-1]) * w[width - 1 - j:width - j, :]
        return out.reshape(rows, ch)
    tmod = lax.broadcasted_iota(jnp.int32, (rows, 1), 0) & (lt - 1)
    out = x * w[width - 1:width, :]
    for j in range(1, width):
        sh = pltpu.roll(x, j, axis=0)
        for t in range(j):
            p = width - 1 - j + t
            sh = jnp.where(tmod == t, _rows(prev[:, p:p + 1, :], nb, lt), sh)
        out = out + sh * w[width - 1 - j:width - j, :]
    return out


def _last_rows(x, nb, lt, n_state):
    ch = x.shape[-1]
    if nb == 1 and lt > SUBLANES:
        return x[lt - SUBLANES:].reshape(1, SUBLANES, ch)
    return x.reshape(nb, lt, ch)[:, lt - n_state:, :]


def _state_rows(nb, lt, n_state):
    return SUBLANES if (nb == 1 and lt > SUBLANES) else n_state


MOD_PARTS = 6


def _mod_kernel(c_ref, w_ref, b_ref, os_ref, op_ref, *, n_s, n_p):
    part = pl.program_id(0)
    res = _dot_x3(_silu(c_ref[...]), w_ref[...]) + b_ref[...]
    for k in range(MOD_PARTS):
        @pl.when(part == k)
        def _():
            os_ref[:, k, :] = res[:n_s]
            op_ref[:, k, :] = res[n_s:n_s + n_p]


def _mod_call(c_all, w_ada, b_ada, n_s, n_p):
    rows = c_all.shape[0]
    return pl.pallas_call(
        functools.partial(_mod_kernel, n_s=n_s, n_p=n_p),
        grid=(MOD_PARTS,),
        in_specs=[pl.BlockSpec((rows, D_MODEL), lambda k: (0, 0)),
                  pl.BlockSpec((D_MODEL, D_MODEL), lambda k: (0, k)),
                  pl.BlockSpec((1, D_MODEL), lambda k: (0, k))],
        out_specs=[pl.BlockSpec((n_s, MOD_PARTS, D_MODEL), lambda k: (0, 0, 0)),
                   pl.BlockSpec((n_p, MOD_PARTS, D_MODEL), lambda k: (0, 0, 0))],
        out_shape=[jax.ShapeDtypeStruct((n_s, MOD_PARTS, D_MODEL), F32),
                   jax.ShapeDtypeStruct((n_p, MOD_PARTS, D_MODEL), F32)],
        compiler_params=_cparams(("arbitrary",)),
        name="adaln_mod",
    )(c_all, w_ada, b_ada)


IN_SPLIT = (DN_CONV_CH, DN_V, BA_PAD, SWA_Q, SWA_KV, SWA_KV)


W_PREP_ROWS = 256
IN_SLAB = 2 * DN_DK


def _in_kernel(*refs, nb, lt, carry):
    if carry:
        (x_ref, mod_ref, nw_ref, w_ref, cw_ref,
         qkv_ref, z_ref, ba_ref, sq_ref, sk_ref, sv_ref, tail_ref, h_scr, wdn_ref, wba_ref, wsw_ref) = refs
        prev_ref = tail_ref

        @pl.when(pl.program_id(1) == 0)
        def _():
            tail_ref[...] = jnp.zeros_like(tail_ref)
    else:
        (x_ref, mod_ref, nw_ref, w_ref, cw_ref, prev_ref,
         qkv_ref, z_ref, ba_ref, sq_ref, sk_ref, sv_ref, tail_ref, h_scr, wdn_ref, wba_ref, wsw_ref) = refs

    @pl.when((pl.program_id(0) == 0) & (pl.program_id(1) == 0))
    def _():
        ba_lo = DN_CONV_CH + DN_V
        n_ba = 2 * DN_HEADS
        for r in range(0, ba_lo, W_PREP_ROWS):
            wdn_ref[r:r + W_PREP_ROWS, :] = _bf(w_ref[r:r + W_PREP_ROWS, :])
        row = lax.broadcasted_iota(jnp.int32, (BA_PAD, 1), 0)
        wba_ref[...] = _bf(jnp.where(row < n_ba, w_ref[ba_lo:ba_lo + BA_PAD, :], 0.0))
        for r in range(0, SWA_Q + 2 * SWA_KV, W_PREP_ROWS):
            wsw_ref[r:r + W_PREP_ROWS, :] = _bf(w_ref[ba_lo + n_ba + r:ba_lo + n_ba + r + W_PREP_ROWS, :])

    rows = nb * lt
    x = x_ref[...]
    ms = jnp.mean(x * x, axis=-1, keepdims=True)
    y = x * lax.rsqrt(ms + EPS) * nw_ref[...]
    h = y * (1.0 + mod_ref[:, 1:2, :]) + mod_ref[:, 0:1, :]
    h_scr[...] = _bf(h.reshape(rows, D_MODEL))

    def proj(wt_ref, lo, n):
        return lax.dot_general(h_scr[...], wt_ref[lo:lo + n, :], (((1,), (1,)), ((), ())),
                               preferred_element_type=F32)

    def plain_proj(i):
        if i < 2:
            z_ref[:, i * IN_SLAB:(i + 1) * IN_SLAB] = proj(wdn_ref, DN_CONV_CH + i * IN_SLAB, IN_SLAB)
        elif i < 4:
            sq_ref[:, (i - 2) * IN_SLAB:(i - 1) * IN_SLAB] = proj(wsw_ref, (i - 2) * IN_SLAB, IN_SLAB)
        elif i == 4:
            kv = proj(wsw_ref, SWA_Q, 2 * SWA_KV)
            sk_ref[...] = kv[:, :SWA_KV]
            sv_ref[...] = kv[:, SWA_KV:]
        else:
            ba_ref[...] = proj(wba_ref, 0, BA_PAD)

    n_slabs = DN_CONV_CH // IN_SLAB
    raw_next = proj(wdn_ref, 0, IN_SLAB)
    for slab in range(n_slabs):
        raw = raw_next
        plain_proj(slab)
        if slab + 1 < n_slabs:
            raw_next = proj(wdn_ref, (slab + 1) * IN_SLAB, IN_SLAB)
        cols = slice(slab * IN_SLAB, (slab + 1) * IN_SLAB)
        c = _silu(_causal_conv(raw, prev_ref[:, :, cols], cw_ref[:, cols], nb, lt))
        tail_ref[:, :, cols] = _last_rows(raw, nb, lt, DN_CONV - 1)
        if slab * IN_SLAB < 2 * DN_QK:
            scale = DN_DK ** -0.5 if slab * IN_SLAB < DN_QK else 1.0
            c = jnp.concatenate([_l2norm(c[:, i * DN_DK:(i + 1) * DN_DK]) * scale
                                 for i in range(IN_SLAB // DN_DK)], axis=1)
        qkv_ref[:, cols] = c


def _in_call(x, mod3, norm_w, w_in_t, conv_w, nb, lt, state=None):
    carry = state is None
    bsz, seq, _ = x.shape
    nt = seq // lt
    rows = nb * lt
    n_tok = bsz * seq
    n_state = _state_rows(nb, lt, DN_CONV - 1)
    row_map = lambda i, j: (i * nt + j, 0)
    const = lambda i, j: (0, 0)
    per_seq = lambda i, j: (i, 0, 0)
    in_specs = [pl.BlockSpec((nb, lt, D_MODEL), lambda i, j: (i, j, 0)),
                pl.BlockSpec((nb, 6, D_MODEL), per_seq),
                pl.BlockSpec((1, D_MODEL), const),
                pl.BlockSpec(w_in_t.shape, const, pipeline_mode=RESIDENT),
                pl.BlockSpec((DN_CONV, DN_CONV_CH), const)]
    args = [x, mod3, norm_w, w_in_t, conv_w]
    if not carry:
        in_specs.append(pl.BlockSpec((nb, DN_CONV - 1, DN_CONV_CH), per_seq))
        args.append(state)
    return pl.pallas_call(
        functools.partial(_in_kernel, nb=nb, lt=lt, carry=carry),
        grid=(bsz // nb, nt),
        in_specs=in_specs,
        out_specs=[pl.BlockSpec((rows, n), row_map) for n in IN_SPLIT]
        + [pl.BlockSpec((nb, n_state, DN_CONV_CH), per_seq)],
        out_shape=[jax.ShapeDtypeStruct((n_tok, n), F32) for n in IN_SPLIT]
        + [jax.ShapeDtypeStruct((bsz, n_state, DN_CONV_CH), F32)],
        scratch_shapes=[pltpu.VMEM((rows, D_MODEL), BF16),
                        pltpu.VMEM((DN_CONV_CH + DN_V, D_MODEL), BF16),
                        pltpu.VMEM((BA_PAD, D_MODEL), BF16),
                        pltpu.VMEM((SWA_Q + 2 * SWA_KV, D_MODEL), BF16)],
        compiler_params=_cparams(("arbitrary", "arbitrary")),
        name="norm_in_proj_carry" if carry else "norm_in_proj_state",
    )(*args)


INV_BASE_SHIFT = 2
DN_UNITS_CARRY = 4
SWA_BLOCKS_PER_STEP = 4
DN_UNITS_STATE = 1


def _unit_lower_inverses(a_mats, ri, ci, chunk_shift):
    def blocks(s):
        return (ri >> s) == (ci >> s)

    base = min(INV_BASE_SHIFT, chunk_shift)
    xs = [jnp.where(blocks(base), -a, 0.0) for a in a_mats]
    ts = [jnp.where(ri == ci, 1.0, n) for n in xs]
    for lvl in range(1, base):
        xs = [_dot(x, x) for x in xs]
        ts = [t + _dot(t, x) for t, x in zip(ts, xs)]
    for s in range(base, chunk_shift):
        sel = blocks(s + 1) & jnp.logical_not(blocks(s))
        ets = [_dot(jnp.where(sel, a, 0.0), t) for a, t in zip(a_mats, ts)]
        ts = [t - _dot(t, et) for t, et in zip(ts, ets)]
    return ts


def _dn_kernel(*refs, nb, chunk, carry, units):
    if carry:
        (qkv_ref, z_ref, ba_ref, alog_ref, dt_ref, nw_ref, o_ref, s_ref) = refs
        s0_ref = s_ref

        @pl.when(pl.program_id(1) == 0)
        def _():
            s_ref[...] = jnp.zeros_like(s_ref)
    else:
        (qkv_ref, z_ref, ba_ref, alog_ref, dt_ref, nw_ref, s0_ref, o_ref, s_ref) = refs

    ri = lax.broadcasted_iota(jnp.int32, (UNIT, UNIT), 0)
    ci = lax.broadcasted_iota(jnp.int32, (UNIT, UNIT), 1)
    shift = int(math.log2(chunk))
    same = (ri >> shift) == (ci >> shift)
    incl = same & (ri >= ci)
    strict = same & (ri > ci)

    ba = ba_ref[...]
    beta_full = _sigmoid(ba)
    g_full = -jnp.exp(alog_ref[...]) * _softplus(ba + dt_ref[...])
    masks = jnp.concatenate([jnp.where(incl, 1.0, 0.0), jnp.where(same, 1.0, 0.0)], axis=0).astype(BF16)
    gsums = [_dot_mask(masks, g_full[u * UNIT:(u + 1) * UNIT]) for u in range(units)]
    g_cum = [g[:UNIT] for g in gsums]
    g_tot = [g[UNIT:] for g in gsums]
    g_cum_t = [g.T for g in g_cum]

    probs = [(u, h) for u in range(units) for h in range(DN_HEADS)]
    n_p = len(probs)
    rs = lambda u: slice(u * UNIT, (u + 1) * UNIT)

    def head_cols(base):
        return [qkv_ref[rs(u), base + h * DN_DK:base + (h + 1) * DN_DK] for u, h in probs]

    q = head_cols(0)
    k = head_cols(DN_QK)
    v = head_cols(2 * DN_QK)
    gc = [g_cum[u][:, DN_HEADS + h:DN_HEADS + h + 1] for u, h in probs]
    gr = [g_cum_t[u][DN_HEADS + h:DN_HEADS + h + 1, :] for u, h in probs]
    gt = [g_tot[u][:, DN_HEADS + h:DN_HEADS + h + 1] for u, h in probs]
    bc = [beta_full[rs(u), h:h + 1] for u, h in probs]
    decay = [jnp.where(incl, jnp.exp(jnp.where(incl, gc[p] - gr[p], 0.0)), 0.0) for p in range(n_p)]
    e_g = [jnp.exp(gc[p]) for p in range(n_p)]
    kq = [_dot_nt(jnp.concatenate([k[p], q[p]], axis=0), k[p]) for p in range(n_p)]
    qk = [kq[p][UNIT:] * decay[p] for p in range(n_p)]
    a_mats = [jnp.where(strict, bc[p] * kq[p][:UNIT] * decay[p], 0.0) for p in range(n_p)]
    t_inv = _unit_lower_inverses(a_mats, ri, ci, shift)
    wvk = [_dot(t_inv[p], jnp.concatenate([v[p] * bc[p], k[p] * (bc[p] * e_g[p])], axis=1)) for p in range(n_p)]
    w_v = [w[:, :DN_DV] for w in wvk]
    w_k = [w[:, DN_DV:] for w in wvk]
    q_dec = [q[p] * e_g[p] for p in range(n_p)]
    k_tail_t = [(k[p] * jnp.exp(gt[p] - gc[p])).T for p in range(n_p)]
    c_dec = [jnp.exp(gt[p]) for p in range(n_p)]

    outs = {}
    if carry:
        state = [s_ref[0, h] for h in range(DN_HEADS)]
        for u in range(units):
            ps = [u * DN_HEADS + h for h in range(DN_HEADS)]
            r = [_dot(jnp.concatenate([w_k[p], q_dec[p]], axis=0), state[h]) for h, p in enumerate(ps)]
            uu = [w_v[p] - r[h][:UNIT] for h, p in enumerate(ps)]
            for h, p in enumerate(ps):
                outs[p] = r[h][UNIT:] + _dot(qk[p], uu[h])
            state = [state[h] * c_dec[p][0:1, :] + _dot(k_tail_t[p], uu[h]) for h, p in enumerate(ps)]
        for h in range(DN_HEADS):
            s_ref[0, h] = state[h]
    else:
        per_unit = nb // units
        for p, (u, h) in enumerate(probs):
            us, qs = [], []
            for s in range(per_unit):
                lo = s * chunk
                lhs = jnp.concatenate([w_k[p][lo:lo + chunk], q_dec[p][lo:lo + chunk]], axis=0)
                r = jnp.dot(lhs, s0_ref[u * per_unit + s, h], preferred_element_type=F32)
                us.append(w_v[p][lo:lo + chunk] - r[:chunk])
                qs.append(r[chunk:])
            uu = jnp.concatenate(us, axis=0)
            outs[p] = jnp.concatenate(qs, axis=0) + _dot(qk[p], uu)
            for s in range(per_unit):
                lo = s * chunk
                upd = jnp.dot(k_tail_t[p][:, lo:lo + chunk], uu[lo:lo + chunk], preferred_element_type=F32)
                s_ref[u * per_unit + s, h] = s0_ref[u * per_unit + s, h] * c_dec[p][lo:lo + 1, :] + upd

    for p, (u, h) in enumerate(probs):
        zz = z_ref[rs(u), h * DN_DV:(h + 1) * DN_DV]
        o_ref[rs(u), h * DN_DV:(h + 1) * DN_DV] = _rms(outs[p], nw_ref[...]) * _silu(zz)


def _dn_call(qkv, z, ba, alog_row, dt_row, norm_w, bsz, seq, state=None):
    carry = state is None
    units = DN_UNITS_CARRY if carry else DN_UNITS_STATE
    rows = units * UNIT
    nb = 1 if carry else rows // seq
    chunk = UNIT if carry else seq
    nt = seq * nb // rows
    n_tok = bsz * seq
    row_map = lambda i, j: (i * nt + j, 0)
    const = lambda i, j: (0, 0)
    in_specs = [pl.BlockSpec((rows, DN_CONV_CH), row_map),
                pl.BlockSpec((rows, DN_V), row_map),
                pl.BlockSpec((rows, BA_PAD), row_map),
                pl.BlockSpec((1, BA_PAD), const),
                pl.BlockSpec((1, BA_PAD), const),
                pl.BlockSpec((1, DN_DV), const)]
    args = [qkv, z, ba, alog_row, dt_row, norm_w]
    if not carry:
        in_specs.append(pl.BlockSpec((nb, DN_HEADS, DN_DK, DN_DV), lambda i, j: (i, 0, 0, 0)))
        args.append(state)
    return pl.pallas_call(
        functools.partial(_dn_kernel, nb=nb, chunk=chunk, carry=carry, units=units),
        grid=(bsz // nb, nt),
        in_specs=in_specs,
        out_specs=[pl.BlockSpec((rows, DN_V), row_map),
                   pl.BlockSpec((nb, DN_HEADS, DN_DK, DN_DV), lambda i, j: (i, 0, 0, 0))],
        out_shape=[jax.ShapeDtypeStruct((n_tok, DN_V), F32),
                   jax.ShapeDtypeStruct((bsz, DN_HEADS, DN_DK, DN_DV), F32)],
        compiler_params=_cparams(("arbitrary", "arbitrary")),
        name="gated_deltanet_carry" if carry else "gated_deltanet_state",
    )(*args)


def _bucket_table():
    i = np.arange(WINDOW, dtype=np.int64)[:, None]
    j = np.arange(2 * WINDOW, dtype=np.int64)[None, :]
    d = np.maximum(i + WINDOW - j, 0)
    exact = N_BUCKETS // 2
    logv = (np.log(np.maximum(d, 1).astype(np.float32) / np.float32(exact)).astype(np.float32)
            / np.float32(math.log(MAX_DISTANCE / exact)))
    large = np.minimum(exact + (logv * np.float32(N_BUCKETS - exact)).astype(np.int32), N_BUCKETS - 1)
    return np.where(d < exact, d, large).astype(np.int32)


def _bias_lookup(rb_ref, bucket, head):
    acc = jnp.zeros(bucket.shape, F32)
    for b in range(N_BUCKETS):
        acc = jnp.where(bucket == b, rb_ref[b, head], acc)
    return acc


def _bias_kernel(rb_ref, bucket_t_ref, bucket_ref, ot_ref, os_ref):
    bucket_t = bucket_t_ref[...]
    kj = lax.broadcasted_iota(jnp.int32, bucket_t.shape, 0)
    qi = lax.broadcasted_iota(jnp.int32, bucket_t.shape, 1)
    dist = qi + WINDOW - kj
    valid = (dist >= 0) & (dist < WINDOW)
    for hk in range(SWA_KV_HEADS):
        for par in range(2):
            for st in range(SWA_GROUP // 2):
                head = hk * SWA_GROUP + 2 * st + par
                gen = jnp.where(valid, _bias_lookup(rb_ref, bucket_t, head), NEG_INF)
                ot_ref[1, hk, par, :, st * WINDOW:(st + 1) * WINDOW] = gen
                ot_ref[0, hk, par, :, st * WINDOW:(st + 1) * WINDOW] = jnp.where(kj >= WINDOW, gen, NEG_INF)
    bucket = bucket_ref[...]
    qi = lax.broadcasted_iota(jnp.int32, bucket.shape, 0)
    kj = lax.broadcasted_iota(jnp.int32, bucket.shape, 1)
    dist = qi + WINDOW - kj
    valid = (dist >= 0) & (dist < WINDOW)
    for head in range(SWA_HEADS):
        os_ref[head] = jnp.where(valid, _bias_lookup(rb_ref, bucket, head), NEG_INF)


BIAS_T_SHAPE = (2, SWA_KV_HEADS, 2, 2 * WINDOW, (SWA_GROUP // 2) * WINDOW)
BIAS_S_SHAPE = (SWA_HEADS, SUBLANES, 2 * WINDOW)


def _bias_call(rel_bias):
    bucket = _bucket_table()
    return pl.pallas_call(
        _bias_kernel,
        in_specs=[pl.BlockSpec(memory_space=pltpu.SMEM),
                  pl.BlockSpec((2 * WINDOW, WINDOW), lambda: (0, 0)),
                  pl.BlockSpec((SUBLANES, 2 * WINDOW), lambda: (0, 0))],
        out_specs=[pl.BlockSpec(BIAS_T_SHAPE, lambda: (0,) * len(BIAS_T_SHAPE)),
                   pl.BlockSpec(BIAS_S_SHAPE, lambda: (0,) * len(BIAS_S_SHAPE))],
        out_shape=[jax.ShapeDtypeStruct(BIAS_T_SHAPE, F32), jax.ShapeDtypeStruct(BIAS_S_SHAPE, F32)],
        name="swa_rel_bias_table",
    )(rel_bias, jnp.asarray(np.ascontiguousarray(bucket.T)), jnp.asarray(bucket[:SUBLANES]))


def _softmax_sink_parts(s, sink):
    m = jnp.maximum(jnp.max(s, axis=0, keepdims=True), sink)
    p = jnp.exp(s - m)
    return p, 1.0 / (jnp.sum(p, axis=0, keepdims=True) + jnp.exp(sink - m))


def _half_lane_variants(full, hk, lo_half):
    rolled = pltpu.roll(full, SWA_HD, axis=1)
    low_src, high_src = (full, rolled) if hk == 0 else (rolled, full)
    return jnp.where(lo_half, low_src, 0.0), jnp.where(lo_half, 0.0, high_src)


def _swa_prompt_kernel(sink_ref, q_ref, kp_ref, kc_ref, vp_ref, vc_ref, bias_ref, o_ref, *, n_blk):
    step = pl.program_id(1)
    lo_half = lax.broadcasted_iota(jnp.int32, (1, 2 * SWA_HD), 1) < SWA_HD
    lo_rows = lax.broadcasted_iota(jnp.int32, (2 * SWA_HD, 1), 0) < SWA_HD
    q = _bf(q_ref[...] * (SWA_HD ** -0.5))
    keys = jnp.concatenate([kp_ref[...], kc_ref[...]], axis=0)
    vals = jnp.concatenate([vp_ref[...], vc_ref[...]], axis=0)
    k_var = [[_bf(t) for t in _half_lane_variants(keys, hk, lo_half)] for hk in range(SWA_KV_HEADS)]
    v_var_t = [[_bf(t.T) for t in _half_lane_variants(vals, hk, lo_half)] for hk in range(SWA_KV_HEADS)]
    n_stack = SWA_GROUP // 2
    sinks = [[jnp.concatenate([jnp.full((1, WINDOW), sink_ref[hk * SWA_GROUP + 2 * st + par], F32)
                               for st in range(n_stack)], axis=1) for par in range(2)]
             for hk in range(SWA_KV_HEADS)]
    nt_dims = (((1,), (1,)), ((), ()))

    def scores(b):
        rows = slice(b * WINDOW, (b + 1) * WINDOW)
        win = slice(b * WINDOW, (b + 2) * WINDOW)
        variant = jnp.where(step == 0, 0, 1) if b == 0 else 1
        out = []
        for hk in range(SWA_KV_HEADS):
            q2 = jnp.concatenate([q[rows, (hk * n_stack + st) * 2 * SWA_HD:(hk * n_stack + st + 1) * 2 * SWA_HD]
                                  for st in range(n_stack)], axis=0)
            out.append([lax.dot_general(k_var[hk][par][win], q2, nt_dims, preferred_element_type=F32)
                        + bias_ref[variant, hk, par] for par in range(2)])
        return out

    s_next = scores(0)
    for b in range(n_blk):
        s_cur = s_next
        if b + 1 < n_blk:
            s_next = scores(b + 1)
        rows = slice(b * WINDOW, (b + 1) * WINDOW)
        win = slice(b * WINDOW, (b + 2) * WINDOW)
        for hk in range(SWA_KV_HEADS):
            parts = [_softmax_sink_parts(s_cur[hk][par], sinks[hk][par]) for par in range(2)]
            o_t = (jnp.dot(v_var_t[hk][0][:, win], _bf(parts[0][0]), preferred_element_type=F32)
                   + jnp.dot(v_var_t[hk][1][:, win], _bf(parts[1][0]), preferred_element_type=F32))
            o_t = o_t * jnp.where(lo_rows, parts[0][1], parts[1][1])
            for st in range(n_stack):
                lo = (hk * n_stack + st) * 2 * SWA_HD
                o_ref[rows, lo:lo + 2 * SWA_HD] = o_t[:, st * WINDOW:(st + 1) * WINDOW].T


def _swa_prompt_call(sinks, sq, sk, sv, bias, bsz, seq):
    n_blk = SWA_BLOCKS_PER_STEP
    tile = n_blk * WINDOW
    nt = seq // tile
    cur = lambda b, i: (b * nt + i, 0)
    prv = lambda b, i: (b * nt * n_blk + jnp.maximum(i * n_blk - 1, 0), 0)
    return pl.pallas_call(
        functools.partial(_swa_prompt_kernel, n_blk=n_blk),
        grid=(bsz, nt),
        in_specs=[pl.BlockSpec(memory_space=pltpu.SMEM),
                  pl.BlockSpec((tile, SWA_Q), cur),
                  pl.BlockSpec((WINDOW, SWA_KV), prv),
                  pl.BlockSpec((tile, SWA_KV), cur),
                  pl.BlockSpec((WINDOW, SWA_KV), prv),
                  pl.BlockSpec((tile, SWA_KV), cur),
                  pl.BlockSpec(BIAS_T_SHAPE, lambda b, i: (0,) * len(BIAS_T_SHAPE))],
        out_specs=pl.BlockSpec((tile, SWA_Q), cur),
        out_shape=jax.ShapeDtypeStruct((bsz * seq, SWA_Q), F32),
        compiler_params=_cparams(("arbitrary", "arbitrary")),
        name="swa_banded",
    )(sinks, sq, sk, sk, sv, sv, bias)


def _swa_sample_kernel(sink_ref, q_ref, kn_ref, vn_ref, kc_ref, vc_ref, bias_ref, o_ref, ko_ref, vo_ref, *, nb, lt):
    lo_half = lax.broadcasted_iota(jnp.int32, (1, 2 * SWA_HD), 1) < SWA_HD
    q_all = q_ref[...] * (SWA_HD ** -0.5)
    kn_all = kn_ref[...]
    vn_all = vn_ref[...]
    pad = jnp.zeros((WINDOW - lt, SWA_KV), F32)
    bias = jnp.concatenate([bias_ref[h, 0:lt, :] for h in range(SWA_HEADS)], axis=0)
    sink = jnp.concatenate([jnp.full((lt, 1), sink_ref[h], F32) for h in range(SWA_HEADS)], axis=0)
    nt_dims = (((1,), (1,)), ((), ()))

    def to_half(tile, src_half, dst_half):
        return tile if src_half == dst_half else pltpu.roll(tile, SWA_HD, axis=1)

    lhs, keys, vals = [], [], []
    for s in range(nb):
        rows = slice(s * lt, (s + 1) * lt)
        kc, vc, kn, vn = kc_ref[s], vc_ref[s], kn_all[rows], vn_all[rows]
        ko_ref[s] = jnp.concatenate([kc[lt:], kn], axis=0)
        vo_ref[s] = jnp.concatenate([vc[lt:], vn], axis=0)
        keys.append(_bf(jnp.concatenate([kc, kn, pad], axis=0)))
        vals.append(_bf(jnp.concatenate([vc, vn, pad], axis=0)))
        q = q_all[rows]
        tiles = []
        for h in range(SWA_HEADS):
            hk = h // SWA_GROUP
            t = to_half(q[:, (h // 2) * 2 * SWA_HD:(h // 2 + 1) * 2 * SWA_HD], h % 2, hk)
            tiles.append(jnp.where(lo_half if hk == 0 else jnp.logical_not(lo_half), t, 0.0))
        lhs.append(_bf(jnp.concatenate(tiles, axis=0)))
    scores = [lax.dot_general(lhs[s], keys[s], nt_dims, preferred_element_type=F32) + bias for s in range(nb)]
    m = [jnp.maximum(jnp.max(sc, axis=-1, keepdims=True), sink) for sc in scores]
    p = [jnp.exp(sc - mm) for sc, mm in zip(scores, m)]
    rinv = [1.0 / (jnp.sum(pp, axis=-1, keepdims=True) + jnp.exp(sink - mm)) for pp, mm in zip(p, m)]
    res = [jnp.dot(_bf(p[s]), vals[s], preferred_element_type=F32) * rinv[s] for s in range(nb)]
    rows_out = []
    for s in range(nb):
        tiles = []
        for pair in range(SWA_HEADS // 2):
            hk = (2 * pair) // SWA_GROUP
            low = to_half(res[s][(2 * pair) * lt:(2 * pair + 1) * lt], hk, 0)
            high = to_half(res[s][(2 * pair + 1) * lt:(2 * pair + 2) * lt], hk, 1)
            tiles.append(jnp.where(lo_half, low, high))
        rows_out.append(jnp.concatenate(tiles, axis=1))
    o_ref[...] = jnp.concatenate(rows_out, axis=0)


def _swa_sample_call(sinks, sq, sk, sv, cache_k, cache_v, bias, bsz, seq):
    nb = UNIT // seq
    rows = lambda i: (i, 0)
    seqs = lambda i: (i, 0, 0)
    return pl.pallas_call(
        functools.partial(_swa_sample_kernel, nb=nb, lt=seq),
        grid=(bsz // nb,),
        in_specs=[pl.BlockSpec(memory_space=pltpu.SMEM),
                  pl.BlockSpec((UNIT, SWA_Q), rows),
                  pl.BlockSpec((UNIT, SWA_KV), rows),
                  pl.BlockSpec((UNIT, SWA_KV), rows),
                  pl.BlockSpec((nb, WINDOW, SWA_KV), seqs),
                  pl.BlockSpec((nb, WINDOW, SWA_KV), seqs),
                  pl.BlockSpec(BIAS_S_SHAPE, lambda i: (0,) * len(BIAS_S_SHAPE))],
        out_specs=[pl.BlockSpec((UNIT, SWA_Q), rows),
                   pl.BlockSpec((nb, WINDOW, SWA_KV), seqs),
                   pl.BlockSpec((nb, WINDOW, SWA_KV), seqs)],
        out_shape=[jax.ShapeDtypeStruct((bsz * seq, SWA_Q), F32),
                   jax.ShapeDtypeStruct((bsz, WINDOW, SWA_KV), F32),
                   jax.ShapeDtypeStruct((bsz, WINDOW, SWA_KV), F32)],
        compiler_params=_cparams(("arbitrary",)),
        name="swa_cached",
    )(sinks, sq, sk, sv, cache_k, cache_v, bias)


def _post_kernel(*refs, nb, lt, carry):
    if carry:
        (odn_ref, oswa_ref, x_ref, mod_ref, wout_ref, nfw_ref, wup_ref, cw_ref, cb_ref, wdn_ref, fnw_ref,
         y_ref, fbuf_ref, h_scr, x1_scr, acc_scr) = refs
        prev_ref = fbuf_ref

        @pl.when(pl.program_id(1) == 0)
        def _():
            fbuf_ref[...] = jnp.zeros_like(fbuf_ref)
    else:
        (odn_ref, oswa_ref, x_ref, mod_ref, wout_ref, nfw_ref, wup_ref, cw_ref, cb_ref, wdn_ref, fnw_ref,
         prev_ref, y_ref, fbuf_ref, h_scr, x1_scr, acc_scr) = refs
    rows = nb * lt
    attn = (jnp.dot(_bf(odn_ref[...]), wout_ref[0:DN_V, :], preferred_element_type=F32)
            + jnp.dot(_bf(oswa_ref[...]), wout_ref[DN_V:, :], preferred_element_type=F32))
    x1 = x_ref[...].reshape(rows, D_MODEL) + _rows(mod_ref[:, 2:3, :], nb, lt) * attn
    x1_scr[...] = x1
    h = _rms(x1, nfw_ref[...]) * (1.0 + _rows(mod_ref[:, 4:5, :], nb, lt)) + _rows(mod_ref[:, 3:4, :], nb, lt)
    h_scr[...] = _bf(h)
    acc_scr[...] = jnp.zeros_like(acc_scr)
    n_chunks = D_FF // FFN_CHUNK

    def col_slices(c):
        return [slice(base + c * FFN_CHUNK, base + (c + 1) * FFN_CHUNK) for base in (0, D_FF)]

    def up_proj(c):
        return [jnp.dot(h_scr[...], wup_ref[:, cols], preferred_element_type=F32) for cols in col_slices(c)]

    u_next = up_proj(0)
    for c in range(n_chunks):
        u_cur = u_next
        if c + 1 < n_chunks:
            u_next = up_proj(c + 1)
        halves = []
        for u, cols in zip(u_cur, col_slices(c)):
            prev = prev_ref[:, :, cols]
            halves.append(_causal_conv(u, prev, cw_ref[:, cols], nb, lt) + cb_ref[:, cols])
            fbuf_ref[:, :, cols] = _last_rows(u, nb, lt, FFN_CONV - 1)
        act = _silu(halves[0]) * halves[1]
        acc_scr[...] += jnp.dot(_bf(act), wdn_ref[c * FFN_CHUNK:(c + 1) * FFN_CHUNK, :],
                                preferred_element_type=F32)
    x2 = x1_scr[...] + _rows(mod_ref[:, 5:6, :], nb, lt) * acc_scr[...]
    y_ref[...] = _rms(x2, fnw_ref[...]).reshape(nb, lt, D_MODEL)


def _post_call(o_dn, o_swa, x, mod3, w_out, norm_ffn_w, w_up, conv_w, conv_b, w_down, final_w, nb, lt, state=None):
    carry = state is None
    bsz, seq, _ = x.shape
    nt = seq // lt
    rows = nb * lt
    n_state = _state_rows(nb, lt, FFN_CONV - 1)
    row_map = lambda i, j: (i * nt + j, 0)
    const = lambda i, j: (0, 0)
    in_specs = [pl.BlockSpec((rows, DN_V), row_map),
                pl.BlockSpec((rows, SWA_Q), row_map),
                pl.BlockSpec((nb, lt, D_MODEL), lambda i, j: (i, j, 0)),
                pl.BlockSpec((nb, 6, D_MODEL), lambda i, j: (i, 0, 0)),
                pl.BlockSpec((D_MODEL, D_MODEL), const, pipeline_mode=RESIDENT),
                pl.BlockSpec((1, D_MODEL), const),
                pl.BlockSpec((D_MODEL, 2 * D_FF), const, pipeline_mode=RESIDENT),
                pl.BlockSpec((FFN_CONV, 2 * D_FF), const),
                pl.BlockSpec((1, 2 * D_FF), const),
                pl.BlockSpec((D_FF, D_MODEL), const, pipeline_mode=RESIDENT),
                pl.BlockSpec((1, D_MODEL), const)]
    args = [o_dn, o_swa, x, mod3, w_out, norm_ffn_w, w_up, conv_w, conv_b, w_down, final_w]
    if not carry:
        in_specs.append(pl.BlockSpec((nb, FFN_CONV - 1, 2 * D_FF), lambda i, j: (i, 0, 0)))
        args.append(state)
    return pl.pallas_call(
        functools.partial(_post_kernel, nb=nb, lt=lt, carry=carry),
        grid=(bsz // nb, nt),
        in_specs=in_specs,
        out_specs=[pl.BlockSpec((nb, lt, D_MODEL), lambda i, j: (i, j, 0)),
                   pl.BlockSpec((nb, n_state, 2 * D_FF), lambda i, j: (i, 0, 0))],
        out_shape=[jax.ShapeDtypeStruct((bsz, seq, D_MODEL), F32),
                   jax.ShapeDtypeStruct((bsz, n_state, 2 * D_FF), F32)],
        scratch_shapes=[pltpu.VMEM((rows, D_MODEL), BF16),
                        pltpu.VMEM((rows, D_MODEL), F32),
                        pltpu.VMEM((rows, D_MODEL), F32)],
        compiler_params=_cparams(("arbitrary", "arbitrary")),
        name="out_proj_convffn_carry" if carry else "out_proj_convffn_state",
    )(*args)


def _pad_row(vec, offset):
    return jnp.zeros((1, BA_PAD), F32).at[0, offset:offset + vec.shape[0]].set(vec)


def kernel(x_prompt, x_sample, state_dn_conv, state_dn_ssm, cache_swa_k, cache_swa_v, state_ffn_conv, c_prompt, c_sample, rel_bias, final_norm_w, w_ada, b_ada, norm_mix_w, w_in, dn_conv_w, dn_A_log, dn_dt_bias, dn_norm_w, swa_sinks, w_out, norm_ffn_w, ffn_w_up, ffn_conv_w, ffn_conv_b, ffn_w_down):
    bp, lp, _ = x_prompt.shape
    bs, ls, _ = x_sample.shape
    layer = 0

    w_in_t = jnp.transpose(w_in[layer])
    w_out_b = w_out[layer].astype(BF16)
    w_up_b = ffn_w_up[layer].astype(BF16)
    w_dn_b = ffn_w_down[layer].astype(BF16)
    alog_row = _pad_row(dn_A_log[layer], DN_HEADS)
    dt_row = _pad_row(dn_dt_bias[layer], DN_HEADS)
    row = lambda v: v.reshape(1, -1)

    n_c = bp + bs
    n_c_pad = -(-n_c // SUBLANES) * SUBLANES
    c_all = jnp.pad(jnp.concatenate([c_sample, c_prompt], axis=0), ((0, n_c_pad - n_c), (0, 0)))
    mod_s, mod_p = _mod_call(c_all, w_ada[layer], row(b_ada[layer]), bs, bp)

    bias_t, bias_s = _bias_call(rel_bias)
    sinks = swa_sinks[layer]

    def mixer_in(x, mod3, nb, lt, state=None):
        return _in_call(x, mod3, row(norm_mix_w[layer]), w_in_t, dn_conv_w[layer], nb, lt, state)

    def post(o_dn, o_swa, x, mod3, nb, lt, state=None):
        return _post_call(o_dn, o_swa, x, mod3, w_out_b, row(norm_ffn_w[layer]), w_up_b, ffn_conv_w[layer],
                          row(ffn_conv_b[layer]), w_dn_b, row(final_norm_w), nb, lt, state)

    dn_args = (alog_row, dt_row, row(dn_norm_w[layer]))

    qkv, z, ba, sq, sk, sv, p_tail = mixer_in(x_prompt, mod_p, 1, 512)
    o_dn, p_ssm = _dn_call(qkv, z, ba, *dn_args, bp, lp)
    o_swa = _swa_prompt_call(sinks, sq, sk, sv, bias_t, bp, lp)
    y_prompt, p_ffn_tail = post(o_dn, o_swa, x_prompt, mod_p, 1, 256)
    p_dn_conv = p_tail[:, SUBLANES - (DN_CONV - 1):]
    p_fbuf = p_ffn_tail[:, SUBLANES - (FFN_CONV - 1):]
    last_window = lambda t: t.reshape(bp, lp, SWA_KV)[:, lp - WINDOW:].reshape(bp, WINDOW, SWA_KV_HEADS, SWA_HD)
    p_swa_k = last_window(sk)
    p_swa_v = last_window(sv)

    qkv_s, z_s, ba_s, sq_s, sk_s, sv_s, s_dn_conv = mixer_in(x_sample, mod_s, 64, ls, state=state_dn_conv[layer])
    o_dn_s, s_ssm = _dn_call(qkv_s, z_s, ba_s, *dn_args, bs, ls, state=state_dn_ssm[layer])
    o_swa_s, s_k, s_v = _swa_sample_call(sinks, sq_s, sk_s, sv_s,
                                         cache_swa_k[layer].reshape(bs, WINDOW, SWA_KV),
                                         cache_swa_v[layer].reshape(bs, WINDOW, SWA_KV), bias_s, bs, ls)
    y_sample, s_fbuf = post(o_dn_s, o_swa_s, x_sample, mod_s, 32, ls, state=state_ffn_conv[layer])

    return (y_prompt, y_sample, p_dn_conv[None], s_dn_conv[None], p_ssm[None], s_ssm[None],
            p_swa_k[None], s_k.reshape(bs, WINDOW, SWA_KV_HEADS, SWA_HD)[None],
            p_swa_v[None], s_v.reshape(bs, WINDOW, SWA_KV_HEADS, SWA_HD)[None],
            p_fbuf[None], s_fbuf[None])
```

```python
import functools
import math

import numpy as np
import jax
import jax.numpy as jnp
from jax import lax
from jax.experimental import pallas as pl
from jax.experimental.pallas import tpu as pltpu

F32 = jnp.float32
BF16 = jnp.bfloat16

D_MODEL = 1024
PAST_LEN = 16384
DN_HEADS = 4
DN_DK = 128
DN_DV = 128
DN_CONV = 4
SWA_HEADS = 8
SWA_KV_HEADS = 2
SWA_GROUP = SWA_HEADS // SWA_KV_HEADS
SWA_HD = 64
WINDOW = 128
N_BUCKETS = 32
MAX_DISTANCE = 128
D_FF = 2816
FFN_CONV = 3
EPS = 1e-6
NEG_INF = -1e30

DN_QK = DN_HEADS * DN_DK
DN_V = DN_HEADS * DN_DV
DN_CONV_CH = 2 * DN_QK + DN_V
SWA_Q = SWA_HEADS * SWA_HD
SWA_KV = SWA_KV_HEADS * SWA_HD
BA_PAD = 128
SUBLANES = 8
UNIT = 128
POST_SUB_TILES = 2
POST_LOOKAHEAD_CHUNKS = 3
FFN_CHUNK = 256
VMEM_LIMIT = 56 * 1024 * 1024
RESIDENT = pl.Buffered(1)


def _cparams(sem):
    return pltpu.CompilerParams(dimension_semantics=sem, vmem_limit_bytes=VMEM_LIMIT)


def _bf(x):
    return x.astype(BF16)


def _dot(a, b):
    return jnp.dot(_bf(a), _bf(b), preferred_element_type=F32)


def _dot_nt(a, b):
    return lax.dot_general(_bf(a), _bf(b), (((1,), (1,)), ((), ())), preferred_element_type=F32)


def _split2(x):
    hi = _bf(x)
    lo = _bf(x - hi.astype(F32))
    return hi, lo


def _dot_x3(a, b):
    ah, al = _split2(a)
    bh, bl = _split2(b)
    d = functools.partial(jnp.dot, preferred_element_type=F32)
    return d(ah, bh) + (d(al, bh) + d(ah, bl))


def _dot_mask(m, x):
    hi = _bf(x)
    r = x - hi.astype(F32)
    mid = _bf(r)
    lo = _bf(r - mid.astype(F32))
    d = functools.partial(jnp.dot, preferred_element_type=F32)
    return d(m, hi) + (d(m, mid) + d(m, lo))


def _sigmoid(x):
    return 1.0 / (1.0 + jnp.exp(-x))


def _silu(x):
    return x * _sigmoid(x)


def _softplus(x):
    return jnp.maximum(x, 0.0) + jnp.log1p(jnp.exp(-jnp.abs(x)))


def _rms(x, w):
    ms = jnp.mean(x * x, axis=-1, keepdims=True)
    return x * lax.rsqrt(ms + EPS) * w


def _l2norm(t):
    return t * lax.rsqrt(jnp.sum(t * t, axis=-1, keepdims=True) + EPS)


def _rows(m3, nb, lt):
    return jnp.broadcast_to(m3, (nb, lt, m3.shape[-1])).reshape(nb * lt, m3.shape[-1])


def _causal_conv(x, prev, w, nb, lt):
    width = w.shape[0]
    rows, ch = x.shape
    if nb == 1 and lt > SUBLANES:
        tiles = jnp.concatenate([prev, x.reshape(lt // SUBLANES, SUBLANES, ch)], axis=0)
        sub = lax.broadcasted_iota(jnp.int32, (1, SUBLANES, 1), 1)
        out = tiles[1:] * w[width - 1:width, :]
        for j in range(1, width):
            rot = pltpu.roll(tiles, j, axis=1)
            out = out + jnp.where(sub >= j, rot[1:], rot[:-1]) * w[width - 1 - j:width - j, :]
        return out.reshape(rows, ch)
    tmod = lax.broadcasted_iota(jnp.int32, (rows, 1), 0) & (lt - 1)
    out = x * w[width - 1:width, :]
    for j in range(1, width):
        sh = pltpu.roll(x, j, axis=0)
        for t in range(j):
            p = width - 1 - j + t
            sh = jnp.where(tmod == t, _rows(prev[:, p:p + 1, :], nb, lt), sh)
        out = out + sh * w[width - 1 - j:width - j, :]
    return out


def _last_rows(x, nb, lt, n_state):
    ch = x.shape[-1]
    if nb == 1 and lt > SUBLANES:
        return x[lt - SUBLANES:].reshape(1, SUBLANES, ch)
    return x.reshape(nb, lt, ch)[:, lt - n_state:, :]


def _state_rows(nb, lt, n_state):
    return SUBLANES if (nb == 1 and lt > SUBLANES) else n_state


MOD_PARTS = 6


def _mod_kernel(c_ref, w_ref, b_ref, os_ref, op_ref, *, n_s, n_p):
    part = pl.program_id(0)
    res = _dot_x3(_silu(c_ref[...]), w_ref[...]) + b_ref[...]
    for k in range(MOD_PARTS):
        @pl.when(part == k)
        def _():
            os_ref[:, k, :] = res[:n_s]
            op_ref[:, k, :] = res[n_s:n_s + n_p]


def _mod_call(c_all, w_ada, b_ada, n_s, n_p):
    rows = c_all.shape[0]
    return pl.pallas_call(
        functools.partial(_mod_kernel, n_s=n_s, n_p=n_p),
        grid=(MOD_PARTS,),
        in_specs=[pl.BlockSpec((rows, D_MODEL), lambda k: (0, 0)),
                  pl.BlockSpec((D_MODEL, D_MODEL), lambda k: (0, k)),
                  pl.BlockSpec((1, D_MODEL), lambda k: (0, k))],
        out_specs=[pl.BlockSpec((n_s, MOD_PARTS, D_MODEL), lambda k: (0, 0, 0)),
                   pl.BlockSpec((n_p, MOD_PARTS, D_MODEL), lambda k: (0, 0, 0))],
        out_shape=[jax.ShapeDtypeStruct((n_s, MOD_PARTS, D_MODEL), F32),
                   jax.ShapeDtypeStruct((n_p, MOD_PARTS, D_MODEL), F32)],
        compiler_params=_cparams(("arbitrary",)),
        name="adaln_mod",
    )(c_all, w_ada, b_ada)


IN_SPLIT = (DN_CONV_CH, DN_V, BA_PAD, SWA_Q, SWA_KV, SWA_KV)


W_PREP_ROWS = 256
IN_SLAB = 2 * DN_DK


def _in_kernel(*refs, nb, lt, carry):
    if carry:
        (x_ref, mod_ref, nw_ref, w_ref, cw_ref,
         qkv_ref, z_ref, ba_ref, sq_ref, sk_ref, sv_ref, tail_ref, h_scr, wdn_ref, wba_ref, wsw_ref) = refs
        prev_ref = tail_ref

        @pl.when(pl.program_id(1) == 0)
        def _():
            tail_ref[...] = jnp.zeros_like(tail_ref)
    else:
        (x_ref, mod_ref, nw_ref, w_ref, cw_ref, prev_ref,
         qkv_ref, z_ref, ba_ref, sq_ref, sk_ref, sv_ref, tail_ref, h_scr, wdn_ref, wba_ref, wsw_ref) = refs

    @pl.when((pl.program_id(0) == 0) & (pl.program_id(1) == 0))
    def _():
        ba_lo = DN_CONV_CH + DN_V
        n_ba = 2 * DN_HEADS
        for r in range(0, ba_lo, W_PREP_ROWS):
            wdn_ref[:, r:r + W_PREP_ROWS] = _bf(w_ref[r:r + W_PREP_ROWS, :].T)
        row = lax.broadcasted_iota(jnp.int32, (BA_PAD, 1), 0)
        wba_ref[...] = _bf(jnp.where(row < n_ba, w_ref[ba_lo:ba_lo + BA_PAD, :], 0.0).T)
        for r in range(0, SWA_Q + 2 * SWA_KV, W_PREP_ROWS):
            wsw_ref[:, r:r + W_PREP_ROWS] = _bf(w_ref[ba_lo + n_ba + r:ba_lo + n_ba + r + W_PREP_ROWS, :].T)

    rows = nb * lt
    x = x_ref[...]
    ms = jnp.mean(x * x, axis=-1, keepdims=True)
    y = x * lax.rsqrt(ms + EPS) * nw_ref[...]
    h = y * (1.0 + mod_ref[:, 1:2, :]) + mod_ref[:, 0:1, :]
    h_scr[...] = _bf(h.reshape(rows, D_MODEL))

    def proj(w_bf_ref, lo, n):
        return jnp.dot(h_scr[...], w_bf_ref[:, lo:lo + n], preferred_element_type=F32)

    def plain_proj(i):
        if i < 2:
            z_ref[:, i * IN_SLAB:(i + 1) * IN_SLAB] = proj(wdn_ref, DN_CONV_CH + i * IN_SLAB, IN_SLAB)
        elif i < 4:
            sq_ref[:, (i - 2) * IN_SLAB:(i - 1) * IN_SLAB] = proj(wsw_ref, (i - 2) * IN_SLAB, IN_SLAB)
        elif i == 4:
            kv = proj(wsw_ref, SWA_Q, 2 * SWA_KV)
            sk_ref[...] = kv[:, :SWA_KV]
            sv_ref[...] = kv[:, SWA_KV:]
        else:
            ba_ref[...] = proj(wba_ref, 0, BA_PAD)

    n_slabs = DN_CONV_CH // IN_SLAB
    raw_next = proj(wdn_ref, 0, IN_SLAB)
    for slab in range(n_slabs):
        raw = raw_next
        plain_proj(slab)
        if slab + 1 < n_slabs:
            raw_next = proj(wdn_ref, (slab + 1) * IN_SLAB, IN_SLAB)
        cols = slice(slab * IN_SLAB, (slab + 1) * IN_SLAB)
        c = _silu(_causal_conv(raw, prev_ref[:, :, cols], cw_ref[:, cols], nb, lt))
        tail_ref[:, :, cols] = _last_rows(raw, nb, lt, DN_CONV - 1)
        if slab * IN_SLAB < 2 * DN_QK:
            scale = DN_DK ** -0.5 if slab * IN_SLAB < DN_QK else 1.0
            c = jnp.concatenate([_l2norm(c[:, i * DN_DK:(i + 1) * DN_DK]) * scale
                                 for i in range(IN_SLAB // DN_DK)], axis=1)
        qkv_ref[:, cols] = c


def _in_call(x, mod3, norm_w, w_in_t, conv_w, nb, lt, state=None):
    carry = state is None
    bsz, seq, _ = x.shape
    nt = seq // lt
    rows = nb * lt
    n_tok = bsz * seq
    n_state = _state_rows(nb, lt, DN_CONV - 1)
    row_map = lambda i, j: (i * nt + j, 0)
    const = lambda i, j: (0, 0)
    per_seq = lambda i, j: (i, 0, 0)
    in_specs = [pl.BlockSpec((nb, lt, D_MODEL), lambda i, j: (i, j, 0)),
                pl.BlockSpec((nb, 6, D_MODEL), per_seq),
                pl.BlockSpec((1, D_MODEL), const),
                pl.BlockSpec(w_in_t.shape, const, pipeline_mode=RESIDENT),
                pl.BlockSpec((DN_CONV, DN_CONV_CH), const)]
    args = [x, mod3, norm_w, w_in_t, conv_w]
    if not carry:
        in_specs.append(pl.BlockSpec((nb, DN_CONV - 1, DN_CONV_CH), per_seq))
        args.append(state)
    return pl.pallas_call(
        functools.partial(_in_kernel, nb=nb, lt=lt, carry=carry),
        grid=(bsz // nb, nt),
        in_specs=in_specs,
        out_specs=[pl.BlockSpec((rows, n), row_map) for n in IN_SPLIT]
        + [pl.BlockSpec((nb, n_state, DN_CONV_CH), per_seq)],
        out_shape=[jax.ShapeDtypeStruct((n_tok, n), F32) for n in IN_SPLIT]
        + [jax.ShapeDtypeStruct((bsz, n_state, DN_CONV_CH), F32)],
        scratch_shapes=[pltpu.VMEM((rows, D_MODEL), BF16),
                        pltpu.VMEM((D_MODEL, DN_CONV_CH + DN_V), BF16),
                        pltpu.VMEM((D_MODEL, BA_PAD), BF16),
                        pltpu.VMEM((D_MODEL, SWA_Q + 2 * SWA_KV), BF16)],
        compiler_params=_cparams(("arbitrary", "arbitrary")),
        name="norm_in_proj_carry" if carry else "norm_in_proj_state",
    )(*args)


INV_BASE_SHIFT = 2
DN_UNITS_CARRY = 4
SWA_BLOCKS_PER_STEP = 4
DN_UNITS_STATE = 1


def _unit_lower_inverses(a_mats, ri, ci, chunk_shift):
    def blocks(s):
        return (ri >> s) == (ci >> s)

    base = min(INV_BASE_SHIFT, chunk_shift)
    xs = [jnp.where(blocks(base), -a, 0.0) for a in a_mats]
    ts = [jnp.where(ri == ci, 1.0, n) for n in xs]
    for lvl in range(1, base):
        xs = [_dot(x, x) for x in xs]
        ts = [t + _dot(t, x) for t, x in zip(ts, xs)]
    for s in range(base, chunk_shift):
        sel = blocks(s + 1) & jnp.logical_not(blocks(s))
        ets = [_dot(jnp.where(sel, a, 0.0), t) for a, t in zip(a_mats, ts)]
        ts = [t - _dot(t, et) for t, et in zip(ts, ets)]
    return ts


def _dn_kernel(*refs, nb, chunk, carry, units):
    if carry:
        (qkv_ref, z_ref, ba_ref, alog_ref, dt_ref, nw_ref, o_ref, s_ref) = refs
        s0_ref = s_ref

        @pl.when(pl.program_id(1) == 0)
        def _():
            s_ref[...] = jnp.zeros_like(s_ref)
    else:
        (qkv_ref, z_ref, ba_ref, alog_ref, dt_ref, nw_ref, s0_ref, o_ref, s_ref) = refs

    ri = lax.broadcasted_iota(jnp.int32, (UNIT, UNIT), 0)
    ci = lax.broadcasted_iota(jnp.int32, (UNIT, UNIT), 1)
    shift = int(math.log2(chunk))
    same = (ri >> shift) == (ci >> shift)
    incl = same & (ri >= ci)
    strict = same & (ri > ci)

    ba = ba_ref[...]
    beta_full = _sigmoid(ba)
    g_full = -jnp.exp(alog_ref[...]) * _softplus(ba + dt_ref[...])
    masks = jnp.concatenate([jnp.where(incl, 1.0, 0.0), jnp.where(same, 1.0, 0.0)], axis=0).astype(BF16)
    gsums = [_dot_mask(masks, g_full[u * UNIT:(u + 1) * UNIT]) for u in range(units)]
    g_cum = [g[:UNIT] for g in gsums]
    g_tot = [g[UNIT:] for g in gsums]
    g_cum_t = [g.T for g in g_cum]

    probs = [(u, h) for u in range(units) for h in range(DN_HEADS)]
    n_p = len(probs)
    rs = lambda u: slice(u * UNIT, (u + 1) * UNIT)

    def head_cols(base):
        return [qkv_ref[rs(u), base + h * DN_DK:base + (h + 1) * DN_DK] for u, h in probs]

    q = head_cols(0)
    k = head_cols(DN_QK)
    v = head_cols(2 * DN_QK)
    gc = [g_cum[u][:, DN_HEADS + h:DN_HEADS + h + 1] for u, h in probs]
    gr = [g_cum_t[u][DN_HEADS + h:DN_HEADS + h + 1, :] for u, h in probs]
    gt = [g_tot[u][:, DN_HEADS + h:DN_HEADS + h + 1] for u, h in probs]
    bc = [beta_full[rs(u), h:h + 1] for u, h in probs]
    decay = [jnp.where(incl, jnp.exp(jnp.where(incl, gc[p] - gr[p], 0.0)), 0.0) for p in range(n_p)]
    e_g = [jnp.exp(gc[p]) for p in range(n_p)]
    kq = [_dot_nt(jnp.concatenate([k[p], q[p]], axis=0), k[p]) for p in range(n_p)]
    qk = [kq[p][UNIT:] * decay[p] for p in range(n_p)]
    a_mats = [jnp.where(strict, bc[p] * kq[p][:UNIT] * decay[p], 0.0) for p in range(n_p)]
    t_inv = _unit_lower_inverses(a_mats, ri, ci, shift)
    wvk = [_dot(t_inv[p], jnp.concatenate([v[p] * bc[p], k[p] * (bc[p] * e_g[p])], axis=1)) for p in range(n_p)]
    w_v = [w[:, :DN_DV] for w in wvk]
    w_k = [w[:, DN_DV:] for w in wvk]
    q_dec = [q[p] * e_g[p] for p in range(n_p)]
    k_tail_t = [(k[p] * jnp.exp(gt[p] - gc[p])).T for p in range(n_p)]
    c_dec = [jnp.exp(gt[p]) for p in range(n_p)]

    outs = {}
    if carry:
        state = [s_ref[0, h] for h in range(DN_HEADS)]
        for u in range(units):
            ps = [u * DN_HEADS + h for h in range(DN_HEADS)]
            r = [_dot(jnp.concatenate([w_k[p], q_dec[p]], axis=0), state[h]) for h, p in enumerate(ps)]
            uu = [w_v[p] - r[h][:UNIT] for h, p in enumerate(ps)]
            for h, p in enumerate(ps):
                outs[p] = r[h][UNIT:] + _dot(qk[p], uu[h])
            state = [state[h] * c_dec[p][0:1, :] + _dot(k_tail_t[p], uu[h]) for h, p in enumerate(ps)]
        for h in range(DN_HEADS):
            s_ref[0, h] = state[h]
    else:
        per_unit = nb // units
        for p, (u, h) in enumerate(probs):
            us, qs = [], []
            for s in range(per_unit):
                lo = s * chunk
                lhs = jnp.concatenate([w_k[p][lo:lo + chunk], q_dec[p][lo:lo + chunk]], axis=0)
                r = jnp.dot(lhs, s0_ref[u * per_unit + s, h], preferred_element_type=F32)
                us.append(w_v[p][lo:lo + chunk] - r[:chunk])
                qs.append(r[chunk:])
            uu = jnp.concatenate(us, axis=0)
            outs[p] = jnp.concatenate(qs, axis=0) + _dot(qk[p], uu)
            for s in range(per_unit):
                lo = s * chunk
                upd = jnp.dot(k_tail_t[p][:, lo:lo + chunk], uu[lo:lo + chunk], preferred_element_type=F32)
                s_ref[u * per_unit + s, h] = s0_ref[u * per_unit + s, h] * c_dec[p][lo:lo + 1, :] + upd

    for p, (u, h) in enumerate(probs):
        zz = z_ref[rs(u), h * DN_DV:(h + 1) * DN_DV]
        o_ref[rs(u), h * DN_DV:(h + 1) * DN_DV] = _rms(outs[p], nw_ref[...]) * _silu(zz)


def _dn_call(qkv, z, ba, alog_row, dt_row, norm_w, bsz, seq, state=None):
    carry = state is None
    units = DN_UNITS_CARRY if carry else DN_UNITS_STATE
    rows = units * UNIT
    nb = 1 if carry else rows // seq
    chunk = UNIT if carry else seq
    nt = seq * nb // rows
    n_tok = bsz * seq
    row_map = lambda i, j: (i * nt + j, 0)
    const = lambda i, j: (0, 0)
    in_specs = [pl.BlockSpec((rows, DN_CONV_CH), row_map),
                pl.BlockSpec((rows, DN_V), row_map),
                pl.BlockSpec((rows, BA_PAD), row_map),
                pl.BlockSpec((1, BA_PAD), const),
                pl.BlockSpec((1, BA_PAD), const),
                pl.BlockSpec((1, DN_DV), const)]
    args = [qkv, z, ba, alog_row, dt_row, norm_w]
    if not carry:
        in_specs.append(pl.BlockSpec((nb, DN_HEADS, DN_DK, DN_DV), lambda i, j: (i, 0, 0, 0)))
        args.append(state)
    return pl.pallas_call(
        functools.partial(_dn_kernel, nb=nb, chunk=chunk, carry=carry, units=units),
        grid=(bsz // nb, nt),
        in_specs=in_specs,
        out_specs=[pl.BlockSpec((rows, DN_V), row_map),
                   pl.BlockSpec((nb, DN_HEADS, DN_DK, DN_DV), lambda i, j: (i, 0, 0, 0))],
        out_shape=[jax.ShapeDtypeStruct((n_tok, DN_V), F32),
                   jax.ShapeDtypeStruct((bsz, DN_HEADS, DN_DK, DN_DV), F32)],
        compiler_params=_cparams(("arbitrary", "arbitrary")),
        name="gated_deltanet_carry" if carry else "gated_deltanet_state",
    )(*args)


def _bucket_table():
    i = np.arange(WINDOW, dtype=np.int64)[:, None]
    j = np.arange(2 * WINDOW, dtype=np.int64)[None, :]
    d = np.maximum(i + WINDOW - j, 0)
    exact = N_BUCKETS // 2
    logv = (np.log(np.maximum(d, 1).astype(np.float32) / np.float32(exact)).astype(np.float32)
            / np.float32(math.log(MAX_DISTANCE / exact)))
    large = np.minimum(exact + (logv * np.float32(N_BUCKETS - exact)).astype(np.int32), N_BUCKETS - 1)
    return np.where(d < exact, d, large).astype(np.int32)


def _bias_lookup(rb_ref, bucket, head):
    acc = jnp.zeros(bucket.shape, F32)
    for b in range(N_BUCKETS):
        acc = jnp.where(bucket == b, rb_ref[b, head], acc)
    return acc


def _bias_kernel(rb_ref, bucket_t_ref, bucket_ref, ot_ref, os_ref):
    bucket_t = bucket_t_ref[...]
    kj = lax.broadcasted_iota(jnp.int32, bucket_t.shape, 0)
    qi = lax.broadcasted_iota(jnp.int32, bucket_t.shape, 1)
    dist = qi + WINDOW - kj
    valid = (dist >= 0) & (dist < WINDOW)
    for hk in range(SWA_KV_HEADS):
        for par in range(2):
            for st in range(SWA_GROUP // 2):
                head = hk * SWA_GROUP + 2 * st + par
                gen = jnp.where(valid, _bias_lookup(rb_ref, bucket_t, head), NEG_INF)
                ot_ref[1, hk, par, :, st * WINDOW:(st + 1) * WINDOW] = gen
                ot_ref[0, hk, par, :, st * WINDOW:(st + 1) * WINDOW] = jnp.where(kj >= WINDOW, gen, NEG_INF)
    bucket = bucket_ref[...]
    qi = lax.broadcasted_iota(jnp.int32, bucket.shape, 0)
    kj = lax.broadcasted_iota(jnp.int32, bucket.shape, 1)
    dist = qi + WINDOW - kj
    valid = (dist >= 0) & (dist < WINDOW)
    for head in range(SWA_HEADS):
        os_ref[head] = jnp.where(valid, _bias_lookup(rb_ref, bucket, head), NEG_INF)


BIAS_T_SHAPE = (2, SWA_KV_HEADS, 2, 2 * WINDOW, (SWA_GROUP // 2) * WINDOW)
BIAS_S_SHAPE = (SWA_HEADS, SUBLANES, 2 * WINDOW)


def _bias_call(rel_bias):
    bucket = _bucket_table()
    return pl.pallas_call(
        _bias_kernel,
        in_specs=[pl.BlockSpec(memory_space=pltpu.SMEM),
                  pl.BlockSpec((2 * WINDOW, WINDOW), lambda: (0, 0)),
                  pl.BlockSpec((SUBLANES, 2 * WINDOW), lambda: (0, 0))],
        out_specs=[pl.BlockSpec(BIAS_T_SHAPE, lambda: (0,) * len(BIAS_T_SHAPE)),
                   pl.BlockSpec(BIAS_S_SHAPE, lambda: (0,) * len(BIAS_S_SHAPE))],
        out_shape=[jax.ShapeDtypeStruct(BIAS_T_SHAPE, F32), jax.ShapeDtypeStruct(BIAS_S_SHAPE, F32)],
        name="swa_rel_bias_table",
    )(rel_bias, jnp.asarray(np.ascontiguousarray(bucket.T)), jnp.asarray(bucket[:SUBLANES]))


def _softmax_sink_parts(s, sink):
    m = jnp.maximum(jnp.max(s, axis=0, keepdims=True), sink)
    p = jnp.exp(s - m)
    return p, 1.0 / (jnp.sum(p, axis=0, keepdims=True) + jnp.exp(sink - m))


def _half_lane_variants(full, hk, lo_half):
    rolled = pltpu.roll(full, SWA_HD, axis=1)
    low_src, high_src = (full, rolled) if hk == 0 else (rolled, full)
    return jnp.where(lo_half, low_src, 0.0), jnp.where(lo_half, 0.0, high_src)


def _swa_prompt_kernel(sink_ref, q_ref, kp_ref, kc_ref, vp_ref, vc_ref, bias_ref, o_ref, *, n_blk):
    step = pl.program_id(1)
    lo_half = lax.broadcasted_iota(jnp.int32, (1, 2 * SWA_HD), 1) < SWA_HD
    lo_rows = lax.broadcasted_iota(jnp.int32, (2 * SWA_HD, 1), 0) < SWA_HD
    q = _bf(q_ref[...] * (SWA_HD ** -0.5))
    keys = jnp.concatenate([kp_ref[...], kc_ref[...]], axis=0)
    vals = jnp.concatenate([vp_ref[...], vc_ref[...]], axis=0)
    k_var = [[_bf(t) for t in _half_lane_variants(keys, hk, lo_half)] for hk in range(SWA_KV_HEADS)]
    v_var_t = [[_bf(t.T) for t in _half_lane_variants(vals, hk, lo_half)] for hk in range(SWA_KV_HEADS)]
    n_stack = SWA_GROUP // 2
    sinks = [[jnp.concatenate([jnp.full((1, WINDOW), sink_ref[hk * SWA_GROUP + 2 * st + par], F32)
                               for st in range(n_stack)], axis=1) for par in range(2)]
             for hk in range(SWA_KV_HEADS)]
    nt_dims = (((1,), (1,)), ((), ()))

    def scores(b):
        rows = slice(b * WINDOW, (b + 1) * WINDOW)
        win = slice(b * WINDOW, (b + 2) * WINDOW)
        variant = jnp.where(step == 0, 0, 1) if b == 0 else 1
        out = []
        for hk in range(SWA_KV_HEADS):
            q2 = jnp.concatenate([q[rows, (hk * n_stack + st) * 2 * SWA_HD:(hk * n_stack + st + 1) * 2 * SWA_HD]
                                  for st in range(n_stack)], axis=0)
            out.append([lax.dot_general(k_var[hk][par][win], q2, nt_dims, preferred_element_type=F32)
                        + bias_ref[variant, hk, par] for par in range(2)])
        return out

    s_next = scores(0)
    for b in range(n_blk):
        s_cur = s_next
        if b + 1 < n_blk:
            s_next = scores(b + 1)
        rows = slice(b * WINDOW, (b + 1) * WINDOW)
        win = slice(b * WINDOW, (b + 2) * WINDOW)
        for hk in range(SWA_KV_HEADS):
            parts = [_softmax_sink_parts(s_cur[hk][par], sinks[hk][par]) for par in range(2)]
            o_t = (jnp.dot(v_var_t[hk][0][:, win], _bf(parts[0][0]), preferred_element_type=F32)
                   + jnp.dot(v_var_t[hk][1][:, win], _bf(parts[1][0]), preferred_element_type=F32))
            o_t = o_t * jnp.where(lo_rows, parts[0][1], parts[1][1])
            for st in range(n_stack):
                lo = (hk * n_stack + st) * 2 * SWA_HD
                o_ref[rows, lo:lo + 2 * SWA_HD] = o_t[:, st * WINDOW:(st + 1) * WINDOW].T


def _swa_prompt_call(sinks, sq, sk, sv, bias, bsz, seq):
    n_blk = SWA_BLOCKS_PER_STEP
    tile = n_blk * WINDOW
    nt = seq // tile
    cur = lambda b, i: (b * nt + i, 0)
    prv = lambda b, i: (b * nt * n_blk + jnp.maximum(i * n_blk - 1, 0), 0)
    return pl.pallas_call(
        functools.partial(_swa_prompt_kernel, n_blk=n_blk),
        grid=(bsz, nt),
        in_specs=[pl.BlockSpec(memory_space=pltpu.SMEM),
                  pl.BlockSpec((tile, SWA_Q), cur),
                  pl.BlockSpec((WINDOW, SWA_KV), prv),
                  pl.BlockSpec((tile, SWA_KV), cur),
                  pl.BlockSpec((WINDOW, SWA_KV), prv),
                  pl.BlockSpec((tile, SWA_KV), cur),
                  pl.BlockSpec(BIAS_T_SHAPE, lambda b, i: (0,) * len(BIAS_T_SHAPE))],
        out_specs=pl.BlockSpec((tile, SWA_Q), cur),
        out_shape=jax.ShapeDtypeStruct((bsz * seq, SWA_Q), F32),
        compiler_params=_cparams(("arbitrary", "arbitrary")),
        name="swa_banded",
    )(sinks, sq, sk, sk, sv, sv, bias)


def _swa_sample_kernel(sink_ref, q_ref, kn_ref, vn_ref, kc_ref, vc_ref, bias_ref, o_ref, ko_ref, vo_ref, *, nb, lt):
    lo_half = lax.broadcasted_iota(jnp.int32, (1, 2 * SWA_HD), 1) < SWA_HD
    q_all = q_ref[...] * (SWA_HD ** -0.5)
    kn_all = kn_ref[...]
    vn_all = vn_ref[...]
    pad = jnp.zeros((WINDOW - lt, SWA_KV), F32)
    bias = jnp.concatenate([bias_ref[h, 0:lt, :] for h in range(SWA_HEADS)], axis=0)
    sink = jnp.concatenate([jnp.full((lt, 1), sink_ref[h], F32) for h in range(SWA_HEADS)], axis=0)
    nt_dims = (((1,), (1,)), ((), ()))

    def to_half(tile, src_half, dst_half):
        return tile if src_half == dst_half else pltpu.roll(tile, SWA_HD, axis=1)

    lhs, keys, vals = [], [], []
    for s in range(nb):
        rows = slice(s * lt, (s + 1) * lt)
        kc, vc, kn, vn = kc_ref[s], vc_ref[s], kn_all[rows], vn_all[rows]
        ko_ref[s] = jnp.concatenate([kc[lt:], kn], axis=0)
        vo_ref[s] = jnp.concatenate([vc[lt:], vn], axis=0)
        keys.append(_bf(jnp.concatenate([kc, kn, pad], axis=0)))
        vals.append(_bf(jnp.concatenate([vc, vn, pad], axis=0)))
        q = q_all[rows]
        tiles = []
        for h in range(SWA_HEADS):
            hk = h // SWA_GROUP
            t = to_half(q[:, (h // 2) * 2 * SWA_HD:(h // 2 + 1) * 2 * SWA_HD], h % 2, hk)
            tiles.append(jnp.where(lo_half if hk == 0 else jnp.logical_not(lo_half), t, 0.0))
        lhs.append(_bf(jnp.concatenate(tiles, axis=0)))
    scores = [lax.dot_general(lhs[s], keys[s], nt_dims, preferred_element_type=F32) + bias for s in range(nb)]
    m = [jnp.maximum(jnp.max(sc, axis=-1, keepdims=True), sink) for sc in scores]
    p = [jnp.exp(sc - mm) for sc, mm in zip(scores, m)]
    rinv = [1.0 / (jnp.sum(pp, axis=-1, keepdims=True) + jnp.exp(sink - mm)) for pp, mm in zip(p, m)]
    res = [jnp.dot(_bf(p[s]), vals[s], preferred_element_type=F32) * rinv[s] for s in range(nb)]
    rows_out = []
    for s in range(nb):
        tiles = []
        for pair in range(SWA_HEADS // 2):
            hk = (2 * pair) // SWA_GROUP
            low = to_half(res[s][(2 * pair) * lt:(2 * pair + 1) * lt], hk, 0)
            high = to_half(res[s][(2 * pair + 1) * lt:(2 * pair + 2) * lt], hk, 1)
            tiles.append(jnp.where(lo_half, low, high))
        rows_out.append(jnp.concatenate(tiles, axis=1))
    o_ref[...] = jnp.concatenate(rows_out, axis=0)


def _swa_sample_call(sinks, sq, sk, sv, cache_k, cache_v, bias, bsz, seq):
    nb = UNIT // seq
    rows = lambda i: (i, 0)
    seqs = lambda i: (i, 0, 0)
    return pl.pallas_call(
        functools.partial(_swa_sample_kernel, nb=nb, lt=seq),
        grid=(bsz // nb,),
        in_specs=[pl.BlockSpec(memory_space=pltpu.SMEM),
                  pl.BlockSpec((UNIT, SWA_Q), rows),
                  pl.BlockSpec((UNIT, SWA_KV), rows),
                  pl.BlockSpec((UNIT, SWA_KV), rows),
                  pl.BlockSpec((nb, WINDOW, SWA_KV), seqs),
                  pl.BlockSpec((nb, WINDOW, SWA_KV), seqs),
                  pl.BlockSpec(BIAS_S_SHAPE, lambda i: (0,) * len(BIAS_S_SHAPE))],
        out_specs=[pl.BlockSpec((UNIT, SWA_Q), rows),
                   pl.BlockSpec((nb, WINDOW, SWA_KV), seqs),
                   pl.BlockSpec((nb, WINDOW, SWA_KV), seqs)],
        out_shape=[jax.ShapeDtypeStruct((bsz * seq, SWA_Q), F32),
                   jax.ShapeDtypeStruct((bsz, WINDOW, SWA_KV), F32),
                   jax.ShapeDtypeStruct((bsz, WINDOW, SWA_KV), F32)],
        compiler_params=_cparams(("arbitrary",)),
        name="swa_cached",
    )(sinks, sq, sk, sv, cache_k, cache_v, bias)


def _post_kernel(*refs, nb, lt, carry, n_sub):
    if carry:
        (odn_ref, oswa_ref, x_ref, mod_ref, wout_ref, nfw_ref, wup_ref, cw_ref, cb_ref, wdn_ref, fnw_ref,
         y_ref, fbuf_ref, h_scr, x1_scr, acc_scr) = refs
        prev_ref = fbuf_ref

        @pl.when(pl.program_id(1) == 0)
        def _():
            fbuf_ref[...] = jnp.zeros_like(fbuf_ref)
    else:
        (odn_ref, oswa_ref, x_ref, mod_ref, wout_ref, nfw_ref, wup_ref, cw_ref, cb_ref, wdn_ref, fnw_ref,
         prev_ref, y_ref, fbuf_ref, h_scr, x1_scr, acc_scr) = refs
    rows = nb * lt
    n_chunks = D_FF // FFN_CHUNK

    def col_slices(c):
        return [slice(base + c * FFN_CHUNK, base + (c + 1) * FFN_CHUNK) for base in (0, D_FF)]

    def prologue(sub):
        rr = slice(sub * rows, (sub + 1) * rows)
        attn = (jnp.dot(_bf(odn_ref[rr, :]), wout_ref[0:DN_V, :], preferred_element_type=F32)
                + jnp.dot(_bf(oswa_ref[rr, :]), wout_ref[DN_V:, :], preferred_element_type=F32))
        x = x_ref[:, sub * lt:(sub + 1) * lt, :].reshape(rows, D_MODEL)
        x1 = x + _rows(mod_ref[:, 2:3, :], nb, lt) * attn
        x1_scr[sub] = x1
        h = (_rms(x1, nfw_ref[...]) * (1.0 + _rows(mod_ref[:, 4:5, :], nb, lt))
             + _rows(mod_ref[:, 3:4, :], nb, lt))
        h_scr[sub] = _bf(h)
        acc_scr[sub] = jnp.zeros((rows, D_MODEL), F32)

    def up_proj(sub, c):
        return [jnp.dot(h_scr[sub], wup_ref[:, cols], preferred_element_type=F32) for cols in col_slices(c)]

    def chunk(sub, c, u_cur):
        halves = []
        for u, cols in zip(u_cur, col_slices(c)):
            prev = prev_ref[:, :, cols]
            halves.append(_causal_conv(u, prev, cw_ref[:, cols], nb, lt) + cb_ref[:, cols])
            fbuf_ref[:, :, cols] = _last_rows(u, nb, lt, FFN_CONV - 1)
        act = _silu(halves[0]) * halves[1]
        acc_scr[sub] += jnp.dot(_bf(act), wdn_ref[c * FFN_CHUNK:(c + 1) * FFN_CHUNK, :],
                                preferred_element_type=F32)

    def epilogue(sub):
        x2 = x1_scr[sub] + _rows(mod_ref[:, 5:6, :], nb, lt) * acc_scr[sub]
        y_ref[:, sub * lt:(sub + 1) * lt, :] = _rms(x2, fnw_ref[...]).reshape(nb, lt, D_MODEL)

    prologue(0)
    for sub in range(n_sub):
        u_next = up_proj(sub, 0)
        for c in range(n_chunks):
            u_cur = u_next
            if c + 1 < n_chunks:
                u_next = up_proj(sub, c + 1)
            chunk(sub, c, u_cur)
            if sub + 1 < n_sub and c == n_chunks - 1 - POST_LOOKAHEAD_CHUNKS:
                prologue(sub + 1)
        epilogue(sub)


def _post_call(o_dn, o_swa, x, mod3, w_out, norm_ffn_w, w_up, conv_w, conv_b, w_down, final_w, nb, lt, state=None):
    carry = state is None
    n_sub = POST_SUB_TILES if carry else 1
    bsz, seq, _ = x.shape
    nt = seq // (lt * n_sub)
    rows = nb * lt
    tile = rows * n_sub
    n_state = _state_rows(nb, lt, FFN_CONV - 1)
    row_map = lambda i, j: (i * nt + j, 0)
    const = lambda i, j: (0, 0)
    in_specs = [pl.BlockSpec((tile, DN_V), row_map),
                pl.BlockSpec((tile, SWA_Q), row_map),
                pl.BlockSpec((nb, lt * n_sub, D_MODEL), lambda i, j: (i, j, 0)),
                pl.BlockSpec((nb, 6, D_MODEL), lambda i, j: (i, 0, 0)),
                pl.BlockSpec((D_MODEL, D_MODEL), const, pipeline_mode=RESIDENT),
                pl.BlockSpec((1, D_MODEL), const),
                pl.BlockSpec((D_MODEL, 2 * D_FF), const, pipeline_mode=RESIDENT),
                pl.BlockSpec((FFN_CONV, 2 * D_FF), const),
                pl.BlockSpec((1, 2 * D_FF), const),
                pl.BlockSpec((D_FF, D_MODEL), const, pipeline_mode=RESIDENT),
                pl.BlockSpec((1, D_MODEL), const)]
    args = [o_dn, o_swa, x, mod3, w_out, norm_ffn_w, w_up, conv_w, conv_b, w_down, final_w]
    if not carry:
        in_specs.append(pl.BlockSpec((nb, FFN_CONV - 1, 2 * D_FF), lambda i, j: (i, 0, 0)))
        args.append(state)
    return pl.pallas_call(
        functools.partial(_post_kernel, nb=nb, lt=lt, carry=carry, n_sub=n_sub),
        grid=(bsz // nb, nt),
        in_specs=in_specs,
        out_specs=[pl.BlockSpec((nb, lt * n_sub, D_MODEL), lambda i, j: (i, j, 0)),
                   pl.BlockSpec((nb, n_state, 2 * D_FF), lambda i, j: (i, 0, 0))],
        out_shape=[jax.ShapeDtypeStruct((bsz, seq, D_MODEL), F32),
                   jax.ShapeDtypeStruct((bsz, n_state, 2 * D_FF), F32)],
        scratch_shapes=[pltpu.VMEM((n_sub, rows, D_MODEL), BF16),
                        pltpu.VMEM((n_sub, rows, D_MODEL), F32),
                        pltpu.VMEM((n_sub, rows, D_MODEL), F32)],
        compiler_params=_cparams(("arbitrary", "arbitrary")),
        name="out_proj_convffn_carry" if carry else "out_proj_convffn_state",
    )(*args)


def _pad_row(vec, offset):
    return jnp.zeros((1, BA_PAD), F32).at[0, offset:offset + vec.shape[0]].set(vec)


def kernel(x_prompt, x_sample, state_dn_conv, state_dn_ssm, cache_swa_k, cache_swa_v, state_ffn_conv, c_prompt, c_sample, rel_bias, final_norm_w, w_ada, b_ada, norm_mix_w, w_in, dn_conv_w, dn_A_log, dn_dt_bias, dn_norm_w, swa_sinks, w_out, norm_ffn_w, ffn_w_up, ffn_conv_w, ffn_conv_b, ffn_w_down):
    bp, lp, _ = x_prompt.shape
    bs, ls, _ = x_sample.shape
    layer = 0

    w_in_t = jnp.transpose(w_in[layer])
    w_out_b = w_out[layer].astype(BF16)
    w_up_b = ffn_w_up[layer].astype(BF16)
    w_dn_b = ffn_w_down[layer].astype(BF16)
    alog_row = _pad_row(dn_A_log[layer], DN_HEADS)
    dt_row = _pad_row(dn_dt_bias[layer], DN_HEADS)
    row = lambda v: v.reshape(1, -1)

    n_c = bp + bs
    n_c_pad = -(-n_c // SUBLANES) * SUBLANES
    c_all = jnp.pad(jnp.concatenate([c_sample, c_prompt], axis=0), ((0, n_c_pad - n_c), (0, 0)))
    mod_s, mod_p = _mod_call(c_all, w_ada[layer], row(b_ada[layer]), bs, bp)

    bias_t, bias_s = _bias_call(rel_bias)
    sinks = swa_sinks[layer]

    def mixer_in(x, mod3, nb, lt, state=None):
        return _in_call(x, mod3, row(norm_mix_w[layer]), w_in_t, dn_conv_w[layer], nb, lt, state)

    def post(o_dn, o_swa, x, mod3, nb, lt, state=None):
        return _post_call(o_dn, o_swa, x, mod3, w_out_b, row(norm_ffn_w[layer]), w_up_b, ffn_conv_w[layer],
                          row(ffn_conv_b[layer]), w_dn_b, row(final_norm_w), nb, lt, state)

    dn_args = (alog_row, dt_row, row(dn_norm_w[layer]))

    qkv, z, ba, sq, sk, sv, p_tail = mixer_in(x_prompt, mod_p, 1, 512)
    o_dn, p_ssm = _dn_call(qkv, z, ba, *dn_args, bp, lp)
    o_swa = _swa_prompt_call(sinks, sq, sk, sv, bias_t, bp, lp)
    y_prompt, p_ffn_tail = post(o_dn, o_swa, x_prompt, mod_p, 1, 256)
    p_dn_conv = p_tail[:, SUBLANES - (DN_CONV - 1):]
    p_fbuf = p_ffn_tail[:, SUBLANES - (FFN_CONV - 1):]
    last_window = lambda t: t.reshape(bp, lp, SWA_KV)[:, lp - WINDOW:].reshape(bp, WINDOW, SWA_KV_HEADS, SWA_HD)
    p_swa_k = last_window(sk)
    p_swa_v = last_window(sv)

    qkv_s, z_s, ba_s, sq_s, sk_s, sv_s, s_dn_conv = mixer_in(x_sample, mod_s, 64, ls, state=state_dn_conv[layer])
    o_dn_s, s_ssm = _dn_call(qkv_s, z_s, ba_s, *dn_args, bs, ls, state=state_dn_ssm[layer])
    o_swa_s, s_k, s_v = _swa_sample_call(sinks, sq_s, sk_s, sv_s,
                                         cache_swa_k[layer].reshape(bs, WINDOW, SWA_KV),
                                         cache_swa_v[layer].reshape(bs, WINDOW, SWA_KV), bias_s, bs, ls)
    y_sample, s_fbuf = post(o_dn_s, o_swa_s, x_sample, mod_s, 32, ls, state=state_ffn_conv[layer])

    return (y_prompt, y_sample, p_dn_conv[None], s_dn_conv[None], p_ssm[None], s_ssm[None],
            p_swa_k[None], s_k.reshape(bs, WINDOW, SWA_KV_HEADS, SWA_HD)[None],
            p_swa_v[None], s_v.reshape(bs, WINDOW, SWA_KV_HEADS, SWA_HD)[None],
            p_fbuf[None], s_fbuf[None])
```

```python
import functools
import math

import numpy as np
import jax
import jax.numpy as jnp
from jax import lax
from jax.experimental import pallas as pl
from jax.experimental.pallas import tpu as pltpu

F32 = jnp.float32
BF16 = jnp.bfloat16

D_MODEL = 1024
PAST_LEN = 16384
DN_HEADS = 4
DN_DK = 128
DN_DV = 128
DN_CONV = 4
SWA_HEADS = 8
SWA_KV_HEADS = 2
SWA_GROUP = SWA_HEADS // SWA_KV_HEADS
SWA_HD = 64
WINDOW = 128
N_BUCKETS = 32
MAX_DISTANCE = 128
D_FF = 2816
FFN_CONV = 3
EPS = 1e-6
NEG_INF = -1e30

DN_QK = DN_HEADS * DN_DK
DN_V = DN_HEADS * DN_DV
DN_CONV_CH = 2 * DN_QK + DN_V
SWA_Q = SWA_HEADS * SWA_HD
SWA_KV = SWA_KV_HEADS * SWA_HD
BA_PAD = 128
SUBLANES = 8
UNIT = 128
POST_SUB_TILES = 2
POST_LOOKAHEAD_CHUNKS = 3
FFN_CHUNK = 256
VMEM_LIMIT = 56 * 1024 * 1024
RESIDENT = pl.Buffered(1)


def _cparams(sem):
    return pltpu.CompilerParams(dimension_semantics=sem, vmem_limit_bytes=VMEM_LIMIT)


def _bf(x):
    return x.astype(BF16)


def _dot(a, b):
    return jnp.dot(_bf(a), _bf(b), preferred_element_type=F32)


def _dot_nt(a, b):
    return lax.dot_general(_bf(a), _bf(b), (((1,), (1,)), ((), ())), preferred_element_type=F32)


def _split2(x):
    hi = _bf(x)
    lo = _bf(x - hi.astype(F32))
    return hi, lo


def _dot_x3(a, b):
    ah, al = _split2(a)
    bh, bl = _split2(b)
    d = functools.partial(jnp.dot, preferred_element_type=F32)
    return d(ah, bh) + (d(al, bh) + d(ah, bl))


def _dot_mask(m, x):
    hi = _bf(x)
    r = x - hi.astype(F32)
    mid = _bf(r)
    lo = _bf(r - mid.astype(F32))
    d = functools.partial(jnp.dot, preferred_element_type=F32)
    return d(m, hi) + (d(m, mid) + d(m, lo))


def _sigmoid(x):
    return 1.0 / (1.0 + jnp.exp(-x))


def _silu(x):
    return x * _sigmoid(x)


def _softplus(x):
    return jnp.maximum(x, 0.0) + jnp.log1p(jnp.exp(-jnp.abs(x)))


def _rms(x, w):
    ms = jnp.mean(x * x, axis=-1, keepdims=True)
    return x * lax.rsqrt(ms + EPS) * w


def _l2norm(t):
    return t * lax.rsqrt(jnp.sum(t * t, axis=-1, keepdims=True) + EPS)


def _rows(m3, nb, lt):
    return jnp.broadcast_to(m3, (nb, lt, m3.shape[-1])).reshape(nb * lt, m3.shape[-1])


def _causal_conv(x, prev, w, nb, lt):
    width = w.shape[0]
    rows, ch = x.shape
    if nb == 1 and lt > SUBLANES:
        tiles = jnp.concatenate([prev, x.reshape(lt // SUBLANES, SUBLANES, ch)], axis=0)
        sub = lax.broadcasted_iota(jnp.int32, (1, SUBLANES, 1), 1)
        out = tiles[1:] * w[width - 1:width, :]
        for j in range(1, width):
            rot = pltpu.roll(tiles, j, axis=1)
            out = out + jnp.where(sub >= j, rot[1:], rot[:-1]) * w[width - 1 - j:width - j, :]
        return out.reshape(rows, ch)
    tmod = lax.broadcasted_iota(jnp.int32, (rows, 1), 0) & (lt - 1)
    out = x * w[width - 1:width, :]
    for j in range(1, width):
        sh = pltpu.roll(x, j, axis=0)
        for t in range(j):
            p = width - 1 - j + t
            sh = jnp.where(tmod == t, _rows(prev[:, p:p + 1, :], nb, lt), sh)
        out = out + sh * w[width - 1 - j:width - j, :]
    return out


def _last_rows(x, nb, lt, n_state):
    ch = x.shape[-1]
    if nb == 1 and lt > SUBLANES:
        return x[lt - SUBLANES:].reshape(1, SUBLANES, ch)
    return x.reshape(nb, lt, ch)[:, lt - n_state:, :]


def _state_rows(nb, lt, n_state):
    return SUBLANES if (nb == 1 and lt > SUBLANES) else n_state


MOD_PARTS = 6


def _mod_kernel(c_ref, w_ref, b_ref, os_ref, op_ref, *, n_s, n_p):
    part = pl.program_id(0)
    res = _dot_x3(_silu(c_ref[...]), w_ref[...]) + b_ref[...]
    for k in range(MOD_PARTS):
        @pl.when(part == k)
        def _():
            os_ref[:, k, :] = res[:n_s]
            op_ref[:, k, :] = res[n_s:n_s + n_p]


def _mod_call(c_all, w_ada, b_ada, n_s, n_p):
    rows = c_all.shape[0]
    return pl.pallas_call(
        functools.partial(_mod_kernel, n_s=n_s, n_p=n_p),
        grid=(MOD_PARTS,),
        in_specs=[pl.BlockSpec((rows, D_MODEL), lambda k: (0, 0)),
                  pl.BlockSpec((D_MODEL, D_MODEL), lambda k: (0, k)),
                  pl.BlockSpec((1, D_MODEL), lambda k: (0, k))],
        out_specs=[pl.BlockSpec((n_s, MOD_PARTS, D_MODEL), lambda k: (0, 0, 0)),
                   pl.BlockSpec((n_p, MOD_PARTS, D_MODEL), lambda k: (0, 0, 0))],
        out_shape=[jax.ShapeDtypeStruct((n_s, MOD_PARTS, D_MODEL), F32),
                   jax.ShapeDtypeStruct((n_p, MOD_PARTS, D_MODEL), F32)],
        compiler_params=_cparams(("arbitrary",)),
        name="adaln_mod",
    )(c_all, w_ada, b_ada)


IN_SPLIT = (DN_CONV_CH, DN_V, BA_PAD, SWA_Q, SWA_KV, SWA_KV)


IN_SUB_TILES = 4
IN_LOOKAHEAD_SLABS = 1
W_PREP_ROWS = 256
IN_SLAB = 2 * DN_DK


def _in_kernel(*refs, nb, lt, carry, n_sub):
    if carry:
        (x_ref, mod_ref, nw_ref, w_ref, cw_ref,
         qkv_ref, z_ref, ba_ref, sq_ref, sk_ref, sv_ref, tail_ref, h_scr, wdn_ref, wba_ref, wsw_ref) = refs
        prev_ref = tail_ref

        @pl.when(pl.program_id(1) == 0)
        def _():
            tail_ref[...] = jnp.zeros_like(tail_ref)
    else:
        (x_ref, mod_ref, nw_ref, w_ref, cw_ref, prev_ref,
         qkv_ref, z_ref, ba_ref, sq_ref, sk_ref, sv_ref, tail_ref, h_scr, wdn_ref, wba_ref, wsw_ref) = refs

    @pl.when((pl.program_id(0) == 0) & (pl.program_id(1) == 0))
    def _():
        ba_lo = DN_CONV_CH + DN_V
        n_ba = 2 * DN_HEADS
        for r in range(0, ba_lo, W_PREP_ROWS):
            wdn_ref[:, r:r + W_PREP_ROWS] = _bf(w_ref[r:r + W_PREP_ROWS, :].T)
        row = lax.broadcasted_iota(jnp.int32, (BA_PAD, 1), 0)
        wba_ref[...] = _bf(jnp.where(row < n_ba, w_ref[ba_lo:ba_lo + BA_PAD, :], 0.0).T)
        for r in range(0, SWA_Q + 2 * SWA_KV, W_PREP_ROWS):
            wsw_ref[:, r:r + W_PREP_ROWS] = _bf(w_ref[ba_lo + n_ba + r:ba_lo + n_ba + r + W_PREP_ROWS, :].T)

    rows = nb * lt

    def norm(sub):
        x = x_ref[:, sub * lt:(sub + 1) * lt, :]
        ms = jnp.mean(x * x, axis=-1, keepdims=True)
        y = x * lax.rsqrt(ms + EPS) * nw_ref[...]
        h = y * (1.0 + mod_ref[:, 1:2, :]) + mod_ref[:, 0:1, :]
        h_scr[sub] = _bf(h.reshape(rows, D_MODEL))

    def proj(sub, w_bf_ref, lo, n):
        return jnp.dot(h_scr[sub], w_bf_ref[:, lo:lo + n], preferred_element_type=F32)

    def plain_proj(sub, i):
        rr = slice(sub * rows, (sub + 1) * rows)
        if i < 2:
            z_ref[rr, i * IN_SLAB:(i + 1) * IN_SLAB] = proj(sub, wdn_ref, DN_CONV_CH + i * IN_SLAB, IN_SLAB)
        elif i < 4:
            sq_ref[rr, (i - 2) * IN_SLAB:(i - 1) * IN_SLAB] = proj(sub, wsw_ref, (i - 2) * IN_SLAB, IN_SLAB)
        elif i == 4:
            kv = proj(sub, wsw_ref, SWA_Q, 2 * SWA_KV)
            sk_ref[rr, :] = kv[:, :SWA_KV]
            sv_ref[rr, :] = kv[:, SWA_KV:]
        else:
            ba_ref[rr, :] = proj(sub, wba_ref, 0, BA_PAD)

    def conv_slab(sub, slab, raw):
        rr = slice(sub * rows, (sub + 1) * rows)
        cols = slice(slab * IN_SLAB, (slab + 1) * IN_SLAB)
        c = _silu(_causal_conv(raw, prev_ref[:, :, cols], cw_ref[:, cols], nb, lt))
        tail_ref[:, :, cols] = _last_rows(raw, nb, lt, DN_CONV - 1)
        if slab * IN_SLAB < 2 * DN_QK:
            scale = DN_DK ** -0.5 if slab * IN_SLAB < DN_QK else 1.0
            c = jnp.concatenate([_l2norm(c[:, i * DN_DK:(i + 1) * DN_DK]) * scale
                                 for i in range(IN_SLAB // DN_DK)], axis=1)
        qkv_ref[rr, cols] = c

    n_slabs = DN_CONV_CH // IN_SLAB
    norm(0)
    for sub in range(n_sub):
        raw_next = proj(sub, wdn_ref, 0, IN_SLAB)
        for slab in range(n_slabs):
            raw = raw_next
            plain_proj(sub, slab)
            if slab + 1 < n_slabs:
                raw_next = proj(sub, wdn_ref, (slab + 1) * IN_SLAB, IN_SLAB)
            if sub + 1 < n_sub and slab == n_slabs - 1 - IN_LOOKAHEAD_SLABS:
                norm(sub + 1)
            conv_slab(sub, slab, raw)


def _in_call(x, mod3, norm_w, w_in_t, conv_w, nb, lt, state=None):
    carry = state is None
    n_sub = IN_SUB_TILES if carry else 1
    bsz, seq, _ = x.shape
    nt = seq // (lt * n_sub)
    rows = nb * lt
    tile = rows * n_sub
    n_tok = bsz * seq
    n_state = _state_rows(nb, lt, DN_CONV - 1)
    row_map = lambda i, j: (i * nt + j, 0)
    const = lambda i, j: (0, 0)
    per_seq = lambda i, j: (i, 0, 0)
    in_specs = [pl.BlockSpec((nb, lt * n_sub, D_MODEL), lambda i, j: (i, j, 0)),
                pl.BlockSpec((nb, 6, D_MODEL), per_seq),
                pl.BlockSpec((1, D_MODEL), const),
                pl.BlockSpec(w_in_t.shape, const, pipeline_mode=RESIDENT),
                pl.BlockSpec((DN_CONV, DN_CONV_CH), const)]
    args = [x, mod3, norm_w, w_in_t, conv_w]
    if not carry:
        in_specs.append(pl.BlockSpec((nb, DN_CONV - 1, DN_CONV_CH), per_seq))
        args.append(state)
    return pl.pallas_call(
        functools.partial(_in_kernel, nb=nb, lt=lt, carry=carry, n_sub=n_sub),
        grid=(bsz // nb, nt),
        in_specs=in_specs,
        out_specs=[pl.BlockSpec((tile, n), row_map) for n in IN_SPLIT]
        + [pl.BlockSpec((nb, n_state, DN_CONV_CH), per_seq)],
        out_shape=[jax.ShapeDtypeStruct((n_tok, n), F32) for n in IN_SPLIT]
        + [jax.ShapeDtypeStruct((bsz, n_state, DN_CONV_CH), F32)],
        scratch_shapes=[pltpu.VMEM((n_sub, rows, D_MODEL), BF16),
                        pltpu.VMEM((D_MODEL, DN_CONV_CH + DN_V), BF16),
                        pltpu.VMEM((D_MODEL, BA_PAD), BF16),
                        pltpu.VMEM((D_MODEL, SWA_Q + 2 * SWA_KV), BF16)],
        compiler_params=_cparams(("arbitrary", "arbitrary")),
        name="norm_in_proj_carry" if carry else "norm_in_proj_state",
    )(*args)


INV_BASE_SHIFT = 2
DN_UNITS_CARRY = 4
SWA_BLOCKS_PER_STEP = 4
DN_UNITS_STATE = 1


def _unit_lower_inverses(a_mats, ri, ci, chunk_shift):
    def blocks(s):
        return (ri >> s) == (ci >> s)

    base = min(INV_BASE_SHIFT, chunk_shift)
    xs = [jnp.where(blocks(base), -a, 0.0) for a in a_mats]
    ts = [jnp.where(ri == ci, 1.0, n) for n in xs]
    for lvl in range(1, base):
        xs = [_dot(x, x) for x in xs]
        ts = [t + _dot(t, x) for t, x in zip(ts, xs)]
    for s in range(base, chunk_shift):
        sel = blocks(s + 1) & jnp.logical_not(blocks(s))
        ets = [_dot(jnp.where(sel, a, 0.0), t) for a, t in zip(a_mats, ts)]
        ts = [t - _dot(t, et) for t, et in zip(ts, ets)]
    return ts


def _dn_kernel(*refs, nb, chunk, carry, units):
    if carry:
        (qkv_ref, z_ref, ba_ref, alog_ref, dt_ref, nw_ref, o_ref, s_ref) = refs
        s0_ref = s_ref

        @pl.when(pl.program_id(1) == 0)
        def _():
            s_ref[...] = jnp.zeros_like(s_ref)
    else:
        (qkv_ref, z_ref, ba_ref, alog_ref, dt_ref, nw_ref, s0_ref, o_ref, s_ref) = refs

    ri = lax.broadcasted_iota(jnp.int32, (UNIT, UNIT), 0)
    ci = lax.broadcasted_iota(jnp.int32, (UNIT, UNIT), 1)
    shift = int(math.log2(chunk))
    same = (ri >> shift) == (ci >> shift)
    incl = same & (ri >= ci)
    strict = same & (ri > ci)

    ba = ba_ref[...]
    beta_full = _sigmoid(ba)
    g_full = -jnp.exp(alog_ref[...]) * _softplus(ba + dt_ref[...])
    masks = jnp.concatenate([jnp.where(incl, 1.0, 0.0), jnp.where(same, 1.0, 0.0)], axis=0).astype(BF16)
    gsums = [_dot_mask(masks, g_full[u * UNIT:(u + 1) * UNIT]) for u in range(units)]
    g_cum = [g[:UNIT] for g in gsums]
    g_tot = [g[UNIT:] for g in gsums]
    g_cum_t = [g.T for g in g_cum]

    probs = [(u, h) for u in range(units) for h in range(DN_HEADS)]
    n_p = len(probs)
    rs = lambda u: slice(u * UNIT, (u + 1) * UNIT)

    def head_cols(base):
        return [qkv_ref[rs(u), base + h * DN_DK:base + (h + 1) * DN_DK] for u, h in probs]

    q = head_cols(0)
    k = head_cols(DN_QK)
    v = head_cols(2 * DN_QK)
    gc = [g_cum[u][:, DN_HEADS + h:DN_HEADS + h + 1] for u, h in probs]
    gr = [g_cum_t[u][DN_HEADS + h:DN_HEADS + h + 1, :] for u, h in probs]
    gt = [g_tot[u][:, DN_HEADS + h:DN_HEADS + h + 1] for u, h in probs]
    bc = [beta_full[rs(u), h:h + 1] for u, h in probs]
    decay = [jnp.where(incl, jnp.exp(jnp.where(incl, gc[p] - gr[p], 0.0)), 0.0) for p in range(n_p)]
    e_g = [jnp.exp(gc[p]) for p in range(n_p)]
    kq = [_dot_nt(jnp.concatenate([k[p], q[p]], axis=0), k[p]) for p in range(n_p)]
    qk = [kq[p][UNIT:] * decay[p] for p in range(n_p)]
    a_mats = [jnp.where(strict, bc[p] * kq[p][:UNIT] * decay[p], 0.0) for p in range(n_p)]
    t_inv = _unit_lower_inverses(a_mats, ri, ci, shift)
    wvk = [_dot(t_inv[p], jnp.concatenate([v[p] * bc[p], k[p] * (bc[p] * e_g[p])], axis=1)) for p in range(n_p)]
    w_v = [w[:, :DN_DV] for w in wvk]
    w_k = [w[:, DN_DV:] for w in wvk]
    q_dec = [q[p] * e_g[p] for p in range(n_p)]
    k_tail_t = [(k[p] * jnp.exp(gt[p] - gc[p])).T for p in range(n_p)]
    c_dec = [jnp.exp(gt[p]) for p in range(n_p)]

    outs = {}
    if carry:
        state = [s_ref[0, h] for h in range(DN_HEADS)]
        for u in range(units):
            ps = [u * DN_HEADS + h for h in range(DN_HEADS)]
            r = [_dot(jnp.concatenate([w_k[p], q_dec[p]], axis=0), state[h]) for h, p in enumerate(ps)]
            uu = [w_v[p] - r[h][:UNIT] for h, p in enumerate(ps)]
            for h, p in enumerate(ps):
                outs[p] = r[h][UNIT:] + _dot(qk[p], uu[h])
            state = [state[h] * c_dec[p][0:1, :] + _dot(k_tail_t[p], uu[h]) for h, p in enumerate(ps)]
        for h in range(DN_HEADS):
            s_ref[0, h] = state[h]
    else:
        per_unit = nb // units
        for p, (u, h) in enumerate(probs):
            us, qs = [], []
            for s in range(per_unit):
                lo = s * chunk
                lhs = jnp.concatenate([w_k[p][lo:lo + chunk], q_dec[p][lo:lo + chunk]], axis=0)
                r = jnp.dot(lhs, s0_ref[u * per_unit + s, h], preferred_element_type=F32)
                us.append(w_v[p][lo:lo + chunk] - r[:chunk])
                qs.append(r[chunk:])
            uu = jnp.concatenate(us, axis=0)
            outs[p] = jnp.concatenate(qs, axis=0) + _dot(qk[p], uu)
            for s in range(per_unit):
                lo = s * chunk
                upd = jnp.dot(k_tail_t[p][:, lo:lo + chunk], uu[lo:lo + chunk], preferred_element_type=F32)
                s_ref[u * per_unit + s, h] = s0_ref[u * per_unit + s, h] * c_dec[p][lo:lo + 1, :] + upd

    for p, (u, h) in enumerate(probs):
        zz = z_ref[rs(u), h * DN_DV:(h + 1) * DN_DV]
        o_ref[rs(u), h * DN_DV:(h + 1) * DN_DV] = _rms(outs[p], nw_ref[...]) * _silu(zz)


def _dn_call(qkv, z, ba, alog_row, dt_row, norm_w, bsz, seq, state=None):
    carry = state is None
    units = DN_UNITS_CARRY if carry else DN_UNITS_STATE
    rows = units * UNIT
    nb = 1 if carry else rows // seq
    chunk = UNIT if carry else seq
    nt = seq * nb // rows
    n_tok = bsz * seq
    row_map = lambda i, j: (i * nt + j, 0)
    const = lambda i, j: (0, 0)
    in_specs = [pl.BlockSpec((rows, DN_CONV_CH), row_map),
                pl.BlockSpec((rows, DN_V), row_map),
                pl.BlockSpec((rows, BA_PAD), row_map),
                pl.BlockSpec((1, BA_PAD), const),
                pl.BlockSpec((1, BA_PAD), const),
                pl.BlockSpec((1, DN_DV), const)]
    args = [qkv, z, ba, alog_row, dt_row, norm_w]
    if not carry:
        in_specs.append(pl.BlockSpec((nb, DN_HEADS, DN_DK, DN_DV), lambda i, j: (i, 0, 0, 0)))
        args.append(state)
    return pl.pallas_call(
        functools.partial(_dn_kernel, nb=nb, chunk=chunk, carry=carry, units=units),
        grid=(bsz // nb, nt),
        in_specs=in_specs,
        out_specs=[pl.BlockSpec((rows, DN_V), row_map),
                   pl.BlockSpec((nb, DN_HEADS, DN_DK, DN_DV), lambda i, j: (i, 0, 0, 0))],
        out_shape=[jax.ShapeDtypeStruct((n_tok, DN_V), F32),
                   jax.ShapeDtypeStruct((bsz, DN_HEADS, DN_DK, DN_DV), F32)],
        compiler_params=_cparams(("arbitrary", "arbitrary")),
        name="gated_deltanet_carry" if carry else "gated_deltanet_state",
    )(*args)


def _bucket_table():
    i = np.arange(WINDOW, dtype=np.int64)[:, None]
    j = np.arange(2 * WINDOW, dtype=np.int64)[None, :]
    d = np.maximum(i + WINDOW - j, 0)
    exact = N_BUCKETS // 2
    logv = (np.log(np.maximum(d, 1).astype(np.float32) / np.float32(exact)).astype(np.float32)
            / np.float32(math.log(MAX_DISTANCE / exact)))
    large = np.minimum(exact + (logv * np.float32(N_BUCKETS - exact)).astype(np.int32), N_BUCKETS - 1)
    return np.where(d < exact, d, large).astype(np.int32)


def _bias_lookup(rb_ref, bucket, head):
    acc = jnp.zeros(bucket.shape, F32)
    for b in range(N_BUCKETS):
        acc = jnp.where(bucket == b, rb_ref[b, head], acc)
    return acc


def _bias_kernel(rb_ref, bucket_t_ref, bucket_ref, ot_ref, os_ref):
    bucket_t = bucket_t_ref[...]
    kj = lax.broadcasted_iota(jnp.int32, bucket_t.shape, 0)
    qi = lax.broadcasted_iota(jnp.int32, bucket_t.shape, 1)
    dist = qi + WINDOW - kj
    valid = (dist >= 0) & (dist < WINDOW)
    for hk in range(SWA_KV_HEADS):
        for par in range(2):
            for st in range(SWA_GROUP // 2):
                head = hk * SWA_GROUP + 2 * st + par
                gen = jnp.where(valid, _bias_lookup(rb_ref, bucket_t, head), NEG_INF)
                ot_ref[1, hk, par, :, st * WINDOW:(st + 1) * WINDOW] = gen
                ot_ref[0, hk, par, :, st * WINDOW:(st + 1) * WINDOW] = jnp.where(kj >= WINDOW, gen, NEG_INF)
    bucket = bucket_ref[...]
    qi = lax.broadcasted_iota(jnp.int32, bucket.shape, 0)
    kj = lax.broadcasted_iota(jnp.int32, bucket.shape, 1)
    dist = qi + WINDOW - kj
    valid = (dist >= 0) & (dist < WINDOW)
    for head in range(SWA_HEADS):
        os_ref[head] = jnp.where(valid, _bias_lookup(rb_ref, bucket, head), NEG_INF)


BIAS_T_SHAPE = (2, SWA_KV_HEADS, 2, 2 * WINDOW, (SWA_GROUP // 2) * WINDOW)
BIAS_S_SHAPE = (SWA_HEADS, SUBLANES, 2 * WINDOW)


def _bias_call(rel_bias):
    bucket = _bucket_table()
    return pl.pallas_call(
        _bias_kernel,
        in_specs=[pl.BlockSpec(memory_space=pltpu.SMEM),
                  pl.BlockSpec((2 * WINDOW, WINDOW), lambda: (0, 0)),
                  pl.BlockSpec((SUBLANES, 2 * WINDOW), lambda: (0, 0))],
        out_specs=[pl.BlockSpec(BIAS_T_SHAPE, lambda: (0,) * len(BIAS_T_SHAPE)),
                   pl.BlockSpec(BIAS_S_SHAPE, lambda: (0,) * len(BIAS_S_SHAPE))],
        out_shape=[jax.ShapeDtypeStruct(BIAS_T_SHAPE, F32), jax.ShapeDtypeStruct(BIAS_S_SHAPE, F32)],
        name="swa_rel_bias_table",
    )(rel_bias, jnp.asarray(np.ascontiguousarray(bucket.T)), jnp.asarray(bucket[:SUBLANES]))


def _softmax_sink_parts(s, sink):
    m = jnp.maximum(jnp.max(s, axis=0, keepdims=True), sink)
    p = jnp.exp(s - m)
    return p, 1.0 / (jnp.sum(p, axis=0, keepdims=True) + jnp.exp(sink - m))


def _half_lane_variants(full, hk, lo_half):
    rolled = pltpu.roll(full, SWA_HD, axis=1)
    low_src, high_src = (full, rolled) if hk == 0 else (rolled, full)
    return jnp.where(lo_half, low_src, 0.0), jnp.where(lo_half, 0.0, high_src)


def _swa_prompt_kernel(sink_ref, q_ref, kp_ref, kc_ref, vp_ref, vc_ref, bias_ref, o_ref, *, n_blk):
    step = pl.program_id(1)
    lo_half = lax.broadcasted_iota(jnp.int32, (1, 2 * SWA_HD), 1) < SWA_HD
    lo_rows = lax.broadcasted_iota(jnp.int32, (2 * SWA_HD, 1), 0) < SWA_HD
    q = _bf(q_ref[...] * (SWA_HD ** -0.5))
    keys = jnp.concatenate([kp_ref[...], kc_ref[...]], axis=0)
    vals = jnp.concatenate([vp_ref[...], vc_ref[...]], axis=0)
    k_var = [[_bf(t) for t in _half_lane_variants(keys, hk, lo_half)] for hk in range(SWA_KV_HEADS)]
    v_var_t = [[_bf(t.T) for t in _half_lane_variants(vals, hk, lo_half)] for hk in range(SWA_KV_HEADS)]
    n_stack = SWA_GROUP // 2
    sinks = [[jnp.concatenate([jnp.full((1, WINDOW), sink_ref[hk * SWA_GROUP + 2 * st + par], F32)
                               for st in range(n_stack)], axis=1) for par in range(2)]
             for hk in range(SWA_KV_HEADS)]
    nt_dims = (((1,), (1,)), ((), ()))

    def scores(b):
        rows = slice(b * WINDOW, (b + 1) * WINDOW)
        win = slice(b * WINDOW, (b + 2) * WINDOW)
        variant = jnp.where(step == 0, 0, 1) if b == 0 else 1
        out = []
        for hk in range(SWA_KV_HEADS):
            q2 = jnp.concatenate([q[rows, (hk * n_stack + st) * 2 * SWA_HD:(hk * n_stack + st + 1) * 2 * SWA_HD]
                                  for st in range(n_stack)], axis=0)
            out.append([lax.dot_general(k_var[hk][par][win], q2, nt_dims, preferred_element_type=F32)
                        + bias_ref[variant, hk, par] for par in range(2)])
        return out

    s_next = scores(0)
    for b in range(n_blk):
        s_cur = s_next
        if b + 1 < n_blk:
            s_next = scores(b + 1)
        rows = slice(b * WINDOW, (b + 1) * WINDOW)
        win = slice(b * WINDOW, (b + 2) * WINDOW)
        for hk in range(SWA_KV_HEADS):
            parts = [_softmax_sink_parts(s_cur[hk][par], sinks[hk][par]) for par in range(2)]
            o_t = (jnp.dot(v_var_t[hk][0][:, win], _bf(parts[0][0]), preferred_element_type=F32)
                   + jnp.dot(v_var_t[hk][1][:, win], _bf(parts[1][0]), preferred_element_type=F32))
            o_t = o_t * jnp.where(lo_rows, parts[0][1], parts[1][1])
            for st in range(n_stack):
                lo = (hk * n_stack + st) * 2 * SWA_HD
                o_ref[rows, lo:lo + 2 * SWA_HD] = o_t[:, st * WINDOW:(st + 1) * WINDOW].T


def _swa_prompt_call(sinks, sq, sk, sv, bias, bsz, seq):
    n_blk = SWA_BLOCKS_PER_STEP
    tile = n_blk * WINDOW
    nt = seq // tile
    cur = lambda b, i: (b * nt + i, 0)
    prv = lambda b, i: (b * nt * n_blk + jnp.maximum(i * n_blk - 1, 0), 0)
    return pl.pallas_call(
        functools.partial(_swa_prompt_kernel, n_blk=n_blk),
        grid=(bsz, nt),
        in_specs=[pl.BlockSpec(memory_space=pltpu.SMEM),
                  pl.BlockSpec((tile, SWA_Q), cur),
                  pl.BlockSpec((WINDOW, SWA_KV), prv),
                  pl.BlockSpec((tile, SWA_KV), cur),
                  pl.BlockSpec((WINDOW, SWA_KV), prv),
                  pl.BlockSpec((tile, SWA_KV), cur),
                  pl.BlockSpec(BIAS_T_SHAPE, lambda b, i: (0,) * len(BIAS_T_SHAPE))],
        out_specs=pl.BlockSpec((tile, SWA_Q), cur),
        out_shape=jax.ShapeDtypeStruct((bsz * seq, SWA_Q), F32),
        compiler_params=_cparams(("arbitrary", "arbitrary")),
        name="swa_banded",
    )(sinks, sq, sk, sk, sv, sv, bias)


def _swa_sample_kernel(sink_ref, q_ref, kn_ref, vn_ref, kc_ref, vc_ref, bias_ref, o_ref, ko_ref, vo_ref, *, nb, lt):
    lo_half = lax.broadcasted_iota(jnp.int32, (1, 2 * SWA_HD), 1) < SWA_HD
    q_all = q_ref[...] * (SWA_HD ** -0.5)
    kn_all = kn_ref[...]
    vn_all = vn_ref[...]
    pad = jnp.zeros((WINDOW - lt, SWA_KV), F32)
    bias = jnp.concatenate([bias_ref[h, 0:lt, :] for h in range(SWA_HEADS)], axis=0)
    sink = jnp.concatenate([jnp.full((lt, 1), sink_ref[h], F32) for h in range(SWA_HEADS)], axis=0)
    nt_dims = (((1,), (1,)), ((), ()))

    def to_half(tile, src_half, dst_half):
        return tile if src_half == dst_half else pltpu.roll(tile, SWA_HD, axis=1)

    lhs, keys, vals = [], [], []
    for s in range(nb):
        rows = slice(s * lt, (s + 1) * lt)
        kc, vc, kn, vn = kc_ref[s], vc_ref[s], kn_all[rows], vn_all[rows]
        ko_ref[s] = jnp.concatenate([kc[lt:], kn], axis=0)
        vo_ref[s] = jnp.concatenate([vc[lt:], vn], axis=0)
        keys.append(_bf(jnp.concatenate([kc, kn, pad], axis=0)))
        vals.append(_bf(jnp.concatenate([vc, vn, pad], axis=0)))
        q = q_all[rows]
        tiles = []
        for h in range(SWA_HEADS):
            hk = h // SWA_GROUP
            t = to_half(q[:, (h // 2) * 2 * SWA_HD:(h // 2 + 1) * 2 * SWA_HD], h % 2, hk)
            tiles.append(jnp.where(lo_half if hk == 0 else jnp.logical_not(lo_half), t, 0.0))
        lhs.append(_bf(jnp.concatenate(tiles, axis=0)))
    scores = [lax.dot_general(lhs[s], keys[s], nt_dims, preferred_element_type=F32) + bias for s in range(nb)]
    m = [jnp.maximum(jnp.max(sc, axis=-1, keepdims=True), sink) for sc in scores]
    p = [jnp.exp(sc - mm) for sc, mm in zip(scores, m)]
    rinv = [1.0 / (jnp.sum(pp, axis=-1, keepdims=True) + jnp.exp(sink - mm)) for pp, mm in zip(p, m)]
    res = [jnp.dot(_bf(p[s]), vals[s], preferred_element_type=F32) * rinv[s] for s in range(nb)]
    rows_out = []
    for s in range(nb):
        tiles = []
        for pair in range(SWA_HEADS // 2):
            hk = (2 * pair) // SWA_GROUP
            low = to_half(res[s][(2 * pair) * lt:(2 * pair + 1) * lt], hk, 0)
            high = to_half(res[s][(2 * pair + 1) * lt:(2 * pair + 2) * lt], hk, 1)
            tiles.append(jnp.where(lo_half, low, high))
        rows_out.append(jnp.concatenate(tiles, axis=1))
    o_ref[...] = jnp.concatenate(rows_out, axis=0)


def _swa_sample_call(sinks, sq, sk, sv, cache_k, cache_v, bias, bsz, seq):
    nb = UNIT // seq
    rows = lambda i: (i, 0)
    seqs = lambda i: (i, 0, 0)
    return pl.pallas_call(
        functools.partial(_swa_sample_kernel, nb=nb, lt=seq),
        grid=(bsz // nb,),
        in_specs=[pl.BlockSpec(memory_space=pltpu.SMEM),
                  pl.BlockSpec((UNIT, SWA_Q), rows),
                  pl.BlockSpec((UNIT, SWA_KV), rows),
                  pl.BlockSpec((UNIT, SWA_KV), rows),
                  pl.BlockSpec((nb, WINDOW, SWA_KV), seqs),
                  pl.BlockSpec((nb, WINDOW, SWA_KV), seqs),
                  pl.BlockSpec(BIAS_S_SHAPE, lambda i: (0,) * len(BIAS_S_SHAPE))],
        out_specs=[pl.BlockSpec((UNIT, SWA_Q), rows),
                   pl.BlockSpec((nb, WINDOW, SWA_KV), seqs),
                   pl.BlockSpec((nb, WINDOW, SWA_KV), seqs)],
        out_shape=[jax.ShapeDtypeStruct((bsz * seq, SWA_Q), F32),
                   jax.ShapeDtypeStruct((bsz, WINDOW, SWA_KV), F32),
                   jax.ShapeDtypeStruct((bsz, WINDOW, SWA_KV), F32)],
        compiler_params=_cparams(("arbitrary",)),
        name="swa_cached",
    )(sinks, sq, sk, sv, cache_k, cache_v, bias)


def _post_kernel(*refs, nb, lt, carry, n_sub):
    if carry:
        (odn_ref, oswa_ref, x_ref, mod_ref, wout_ref, nfw_ref, wup_ref, cw_ref, cb_ref, wdn_ref, fnw_ref,
         y_ref, fbuf_ref, h_scr, x1_scr, acc_scr) = refs
        prev_ref = fbuf_ref

        @pl.when(pl.program_id(1) == 0)
        def _():
            fbuf_ref[...] = jnp.zeros_like(fbuf_ref)
    else:
        (odn_ref, oswa_ref, x_ref, mod_ref, wout_ref, nfw_ref, wup_ref, cw_ref, cb_ref, wdn_ref, fnw_ref,
         prev_ref, y_ref, fbuf_ref, h_scr, x1_scr, acc_scr) = refs
    rows = nb * lt
    n_chunks = D_FF // FFN_CHUNK

    def col_slices(c):
        return [slice(base + c * FFN_CHUNK, base + (c + 1) * FFN_CHUNK) for base in (0, D_FF)]

    def prologue(sub):
        rr = slice(sub * rows, (sub + 1) * rows)
        attn = (jnp.dot(_bf(odn_ref[rr, :]), wout_ref[0:DN_V, :], preferred_element_type=F32)
                + jnp.dot(_bf(oswa_ref[rr, :]), wout_ref[DN_V:, :], preferred_element_type=F32))
        x = x_ref[:, sub * lt:(sub + 1) * lt, :].reshape(rows, D_MODEL)
        x1 = x + _rows(mod_ref[:, 2:3, :], nb, lt) * attn
        x1_scr[sub] = x1
        h = (_rms(x1, nfw_ref[...]) * (1.0 + _rows(mod_ref[:, 4:5, :], nb, lt))
             + _rows(mod_ref[:, 3:4, :], nb, lt))
        h_scr[sub] = _bf(h)
        acc_scr[sub] = jnp.zeros((rows, D_MODEL), F32)

    def up_proj(sub, c):
        return [jnp.dot(h_scr[sub], wup_ref[:, cols], preferred_element_type=F32) for cols in col_slices(c)]

    def chunk(sub, c, u_cur):
        halves = []
        for u, cols in zip(u_cur, col_slices(c)):
            prev = prev_ref[:, :, cols]
            halves.append(_causal_conv(u, prev, cw_ref[:, cols], nb, lt) + cb_ref[:, cols])
            fbuf_ref[:, :, cols] = _last_rows(u, nb, lt, FFN_CONV - 1)
        act = _silu(halves[0]) * halves[1]
        acc_scr[sub] += jnp.dot(_bf(act), wdn_ref[c * FFN_CHUNK:(c + 1) * FFN_CHUNK, :],
                                preferred_element_type=F32)

    def epilogue(sub):
        x2 = x1_scr[sub] + _rows(mod_ref[:, 5:6, :], nb, lt) * acc_scr[sub]
        y_ref[:, sub * lt:(sub + 1) * lt, :] = _rms(x2, fnw_ref[...]).reshape(nb, lt, D_MODEL)

    prologue(0)
    for sub in range(n_sub):
        u_next = up_proj(sub, 0)
        for c in range(n_chunks):
            u_cur = u_next
            if c + 1 < n_chunks:
                u_next = up_proj(sub, c + 1)
            chunk(sub, c, u_cur)
            if sub + 1 < n_sub and c == n_chunks - 1 - POST_LOOKAHEAD_CHUNKS:
                prologue(sub + 1)
        epilogue(sub)


def _post_call(o_dn, o_swa, x, mod3, w_out, norm_ffn_w, w_up, conv_w, conv_b, w_down, final_w, nb, lt, state=None):
    carry = state is None
    n_sub = POST_SUB_TILES if carry else 1
    bsz, seq, _ = x.shape
    nt = seq // (lt * n_sub)
    rows = nb * lt
    tile = rows * n_sub
    n_state = _state_rows(nb, lt, FFN_CONV - 1)
    row_map = lambda i, j: (i * nt + j, 0)
    const = lambda i, j: (0, 0)
    in_specs = [pl.BlockSpec((tile, DN_V), row_map),
                pl.BlockSpec((tile, SWA_Q), row_map),
                pl.BlockSpec((nb, lt * n_sub, D_MODEL), lambda i, j: (i, j, 0)),
                pl.BlockSpec((nb, 6, D_MODEL), lambda i, j: (i, 0, 0)),
                pl.BlockSpec((D_MODEL, D_MODEL), const, pipeline_mode=RESIDENT),
                pl.BlockSpec((1, D_MODEL), const),
                pl.BlockSpec((D_MODEL, 2 * D_FF), const, pipeline_mode=RESIDENT),
                pl.BlockSpec((FFN_CONV, 2 * D_FF), const),
                pl.BlockSpec((1, 2 * D_FF), const),
                pl.BlockSpec((D_FF, D_MODEL), const, pipeline_mode=RESIDENT),
                pl.BlockSpec((1, D_MODEL), const)]
    args = [o_dn, o_swa, x, mod3, w_out, norm_ffn_w, w_up, conv_w, conv_b, w_down, final_w]
    if not carry:
        in_specs.append(pl.BlockSpec((nb, FFN_CONV - 1, 2 * D_FF), lambda i, j: (i, 0, 0)))
        args.append(state)
    return pl.pallas_call(
        functools.partial(_post_kernel, nb=nb, lt=lt, carry=carry, n_sub=n_sub),
        grid=(bsz // nb, nt),
        in_specs=in_specs,
        out_specs=[pl.BlockSpec((nb, lt * n_sub, D_MODEL), lambda i, j: (i, j, 0)),
                   pl.BlockSpec((nb, n_state, 2 * D_FF), lambda i, j: (i, 0, 0))],
        out_shape=[jax.ShapeDtypeStruct((bsz, seq, D_MODEL), F32),
                   jax.ShapeDtypeStruct((bsz, n_state, 2 * D_FF), F32)],
        scratch_shapes=[pltpu.VMEM((n_sub, rows, D_MODEL), BF16),
                        pltpu.VMEM((n_sub, rows, D_MODEL), F32),
                        pltpu.VMEM((n_sub, rows, D_MODEL), F32)],
        compiler_params=_cparams(("arbitrary", "arbitrary")),
        name="out_proj_convffn_carry" if carry else "out_proj_convffn_state",
    )(*args)


def _pad_row(vec, offset):
    return jnp.zeros((1, BA_PAD), F32).at[0, offset:offset + vec.shape[0]].set(vec)


def kernel(x_prompt, x_sample, state_dn_conv, state_dn_ssm, cache_swa_k, cache_swa_v, state_ffn_conv, c_prompt, c_sample, rel_bias, final_norm_w, w_ada, b_ada, norm_mix_w, w_in, dn_conv_w, dn_A_log, dn_dt_bias, dn_norm_w, swa_sinks, w_out, norm_ffn_w, ffn_w_up, ffn_conv_w, ffn_conv_b, ffn_w_down):
    bp, lp, _ = x_prompt.shape
    bs, ls, _ = x_sample.shape
    layer = 0

    w_in_t = jnp.transpose(w_in[layer])
    w_out_b = w_out[layer].astype(BF16)
    w_up_b = ffn_w_up[layer].astype(BF16)
    w_dn_b = ffn_w_down[layer].astype(BF16)
    alog_row = _pad_row(dn_A_log[layer], DN_HEADS)
    dt_row = _pad_row(dn_dt_bias[layer], DN_HEADS)
    row = lambda v: v.reshape(1, -1)

    n_c = bp + bs
    n_c_pad = -(-n_c // SUBLANES) * SUBLANES
    c_all = jnp.pad(jnp.concatenate([c_sample, c_prompt], axis=0), ((0, n_c_pad - n_c), (0, 0)))
    mod_s, mod_p = _mod_call(c_all, w_ada[layer], row(b_ada[layer]), bs, bp)

    bias_t, bias_s = _bias_call(rel_bias)
    sinks = swa_sinks[layer]

    def mixer_in(x, mod3, nb, lt, state=None):
        return _in_call(x, mod3, row(norm_mix_w[layer]), w_in_t, dn_conv_w[layer], nb, lt, state)

    def post(o_dn, o_swa, x, mod3, nb, lt, state=None):
        return _post_call(o_dn, o_swa, x, mod3, w_out_b, row(norm_ffn_w[layer]), w_up_b, ffn_conv_w[layer],
                          row(ffn_conv_b[layer]), w_dn_b, row(final_norm_w), nb, lt, state)

    dn_args = (alog_row, dt_row, row(dn_norm_w[layer]))

    qkv, z, ba, sq, sk, sv, p_tail = mixer_in(x_prompt, mod_p, 1, 256)
    o_dn, p_ssm = _dn_call(qkv, z, ba, *dn_args, bp, lp)
    o_swa = _swa_prompt_call(sinks, sq, sk, sv, bias_t, bp, lp)
    y_prompt, p_ffn_tail = post(o_dn, o_swa, x_prompt, mod_p, 1, 256)
    p_dn_conv = p_tail[:, SUBLANES - (DN_CONV - 1):]
    p_fbuf = p_ffn_tail[:, SUBLANES - (FFN_CONV - 1):]
    last_window = lambda t: t.reshape(bp, lp, SWA_KV)[:, lp - WINDOW:].reshape(bp, WINDOW, SWA_KV_HEADS, SWA_HD)
    p_swa_k = last_window(sk)
    p_swa_v = last_window(sv)

    qkv_s, z_s, ba_s, sq_s, sk_s, sv_s, s_dn_conv = mixer_in(x_sample, mod_s, 64, ls, state=state_dn_conv[layer])
    o_dn_s, s_ssm = _dn_call(qkv_s, z_s, ba_s, *dn_args, bs, ls, state=state_dn_ssm[layer])
    o_swa_s, s_k, s_v = _swa_sample_call(sinks, sq_s, sk_s, sv_s,
                                         cache_swa_k[layer].reshape(bs, WINDOW, SWA_KV),
                                         cache_swa_v[layer].reshape(bs, WINDOW, SWA_KV), bias_s, bs, ls)
    y_sample, s_fbuf = post(o_dn_s, o_swa_s, x_sample, mod_s, 32, ls, state=state_ffn_conv[layer])

    return (y_prompt, y_sample, p_dn_conv[None], s_dn_conv[None], p_ssm[None], s_ssm[None],
            p_swa_k[None], s_k.reshape(bs, WINDOW, SWA_KV_HEADS, SWA_HD)[None],
            p_swa_v[None], s_v.reshape(bs, WINDOW, SWA_KV_HEADS, SWA_HD)[None],
            p_fbuf[None], s_fbuf[None])
```

```python
import functools
import math

import numpy as np
import jax
import jax.numpy as jnp
from jax import lax
from jax.experimental import pallas as pl
from jax.experimental.pallas import tpu as pltpu

F32 = jnp.float32
BF16 = jnp.bfloat16

D_MODEL = 1024
PAST_LEN = 16384
DN_HEADS = 4
DN_DK = 128
DN_DV = 128
DN_CONV = 4
SWA_HEADS = 8
SWA_KV_HEADS = 2
SWA_GROUP = SWA_HEADS // SWA_KV_HEADS
SWA_HD = 64
WINDOW = 128
N_BUCKETS = 32
MAX_DISTANCE = 128
D_FF = 2816
FFN_CONV = 3
EPS = 1e-6
NEG_INF = -1e30

DN_QK = DN_HEADS * DN_DK
DN_V = DN_HEADS * DN_DV
DN_CONV_CH = 2 * DN_QK + DN_V
SWA_Q = SWA_HEADS * SWA_HD
SWA_KV = SWA_KV_HEADS * SWA_HD
BA_PAD = 128
SUBLANES = 8
UNIT = 128
POST_SUB_TILES = 2
POST_LOOKAHEAD_CHUNKS = 3
FFN_CHUNK = 256
VMEM_LIMIT = 56 * 1024 * 1024
RESIDENT = pl.Buffered(1)


def _cparams(sem):
    return pltpu.CompilerParams(dimension_semantics=sem, vmem_limit_bytes=VMEM_LIMIT)


def _bf(x):
    return x.astype(BF16)


def _dot(a, b):
    return jnp.dot(_bf(a), _bf(b), preferred_element_type=F32)


def _dot_nt(a, b):
    return lax.dot_general(_bf(a), _bf(b), (((1,), (1,)), ((), ())), preferred_element_type=F32)


def _split2(x):
    hi = _bf(x)
    lo = _bf(x - hi.astype(F32))
    return hi, lo


def _dot_x3(a, b):
    ah, al = _split2(a)
    bh, bl = _split2(b)
    d = functools.partial(jnp.dot, preferred_element_type=F32)
    return d(ah, bh) + (d(al, bh) + d(ah, bl))


def _dot_mask(m, x):
    hi = _bf(x)
    r = x - hi.astype(F32)
    mid = _bf(r)
    lo = _bf(r - mid.astype(F32))
    d = functools.partial(jnp.dot, preferred_element_type=F32)
    return d(m, hi) + (d(m, mid) + d(m, lo))


def _sigmoid(x):
    return 1.0 / (1.0 + jnp.exp(-x))


def _silu(x):
    return x * _sigmoid(x)


def _softplus(x):
    return jnp.maximum(x, 0.0) + jnp.log1p(jnp.exp(-jnp.abs(x)))


def _rms(x, w):
    ms = jnp.mean(x * x, axis=-1, keepdims=True)
    return x * lax.rsqrt(ms + EPS) * w


def _l2norm(t):
    return t * lax.rsqrt(jnp.sum(t * t, axis=-1, keepdims=True) + EPS)


def _rows(m3, nb, lt):
    return jnp.broadcast_to(m3, (nb, lt, m3.shape[-1])).reshape(nb * lt, m3.shape[-1])


def _causal_conv(x, prev, w, nb, lt):
    width = w.shape[0]
    rows, ch = x.shape
    if nb == 1 and lt > SUBLANES:
        tiles = jnp.concatenate([prev, x.reshape(lt // SUBLANES, SUBLANES, ch)], axis=0)
        sub = lax.broadcasted_iota(jnp.int32, (1, SUBLANES, 1), 1)
        out = tiles[1:] * w[width - 1:width, :]
        for j in range(1, width):
            rot = pltpu.roll(tiles, j, axis=1)
            out = out + jnp.where(sub >= j, rot[1:], rot[:-1]) * w[width - 1 - j:width - j, :]
        return out.reshape(rows, ch)
    tmod = lax.broadcasted_iota(jnp.int32, (rows, 1), 0) & (lt - 1)
    out = x * w[width - 1:width, :]
    for j in range(1, width):
        sh = pltpu.roll(x, j, axis=0)
        for t in range(j):
            p = width - 1 - j + t
            sh = jnp.where(tmod == t, _rows(prev[:, p:p + 1, :], nb, lt), sh)
        out = out + sh * w[width - 1 - j:width - j, :]
    return out


def _last_rows(x, nb, lt, n_state):
    ch = x.shape[-1]
    if nb == 1 and lt > SUBLANES:
        return x[lt - SUBLANES:].reshape(1, SUBLANES, ch)
    return x.reshape(nb, lt, ch)[:, lt - n_state:, :]


def _state_rows(nb, lt, n_state):
    return SUBLANES if (nb == 1 and lt > SUBLANES) else n_state


MOD_PARTS = 6


def _mod_kernel(c_ref, w_ref, b_ref, os_ref, op_ref, *, n_s, n_p):
    part = pl.program_id(0)
    res = _dot_x3(_silu(c_ref[...]), w_ref[...]) + b_ref[...]
    for k in range(MOD_PARTS):
        @pl.when(part == k)
        def _():
            os_ref[:, k, :] = res[:n_s]
            op_ref[:, k, :] = res[n_s:n_s + n_p]


def _mod_call(c_all, w_ada, b_ada, n_s, n_p):
    rows = c_all.shape[0]
    return pl.pallas_call(
        functools.partial(_mod_kernel, n_s=n_s, n_p=n_p),
        grid=(MOD_PARTS,),
        in_specs=[pl.BlockSpec((rows, D_MODEL), lambda k: (0, 0)),
                  pl.BlockSpec((D_MODEL, D_MODEL), lambda k: (0, k)),
                  pl.BlockSpec((1, D_MODEL), lambda k: (0, k))],
        out_specs=[pl.BlockSpec((n_s, MOD_PARTS, D_MODEL), lambda k: (0, 0, 0)),
                   pl.BlockSpec((n_p, MOD_PARTS, D_MODEL), lambda k: (0, 0, 0))],
        out_shape=[jax.ShapeDtypeStruct((n_s, MOD_PARTS, D_MODEL), F32),
                   jax.ShapeDtypeStruct((n_p, MOD_PARTS, D_MODEL), F32)],
        compiler_params=_cparams(("arbitrary",)),
        name="adaln_mod",
    )(c_all, w_ada, b_ada)


IN_SPLIT = (DN_CONV_CH, DN_V, BA_PAD, SWA_Q, SWA_KV, SWA_KV)


IN_SUB_TILES = 4
IN_LOOKAHEAD_SLABS = 1
W_PREP_ROWS = 256
IN_SLAB = 2 * DN_DK


def _in_kernel(*refs, nb, lt, carry, n_sub):
    if carry:
        (x_ref, mod_ref, nw_ref, w_ref, cw_ref,
         qkv_ref, z_ref, ba_ref, sq_ref, sk_ref, sv_ref, tail_ref, h_scr, wdn_ref, wba_ref, wsw_ref) = refs
        prev_ref = tail_ref

        @pl.when(pl.program_id(1) == 0)
        def _():
            tail_ref[...] = jnp.zeros_like(tail_ref)
    else:
        (x_ref, mod_ref, nw_ref, w_ref, cw_ref, prev_ref,
         qkv_ref, z_ref, ba_ref, sq_ref, sk_ref, sv_ref, tail_ref, h_scr, wdn_ref, wba_ref, wsw_ref) = refs

    @pl.when((pl.program_id(0) == 0) & (pl.program_id(1) == 0))
    def _():
        ba_lo = DN_CONV_CH + DN_V
        n_ba = 2 * DN_HEADS
        for r in range(0, ba_lo, W_PREP_ROWS):
            wdn_ref[:, r:r + W_PREP_ROWS] = _bf(w_ref[r:r + W_PREP_ROWS, :].T)
        row = lax.broadcasted_iota(jnp.int32, (BA_PAD, 1), 0)
        wba_ref[...] = _bf(jnp.where(row < n_ba, w_ref[ba_lo:ba_lo + BA_PAD, :], 0.0).T)
        for r in range(0, SWA_Q + 2 * SWA_KV, W_PREP_ROWS):
            wsw_ref[:, r:r + W_PREP_ROWS] = _bf(w_ref[ba_lo + n_ba + r:ba_lo + n_ba + r + W_PREP_ROWS, :].T)

    rows = nb * lt

    def norm(sub):
        x = x_ref[:, sub * lt:(sub + 1) * lt, :]
        ms = jnp.mean(x * x, axis=-1, keepdims=True)
        y = x * lax.rsqrt(ms + EPS) * nw_ref[...]
        h = y * (1.0 + mod_ref[:, 1:2, :]) + mod_ref[:, 0:1, :]
        h_scr[sub] = _bf(h.reshape(rows, D_MODEL))

    def proj(sub, w_bf_ref, lo, n):
        return jnp.dot(h_scr[sub], w_bf_ref[:, lo:lo + n], preferred_element_type=F32)

    def plain_proj(sub, i):
        rr = slice(sub * rows, (sub + 1) * rows)
        if i < 2:
            z_ref[rr, i * IN_SLAB:(i + 1) * IN_SLAB] = proj(sub, wdn_ref, DN_CONV_CH + i * IN_SLAB, IN_SLAB)
        elif i < 4:
            sq_ref[rr, (i - 2) * IN_SLAB:(i - 1) * IN_SLAB] = proj(sub, wsw_ref, (i - 2) * IN_SLAB, IN_SLAB)
        elif i == 4:
            kv = proj(sub, wsw_ref, SWA_Q, 2 * SWA_KV)
            sk_ref[rr, :] = kv[:, :SWA_KV]
            sv_ref[rr, :] = kv[:, SWA_KV:]
        else:
            ba_ref[rr, :] = proj(sub, wba_ref, 0, BA_PAD)

    def conv_slab(sub, slab, raw):
        rr = slice(sub * rows, (sub + 1) * rows)
        cols = slice(slab * IN_SLAB, (slab + 1) * IN_SLAB)
        c = _silu(_causal_conv(raw, prev_ref[:, :, cols], cw_ref[:, cols], nb, lt))
        tail_ref[:, :, cols] = _last_rows(raw, nb, lt, DN_CONV - 1)
        if slab * IN_SLAB < 2 * DN_QK:
            scale = DN_DK ** -0.5 if slab * IN_SLAB < DN_QK else 1.0
            c = jnp.concatenate([_l2norm(c[:, i * DN_DK:(i + 1) * DN_DK]) * scale
                                 for i in range(IN_SLAB // DN_DK)], axis=1)
        qkv_ref[rr, cols] = c

    n_slabs = DN_CONV_CH // IN_SLAB
    norm(0)
    for sub in range(n_sub):
        raw_next = proj(sub, wdn_ref, 0, IN_SLAB)
        for slab in range(n_slabs):
            raw = raw_next
            plain_proj(sub, slab)
            if slab + 1 < n_slabs:
                raw_next = proj(sub, wdn_ref, (slab + 1) * IN_SLAB, IN_SLAB)
            if sub + 1 < n_sub and slab == n_slabs - 1 - IN_LOOKAHEAD_SLABS:
                norm(sub + 1)
            conv_slab(sub, slab, raw)


def _in_call(x, mod3, norm_w, w_in_t, conv_w, nb, lt, state=None):
    carry = state is None
    n_sub = IN_SUB_TILES if carry else 1
    bsz, seq, _ = x.shape
    nt = seq // (lt * n_sub)
    rows = nb * lt
    tile = rows * n_sub
    n_tok = bsz * seq
    n_state = _state_rows(nb, lt, DN_CONV - 1)
    row_map = lambda i, j: (i * nt + j, 0)
    const = lambda i, j: (0, 0)
    per_seq = lambda i, j: (i, 0, 0)
    in_specs = [pl.BlockSpec((nb, lt * n_sub, D_MODEL), lambda i, j: (i, j, 0)),
                pl.BlockSpec((nb, 6, D_MODEL), per_seq),
                pl.BlockSpec((1, D_MODEL), const),
                pl.BlockSpec(w_in_t.shape, const, pipeline_mode=RESIDENT),
                pl.BlockSpec((DN_CONV, DN_CONV_CH), const)]
    args = [x, mod3, norm_w, w_in_t, conv_w]
    if not carry:
        in_specs.append(pl.BlockSpec((nb, DN_CONV - 1, DN_CONV_CH), per_seq))
        args.append(state)
    return pl.pallas_call(
        functools.partial(_in_kernel, nb=nb, lt=lt, carry=carry, n_sub=n_sub),
        grid=(bsz // nb, nt),
        in_specs=in_specs,
        out_specs=[pl.BlockSpec((tile, n), row_map) for n in IN_SPLIT]
        + [pl.BlockSpec((nb, n_state, DN_CONV_CH), per_seq)],
        out_shape=[jax.ShapeDtypeStruct((n_tok, n), F32) for n in IN_SPLIT]
        + [jax.ShapeDtypeStruct((bsz, n_state, DN_CONV_CH), F32)],
        scratch_shapes=[pltpu.VMEM((n_sub, rows, D_MODEL), BF16),
                        pltpu.VMEM((D_MODEL, DN_CONV_CH + DN_V), BF16),
                        pltpu.VMEM((D_MODEL, BA_PAD), BF16),
                        pltpu.VMEM((D_MODEL, SWA_Q + 2 * SWA_KV), BF16)],
        compiler_params=_cparams(("arbitrary", "arbitrary")),
        name="norm_in_proj_carry" if carry else "norm_in_proj_state",
    )(*args)


INV_BASE_SHIFT = 2
DN_UNITS_CARRY = 4
SWA_BLOCKS_PER_STEP = 4
DN_UNITS_STATE = 1


def _unit_lower_inverses(a_mats, ri, ci, chunk_shift):
    def blocks(s):
        return (ri >> s) == (ci >> s)

    base = min(INV_BASE_SHIFT, chunk_shift)
    xs = [jnp.where(blocks(base), -a, 0.0) for a in a_mats]
    ts = [jnp.where(ri == ci, 1.0, n) for n in xs]
    for lvl in range(1, base):
        xs = [_dot(x, x) for x in xs]
        ts = [t + _dot(t, x) for t, x in zip(ts, xs)]
    for s in range(base, chunk_shift):
        sel = blocks(s + 1) & jnp.logical_not(blocks(s))
        ets = [_dot(jnp.where(sel, a, 0.0), t) for a, t in zip(a_mats, ts)]
        ts = [t - _dot(t, et) for t, et in zip(ts, ets)]
    return ts


def _dn_kernel(*refs, nb, chunk, carry, units):
    if carry:
        (qkv_ref, z_ref, ba_ref, alog_ref, dt_ref, nw_ref, o_ref, s_ref) = refs
        s0_ref = s_ref

        @pl.when(pl.program_id(1) == 0)
        def _():
            s_ref[...] = jnp.zeros_like(s_ref)
    else:
        (qkv_ref, z_ref, ba_ref, alog_ref, dt_ref, nw_ref, s0_ref, o_ref, s_ref) = refs

    ri = lax.broadcasted_iota(jnp.int32, (UNIT, UNIT), 0)
    ci = lax.broadcasted_iota(jnp.int32, (UNIT, UNIT), 1)
    shift = int(math.log2(chunk))
    same = (ri >> shift) == (ci >> shift)
    incl = same & (ri >= ci)
    strict = same & (ri > ci)

    ba = ba_ref[...]
    beta_full = _sigmoid(ba)
    g_full = -jnp.exp(alog_ref[...]) * _softplus(ba + dt_ref[...])
    masks = jnp.concatenate([jnp.where(incl, 1.0, 0.0), jnp.where(same, 1.0, 0.0)], axis=0).astype(BF16)
    gsums = [_dot_mask(masks, g_full[u * UNIT:(u + 1) * UNIT]) for u in range(units)]
    g_cum = [g[:UNIT] for g in gsums]
    g_tot = [g[UNIT:] for g in gsums]
    g_cum_t = [g.T for g in g_cum]

    probs = [(u, h) for u in range(units) for h in range(DN_HEADS)]
    n_p = len(probs)
    rs = lambda u: slice(u * UNIT, (u + 1) * UNIT)

    def head_cols(base):
        return [qkv_ref[rs(u), base + h * DN_DK:base + (h + 1) * DN_DK] for u, h in probs]

    q = head_cols(0)
    k = head_cols(DN_QK)
    v = head_cols(2 * DN_QK)
    gc = [g_cum[u][:, DN_HEADS + h:DN_HEADS + h + 1] for u, h in probs]
    gr = [g_cum_t[u][DN_HEADS + h:DN_HEADS + h + 1, :] for u, h in probs]
    gt = [g_tot[u][:, DN_HEADS + h:DN_HEADS + h + 1] for u, h in probs]
    bc = [beta_full[rs(u), h:h + 1] for u, h in probs]
    decay = [jnp.where(incl, jnp.exp(jnp.where(incl, gc[p] - gr[p], 0.0)), 0.0) for p in range(n_p)]
    e_g = [jnp.exp(gc[p]) for p in range(n_p)]
    kq = [_dot_nt(jnp.concatenate([k[p], q[p]], axis=0), k[p]) for p in range(n_p)]
    qk = [kq[p][UNIT:] * decay[p] for p in range(n_p)]
    a_mats = [jnp.where(strict, bc[p] * kq[p][:UNIT] * decay[p], 0.0) for p in range(n_p)]
    t_inv = _unit_lower_inverses(a_mats, ri, ci, shift)
    wvk = [_dot(t_inv[p], jnp.concatenate([v[p] * bc[p], k[p] * (bc[p] * e_g[p])], axis=1)) for p in range(n_p)]
    w_v = [w[:, :DN_DV] for w in wvk]
    w_k = [w[:, DN_DV:] for w in wvk]
    q_dec = [q[p] * e_g[p] for p in range(n_p)]
    k_tail_t = [(k[p] * jnp.exp(gt[p] - gc[p])).T for p in range(n_p)]
    c_dec = [jnp.exp(gt[p]) for p in range(n_p)]

    outs = {}
    if carry:
        state = [s_ref[0, h] for h in range(DN_HEADS)]
        for u in range(units):
            ps = [u * DN_HEADS + h for h in range(DN_HEADS)]
            r = [_dot(jnp.concatenate([w_k[p], q_dec[p]], axis=0), state[h]) for h, p in enumerate(ps)]
            uu = [w_v[p] - r[h][:UNIT] for h, p in enumerate(ps)]
            for h, p in enumerate(ps):
                outs[p] = r[h][UNIT:] + _dot(qk[p], uu[h])
            state = [state[h] * c_dec[p][0:1, :] + _dot(k_tail_t[p], uu[h]) for h, p in enumerate(ps)]
        for h in range(DN_HEADS):
            s_ref[0, h] = state[h]
    else:
        per_unit = nb // units
        for p, (u, h) in enumerate(probs):
            us, qs = [], []
            for s in range(per_unit):
                lo = s * chunk
                lhs = jnp.concatenate([w_k[p][lo:lo + chunk], q_dec[p][lo:lo + chunk]], axis=0)
                r = jnp.dot(lhs, s0_ref[u * per_unit + s, h], preferred_element_type=F32)
                us.append(w_v[p][lo:lo + chunk] - r[:chunk])
                qs.append(r[chunk:])
            uu = jnp.concatenate(us, axis=0)
            outs[p] = jnp.concatenate(qs, axis=0) + _dot(qk[p], uu)
            for s in range(per_unit):
                lo = s * chunk
                upd = jnp.dot(k_tail_t[p][:, lo:lo + chunk], uu[lo:lo + chunk], preferred_element_type=F32)
                s_ref[u * per_unit + s, h] = s0_ref[u * per_unit + s, h] * c_dec[p][lo:lo + 1, :] + upd

    for p, (u, h) in enumerate(probs):
        zz = z_ref[rs(u), h * DN_DV:(h + 1) * DN_DV]
        o_ref[rs(u), h * DN_DV:(h + 1) * DN_DV] = _rms(outs[p], nw_ref[...]) * _silu(zz)


def _dn_call(qkv, z, ba, alog_row, dt_row, norm_w, bsz, seq, state=None):
    carry = state is None
    units = DN_UNITS_CARRY if carry else DN_UNITS_STATE
    rows = units * UNIT
    nb = 1 if carry else rows // seq
    chunk = UNIT if carry else seq
    nt = seq * nb // rows
    n_tok = bsz * seq
    row_map = lambda i, j: (i * nt + j, 0)
    const = lambda i, j: (0, 0)
    in_specs = [pl.BlockSpec((rows, DN_CONV_CH), row_map),
                pl.BlockSpec((rows, DN_V), row_map),
                pl.BlockSpec((rows, BA_PAD), row_map),
                pl.BlockSpec((1, BA_PAD), const),
                pl.BlockSpec((1, BA_PAD), const),
                pl.BlockSpec((1, DN_DV), const)]
    args = [qkv, z, ba, alog_row, dt_row, norm_w]
    if not carry:
        in_specs.append(pl.BlockSpec((nb, DN_HEADS, DN_DK, DN_DV), lambda i, j: (i, 0, 0, 0)))
        args.append(state)
    return pl.pallas_call(
        functools.partial(_dn_kernel, nb=nb, chunk=chunk, carry=carry, units=units),
        grid=(bsz // nb, nt),
        in_specs=in_specs,
        out_specs=[pl.BlockSpec((rows, DN_V), row_map),
                   pl.BlockSpec((nb, DN_HEADS, DN_DK, DN_DV), lambda i, j: (i, 0, 0, 0))],
        out_shape=[jax.ShapeDtypeStruct((n_tok, DN_V), F32),
                   jax.ShapeDtypeStruct((bsz, DN_HEADS, DN_DK, DN_DV), F32)],
        compiler_params=_cparams(("arbitrary", "arbitrary")),
        name="gated_deltanet_carry" if carry else "gated_deltanet_state",
    )(*args)


MIX_SUB_TILES = 2
MIX_ROWS = MIX_SUB_TILES * 2 * UNIT


def _mix_kernel(x_ref, mod_ref, nw_ref, w_ref, cw_ref, alog_ref, dt_ref, dnw_ref,
                sq_ref, sk_ref, sv_ref, tail_ref, o_ref, s_ref,
                h_scr, wdn_ref, wba_ref, wsw_ref, qkv_scr, z_scr, ba_scr):
    _in_kernel(x_ref, mod_ref, nw_ref, w_ref, cw_ref,
               qkv_scr, z_scr, ba_scr, sq_ref, sk_ref, sv_ref, tail_ref, h_scr, wdn_ref, wba_ref, wsw_ref,
               nb=1, lt=2 * UNIT, carry=True, n_sub=MIX_SUB_TILES)
    _dn_kernel(qkv_scr, z_scr, ba_scr, alog_ref, dt_ref, dnw_ref, o_ref, s_ref,
               nb=1, chunk=UNIT, carry=True, units=MIX_ROWS // UNIT)


def _mix_call(x, mod3, norm_w, w_in_t, conv_w, alog_row, dt_row, dn_norm_w):
    bsz, seq, _ = x.shape
    nt = seq // MIX_ROWS
    n_tok = bsz * seq
    row_map = lambda i, j: (i * nt + j, 0)
    const = lambda i, j: (0, 0)
    per_seq = lambda i, j: (i, 0, 0)
    swa_cols = (SWA_Q, SWA_KV, SWA_KV)
    return pl.pallas_call(
        _mix_kernel,
        grid=(bsz, nt),
        in_specs=[pl.BlockSpec((1, MIX_ROWS, D_MODEL), lambda i, j: (i, j, 0)),
                  pl.BlockSpec((1, 6, D_MODEL), per_seq),
                  pl.BlockSpec((1, D_MODEL), const),
                  pl.BlockSpec(w_in_t.shape, const, pipeline_mode=RESIDENT),
                  pl.BlockSpec((DN_CONV, DN_CONV_CH), const),
                  pl.BlockSpec((1, BA_PAD), const),
                  pl.BlockSpec((1, BA_PAD), const),
                  pl.BlockSpec((1, DN_DV), const)],
        out_specs=[pl.BlockSpec((MIX_ROWS, n), row_map) for n in swa_cols]
        + [pl.BlockSpec((1, SUBLANES, DN_CONV_CH), per_seq),
           pl.BlockSpec((MIX_ROWS, DN_V), row_map),
           pl.BlockSpec((1, DN_HEADS, DN_DK, DN_DV), lambda i, j: (i, 0, 0, 0))],
        out_shape=[jax.ShapeDtypeStruct((n_tok, n), F32) for n in swa_cols]
        + [jax.ShapeDtypeStruct((bsz, SUBLANES, DN_CONV_CH), F32),
           jax.ShapeDtypeStruct((n_tok, DN_V), F32),
           jax.ShapeDtypeStruct((bsz, DN_HEADS, DN_DK, DN_DV), F32)],
        scratch_shapes=[pltpu.VMEM((MIX_SUB_TILES, 2 * UNIT, D_MODEL), BF16),
                        pltpu.VMEM((D_MODEL, DN_CONV_CH + DN_V), BF16),
                        pltpu.VMEM((D_MODEL, BA_PAD), BF16),
                        pltpu.VMEM((D_MODEL, SWA_Q + 2 * SWA_KV), BF16),
                        pltpu.VMEM((MIX_ROWS, DN_CONV_CH), F32),
                        pltpu.VMEM((MIX_ROWS, DN_V), F32),
                        pltpu.VMEM((MIX_ROWS, BA_PAD), F32)],
        compiler_params=_cparams(("arbitrary", "arbitrary")),
        name="in_proj_deltanet_carry",
    )(x, mod3, norm_w, w_in_t, conv_w, alog_row, dt_row, dn_norm_w)


def _bucket_table():
    i = np.arange(WINDOW, dtype=np.int64)[:, None]
    j = np.arange(2 * WINDOW, dtype=np.int64)[None, :]
    d = np.maximum(i + WINDOW - j, 0)
    exact = N_BUCKETS // 2
    logv = (np.log(np.maximum(d, 1).astype(np.float32) / np.float32(exact)).astype(np.float32)
            / np.float32(math.log(MAX_DISTANCE / exact)))
    large = np.minimum(exact + (logv * np.float32(N_BUCKETS - exact)).astype(np.int32), N_BUCKETS - 1)
    return np.where(d < exact, d, large).astype(np.int32)


def _bias_lookup(rb_ref, bucket, head):
    acc = jnp.zeros(bucket.shape, F32)
    for b in range(N_BUCKETS):
        acc = jnp.where(bucket == b, rb_ref[b, head], acc)
    return acc


def _bias_kernel(rb_ref, bucket_t_ref, bucket_ref, ot_ref, os_ref):
    bucket_t = bucket_t_ref[...]
    kj = lax.broadcasted_iota(jnp.int32, bucket_t.shape, 0)
    qi = lax.broadcasted_iota(jnp.int32, bucket_t.shape, 1)
    dist = qi + WINDOW - kj
    valid = (dist >= 0) & (dist < WINDOW)
    for hk in range(SWA_KV_HEADS):
        for par in range(2):
            for st in range(SWA_GROUP // 2):
                head = hk * SWA_GROUP + 2 * st + par
                gen = jnp.where(valid, _bias_lookup(rb_ref, bucket_t, head), NEG_INF)
                ot_ref[1, hk, par, :, st * WINDOW:(st + 1) * WINDOW] = gen
                ot_ref[0, hk, par, :, st * WINDOW:(st + 1) * WINDOW] = jnp.where(kj >= WINDOW, gen, NEG_INF)
    bucket = bucket_ref[...]
    qi = lax.broadcasted_iota(jnp.int32, bucket.shape, 0)
    kj = lax.broadcasted_iota(jnp.int32, bucket.shape, 1)
    dist = qi + WINDOW - kj
    valid = (dist >= 0) & (dist < WINDOW)
    for head in range(SWA_HEADS):
        os_ref[head] = jnp.where(valid, _bias_lookup(rb_ref, bucket, head), NEG_INF)


BIAS_T_SHAPE = (2, SWA_KV_HEADS, 2, 2 * WINDOW, (SWA_GROUP // 2) * WINDOW)
BIAS_S_SHAPE = (SWA_HEADS, SUBLANES, 2 * WINDOW)


def _bias_call(rel_bias):
    bucket = _bucket_table()
    return pl.pallas_call(
        _bias_kernel,
        in_specs=[pl.BlockSpec(memory_space=pltpu.SMEM),
                  pl.BlockSpec((2 * WINDOW, WINDOW), lambda: (0, 0)),
                  pl.BlockSpec((SUBLANES, 2 * WINDOW), lambda: (0, 0))],
        out_specs=[pl.BlockSpec(BIAS_T_SHAPE, lambda: (0,) * len(BIAS_T_SHAPE)),
                   pl.BlockSpec(BIAS_S_SHAPE, lambda: (0,) * len(BIAS_S_SHAPE))],
        out_shape=[jax.ShapeDtypeStruct(BIAS_T_SHAPE, F32), jax.ShapeDtypeStruct(BIAS_S_SHAPE, F32)],
        name="swa_rel_bias_table",
    )(rel_bias, jnp.asarray(np.ascontiguousarray(bucket.T)), jnp.asarray(bucket[:SUBLANES]))


def _softmax_sink_parts(s, sink):
    m = jnp.maximum(jnp.max(s, axis=0, keepdims=True), sink)
    p = jnp.exp(s - m)
    return p, 1.0 / (jnp.sum(p, axis=0, keepdims=True) + jnp.exp(sink - m))


def _half_lane_variants(full, hk, lo_half):
    rolled = pltpu.roll(full, SWA_HD, axis=1)
    low_src, high_src = (full, rolled) if hk == 0 else (rolled, full)
    return jnp.where(lo_half, low_src, 0.0), jnp.where(lo_half, 0.0, high_src)


def _swa_prompt_kernel(sink_ref, q_ref, kp_ref, kc_ref, vp_ref, vc_ref, bias_ref, o_ref, *, n_blk):
    step = pl.program_id(1)
    lo_half = lax.broadcasted_iota(jnp.int32, (1, 2 * SWA_HD), 1) < SWA_HD
    lo_rows = lax.broadcasted_iota(jnp.int32, (2 * SWA_HD, 1), 0) < SWA_HD
    q = _bf(q_ref[...] * (SWA_HD ** -0.5))
    keys = jnp.concatenate([kp_ref[...], kc_ref[...]], axis=0)
    vals = jnp.concatenate([vp_ref[...], vc_ref[...]], axis=0)
    k_var = [[_bf(t) for t in _half_lane_variants(keys, hk, lo_half)] for hk in range(SWA_KV_HEADS)]
    v_var_t = [[_bf(t.T) for t in _half_lane_variants(vals, hk, lo_half)] for hk in range(SWA_KV_HEADS)]
    n_stack = SWA_GROUP // 2
    sinks = [[jnp.concatenate([jnp.full((1, WINDOW), sink_ref[hk * SWA_GROUP + 2 * st + par], F32)
                               for st in range(n_stack)], axis=1) for par in range(2)]
             for hk in range(SWA_KV_HEADS)]
    nt_dims = (((1,), (1,)), ((), ()))

    def scores(b):
        rows = slice(b * WINDOW, (b + 1) * WINDOW)
        win = slice(b * WINDOW, (b + 2) * WINDOW)
        variant = jnp.where(step == 0, 0, 1) if b == 0 else 1
        out = []
        for hk in range(SWA_KV_HEADS):
            q2 = jnp.concatenate([q[rows, (hk * n_stack + st) * 2 * SWA_HD:(hk * n_stack + st + 1) * 2 * SWA_HD]
                                  for st in range(n_stack)], axis=0)
            out.append([lax.dot_general(k_var[hk][par][win], q2, nt_dims, preferred_element_type=F32)
                        + bias_ref[variant, hk, par] for par in range(2)])
        return out

    s_next = scores(0)
    for b in range(n_blk):
        s_cur = s_next
        if b + 1 < n_blk:
            s_next = scores(b + 1)
        rows = slice(b * WINDOW, (b + 1) * WINDOW)
        win = slice(b * WINDOW, (b + 2) * WINDOW)
        for hk in range(SWA_KV_HEADS):
            parts = [_softmax_sink_parts(s_cur[hk][par], sinks[hk][par]) for par in range(2)]
            o_t = (jnp.dot(v_var_t[hk][0][:, win], _bf(parts[0][0]), preferred_element_type=F32)
                   + jnp.dot(v_var_t[hk][1][:, win], _bf(parts[1][0]), preferred_element_type=F32))
            o_t = o_t * jnp.where(lo_rows, parts[0][1], parts[1][1])
            for st in range(n_stack):
                lo = (hk * n_stack + st) * 2 * SWA_HD
                o_ref[rows, lo:lo + 2 * SWA_HD] = o_t[:, st * WINDOW:(st + 1) * WINDOW].T


def _swa_prompt_call(sinks, sq, sk, sv, bias, bsz, seq):
    n_blk = SWA_BLOCKS_PER_STEP
    tile = n_blk * WINDOW
    nt = seq // tile
    cur = lambda b, i: (b * nt + i, 0)
    prv = lambda b, i: (b * nt * n_blk + jnp.maximum(i * n_blk - 1, 0), 0)
    return pl.pallas_call(
        functools.partial(_swa_prompt_kernel, n_blk=n_blk),
        grid=(bsz, nt),
        in_specs=[pl.BlockSpec(memory_space=pltpu.SMEM),
                  pl.BlockSpec((tile, SWA_Q), cur),
                  pl.BlockSpec((WINDOW, SWA_KV), prv),
                  pl.BlockSpec((tile, SWA_KV), cur),
                  pl.BlockSpec((WINDOW, SWA_KV), prv),
                  pl.BlockSpec((tile, SWA_KV), cur),
                  pl.BlockSpec(BIAS_T_SHAPE, lambda b, i: (0,) * len(BIAS_T_SHAPE))],
        out_specs=pl.BlockSpec((tile, SWA_Q), cur),
        out_shape=jax.ShapeDtypeStruct((bsz * seq, SWA_Q), F32),
        compiler_params=_cparams(("arbitrary", "arbitrary")),
        name="swa_banded",
    )(sinks, sq, sk, sk, sv, sv, bias)


def _swa_sample_kernel(sink_ref, q_ref, kn_ref, vn_ref, kc_ref, vc_ref, bias_ref, o_ref, ko_ref, vo_ref, *, nb, lt):
    lo_half = lax.broadcasted_iota(jnp.int32, (1, 2 * SWA_HD), 1) < SWA_HD
    q_all = q_ref[...] * (SWA_HD ** -0.5)
    kn_all = kn_ref[...]
    vn_all = vn_ref[...]
    pad = jnp.zeros((WINDOW - lt, SWA_KV), F32)
    bias = jnp.concatenate([bias_ref[h, 0:lt, :] for h in range(SWA_HEADS)], axis=0)
    sink = jnp.concatenate([jnp.full((lt, 1), sink_ref[h], F32) for h in range(SWA_HEADS)], axis=0)
    nt_dims = (((1,), (1,)), ((), ()))

    def to_half(tile, src_half, dst_half):
        return tile if src_half == dst_half else pltpu.roll(tile, SWA_HD, axis=1)

    lhs, keys, vals = [], [], []
    for s in range(nb):
        rows = slice(s * lt, (s + 1) * lt)
        kc, vc, kn, vn = kc_ref[s], vc_ref[s], kn_all[rows], vn_all[rows]
        ko_ref[s] = jnp.concatenate([kc[lt:], kn], axis=0)
        vo_ref[s] = jnp.concatenate([vc[lt:], vn], axis=0)
        keys.append(_bf(jnp.concatenate([kc, kn, pad], axis=0)))
        vals.append(_bf(jnp.concatenate([vc, vn, pad], axis=0)))
        q = q_all[rows]
        tiles = []
        for h in range(SWA_HEADS):
            hk = h // SWA_GROUP
            t = to_half(q[:, (h // 2) * 2 * SWA_HD:(h // 2 + 1) * 2 * SWA_HD], h % 2, hk)
            tiles.append(jnp.where(lo_half if hk == 0 else jnp.logical_not(lo_half), t, 0.0))
        lhs.append(_bf(jnp.concatenate(tiles, axis=0)))
    scores = [lax.dot_general(lhs[s], keys[s], nt_dims, preferred_element_type=F32) + bias for s in range(nb)]
    m = [jnp.maximum(jnp.max(sc, axis=-1, keepdims=True), sink) for sc in scores]
    p = [jnp.exp(sc - mm) for sc, mm in zip(scores, m)]
    rinv = [1.0 / (jnp.sum(pp, axis=-1, keepdims=True) + jnp.exp(sink - mm)) for pp, mm in zip(p, m)]
    res = [jnp.dot(_bf(p[s]), vals[s], preferred_element_type=F32) * rinv[s] for s in range(nb)]
    rows_out = []
    for s in range(nb):
        tiles = []
        for pair in range(SWA_HEADS // 2):
            hk = (2 * pair) // SWA_GROUP
            low = to_half(res[s][(2 * pair) * lt:(2 * pair + 1) * lt], hk, 0)
            high = to_half(res[s][(2 * pair + 1) * lt:(2 * pair + 2) * lt], hk, 1)
            tiles.append(jnp.where(lo_half, low, high))
        rows_out.append(jnp.concatenate(tiles, axis=1))
    o_ref[...] = jnp.concatenate(rows_out, axis=0)


def _swa_sample_call(sinks, sq, sk, sv, cache_k, cache_v, bias, bsz, seq):
    nb = UNIT // seq
    rows = lambda i: (i, 0)
    seqs = lambda i: (i, 0, 0)
    return pl.pallas_call(
        functools.partial(_swa_sample_kernel, nb=nb, lt=seq),
        grid=(bsz // nb,),
        in_specs=[pl.BlockSpec(memory_space=pltpu.SMEM),
                  pl.BlockSpec((UNIT, SWA_Q), rows),
                  pl.BlockSpec((UNIT, SWA_KV), rows),
                  pl.BlockSpec((UNIT, SWA_KV), rows),
                  pl.BlockSpec((nb, WINDOW, SWA_KV), seqs),
                  pl.BlockSpec((nb, WINDOW, SWA_KV), seqs),
                  pl.BlockSpec(BIAS_S_SHAPE, lambda i: (0,) * len(BIAS_S_SHAPE))],
        out_specs=[pl.BlockSpec((UNIT, SWA_Q), rows),
                   pl.BlockSpec((nb, WINDOW, SWA_KV), seqs),
                   pl.BlockSpec((nb, WINDOW, SWA_KV), seqs)],
        out_shape=[jax.ShapeDtypeStruct((bsz * seq, SWA_Q), F32),
                   jax.ShapeDtypeStruct((bsz, WINDOW, SWA_KV), F32),
                   jax.ShapeDtypeStruct((bsz, WINDOW, SWA_KV), F32)],
        compiler_params=_cparams(("arbitrary",)),
        name="swa_cached",
    )(sinks, sq, sk, sv, cache_k, cache_v, bias)


def _post_kernel(*refs, nb, lt, carry, n_sub):
    if carry:
        (odn_ref, oswa_ref, x_ref, mod_ref, wout_ref, nfw_ref, wup_ref, cw_ref, cb_ref, wdn_ref, fnw_ref,
         y_ref, fbuf_ref, h_scr, x1_scr, acc_scr) = refs
        prev_ref = fbuf_ref

        @pl.when(pl.program_id(1) == 0)
        def _():
            fbuf_ref[...] = jnp.zeros_like(fbuf_ref)
    else:
        (odn_ref, oswa_ref, x_ref, mod_ref, wout_ref, nfw_ref, wup_ref, cw_ref, cb_ref, wdn_ref, fnw_ref,
         prev_ref, y_ref, fbuf_ref, h_scr, x1_scr, acc_scr) = refs
    rows = nb * lt
    n_chunks = D_FF // FFN_CHUNK

    def col_slices(c):
        return [slice(base + c * FFN_CHUNK, base + (c + 1) * FFN_CHUNK) for base in (0, D_FF)]

    def prologue(sub):
        rr = slice(sub * rows, (sub + 1) * rows)
        attn = (jnp.dot(_bf(odn_ref[rr, :]), wout_ref[0:DN_V, :], preferred_element_type=F32)
                + jnp.dot(_bf(oswa_ref[rr, :]), wout_ref[DN_V:, :], preferred_element_type=F32))
        x = x_ref[:, sub * lt:(sub + 1) * lt, :].reshape(rows, D_MODEL)
        x1 = x + _rows(mod_ref[:, 2:3, :], nb, lt) * attn
        x1_scr[sub] = x1
        h = (_rms(x1, nfw_ref[...]) * (1.0 + _rows(mod_ref[:, 4:5, :], nb, lt))
             + _rows(mod_ref[:, 3:4, :], nb, lt))
        h_scr[sub] = _bf(h)
        acc_scr[sub] = jnp.zeros((rows, D_MODEL), F32)

    def up_proj(sub, c):
        return [jnp.dot(h_scr[sub], wup_ref[:, cols], preferred_element_type=F32) for cols in col_slices(c)]

    def chunk(sub, c, u_cur):
        halves = []
        for u, cols in zip(u_cur, col_slices(c)):
            prev = prev_ref[:, :, cols]
            halves.append(_causal_conv(u, prev, cw_ref[:, cols], nb, lt) + cb_ref[:, cols])
            fbuf_ref[:, :, cols] = _last_rows(u, nb, lt, FFN_CONV - 1)
        act = _silu(halves[0]) * halves[1]
        acc_scr[sub] += jnp.dot(_bf(act), wdn_ref[c * FFN_CHUNK:(c + 1) * FFN_CHUNK, :],
                                preferred_element_type=F32)

    def epilogue(sub):
        x2 = x1_scr[sub] + _rows(mod_ref[:, 5:6, :], nb, lt) * acc_scr[sub]
        y_ref[:, sub * lt:(sub + 1) * lt, :] = _rms(x2, fnw_ref[...]).reshape(nb, lt, D_MODEL)

    prologue(0)
    for sub in range(n_sub):
        u_next = up_proj(sub, 0)
        for c in range(n_chunks):
            u_cur = u_next
            if c + 1 < n_chunks:
                u_next = up_proj(sub, c + 1)
            chunk(sub, c, u_cur)
            if sub + 1 < n_sub and c == n_chunks - 1 - POST_LOOKAHEAD_CHUNKS:
                prologue(sub + 1)
        epilogue(sub)


def _post_call(o_dn, o_swa, x, mod3, w_out, norm_ffn_w, w_up, conv_w, conv_b, w_down, final_w, nb, lt, state=None):
    carry = state is None
    n_sub = POST_SUB_TILES if carry else 1
    bsz, seq, _ = x.shape
    nt = seq // (lt * n_sub)
    rows = nb * lt
    tile = rows * n_sub
    n_state = _state_rows(nb, lt, FFN_CONV - 1)
    row_map = lambda i, j: (i * nt + j, 0)
    const = lambda i, j: (0, 0)
    in_specs = [pl.BlockSpec((tile, DN_V), row_map),
                pl.BlockSpec((tile, SWA_Q), row_map),
                pl.BlockSpec((nb, lt * n_sub, D_MODEL), lambda i, j: (i, j, 0)),
                pl.BlockSpec((nb, 6, D_MODEL), lambda i, j: (i, 0, 0)),
                pl.BlockSpec((D_MODEL, D_MODEL), const, pipeline_mode=RESIDENT),
                pl.BlockSpec((1, D_MODEL), const),
                pl.BlockSpec((D_MODEL, 2 * D_FF), const, pipeline_mode=RESIDENT),
                pl.BlockSpec((FFN_CONV, 2 * D_FF), const),
                pl.BlockSpec((1, 2 * D_FF), const),
                pl.BlockSpec((D_FF, D_MODEL), const, pipeline_mode=RESIDENT),
                pl.BlockSpec((1, D_MODEL), const)]
    args = [o_dn, o_swa, x, mod3, w_out, norm_ffn_w, w_up, conv_w, conv_b, w_down, final_w]
    if not carry:
        in_specs.append(pl.BlockSpec((nb, FFN_CONV - 1, 2 * D_FF), lambda i, j: (i, 0, 0)))
        args.append(state)
    return pl.pallas_call(
        functools.partial(_post_kernel, nb=nb, lt=lt, carry=carry, n_sub=n_sub),
        grid=(bsz // nb, nt),
        in_specs=in_specs,
        out_specs=[pl.BlockSpec((nb, lt * n_sub, D_MODEL), lambda i, j: (i, j, 0)),
                   pl.BlockSpec((nb, n_state, 2 * D_FF), lambda i, j: (i, 0, 0))],
        out_shape=[jax.ShapeDtypeStruct((bsz, seq, D_MODEL), F32),
                   jax.ShapeDtypeStruct((bsz, n_state, 2 * D_FF), F32)],
        scratch_shapes=[pltpu.VMEM((n_sub, rows, D_MODEL), BF16),
                        pltpu.VMEM((n_sub, rows, D_MODEL), F32),
                        pltpu.VMEM((n_sub, rows, D_MODEL), F32)],
        compiler_params=_cparams(("arbitrary", "arbitrary")),
        name="out_proj_convffn_carry" if carry else "out_proj_convffn_state",
    )(*args)


def _pad_row(vec, offset):
    return jnp.zeros((1, BA_PAD), F32).at[0, offset:offset + vec.shape[0]].set(vec)


def kernel(x_prompt, x_sample, state_dn_conv, state_dn_ssm, cache_swa_k, cache_swa_v, state_ffn_conv, c_prompt, c_sample, rel_bias, final_norm_w, w_ada, b_ada, norm_mix_w, w_in, dn_conv_w, dn_A_log, dn_dt_bias, dn_norm_w, swa_sinks, w_out, norm_ffn_w, ffn_w_up, ffn_conv_w, ffn_conv_b, ffn_w_down):
    bp, lp, _ = x_prompt.shape
    bs, ls, _ = x_sample.shape
    layer = 0

    w_in_t = jnp.transpose(w_in[layer])
    w_out_b = w_out[layer].astype(BF16)
    w_up_b = ffn_w_up[layer].astype(BF16)
    w_dn_b = ffn_w_down[layer].astype(BF16)
    alog_row = _pad_row(dn_A_log[layer], DN_HEADS)
    dt_row = _pad_row(dn_dt_bias[layer], DN_HEADS)
    row = lambda v: v.reshape(1, -1)

    n_c = bp + bs
    n_c_pad = -(-n_c // SUBLANES) * SUBLANES
    c_all = jnp.pad(jnp.concatenate([c_sample, c_prompt], axis=0), ((0, n_c_pad - n_c), (0, 0)))
    mod_s, mod_p = _mod_call(c_all, w_ada[layer], row(b_ada[layer]), bs, bp)

    bias_t, bias_s = _bias_call(rel_bias)
    sinks = swa_sinks[layer]

    def mixer_in(x, mod3, nb, lt, state=None):
        return _in_call(x, mod3, row(norm_mix_w[layer]), w_in_t, dn_conv_w[layer], nb, lt, state)

    def post(o_dn, o_swa, x, mod3, nb, lt, state=None):
        return _post_call(o_dn, o_swa, x, mod3, w_out_b, row(norm_ffn_w[layer]), w_up_b, ffn_conv_w[layer],
                          row(ffn_conv_b[layer]), w_dn_b, row(final_norm_w), nb, lt, state)

    dn_args = (alog_row, dt_row, row(dn_norm_w[layer]))

    sq, sk, sv, p_tail, o_dn, p_ssm = _mix_call(x_prompt, mod_p, row(norm_mix_w[layer]), w_in_t,
                                                dn_conv_w[layer], *dn_args)
    o_swa = _swa_prompt_call(sinks, sq, sk, sv, bias_t, bp, lp)
    y_prompt, p_ffn_tail = post(o_dn, o_swa, x_prompt, mod_p, 1, 256)
    p_dn_conv = p_tail[:, SUBLANES - (DN_CONV - 1):]
    p_fbuf = p_ffn_tail[:, SUBLANES - (FFN_CONV - 1):]
    last_window = lambda t: t.reshape(bp, lp, SWA_KV)[:, lp - WINDOW:].reshape(bp, WINDOW, SWA_KV_HEADS, SWA_HD)
    p_swa_k = last_window(sk)
    p_swa_v = last_window(sv)

    qkv_s, z_s, ba_s, sq_s, sk_s, sv_s, s_dn_conv = mixer_in(x_sample, mod_s, 64, ls, state=state_dn_conv[layer])
    o_dn_s, s_ssm = _dn_call(qkv_s, z_s, ba_s, *dn_args, bs, ls, state=state_dn_ssm[layer])
    o_swa_s, s_k, s_v = _swa_sample_call(sinks, sq_s, sk_s, sv_s,
                                         cache_swa_k[layer].reshape(bs, WINDOW, SWA_KV),
                                         cache_swa_v[layer].reshape(bs, WINDOW, SWA_KV), bias_s, bs, ls)
    y_sample, s_fbuf = post(o_dn_s, o_swa_s, x_sample, mod_s, 32, ls, state=state_ffn_conv[layer])

    return (y_prompt, y_sample, p_dn_conv[None], s_dn_conv[None], p_ssm[None], s_ssm[None],
            p_swa_k[None], s_k.reshape(bs, WINDOW, SWA_KV_HEADS, SWA_HD)[None],
            p_swa_v[None], s_v.reshape(bs, WINDOW, SWA_KV_HEADS, SWA_HD)[None],
            p_fbuf[None], s_fbuf[None])
```

```python
import functools
import math

import numpy as np
import jax
import jax.numpy as jnp
from jax import lax
from jax.experimental import pallas as pl
from jax.experimental.pallas import tpu as pltpu

F32 = jnp.float32
BF16 = jnp.bfloat16

D_MODEL = 1024
PAST_LEN = 16384
DN_HEADS = 4
DN_DK = 128
DN_DV = 128
DN_CONV = 4
SWA_HEADS = 8
SWA_KV_HEADS = 2
SWA_GROUP = SWA_HEADS // SWA_KV_HEADS
SWA_HD = 64
WINDOW = 128
N_BUCKETS = 32
MAX_DISTANCE = 128
D_FF = 2816
FFN_CONV = 3
EPS = 1e-6
NEG_INF = -1e30

DN_QK = DN_HEADS * DN_DK
DN_V = DN_HEADS * DN_DV
DN_CONV_CH = 2 * DN_QK + DN_V
SWA_Q = SWA_HEADS * SWA_HD
SWA_KV = SWA_KV_HEADS * SWA_HD
BA_PAD = 128
SUBLANES = 8
UNIT = 128
POST_SUB_TILES = 2
POST_LOOKAHEAD_CHUNKS = 3
FFN_CHUNK = 256
ROW_PITCH_PAD = 128
VMEM_LIMIT = 56 * 1024 * 1024
RESIDENT = pl.Buffered(1)


def _cparams(sem):
    return pltpu.CompilerParams(dimension_semantics=sem, vmem_limit_bytes=VMEM_LIMIT)


def _bf(x):
    return x.astype(BF16)


def _dot(a, b):
    return jnp.dot(_bf(a), _bf(b), preferred_element_type=F32)


def _dot_nt(a, b):
    return lax.dot_general(_bf(a), _bf(b), (((1,), (1,)), ((), ())), preferred_element_type=F32)


def _split2(x):
    hi = _bf(x)
    lo = _bf(x - hi.astype(F32))
    return hi, lo


def _dot_x3(a, b):
    ah, al = _split2(a)
    bh, bl = _split2(b)
    d = functools.partial(jnp.dot, preferred_element_type=F32)
    return d(ah, bh) + (d(al, bh) + d(ah, bl))


def _dot_mask(m, x):
    hi = _bf(x)
    r = x - hi.astype(F32)
    mid = _bf(r)
    lo = _bf(r - mid.astype(F32))
    d = functools.partial(jnp.dot, preferred_element_type=F32)
    return d(m, hi) + (d(m, mid) + d(m, lo))


def _sigmoid(x):
    return 1.0 / (1.0 + jnp.exp(-x))


def _silu(x):
    return x * _sigmoid(x)


def _softplus(x):
    return jnp.maximum(x, 0.0) + jnp.log1p(jnp.exp(-jnp.abs(x)))


def _rms(x, w):
    ms = jnp.mean(x * x, axis=-1, keepdims=True)
    return x * lax.rsqrt(ms + EPS) * w


def _l2norm(t):
    return t * lax.rsqrt(jnp.sum(t * t, axis=-1, keepdims=True) + EPS)


def _rows(m3, nb, lt):
    return jnp.broadcast_to(m3, (nb, lt, m3.shape[-1])).reshape(nb * lt, m3.shape[-1])


def _causal_conv(x, prev, w, nb, lt):
    width = w.shape[0]
    rows, ch = x.shape
    if nb == 1 and lt > SUBLANES:
        tiles = jnp.concatenate([prev, x.reshape(lt // SUBLANES, SUBLANES, ch)], axis=0)
        sub = lax.broadcasted_iota(jnp.int32, (1, SUBLANES, 1), 1)
        out = tiles[1:] * w[width - 1:width, :]
        for j in range(1, width):
            rot = pltpu.roll(tiles, j, axis=1)
            out = out + jnp.where(sub >= j, rot[1:], rot[:-1]) * w[width - 1 - j:width - j, :]
        return out.reshape(rows, ch)
    tmod = lax.broadcasted_iota(jnp.int32, (rows, 1), 0) & (lt - 1)
    out = x * w[width - 1:width, :]
    for j in range(1, width):
        sh = pltpu.roll(x, j, axis=0)
        for t in range(j):
            p = width - 1 - j + t
            sh = jnp.where(tmod == t, _rows(prev[:, p:p + 1, :], nb, lt), sh)
        out = out + sh * w[width - 1 - j:width - j, :]
    return out


def _last_rows(x, nb, lt, n_state):
    ch = x.shape[-1]
    if nb == 1 and lt > SUBLANES:
        return x[lt - SUBLANES:].reshape(1, SUBLANES, ch)
    return x.reshape(nb, lt, ch)[:, lt - n_state:, :]


def _state_rows(nb, lt, n_state):
    return SUBLANES if (nb == 1 and lt > SUBLANES) else n_state


MOD_PARTS = 6


def _mod_kernel(c_ref, w_ref, b_ref, os_ref, op_ref, *, n_s, n_p):
    part = pl.program_id(0)
    res = _dot_x3(_silu(c_ref[...]), w_ref[...]) + b_ref[...]
    for k in range(MOD_PARTS):
        @pl.when(part == k)
        def _():
            os_ref[:, k, :] = res[:n_s]
            op_ref[:, k, :] = res[n_s:n_s + n_p]


def _mod_call(c_all, w_ada, b_ada, n_s, n_p):
    rows = c_all.shape[0]
    return pl.pallas_call(
        functools.partial(_mod_kernel, n_s=n_s, n_p=n_p),
        grid=(MOD_PARTS,),
        in_specs=[pl.BlockSpec((rows, D_MODEL), lambda k: (0, 0)),
                  pl.BlockSpec((D_MODEL, D_MODEL), lambda k: (0, k)),
                  pl.BlockSpec((1, D_MODEL), lambda k: (0, k))],
        out_specs=[pl.BlockSpec((n_s, MOD_PARTS, D_MODEL), lambda k: (0, 0, 0)),
                   pl.BlockSpec((n_p, MOD_PARTS, D_MODEL), lambda k: (0, 0, 0))],
        out_shape=[jax.ShapeDtypeStruct((n_s, MOD_PARTS, D_MODEL), F32),
                   jax.ShapeDtypeStruct((n_p, MOD_PARTS, D_MODEL), F32)],
        compiler_params=_cparams(("arbitrary",)),
        name="adaln_mod",
    )(c_all, w_ada, b_ada)


IN_SPLIT = (DN_CONV_CH, DN_V, BA_PAD, SWA_Q, SWA_KV, SWA_KV)


IN_SUB_TILES = 4
IN_LOOKAHEAD_SLABS = 1
W_PREP_ROWS = 256
IN_SLAB = 2 * DN_DK


def _in_kernel(*refs, nb, lt, carry, n_sub):
    if carry:
        (x_ref, mod_ref, nw_ref, w_ref, cw_ref,
         qkv_ref, z_ref, ba_ref, sq_ref, sk_ref, sv_ref, tail_ref, h_scr, wdn_ref, wba_ref, wsw_ref) = refs
        prev_ref = tail_ref

        @pl.when(pl.program_id(1) == 0)
        def _():
            tail_ref[...] = jnp.zeros_like(tail_ref)
    else:
        (x_ref, mod_ref, nw_ref, w_ref, cw_ref, prev_ref,
         qkv_ref, z_ref, ba_ref, sq_ref, sk_ref, sv_ref, tail_ref, h_scr, wdn_ref, wba_ref, wsw_ref) = refs

    @pl.when((pl.program_id(0) == 0) & (pl.program_id(1) == 0))
    def _():
        ba_lo = DN_CONV_CH + DN_V
        n_ba = 2 * DN_HEADS
        for r in range(0, ba_lo, W_PREP_ROWS):
            wdn_ref[:, r:r + W_PREP_ROWS] = _bf(w_ref[r:r + W_PREP_ROWS, :].T)
        row = lax.broadcasted_iota(jnp.int32, (BA_PAD, 1), 0)
        wba_ref[...] = _bf(jnp.where(row < n_ba, w_ref[ba_lo:ba_lo + BA_PAD, :], 0.0).T)
        for r in range(0, SWA_Q + 2 * SWA_KV, W_PREP_ROWS):
            wsw_ref[:, r:r + W_PREP_ROWS] = _bf(w_ref[ba_lo + n_ba + r:ba_lo + n_ba + r + W_PREP_ROWS, :].T)

    rows = nb * lt

    def norm(sub):
        x = x_ref[:, sub * lt:(sub + 1) * lt, :]
        ms = jnp.mean(x * x, axis=-1, keepdims=True)
        y = x * lax.rsqrt(ms + EPS) * nw_ref[...]
        h = y * (1.0 + mod_ref[:, 1:2, :]) + mod_ref[:, 0:1, :]
        h_scr[sub, :, :D_MODEL] = _bf(h.reshape(rows, D_MODEL))

    def proj(sub, w_bf_ref, lo, n):
        return jnp.dot(h_scr[sub, :, :D_MODEL], w_bf_ref[:, lo:lo + n], preferred_element_type=F32)

    def plain_proj(sub, i):
        rr = slice(sub * rows, (sub + 1) * rows)
        if i < 2:
            z_ref[rr, i * IN_SLAB:(i + 1) * IN_SLAB] = proj(sub, wdn_ref, DN_CONV_CH + i * IN_SLAB, IN_SLAB)
        elif i < 4:
            sq_ref[rr, (i - 2) * IN_SLAB:(i - 1) * IN_SLAB] = proj(sub, wsw_ref, (i - 2) * IN_SLAB, IN_SLAB)
        elif i == 4:
            kv = proj(sub, wsw_ref, SWA_Q, 2 * SWA_KV)
            sk_ref[rr, :] = kv[:, :SWA_KV]
            sv_ref[rr, :] = kv[:, SWA_KV:]
        else:
            ba_ref[rr, :] = proj(sub, wba_ref, 0, BA_PAD)

    def conv_slab(sub, slab, raw):
        rr = slice(sub * rows, (sub + 1) * rows)
        cols = slice(slab * IN_SLAB, (slab + 1) * IN_SLAB)
        c = _silu(_causal_conv(raw, prev_ref[:, :, cols], cw_ref[:, cols], nb, lt))
        tail_ref[:, :, cols] = _last_rows(raw, nb, lt, DN_CONV - 1)
        if slab * IN_SLAB < 2 * DN_QK:
            scale = DN_DK ** -0.5 if slab * IN_SLAB < DN_QK else 1.0
            c = jnp.concatenate([_l2norm(c[:, i * DN_DK:(i + 1) * DN_DK]) * scale
                                 for i in range(IN_SLAB // DN_DK)], axis=1)
        qkv_ref[rr, cols] = c

    n_slabs = DN_CONV_CH // IN_SLAB
    norm(0)
    for sub in range(n_sub):
        raw_next = proj(sub, wdn_ref, 0, IN_SLAB)
        for slab in range(n_slabs):
            raw = raw_next
            plain_proj(sub, slab)
            if slab + 1 < n_slabs:
                raw_next = proj(sub, wdn_ref, (slab + 1) * IN_SLAB, IN_SLAB)
            if sub + 1 < n_sub and slab == n_slabs - 1 - IN_LOOKAHEAD_SLABS:
                norm(sub + 1)
            conv_slab(sub, slab, raw)


def _in_call(x, mod3, norm_w, w_in_t, conv_w, nb, lt, state=None):
    carry = state is None
    n_sub = IN_SUB_TILES if carry else 1
    bsz, seq, _ = x.shape
    nt = seq // (lt * n_sub)
    rows = nb * lt
    tile = rows * n_sub
    n_tok = bsz * seq
    n_state = _state_rows(nb, lt, DN_CONV - 1)
    row_map = lambda i, j: (i * nt + j, 0)
    const = lambda i, j: (0, 0)
    per_seq = lambda i, j: (i, 0, 0)
    in_specs = [pl.BlockSpec((nb, lt * n_sub, D_MODEL), lambda i, j: (i, j, 0)),
                pl.BlockSpec((nb, 6, D_MODEL), per_seq),
                pl.BlockSpec((1, D_MODEL), const),
                pl.BlockSpec(w_in_t.shape, const, pipeline_mode=RESIDENT),
                pl.BlockSpec((DN_CONV, DN_CONV_CH), const)]
    args = [x, mod3, norm_w, w_in_t, conv_w]
    if not carry:
        in_specs.append(pl.BlockSpec((nb, DN_CONV - 1, DN_CONV_CH), per_seq))
        args.append(state)
    return pl.pallas_call(
        functools.partial(_in_kernel, nb=nb, lt=lt, carry=carry, n_sub=n_sub),
        grid=(bsz // nb, nt),
        in_specs=in_specs,
        out_specs=[pl.BlockSpec((tile, n), row_map) for n in IN_SPLIT]
        + [pl.BlockSpec((nb, n_state, DN_CONV_CH), per_seq)],
        out_shape=[jax.ShapeDtypeStruct((n_tok, n), F32) for n in IN_SPLIT]
        + [jax.ShapeDtypeStruct((bsz, n_state, DN_CONV_CH), F32)],
        scratch_shapes=[pltpu.VMEM((n_sub, rows, D_MODEL + ROW_PITCH_PAD), BF16),
                        pltpu.VMEM((D_MODEL, DN_CONV_CH + DN_V + ROW_PITCH_PAD), BF16),
                        pltpu.VMEM((D_MODEL, BA_PAD), BF16),
                        pltpu.VMEM((D_MODEL, SWA_Q + 2 * SWA_KV), BF16)],
        compiler_params=_cparams(("arbitrary", "arbitrary")),
        name="norm_in_proj_carry" if carry else "norm_in_proj_state",
    )(*args)


INV_BASE_SHIFT = 2
DN_UNITS_CARRY = 4
SWA_BLOCKS_PER_STEP = 4
DN_UNITS_STATE = 1


def _unit_lower_inverses(a_mats, ri, ci, chunk_shift):
    def blocks(s):
        return (ri >> s) == (ci >> s)

    base = min(INV_BASE_SHIFT, chunk_shift)
    xs = [jnp.where(blocks(base), -a, 0.0) for a in a_mats]
    ts = [jnp.where(ri == ci, 1.0, n) for n in xs]
    for lvl in range(1, base):
        xs = [_dot(x, x) for x in xs]
        ts = [t + _dot(t, x) for t, x in zip(ts, xs)]
    for s in range(base, chunk_shift):
        sel = blocks(s + 1) & jnp.logical_not(blocks(s))
        ets = [_dot(jnp.where(sel, a, 0.0), t) for a, t in zip(a_mats, ts)]
        ts = [t - _dot(t, et) for t, et in zip(ts, ets)]
    return ts


def _dn_kernel(*refs, nb, chunk, carry, units):
    if carry:
        (qkv_ref, z_ref, ba_ref, alog_ref, dt_ref, nw_ref, o_ref, s_ref) = refs
        s0_ref = s_ref

        @pl.when(pl.program_id(1) == 0)
        def _():
            s_ref[...] = jnp.zeros_like(s_ref)
    else:
        (qkv_ref, z_ref, ba_ref, alog_ref, dt_ref, nw_ref, s0_ref, o_ref, s_ref) = refs

    ri = lax.broadcasted_iota(jnp.int32, (UNIT, UNIT), 0)
    ci = lax.broadcasted_iota(jnp.int32, (UNIT, UNIT), 1)
    shift = int(math.log2(chunk))
    same = (ri >> shift) == (ci >> shift)
    incl = same & (ri >= ci)
    strict = same & (ri > ci)

    ba = ba_ref[...]
    beta_full = _sigmoid(ba)
    g_full = -jnp.exp(alog_ref[...]) * _softplus(ba + dt_ref[...])
    masks = jnp.concatenate([jnp.where(incl, 1.0, 0.0), jnp.where(same, 1.0, 0.0)], axis=0).astype(BF16)
    gsums = [_dot_mask(masks, g_full[u * UNIT:(u + 1) * UNIT]) for u in range(units)]
    g_cum = [g[:UNIT] for g in gsums]
    g_tot = [g[UNIT:] for g in gsums]
    g_cum_t = [g.T for g in g_cum]

    probs = [(u, h) for u in range(units) for h in range(DN_HEADS)]
    n_p = len(probs)
    rs = lambda u: slice(u * UNIT, (u + 1) * UNIT)

    def head_cols(base):
        return [qkv_ref[rs(u), base + h * DN_DK:base + (h + 1) * DN_DK] for u, h in probs]

    q = head_cols(0)
    k = head_cols(DN_QK)
    v = head_cols(2 * DN_QK)
    gc = [g_cum[u][:, DN_HEADS + h:DN_HEADS + h + 1] for u, h in probs]
    gr = [g_cum_t[u][DN_HEADS + h:DN_HEADS + h + 1, :] for u, h in probs]
    gt = [g_tot[u][:, DN_HEADS + h:DN_HEADS + h + 1] for u, h in probs]
    bc = [beta_full[rs(u), h:h + 1] for u, h in probs]
    decay = [jnp.where(incl, jnp.exp(jnp.where(incl, gc[p] - gr[p], 0.0)), 0.0) for p in range(n_p)]
    e_g = [jnp.exp(gc[p]) for p in range(n_p)]
    kq = [_dot_nt(jnp.concatenate([k[p], q[p]], axis=0), k[p]) for p in range(n_p)]
    qk = [kq[p][UNIT:] * decay[p] for p in range(n_p)]
    a_mats = [jnp.where(strict, bc[p] * kq[p][:UNIT] * decay[p], 0.0) for p in range(n_p)]
    t_inv = _unit_lower_inverses(a_mats, ri, ci, shift)
    wvk = [_dot(t_inv[p], jnp.concatenate([v[p] * bc[p], k[p] * (bc[p] * e_g[p])], axis=1)) for p in range(n_p)]
    w_v = [w[:, :DN_DV] for w in wvk]
    w_k = [w[:, DN_DV:] for w in wvk]
    q_dec = [q[p] * e_g[p] for p in range(n_p)]
    k_tail_t = [(k[p] * jnp.exp(gt[p] - gc[p])).T for p in range(n_p)]
    c_dec = [jnp.exp(gt[p]) for p in range(n_p)]

    outs = {}
    if carry:
        state = [s_ref[0, h] for h in range(DN_HEADS)]
        for u in range(units):
            ps = [u * DN_HEADS + h for h in range(DN_HEADS)]
            r = [_dot(jnp.concatenate([w_k[p], q_dec[p]], axis=0), state[h]) for h, p in enumerate(ps)]
            uu = [w_v[p] - r[h][:UNIT] for h, p in enumerate(ps)]
            for h, p in enumerate(ps):
                outs[p] = r[h][UNIT:] + _dot(qk[p], uu[h])
            state = [state[h] * c_dec[p][0:1, :] + _dot(k_tail_t[p], uu[h]) for h, p in enumerate(ps)]
        for h in range(DN_HEADS):
            s_ref[0, h] = state[h]
    else:
        per_unit = nb // units
        for p, (u, h) in enumerate(probs):
            us, qs = [], []
            for s in range(per_unit):
                lo = s * chunk
                lhs = jnp.concatenate([w_k[p][lo:lo + chunk], q_dec[p][lo:lo + chunk]], axis=0)
                r = jnp.dot(lhs, s0_ref[u * per_unit + s, h], preferred_element_type=F32)
                us.append(w_v[p][lo:lo + chunk] - r[:chunk])
                qs.append(r[chunk:])
            uu = jnp.concatenate(us, axis=0)
            outs[p] = jnp.concatenate(qs, axis=0) + _dot(qk[p], uu)
            for s in range(per_unit):
                lo = s * chunk
                upd = jnp.dot(k_tail_t[p][:, lo:lo + chunk], uu[lo:lo + chunk], preferred_element_type=F32)
                s_ref[u * per_unit + s, h] = s0_ref[u * per_unit + s, h] * c_dec[p][lo:lo + 1, :] + upd

    for p, (u, h) in enumerate(probs):
        zz = z_ref[rs(u), h * DN_DV:(h + 1) * DN_DV]
        o_ref[rs(u), h * DN_DV:(h + 1) * DN_DV] = _rms(outs[p], nw_ref[...]) * _silu(zz)


def _dn_call(qkv, z, ba, alog_row, dt_row, norm_w, bsz, seq, state=None):
    carry = state is None
    units = DN_UNITS_CARRY if carry else DN_UNITS_STATE
    rows = units * UNIT
    nb = 1 if carry else rows // seq
    chunk = UNIT if carry else seq
    nt = seq * nb // rows
    n_tok = bsz * seq
    row_map = lambda i, j: (i * nt + j, 0)
    const = lambda i, j: (0, 0)
    in_specs = [pl.BlockSpec((rows, DN_CONV_CH), row_map),
                pl.BlockSpec((rows, DN_V), row_map),
                pl.BlockSpec((rows, BA_PAD), row_map),
                pl.BlockSpec((1, BA_PAD), const),
                pl.BlockSpec((1, BA_PAD), const),
                pl.BlockSpec((1, DN_DV), const)]
    args = [qkv, z, ba, alog_row, dt_row, norm_w]
    if not carry:
        in_specs.append(pl.BlockSpec((nb, DN_HEADS, DN_DK, DN_DV), lambda i, j: (i, 0, 0, 0)))
        args.append(state)
    return pl.pallas_call(
        functools.partial(_dn_kernel, nb=nb, chunk=chunk, carry=carry, units=units),
        grid=(bsz // nb, nt),
        in_specs=in_specs,
        out_specs=[pl.BlockSpec((rows, DN_V), row_map),
                   pl.BlockSpec((nb, DN_HEADS, DN_DK, DN_DV), lambda i, j: (i, 0, 0, 0))],
        out_shape=[jax.ShapeDtypeStruct((n_tok, DN_V), F32),
                   jax.ShapeDtypeStruct((bsz, DN_HEADS, DN_DK, DN_DV), F32)],
        compiler_params=_cparams(("arbitrary", "arbitrary")),
        name="gated_deltanet_carry" if carry else "gated_deltanet_state",
    )(*args)


MIX_SUB_TILES = 2
MIX_ROWS = MIX_SUB_TILES * 2 * UNIT


def _mix_kernel(x_ref, mod_ref, nw_ref, w_ref, cw_ref, alog_ref, dt_ref, dnw_ref,
                sq_ref, sk_ref, sv_ref, tail_ref, o_ref, s_ref,
                h_scr, wdn_ref, wba_ref, wsw_ref, qkv_scr, z_scr, ba_scr):
    _in_kernel(x_ref, mod_ref, nw_ref, w_ref, cw_ref,
               qkv_scr, z_scr, ba_scr, sq_ref, sk_ref, sv_ref, tail_ref, h_scr, wdn_ref, wba_ref, wsw_ref,
               nb=1, lt=2 * UNIT, carry=True, n_sub=MIX_SUB_TILES)
    _dn_kernel(qkv_scr, z_scr, ba_scr, alog_ref, dt_ref, dnw_ref, o_ref, s_ref,
               nb=1, chunk=UNIT, carry=True, units=MIX_ROWS // UNIT)


def _mix_call(x, mod3, norm_w, w_in_t, conv_w, alog_row, dt_row, dn_norm_w):
    bsz, seq, _ = x.shape
    nt = seq // MIX_ROWS
    n_tok = bsz * seq
    row_map = lambda i, j: (i * nt + j, 0)
    const = lambda i, j: (0, 0)
    per_seq = lambda i, j: (i, 0, 0)
    swa_cols = (SWA_Q, SWA_KV, SWA_KV)
    return pl.pallas_call(
        _mix_kernel,
        grid=(bsz, nt),
        in_specs=[pl.BlockSpec((1, MIX_ROWS, D_MODEL), lambda i, j: (i, j, 0)),
                  pl.BlockSpec((1, 6, D_MODEL), per_seq),
                  pl.BlockSpec((1, D_MODEL), const),
                  pl.BlockSpec(w_in_t.shape, const, pipeline_mode=RESIDENT),
                  pl.BlockSpec((DN_CONV, DN_CONV_CH), const),
                  pl.BlockSpec((1, BA_PAD), const),
                  pl.BlockSpec((1, BA_PAD), const),
                  pl.BlockSpec((1, DN_DV), const)],
        out_specs=[pl.BlockSpec((MIX_ROWS, n), row_map) for n in swa_cols]
        + [pl.BlockSpec((1, SUBLANES, DN_CONV_CH), per_seq),
           pl.BlockSpec((MIX_ROWS, DN_V), row_map),
           pl.BlockSpec((1, DN_HEADS, DN_DK, DN_DV), lambda i, j: (i, 0, 0, 0))],
        out_shape=[jax.ShapeDtypeStruct((n_tok, n), F32) for n in swa_cols]
        + [jax.ShapeDtypeStruct((bsz, SUBLANES, DN_CONV_CH), F32),
           jax.ShapeDtypeStruct((n_tok, DN_V), F32),
           jax.ShapeDtypeStruct((bsz, DN_HEADS, DN_DK, DN_DV), F32)],
        scratch_shapes=[pltpu.VMEM((MIX_SUB_TILES, 2 * UNIT, D_MODEL + ROW_PITCH_PAD), BF16),
                        pltpu.VMEM((D_MODEL, DN_CONV_CH + DN_V + ROW_PITCH_PAD), BF16),
                        pltpu.VMEM((D_MODEL, BA_PAD), BF16),
                        pltpu.VMEM((D_MODEL, SWA_Q + 2 * SWA_KV), BF16),
                        pltpu.VMEM((MIX_ROWS, DN_CONV_CH), F32),
                        pltpu.VMEM((MIX_ROWS, DN_V + ROW_PITCH_PAD), F32),
                        pltpu.VMEM((MIX_ROWS, BA_PAD), F32)],
        compiler_params=_cparams(("arbitrary", "arbitrary")),
        name="in_proj_deltanet_carry",
    )(x, mod3, norm_w, w_in_t, conv_w, alog_row, dt_row, dn_norm_w)


def _bucket_table():
    i = np.arange(WINDOW, dtype=np.int64)[:, None]
    j = np.arange(2 * WINDOW, dtype=np.int64)[None, :]
    d = np.maximum(i + WINDOW - j, 0)
    exact = N_BUCKETS // 2
    logv = (np.log(np.maximum(d, 1).astype(np.float32) / np.float32(exact)).astype(np.float32)
            / np.float32(math.log(MAX_DISTANCE / exact)))
    large = np.minimum(exact + (logv * np.float32(N_BUCKETS - exact)).astype(np.int32), N_BUCKETS - 1)
    return np.where(d < exact, d, large).astype(np.int32)


def _bias_lookup(rb_ref, bucket, head):
    acc = jnp.zeros(bucket.shape, F32)
    for b in range(N_BUCKETS):
        acc = jnp.where(bucket == b, rb_ref[b, head], acc)
    return acc


def _bias_kernel(rb_ref, bucket_t_ref, bucket_ref, ot_ref, os_ref):
    bucket_t = bucket_t_ref[...]
    kj = lax.broadcasted_iota(jnp.int32, bucket_t.shape, 0)
    qi = lax.broadcasted_iota(jnp.int32, bucket_t.shape, 1)
    dist = qi + WINDOW - kj
    valid = (dist >= 0) & (dist < WINDOW)
    for hk in range(SWA_KV_HEADS):
        for par in range(2):
            for st in range(SWA_GROUP // 2):
                head = hk * SWA_GROUP + 2 * st + par
                gen = jnp.where(valid, _bias_lookup(rb_ref, bucket_t, head), NEG_INF)
                ot_ref[1, hk, par, :, st * WINDOW:(st + 1) * WINDOW] = gen
                ot_ref[0, hk, par, :, st * WINDOW:(st + 1) * WINDOW] = jnp.where(kj >= WINDOW, gen, NEG_INF)
    bucket = bucket_ref[...]
    qi = lax.broadcasted_iota(jnp.int32, bucket.shape, 0)
    kj = lax.broadcasted_iota(jnp.int32, bucket.shape, 1)
    dist = qi + WINDOW - kj
    valid = (dist >= 0) & (dist < WINDOW)
    for head in range(SWA_HEADS):
        os_ref[head] = jnp.where(valid, _bias_lookup(rb_ref, bucket, head), NEG_INF)


BIAS_T_SHAPE = (2, SWA_KV_HEADS, 2, 2 * WINDOW, (SWA_GROUP // 2) * WINDOW)
BIAS_S_SHAPE = (SWA_HEADS, SUBLANES, 2 * WINDOW)


def _bias_call(rel_bias):
    bucket = _bucket_table()
    return pl.pallas_call(
        _bias_kernel,
        in_specs=[pl.BlockSpec(memory_space=pltpu.SMEM),
                  pl.BlockSpec((2 * WINDOW, WINDOW), lambda: (0, 0)),
                  pl.BlockSpec((SUBLANES, 2 * WINDOW), lambda: (0, 0))],
        out_specs=[pl.BlockSpec(BIAS_T_SHAPE, lambda: (0,) * len(BIAS_T_SHAPE)),
                   pl.BlockSpec(BIAS_S_SHAPE, lambda: (0,) * len(BIAS_S_SHAPE))],
        out_shape=[jax.ShapeDtypeStruct(BIAS_T_SHAPE, F32), jax.ShapeDtypeStruct(BIAS_S_SHAPE, F32)],
        name="swa_rel_bias_table",
    )(rel_bias, jnp.asarray(np.ascontiguousarray(bucket.T)), jnp.asarray(bucket[:SUBLANES]))


def _softmax_sink_parts(s, sink):
    m = jnp.maximum(jnp.max(s, axis=0, keepdims=True), sink)
    p = jnp.exp(s - m)
    return p, 1.0 / (jnp.sum(p, axis=0, keepdims=True) + jnp.exp(sink - m))


def _half_lane_variants(full, hk, lo_half):
    rolled = pltpu.roll(full, SWA_HD, axis=1)
    low_src, high_src = (full, rolled) if hk == 0 else (rolled, full)
    return jnp.where(lo_half, low_src, 0.0), jnp.where(lo_half, 0.0, high_src)


def _swa_prompt_kernel(sink_ref, q_ref, kp_ref, kc_ref, vp_ref, vc_ref, bias_ref, o_ref, *, n_blk):
    step = pl.program_id(1)
    lo_half = lax.broadcasted_iota(jnp.int32, (1, 2 * SWA_HD), 1) < SWA_HD
    lo_rows = lax.broadcasted_iota(jnp.int32, (2 * SWA_HD, 1), 0) < SWA_HD
    q = _bf(q_ref[...] * (SWA_HD ** -0.5))
    keys = jnp.concatenate([kp_ref[...], kc_ref[...]], axis=0)
    vals = jnp.concatenate([vp_ref[...], vc_ref[...]], axis=0)
    k_var = [[_bf(t) for t in _half_lane_variants(keys, hk, lo_half)] for hk in range(SWA_KV_HEADS)]
    v_var_t = [[_bf(t.T) for t in _half_lane_variants(vals, hk, lo_half)] for hk in range(SWA_KV_HEADS)]
    n_stack = SWA_GROUP // 2
    sinks = [[jnp.concatenate([jnp.full((1, WINDOW), sink_ref[hk * SWA_GROUP + 2 * st + par], F32)
                               for st in range(n_stack)], axis=1) for par in range(2)]
             for hk in range(SWA_KV_HEADS)]
    nt_dims = (((1,), (1,)), ((), ()))

    def scores(b):
        rows = slice(b * WINDOW, (b + 1) * WINDOW)
        win = slice(b * WINDOW, (b + 2) * WINDOW)
        variant = jnp.where(step == 0, 0, 1) if b == 0 else 1
        out = []
        for hk in range(SWA_KV_HEADS):
            q2 = jnp.concatenate([q[rows, (hk * n_stack + st) * 2 * SWA_HD:(hk * n_stack + st + 1) * 2 * SWA_HD]
                                  for st in range(n_stack)], axis=0)
            out.append([lax.dot_general(k_var[hk][par][win], q2, nt_dims, preferred_element_type=F32)
                        + bias_ref[variant, hk, par] for par in range(2)])
        return out

    s_next = scores(0)
    for b in range(n_blk):
        s_cur = s_next
        if b + 1 < n_blk:
            s_next = scores(b + 1)
        rows = slice(b * WINDOW, (b + 1) * WINDOW)
        win = slice(b * WINDOW, (b + 2) * WINDOW)
        for hk in range(SWA_KV_HEADS):
            parts = [_softmax_sink_parts(s_cur[hk][par], sinks[hk][par]) for par in range(2)]
            o_t = (jnp.dot(v_var_t[hk][0][:, win], _bf(parts[0][0]), preferred_element_type=F32)
                   + jnp.dot(v_var_t[hk][1][:, win], _bf(parts[1][0]), preferred_element_type=F32))
            o_t = o_t * jnp.where(lo_rows, parts[0][1], parts[1][1])
            for st in range(n_stack):
                lo = (hk * n_stack + st) * 2 * SWA_HD
                o_ref[rows, lo:lo + 2 * SWA_HD] = o_t[:, st * WINDOW:(st + 1) * WINDOW].T


def _swa_prompt_call(sinks, sq, sk, sv, bias, bsz, seq):
    n_blk = SWA_BLOCKS_PER_STEP
    tile = n_blk * WINDOW
    nt = seq // tile
    cur = lambda b, i: (b * nt + i, 0)
    prv = lambda b, i: (b * nt * n_blk + jnp.maximum(i * n_blk - 1, 0), 0)
    return pl.pallas_call(
        functools.partial(_swa_prompt_kernel, n_blk=n_blk),
        grid=(bsz, nt),
        in_specs=[pl.BlockSpec(memory_space=pltpu.SMEM),
                  pl.BlockSpec((tile, SWA_Q), cur),
                  pl.BlockSpec((WINDOW, SWA_KV), prv),
                  pl.BlockSpec((tile, SWA_KV), cur),
                  pl.BlockSpec((WINDOW, SWA_KV), prv),
                  pl.BlockSpec((tile, SWA_KV), cur),
                  pl.BlockSpec(BIAS_T_SHAPE, lambda b, i: (0,) * len(BIAS_T_SHAPE))],
        out_specs=pl.BlockSpec((tile, SWA_Q), cur),
        out_shape=jax.ShapeDtypeStruct((bsz * seq, SWA_Q), F32),
        compiler_params=_cparams(("arbitrary", "arbitrary")),
        name="swa_banded",
    )(sinks, sq, sk, sk, sv, sv, bias)


def _swa_sample_kernel(sink_ref, q_ref, kn_ref, vn_ref, kc_ref, vc_ref, bias_ref, o_ref, ko_ref, vo_ref, *, nb, lt):
    lo_half = lax.broadcasted_iota(jnp.int32, (1, 2 * SWA_HD), 1) < SWA_HD
    q_all = q_ref[...] * (SWA_HD ** -0.5)
    kn_all = kn_ref[...]
    vn_all = vn_ref[...]
    pad = jnp.zeros((WINDOW - lt, SWA_KV), F32)
    bias = jnp.concatenate([bias_ref[h, 0:lt, :] for h in range(SWA_HEADS)], axis=0)
    sink = jnp.concatenate([jnp.full((lt, 1), sink_ref[h], F32) for h in range(SWA_HEADS)], axis=0)
    nt_dims = (((1,), (1,)), ((), ()))

    def to_half(tile, src_half, dst_half):
        return tile if src_half == dst_half else pltpu.roll(tile, SWA_HD, axis=1)

    lhs, keys, vals = [], [], []
    for s in range(nb):
        rows = slice(s * lt, (s + 1) * lt)
        kc, vc, kn, vn = kc_ref[s], vc_ref[s], kn_all[rows], vn_all[rows]
        ko_ref[s] = jnp.concatenate([kc[lt:], kn], axis=0)
        vo_ref[s] = jnp.concatenate([vc[lt:], vn], axis=0)
        keys.append(_bf(jnp.concatenate([kc, kn, pad], axis=0)))
        vals.append(_bf(jnp.concatenate([vc, vn, pad], axis=0)))
        q = q_all[rows]
        tiles = []
        for h in range(SWA_HEADS):
            hk = h // SWA_GROUP
            t = to_half(q[:, (h // 2) * 2 * SWA_HD:(h // 2 + 1) * 2 * SWA_HD], h % 2, hk)
            tiles.append(jnp.where(lo_half if hk == 0 else jnp.logical_not(lo_half), t, 0.0))
        lhs.append(_bf(jnp.concatenate(tiles, axis=0)))
    scores = [lax.dot_general(lhs[s], keys[s], nt_dims, preferred_element_type=F32) + bias for s in range(nb)]
    m = [jnp.maximum(jnp.max(sc, axis=-1, keepdims=True), sink) for sc in scores]
    p = [jnp.exp(sc - mm) for sc, mm in zip(scores, m)]
    rinv = [1.0 / (jnp.sum(pp, axis=-1, keepdims=True) + jnp.exp(sink - mm)) for pp, mm in zip(p, m)]
    res = [jnp.dot(_bf(p[s]), vals[s], preferred_element_type=F32) * rinv[s] for s in range(nb)]
    rows_out = []
    for s in range(nb):
        tiles = []
        for pair in range(SWA_HEADS // 2):
            hk = (2 * pair) // SWA_GROUP
            low = to_half(res[s][(2 * pair) * lt:(2 * pair + 1) * lt], hk, 0)
            high = to_half(res[s][(2 * pair + 1) * lt:(2 * pair + 2) * lt], hk, 1)
            tiles.append(jnp.where(lo_half, low, high))
        rows_out.append(jnp.concatenate(tiles, axis=1))
    o_ref[...] = jnp.concatenate(rows_out, axis=0)


def _swa_sample_call(sinks, sq, sk, sv, cache_k, cache_v, bias, bsz, seq):
    nb = UNIT // seq
    rows = lambda i: (i, 0)
    seqs = lambda i: (i, 0, 0)
    return pl.pallas_call(
        functools.partial(_swa_sample_kernel, nb=nb, lt=seq),
        grid=(bsz // nb,),
        in_specs=[pl.BlockSpec(memory_space=pltpu.SMEM),
                  pl.BlockSpec((UNIT, SWA_Q), rows),
                  pl.BlockSpec((UNIT, SWA_KV), rows),
                  pl.BlockSpec((UNIT, SWA_KV), rows),
                  pl.BlockSpec((nb, WINDOW, SWA_KV), seqs),
                  pl.BlockSpec((nb, WINDOW, SWA_KV), seqs),
                  pl.BlockSpec(BIAS_S_SHAPE, lambda i: (0,) * len(BIAS_S_SHAPE))],
        out_specs=[pl.BlockSpec((UNIT, SWA_Q), rows),
                   pl.BlockSpec((nb, WINDOW, SWA_KV), seqs),
                   pl.BlockSpec((nb, WINDOW, SWA_KV), seqs)],
        out_shape=[jax.ShapeDtypeStruct((bsz * seq, SWA_Q), F32),
                   jax.ShapeDtypeStruct((bsz, WINDOW, SWA_KV), F32),
                   jax.ShapeDtypeStruct((bsz, WINDOW, SWA_KV), F32)],
        compiler_params=_cparams(("arbitrary",)),
        name="swa_cached",
    )(sinks, sq, sk, sv, cache_k, cache_v, bias)


def _post_kernel(*refs, nb, lt, carry, n_sub):
    if carry:
        (odn_ref, oswa_ref, x_ref, mod_ref, wout_ref, nfw_ref, wup_ref, cw_ref, cb_ref, wdn_ref, fnw_ref,
         y_ref, fbuf_ref, h_scr, x1_scr, acc_scr) = refs
        prev_ref = fbuf_ref

        @pl.when(pl.program_id(1) == 0)
        def _():
            fbuf_ref[...] = jnp.zeros_like(fbuf_ref)
    else:
        (odn_ref, oswa_ref, x_ref, mod_ref, wout_ref, nfw_ref, wup_ref, cw_ref, cb_ref, wdn_ref, fnw_ref,
         prev_ref, y_ref, fbuf_ref, h_scr, x1_scr, acc_scr) = refs
    rows = nb * lt
    n_chunks = D_FF // FFN_CHUNK

    def col_slices(c):
        return [slice(base + c * FFN_CHUNK, base + (c + 1) * FFN_CHUNK) for base in (0, D_FF)]

    def prologue(sub):
        rr = slice(sub * rows, (sub + 1) * rows)
        attn = (jnp.dot(_bf(odn_ref[rr, :]), wout_ref[0:DN_V, :], preferred_element_type=F32)
                + jnp.dot(_bf(oswa_ref[rr, :]), wout_ref[DN_V:, :], preferred_element_type=F32))
        x = x_ref[:, sub * lt:(sub + 1) * lt, :].reshape(rows, D_MODEL)
        x1 = x + _rows(mod_ref[:, 2:3, :], nb, lt) * attn
        x1_scr[sub] = x1
        h = (_rms(x1, nfw_ref[...]) * (1.0 + _rows(mod_ref[:, 4:5, :], nb, lt))
             + _rows(mod_ref[:, 3:4, :], nb, lt))
        h_scr[sub] = _bf(h)
        acc_scr[sub] = jnp.zeros((rows, D_MODEL), F32)

    def up_proj(sub, c):
        return [jnp.dot(h_scr[sub], wup_ref[:, cols], preferred_element_type=F32) for cols in col_slices(c)]

    def chunk(sub, c, u_cur):
        halves = []
        for u, cols in zip(u_cur, col_slices(c)):
            prev = prev_ref[:, :, cols]
            halves.append(_causal_conv(u, prev, cw_ref[:, cols], nb, lt) + cb_ref[:, cols])
            fbuf_ref[:, :, cols] = _last_rows(u, nb, lt, FFN_CONV - 1)
        act = _silu(halves[0]) * halves[1]
        acc_scr[sub] += jnp.dot(_bf(act), wdn_ref[c * FFN_CHUNK:(c + 1) * FFN_CHUNK, :],
                                preferred_element_type=F32)

    def epilogue(sub):
        x2 = x1_scr[sub] + _rows(mod_ref[:, 5:6, :], nb, lt) * acc_scr[sub]
        y_ref[:, sub * lt:(sub + 1) * lt, :] = _rms(x2, fnw_ref[...]).reshape(nb, lt, D_MODEL)

    prologue(0)
    for sub in range(n_sub):
        u_next = up_proj(sub, 0)
        for c in range(n_chunks):
            u_cur = u_next
            if c + 1 < n_chunks:
                u_next = up_proj(sub, c + 1)
            chunk(sub, c, u_cur)
            if sub + 1 < n_sub and c == n_chunks - 1 - POST_LOOKAHEAD_CHUNKS:
                prologue(sub + 1)
        epilogue(sub)


def _post_call(o_dn, o_swa, x, mod3, w_out, norm_ffn_w, w_up, conv_w, conv_b, w_down, final_w, nb, lt, state=None):
    carry = state is None
    n_sub = POST_SUB_TILES if carry else 1
    bsz, seq, _ = x.shape
    nt = seq // (lt * n_sub)
    rows = nb * lt
    tile = rows * n_sub
    n_state = _state_rows(nb, lt, FFN_CONV - 1)
    row_map = lambda i, j: (i * nt + j, 0)
    const = lambda i, j: (0, 0)
    in_specs = [pl.BlockSpec((tile, DN_V), row_map),
                pl.BlockSpec((tile, SWA_Q), row_map),
                pl.BlockSpec((nb, lt * n_sub, D_MODEL), lambda i, j: (i, j, 0)),
                pl.BlockSpec((nb, 6, D_MODEL), lambda i, j: (i, 0, 0)),
                pl.BlockSpec((D_MODEL, D_MODEL), const, pipeline_mode=RESIDENT),
                pl.BlockSpec((1, D_MODEL), const),
                pl.BlockSpec((D_MODEL, 2 * D_FF), const, pipeline_mode=RESIDENT),
                pl.BlockSpec((FFN_CONV, 2 * D_FF), const),
                pl.BlockSpec((1, 2 * D_FF), const),
                pl.BlockSpec((D_FF, D_MODEL), const, pipeline_mode=RESIDENT),
                pl.BlockSpec((1, D_MODEL), const)]
    args = [o_dn, o_swa, x, mod3, w_out, norm_ffn_w, w_up, conv_w, conv_b, w_down, final_w]
    if not carry:
        in_specs.append(pl.BlockSpec((nb, FFN_CONV - 1, 2 * D_FF), lambda i, j: (i, 0, 0)))
        args.append(state)
    return pl.pallas_call(
        functools.partial(_post_kernel, nb=nb, lt=lt, carry=carry, n_sub=n_sub),
        grid=(bsz // nb, nt),
        in_specs=in_specs,
        out_specs=[pl.BlockSpec((nb, lt * n_sub, D_MODEL), lambda i, j: (i, j, 0)),
                   pl.BlockSpec((nb, n_state, 2 * D_FF), lambda i, j: (i, 0, 0))],
        out_shape=[jax.ShapeDtypeStruct((bsz, seq, D_MODEL), F32),
                   jax.ShapeDtypeStruct((bsz, n_state, 2 * D_FF), F32)],
        scratch_shapes=[pltpu.VMEM((n_sub, rows, D_MODEL), BF16),
                        pltpu.VMEM((n_sub, rows, D_MODEL), F32),
                        pltpu.VMEM((n_sub, rows, D_MODEL), F32)],
        compiler_params=_cparams(("arbitrary", "arbitrary")),
        name="out_proj_convffn_carry" if carry else "out_proj_convffn_state",
    )(*args)


def _pad_row(vec, offset):
    return jnp.zeros((1, BA_PAD), F32).at[0, offset:offset + vec.shape[0]].set(vec)


def kernel(x_prompt, x_sample, state_dn_conv, state_dn_ssm, cache_swa_k, cache_swa_v, state_ffn_conv, c_prompt, c_sample, rel_bias, final_norm_w, w_ada, b_ada, norm_mix_w, w_in, dn_conv_w, dn_A_log, dn_dt_bias, dn_norm_w, swa_sinks, w_out, norm_ffn_w, ffn_w_up, ffn_conv_w, ffn_conv_b, ffn_w_down):
    bp, lp, _ = x_prompt.shape
    bs, ls, _ = x_sample.shape
    layer = 0

    w_in_t = jnp.transpose(w_in[layer])
    w_out_b = w_out[layer].astype(BF16)
    w_up_b = ffn_w_up[layer].astype(BF16)
    w_dn_b = ffn_w_down[layer].astype(BF16)
    alog_row = _pad_row(dn_A_log[layer], DN_HEADS)
    dt_row = _pad_row(dn_dt_bias[layer], DN_HEADS)
    row = lambda v: v.reshape(1, -1)

    n_c = bp + bs
    n_c_pad = -(-n_c // SUBLANES) * SUBLANES
    c_all = jnp.pad(jnp.concatenate([c_sample, c_prompt], axis=0), ((0, n_c_pad - n_c), (0, 0)))
    mod_s, mod_p = _mod_call(c_all, w_ada[layer], row(b_ada[layer]), bs, bp)

    bias_t, bias_s = _bias_call(rel_bias)
    sinks = swa_sinks[layer]

    def mixer_in(x, mod3, nb, lt, state=None):
        return _in_call(x, mod3, row(norm_mix_w[layer]), w_in_t, dn_conv_w[layer], nb, lt, state)

    def post(o_dn, o_swa, x, mod3, nb, lt, state=None):
        return _post_call(o_dn, o_swa, x, mod3, w_out_b, row(norm_ffn_w[layer]), w_up_b, ffn_conv_w[layer],
                          row(ffn_conv_b[layer]), w_dn_b, row(final_norm_w), nb, lt, state)

    dn_args = (alog_row, dt_row, row(dn_norm_w[layer]))

    sq, sk, sv, p_tail, o_dn, p_ssm = _mix_call(x_prompt, mod_p, row(norm_mix_w[layer]), w_in_t,
                                                dn_conv_w[layer], *dn_args)
    o_swa = _swa_prompt_call(sinks, sq, sk, sv, bias_t, bp, lp)
    y_prompt, p_ffn_tail = post(o_dn, o_swa, x_prompt, mod_p, 1, 256)
    p_dn_conv = p_tail[:, SUBLANES - (DN_CONV - 1):]
    p_fbuf = p_ffn_tail[:, SUBLANES - (FFN_CONV - 1):]
    last_window = lambda t: t.reshape(bp, lp, SWA_KV)[:, lp - WINDOW:].reshape(bp, WINDOW, SWA_KV_HEADS, SWA_HD)
    p_swa_k = last_window(sk)
    p_swa_v = last_window(sv)

    qkv_s, z_s, ba_s, sq_s, sk_s, sv_s, s_dn_conv = mixer_in(x_sample, mod_s, 64, ls, state=state_dn_conv[layer])
    o_dn_s, s_ssm = _dn_call(qkv_s, z_s, ba_s, *dn_args, bs, ls, state=state_dn_ssm[layer])
    o_swa_s, s_k, s_v = _swa_sample_call(sinks, sq_s, sk_s, sv_s,
                                         cache_swa_k[layer].reshape(bs, WINDOW, SWA_KV),
                                         cache_swa_v[layer].reshape(bs, WINDOW, SWA_KV), bias_s, bs, ls)
    y_sample, s_fbuf = post(o_dn_s, o_swa_s, x_sample, mod_s, 32, ls, state=state_ffn_conv[layer])

    return (y_prompt, y_sample, p_dn_conv[None], s_dn_conv[None], p_ssm[None], s_ssm[None],
            p_swa_k[None], s_k.reshape(bs, WINDOW, SWA_KV_HEADS, SWA_HD)[None],
            p_swa_v[None], s_v.reshape(bs, WINDOW, SWA_KV_HEADS, SWA_HD)[None],
            p_fbuf[None], s_fbuf[None])
```

```python
import functools
import math

import numpy as np
import jax
import jax.numpy as jnp
from jax import lax
from jax.experimental import pallas as pl
from jax.experimental.pallas import tpu as pltpu

F32 = jnp.float32
BF16 = jnp.bfloat16

D_MODEL = 1024
PAST_LEN = 16384
DN_HEADS = 4
DN_DK = 128
DN_DV = 128
DN_CONV = 4
SWA_HEADS = 8
SWA_KV_HEADS = 2
SWA_GROUP = SWA_HEADS // SWA_KV_HEADS
SWA_HD = 64
WINDOW = 128
N_BUCKETS = 32
MAX_DISTANCE = 128
D_FF = 2816
FFN_CONV = 3
EPS = 1e-6
NEG_INF = -1e30

DN_QK = DN_HEADS * DN_DK
DN_V = DN_HEADS * DN_DV
DN_CONV_CH = 2 * DN_QK + DN_V
SWA_Q = SWA_HEADS * SWA_HD
SWA_KV = SWA_KV_HEADS * SWA_HD
BA_PAD = 128
SUBLANES = 8
UNIT = 128
POST_SUB_TILES = 2
POST_LOOKAHEAD_CHUNKS = 3
FFN_CHUNK = 256
ROW_PITCH_PAD = 128
VMEM_LIMIT = 56 * 1024 * 1024
RESIDENT = pl.Buffered(1)


def _cparams(sem):
    return pltpu.CompilerParams(dimension_semantics=sem, vmem_limit_bytes=VMEM_LIMIT)


def _bf(x):
    return x.astype(BF16)


def _dot(a, b):
    return jnp.dot(_bf(a), _bf(b), preferred_element_type=F32)


def _dot_nt(a, b):
    return lax.dot_general(_bf(a), _bf(b), (((1,), (1,)), ((), ())), preferred_element_type=F32)


def _split2(x):
    hi = _bf(x)
    lo = _bf(x - hi.astype(F32))
    return hi, lo


def _dot_x3(a, b):
    ah, al = _split2(a)
    bh, bl = _split2(b)
    d = functools.partial(jnp.dot, preferred_element_type=F32)
    return d(ah, bh) + (d(al, bh) + d(ah, bl))


def _dot_mask(m, x):
    hi = _bf(x)
    r = x - hi.astype(F32)
    mid = _bf(r)
    lo = _bf(r - mid.astype(F32))
    d = functools.partial(jnp.dot, preferred_element_type=F32)
    return d(m, hi) + (d(m, mid) + d(m, lo))


def _sigmoid(x):
    return 1.0 / (1.0 + jnp.exp(-x))


def _silu(x):
    return x * _sigmoid(x)


def _softplus(x):
    return jnp.maximum(x, 0.0) + jnp.log1p(jnp.exp(-jnp.abs(x)))


def _rms(x, w):
    ms = jnp.mean(x * x, axis=-1, keepdims=True)
    return x * lax.rsqrt(ms + EPS) * w


def _l2norm(t):
    return t * lax.rsqrt(jnp.sum(t * t, axis=-1, keepdims=True) + EPS)


def _rows(m3, nb, lt):
    return jnp.broadcast_to(m3, (nb, lt, m3.shape[-1])).reshape(nb * lt, m3.shape[-1])


def _causal_conv(x, prev, w, nb, lt):
    width = w.shape[0]
    rows, ch = x.shape
    if nb == 1 and lt > SUBLANES:
        tiles = jnp.concatenate([prev, x.reshape(lt // SUBLANES, SUBLANES, ch)], axis=0)
        sub = lax.broadcasted_iota(jnp.int32, (1, SUBLANES, 1), 1)
        out = tiles[1:] * w[width - 1:width, :]
        for j in range(1, width):
            rot = pltpu.roll(tiles, j, axis=1)
            out = out + jnp.where(sub >= j, rot[1:], rot[:-1]) * w[width - 1 - j:width - j, :]
        return out.reshape(rows, ch)
    tmod = lax.broadcasted_iota(jnp.int32, (rows, 1), 0) & (lt - 1)
    out = x * w[width - 1:width, :]
    for j in range(1, width):
        sh = pltpu.roll(x, j, axis=0)
        for t in range(j):
            p = width - 1 - j + t
            sh = jnp.where(tmod == t, _rows(prev[:, p:p + 1, :], nb, lt), sh)
        out = out + sh * w[width - 1 - j:width - j, :]
    return out


def _last_rows(x, nb, lt, n_state):
    ch = x.shape[-1]
    if nb == 1 and lt > SUBLANES:
        return x[lt - SUBLANES:].reshape(1, SUBLANES, ch)
    return x.reshape(nb, lt, ch)[:, lt - n_state:, :]


def _state_rows(nb, lt, n_state):
    return SUBLANES if (nb == 1 and lt > SUBLANES) else n_state


MOD_PARTS = 6


def _mod_kernel(c_ref, w_ref, b_ref, os_ref, op_ref, *, n_s, n_p):
    part = pl.program_id(0)
    res = _dot_x3(_silu(c_ref[...]), w_ref[...]) + b_ref[...]
    for k in range(MOD_PARTS):
        @pl.when(part == k)
        def _():
            os_ref[:, k, :] = res[:n_s]
            op_ref[:, k, :] = res[n_s:n_s + n_p]


def _mod_call(c_all, w_ada, b_ada, n_s, n_p):
    rows = c_all.shape[0]
    return pl.pallas_call(
        functools.partial(_mod_kernel, n_s=n_s, n_p=n_p),
        grid=(MOD_PARTS,),
        in_specs=[pl.BlockSpec((rows, D_MODEL), lambda k: (0, 0)),
                  pl.BlockSpec((D_MODEL, D_MODEL), lambda k: (0, k)),
                  pl.BlockSpec((1, D_MODEL), lambda k: (0, k))],
        out_specs=[pl.BlockSpec((n_s, MOD_PARTS, D_MODEL), lambda k: (0, 0, 0)),
                   pl.BlockSpec((n_p, MOD_PARTS, D_MODEL), lambda k: (0, 0, 0))],
        out_shape=[jax.ShapeDtypeStruct((n_s, MOD_PARTS, D_MODEL), F32),
                   jax.ShapeDtypeStruct((n_p, MOD_PARTS, D_MODEL), F32)],
        compiler_params=_cparams(("arbitrary",)),
        name="adaln_mod",
    )(c_all, w_ada, b_ada)


IN_SPLIT = (DN_CONV_CH, DN_V, BA_PAD, SWA_Q, SWA_KV, SWA_KV)


IN_SUB_TILES = 4
IN_LOOKAHEAD_SLABS = 1
W_PREP_ROWS = 256
IN_SLAB = 2 * DN_DK


def _in_kernel(*refs, nb, lt, carry, n_sub):
    if carry:
        (x_ref, mod_ref, nw_ref, w_ref, cw_ref,
         qkv_ref, z_ref, ba_ref, sq_ref, sk_ref, sv_ref, tail_ref, h_scr, wdn_ref, wba_ref, wsw_ref) = refs
        prev_ref = tail_ref

        @pl.when(pl.program_id(1) == 0)
        def _():
            tail_ref[...] = jnp.zeros_like(tail_ref)
    else:
        (x_ref, mod_ref, nw_ref, w_ref, cw_ref, prev_ref,
         qkv_ref, z_ref, ba_ref, sq_ref, sk_ref, sv_ref, tail_ref, h_scr, wdn_ref, wba_ref, wsw_ref) = refs

    @pl.when((pl.program_id(0) == 0) & (pl.program_id(1) == 0))
    def _():
        ba_lo = DN_CONV_CH + DN_V
        n_ba = 2 * DN_HEADS
        for r in range(0, ba_lo, W_PREP_ROWS):
            wdn_ref[:, r:r + W_PREP_ROWS] = _bf(w_ref[r:r + W_PREP_ROWS, :].T)
        row = lax.broadcasted_iota(jnp.int32, (BA_PAD, 1), 0)
        wba_ref[...] = _bf(jnp.where(row < n_ba, w_ref[ba_lo:ba_lo + BA_PAD, :], 0.0).T)
        for r in range(0, SWA_Q + 2 * SWA_KV, W_PREP_ROWS):
            wsw_ref[:, r:r + W_PREP_ROWS] = _bf(w_ref[ba_lo + n_ba + r:ba_lo + n_ba + r + W_PREP_ROWS, :].T)

    rows = nb * lt

    def norm(sub):
        x = x_ref[:, sub * lt:(sub + 1) * lt, :]
        ms = jnp.mean(x * x, axis=-1, keepdims=True)
        y = x * lax.rsqrt(ms + EPS) * nw_ref[...]
        h = y * (1.0 + mod_ref[:, 1:2, :]) + mod_ref[:, 0:1, :]
        h_scr[sub, :, :D_MODEL] = _bf(h.reshape(rows, D_MODEL))

    def proj(sub, w_bf_ref, lo, n):
        return jnp.dot(h_scr[sub, :, :D_MODEL], w_bf_ref[:, lo:lo + n], preferred_element_type=F32)

    def plain_proj(sub, i):
        rr = slice(sub * rows, (sub + 1) * rows)
        if i < 2:
            z_ref[rr, i * IN_SLAB:(i + 1) * IN_SLAB] = proj(sub, wdn_ref, DN_CONV_CH + i * IN_SLAB, IN_SLAB)
        elif i == 2:
            sq_ref[rr, :] = proj(sub, wsw_ref, 0, SWA_Q)
        elif i == 3:
            kv = proj(sub, wsw_ref, SWA_Q, 2 * SWA_KV)
            sk_ref[rr, :] = kv[:, :SWA_KV]
            sv_ref[rr, :] = kv[:, SWA_KV:]
        elif i == 4:
            ba_ref[rr, :] = proj(sub, wba_ref, 0, BA_PAD)

    def conv_slab(sub, slab, raw):
        rr = slice(sub * rows, (sub + 1) * rows)
        cols = slice(slab * IN_SLAB, (slab + 1) * IN_SLAB)
        c = _silu(_causal_conv(raw, prev_ref[:, :, cols], cw_ref[:, cols], nb, lt))
        tail_ref[:, :, cols] = _last_rows(raw, nb, lt, DN_CONV - 1)
        if slab * IN_SLAB < 2 * DN_QK:
            scale = DN_DK ** -0.5 if slab * IN_SLAB < DN_QK else 1.0
            c = jnp.concatenate([_l2norm(c[:, i * DN_DK:(i + 1) * DN_DK]) * scale
                                 for i in range(IN_SLAB // DN_DK)], axis=1)
        qkv_ref[rr, cols] = c

    n_slabs = DN_CONV_CH // IN_SLAB
    norm(0)
    for sub in range(n_sub):
        raw_next = proj(sub, wdn_ref, 0, IN_SLAB)
        for slab in range(n_slabs):
            raw = raw_next
            plain_proj(sub, slab)
            if slab + 1 < n_slabs:
                raw_next = proj(sub, wdn_ref, (slab + 1) * IN_SLAB, IN_SLAB)
            if sub + 1 < n_sub and slab == n_slabs - 1 - IN_LOOKAHEAD_SLABS:
                norm(sub + 1)
            conv_slab(sub, slab, raw)


def _in_call(x, mod3, norm_w, w_in_t, conv_w, nb, lt, state=None):
    carry = state is None
    n_sub = IN_SUB_TILES if carry else 1
    bsz, seq, _ = x.shape
    nt = seq // (lt * n_sub)
    rows = nb * lt
    tile = rows * n_sub
    n_tok = bsz * seq
    n_state = _state_rows(nb, lt, DN_CONV - 1)
    row_map = lambda i, j: (i * nt + j, 0)
    const = lambda i, j: (0, 0)
    per_seq = lambda i, j: (i, 0, 0)
    in_specs = [pl.BlockSpec((nb, lt * n_sub, D_MODEL), lambda i, j: (i, j, 0)),
                pl.BlockSpec((nb, 6, D_MODEL), per_seq),
                pl.BlockSpec((1, D_MODEL), const),
                pl.BlockSpec(w_in_t.shape, const, pipeline_mode=RESIDENT),
                pl.BlockSpec((DN_CONV, DN_CONV_CH), const)]
    args = [x, mod3, norm_w, w_in_t, conv_w]
    if not carry:
        in_specs.append(pl.BlockSpec((nb, DN_CONV - 1, DN_CONV_CH), per_seq))
        args.append(state)
    return pl.pallas_call(
        functools.partial(_in_kernel, nb=nb, lt=lt, carry=carry, n_sub=n_sub),
        grid=(bsz // nb, nt),
        in_specs=in_specs,
        out_specs=[pl.BlockSpec((tile, n), row_map) for n in IN_SPLIT]
        + [pl.BlockSpec((nb, n_state, DN_CONV_CH), per_seq)],
        out_shape=[jax.ShapeDtypeStruct((n_tok, n), F32) for n in IN_SPLIT]
        + [jax.ShapeDtypeStruct((bsz, n_state, DN_CONV_CH), F32)],
        scratch_shapes=[pltpu.VMEM((n_sub, rows, D_MODEL + ROW_PITCH_PAD), BF16),
                        pltpu.VMEM((D_MODEL, DN_CONV_CH + DN_V + ROW_PITCH_PAD), BF16),
                        pltpu.VMEM((D_MODEL, BA_PAD), BF16),
                        pltpu.VMEM((D_MODEL, SWA_Q + 2 * SWA_KV + ROW_PITCH_PAD), BF16)],
        compiler_params=_cparams(("arbitrary", "arbitrary")),
        name="norm_in_proj_carry" if carry else "norm_in_proj_state",
    )(*args)


INV_BASE_SHIFT = 2
DN_UNITS_CARRY = 4
SWA_BLOCKS_PER_STEP = 4
DN_UNITS_STATE = 1


def _unit_lower_inverses(a_mats, ri, ci, chunk_shift):
    def blocks(s):
        return (ri >> s) == (ci >> s)

    base = min(INV_BASE_SHIFT, chunk_shift)
    xs = [jnp.where(blocks(base), -a, 0.0) for a in a_mats]
    ts = [jnp.where(ri == ci, 1.0, n) for n in xs]
    for lvl in range(1, base):
        xs = [_dot(x, x) for x in xs]
        ts = [t + _dot(t, x) for t, x in zip(ts, xs)]
    for s in range(base, chunk_shift):
        sel = blocks(s + 1) & jnp.logical_not(blocks(s))
        ets = [_dot(jnp.where(sel, a, 0.0), t) for a, t in zip(a_mats, ts)]
        ts = [t - _dot(t, et) for t, et in zip(ts, ets)]
    return ts


def _dn_kernel(*refs, nb, chunk, carry, units):
    if carry:
        (qkv_ref, z_ref, ba_ref, alog_ref, dt_ref, nw_ref, o_ref, s_ref) = refs
        s0_ref = s_ref

        @pl.when(pl.program_id(1) == 0)
        def _():
            s_ref[...] = jnp.zeros_like(s_ref)
    else:
        (qkv_ref, z_ref, ba_ref, alog_ref, dt_ref, nw_ref, s0_ref, o_ref, s_ref) = refs

    ri = lax.broadcasted_iota(jnp.int32, (UNIT, UNIT), 0)
    ci = lax.broadcasted_iota(jnp.int32, (UNIT, UNIT), 1)
    shift = int(math.log2(chunk))
    same = (ri >> shift) == (ci >> shift)
    incl = same & (ri >= ci)
    strict = same & (ri > ci)

    ba = ba_ref[...]
    beta_full = _sigmoid(ba)
    g_full = -jnp.exp(alog_ref[...]) * _softplus(ba + dt_ref[...])
    masks = jnp.concatenate([jnp.where(incl, 1.0, 0.0), jnp.where(same, 1.0, 0.0)], axis=0).astype(BF16)
    gsums = [_dot_mask(masks, g_full[u * UNIT:(u + 1) * UNIT]) for u in range(units)]
    g_cum = [g[:UNIT] for g in gsums]
    g_tot = [g[UNIT:] for g in gsums]
    g_cum_t = [g.T for g in g_cum]

    probs = [(u, h) for u in range(units) for h in range(DN_HEADS)]
    n_p = len(probs)
    rs = lambda u: slice(u * UNIT, (u + 1) * UNIT)

    def head_cols(base):
        return [qkv_ref[rs(u), base + h * DN_DK:base + (h + 1) * DN_DK] for u, h in probs]

    q = head_cols(0)
    k = head_cols(DN_QK)
    v = head_cols(2 * DN_QK)
    gc = [g_cum[u][:, DN_HEADS + h:DN_HEADS + h + 1] for u, h in probs]
    gr = [g_cum_t[u][DN_HEADS + h:DN_HEADS + h + 1, :] for u, h in probs]
    gt = [g_tot[u][:, DN_HEADS + h:DN_HEADS + h + 1] for u, h in probs]
    bc = [beta_full[rs(u), h:h + 1] for u, h in probs]
    decay = [jnp.where(incl, jnp.exp(jnp.where(incl, gc[p] - gr[p], 0.0)), 0.0) for p in range(n_p)]
    e_g = [jnp.exp(gc[p]) for p in range(n_p)]
    kq = [_dot_nt(jnp.concatenate([k[p], q[p]], axis=0), k[p]) for p in range(n_p)]
    qk = [kq[p][UNIT:] * decay[p] for p in range(n_p)]
    a_mats = [jnp.where(strict, bc[p] * kq[p][:UNIT] * decay[p], 0.0) for p in range(n_p)]
    t_inv = _unit_lower_inverses(a_mats, ri, ci, shift)
    wvk = [_dot(t_inv[p], jnp.concatenate([v[p] * bc[p], k[p] * (bc[p] * e_g[p])], axis=1)) for p in range(n_p)]
    w_v = [w[:, :DN_DV] for w in wvk]
    w_k = [w[:, DN_DV:] for w in wvk]
    q_dec = [q[p] * e_g[p] for p in range(n_p)]
    k_tail_t = [(k[p] * jnp.exp(gt[p] - gc[p])).T for p in range(n_p)]
    c_dec = [jnp.exp(gt[p]) for p in range(n_p)]

    outs = {}
    if carry:
        state = [s_ref[0, h] for h in range(DN_HEADS)]
        for u in range(units):
            ps = [u * DN_HEADS + h for h in range(DN_HEADS)]
            r = [_dot(jnp.concatenate([w_k[p], q_dec[p]], axis=0), state[h]) for h, p in enumerate(ps)]
            uu = [w_v[p] - r[h][:UNIT] for h, p in enumerate(ps)]
            for h, p in enumerate(ps):
                outs[p] = r[h][UNIT:] + _dot(qk[p], uu[h])
            state = [state[h] * c_dec[p][0:1, :] + _dot(k_tail_t[p], uu[h]) for h, p in enumerate(ps)]
        for h in range(DN_HEADS):
            s_ref[0, h] = state[h]
    else:
        per_unit = nb // units
        for p, (u, h) in enumerate(probs):
            us, qs = [], []
            for s in range(per_unit):
                lo = s * chunk
                lhs = jnp.concatenate([w_k[p][lo:lo + chunk], q_dec[p][lo:lo + chunk]], axis=0)
                r = jnp.dot(lhs, s0_ref[u * per_unit + s, h], preferred_element_type=F32)
                us.append(w_v[p][lo:lo + chunk] - r[:chunk])
                qs.append(r[chunk:])
            uu = jnp.concatenate(us, axis=0)
            outs[p] = jnp.concatenate(qs, axis=0) + _dot(qk[p], uu)
            for s in range(per_unit):
                lo = s * chunk
                upd = jnp.dot(k_tail_t[p][:, lo:lo + chunk], uu[lo:lo + chunk], preferred_element_type=F32)
                s_ref[u * per_unit + s, h] = s0_ref[u * per_unit + s, h] * c_dec[p][lo:lo + 1, :] + upd

    for u in range(units):
        gated = [_rms(outs[u * DN_HEADS + h], nw_ref[...]) * _silu(z_ref[rs(u), h * DN_DV:(h + 1) * DN_DV])
                 for h in range(DN_HEADS)]
        o_ref[rs(u), :] = jnp.concatenate(gated, axis=1)


def _dn_call(qkv, z, ba, alog_row, dt_row, norm_w, bsz, seq, state=None):
    carry = state is None
    units = DN_UNITS_CARRY if carry else DN_UNITS_STATE
    rows = units * UNIT
    nb = 1 if carry else rows // seq
    chunk = UNIT if carry else seq
    nt = seq * nb // rows
    n_tok = bsz * seq
    row_map = lambda i, j: (i * nt + j, 0)
    const = lambda i, j: (0, 0)
    in_specs = [pl.BlockSpec((rows, DN_CONV_CH), row_map),
                pl.BlockSpec((rows, DN_V), row_map),
                pl.BlockSpec((rows, BA_PAD), row_map),
                pl.BlockSpec((1, BA_PAD), const),
                pl.BlockSpec((1, BA_PAD), const),
                pl.BlockSpec((1, DN_DV), const)]
    args = [qkv, z, ba, alog_row, dt_row, norm_w]
    if not carry:
        in_specs.append(pl.BlockSpec((nb, DN_HEADS, DN_DK, DN_DV), lambda i, j: (i, 0, 0, 0)))
        args.append(state)
    return pl.pallas_call(
        functools.partial(_dn_kernel, nb=nb, chunk=chunk, carry=carry, units=units),
        grid=(bsz // nb, nt),
        in_specs=in_specs,
        out_specs=[pl.BlockSpec((rows, DN_V), row_map),
                   pl.BlockSpec((nb, DN_HEADS, DN_DK, DN_DV), lambda i, j: (i, 0, 0, 0))],
        out_shape=[jax.ShapeDtypeStruct((n_tok, DN_V), F32),
                   jax.ShapeDtypeStruct((bsz, DN_HEADS, DN_DK, DN_DV), F32)],
        compiler_params=_cparams(("arbitrary", "arbitrary")),
        name="gated_deltanet_carry" if carry else "gated_deltanet_state",
    )(*args)


MIX_SUB_TILES = 2
MIX_ROWS = MIX_SUB_TILES * 2 * UNIT


def _mix_kernel(x_ref, mod_ref, nw_ref, w_ref, cw_ref, alog_ref, dt_ref, dnw_ref,
                sq_ref, sk_ref, sv_ref, tail_ref, o_ref, s_ref,
                h_scr, wdn_ref, wba_ref, wsw_ref, qkv_scr, z_scr, ba_scr):
    _in_kernel(x_ref, mod_ref, nw_ref, w_ref, cw_ref,
               qkv_scr, z_scr, ba_scr, sq_ref, sk_ref, sv_ref, tail_ref, h_scr, wdn_ref, wba_ref, wsw_ref,
               nb=1, lt=2 * UNIT, carry=True, n_sub=MIX_SUB_TILES)
    _dn_kernel(qkv_scr, z_scr, ba_scr, alog_ref, dt_ref, dnw_ref, o_ref, s_ref,
               nb=1, chunk=UNIT, carry=True, units=MIX_ROWS // UNIT)


def _mix_call(x, mod3, norm_w, w_in_t, conv_w, alog_row, dt_row, dn_norm_w):
    bsz, seq, _ = x.shape
    nt = seq // MIX_ROWS
    n_tok = bsz * seq
    row_map = lambda i, j: (i * nt + j, 0)
    const = lambda i, j: (0, 0)
    per_seq = lambda i, j: (i, 0, 0)
    swa_cols = (SWA_Q, SWA_KV, SWA_KV)
    return pl.pallas_call(
        _mix_kernel,
        grid=(bsz, nt),
        in_specs=[pl.BlockSpec((1, MIX_ROWS, D_MODEL), lambda i, j: (i, j, 0)),
                  pl.BlockSpec((1, 6, D_MODEL), per_seq),
                  pl.BlockSpec((1, D_MODEL), const),
                  pl.BlockSpec(w_in_t.shape, const, pipeline_mode=RESIDENT),
                  pl.BlockSpec((DN_CONV, DN_CONV_CH), const),
                  pl.BlockSpec((1, BA_PAD), const),
                  pl.BlockSpec((1, BA_PAD), const),
                  pl.BlockSpec((1, DN_DV), const)],
        out_specs=[pl.BlockSpec((MIX_ROWS, n), row_map) for n in swa_cols]
        + [pl.BlockSpec((1, SUBLANES, DN_CONV_CH), per_seq),
           pl.BlockSpec((MIX_ROWS, DN_V), row_map),
           pl.BlockSpec((1, DN_HEADS, DN_DK, DN_DV), lambda i, j: (i, 0, 0, 0))],
        out_shape=[jax.ShapeDtypeStruct((n_tok, n), F32) for n in swa_cols]
        + [jax.ShapeDtypeStruct((bsz, SUBLANES, DN_CONV_CH), F32),
           jax.ShapeDtypeStruct((n_tok, DN_V), F32),
           jax.ShapeDtypeStruct((bsz, DN_HEADS, DN_DK, DN_DV), F32)],
        scratch_shapes=[pltpu.VMEM((MIX_SUB_TILES, 2 * UNIT, D_MODEL + ROW_PITCH_PAD), BF16),
                        pltpu.VMEM((D_MODEL, DN_CONV_CH + DN_V + ROW_PITCH_PAD), BF16),
                        pltpu.VMEM((D_MODEL, BA_PAD), BF16),
                        pltpu.VMEM((D_MODEL, SWA_Q + 2 * SWA_KV + ROW_PITCH_PAD), BF16),
                        pltpu.VMEM((MIX_ROWS, DN_CONV_CH + ROW_PITCH_PAD), F32),
                        pltpu.VMEM((MIX_ROWS, DN_V + ROW_PITCH_PAD), F32),
                        pltpu.VMEM((MIX_ROWS, BA_PAD), F32)],
        compiler_params=_cparams(("arbitrary", "arbitrary")),
        name="in_proj_deltanet_carry",
    )(x, mod3, norm_w, w_in_t, conv_w, alog_row, dt_row, dn_norm_w)


def _bucket_table():
    i = np.arange(WINDOW, dtype=np.int64)[:, None]
    j = np.arange(2 * WINDOW, dtype=np.int64)[None, :]
    d = np.maximum(i + WINDOW - j, 0)
    exact = N_BUCKETS // 2
    logv = (np.log(np.maximum(d, 1).astype(np.float32) / np.float32(exact)).astype(np.float32)
            / np.float32(math.log(MAX_DISTANCE / exact)))
    large = np.minimum(exact + (logv * np.float32(N_BUCKETS - exact)).astype(np.int32), N_BUCKETS - 1)
    return np.where(d < exact, d, large).astype(np.int32)


def _bias_lookup(rb_ref, bucket, head):
    acc = jnp.zeros(bucket.shape, F32)
    for b in range(N_BUCKETS):
        acc = jnp.where(bucket == b, rb_ref[b, head], acc)
    return acc


def _bias_kernel(rb_ref, bucket_t_ref, bucket_ref, ot_ref, os_ref):
    bucket_t = bucket_t_ref[...]
    kj = lax.broadcasted_iota(jnp.int32, bucket_t.shape, 0)
    qi = lax.broadcasted_iota(jnp.int32, bucket_t.shape, 1)
    dist = qi + WINDOW - kj
    valid = (dist >= 0) & (dist < WINDOW)
    for hk in range(SWA_KV_HEADS):
        for par in range(2):
            for st in range(SWA_GROUP // 2):
                head = hk * SWA_GROUP + 2 * st + par
                gen = jnp.where(valid, _bias_lookup(rb_ref, bucket_t, head), NEG_INF)
                ot_ref[1, hk, par, :, st * WINDOW:(st + 1) * WINDOW] = gen
                ot_ref[0, hk, par, :, st * WINDOW:(st + 1) * WINDOW] = jnp.where(kj >= WINDOW, gen, NEG_INF)
    bucket = bucket_ref[...]
    qi = lax.broadcasted_iota(jnp.int32, bucket.shape, 0)
    kj = lax.broadcasted_iota(jnp.int32, bucket.shape, 1)
    dist = qi + WINDOW - kj
    valid = (dist >= 0) & (dist < WINDOW)
    for head in range(SWA_HEADS):
        os_ref[head] = jnp.where(valid, _bias_lookup(rb_ref, bucket, head), NEG_INF)


BIAS_T_SHAPE = (2, SWA_KV_HEADS, 2, 2 * WINDOW, (SWA_GROUP // 2) * WINDOW)
BIAS_S_SHAPE = (SWA_HEADS, SUBLANES, 2 * WINDOW)


def _bias_call(rel_bias):
    bucket = _bucket_table()
    return pl.pallas_call(
        _bias_kernel,
        in_specs=[pl.BlockSpec(memory_space=pltpu.SMEM),
                  pl.BlockSpec((2 * WINDOW, WINDOW), lambda: (0, 0)),
                  pl.BlockSpec((SUBLANES, 2 * WINDOW), lambda: (0, 0))],
        out_specs=[pl.BlockSpec(BIAS_T_SHAPE, lambda: (0,) * len(BIAS_T_SHAPE)),
                   pl.BlockSpec(BIAS_S_SHAPE, lambda: (0,) * len(BIAS_S_SHAPE))],
        out_shape=[jax.ShapeDtypeStruct(BIAS_T_SHAPE, F32), jax.ShapeDtypeStruct(BIAS_S_SHAPE, F32)],
        name="swa_rel_bias_table",
    )(rel_bias, jnp.asarray(np.ascontiguousarray(bucket.T)), jnp.asarray(bucket[:SUBLANES]))


def _softmax_sink_parts(s, sink):
    m = jnp.maximum(jnp.max(s, axis=0, keepdims=True), sink)
    p = jnp.exp(s - m)
    return p, 1.0 / (jnp.sum(p, axis=0, keepdims=True) + jnp.exp(sink - m))


def _half_lane_variants(full, hk, lo_half):
    rolled = pltpu.roll(full, SWA_HD, axis=1)
    low_src, high_src = (full, rolled) if hk == 0 else (rolled, full)
    return jnp.where(lo_half, low_src, 0.0), jnp.where(lo_half, 0.0, high_src)


def _swa_prompt_kernel(sink_ref, q_ref, kp_ref, kc_ref, vp_ref, vc_ref, bias_ref, o_ref, *, n_blk):
    step = pl.program_id(1)
    lo_half = lax.broadcasted_iota(jnp.int32, (1, 2 * SWA_HD), 1) < SWA_HD
    lo_rows = lax.broadcasted_iota(jnp.int32, (2 * SWA_HD, 1), 0) < SWA_HD
    q = _bf(q_ref[...] * (SWA_HD ** -0.5))
    keys = jnp.concatenate([kp_ref[...], kc_ref[...]], axis=0)
    vals = jnp.concatenate([vp_ref[...], vc_ref[...]], axis=0)
    k_var = [[_bf(t) for t in _half_lane_variants(keys, hk, lo_half)] for hk in range(SWA_KV_HEADS)]
    v_var_t = [[_bf(t.T) for t in _half_lane_variants(vals, hk, lo_half)] for hk in range(SWA_KV_HEADS)]
    n_stack = SWA_GROUP // 2
    sinks = [[jnp.concatenate([jnp.full((1, WINDOW), sink_ref[hk * SWA_GROUP + 2 * st + par], F32)
                               for st in range(n_stack)], axis=1) for par in range(2)]
             for hk in range(SWA_KV_HEADS)]
    nt_dims = (((1,), (1,)), ((), ()))

    def scores(b):
        rows = slice(b * WINDOW, (b + 1) * WINDOW)
        win = slice(b * WINDOW, (b + 2) * WINDOW)
        variant = jnp.where(step == 0, 0, 1) if b == 0 else 1
        out = []
        for hk in range(SWA_KV_HEADS):
            q2 = jnp.concatenate([q[rows, (hk * n_stack + st) * 2 * SWA_HD:(hk * n_stack + st + 1) * 2 * SWA_HD]
                                  for st in range(n_stack)], axis=0)
            out.append([lax.dot_general(k_var[hk][par][win], q2, nt_dims, preferred_element_type=F32)
                        + bias_ref[variant, hk, par] for par in range(2)])
        return out

    s_next = scores(0)
    for b in range(n_blk):
        s_cur = s_next
        if b + 1 < n_blk:
            s_next = scores(b + 1)
        rows = slice(b * WINDOW, (b + 1) * WINDOW)
        win = slice(b * WINDOW, (b + 2) * WINDOW)
        for hk in range(SWA_KV_HEADS):
            parts = [_softmax_sink_parts(s_cur[hk][par], sinks[hk][par]) for par in range(2)]
            o_t = (jnp.dot(v_var_t[hk][0][:, win], _bf(parts[0][0]), preferred_element_type=F32)
                   + jnp.dot(v_var_t[hk][1][:, win], _bf(parts[1][0]), preferred_element_type=F32))
            o_t = o_t * jnp.where(lo_rows, parts[0][1], parts[1][1])
            for st in range(n_stack):
                lo = (hk * n_stack + st) * 2 * SWA_HD
                o_ref[rows, lo:lo + 2 * SWA_HD] = o_t[:, st * WINDOW:(st + 1) * WINDOW].T


def _swa_prompt_call(sinks, sq, sk, sv, bias, bsz, seq):
    n_blk = SWA_BLOCKS_PER_STEP
    tile = n_blk * WINDOW
    nt = seq // tile
    cur = lambda b, i: (b * nt + i, 0)
    prv = lambda b, i: (b * nt * n_blk + jnp.maximum(i * n_blk - 1, 0), 0)
    return pl.pallas_call(
        functools.partial(_swa_prompt_kernel, n_blk=n_blk),
        grid=(bsz, nt),
        in_specs=[pl.BlockSpec(memory_space=pltpu.SMEM),
                  pl.BlockSpec((tile, SWA_Q), cur),
                  pl.BlockSpec((WINDOW, SWA_KV), prv),
                  pl.BlockSpec((tile, SWA_KV), cur),
                  pl.BlockSpec((WINDOW, SWA_KV), prv),
                  pl.BlockSpec((tile, SWA_KV), cur),
                  pl.BlockSpec(BIAS_T_SHAPE, lambda b, i: (0,) * len(BIAS_T_SHAPE))],
        out_specs=pl.BlockSpec((tile, SWA_Q), cur),
        out_shape=jax.ShapeDtypeStruct((bsz * seq, SWA_Q), F32),
        compiler_params=_cparams(("arbitrary", "arbitrary")),
        name="swa_banded",
    )(sinks, sq, sk, sk, sv, sv, bias)


def _swa_sample_kernel(sink_ref, q_ref, kn_ref, vn_ref, kc_ref, vc_ref, bias_ref, o_ref, ko_ref, vo_ref, *, nb, lt):
    lo_half = lax.broadcasted_iota(jnp.int32, (1, 2 * SWA_HD), 1) < SWA_HD
    q_all = q_ref[...] * (SWA_HD ** -0.5)
    kn_all = kn_ref[...]
    vn_all = vn_ref[...]
    pad = jnp.zeros((WINDOW - lt, SWA_KV), F32)
    bias = jnp.concatenate([bias_ref[h, 0:lt, :] for h in range(SWA_HEADS)], axis=0)
    sink = jnp.concatenate([jnp.full((lt, 1), sink_ref[h], F32) for h in range(SWA_HEADS)], axis=0)
    nt_dims = (((1,), (1,)), ((), ()))

    def to_half(tile, src_half, dst_half):
        return tile if src_half == dst_half else pltpu.roll(tile, SWA_HD, axis=1)

    lhs, keys, vals = [], [], []
    for s in range(nb):
        rows = slice(s * lt, (s + 1) * lt)
        kc, vc, kn, vn = kc_ref[s], vc_ref[s], kn_all[rows], vn_all[rows]
        ko_ref[s] = jnp.concatenate([kc[lt:], kn], axis=0)
        vo_ref[s] = jnp.concatenate([vc[lt:], vn], axis=0)
        keys.append(_bf(jnp.concatenate([kc, kn, pad], axis=0)))
        vals.append(_bf(jnp.concatenate([vc, vn, pad], axis=0)))
        q = q_all[rows]
        tiles = []
        for h in range(SWA_HEADS):
            hk = h // SWA_GROUP
            t = to_half(q[:, (h // 2) * 2 * SWA_HD:(h // 2 + 1) * 2 * SWA_HD], h % 2, hk)
            tiles.append(jnp.where(lo_half if hk == 0 else jnp.logical_not(lo_half), t, 0.0))
        lhs.append(_bf(jnp.concatenate(tiles, axis=0)))
    scores = [lax.dot_general(lhs[s], keys[s], nt_dims, preferred_element_type=F32) + bias for s in range(nb)]
    m = [jnp.maximum(jnp.max(sc, axis=-1, keepdims=True), sink) for sc in scores]
    p = [jnp.exp(sc - mm) for sc, mm in zip(scores, m)]
    rinv = [1.0 / (jnp.sum(pp, axis=-1, keepdims=True) + jnp.exp(sink - mm)) for pp, mm in zip(p, m)]
    res = [jnp.dot(_bf(p[s]), vals[s], preferred_element_type=F32) * rinv[s] for s in range(nb)]
    rows_out = []
    for s in range(nb):
        tiles = []
        for pair in range(SWA_HEADS // 2):
            hk = (2 * pair) // SWA_GROUP
            low = to_half(res[s][(2 * pair) * lt:(2 * pair + 1) * lt], hk, 0)
            high = to_half(res[s][(2 * pair + 1) * lt:(2 * pair + 2) * lt], hk, 1)
            tiles.append(jnp.where(lo_half, low, high))
        rows_out.append(jnp.concatenate(tiles, axis=1))
    o_ref[...] = jnp.concatenate(rows_out, axis=0)


def _swa_sample_call(sinks, sq, sk, sv, cache_k, cache_v, bias, bsz, seq):
    nb = UNIT // seq
    rows = lambda i: (i, 0)
    seqs = lambda i: (i, 0, 0)
    return pl.pallas_call(
        functools.partial(_swa_sample_kernel, nb=nb, lt=seq),
        grid=(bsz // nb,),
        in_specs=[pl.BlockSpec(memory_space=pltpu.SMEM),
                  pl.BlockSpec((UNIT, SWA_Q), rows),
                  pl.BlockSpec((UNIT, SWA_KV), rows),
                  pl.BlockSpec((UNIT, SWA_KV), rows),
                  pl.BlockSpec((nb, WINDOW, SWA_KV), seqs),
                  pl.BlockSpec((nb, WINDOW, SWA_KV), seqs),
                  pl.BlockSpec(BIAS_S_SHAPE, lambda i: (0,) * len(BIAS_S_SHAPE))],
        out_specs=[pl.BlockSpec((UNIT, SWA_Q), rows),
                   pl.BlockSpec((nb, WINDOW, SWA_KV), seqs),
                   pl.BlockSpec((nb, WINDOW, SWA_KV), seqs)],
        out_shape=[jax.ShapeDtypeStruct((bsz * seq, SWA_Q), F32),
                   jax.ShapeDtypeStruct((bsz, WINDOW, SWA_KV), F32),
                   jax.ShapeDtypeStruct((bsz, WINDOW, SWA_KV), F32)],
        compiler_params=_cparams(("arbitrary",)),
        name="swa_cached",
    )(sinks, sq, sk, sv, cache_k, cache_v, bias)


def _post_kernel(*refs, nb, lt, carry, n_sub):
    if carry:
        (odn_ref, oswa_ref, x_ref, mod_ref, wout_ref, nfw_ref, wup_ref, cw_ref, cb_ref, wdn_ref, fnw_ref,
         y_ref, fbuf_ref, h_scr, x1_scr, acc_scr) = refs
        prev_ref = fbuf_ref

        @pl.when(pl.program_id(1) == 0)
        def _():
            fbuf_ref[...] = jnp.zeros_like(fbuf_ref)
    else:
        (odn_ref, oswa_ref, x_ref, mod_ref, wout_ref, nfw_ref, wup_ref, cw_ref, cb_ref, wdn_ref, fnw_ref,
         prev_ref, y_ref, fbuf_ref, h_scr, x1_scr, acc_scr) = refs
    rows = nb * lt
    n_chunks = D_FF // FFN_CHUNK

    def col_slices(c):
        return [slice(base + c * FFN_CHUNK, base + (c + 1) * FFN_CHUNK) for base in (0, D_FF)]

    def prologue(sub):
        rr = slice(sub * rows, (sub + 1) * rows)
        attn = (jnp.dot(_bf(odn_ref[rr, :]), wout_ref[0:DN_V, :], preferred_element_type=F32)
                + jnp.dot(_bf(oswa_ref[rr, :]), wout_ref[DN_V:, :], preferred_element_type=F32))
        x = x_ref[:, sub * lt:(sub + 1) * lt, :].reshape(rows, D_MODEL)
        x1 = x + _rows(mod_ref[:, 2:3, :], nb, lt) * attn
        x1_scr[sub] = x1
        h = (_rms(x1, nfw_ref[...]) * (1.0 + _rows(mod_ref[:, 4:5, :], nb, lt))
             + _rows(mod_ref[:, 3:4, :], nb, lt))
        h_scr[sub] = _bf(h)
        acc_scr[sub] = jnp.zeros((rows, D_MODEL), F32)

    def up_proj(sub, c):
        return [jnp.dot(h_scr[sub], wup_ref[:, cols], preferred_element_type=F32) for cols in col_slices(c)]

    def chunk(sub, c, u_cur):
        halves = []
        for u, cols in zip(u_cur, col_slices(c)):
            prev = prev_ref[:, :, cols]
            halves.append(_causal_conv(u, prev, cw_ref[:, cols], nb, lt) + cb_ref[:, cols])
            fbuf_ref[:, :, cols] = _last_rows(u, nb, lt, FFN_CONV - 1)
        act = _silu(halves[0]) * halves[1]
        acc_scr[sub] += jnp.dot(_bf(act), wdn_ref[c * FFN_CHUNK:(c + 1) * FFN_CHUNK, :],
                                preferred_element_type=F32)

    def epilogue(sub):
        x2 = x1_scr[sub] + _rows(mod_ref[:, 5:6, :], nb, lt) * acc_scr[sub]
        y_ref[:, sub * lt:(sub + 1) * lt, :] = _rms(x2, fnw_ref[...]).reshape(nb, lt, D_MODEL)

    prologue(0)
    for sub in range(n_sub):
        u_next = up_proj(sub, 0)
        for c in range(n_chunks):
            u_cur = u_next
            if c + 1 < n_chunks:
                u_next = up_proj(sub, c + 1)
            chunk(sub, c, u_cur)
            if sub + 1 < n_sub and c == n_chunks - 1 - POST_LOOKAHEAD_CHUNKS:
                prologue(sub + 1)
        epilogue(sub)


def _post_call(o_dn, o_swa, x, mod3, w_out, norm_ffn_w, w_up, conv_w, conv_b, w_down, final_w, nb, lt, state=None):
    carry = state is None
    n_sub = POST_SUB_TILES if carry else 1
    bsz, seq, _ = x.shape
    nt = seq // (lt * n_sub)
    rows = nb * lt
    tile = rows * n_sub
    n_state = _state_rows(nb, lt, FFN_CONV - 1)
    row_map = lambda i, j: (i * nt + j, 0)
    const = lambda i, j: (0, 0)
    in_specs = [pl.BlockSpec((tile, DN_V), row_map),
                pl.BlockSpec((tile, SWA_Q), row_map),
                pl.BlockSpec((nb, lt * n_sub, D_MODEL), lambda i, j: (i, j, 0)),
                pl.BlockSpec((nb, 6, D_MODEL), lambda i, j: (i, 0, 0)),
                pl.BlockSpec((D_MODEL, D_MODEL), const, pipeline_mode=RESIDENT),
                pl.BlockSpec((1, D_MODEL), const),
                pl.BlockSpec((D_MODEL, 2 * D_FF), const, pipeline_mode=RESIDENT),
                pl.BlockSpec((FFN_CONV, 2 * D_FF), const),
                pl.BlockSpec((1, 2 * D_FF), const),
                pl.BlockSpec((D_FF, D_MODEL), const, pipeline_mode=RESIDENT),
                pl.BlockSpec((1, D_MODEL), const)]
    args = [o_dn, o_swa, x, mod3, w_out, norm_ffn_w, w_up, conv_w, conv_b, w_down, final_w]
    if not carry:
        in_specs.append(pl.BlockSpec((nb, FFN_CONV - 1, 2 * D_FF), lambda i, j: (i, 0, 0)))
        args.append(state)
    return pl.pallas_call(
        functools.partial(_post_kernel, nb=nb, lt=lt, carry=carry, n_sub=n_sub),
        grid=(bsz // nb, nt),
        in_specs=in_specs,
        out_specs=[pl.BlockSpec((nb, lt * n_sub, D_MODEL), lambda i, j: (i, j, 0)),
                   pl.BlockSpec((nb, n_state, 2 * D_FF), lambda i, j: (i, 0, 0))],
        out_shape=[jax.ShapeDtypeStruct((bsz, seq, D_MODEL), F32),
                   jax.ShapeDtypeStruct((bsz, n_state, 2 * D_FF), F32)],
        scratch_shapes=[pltpu.VMEM((n_sub, rows, D_MODEL), BF16),
                        pltpu.VMEM((n_sub, rows, D_MODEL), F32),
                        pltpu.VMEM((n_sub, rows, D_MODEL), F32)],
        compiler_params=_cparams(("arbitrary", "arbitrary")),
        name="out_proj_convffn_carry" if carry else "out_proj_convffn_state",
    )(*args)


def _pad_row(vec, offset):
    return jnp.zeros((1, BA_PAD), F32).at[0, offset:offset + vec.shape[0]].set(vec)


def kernel(x_prompt, x_sample, state_dn_conv, state_dn_ssm, cache_swa_k, cache_swa_v, state_ffn_conv, c_prompt, c_sample, rel_bias, final_norm_w, w_ada, b_ada, norm_mix_w, w_in, dn_conv_w, dn_A_log, dn_dt_bias, dn_norm_w, swa_sinks, w_out, norm_ffn_w, ffn_w_up, ffn_conv_w, ffn_conv_b, ffn_w_down):
    bp, lp, _ = x_prompt.shape
    bs, ls, _ = x_sample.shape
    layer = 0

    w_in_t = jnp.transpose(w_in[layer])
    w_out_b = w_out[layer].astype(BF16)
    w_up_b = ffn_w_up[layer].astype(BF16)
    w_dn_b = ffn_w_down[layer].astype(BF16)
    alog_row = _pad_row(dn_A_log[layer], DN_HEADS)
    dt_row = _pad_row(dn_dt_bias[layer], DN_HEADS)
    row = lambda v: v.reshape(1, -1)

    n_c = bp + bs
    n_c_pad = -(-n_c // SUBLANES) * SUBLANES
    c_all = jnp.pad(jnp.concatenate([c_sample, c_prompt], axis=0), ((0, n_c_pad - n_c), (0, 0)))
    mod_s, mod_p = _mod_call(c_all, w_ada[layer], row(b_ada[layer]), bs, bp)

    bias_t, bias_s = _bias_call(rel_bias)
    sinks = swa_sinks[layer]

    def mixer_in(x, mod3, nb, lt, state=None):
        return _in_call(x, mod3, row(norm_mix_w[layer]), w_in_t, dn_conv_w[layer], nb, lt, state)

    def post(o_dn, o_swa, x, mod3, nb, lt, state=None):
        return _post_call(o_dn, o_swa, x, mod3, w_out_b, row(norm_ffn_w[layer]), w_up_b, ffn_conv_w[layer],
                          row(ffn_conv_b[layer]), w_dn_b, row(final_norm_w), nb, lt, state)

    dn_args = (alog_row, dt_row, row(dn_norm_w[layer]))

    sq, sk, sv, p_tail, o_dn, p_ssm = _mix_call(x_prompt, mod_p, row(norm_mix_w[layer]), w_in_t,
                                                dn_conv_w[layer], *dn_args)
    o_swa = _swa_prompt_call(sinks, sq, sk, sv, bias_t, bp, lp)
    y_prompt, p_ffn_tail = post(o_dn, o_swa, x_prompt, mod_p, 1, 256)
    p_dn_conv = p_tail[:, SUBLANES - (DN_CONV - 1):]
    p_fbuf = p_ffn_tail[:, SUBLANES - (FFN_CONV - 1):]
    last_window = lambda t: t.reshape(bp, lp, SWA_KV)[:, lp - WINDOW:].reshape(bp, WINDOW, SWA_KV_HEADS, SWA_HD)
    p_swa_k = last_window(sk)
    p_swa_v = last_window(sv)

    qkv_s, z_s, ba_s, sq_s, sk_s, sv_s, s_dn_conv = mixer_in(x_sample, mod_s, 64, ls, state=state_dn_conv[layer])
    o_dn_s, s_ssm = _dn_call(qkv_s, z_s, ba_s, *dn_args, bs, ls, state=state_dn_ssm[layer])
    o_swa_s, s_k, s_v = _swa_sample_call(sinks, sq_s, sk_s, sv_s,
                                         cache_swa_k[layer].reshape(bs, WINDOW, SWA_KV),
                                         cache_swa_v[layer].reshape(bs, WINDOW, SWA_KV), bias_s, bs, ls)
    y_sample, s_fbuf = post(o_dn_s, o_swa_s, x_sample, mod_s, 32, ls, state=state_ffn_conv[layer])

    return (y_prompt, y_sample, p_dn_conv[None], s_dn_conv[None], p_ssm[None], s_ssm[None],
            p_swa_k[None], s_k.reshape(bs, WINDOW, SWA_KV_HEADS, SWA_HD)[None],
            p_swa_v[None], s_v.reshape(bs, WINDOW, SWA_KV_HEADS, SWA_HD)[None],
            p_fbuf[None], s_fbuf[None])
```

```python
import functools
import math

import numpy as np
import jax
import jax.numpy as jnp
from jax import lax
from jax.experimental import pallas as pl
from jax.experimental.pallas import tpu as pltpu

F32 = jnp.float32
BF16 = jnp.bfloat16

D_MODEL = 1024
PAST_LEN = 16384
DN_HEADS = 4
DN_DK = 128
DN_DV = 128
DN_CONV = 4
SWA_HEADS = 8
SWA_KV_HEADS = 2
SWA_GROUP = SWA_HEADS // SWA_KV_HEADS
SWA_HD = 64
WINDOW = 128
N_BUCKETS = 32
MAX_DISTANCE = 128
D_FF = 2816
FFN_CONV = 3
EPS = 1e-6
NEG_INF = -1e30

DN_QK = DN_HEADS * DN_DK
DN_V = DN_HEADS * DN_DV
DN_CONV_CH = 2 * DN_QK + DN_V
SWA_Q = SWA_HEADS * SWA_HD
SWA_KV = SWA_KV_HEADS * SWA_HD
BA_PAD = 128
SUBLANES = 8
UNIT = 128
POST_SUB_TILES = 2
POST_LOOKAHEAD_CHUNKS = 3
FFN_CHUNK = 256
ROW_PITCH_PAD = 128
VMEM_LIMIT = 56 * 1024 * 1024
RESIDENT = pl.Buffered(1)


def _cparams(sem):
    return pltpu.CompilerParams(dimension_semantics=sem, vmem_limit_bytes=VMEM_LIMIT)


def _bf(x):
    return x.astype(BF16)


def _dot(a, b):
    return jnp.dot(_bf(a), _bf(b), preferred_element_type=F32)


def _dot_nt(a, b):
    return lax.dot_general(_bf(a), _bf(b), (((1,), (1,)), ((), ())), preferred_element_type=F32)


def _dot_mask(m, x):
    hi = _bf(x)
    r = x - hi.astype(F32)
    mid = _bf(r)
    lo = _bf(r - mid.astype(F32))
    d = functools.partial(jnp.dot, preferred_element_type=F32)
    return d(m, hi) + (d(m, mid) + d(m, lo))


def _sigmoid(x):
    return 1.0 / (1.0 + jnp.exp(-x))


def _silu(x):
    return x * _sigmoid(x)


def _softplus(x):
    return jnp.maximum(x, 0.0) + jnp.log1p(jnp.exp(-jnp.abs(x)))


def _rms(x, w):
    ms = jnp.mean(x * x, axis=-1, keepdims=True)
    return x * lax.rsqrt(ms + EPS) * w


def _l2norm(t):
    return t * lax.rsqrt(jnp.sum(t * t, axis=-1, keepdims=True) + EPS)


def _rows(m3, nb, lt):
    return jnp.broadcast_to(m3, (nb, lt, m3.shape[-1])).reshape(nb * lt, m3.shape[-1])


def _causal_conv(x, prev, w, nb, lt):
    width = w.shape[0]
    rows, ch = x.shape
    if nb == 1 and lt > SUBLANES:
        tiles = jnp.concatenate([prev, x.reshape(lt // SUBLANES, SUBLANES, ch)], axis=0)
        sub = lax.broadcasted_iota(jnp.int32, (1, SUBLANES, 1), 1)
        out = tiles[1:] * w[width - 1:width, :]
        for j in range(1, width):
            rot = pltpu.roll(tiles, j, axis=1)
            out = out + jnp.where(sub >= j, rot[1:], rot[:-1]) * w[width - 1 - j:width - j, :]
        return out.reshape(rows, ch)
    tmod = lax.broadcasted_iota(jnp.int32, (rows, 1), 0) & (lt - 1)
    out = x * w[width - 1:width, :]
    for j in range(1, width):
        sh = pltpu.roll(x, j, axis=0)
        for t in range(j):
            p = width - 1 - j + t
            sh = jnp.where(tmod == t, _rows(prev[:, p:p + 1, :], nb, lt), sh)
        out = out + sh * w[width - 1 - j:width - j, :]
    return out


def _last_rows(x, nb, lt, n_state):
    ch = x.shape[-1]
    if nb == 1 and lt > SUBLANES:
        return x[lt - SUBLANES:].reshape(1, SUBLANES, ch)
    return x.reshape(nb, lt, ch)[:, lt - n_state:, :]


def _state_rows(nb, lt, n_state):
    return SUBLANES if (nb == 1 and lt > SUBLANES) else n_state


MOD_PARTS = 6


def _mod_kernel(c_ref, w_ref, b_ref, os_ref, op_ref, *, n_s, n_p):
    part = pl.program_id(0)
    res = _dot(_silu(c_ref[...]), w_ref[...]) + b_ref[...]
    for k in range(MOD_PARTS):
        @pl.when(part == k)
        def _():
            os_ref[:, k, :] = res[:n_s]
            op_ref[:, k, :] = res[n_s:n_s + n_p]


def _mod_call(c_all, w_ada, b_ada, n_s, n_p):
    rows = c_all.shape[0]
    return pl.pallas_call(
        functools.partial(_mod_kernel, n_s=n_s, n_p=n_p),
        grid=(MOD_PARTS,),
        in_specs=[pl.BlockSpec((rows, D_MODEL), lambda k: (0, 0)),
                  pl.BlockSpec((D_MODEL, D_MODEL), lambda k: (0, k)),
                  pl.BlockSpec((1, D_MODEL), lambda k: (0, k))],
        out_specs=[pl.BlockSpec((n_s, MOD_PARTS, D_MODEL), lambda k: (0, 0, 0)),
                   pl.BlockSpec((n_p, MOD_PARTS, D_MODEL), lambda k: (0, 0, 0))],
        out_shape=[jax.ShapeDtypeStruct((n_s, MOD_PARTS, D_MODEL), F32),
                   jax.ShapeDtypeStruct((n_p, MOD_PARTS, D_MODEL), F32)],
        compiler_params=_cparams(("arbitrary",)),
        name="adaln_mod",
    )(c_all, w_ada, b_ada)


IN_SPLIT = (DN_CONV_CH, DN_V, BA_PAD, SWA_Q, SWA_KV, SWA_KV)


IN_SUB_TILES = 4
IN_LOOKAHEAD_SLABS = 1
W_PREP_ROWS = 256
IN_SLAB = 2 * DN_DK


def _in_kernel(*refs, nb, lt, carry, n_sub):
    if carry:
        (x_ref, mod_ref, nw_ref, w_ref, cw_ref,
         qkv_ref, z_ref, ba_ref, sq_ref, sk_ref, sv_ref, tail_ref, h_scr, wdn_ref, wba_ref, wsw_ref) = refs
        prev_ref = tail_ref

        @pl.when(pl.program_id(1) == 0)
        def _():
            tail_ref[...] = jnp.zeros_like(tail_ref)
    else:
        (x_ref, mod_ref, nw_ref, w_ref, cw_ref, prev_ref,
         qkv_ref, z_ref, ba_ref, sq_ref, sk_ref, sv_ref, tail_ref, h_scr, wdn_ref, wba_ref, wsw_ref) = refs

    @pl.when((pl.program_id(0) == 0) & (pl.program_id(1) == 0))
    def _():
        ba_lo = DN_CONV_CH + DN_V
        n_ba = 2 * DN_HEADS
        for r in range(0, ba_lo, W_PREP_ROWS):
            wdn_ref[:, r:r + W_PREP_ROWS] = _bf(w_ref[r:r + W_PREP_ROWS, :].T)
        row = lax.broadcasted_iota(jnp.int32, (BA_PAD, 1), 0)
        wba_ref[...] = _bf(jnp.where(row < n_ba, w_ref[ba_lo:ba_lo + BA_PAD, :], 0.0).T)
        for r in range(0, SWA_Q + 2 * SWA_KV, W_PREP_ROWS):
            wsw_ref[:, r:r + W_PREP_ROWS] = _bf(w_ref[ba_lo + n_ba + r:ba_lo + n_ba + r + W_PREP_ROWS, :].T)

    rows = nb * lt

    def norm(sub):
        x = x_ref[:, sub * lt:(sub + 1) * lt, :]
        ms = jnp.mean(x * x, axis=-1, keepdims=True)
        y = x * lax.rsqrt(ms + EPS) * nw_ref[...]
        h = y * (1.0 + mod_ref[:, 1:2, :]) + mod_ref[:, 0:1, :]
        h_scr[sub, :, :D_MODEL] = _bf(h.reshape(rows, D_MODEL))

    def proj(sub, w_bf_ref, lo, n):
        return jnp.dot(h_scr[sub, :, :D_MODEL], w_bf_ref[:, lo:lo + n], preferred_element_type=F32)

    def plain_proj(sub, i):
        rr = slice(sub * rows, (sub + 1) * rows)
        if i < 2:
            z_ref[rr, i * IN_SLAB:(i + 1) * IN_SLAB] = proj(sub, wdn_ref, DN_CONV_CH + i * IN_SLAB, IN_SLAB)
        elif i < 4:
            sq_ref[rr, (i - 2) * IN_SLAB:(i - 1) * IN_SLAB] = proj(sub, wsw_ref, (i - 2) * IN_SLAB, IN_SLAB)
        elif i == 4:
            kv = proj(sub, wsw_ref, SWA_Q, 2 * SWA_KV)
            sk_ref[rr, :] = kv[:, :SWA_KV]
            sv_ref[rr, :] = kv[:, SWA_KV:]
        else:
            ba_ref[rr, :] = proj(sub, wba_ref, 0, BA_PAD)

    def conv_slab(sub, slab, raw):
        rr = slice(sub * rows, (sub + 1) * rows)
        cols = slice(slab * IN_SLAB, (slab + 1) * IN_SLAB)
        c = _silu(_causal_conv(raw, prev_ref[:, :, cols], cw_ref[:, cols], nb, lt))
        tail_ref[:, :, cols] = _last_rows(raw, nb, lt, DN_CONV - 1)
        if slab * IN_SLAB < 2 * DN_QK:
            scale = DN_DK ** -0.5 if slab * IN_SLAB < DN_QK else 1.0
            c = jnp.concatenate([_l2norm(c[:, i * DN_DK:(i + 1) * DN_DK]) * scale
                                 for i in range(IN_SLAB // DN_DK)], axis=1)
        qkv_ref[rr, cols] = c

    n_slabs = DN_CONV_CH // IN_SLAB
    norm(0)
    for sub in range(n_sub):
        raw_next = proj(sub, wdn_ref, 0, IN_SLAB)
        for slab in range(n_slabs):
            raw = raw_next
            plain_proj(sub, slab)
            if slab + 1 < n_slabs:
                raw_next = proj(sub, wdn_ref, (slab + 1) * IN_SLAB, IN_SLAB)
            if sub + 1 < n_sub and slab == n_slabs - 1 - IN_LOOKAHEAD_SLABS:
                norm(sub + 1)
            conv_slab(sub, slab, raw)


def _in_call(x, mod3, norm_w, w_in_t, conv_w, nb, lt, state=None):
    carry = state is None
    n_sub = IN_SUB_TILES if carry else 1
    bsz, seq, _ = x.shape
    nt = seq // (lt * n_sub)
    rows = nb * lt
    tile = rows * n_sub
    n_tok = bsz * seq
    n_state = _state_rows(nb, lt, DN_CONV - 1)
    row_map = lambda i, j: (i * nt + j, 0)
    const = lambda i, j: (0, 0)
    per_seq = lambda i, j: (i, 0, 0)
    in_specs = [pl.BlockSpec((nb, lt * n_sub, D_MODEL), lambda i, j: (i, j, 0)),
                pl.BlockSpec((nb, 6, D_MODEL), per_seq),
                pl.BlockSpec((1, D_MODEL), const),
                pl.BlockSpec(w_in_t.shape, const, pipeline_mode=RESIDENT),
                pl.BlockSpec((DN_CONV, DN_CONV_CH), const)]
    args = [x, mod3, norm_w, w_in_t, conv_w]
    if not carry:
        in_specs.append(pl.BlockSpec((nb, DN_CONV - 1, DN_CONV_CH), per_seq))
        args.append(state)
    return pl.pallas_call(
        functools.partial(_in_kernel, nb=nb, lt=lt, carry=carry, n_sub=n_sub),
        grid=(bsz // nb, nt),
        in_specs=in_specs,
        out_specs=[pl.BlockSpec((tile, n), row_map) for n in IN_SPLIT]
        + [pl.BlockSpec((nb, n_state, DN_CONV_CH), per_seq)],
        out_shape=[jax.ShapeDtypeStruct((n_tok, n), F32) for n in IN_SPLIT]
        + [jax.ShapeDtypeStruct((bsz, n_state, DN_CONV_CH), F32)],
        scratch_shapes=[pltpu.VMEM((n_sub, rows, D_MODEL + ROW_PITCH_PAD), BF16),
                        pltpu.VMEM((D_MODEL, DN_CONV_CH + DN_V + ROW_PITCH_PAD), BF16),
                        pltpu.VMEM((D_MODEL, BA_PAD), BF16),
                        pltpu.VMEM((D_MODEL, SWA_Q + 2 * SWA_KV), BF16)],
        compiler_params=_cparams(("arbitrary", "arbitrary")),
        name="norm_in_proj_carry" if carry else "norm_in_proj_state",
    )(*args)


INV_BASE_SHIFT = 2
DN_UNITS_CARRY = 4
SWA_BLOCKS_PER_STEP = 8
DN_UNITS_STATE = 1


def _unit_lower_inverses(a_mats, ri, ci, chunk_shift):
    def blocks(s):
        return (ri >> s) == (ci >> s)

    base = min(INV_BASE_SHIFT, chunk_shift)
    xs = [jnp.where(blocks(base), -a, 0.0) for a in a_mats]
    ts = [jnp.where(ri == ci, 1.0, n) for n in xs]
    for lvl in range(1, base):
        xs = [_dot(x, x) for x in xs]
        ts = [t + _dot(t, x) for t, x in zip(ts, xs)]
    for s in range(base, chunk_shift):
        sel = blocks(s + 1) & jnp.logical_not(blocks(s))
        ets = [_dot(jnp.where(sel, a, 0.0), t) for a, t in zip(a_mats, ts)]
        ts = [t - _dot(t, et) for t, et in zip(ts, ets)]
    return ts


def _dn_kernel(*refs, nb, chunk, carry, units):
    if carry:
        (qkv_ref, z_ref, ba_ref, alog_ref, dt_ref, nw_ref, o_ref, s_ref) = refs
        s0_ref = s_ref

        @pl.when(pl.program_id(1) == 0)
        def _():
            s_ref[...] = jnp.zeros_like(s_ref)
    else:
        (qkv_ref, z_ref, ba_ref, alog_ref, dt_ref, nw_ref, s0_ref, o_ref, s_ref) = refs

    ri = lax.broadcasted_iota(jnp.int32, (UNIT, UNIT), 0)
    ci = lax.broadcasted_iota(jnp.int32, (UNIT, UNIT), 1)
    shift = int(math.log2(chunk))
    same = (ri >> shift) == (ci >> shift)
    incl = same & (ri >= ci)
    strict = same & (ri > ci)

    ba = ba_ref[...]
    beta_full = _sigmoid(ba)
    g_full = -jnp.exp(alog_ref[...]) * _softplus(ba + dt_ref[...])
    masks = jnp.concatenate([jnp.where(incl, 1.0, 0.0), jnp.where(same, 1.0, 0.0)], axis=0).astype(BF16)
    gsums = [_dot_mask(masks, g_full[u * UNIT:(u + 1) * UNIT]) for u in range(units)]
    g_cum = [g[:UNIT] for g in gsums]
    g_tot = [g[UNIT:] for g in gsums]
    g_cum_t = [g.T for g in g_cum]

    probs = [(u, h) for u in range(units) for h in range(DN_HEADS)]
    n_p = len(probs)
    rs = lambda u: slice(u * UNIT, (u + 1) * UNIT)

    def head_cols(base):
        return [qkv_ref[rs(u), base + h * DN_DK:base + (h + 1) * DN_DK] for u, h in probs]

    q = head_cols(0)
    k = head_cols(DN_QK)
    v = head_cols(2 * DN_QK)
    gc = [g_cum[u][:, DN_HEADS + h:DN_HEADS + h + 1] for u, h in probs]
    gr = [g_cum_t[u][DN_HEADS + h:DN_HEADS + h + 1, :] for u, h in probs]
    gt = [g_tot[u][:, DN_HEADS + h:DN_HEADS + h + 1] for u, h in probs]
    bc = [beta_full[rs(u), h:h + 1] for u, h in probs]
    decay = [jnp.where(incl, jnp.exp(jnp.where(incl, gc[p] - gr[p], 0.0)), 0.0) for p in range(n_p)]
    e_g = [jnp.exp(gc[p]) for p in range(n_p)]
    kq = [_dot_nt(jnp.concatenate([k[p], q[p]], axis=0), k[p]) for p in range(n_p)]
    qk = [kq[p][UNIT:] * decay[p] for p in range(n_p)]
    a_mats = [jnp.where(strict, bc[p] * kq[p][:UNIT] * decay[p], 0.0) for p in range(n_p)]
    t_inv = _unit_lower_inverses(a_mats, ri, ci, shift)
    wvk = [_dot(t_inv[p], jnp.concatenate([v[p] * bc[p], k[p] * (bc[p] * e_g[p])], axis=1)) for p in range(n_p)]
    w_v = [w[:, :DN_DV] for w in wvk]
    w_k = [w[:, DN_DV:] for w in wvk]
    q_dec = [q[p] * e_g[p] for p in range(n_p)]
    k_tail_t = [(k[p] * jnp.exp(gt[p] - gc[p])).T for p in range(n_p)]
    c_dec = [jnp.exp(gt[p]) for p in range(n_p)]

    outs = {}
    if carry:
        state = [s_ref[0, h] for h in range(DN_HEADS)]
        for u in range(units):
            ps = [u * DN_HEADS + h for h in range(DN_HEADS)]
            r = [_dot(jnp.concatenate([w_k[p], q_dec[p]], axis=0), state[h]) for h, p in enumerate(ps)]
            uu = [w_v[p] - r[h][:UNIT] for h, p in enumerate(ps)]
            for h, p in enumerate(ps):
                outs[p] = r[h][UNIT:] + _dot(qk[p], uu[h])
            state = [state[h] * c_dec[p][0:1, :] + _dot(k_tail_t[p], uu[h]) for h, p in enumerate(ps)]
        for h in range(DN_HEADS):
            s_ref[0, h] = state[h]
    else:
        per_unit = nb // units
        for p, (u, h) in enumerate(probs):
            us, qs = [], []
            for s in range(per_unit):
                lo = s * chunk
                lhs = jnp.concatenate([w_k[p][lo:lo + chunk], q_dec[p][lo:lo + chunk]], axis=0)
                r = jnp.dot(lhs, s0_ref[u * per_unit + s, h], preferred_element_type=F32)
                us.append(w_v[p][lo:lo + chunk] - r[:chunk])
                qs.append(r[chunk:])
            uu = jnp.concatenate(us, axis=0)
            outs[p] = jnp.concatenate(qs, axis=0) + _dot(qk[p], uu)
            for s in range(per_unit):
                lo = s * chunk
                upd = jnp.dot(k_tail_t[p][:, lo:lo + chunk], uu[lo:lo + chunk], preferred_element_type=F32)
                s_ref[u * per_unit + s, h] = s0_ref[u * per_unit + s, h] * c_dec[p][lo:lo + 1, :] + upd

    for p, (u, h) in enumerate(probs):
        zz = z_ref[rs(u), h * DN_DV:(h + 1) * DN_DV]
        o_ref[rs(u), h * DN_DV:(h + 1) * DN_DV] = _rms(outs[p], nw_ref[...]) * _silu(zz)


def _dn_call(qkv, z, ba, alog_row, dt_row, norm_w, bsz, seq, state=None):
    carry = state is None
    units = DN_UNITS_CARRY if carry else DN_UNITS_STATE
    rows = units * UNIT
    nb = 1 if carry else rows // seq
    chunk = UNIT if carry else seq
    nt = seq * nb // rows
    n_tok = bsz * seq
    row_map = lambda i, j: (i * nt + j, 0)
    const = lambda i, j: (0, 0)
    in_specs = [pl.BlockSpec((rows, DN_CONV_CH), row_map),
                pl.BlockSpec((rows, DN_V), row_map),
                pl.BlockSpec((rows, BA_PAD), row_map),
                pl.BlockSpec((1, BA_PAD), const),
                pl.BlockSpec((1, BA_PAD), const),
                pl.BlockSpec((1, DN_DV), const)]
    args = [qkv, z, ba, alog_row, dt_row, norm_w]
    if not carry:
        in_specs.append(pl.BlockSpec((nb, DN_HEADS, DN_DK, DN_DV), lambda i, j: (i, 0, 0, 0)))
        args.append(state)
    return pl.pallas_call(
        functools.partial(_dn_kernel, nb=nb, chunk=chunk, carry=carry, units=units),
        grid=(bsz // nb, nt),
        in_specs=in_specs,
        out_specs=[pl.BlockSpec((rows, DN_V), row_map),
                   pl.BlockSpec((nb, DN_HEADS, DN_DK, DN_DV), lambda i, j: (i, 0, 0, 0))],
        out_shape=[jax.ShapeDtypeStruct((n_tok, DN_V), F32),
                   jax.ShapeDtypeStruct((bsz, DN_HEADS, DN_DK, DN_DV), F32)],
        compiler_params=_cparams(("arbitrary", "arbitrary")),
        name="gated_deltanet_carry" if carry else "gated_deltanet_state",
    )(*args)


MIX_SUB_TILES = 2
MIX_ROWS = MIX_SUB_TILES * 2 * UNIT


def _mix_kernel(x_ref, mod_ref, nw_ref, w_ref, cw_ref, alog_ref, dt_ref, dnw_ref,
                sq_ref, sk_ref, sv_ref, tail_ref, o_ref, s_ref,
                h_scr, wdn_ref, wba_ref, wsw_ref, qkv_scr, z_scr, ba_scr):
    _in_kernel(x_ref, mod_ref, nw_ref, w_ref, cw_ref,
               qkv_scr, z_scr, ba_scr, sq_ref, sk_ref, sv_ref, tail_ref, h_scr, wdn_ref, wba_ref, wsw_ref,
               nb=1, lt=2 * UNIT, carry=True, n_sub=MIX_SUB_TILES)
    _dn_kernel(qkv_scr, z_scr, ba_scr, alog_ref, dt_ref, dnw_ref, o_ref, s_ref,
               nb=1, chunk=UNIT, carry=True, units=MIX_ROWS // UNIT)


def _mix_call(x, mod3, norm_w, w_in_t, conv_w, alog_row, dt_row, dn_norm_w):
    bsz, seq, _ = x.shape
    nt = seq // MIX_ROWS
    n_tok = bsz * seq
    row_map = lambda i, j: (i * nt + j, 0)
    const = lambda i, j: (0, 0)
    per_seq = lambda i, j: (i, 0, 0)
    swa_cols = (SWA_Q, SWA_KV, SWA_KV)
    return pl.pallas_call(
        _mix_kernel,
        grid=(bsz, nt),
        in_specs=[pl.BlockSpec((1, MIX_ROWS, D_MODEL), lambda i, j: (i, j, 0)),
                  pl.BlockSpec((1, 6, D_MODEL), per_seq),
                  pl.BlockSpec((1, D_MODEL), const),
                  pl.BlockSpec(w_in_t.shape, const, pipeline_mode=RESIDENT),
                  pl.BlockSpec((DN_CONV, DN_CONV_CH), const),
                  pl.BlockSpec((1, BA_PAD), const),
                  pl.BlockSpec((1, BA_PAD), const),
                  pl.BlockSpec((1, DN_DV), const)],
        out_specs=[pl.BlockSpec((MIX_ROWS, n), row_map) for n in swa_cols]
        + [pl.BlockSpec((1, SUBLANES, DN_CONV_CH), per_seq),
           pl.BlockSpec((MIX_ROWS, DN_V), row_map),
           pl.BlockSpec((1, DN_HEADS, DN_DK, DN_DV), lambda i, j: (i, 0, 0, 0))],
        out_shape=[jax.ShapeDtypeStruct((n_tok, n), F32) for n in swa_cols]
        + [jax.ShapeDtypeStruct((bsz, SUBLANES, DN_CONV_CH), F32),
           jax.ShapeDtypeStruct((n_tok, DN_V), F32),
           jax.ShapeDtypeStruct((bsz, DN_HEADS, DN_DK, DN_DV), F32)],
        scratch_shapes=[pltpu.VMEM((MIX_SUB_TILES, 2 * UNIT, D_MODEL + ROW_PITCH_PAD), BF16),
                        pltpu.VMEM((D_MODEL, DN_CONV_CH + DN_V + ROW_PITCH_PAD), BF16),
                        pltpu.VMEM((D_MODEL, BA_PAD), BF16),
                        pltpu.VMEM((D_MODEL, SWA_Q + 2 * SWA_KV), BF16),
                        pltpu.VMEM((MIX_ROWS, DN_CONV_CH), F32),
                        pltpu.VMEM((MIX_ROWS, DN_V + ROW_PITCH_PAD), F32),
                        pltpu.VMEM((MIX_ROWS, BA_PAD), F32)],
        compiler_params=_cparams(("arbitrary", "arbitrary")),
        name="in_proj_deltanet_carry",
    )(x, mod3, norm_w, w_in_t, conv_w, alog_row, dt_row, dn_norm_w)


def _bucket_table():
    i = np.arange(WINDOW, dtype=np.int64)[:, None]
    j = np.arange(2 * WINDOW, dtype=np.int64)[None, :]
    d = np.maximum(i + WINDOW - j, 0)
    exact = N_BUCKETS // 2
    logv = (np.log(np.maximum(d, 1).astype(np.float32) / np.float32(exact)).astype(np.float32)
            / np.float32(math.log(MAX_DISTANCE / exact)))
    large = np.minimum(exact + (logv * np.float32(N_BUCKETS - exact)).astype(np.int32), N_BUCKETS - 1)
    return np.where(d < exact, d, large).astype(np.int32)


def _bias_lookup(rb_ref, bucket, head):
    acc = jnp.zeros(bucket.shape, F32)
    for b in range(N_BUCKETS):
        acc = jnp.where(bucket == b, rb_ref[b, head], acc)
    return acc


def _bias_kernel(rb_ref, bucket_t_ref, bucket_ref, ot_ref, os_ref):
    bucket_t = bucket_t_ref[...]
    kj = lax.broadcasted_iota(jnp.int32, bucket_t.shape, 0)
    qi = lax.broadcasted_iota(jnp.int32, bucket_t.shape, 1)
    dist = qi + WINDOW - kj
    valid = (dist >= 0) & (dist < WINDOW)
    for hk in range(SWA_KV_HEADS):
        for par in range(2):
            for st in range(SWA_GROUP // 2):
                head = hk * SWA_GROUP + 2 * st + par
                gen = jnp.where(valid, _bias_lookup(rb_ref, bucket_t, head), NEG_INF)
                ot_ref[1, hk, par, :, st * WINDOW:(st + 1) * WINDOW] = gen
                ot_ref[0, hk, par, :, st * WINDOW:(st + 1) * WINDOW] = jnp.where(kj >= WINDOW, gen, NEG_INF)
    bucket = bucket_ref[...]
    qi = lax.broadcasted_iota(jnp.int32, bucket.shape, 0)
    kj = lax.broadcasted_iota(jnp.int32, bucket.shape, 1)
    dist = qi + WINDOW - kj
    valid = (dist >= 0) & (dist < WINDOW)
    for head in range(SWA_HEADS):
        os_ref[head] = jnp.where(valid, _bias_lookup(rb_ref, bucket, head), NEG_INF)


BIAS_T_SHAPE = (2, SWA_KV_HEADS, 2, 2 * WINDOW, (SWA_GROUP // 2) * WINDOW)
BIAS_S_SHAPE = (SWA_HEADS, SUBLANES, 2 * WINDOW)


def _bias_call(rel_bias):
    bucket = _bucket_table()
    return pl.pallas_call(
        _bias_kernel,
        in_specs=[pl.BlockSpec(memory_space=pltpu.SMEM),
                  pl.BlockSpec((2 * WINDOW, WINDOW), lambda: (0, 0)),
                  pl.BlockSpec((SUBLANES, 2 * WINDOW), lambda: (0, 0))],
        out_specs=[pl.BlockSpec(BIAS_T_SHAPE, lambda: (0,) * len(BIAS_T_SHAPE)),
                   pl.BlockSpec(BIAS_S_SHAPE, lambda: (0,) * len(BIAS_S_SHAPE))],
        out_shape=[jax.ShapeDtypeStruct(BIAS_T_SHAPE, F32), jax.ShapeDtypeStruct(BIAS_S_SHAPE, F32)],
        name="swa_rel_bias_table",
    )(rel_bias, jnp.asarray(np.ascontiguousarray(bucket.T)), jnp.asarray(bucket[:SUBLANES]))


def _softmax_sink_parts(s, sink):
    m = jnp.maximum(jnp.max(s, axis=0, keepdims=True), sink)
    p = jnp.exp(s - m)
    return p, 1.0 / (jnp.sum(p, axis=0, keepdims=True) + jnp.exp(sink - m))


def _half_lane_variants(full, hk, lo_half):
    rolled = pltpu.roll(full, SWA_HD, axis=1)
    low_src, high_src = (full, rolled) if hk == 0 else (rolled, full)
    return jnp.where(lo_half, low_src, 0.0), jnp.where(lo_half, 0.0, high_src)


def _swa_prompt_kernel(sink_ref, q_ref, kp_ref, kc_ref, vp_ref, vc_ref, bias_ref, o_ref, *, n_blk):
    step = pl.program_id(1)
    lo_half = lax.broadcasted_iota(jnp.int32, (1, 2 * SWA_HD), 1) < SWA_HD
    lo_rows = lax.broadcasted_iota(jnp.int32, (2 * SWA_HD, 1), 0) < SWA_HD
    q = _bf(q_ref[...] * (SWA_HD ** -0.5))
    keys = jnp.concatenate([kp_ref[...], kc_ref[...]], axis=0)
    vals = jnp.concatenate([vp_ref[...], vc_ref[...]], axis=0)
    k_var = [[_bf(t) for t in _half_lane_variants(keys, hk, lo_half)] for hk in range(SWA_KV_HEADS)]
    v_var_t = [[_bf(t.T) for t in _half_lane_variants(vals, hk, lo_half)] for hk in range(SWA_KV_HEADS)]
    n_stack = SWA_GROUP // 2
    sinks = [[jnp.concatenate([jnp.full((1, WINDOW), sink_ref[hk * SWA_GROUP + 2 * st + par], F32)
                               for st in range(n_stack)], axis=1) for par in range(2)]
             for hk in range(SWA_KV_HEADS)]
    nt_dims = (((1,), (1,)), ((), ()))

    def scores(b):
        rows = slice(b * WINDOW, (b + 1) * WINDOW)
        win = slice(b * WINDOW, (b + 2) * WINDOW)
        variant = jnp.where(step == 0, 0, 1) if b == 0 else 1
        out = []
        for hk in range(SWA_KV_HEADS):
            q2 = jnp.concatenate([q[rows, (hk * n_stack + st) * 2 * SWA_HD:(hk * n_stack + st + 1) * 2 * SWA_HD]
                                  for st in range(n_stack)], axis=0)
            out.append([lax.dot_general(k_var[hk][par][win], q2, nt_dims, preferred_element_type=F32)
                        + bias_ref[variant, hk, par] for par in range(2)])
        return out

    s_next = scores(0)
    for b in range(n_blk):
        s_cur = s_next
        if b + 1 < n_blk:
            s_next = scores(b + 1)
        rows = slice(b * WINDOW, (b + 1) * WINDOW)
        win = slice(b * WINDOW, (b + 2) * WINDOW)
        for hk in range(SWA_KV_HEADS):
            parts = [_softmax_sink_parts(s_cur[hk][par], sinks[hk][par]) for par in range(2)]
            o_t = (jnp.dot(v_var_t[hk][0][:, win], _bf(parts[0][0]), preferred_element_type=F32)
                   + jnp.dot(v_var_t[hk][1][:, win], _bf(parts[1][0]), preferred_element_type=F32))
            o_t = o_t * jnp.where(lo_rows, parts[0][1], parts[1][1])
            for st in range(n_stack):
                lo = (hk * n_stack + st) * 2 * SWA_HD
                o_ref[rows, lo:lo + 2 * SWA_HD] = o_t[:, st * WINDOW:(st + 1) * WINDOW].T


def _swa_prompt_call(sinks, sq, sk, sv, bias, bsz, seq):
    n_blk = SWA_BLOCKS_PER_STEP
    tile = n_blk * WINDOW
    nt = seq // tile
    cur = lambda b, i: (b * nt + i, 0)
    prv = lambda b, i: (b * nt * n_blk + jnp.maximum(i * n_blk - 1, 0), 0)
    return pl.pallas_call(
        functools.partial(_swa_prompt_kernel, n_blk=n_blk),
        grid=(bsz, nt),
        in_specs=[pl.BlockSpec(memory_space=pltpu.SMEM),
                  pl.BlockSpec((tile, SWA_Q), cur),
                  pl.BlockSpec((WINDOW, SWA_KV), prv),
                  pl.BlockSpec((tile, SWA_KV), cur),
                  pl.BlockSpec((WINDOW, SWA_KV), prv),
                  pl.BlockSpec((tile, SWA_KV), cur),
                  pl.BlockSpec(BIAS_T_SHAPE, lambda b, i: (0,) * len(BIAS_T_SHAPE))],
        out_specs=pl.BlockSpec((tile, SWA_Q), cur),
        out_shape=jax.ShapeDtypeStruct((bsz * seq, SWA_Q), F32),
        compiler_params=_cparams(("arbitrary", "arbitrary")),
        name="swa_banded",
    )(sinks, sq, sk, sk, sv, sv, bias)


def _swa_sample_kernel(sink_ref, q_ref, kn_ref, vn_ref, kc_ref, vc_ref, bias_ref, o_ref, ko_ref, vo_ref, *, nb, lt):
    lo_half = lax.broadcasted_iota(jnp.int32, (1, 2 * SWA_HD), 1) < SWA_HD
    q_all = q_ref[...] * (SWA_HD ** -0.5)
    kn_all = kn_ref[...]
    vn_all = vn_ref[...]
    pad = jnp.zeros((WINDOW - lt, SWA_KV), F32)
    bias = jnp.concatenate([bias_ref[h, 0:lt, :] for h in range(SWA_HEADS)], axis=0)
    sink = jnp.concatenate([jnp.full((lt, 1), sink_ref[h], F32) for h in range(SWA_HEADS)], axis=0)
    nt_dims = (((1,), (1,)), ((), ()))

    def to_half(tile, src_half, dst_half):
        return tile if src_half == dst_half else pltpu.roll(tile, SWA_HD, axis=1)

    lhs, keys, vals = [], [], []
    for s in range(nb):
        rows = slice(s * lt, (s + 1) * lt)
        kc, vc, kn, vn = kc_ref[s], vc_ref[s], kn_all[rows], vn_all[rows]
        ko_ref[s] = jnp.concatenate([kc[lt:], kn], axis=0)
        vo_ref[s] = jnp.concatenate([vc[lt:], vn], axis=0)
        keys.append(_bf(jnp.concatenate([kc, kn, pad], axis=0)))
        vals.append(_bf(jnp.concatenate([vc, vn, pad], axis=0)))
        q = q_all[rows]
        tiles = []
        for h in range(SWA_HEADS):
            hk = h // SWA_GROUP
            t = to_half(q[:, (h // 2) * 2 * SWA_HD:(h // 2 + 1) * 2 * SWA_HD], h % 2, hk)
            tiles.append(jnp.where(lo_half if hk == 0 else jnp.logical_not(lo_half), t, 0.0))
        lhs.append(_bf(jnp.concatenate(tiles, axis=0)))
    scores = [lax.dot_general(lhs[s], keys[s], nt_dims, preferred_element_type=F32) + bias for s in range(nb)]
    m = [jnp.maximum(jnp.max(sc, axis=-1, keepdims=True), sink) for sc in scores]
    p = [jnp.exp(sc - mm) for sc, mm in zip(scores, m)]
    rinv = [1.0 / (jnp.sum(pp, axis=-1, keepdims=True) + jnp.exp(sink - mm)) for pp, mm in zip(p, m)]
    res = [jnp.dot(_bf(p[s]), vals[s], preferred_element_type=F32) * rinv[s] for s in range(nb)]
    rows_out = []
    for s in range(nb):
        tiles = []
        for pair in range(SWA_HEADS // 2):
            hk = (2 * pair) // SWA_GROUP
            low = to_half(res[s][(2 * pair) * lt:(2 * pair + 1) * lt], hk, 0)
            high = to_half(res[s][(2 * pair + 1) * lt:(2 * pair + 2) * lt], hk, 1)
            tiles.append(jnp.where(lo_half, low, high))
        rows_out.append(jnp.concatenate(tiles, axis=1))
    o_ref[...] = jnp.concatenate(rows_out, axis=0)


def _swa_sample_call(sinks, sq, sk, sv, cache_k, cache_v, bias, bsz, seq):
    nb = UNIT // seq
    rows = lambda i: (i, 0)
    seqs = lambda i: (i, 0, 0)
    return pl.pallas_call(
        functools.partial(_swa_sample_kernel, nb=nb, lt=seq),
        grid=(bsz // nb,),
        in_specs=[pl.BlockSpec(memory_space=pltpu.SMEM),
                  pl.BlockSpec((UNIT, SWA_Q), rows),
                  pl.BlockSpec((UNIT, SWA_KV), rows),
                  pl.BlockSpec((UNIT, SWA_KV), rows),
                  pl.BlockSpec((nb, WINDOW, SWA_KV), seqs),
                  pl.BlockSpec((nb, WINDOW, SWA_KV), seqs),
                  pl.BlockSpec(BIAS_S_SHAPE, lambda i: (0,) * len(BIAS_S_SHAPE))],
        out_specs=[pl.BlockSpec((UNIT, SWA_Q), rows),
                   pl.BlockSpec((nb, WINDOW, SWA_KV), seqs),
                   pl.BlockSpec((nb, WINDOW, SWA_KV), seqs)],
        out_shape=[jax.ShapeDtypeStruct((bsz * seq, SWA_Q), F32),
                   jax.ShapeDtypeStruct((bsz, WINDOW, SWA_KV), F32),
                   jax.ShapeDtypeStruct((bsz, WINDOW, SWA_KV), F32)],
        compiler_params=_cparams(("arbitrary",)),
        name="swa_cached",
    )(sinks, sq, sk, sv, cache_k, cache_v, bias)


def _post_kernel(*refs, nb, lt, carry, n_sub):
    if carry:
        (odn_ref, oswa_ref, x_ref, mod_ref, wout_ref, nfw_ref, wup_ref, cw_ref, cb_ref, wdn_ref, fnw_ref,
         y_ref, fbuf_ref, h_scr, x1_scr, acc_scr) = refs
        prev_ref = fbuf_ref

        @pl.when(pl.program_id(1) == 0)
        def _():
            fbuf_ref[...] = jnp.zeros_like(fbuf_ref)
    else:
        (odn_ref, oswa_ref, x_ref, mod_ref, wout_ref, nfw_ref, wup_ref, cw_ref, cb_ref, wdn_ref, fnw_ref,
         prev_ref, y_ref, fbuf_ref, h_scr, x1_scr, acc_scr) = refs
    rows = nb * lt
    n_chunks = D_FF // FFN_CHUNK

    def col_slices(c):
        return [slice(base + c * FFN_CHUNK, base + (c + 1) * FFN_CHUNK) for base in (0, D_FF)]

    def prologue(sub):
        rr = slice(sub * rows, (sub + 1) * rows)
        attn = (jnp.dot(_bf(odn_ref[rr, :]), wout_ref[0:DN_V, :], preferred_element_type=F32)
                + jnp.dot(_bf(oswa_ref[rr, :]), wout_ref[DN_V:, :], preferred_element_type=F32))
        x = x_ref[:, sub * lt:(sub + 1) * lt, :].reshape(rows, D_MODEL)
        x1 = x + _rows(mod_ref[:, 2:3, :], nb, lt) * attn
        x1_scr[sub] = x1
        h = (_rms(x1, nfw_ref[...]) * (1.0 + _rows(mod_ref[:, 4:5, :], nb, lt))
             + _rows(mod_ref[:, 3:4, :], nb, lt))
        h_scr[sub] = _bf(h)
        acc_scr[sub] = jnp.zeros((rows, D_MODEL), F32)

    def up_proj(sub, c):
        return [jnp.dot(h_scr[sub], wup_ref[:, cols], preferred_element_type=F32) for cols in col_slices(c)]

    def chunk(sub, c, u_cur):
        halves = []
        for u, cols in zip(u_cur, col_slices(c)):
            prev = prev_ref[:, :, cols]
            halves.append(_causal_conv(u, prev, cw_ref[:, cols], nb, lt) + cb_ref[:, cols])
            fbuf_ref[:, :, cols] = _last_rows(u, nb, lt, FFN_CONV - 1)
        act = _silu(halves[0]) * halves[1]
        acc_scr[sub] += jnp.dot(_bf(act), wdn_ref[c * FFN_CHUNK:(c + 1) * FFN_CHUNK, :],
                                preferred_element_type=F32)

    def epilogue(sub):
        x2 = x1_scr[sub] + _rows(mod_ref[:, 5:6, :], nb, lt) * acc_scr[sub]
        y_ref[:, sub * lt:(sub + 1) * lt, :] = _rms(x2, fnw_ref[...]).reshape(nb, lt, D_MODEL)

    prologue(0)
    for sub in range(n_sub):
        u_next = up_proj(sub, 0)
        for c in range(n_chunks):
            u_cur = u_next
            if c + 1 < n_chunks:
                u_next = up_proj(sub, c + 1)
            chunk(sub, c, u_cur)
            if sub + 1 < n_sub and c == n_chunks - 1 - POST_LOOKAHEAD_CHUNKS:
                prologue(sub + 1)
        epilogue(sub)


def _post_call(o_dn, o_swa, x, mod3, w_out, norm_ffn_w, w_up, conv_w, conv_b, w_down, final_w, nb, lt, state=None):
    carry = state is None
    n_sub = POST_SUB_TILES if carry else 1
    bsz, seq, _ = x.shape
    nt = seq // (lt * n_sub)
    rows = nb * lt
    tile = rows * n_sub
    n_state = _state_rows(nb, lt, FFN_CONV - 1)
    row_map = lambda i, j: (i * nt + j, 0)
    const = lambda i, j: (0, 0)
    in_specs = [pl.BlockSpec((tile, DN_V), row_map),
                pl.BlockSpec((tile, SWA_Q), row_map),
                pl.BlockSpec((nb, lt * n_sub, D_MODEL), lambda i, j: (i, j, 0)),
                pl.BlockSpec((nb, 6, D_MODEL), lambda i, j: (i, 0, 0)),
                pl.BlockSpec((D_MODEL, D_MODEL), const, pipeline_mode=RESIDENT),
                pl.BlockSpec((1, D_MODEL), const),
                pl.BlockSpec(w_up.shape, const, pipeline_mode=RESIDENT),
                pl.BlockSpec((FFN_CONV, 2 * D_FF), const),
                pl.BlockSpec((1, 2 * D_FF), const),
                pl.BlockSpec((D_FF, D_MODEL), const, pipeline_mode=RESIDENT),
                pl.BlockSpec((1, D_MODEL), const)]
    args = [o_dn, o_swa, x, mod3, w_out, norm_ffn_w, w_up, conv_w, conv_b, w_down, final_w]
    if not carry:
        in_specs.append(pl.BlockSpec((nb, FFN_CONV - 1, 2 * D_FF), lambda i, j: (i, 0, 0)))
        args.append(state)
    return pl.pallas_call(
        functools.partial(_post_kernel, nb=nb, lt=lt, carry=carry, n_sub=n_sub),
        grid=(bsz // nb, nt),
        in_specs=in_specs,
        out_specs=[pl.BlockSpec((nb, lt * n_sub, D_MODEL), lambda i, j: (i, j, 0)),
                   pl.BlockSpec((nb, n_state, 2 * D_FF), lambda i, j: (i, 0, 0))],
        out_shape=[jax.ShapeDtypeStruct((bsz, seq, D_MODEL), F32),
                   jax.ShapeDtypeStruct((bsz, n_state, 2 * D_FF), F32)],
        scratch_shapes=[pltpu.VMEM((n_sub, rows, D_MODEL), BF16),
                        pltpu.VMEM((n_sub, rows, D_MODEL), F32),
                        pltpu.VMEM((n_sub, rows, D_MODEL), F32)],
        compiler_params=_cparams(("arbitrary", "arbitrary")),
        name="out_proj_convffn_carry" if carry else "out_proj_convffn_state",
    )(*args)


def _pad_row(vec, offset):
    return jnp.zeros((1, BA_PAD), F32).at[0, offset:offset + vec.shape[0]].set(vec)


def kernel(x_prompt, x_sample, state_dn_conv, state_dn_ssm, cache_swa_k, cache_swa_v, state_ffn_conv, c_prompt, c_sample, rel_bias, final_norm_w, w_ada, b_ada, norm_mix_w, w_in, dn_conv_w, dn_A_log, dn_dt_bias, dn_norm_w, swa_sinks, w_out, norm_ffn_w, ffn_w_up, ffn_conv_w, ffn_conv_b, ffn_w_down):
    bp, lp, _ = x_prompt.shape
    bs, ls, _ = x_sample.shape
    layer = 0

    w_in_t = jnp.transpose(w_in[layer])
    w_out_b = w_out[layer].astype(BF16)
    w_up_b = jnp.pad(ffn_w_up[layer].astype(BF16), ((0, 0), (0, ROW_PITCH_PAD)))
    w_dn_b = ffn_w_down[layer].astype(BF16)
    alog_row = _pad_row(dn_A_log[layer], DN_HEADS)
    dt_row = _pad_row(dn_dt_bias[layer], DN_HEADS)
    row = lambda v: v.reshape(1, -1)

    n_c = bp + bs
    n_c_pad = -(-n_c // SUBLANES) * SUBLANES
    c_all = jnp.pad(jnp.concatenate([c_sample, c_prompt], axis=0), ((0, n_c_pad - n_c), (0, 0)))
    mod_s, mod_p = _mod_call(c_all, w_ada[layer], row(b_ada[layer]), bs, bp)

    bias_t, bias_s = _bias_call(rel_bias)
    sinks = swa_sinks[layer]

    def mixer_in(x, mod3, nb, lt, state=None):
        return _in_call(x, mod3, row(norm_mix_w[layer]), w_in_t, dn_conv_w[layer], nb, lt, state)

    def post(o_dn, o_swa, x, mod3, nb, lt, state=None):
        return _post_call(o_dn, o_swa, x, mod3, w_out_b, row(norm_ffn_w[layer]), w_up_b, ffn_conv_w[layer],
                          row(ffn_conv_b[layer]), w_dn_b, row(final_norm_w), nb, lt, state)

    dn_args = (alog_row, dt_row, row(dn_norm_w[layer]))

    sq, sk, sv, p_tail, o_dn, p_ssm = _mix_call(x_prompt, mod_p, row(norm_mix_w[layer]), w_in_t,
                                                dn_conv_w[layer], *dn_args)
    o_swa = _swa_prompt_call(sinks, sq, sk, sv, bias_t, bp, lp)
    y_prompt, p_ffn_tail = post(o_dn, o_swa, x_prompt, mod_p, 1, 256)
    p_dn_conv = p_tail[:, SUBLANES - (DN_CONV - 1):]
    p_fbuf = p_ffn_tail[:, SUBLANES - (FFN_CONV - 1):]
    last_window = lambda t: t.reshape(bp, lp, SWA_KV)[:, lp - WINDOW:].reshape(bp, WINDOW, SWA_KV_HEADS, SWA_HD)
    p_swa_k = last_window(sk)
    p_swa_v = last_window(sv)

    qkv_s, z_s, ba_s, sq_s, sk_s, sv_s, s_dn_conv = mixer_in(x_sample, mod_s, 64, ls, state=state_dn_conv[layer])
    o_dn_s, s_ssm = _dn_call(qkv_s, z_s, ba_s, *dn_args, bs, ls, state=state_dn_ssm[layer])
    o_swa_s, s_k, s_v = _swa_sample_call(sinks, sq_s, sk_s, sv_s,
                                         cache_swa_k[layer].reshape(bs, WINDOW, SWA_KV),
                                         cache_swa_v[layer].reshape(bs, WINDOW, SWA_KV), bias_s, bs, ls)
    y_sample, s_fbuf = post(o_dn_s, o_swa_s, x_sample, mod_s, 32, ls, state=state_ffn_conv[layer])

    return (y_prompt, y_sample, p_dn_conv[None], s_dn_conv[None], p_ssm[None], s_ssm[None],
            p_swa_k[None], s_k.reshape(bs, WINDOW, SWA_KV_HEADS, SWA_HD)[None],
            p_swa_v[None], s_v.reshape(bs, WINDOW, SWA_KV_HEADS, SWA_HD)[None],
            p_fbuf[None], s_fbuf[None])
```

```python
import functools
import math

import numpy as np
import jax
import jax.numpy as jnp
from jax import lax
from jax.experimental import pallas as pl
from jax.experimental.pallas import tpu as pltpu

F32 = jnp.float32
BF16 = jnp.bfloat16

D_MODEL = 1024
DN_HEADS = 4
DN_DK = 128
DN_DV = 128
DN_CONV = 4
SWA_HEADS = 8
SWA_KV_HEADS = 2
SWA_GROUP = SWA_HEADS // SWA_KV_HEADS
SWA_HD = 64
WINDOW = 128
N_BUCKETS = 32
MAX_DISTANCE = 128
D_FF = 2816
FFN_CONV = 3
EPS = 1e-6
NEG_INF = -1e30
LOG2_E = 1.0 / math.log(2.0)

DN_QK = DN_HEADS * DN_DK
DN_V = DN_HEADS * DN_DV
DN_CONV_CH = 2 * DN_QK + DN_V
SWA_Q = SWA_HEADS * SWA_HD
SWA_KV = SWA_KV_HEADS * SWA_HD
BA_PAD = 128
SUBLANES = 8
UNIT = 128
POST_SUB_TILES = 2
POST_LOOKAHEAD_CHUNKS = 3
FFN_CHUNK = 256
ROW_PITCH_PAD = 128
VMEM_LIMIT = 56 * 1024 * 1024
RESIDENT = pl.Buffered(1)


def _cparams(sem):
    return pltpu.CompilerParams(dimension_semantics=sem, vmem_limit_bytes=VMEM_LIMIT)


def _bf(x):
    return x.astype(BF16)


def _dot(a, b):
    return jnp.dot(_bf(a), _bf(b), preferred_element_type=F32)


def _dot_nt(a, b):
    return lax.dot_general(_bf(a), _bf(b), (((1,), (1,)), ((), ())), preferred_element_type=F32)


def _dot_mask(m, x):
    hi = _bf(x)
    r = x - hi.astype(F32)
    mid = _bf(r)
    lo = _bf(r - mid.astype(F32))
    d = functools.partial(jnp.dot, preferred_element_type=F32)
    return d(m, hi) + (d(m, mid) + d(m, lo))


def _sigmoid(x):
    return 1.0 / (1.0 + jnp.exp(-x))


def _silu(x):
    return x * _sigmoid(x)


def _softplus(x):
    return jnp.maximum(x, 0.0) + jnp.log1p(jnp.exp(-jnp.abs(x)))


def _rms(x, w):
    ms = jnp.mean(x * x, axis=-1, keepdims=True)
    return x * lax.rsqrt(ms + EPS) * w


def _l2norm(t):
    return t * lax.rsqrt(jnp.sum(t * t, axis=-1, keepdims=True) + EPS)


def _rows(m3, nb, lt):
    return jnp.broadcast_to(m3, (nb, lt, m3.shape[-1])).reshape(nb * lt, m3.shape[-1])


def _causal_conv(x, prev, w, nb, lt):
    width = w.shape[0]
    rows, ch = x.shape
    if nb == 1 and lt > SUBLANES:
        tiles = jnp.concatenate([prev, x.reshape(lt // SUBLANES, SUBLANES, ch)], axis=0)
        sub = lax.broadcasted_iota(jnp.int32, (1, SUBLANES, 1), 1)
        out = tiles[1:] * w[width - 1:width, :]
        for j in range(1, width):
            rot = pltpu.roll(tiles, j, axis=1)
            out = out + jnp.where(sub >= j, rot[1:], rot[:-1]) * w[width - 1 - j:width - j, :]
        return out.reshape(rows, ch)
    tmod = lax.broadcasted_iota(jnp.int32, (rows, 1), 0) & (lt - 1)
    out = x * w[width - 1:width, :]
    for j in range(1, width):
        sh = pltpu.roll(x, j, axis=0)
        for t in range(j):
            p = width - 1 - j + t
            sh = jnp.where(tmod == t, _rows(prev[:, p:p + 1, :], nb, lt), sh)
        out = out + sh * w[width - 1 - j:width - j, :]
    return out


def _last_rows(x, nb, lt, n_state):
    ch = x.shape[-1]
    if nb == 1 and lt > SUBLANES:
        return x[lt - SUBLANES:].reshape(1, SUBLANES, ch)
    return x.reshape(nb, lt, ch)[:, lt - n_state:, :]


def _state_rows(nb, lt, n_state):
    return SUBLANES if (nb == 1 and lt > SUBLANES) else n_state


MOD_PARTS = 6


def _mod_kernel(c_ref, w_ref, b_ref, os_ref, op_ref, *, n_s, n_p):
    part = pl.program_id(0)
    res = _dot(_silu(c_ref[...]), w_ref[...]) + b_ref[...]
    for k in range(MOD_PARTS):
        @pl.when(part == k)
        def _():
            os_ref[:, k, :] = res[:n_s]
            op_ref[:, k, :] = res[n_s:n_s + n_p]


def _mod_call(c_all, w_ada, b_ada, n_s, n_p):
    rows = c_all.shape[0]
    return pl.pallas_call(
        functools.partial(_mod_kernel, n_s=n_s, n_p=n_p),
        grid=(MOD_PARTS,),
        in_specs=[pl.BlockSpec((rows, D_MODEL), lambda k: (0, 0)),
                  pl.BlockSpec((D_MODEL, D_MODEL), lambda k: (0, k)),
                  pl.BlockSpec((1, D_MODEL), lambda k: (0, k))],
        out_specs=[pl.BlockSpec((n_s, MOD_PARTS, D_MODEL), lambda k: (0, 0, 0)),
                   pl.BlockSpec((n_p, MOD_PARTS, D_MODEL), lambda k: (0, 0, 0))],
        out_shape=[jax.ShapeDtypeStruct((n_s, MOD_PARTS, D_MODEL), F32),
                   jax.ShapeDtypeStruct((n_p, MOD_PARTS, D_MODEL), F32)],
        compiler_params=_cparams(("arbitrary",)),
        name="adaln_mod",
    )(c_all, w_ada, b_ada)


IN_SPLIT = (DN_CONV_CH, DN_V, BA_PAD, SWA_Q, SWA_KV, SWA_KV)


IN_LOOKAHEAD_SLABS = 1
W_PREP_ROWS = 256
IN_SLAB = 2 * DN_DK


def _in_kernel(*refs, nb, lt, carry, n_sub):
    if carry:
        (x_ref, mod_ref, nw_ref, w_ref, cw_ref,
         qkv_ref, z_ref, ba_ref, sq_ref, sk_ref, sv_ref, tail_ref, h_scr, wdn_ref, wba_ref, wsw_ref) = refs
        prev_ref = tail_ref

        @pl.when(pl.program_id(1) == 0)
        def _():
            tail_ref[...] = jnp.zeros_like(tail_ref)
    else:
        (x_ref, mod_ref, nw_ref, w_ref, cw_ref, prev_ref,
         qkv_ref, z_ref, ba_ref, sq_ref, sk_ref, sv_ref, tail_ref, h_scr, wdn_ref, wba_ref, wsw_ref) = refs

    @pl.when((pl.program_id(0) == 0) & (pl.program_id(1) == 0))
    def _():
        ba_lo = DN_CONV_CH + DN_V
        n_ba = 2 * DN_HEADS
        for r in range(0, ba_lo, W_PREP_ROWS):
            wdn_ref[:, r:r + W_PREP_ROWS] = _bf(w_ref[r:r + W_PREP_ROWS, :].T)
        row = lax.broadcasted_iota(jnp.int32, (BA_PAD, 1), 0)
        wba_ref[...] = _bf(jnp.where(row < n_ba, w_ref[ba_lo:ba_lo + BA_PAD, :], 0.0).T)
        for r in range(0, SWA_Q + 2 * SWA_KV, W_PREP_ROWS):
            wsw_ref[:, r:r + W_PREP_ROWS] = _bf(w_ref[ba_lo + n_ba + r:ba_lo + n_ba + r + W_PREP_ROWS, :].T)

    rows = nb * lt

    def norm(sub):
        x = x_ref[:, sub * lt:(sub + 1) * lt, :]
        ms = jnp.mean(x * x, axis=-1, keepdims=True)
        y = x * lax.rsqrt(ms + EPS) * nw_ref[...]
        h = y * (1.0 + mod_ref[:, 1:2, :]) + mod_ref[:, 0:1, :]
        h_scr[sub, :, :D_MODEL] = _bf(h.reshape(rows, D_MODEL))

    def proj(sub, w_bf_ref, lo, n):
        return jnp.dot(h_scr[sub, :, :D_MODEL], w_bf_ref[:, lo:lo + n], preferred_element_type=F32)

    def plain_proj(sub, i):
        rr = slice(sub * rows, (sub + 1) * rows)
        if i < 2:
            z_ref[rr, i * IN_SLAB:(i + 1) * IN_SLAB] = proj(sub, wdn_ref, DN_CONV_CH + i * IN_SLAB, IN_SLAB)
        elif i < 4:
            sq_ref[rr, (i - 2) * IN_SLAB:(i - 1) * IN_SLAB] = proj(sub, wsw_ref, (i - 2) * IN_SLAB, IN_SLAB)
        elif i == 4:
            kv = proj(sub, wsw_ref, SWA_Q, 2 * SWA_KV)
            sk_ref[rr, :] = kv[:, :SWA_KV]
            sv_ref[rr, :] = kv[:, SWA_KV:]
        else:
            ba_ref[rr, :] = proj(sub, wba_ref, 0, BA_PAD)

    def conv_slab(sub, slab, raw):
        rr = slice(sub * rows, (sub + 1) * rows)
        cols = slice(slab * IN_SLAB, (slab + 1) * IN_SLAB)
        c = _silu(_causal_conv(raw, prev_ref[:, :, cols], cw_ref[:, cols], nb, lt))
        tail_ref[:, :, cols] = _last_rows(raw, nb, lt, DN_CONV - 1)
        if slab * IN_SLAB < 2 * DN_QK:
            scale = DN_DK ** -0.5 if slab * IN_SLAB < DN_QK else 1.0
            c = jnp.concatenate([_l2norm(c[:, i * DN_DK:(i + 1) * DN_DK]) * scale
                                 for i in range(IN_SLAB // DN_DK)], axis=1)
        qkv_ref[rr, cols] = c

    n_slabs = DN_CONV_CH // IN_SLAB
    norm(0)
    for sub in range(n_sub):
        raw_next = proj(sub, wdn_ref, 0, IN_SLAB)
        for slab in range(n_slabs):
            raw = raw_next
            plain_proj(sub, slab)
            if slab + 1 < n_slabs:
                raw_next = proj(sub, wdn_ref, (slab + 1) * IN_SLAB, IN_SLAB)
            if sub + 1 < n_sub and slab == n_slabs - 1 - IN_LOOKAHEAD_SLABS:
                norm(sub + 1)
            conv_slab(sub, slab, raw)


def _in_call(x, mod3, norm_w, w_in_t, conv_w, nb, state):
    bsz, lt, _ = x.shape
    rows = nb * lt
    row_map = lambda i, j: (i, 0)
    const = lambda i, j: (0, 0)
    per_seq = lambda i, j: (i, 0, 0)
    return pl.pallas_call(
        functools.partial(_in_kernel, nb=nb, lt=lt, carry=False, n_sub=1),
        grid=(bsz // nb, 1),
        in_specs=[pl.BlockSpec((nb, lt, D_MODEL), per_seq),
                  pl.BlockSpec((nb, 6, D_MODEL), per_seq),
                  pl.BlockSpec((1, D_MODEL), const),
                  pl.BlockSpec(w_in_t.shape, const, pipeline_mode=RESIDENT),
                  pl.BlockSpec((DN_CONV, DN_CONV_CH), const),
                  pl.BlockSpec((nb, DN_CONV - 1, DN_CONV_CH), per_seq)],
        out_specs=[pl.BlockSpec((rows, n), row_map) for n in IN_SPLIT]
        + [pl.BlockSpec((nb, DN_CONV - 1, DN_CONV_CH), per_seq)],
        out_shape=[jax.ShapeDtypeStruct((bsz * lt, n), F32) for n in IN_SPLIT]
        + [jax.ShapeDtypeStruct((bsz, DN_CONV - 1, DN_CONV_CH), F32)],
        scratch_shapes=[pltpu.VMEM((1, rows, D_MODEL + ROW_PITCH_PAD), BF16),
                        pltpu.VMEM((D_MODEL, DN_CONV_CH + DN_V + ROW_PITCH_PAD), BF16),
                        pltpu.VMEM((D_MODEL, BA_PAD), BF16),
                        pltpu.VMEM((D_MODEL, SWA_Q + 2 * SWA_KV), BF16)],
        compiler_params=_cparams(("arbitrary", "arbitrary")),
        name="norm_in_proj_state",
    )(x, mod3, norm_w, w_in_t, conv_w, state)


INV_BASE_SHIFT = 2
SWA_BLOCKS_PER_STEP = 8
DN_UNITS_STATE = 2


def _unit_lower_inverses(a_mats, ri, ci, chunk_shift):
    def blocks(s):
        return (ri >> s) == (ci >> s)

    base = min(INV_BASE_SHIFT, chunk_shift)
    xs = [jnp.where(blocks(base), -a, 0.0) for a in a_mats]
    ts = [jnp.where(ri == ci, 1.0, n) for n in xs]
    for lvl in range(1, base):
        xs = [_dot(x, x) for x in xs]
        ts = [t + _dot(t, x) for t, x in zip(ts, xs)]
    for s in range(base, chunk_shift):
        sel = blocks(s + 1) & jnp.logical_not(blocks(s))
        ets = [_dot(jnp.where(sel, a, 0.0), t) for a, t in zip(a_mats, ts)]
        ts = [t - _dot(t, et) for t, et in zip(ts, ets)]
    return ts


def _dn_kernel(*refs, nb, chunk, carry, units):
    if carry:
        (qkv_ref, z_ref, ba_ref, alog_ref, dt_ref, nw_ref, o_ref, s_ref) = refs
        s0_ref = s_ref

        @pl.when(pl.program_id(1) == 0)
        def _():
            s_ref[...] = jnp.zeros_like(s_ref)
    else:
        (qkv_ref, z_ref, ba_ref, alog_ref, dt_ref, nw_ref, s0_ref, o_ref, s_ref) = refs

    ri = lax.broadcasted_iota(jnp.int32, (UNIT, UNIT), 0)
    ci = lax.broadcasted_iota(jnp.int32, (UNIT, UNIT), 1)
    shift = int(math.log2(chunk))
    same = (ri >> shift) == (ci >> shift)
    incl = same & (ri >= ci)
    strict = same & (ri > ci)

    ba = ba_ref[...]
    beta_full = _sigmoid(ba)
    g_full = -jnp.exp(alog_ref[...]) * _softplus(ba + dt_ref[...])
    masks = jnp.concatenate([jnp.where(incl, 1.0, 0.0), jnp.where(same, 1.0, 0.0)], axis=0).astype(BF16)
    gsums = [_dot_mask(masks, g_full[u * UNIT:(u + 1) * UNIT]) for u in range(units)]
    g_cum = [g[:UNIT] for g in gsums]
    g_tot = [g[UNIT:] for g in gsums]
    g_cum_t = [g.T for g in g_cum]

    probs = [(u, h) for u in range(units) for h in range(DN_HEADS)]
    n_p = len(probs)
    rs = lambda u: slice(u * UNIT, (u + 1) * UNIT)

    def head_cols(base):
        return [qkv_ref[rs(u), base + h * DN_DK:base + (h + 1) * DN_DK] for u, h in probs]

    q = head_cols(0)
    k = head_cols(DN_QK)
    v = head_cols(2 * DN_QK)
    gc = [g_cum[u][:, DN_HEADS + h:DN_HEADS + h + 1] for u, h in probs]
    gr = [g_cum_t[u][DN_HEADS + h:DN_HEADS + h + 1, :] for u, h in probs]
    gt = [g_tot[u][:, DN_HEADS + h:DN_HEADS + h + 1] for u, h in probs]
    bc = [beta_full[rs(u), h:h + 1] for u, h in probs]
    decay = [jnp.where(incl, jnp.exp(jnp.where(incl, gc[p] - gr[p], 0.0)), 0.0) for p in range(n_p)]
    e_g = [jnp.exp(gc[p]) for p in range(n_p)]
    kq = [_dot_nt(jnp.concatenate([k[p], q[p]], axis=0), k[p]) for p in range(n_p)]
    qk = [kq[p][UNIT:] * decay[p] for p in range(n_p)]
    a_mats = [jnp.where(strict, bc[p] * kq[p][:UNIT] * decay[p], 0.0) for p in range(n_p)]
    t_inv = _unit_lower_inverses(a_mats, ri, ci, shift)
    wvk = [_dot(t_inv[p], jnp.concatenate([v[p] * bc[p], k[p] * (bc[p] * e_g[p])], axis=1)) for p in range(n_p)]
    w_v = [w[:, :DN_DV] for w in wvk]
    w_k = [w[:, DN_DV:] for w in wvk]
    q_dec = [q[p] * e_g[p] for p in range(n_p)]
    k_tail_t = [(k[p] * jnp.exp(gt[p] - gc[p])).T for p in range(n_p)]
    c_dec = [jnp.exp(gt[p]) for p in range(n_p)]

    outs = {}
    if carry:
        state = [s_ref[0, h] for h in range(DN_HEADS)]
        for u in range(units):
            ps = [u * DN_HEADS + h for h in range(DN_HEADS)]
            r = [_dot(jnp.concatenate([w_k[p], q_dec[p]], axis=0), state[h]) for h, p in enumerate(ps)]
            uu = [w_v[p] - r[h][:UNIT] for h, p in enumerate(ps)]
            for h, p in enumerate(ps):
                outs[p] = r[h][UNIT:] + _dot(qk[p], uu[h])
            state = [state[h] * c_dec[p][0:1, :] + _dot(k_tail_t[p], uu[h]) for h, p in enumerate(ps)]
        for h in range(DN_HEADS):
            s_ref[0, h] = state[h]
    else:
        per_unit = nb // units
        for p, (u, h) in enumerate(probs):
            us, qs = [], []
            for s in range(per_unit):
                lo = s * chunk
                lhs = jnp.concatenate([w_k[p][lo:lo + chunk], q_dec[p][lo:lo + chunk]], axis=0)
                r = jnp.dot(lhs, s0_ref[u * per_unit + s, h], preferred_element_type=F32)
                us.append(w_v[p][lo:lo + chunk] - r[:chunk])
                qs.append(r[chunk:])
            uu = jnp.concatenate(us, axis=0)
            outs[p] = jnp.concatenate(qs, axis=0) + _dot(qk[p], uu)
            for s in range(per_unit):
                lo = s * chunk
                upd = jnp.dot(k_tail_t[p][:, lo:lo + chunk], uu[lo:lo + chunk], preferred_element_type=F32)
                s_ref[u * per_unit + s, h] = s0_ref[u * per_unit + s, h] * c_dec[p][lo:lo + 1, :] + upd

    for p, (u, h) in enumerate(probs):
        zz = z_ref[rs(u), h * DN_DV:(h + 1) * DN_DV]
        o_ref[rs(u), h * DN_DV:(h + 1) * DN_DV] = _rms(outs[p], nw_ref[...]) * _silu(zz)


def _dn_call(qkv, z, ba, alog_row, dt_row, norm_w, bsz, seq, state):
    rows = DN_UNITS_STATE * UNIT
    nb = rows // seq
    row_map = lambda i, j: (i, 0)
    const = lambda i, j: (0, 0)
    per_seq = lambda i, j: (i, 0, 0, 0)
    return pl.pallas_call(
        functools.partial(_dn_kernel, nb=nb, chunk=seq, carry=False, units=DN_UNITS_STATE),
        grid=(bsz // nb, 1),
        in_specs=[pl.BlockSpec((rows, DN_CONV_CH), row_map),
                  pl.BlockSpec((rows, DN_V), row_map),
                  pl.BlockSpec((rows, BA_PAD), row_map),
                  pl.BlockSpec((1, BA_PAD), const),
                  pl.BlockSpec((1, BA_PAD), const),
                  pl.BlockSpec((1, DN_DV), const),
                  pl.BlockSpec((nb, DN_HEADS, DN_DK, DN_DV), per_seq)],
        out_specs=[pl.BlockSpec((rows, DN_V), row_map),
                   pl.BlockSpec((nb, DN_HEADS, DN_DK, DN_DV), per_seq)],
        out_shape=[jax.ShapeDtypeStruct((bsz * seq, DN_V), F32),
                   jax.ShapeDtypeStruct((bsz, DN_HEADS, DN_DK, DN_DV), F32)],
        compiler_params=_cparams(("arbitrary", "arbitrary")),
        name="gated_deltanet_state",
    )(qkv, z, ba, alog_row, dt_row, norm_w, state)


MIX_SUB_TILES = 2
SUB_ROWS = 2 * UNIT
MIX_ROWS = MIX_SUB_TILES * SUB_ROWS
SAMPLE_IN_SEQS = 64
SAMPLE_POST_SEQS = 32


def _mix_kernel(x_ref, mod_ref, nw_ref, w_ref, cw_ref, alog_ref, dt_ref, dnw_ref,
                sq_ref, sk_ref, sv_ref, tail_ref, o_ref, s_ref,
                h_scr, wdn_ref, wba_ref, wsw_ref, qkv_scr, z_scr, ba_scr):
    _in_kernel(x_ref, mod_ref, nw_ref, w_ref, cw_ref,
               qkv_scr, z_scr, ba_scr, sq_ref, sk_ref, sv_ref, tail_ref, h_scr, wdn_ref, wba_ref, wsw_ref,
               nb=1, lt=SUB_ROWS, carry=True, n_sub=MIX_SUB_TILES)
    _dn_kernel(qkv_scr, z_scr, ba_scr, alog_ref, dt_ref, dnw_ref, o_ref, s_ref,
               nb=1, chunk=UNIT, carry=True, units=MIX_ROWS // UNIT)


def _mix_call(x, mod3, norm_w, w_in_t, conv_w, alog_row, dt_row, dn_norm_w):
    bsz, seq, _ = x.shape
    nt = seq // MIX_ROWS
    n_tok = bsz * seq
    row_map = lambda i, j: (i * nt + j, 0)
    const = lambda i, j: (0, 0)
    per_seq = lambda i, j: (i, 0, 0)
    swa_cols = (SWA_Q, SWA_KV, SWA_KV)
    return pl.pallas_call(
        _mix_kernel,
        grid=(bsz, nt),
        in_specs=[pl.BlockSpec((1, MIX_ROWS, D_MODEL), lambda i, j: (i, j, 0)),
                  pl.BlockSpec((1, 6, D_MODEL), per_seq),
                  pl.BlockSpec((1, D_MODEL), const),
                  pl.BlockSpec(w_in_t.shape, const, pipeline_mode=RESIDENT),
                  pl.BlockSpec((DN_CONV, DN_CONV_CH), const),
                  pl.BlockSpec((1, BA_PAD), const),
                  pl.BlockSpec((1, BA_PAD), const),
                  pl.BlockSpec((1, DN_DV), const)],
        out_specs=[pl.BlockSpec((MIX_ROWS, n), row_map) for n in swa_cols]
        + [pl.BlockSpec((1, SUBLANES, DN_CONV_CH), per_seq),
           pl.BlockSpec((MIX_ROWS, DN_V), row_map),
           pl.BlockSpec((1, DN_HEADS, DN_DK, DN_DV), lambda i, j: (i, 0, 0, 0))],
        out_shape=[jax.ShapeDtypeStruct((n_tok, n), F32) for n in swa_cols]
        + [jax.ShapeDtypeStruct((bsz, SUBLANES, DN_CONV_CH), F32),
           jax.ShapeDtypeStruct((n_tok, DN_V), F32),
           jax.ShapeDtypeStruct((bsz, DN_HEADS, DN_DK, DN_DV), F32)],
        scratch_shapes=[pltpu.VMEM((MIX_SUB_TILES, SUB_ROWS, D_MODEL + ROW_PITCH_PAD), BF16),
                        pltpu.VMEM((D_MODEL, DN_CONV_CH + DN_V + ROW_PITCH_PAD), BF16),
                        pltpu.VMEM((D_MODEL, BA_PAD), BF16),
                        pltpu.VMEM((D_MODEL, SWA_Q + 2 * SWA_KV), BF16),
                        pltpu.VMEM((MIX_ROWS, DN_CONV_CH), F32),
                        pltpu.VMEM((MIX_ROWS, DN_V + ROW_PITCH_PAD), F32),
                        pltpu.VMEM((MIX_ROWS, BA_PAD), F32)],
        compiler_params=_cparams(("arbitrary", "arbitrary")),
        name="in_proj_deltanet_carry",
    )(x, mod3, norm_w, w_in_t, conv_w, alog_row, dt_row, dn_norm_w)


def _bucket_table():
    i = np.arange(WINDOW, dtype=np.int64)[:, None]
    j = np.arange(2 * WINDOW, dtype=np.int64)[None, :]
    d = np.maximum(i + WINDOW - j, 0)
    exact = N_BUCKETS // 2
    logv = (np.log(np.maximum(d, 1).astype(np.float32) / np.float32(exact)).astype(np.float32)
            / np.float32(math.log(MAX_DISTANCE / exact)))
    large = np.minimum(exact + (logv * np.float32(N_BUCKETS - exact)).astype(np.int32), N_BUCKETS - 1)
    return np.where(d < exact, d, large).astype(np.int32)


def _bias_lookup(rb_ref, bucket, head):
    acc = jnp.zeros(bucket.shape, F32)
    for b in range(N_BUCKETS):
        acc = jnp.where(bucket == b, rb_ref[b, head], acc)
    return acc


def _bias_kernel(rb_ref, bucket_t_ref, bucket_ref, ot_ref, os_ref):
    bucket_t = bucket_t_ref[...]
    kj = lax.broadcasted_iota(jnp.int32, bucket_t.shape, 0)
    qi = lax.broadcasted_iota(jnp.int32, bucket_t.shape, 1)
    dist = qi + WINDOW - kj
    valid = (dist >= 0) & (dist < WINDOW)
    for hk in range(SWA_KV_HEADS):
        for par in range(2):
            for st in range(SWA_GROUP // 2):
                head = hk * SWA_GROUP + 2 * st + par
                gen = jnp.where(valid, _bias_lookup(rb_ref, bucket_t, head) * LOG2_E, NEG_INF)
                ot_ref[1, hk, par, :, st * WINDOW:(st + 1) * WINDOW] = gen
                ot_ref[0, hk, par, :, st * WINDOW:(st + 1) * WINDOW] = jnp.where(kj >= WINDOW, gen, NEG_INF)
    bucket = bucket_ref[...]
    qi = lax.broadcasted_iota(jnp.int32, bucket.shape, 0)
    kj = lax.broadcasted_iota(jnp.int32, bucket.shape, 1)
    dist = qi + WINDOW - kj
    valid = (dist >= 0) & (dist < WINDOW)
    for head in range(SWA_HEADS):
        os_ref[head] = jnp.where(valid, _bias_lookup(rb_ref, bucket, head), NEG_INF)


BIAS_T_SHAPE = (2, SWA_KV_HEADS, 2, 2 * WINDOW, (SWA_GROUP // 2) * WINDOW)
BIAS_S_SHAPE = (SWA_HEADS, SUBLANES, 2 * WINDOW)


def _bias_call(rel_bias):
    bucket = _bucket_table()
    return pl.pallas_call(
        _bias_kernel,
        in_specs=[pl.BlockSpec(memory_space=pltpu.SMEM),
                  pl.BlockSpec((2 * WINDOW, WINDOW), lambda: (0, 0)),
                  pl.BlockSpec((SUBLANES, 2 * WINDOW), lambda: (0, 0))],
        out_specs=[pl.BlockSpec(BIAS_T_SHAPE, lambda: (0,) * len(BIAS_T_SHAPE)),
                   pl.BlockSpec(BIAS_S_SHAPE, lambda: (0,) * len(BIAS_S_SHAPE))],
        out_shape=[jax.ShapeDtypeStruct(BIAS_T_SHAPE, F32), jax.ShapeDtypeStruct(BIAS_S_SHAPE, F32)],
        name="swa_rel_bias_table",
    )(rel_bias, jnp.asarray(np.ascontiguousarray(bucket.T)), jnp.asarray(bucket[:SUBLANES]))


def _softmax_sink_parts(s, sink):
    m = jnp.maximum(jnp.max(s, axis=0, keepdims=True), sink)
    p = jnp.exp2(s - m)
    return p, 1.0 / (jnp.sum(p, axis=0, keepdims=True) + jnp.exp2(sink - m))


def _half_lane_variants(full, hk, lo_half):
    rolled = pltpu.roll(full, SWA_HD, axis=1)
    low_src, high_src = (full, rolled) if hk == 0 else (rolled, full)
    return jnp.where(lo_half, low_src, 0.0), jnp.where(lo_half, 0.0, high_src)


def _swa_prompt_kernel(sink_ref, q_ref, kp_ref, kc_ref, vp_ref, vc_ref, bias_ref, o_ref, *, n_blk):
    step = pl.program_id(1)
    lo_half = lax.broadcasted_iota(jnp.int32, (1, 2 * SWA_HD), 1) < SWA_HD
    lo_rows = lax.broadcasted_iota(jnp.int32, (2 * SWA_HD, 1), 0) < SWA_HD
    q = _bf(q_ref[...] * (SWA_HD ** -0.5 * LOG2_E))
    keys = jnp.concatenate([kp_ref[...], kc_ref[...]], axis=0)
    vals = jnp.concatenate([vp_ref[...], vc_ref[...]], axis=0)
    k_var = [[_bf(t) for t in _half_lane_variants(keys, hk, lo_half)] for hk in range(SWA_KV_HEADS)]
    v_var_t = [[_bf(t.T) for t in _half_lane_variants(vals, hk, lo_half)] for hk in range(SWA_KV_HEADS)]
    n_stack = SWA_GROUP // 2
    sinks = [[jnp.concatenate([jnp.full((1, WINDOW), sink_ref[hk * SWA_GROUP + 2 * st + par] * LOG2_E, F32)
                               for st in range(n_stack)], axis=1) for par in range(2)]
             for hk in range(SWA_KV_HEADS)]
    nt_dims = (((1,), (1,)), ((), ()))

    def scores(b):
        rows = slice(b * WINDOW, (b + 1) * WINDOW)
        win = slice(b * WINDOW, (b + 2) * WINDOW)
        variant = jnp.where(step == 0, 0, 1) if b == 0 else 1
        out = []
        for hk in range(SWA_KV_HEADS):
            q2 = jnp.concatenate([q[rows, (hk * n_stack + st) * 2 * SWA_HD:(hk * n_stack + st + 1) * 2 * SWA_HD]
                                  for st in range(n_stack)], axis=0)
            out.append([lax.dot_general(k_var[hk][par][win], q2, nt_dims, preferred_element_type=F32)
                        + bias_ref[variant, hk, par] for par in range(2)])
        return out

    s_next = scores(0)
    for b in range(n_blk):
        s_cur = s_next
        if b + 1 < n_blk:
            s_next = scores(b + 1)
        rows = slice(b * WINDOW, (b + 1) * WINDOW)
        win = slice(b * WINDOW, (b + 2) * WINDOW)
        for hk in range(SWA_KV_HEADS):
            parts = [_softmax_sink_parts(s_cur[hk][par], sinks[hk][par]) for par in range(2)]
            o_t = (jnp.dot(v_var_t[hk][0][:, win], _bf(parts[0][0]), preferred_element_type=F32)
                   + jnp.dot(v_var_t[hk][1][:, win], _bf(parts[1][0]), preferred_element_type=F32))
            o_t = o_t * jnp.where(lo_rows, parts[0][1], parts[1][1])
            for st in range(n_stack):
                lo = (hk * n_stack + st) * 2 * SWA_HD
                o_ref[rows, lo:lo + 2 * SWA_HD] = o_t[:, st * WINDOW:(st + 1) * WINDOW].T


def _swa_prompt_call(sinks, sq, sk, sv, bias, bsz, seq):
    n_blk = SWA_BLOCKS_PER_STEP
    tile = n_blk * WINDOW
    nt = seq // tile
    cur = lambda b, i: (b * nt + i, 0)
    prv = lambda b, i: (b * nt * n_blk + jnp.maximum(i * n_blk - 1, 0), 0)
    return pl.pallas_call(
        functools.partial(_swa_prompt_kernel, n_blk=n_blk),
        grid=(bsz, nt),
        in_specs=[pl.BlockSpec(memory_space=pltpu.SMEM),
                  pl.BlockSpec((tile, SWA_Q), cur),
                  pl.BlockSpec((WINDOW, SWA_KV), prv),
                  pl.BlockSpec((tile, SWA_KV), cur),
                  pl.BlockSpec((WINDOW, SWA_KV), prv),
                  pl.BlockSpec((tile, SWA_KV), cur),
                  pl.BlockSpec(BIAS_T_SHAPE, lambda b, i: (0,) * len(BIAS_T_SHAPE))],
        out_specs=pl.BlockSpec((tile, SWA_Q), cur),
        out_shape=jax.ShapeDtypeStruct((bsz * seq, SWA_Q), F32),
        compiler_params=_cparams(("arbitrary", "arbitrary")),
        name="swa_banded",
    )(sinks, sq, sk, sk, sv, sv, bias)


def _swa_sample_kernel(sink_ref, q_ref, kn_ref, vn_ref, kc_ref, vc_ref, bias_ref, o_ref, ko_ref, vo_ref, *, nb, lt):
    lo_half = lax.broadcasted_iota(jnp.int32, (1, 2 * SWA_HD), 1) < SWA_HD
    q_all = q_ref[...] * (SWA_HD ** -0.5)
    kn_all = kn_ref[...]
    vn_all = vn_ref[...]
    pad = jnp.zeros((WINDOW - lt, SWA_KV), F32)
    bias = jnp.concatenate([bias_ref[h, 0:lt, :] for h in range(SWA_HEADS)], axis=0)
    sink = jnp.concatenate([jnp.full((lt, 1), sink_ref[h], F32) for h in range(SWA_HEADS)], axis=0)
    nt_dims = (((1,), (1,)), ((), ()))

    def to_half(tile, src_half, dst_half):
        return tile if src_half == dst_half else pltpu.roll(tile, SWA_HD, axis=1)

    lhs, keys, vals = [], [], []
    for s in range(nb):
        rows = slice(s * lt, (s + 1) * lt)
        kc, vc, kn, vn = kc_ref[s], vc_ref[s], kn_all[rows], vn_all[rows]
        ko_ref[s] = jnp.concatenate([kc[lt:], kn], axis=0)
        vo_ref[s] = jnp.concatenate([vc[lt:], vn], axis=0)
        keys.append(_bf(jnp.concatenate([kc, kn, pad], axis=0)))
        vals.append(_bf(jnp.concatenate([vc, vn, pad], axis=0)))
        q = q_all[rows]
        tiles = []
        for h in range(SWA_HEADS):
            hk = h // SWA_GROUP
            t = to_half(q[:, (h // 2) * 2 * SWA_HD:(h // 2 + 1) * 2 * SWA_HD], h % 2, hk)
            tiles.append(jnp.where(lo_half if hk == 0 else jnp.logical_not(lo_half), t, 0.0))
        lhs.append(_bf(jnp.concatenate(tiles, axis=0)))
    scores = [lax.dot_general(lhs[s], keys[s], nt_dims, preferred_element_type=F32) + bias for s in range(nb)]
    m = [jnp.maximum(jnp.max(sc, axis=-1, keepdims=True), sink) for sc in scores]
    p = [jnp.exp(sc - mm) for sc, mm in zip(scores, m)]
    rinv = [1.0 / (jnp.sum(pp, axis=-1, keepdims=True) + jnp.exp(sink - mm)) for pp, mm in zip(p, m)]
    res = [jnp.dot(_bf(p[s]), vals[s], preferred_element_type=F32) * rinv[s] for s in range(nb)]
    rows_out = []
    for s in range(nb):
        tiles = []
        for pair in range(SWA_HEADS // 2):
            hk = (2 * pair) // SWA_GROUP
            low = to_half(res[s][(2 * pair) * lt:(2 * pair + 1) * lt], hk, 0)
            high = to_half(res[s][(2 * pair + 1) * lt:(2 * pair + 2) * lt], hk, 1)
            tiles.append(jnp.where(lo_half, low, high))
        rows_out.append(jnp.concatenate(tiles, axis=1))
    o_ref[...] = jnp.concatenate(rows_out, axis=0)


def _swa_sample_call(sinks, sq, sk, sv, cache_k, cache_v, bias, bsz, seq):
    nb = UNIT // seq
    rows = lambda i: (i, 0)
    seqs = lambda i: (i, 0, 0)
    return pl.pallas_call(
        functools.partial(_swa_sample_kernel, nb=nb, lt=seq),
        grid=(bsz // nb,),
        in_specs=[pl.BlockSpec(memory_space=pltpu.SMEM),
                  pl.BlockSpec((UNIT, SWA_Q), rows),
                  pl.BlockSpec((UNIT, SWA_KV), rows),
                  pl.BlockSpec((UNIT, SWA_KV), rows),
                  pl.BlockSpec((nb, WINDOW, SWA_KV), seqs),
                  pl.BlockSpec((nb, WINDOW, SWA_KV), seqs),
                  pl.BlockSpec(BIAS_S_SHAPE, lambda i: (0,) * len(BIAS_S_SHAPE))],
        out_specs=[pl.BlockSpec((UNIT, SWA_Q), rows),
                   pl.BlockSpec((nb, WINDOW, SWA_KV), seqs),
                   pl.BlockSpec((nb, WINDOW, SWA_KV), seqs)],
        out_shape=[jax.ShapeDtypeStruct((bsz * seq, SWA_Q), F32),
                   jax.ShapeDtypeStruct((bsz, WINDOW, SWA_KV), F32),
                   jax.ShapeDtypeStruct((bsz, WINDOW, SWA_KV), F32)],
        compiler_params=_cparams(("arbitrary",)),
        name="swa_cached",
    )(sinks, sq, sk, sv, cache_k, cache_v, bias)


def _post_kernel(*refs, nb, lt, carry, n_sub):
    if carry:
        (odn_ref, oswa_ref, x_ref, mod_ref, wout_ref, nfw_ref, wup_ref, cw_ref, cb_ref, wdn_ref, fnw_ref,
         y_ref, fbuf_ref, h_scr, x1_scr, acc_scr) = refs
        prev_ref = fbuf_ref

        @pl.when(pl.program_id(1) == 0)
        def _():
            fbuf_ref[...] = jnp.zeros_like(fbuf_ref)
    else:
        (odn_ref, oswa_ref, x_ref, mod_ref, wout_ref, nfw_ref, wup_ref, cw_ref, cb_ref, wdn_ref, fnw_ref,
         prev_ref, y_ref, fbuf_ref, h_scr, x1_scr, acc_scr) = refs
    rows = nb * lt
    n_chunks = D_FF // FFN_CHUNK

    def col_slices(c):
        return [slice(base + c * FFN_CHUNK, base + (c + 1) * FFN_CHUNK) for base in (0, D_FF)]

    def prologue(sub):
        rr = slice(sub * rows, (sub + 1) * rows)
        attn = (jnp.dot(_bf(odn_ref[rr, :]), wout_ref[0:DN_V, :], preferred_element_type=F32)
                + jnp.dot(_bf(oswa_ref[rr, :]), wout_ref[DN_V:, :], preferred_element_type=F32))
        x = x_ref[:, sub * lt:(sub + 1) * lt, :].reshape(rows, D_MODEL)
        x1 = x + _rows(mod_ref[:, 2:3, :], nb, lt) * attn
        x1_scr[sub] = x1
        h = (_rms(x1, nfw_ref[...]) * (1.0 + _rows(mod_ref[:, 4:5, :], nb, lt))
             + _rows(mod_ref[:, 3:4, :], nb, lt))
        h_scr[sub] = _bf(h)
        acc_scr[sub] = jnp.zeros((rows, D_MODEL), F32)

    def up_proj(sub, c):
        return [jnp.dot(h_scr[sub], wup_ref[:, cols], preferred_element_type=F32) for cols in col_slices(c)]

    def chunk(sub, c, u_cur):
        halves = []
        for u, cols in zip(u_cur, col_slices(c)):
            prev = prev_ref[:, :, cols]
            halves.append(_causal_conv(u, prev, cw_ref[:, cols], nb, lt) + cb_ref[:, cols])
            fbuf_ref[:, :, cols] = _last_rows(u, nb, lt, FFN_CONV - 1)
        act = _silu(halves[0]) * halves[1]
        acc_scr[sub] += jnp.dot(_bf(act), wdn_ref[c * FFN_CHUNK:(c + 1) * FFN_CHUNK, :],
                                preferred_element_type=F32)

    def epilogue(sub):
        x2 = x1_scr[sub] + _rows(mod_ref[:, 5:6, :], nb, lt) * acc_scr[sub]
        y_ref[:, sub * lt:(sub + 1) * lt, :] = _rms(x2, fnw_ref[...]).reshape(nb, lt, D_MODEL)

    prologue(0)
    for sub in range(n_sub):
        u_next = up_proj(sub, 0)
        for c in range(n_chunks):
            u_cur = u_next
            if c + 1 < n_chunks:
                u_next = up_proj(sub, c + 1)
            chunk(sub, c, u_cur)
            if sub + 1 < n_sub and c == n_chunks - 1 - POST_LOOKAHEAD_CHUNKS:
                prologue(sub + 1)
        epilogue(sub)


def _post_call(o_dn, o_swa, x, mod3, w_out, norm_ffn_w, w_up, conv_w, conv_b, w_down, final_w, nb, lt, state=None):
    carry = state is None
    n_sub = POST_SUB_TILES if carry else 1
    bsz, seq, _ = x.shape
    nt = seq // (lt * n_sub)
    rows = nb * lt
    tile = rows * n_sub
    n_state = _state_rows(nb, lt, FFN_CONV - 1)
    row_map = lambda i, j: (i * nt + j, 0)
    const = lambda i, j: (0, 0)
    in_specs = [pl.BlockSpec((tile, DN_V), row_map),
                pl.BlockSpec((tile, SWA_Q), row_map),
                pl.BlockSpec((nb, lt * n_sub, D_MODEL), lambda i, j: (i, j, 0)),
                pl.BlockSpec((nb, 6, D_MODEL), lambda i, j: (i, 0, 0)),
                pl.BlockSpec((D_MODEL, D_MODEL), const, pipeline_mode=RESIDENT),
                pl.BlockSpec((1, D_MODEL), const),
                pl.BlockSpec(w_up.shape, const, pipeline_mode=RESIDENT),
                pl.BlockSpec((FFN_CONV, 2 * D_FF), const),
                pl.BlockSpec((1, 2 * D_FF), const),
                pl.BlockSpec((D_FF, D_MODEL), const, pipeline_mode=RESIDENT),
                pl.BlockSpec((1, D_MODEL), const)]
    args = [o_dn, o_swa, x, mod3, w_out, norm_ffn_w, w_up, conv_w, conv_b, w_down, final_w]
    if not carry:
        in_specs.append(pl.BlockSpec((nb, FFN_CONV - 1, 2 * D_FF), lambda i, j: (i, 0, 0)))
        args.append(state)
    return pl.pallas_call(
        functools.partial(_post_kernel, nb=nb, lt=lt, carry=carry, n_sub=n_sub),
        grid=(bsz // nb, nt),
        in_specs=in_specs,
        out_specs=[pl.BlockSpec((nb, lt * n_sub, D_MODEL), lambda i, j: (i, j, 0)),
                   pl.BlockSpec((nb, n_state, 2 * D_FF), lambda i, j: (i, 0, 0))],
        out_shape=[jax.ShapeDtypeStruct((bsz, seq, D_MODEL), F32),
                   jax.ShapeDtypeStruct((bsz, n_state, 2 * D_FF), F32)],
        scratch_shapes=[pltpu.VMEM((n_sub, rows, D_MODEL), BF16),
                        pltpu.VMEM((n_sub, rows, D_MODEL), F32),
                        pltpu.VMEM((n_sub, rows, D_MODEL), F32)],
        compiler_params=_cparams(("arbitrary", "arbitrary")),
        name="out_proj_convffn_carry" if carry else "out_proj_convffn_state",
    )(*args)


def _pad_row(vec, offset):
    return jnp.zeros((1, BA_PAD), F32).at[0, offset:offset + vec.shape[0]].set(vec)


def kernel(x_prompt, x_sample, state_dn_conv, state_dn_ssm, cache_swa_k, cache_swa_v, state_ffn_conv, c_prompt, c_sample, rel_bias, final_norm_w, w_ada, b_ada, norm_mix_w, w_in, dn_conv_w, dn_A_log, dn_dt_bias, dn_norm_w, swa_sinks, w_out, norm_ffn_w, ffn_w_up, ffn_conv_w, ffn_conv_b, ffn_w_down):
    bp, lp, _ = x_prompt.shape
    bs, ls, _ = x_sample.shape
    layer = 0

    w_in_t = jnp.transpose(w_in[layer])
    w_out_b = w_out[layer].astype(BF16)
    w_up_b = jnp.pad(ffn_w_up[layer].astype(BF16), ((0, 0), (0, ROW_PITCH_PAD)))
    w_dn_b = ffn_w_down[layer].astype(BF16)
    alog_row = _pad_row(dn_A_log[layer], DN_HEADS)
    dt_row = _pad_row(dn_dt_bias[layer], DN_HEADS)
    row = lambda v: v.reshape(1, -1)

    n_c = bp + bs
    n_c_pad = -(-n_c // SUBLANES) * SUBLANES
    c_all = jnp.pad(jnp.concatenate([c_sample, c_prompt], axis=0), ((0, n_c_pad - n_c), (0, 0)))
    mod_s, mod_p = _mod_call(c_all, w_ada[layer], row(b_ada[layer]), bs, bp)

    bias_t, bias_s = _bias_call(rel_bias)
    sinks = swa_sinks[layer]

    def post(o_dn, o_swa, x, mod3, nb, lt, state=None):
        return _post_call(o_dn, o_swa, x, mod3, w_out_b, row(norm_ffn_w[layer]), w_up_b, ffn_conv_w[layer],
                          row(ffn_conv_b[layer]), w_dn_b, row(final_norm_w), nb, lt, state)

    dn_args = (alog_row, dt_row, row(dn_norm_w[layer]))

    sq, sk, sv, p_tail, o_dn, p_ssm = _mix_call(x_prompt, mod_p, row(norm_mix_w[layer]), w_in_t,
                                                dn_conv_w[layer], *dn_args)
    o_swa = _swa_prompt_call(sinks, sq, sk, sv, bias_t, bp, lp)
    y_prompt, p_ffn_tail = post(o_dn, o_swa, x_prompt, mod_p, 1, SUB_ROWS)
    p_dn_conv = p_tail[:, SUBLANES - (DN_CONV - 1):]
    p_fbuf = p_ffn_tail[:, SUBLANES - (FFN_CONV - 1):]
    last_window = lambda t: t.reshape(bp, lp, SWA_KV)[:, lp - WINDOW:].reshape(bp, WINDOW, SWA_KV_HEADS, SWA_HD)
    p_swa_k = last_window(sk)
    p_swa_v = last_window(sv)

    qkv_s, z_s, ba_s, sq_s, sk_s, sv_s, s_dn_conv = _in_call(
        x_sample, mod_s, row(norm_mix_w[layer]), w_in_t, dn_conv_w[layer], SAMPLE_IN_SEQS, state_dn_conv[layer])
    o_dn_s, s_ssm = _dn_call(qkv_s, z_s, ba_s, *dn_args, bs, ls, state=state_dn_ssm[layer])
    o_swa_s, s_k, s_v = _swa_sample_call(sinks, sq_s, sk_s, sv_s,
                                         cache_swa_k[layer].reshape(bs, WINDOW, SWA_KV),
                                         cache_swa_v[layer].reshape(bs, WINDOW, SWA_KV), bias_s, bs, ls)
    y_sample, s_fbuf = post(o_dn_s, o_swa_s, x_sample, mod_s, SAMPLE_POST_SEQS, ls, state=state_ffn_conv[layer])

    return (y_prompt, y_sample, p_dn_conv[None], s_dn_conv[None], p_ssm[None], s_ssm[None],
            p_swa_k[None], s_k.reshape(bs, WINDOW, SWA_KV_HEADS, SWA_HD)[None],
            p_swa_v[None], s_v.reshape(bs, WINDOW, SWA_KV_HEADS, SWA_HD)[None],
            p_fbuf[None], s_fbuf[None])
```

```python
import functools
import math

import numpy as np
import jax
import jax.numpy as jnp
from jax import lax
from jax.experimental import pallas as pl
from jax.experimental.pallas import tpu as pltpu

F32 = jnp.float32
BF16 = jnp.bfloat16

D_MODEL = 1024
DN_HEADS = 4
DN_DK = 128
DN_DV = 128
DN_CONV = 4
SWA_HEADS = 8
SWA_KV_HEADS = 2
SWA_GROUP = SWA_HEADS // SWA_KV_HEADS
SWA_HD = 64
WINDOW = 128
N_BUCKETS = 32
MAX_DISTANCE = 128
D_FF = 2816
FFN_CONV = 3
EPS = 1e-6
NEG_INF = -1e30
LOG2_E = 1.0 / math.log(2.0)

DN_QK = DN_HEADS * DN_DK
DN_V = DN_HEADS * DN_DV
DN_CONV_CH = 2 * DN_QK + DN_V
SWA_Q = SWA_HEADS * SWA_HD
SWA_KV = SWA_KV_HEADS * SWA_HD
BA_PAD = 128
SUBLANES = 8
UNIT = 128
POST_SUB_TILES = 2
POST_LOOKAHEAD_CHUNKS = 3
FFN_CHUNK = 256
ROW_PITCH_PAD = 128
VMEM_LIMIT = 56 * 1024 * 1024
RESIDENT = pl.Buffered(1)


def _cparams(sem):
    return pltpu.CompilerParams(dimension_semantics=sem, vmem_limit_bytes=VMEM_LIMIT)


def _bf(x):
    return x.astype(BF16)


def _dot(a, b):
    return jnp.dot(_bf(a), _bf(b), preferred_element_type=F32)


def _dot_nt(a, b):
    return lax.dot_general(_bf(a), _bf(b), (((1,), (1,)), ((), ())), preferred_element_type=F32)


def _dot_mask(m, x):
    hi = _bf(x)
    r = x - hi.astype(F32)
    mid = _bf(r)
    lo = _bf(r - mid.astype(F32))
    d = functools.partial(jnp.dot, preferred_element_type=F32)
    return d(m, hi) + (d(m, mid) + d(m, lo))


def _sigmoid(x):
    return 1.0 / (1.0 + jnp.exp(-x))


def _silu(x):
    return x * _sigmoid(x)


def _softplus(x):
    return jnp.maximum(x, 0.0) + jnp.log1p(jnp.exp(-jnp.abs(x)))


def _rms(x, w):
    ms = jnp.mean(x * x, axis=-1, keepdims=True)
    return x * lax.rsqrt(ms + EPS) * w


def _l2norm(t):
    return t * lax.rsqrt(jnp.sum(t * t, axis=-1, keepdims=True) + EPS)


def _rows(m3, nb, lt):
    return jnp.broadcast_to(m3, (nb, lt, m3.shape[-1])).reshape(nb * lt, m3.shape[-1])


def _causal_conv(x, prev, w, nb, lt):
    width = w.shape[0]
    rows, ch = x.shape
    if nb == 1 and lt > SUBLANES:
        tiles = jnp.concatenate([prev, x.reshape(lt // SUBLANES, SUBLANES, ch)], axis=0)
        sub = lax.broadcasted_iota(jnp.int32, (1, SUBLANES, 1), 1)
        out = tiles[1:] * w[width - 1:width, :]
        for j in range(1, width):
            rot = pltpu.roll(tiles, j, axis=1)
            out = out + jnp.where(sub >= j, rot[1:], rot[:-1]) * w[width - 1 - j:width - j, :]
        return out.reshape(rows, ch)
    tmod = lax.broadcasted_iota(jnp.int32, (rows, 1), 0) & (lt - 1)
    out = x * w[width - 1:width, :]
    for j in range(1, width):
        sh = pltpu.roll(x, j, axis=0)
        for t in range(j):
            p = width - 1 - j + t
            sh = jnp.where(tmod == t, _rows(prev[:, p:p + 1, :], nb, lt), sh)
        out = out + sh * w[width - 1 - j:width - j, :]
    return out


def _last_rows(x, nb, lt, n_state):
    ch = x.shape[-1]
    if nb == 1 and lt > SUBLANES:
        return x[lt - SUBLANES:].reshape(1, SUBLANES, ch)
    return x.reshape(nb, lt, ch)[:, lt - n_state:, :]


def _state_rows(nb, lt, n_state):
    return SUBLANES if (nb == 1 and lt > SUBLANES) else n_state


MOD_PARTS = 6


def _mod_kernel(c_ref, w_ref, b_ref, os_ref, op_ref, *, n_s, n_p):
    part = pl.program_id(0)
    res = _dot(_silu(c_ref[...]), w_ref[...]) + b_ref[...]
    for k in range(MOD_PARTS):
        @pl.when(part == k)
        def _():
            os_ref[:, k, :] = res[:n_s]
            op_ref[:, k, :] = res[n_s:n_s + n_p]


def _mod_call(c_all, w_ada, b_ada, n_s, n_p):
    rows = c_all.shape[0]
    return pl.pallas_call(
        functools.partial(_mod_kernel, n_s=n_s, n_p=n_p),
        grid=(MOD_PARTS,),
        in_specs=[pl.BlockSpec((rows, D_MODEL), lambda k: (0, 0)),
                  pl.BlockSpec((D_MODEL, D_MODEL), lambda k: (0, k)),
                  pl.BlockSpec((1, D_MODEL), lambda k: (0, k))],
        out_specs=[pl.BlockSpec((n_s, MOD_PARTS, D_MODEL), lambda k: (0, 0, 0)),
                   pl.BlockSpec((n_p, MOD_PARTS, D_MODEL), lambda k: (0, 0, 0))],
        out_shape=[jax.ShapeDtypeStruct((n_s, MOD_PARTS, D_MODEL), F32),
                   jax.ShapeDtypeStruct((n_p, MOD_PARTS, D_MODEL), F32)],
        compiler_params=_cparams(("arbitrary",)),
        name="adaln_mod",
    )(c_all, w_ada, b_ada)


IN_SPLIT = (DN_CONV_CH, DN_V, BA_PAD, SWA_Q, SWA_KV, SWA_KV)


IN_LOOKAHEAD_SLABS = 1
W_PREP_ROWS = 256
IN_SLAB = 2 * DN_DK


def _in_kernel(*refs, nb, lt, carry, n_sub):
    if carry:
        (x_ref, mod_ref, nw_ref, w_ref, cw_ref,
         qkv_ref, z_ref, ba_ref, sq_ref, sk_ref, sv_ref, tail_ref, h_scr, wdn_ref, wba_ref, wsw_ref) = refs
        prev_ref = tail_ref

        @pl.when(pl.program_id(1) == 0)
        def _():
            tail_ref[...] = jnp.zeros_like(tail_ref)
    else:
        (x_ref, mod_ref, nw_ref, w_ref, cw_ref, prev_ref,
         qkv_ref, z_ref, ba_ref, sq_ref, sk_ref, sv_ref, tail_ref, h_scr, wdn_ref, wba_ref, wsw_ref) = refs

    @pl.when((pl.program_id(0) == 0) & (pl.program_id(1) == 0))
    def _():
        ba_lo = DN_CONV_CH + DN_V
        n_ba = 2 * DN_HEADS
        for r in range(0, ba_lo, W_PREP_ROWS):
            wdn_ref[:, r:r + W_PREP_ROWS] = _bf(w_ref[r:r + W_PREP_ROWS, :].T)
        row = lax.broadcasted_iota(jnp.int32, (BA_PAD, 1), 0)
        wba_ref[...] = _bf(jnp.where(row < n_ba, w_ref[ba_lo:ba_lo + BA_PAD, :], 0.0).T)
        for r in range(0, SWA_Q + 2 * SWA_KV, W_PREP_ROWS):
            wsw_ref[:, r:r + W_PREP_ROWS] = _bf(w_ref[ba_lo + n_ba + r:ba_lo + n_ba + r + W_PREP_ROWS, :].T)

    rows = nb * lt

    def norm(sub):
        x = x_ref[:, sub * lt:(sub + 1) * lt, :]
        ms = jnp.mean(x * x, axis=-1, keepdims=True)
        y = x * lax.rsqrt(ms + EPS) * nw_ref[...]
        h = y * (1.0 + mod_ref[:, 1:2, :]) + mod_ref[:, 0:1, :]
        h_scr[sub, :, :D_MODEL] = _bf(h.reshape(rows, D_MODEL))

    def proj(sub, w_bf_ref, lo, n):
        return jnp.dot(h_scr[sub, :, :D_MODEL], w_bf_ref[:, lo:lo + n], preferred_element_type=F32)

    def plain_proj(sub, i):
        rr = slice(sub * rows, (sub + 1) * rows)
        if i < 2:
            z_ref[rr, i * IN_SLAB:(i + 1) * IN_SLAB] = proj(sub, wdn_ref, DN_CONV_CH + i * IN_SLAB, IN_SLAB)
        elif i < 4:
            sq_ref[rr, (i - 2) * IN_SLAB:(i - 1) * IN_SLAB] = proj(sub, wsw_ref, (i - 2) * IN_SLAB, IN_SLAB)
        elif i == 4:
            kv = proj(sub, wsw_ref, SWA_Q, 2 * SWA_KV)
            sk_ref[rr, :] = kv[:, :SWA_KV]
            sv_ref[rr, :] = kv[:, SWA_KV:]
        else:
            ba_ref[rr, :] = proj(sub, wba_ref, 0, BA_PAD)

    def conv_slab(sub, slab, raw):
        rr = slice(sub * rows, (sub + 1) * rows)
        cols = slice(slab * IN_SLAB, (slab + 1) * IN_SLAB)
        c = _silu(_causal_conv(raw, prev_ref[:, :, cols], cw_ref[:, cols], nb, lt))
        tail_ref[:, :, cols] = _last_rows(raw, nb, lt, DN_CONV - 1)
        if slab * IN_SLAB < 2 * DN_QK:
            scale = DN_DK ** -0.5 if slab * IN_SLAB < DN_QK else 1.0
            c = jnp.concatenate([_l2norm(c[:, i * DN_DK:(i + 1) * DN_DK]) * scale
                                 for i in range(IN_SLAB // DN_DK)], axis=1)
        qkv_ref[rr, cols] = c

    n_slabs = DN_CONV_CH // IN_SLAB
    norm(0)
    for sub in range(n_sub):
        raw_next = proj(sub, wdn_ref, 0, IN_SLAB)
        for slab in range(n_slabs):
            raw = raw_next
            plain_proj(sub, slab)
            if slab + 1 < n_slabs:
                raw_next = proj(sub, wdn_ref, (slab + 1) * IN_SLAB, IN_SLAB)
            if sub + 1 < n_sub and slab == n_slabs - 1 - IN_LOOKAHEAD_SLABS:
                norm(sub + 1)
            conv_slab(sub, slab, raw)


def _in_call(x, mod3, norm_w, w_in_t, conv_w, nb, state):
    bsz, lt, _ = x.shape
    rows = nb * lt
    row_map = lambda i, j: (i, 0)
    const = lambda i, j: (0, 0)
    per_seq = lambda i, j: (i, 0, 0)
    return pl.pallas_call(
        functools.partial(_in_kernel, nb=nb, lt=lt, carry=False, n_sub=1),
        grid=(bsz // nb, 1),
        in_specs=[pl.BlockSpec((nb, lt, D_MODEL), per_seq),
                  pl.BlockSpec((nb, 6, D_MODEL), per_seq),
                  pl.BlockSpec((1, D_MODEL), const),
                  pl.BlockSpec(w_in_t.shape, const, pipeline_mode=RESIDENT),
                  pl.BlockSpec((DN_CONV, DN_CONV_CH), const),
                  pl.BlockSpec((nb, DN_CONV - 1, DN_CONV_CH), per_seq)],
        out_specs=[pl.BlockSpec((rows, n), row_map) for n in IN_SPLIT]
        + [pl.BlockSpec((nb, DN_CONV - 1, DN_CONV_CH), per_seq)],
        out_shape=[jax.ShapeDtypeStruct((bsz * lt, n), F32) for n in IN_SPLIT]
        + [jax.ShapeDtypeStruct((bsz, DN_CONV - 1, DN_CONV_CH), F32)],
        scratch_shapes=[pltpu.VMEM((1, rows, D_MODEL + ROW_PITCH_PAD), BF16),
                        pltpu.VMEM((D_MODEL, DN_CONV_CH + DN_V + ROW_PITCH_PAD), BF16),
                        pltpu.VMEM((D_MODEL, BA_PAD), BF16),
                        pltpu.VMEM((D_MODEL, SWA_Q + 2 * SWA_KV), BF16)],
        compiler_params=_cparams(("arbitrary", "arbitrary")),
        name="norm_in_proj_state",
    )(x, mod3, norm_w, w_in_t, conv_w, state)


INV_BASE_SHIFT = 2
SWA_BLOCKS_PER_STEP = 8
DN_UNITS_STATE = 2


def _unit_lower_inverses(a_mats, ri, ci, chunk_shift):
    def blocks(s):
        return (ri >> s) == (ci >> s)

    base = min(INV_BASE_SHIFT, chunk_shift)
    xs = [jnp.where(blocks(base), -a, 0.0) for a in a_mats]
    ts = [jnp.where(ri == ci, 1.0, n) for n in xs]
    for lvl in range(1, base):
        xs = [_dot(x, x) for x in xs]
        ts = [t + _dot(t, x) for t, x in zip(ts, xs)]
    for s in range(base, chunk_shift):
        sel = blocks(s + 1) & jnp.logical_not(blocks(s))
        ets = [_dot(jnp.where(sel, a, 0.0), t) for a, t in zip(a_mats, ts)]
        ts = [t - _dot(t, et) for t, et in zip(ts, ets)]
    return ts


def _dn_kernel(*refs, nb, chunk, carry, units):
    if carry:
        (qkv_ref, z_ref, ba_ref, alog_ref, dt_ref, nw_ref, o_ref, s_ref) = refs
        s0_ref = s_ref

        @pl.when(pl.program_id(1) == 0)
        def _():
            s_ref[...] = jnp.zeros_like(s_ref)
    else:
        (qkv_ref, z_ref, ba_ref, alog_ref, dt_ref, nw_ref, s0_ref, o_ref, s_ref) = refs

    ri = lax.broadcasted_iota(jnp.int32, (UNIT, UNIT), 0)
    ci = lax.broadcasted_iota(jnp.int32, (UNIT, UNIT), 1)
    shift = int(math.log2(chunk))
    same = (ri >> shift) == (ci >> shift)
    incl = same & (ri >= ci)
    strict = same & (ri > ci)

    ba = ba_ref[...]
    beta_full = _sigmoid(ba)
    g_full = -jnp.exp(alog_ref[...]) * _softplus(ba + dt_ref[...])
    masks = jnp.concatenate([jnp.where(incl, 1.0, 0.0), jnp.where(same, 1.0, 0.0)], axis=0).astype(BF16)
    gsums = [_dot_mask(masks, g_full[u * UNIT:(u + 1) * UNIT]) for u in range(units)]
    g_cum = [g[:UNIT] for g in gsums]
    g_tot = [g[UNIT:] for g in gsums]
    g_cum_t = [g.T for g in g_cum]

    probs = [(u, h) for u in range(units) for h in range(DN_HEADS)]
    n_p = len(probs)
    rs = lambda u: slice(u * UNIT, (u + 1) * UNIT)

    def head_cols(base):
        return [qkv_ref[rs(u), base + h * DN_DK:base + (h + 1) * DN_DK] for u, h in probs]

    q = head_cols(0)
    k = head_cols(DN_QK)
    v = head_cols(2 * DN_QK)
    gc = [g_cum[u][:, DN_HEADS + h:DN_HEADS + h + 1] for u, h in probs]
    gr = [g_cum_t[u][DN_HEADS + h:DN_HEADS + h + 1, :] for u, h in probs]
    gt = [g_tot[u][:, DN_HEADS + h:DN_HEADS + h + 1] for u, h in probs]
    bc = [beta_full[rs(u), h:h + 1] for u, h in probs]
    decay = [jnp.where(incl, jnp.exp(jnp.where(incl, gc[p] - gr[p], 0.0)), 0.0) for p in range(n_p)]
    e_g = [jnp.exp(gc[p]) for p in range(n_p)]
    kq = [_dot_nt(jnp.concatenate([k[p], q[p]], axis=0), k[p]) for p in range(n_p)]
    qk = [kq[p][UNIT:] * decay[p] for p in range(n_p)]
    a_mats = [jnp.where(strict, bc[p] * kq[p][:UNIT] * decay[p], 0.0) for p in range(n_p)]
    t_inv = _unit_lower_inverses(a_mats, ri, ci, shift)
    wvk = [_dot(t_inv[p], jnp.concatenate([v[p] * bc[p], k[p] * (bc[p] * e_g[p])], axis=1)) for p in range(n_p)]
    w_v = [w[:, :DN_DV] for w in wvk]
    w_k = [w[:, DN_DV:] for w in wvk]
    q_dec = [q[p] * e_g[p] for p in range(n_p)]
    k_tail_t = [(k[p] * jnp.exp(gt[p] - gc[p])).T for p in range(n_p)]
    c_dec = [jnp.exp(gt[p]) for p in range(n_p)]

    outs = {}
    if carry:
        state = [s_ref[0, h] for h in range(DN_HEADS)]
        for u in range(units):
            ps = [u * DN_HEADS + h for h in range(DN_HEADS)]
            r = [_dot(jnp.concatenate([w_k[p], q_dec[p]], axis=0), state[h]) for h, p in enumerate(ps)]
            uu = [w_v[p] - r[h][:UNIT] for h, p in enumerate(ps)]
            for h, p in enumerate(ps):
                outs[p] = r[h][UNIT:] + _dot(qk[p], uu[h])
            state = [state[h] * c_dec[p][0:1, :] + _dot(k_tail_t[p], uu[h]) for h, p in enumerate(ps)]
        for h in range(DN_HEADS):
            s_ref[0, h] = state[h]
    else:
        per_unit = nb // units
        for p, (u, h) in enumerate(probs):
            us, qs = [], []
            for s in range(per_unit):
                lo = s * chunk
                lhs = jnp.concatenate([w_k[p][lo:lo + chunk], q_dec[p][lo:lo + chunk]], axis=0)
                r = jnp.dot(lhs, s0_ref[u * per_unit + s, h], preferred_element_type=F32)
                us.append(w_v[p][lo:lo + chunk] - r[:chunk])
                qs.append(r[chunk:])
            uu = jnp.concatenate(us, axis=0)
            outs[p] = jnp.concatenate(qs, axis=0) + _dot(qk[p], uu)
            for s in range(per_unit):
                lo = s * chunk
                upd = jnp.dot(k_tail_t[p][:, lo:lo + chunk], uu[lo:lo + chunk], preferred_element_type=F32)
                s_ref[u * per_unit + s, h] = s0_ref[u * per_unit + s, h] * c_dec[p][lo:lo + 1, :] + upd

    for p, (u, h) in enumerate(probs):
        zz = z_ref[rs(u), h * DN_DV:(h + 1) * DN_DV]
        o_ref[rs(u), h * DN_DV:(h + 1) * DN_DV] = _rms(outs[p], nw_ref[...]) * _silu(zz)


def _dn_call(qkv, z, ba, alog_row, dt_row, norm_w, bsz, seq, state):
    rows = DN_UNITS_STATE * UNIT
    nb = rows // seq
    row_map = lambda i, j: (i, 0)
    const = lambda i, j: (0, 0)
    per_seq = lambda i, j: (i, 0, 0, 0)
    return pl.pallas_call(
        functools.partial(_dn_kernel, nb=nb, chunk=seq, carry=False, units=DN_UNITS_STATE),
        grid=(bsz // nb, 1),
        in_specs=[pl.BlockSpec((rows, DN_CONV_CH), row_map),
                  pl.BlockSpec((rows, DN_V), row_map),
                  pl.BlockSpec((rows, BA_PAD), row_map),
                  pl.BlockSpec((1, BA_PAD), const),
                  pl.BlockSpec((1, BA_PAD), const),
                  pl.BlockSpec((1, DN_DV), const),
                  pl.BlockSpec((nb, DN_HEADS, DN_DK, DN_DV), per_seq)],
        out_specs=[pl.BlockSpec((rows, DN_V), row_map),
                   pl.BlockSpec((nb, DN_HEADS, DN_DK, DN_DV), per_seq)],
        out_shape=[jax.ShapeDtypeStruct((bsz * seq, DN_V), F32),
                   jax.ShapeDtypeStruct((bsz, DN_HEADS, DN_DK, DN_DV), F32)],
        compiler_params=_cparams(("arbitrary", "arbitrary")),
        name="gated_deltanet_state",
    )(qkv, z, ba, alog_row, dt_row, norm_w, state)


MIX_SUB_TILES = 2
SUB_ROWS = 2 * UNIT
MIX_ROWS = MIX_SUB_TILES * SUB_ROWS
SAMPLE_IN_SEQS = 64
SAMPLE_POST_SEQS = 32


def _mix_kernel(x_ref, mod_ref, nw_ref, w_ref, cw_ref, alog_ref, dt_ref, dnw_ref,
                sq_ref, sk_ref, sv_ref, tail_ref, o_ref, s_ref,
                h_scr, wdn_ref, wba_ref, wsw_ref, qkv_scr, z_scr, ba_scr):
    _in_kernel(x_ref, mod_ref, nw_ref, w_ref, cw_ref,
               qkv_scr, z_scr, ba_scr, sq_ref, sk_ref, sv_ref, tail_ref, h_scr, wdn_ref, wba_ref, wsw_ref,
               nb=1, lt=SUB_ROWS, carry=True, n_sub=MIX_SUB_TILES)
    _dn_kernel(qkv_scr, z_scr, ba_scr, alog_ref, dt_ref, dnw_ref, o_ref, s_ref,
               nb=1, chunk=UNIT, carry=True, units=MIX_ROWS // UNIT)


def _mix_call(x, mod3, norm_w, w_in_t, conv_w, alog_row, dt_row, dn_norm_w):
    bsz, seq, _ = x.shape
    nt = seq // MIX_ROWS
    n_tok = bsz * seq
    row_map = lambda i, j: (i * nt + j, 0)
    const = lambda i, j: (0, 0)
    per_seq = lambda i, j: (i, 0, 0)
    swa_cols = (SWA_Q, SWA_KV, SWA_KV)
    return pl.pallas_call(
        _mix_kernel,
        grid=(bsz, nt),
        in_specs=[pl.BlockSpec((1, MIX_ROWS, D_MODEL), lambda i, j: (i, j, 0)),
                  pl.BlockSpec((1, 6, D_MODEL), per_seq),
                  pl.BlockSpec((1, D_MODEL), const),
                  pl.BlockSpec(w_in_t.shape, const, pipeline_mode=RESIDENT),
                  pl.BlockSpec((DN_CONV, DN_CONV_CH), const),
                  pl.BlockSpec((1, BA_PAD), const),
                  pl.BlockSpec((1, BA_PAD), const),
                  pl.BlockSpec((1, DN_DV), const)],
        out_specs=[pl.BlockSpec((MIX_ROWS, n), row_map) for n in swa_cols]
        + [pl.BlockSpec((1, SUBLANES, DN_CONV_CH), per_seq),
           pl.BlockSpec((MIX_ROWS, DN_V), row_map),
           pl.BlockSpec((1, DN_HEADS, DN_DK, DN_DV), lambda i, j: (i, 0, 0, 0))],
        out_shape=[jax.ShapeDtypeStruct((n_tok, n), F32) for n in swa_cols]
        + [jax.ShapeDtypeStruct((bsz, SUBLANES, DN_CONV_CH), F32),
           jax.ShapeDtypeStruct((n_tok, DN_V), F32),
           jax.ShapeDtypeStruct((bsz, DN_HEADS, DN_DK, DN_DV), F32)],
        scratch_shapes=[pltpu.VMEM((MIX_SUB_TILES, SUB_ROWS, D_MODEL + ROW_PITCH_PAD), BF16),
                        pltpu.VMEM((D_MODEL, DN_CONV_CH + DN_V + ROW_PITCH_PAD), BF16),
                        pltpu.VMEM((D_MODEL, BA_PAD), BF16),
                        pltpu.VMEM((D_MODEL, SWA_Q + 2 * SWA_KV), BF16),
                        pltpu.VMEM((MIX_ROWS, DN_CONV_CH), F32),
                        pltpu.VMEM((MIX_ROWS, DN_V + ROW_PITCH_PAD), F32),
                        pltpu.VMEM((MIX_ROWS, BA_PAD), F32)],
        compiler_params=_cparams(("arbitrary", "arbitrary")),
        name="in_proj_deltanet_carry",
    )(x, mod3, norm_w, w_in_t, conv_w, alog_row, dt_row, dn_norm_w)


def _bucket_table():
    i = np.arange(WINDOW, dtype=np.int64)[:, None]
    j = np.arange(2 * WINDOW, dtype=np.int64)[None, :]
    d = np.maximum(i + WINDOW - j, 0)
    exact = N_BUCKETS // 2
    logv = (np.log(np.maximum(d, 1).astype(np.float32) / np.float32(exact)).astype(np.float32)
            / np.float32(math.log(MAX_DISTANCE / exact)))
    large = np.minimum(exact + (logv * np.float32(N_BUCKETS - exact)).astype(np.int32), N_BUCKETS - 1)
    return np.where(d < exact, d, large).astype(np.int32)


def _sample_bucket_table(bucket):
    out = bucket[:SUBLANES].copy()
    out[:, 2 * WINDOW - SUBLANES:] = bucket[:SUBLANES, WINDOW:WINDOW + SUBLANES]
    return out


def _bias_lookup(rb_ref, bucket, head):
    acc = jnp.zeros(bucket.shape, F32)
    for b in range(N_BUCKETS):
        acc = jnp.where(bucket == b, rb_ref[b, head], acc)
    return acc


def _bias_kernel(rb_ref, bucket_t_ref, bucket_ref, ot_ref, os_ref):
    bucket_t = bucket_t_ref[...]
    kj = lax.broadcasted_iota(jnp.int32, bucket_t.shape, 0)
    qi = lax.broadcasted_iota(jnp.int32, bucket_t.shape, 1)
    dist = qi + WINDOW - kj
    valid = (dist >= 0) & (dist < WINDOW)
    for hk in range(SWA_KV_HEADS):
        for par in range(2):
            for st in range(SWA_GROUP // 2):
                head = hk * SWA_GROUP + 2 * st + par
                gen = jnp.where(valid, _bias_lookup(rb_ref, bucket_t, head) * LOG2_E, NEG_INF)
                ot_ref[1, hk, par, :, st * WINDOW:(st + 1) * WINDOW] = gen
                ot_ref[0, hk, par, :, st * WINDOW:(st + 1) * WINDOW] = jnp.where(kj >= WINDOW, gen, NEG_INF)
    bucket = bucket_ref[...]
    qi = lax.broadcasted_iota(jnp.int32, bucket.shape, 0)
    col = lax.broadcasted_iota(jnp.int32, bucket.shape, 1)
    kj = jnp.where(col < WINDOW, col, col - (WINDOW - SUBLANES))
    dist = qi + WINDOW - kj
    valid = (dist >= 0) & (dist < WINDOW) & ((col < WINDOW) | (col >= 2 * WINDOW - SUBLANES))
    for head in range(SWA_HEADS):
        os_ref[head] = jnp.where(valid, _bias_lookup(rb_ref, bucket, head), NEG_INF)


BIAS_T_SHAPE = (2, SWA_KV_HEADS, 2, 2 * WINDOW, (SWA_GROUP // 2) * WINDOW)
BIAS_S_SHAPE = (SWA_HEADS, SUBLANES, 2 * WINDOW)


def _bias_call(rel_bias):
    bucket = _bucket_table()
    return pl.pallas_call(
        _bias_kernel,
        in_specs=[pl.BlockSpec(memory_space=pltpu.SMEM),
                  pl.BlockSpec((2 * WINDOW, WINDOW), lambda: (0, 0)),
                  pl.BlockSpec((SUBLANES, 2 * WINDOW), lambda: (0, 0))],
        out_specs=[pl.BlockSpec(BIAS_T_SHAPE, lambda: (0,) * len(BIAS_T_SHAPE)),
                   pl.BlockSpec(BIAS_S_SHAPE, lambda: (0,) * len(BIAS_S_SHAPE))],
        out_shape=[jax.ShapeDtypeStruct(BIAS_T_SHAPE, F32), jax.ShapeDtypeStruct(BIAS_S_SHAPE, F32)],
        name="swa_rel_bias_table",
    )(rel_bias, jnp.asarray(np.ascontiguousarray(bucket.T)), jnp.asarray(_sample_bucket_table(bucket)))


def _softmax_sink_parts(s, sink):
    m = jnp.maximum(jnp.max(s, axis=0, keepdims=True), sink)
    p = jnp.exp2(s - m)
    return p, 1.0 / (jnp.sum(p, axis=0, keepdims=True) + jnp.exp2(sink - m))


def _half_lane_variants(full, hk, lo_half):
    rolled = pltpu.roll(full, SWA_HD, axis=1)
    low_src, high_src = (full, rolled) if hk == 0 else (rolled, full)
    return jnp.where(lo_half, low_src, 0.0), jnp.where(lo_half, 0.0, high_src)


def _swa_prompt_kernel(sink_ref, q_ref, kp_ref, kc_ref, vp_ref, vc_ref, bias_ref, o_ref, *, n_blk):
    step = pl.program_id(1)
    lo_half = lax.broadcasted_iota(jnp.int32, (1, 2 * SWA_HD), 1) < SWA_HD
    lo_rows = lax.broadcasted_iota(jnp.int32, (2 * SWA_HD, 1), 0) < SWA_HD
    q = _bf(q_ref[...] * (SWA_HD ** -0.5 * LOG2_E))
    keys = jnp.concatenate([kp_ref[...], kc_ref[...]], axis=0)
    vals = jnp.concatenate([vp_ref[...], vc_ref[...]], axis=0)
    k_var = [[_bf(t) for t in _half_lane_variants(keys, hk, lo_half)] for hk in range(SWA_KV_HEADS)]
    v_var_t = [[_bf(t.T) for t in _half_lane_variants(vals, hk, lo_half)] for hk in range(SWA_KV_HEADS)]
    n_stack = SWA_GROUP // 2
    sinks = [[jnp.concatenate([jnp.full((1, WINDOW), sink_ref[hk * SWA_GROUP + 2 * st + par] * LOG2_E, F32)
                               for st in range(n_stack)], axis=1) for par in range(2)]
             for hk in range(SWA_KV_HEADS)]
    nt_dims = (((1,), (1,)), ((), ()))

    def scores(b):
        rows = slice(b * WINDOW, (b + 1) * WINDOW)
        win = slice(b * WINDOW, (b + 2) * WINDOW)
        variant = jnp.where(step == 0, 0, 1) if b == 0 else 1
        out = []
        for hk in range(SWA_KV_HEADS):
            q2 = jnp.concatenate([q[rows, (hk * n_stack + st) * 2 * SWA_HD:(hk * n_stack + st + 1) * 2 * SWA_HD]
                                  for st in range(n_stack)], axis=0)
            out.append([lax.dot_general(k_var[hk][par][win], q2, nt_dims, preferred_element_type=F32)
                        + bias_ref[variant, hk, par] for par in range(2)])
        return out

    s_next = scores(0)
    for b in range(n_blk):
        s_cur = s_next
        if b + 1 < n_blk:
            s_next = scores(b + 1)
        rows = slice(b * WINDOW, (b + 1) * WINDOW)
        win = slice(b * WINDOW, (b + 2) * WINDOW)
        for hk in range(SWA_KV_HEADS):
            parts = [_softmax_sink_parts(s_cur[hk][par], sinks[hk][par]) for par in range(2)]
            o_t = (jnp.dot(v_var_t[hk][0][:, win], _bf(parts[0][0]), preferred_element_type=F32)
                   + jnp.dot(v_var_t[hk][1][:, win], _bf(parts[1][0]), preferred_element_type=F32))
            o_t = o_t * jnp.where(lo_rows, parts[0][1], parts[1][1])
            for st in range(n_stack):
                lo = (hk * n_stack + st) * 2 * SWA_HD
                o_ref[rows, lo:lo + 2 * SWA_HD] = o_t[:, st * WINDOW:(st + 1) * WINDOW].T


def _swa_prompt_call(sinks, sq, sk, sv, bias, bsz, seq):
    n_blk = SWA_BLOCKS_PER_STEP
    tile = n_blk * WINDOW
    nt = seq // tile
    cur = lambda b, i: (b * nt + i, 0)
    prv = lambda b, i: (b * nt * n_blk + jnp.maximum(i * n_blk - 1, 0), 0)
    return pl.pallas_call(
        functools.partial(_swa_prompt_kernel, n_blk=n_blk),
        grid=(bsz, nt),
        in_specs=[pl.BlockSpec(memory_space=pltpu.SMEM),
                  pl.BlockSpec((tile, SWA_Q), cur),
                  pl.BlockSpec((WINDOW, SWA_KV), prv),
                  pl.BlockSpec((tile, SWA_KV), cur),
                  pl.BlockSpec((WINDOW, SWA_KV), prv),
                  pl.BlockSpec((tile, SWA_KV), cur),
                  pl.BlockSpec(BIAS_T_SHAPE, lambda b, i: (0,) * len(BIAS_T_SHAPE))],
        out_specs=pl.BlockSpec((tile, SWA_Q), cur),
        out_shape=jax.ShapeDtypeStruct((bsz * seq, SWA_Q), F32),
        compiler_params=_cparams(("arbitrary", "arbitrary")),
        name="swa_banded",
    )(sinks, sq, sk, sk, sv, sv, bias)


def _swa_sample_kernel(sink_ref, q_ref, kn_ref, vn_ref, kc_ref, vc_ref, bias_ref, o_ref, ko_ref, vo_ref, *, nb, lt):
    lane = lax.broadcasted_iota(jnp.int32, (1, 2 * SWA_HD), 1)
    lo_half = lane < SWA_HD
    new_lanes = lane >= WINDOW - lt
    q_all = q_ref[...] * (SWA_HD ** -0.5)
    kn_all_t = kn_ref[...].T
    vn_all_t = vn_ref[...].T
    bias = jnp.concatenate([bias_ref[h, 0:lt, :] for h in range(SWA_HEADS)], axis=0)
    sink = jnp.concatenate([jnp.full((lt, 1), sink_ref[h], F32) for h in range(SWA_HEADS)], axis=0)
    nt_dims = (((1,), (1,)), ((), ()))

    def to_half(tile, src_half, dst_half):
        return tile if src_half == dst_half else pltpu.roll(tile, SWA_HD, axis=1)

    lhs, keys, vals = [], [], []
    for s in range(nb):
        rows = slice(s * lt, (s + 1) * lt)
        to_end = (WINDOW - lt - s * lt) % WINDOW
        for c_ref, n_all_t, out_ref, acc in ((kc_ref, kn_all_t, ko_ref, keys), (vc_ref, vn_all_t, vo_ref, vals)):
            cached = c_ref[s]
            new = n_all_t if to_end == 0 else pltpu.roll(n_all_t, to_end, axis=1)
            out_ref[s] = jnp.where(new_lanes, new, pltpu.roll(cached, WINDOW - lt, axis=1))
            acc.append(_bf(jnp.concatenate([cached, jnp.where(new_lanes, new, 0.0)], axis=1)))
        q = q_all[rows]
        tiles = []
        for h in range(SWA_HEADS):
            hk = h // SWA_GROUP
            t = to_half(q[:, (h // 2) * 2 * SWA_HD:(h // 2 + 1) * 2 * SWA_HD], h % 2, hk)
            tiles.append(jnp.where(lo_half if hk == 0 else jnp.logical_not(lo_half), t, 0.0))
        lhs.append(_bf(jnp.concatenate(tiles, axis=0)))
    scores = [jnp.dot(lhs[s], keys[s], preferred_element_type=F32) + bias for s in range(nb)]
    m = [jnp.maximum(jnp.max(sc, axis=-1, keepdims=True), sink) for sc in scores]
    p = [jnp.exp(sc - mm) for sc, mm in zip(scores, m)]
    rinv = [1.0 / (jnp.sum(pp, axis=-1, keepdims=True) + jnp.exp(sink - mm)) for pp, mm in zip(p, m)]
    res = [lax.dot_general(_bf(p[s]), vals[s], nt_dims, preferred_element_type=F32) * rinv[s] for s in range(nb)]
    rows_out = []
    for s in range(nb):
        tiles = []
        for pair in range(SWA_HEADS // 2):
            hk = (2 * pair) // SWA_GROUP
            low = to_half(res[s][(2 * pair) * lt:(2 * pair + 1) * lt], hk, 0)
            high = to_half(res[s][(2 * pair + 1) * lt:(2 * pair + 2) * lt], hk, 1)
            tiles.append(jnp.where(lo_half, low, high))
        rows_out.append(jnp.concatenate(tiles, axis=1))
    o_ref[...] = jnp.concatenate(rows_out, axis=0)


def _swa_sample_call(sinks, sq, sk, sv, cache_k, cache_v, bias, bsz, seq):
    assert seq == SUBLANES, "the sample bias table places one sublane tile of new keys"
    nb = UNIT // seq
    rows = lambda i: (i, 0)
    seqs = lambda i: (i, 0, 0)
    return pl.pallas_call(
        functools.partial(_swa_sample_kernel, nb=nb, lt=seq),
        grid=(bsz // nb,),
        in_specs=[pl.BlockSpec(memory_space=pltpu.SMEM),
                  pl.BlockSpec((UNIT, SWA_Q), rows),
                  pl.BlockSpec((UNIT, SWA_KV), rows),
                  pl.BlockSpec((UNIT, SWA_KV), rows),
                  pl.BlockSpec((nb, SWA_KV, WINDOW), seqs),
                  pl.BlockSpec((nb, SWA_KV, WINDOW), seqs),
                  pl.BlockSpec(BIAS_S_SHAPE, lambda i: (0,) * len(BIAS_S_SHAPE))],
        out_specs=[pl.BlockSpec((UNIT, SWA_Q), rows),
                   pl.BlockSpec((nb, SWA_KV, WINDOW), seqs),
                   pl.BlockSpec((nb, SWA_KV, WINDOW), seqs)],
        out_shape=[jax.ShapeDtypeStruct((bsz * seq, SWA_Q), F32),
                   jax.ShapeDtypeStruct((bsz, SWA_KV, WINDOW), F32),
                   jax.ShapeDtypeStruct((bsz, SWA_KV, WINDOW), F32)],
        compiler_params=_cparams(("arbitrary",)),
        name="swa_cached",
    )(sinks, sq, sk, sv, cache_k, cache_v, bias)


def _post_kernel(*refs, nb, lt, carry, n_sub):
    if carry:
        (odn_ref, oswa_ref, x_ref, mod_ref, wout_ref, nfw_ref, wup_ref, cw_ref, cb_ref, wdn_ref, fnw_ref,
         y_ref, fbuf_ref, h_scr, x1_scr, acc_scr) = refs
        prev_ref = fbuf_ref

        @pl.when(pl.program_id(1) == 0)
        def _():
            fbuf_ref[...] = jnp.zeros_like(fbuf_ref)
    else:
        (odn_ref, oswa_ref, x_ref, mod_ref, wout_ref, nfw_ref, wup_ref, cw_ref, cb_ref, wdn_ref, fnw_ref,
         prev_ref, y_ref, fbuf_ref, h_scr, x1_scr, acc_scr) = refs
    rows = nb * lt
    n_chunks = D_FF // FFN_CHUNK

    def col_slices(c):
        return [slice(base + c * FFN_CHUNK, base + (c + 1) * FFN_CHUNK) for base in (0, D_FF)]

    def prologue(sub):
        rr = slice(sub * rows, (sub + 1) * rows)
        attn = (jnp.dot(_bf(odn_ref[rr, :]), wout_ref[0:DN_V, :], preferred_element_type=F32)
                + jnp.dot(_bf(oswa_ref[rr, :]), wout_ref[DN_V:, :], preferred_element_type=F32))
        x = x_ref[:, sub * lt:(sub + 1) * lt, :].reshape(rows, D_MODEL)
        x1 = x + _rows(mod_ref[:, 2:3, :], nb, lt) * attn
        x1_scr[sub] = x1
        h = (_rms(x1, nfw_ref[...]) * (1.0 + _rows(mod_ref[:, 4:5, :], nb, lt))
             + _rows(mod_ref[:, 3:4, :], nb, lt))
        h_scr[sub] = _bf(h)
        acc_scr[sub] = jnp.zeros((rows, D_MODEL), F32)

    def up_proj(sub, c):
        return [jnp.dot(h_scr[sub], wup_ref[:, cols], preferred_element_type=F32) for cols in col_slices(c)]

    def chunk(sub, c, u_cur):
        halves = []
        for u, cols in zip(u_cur, col_slices(c)):
            prev = prev_ref[:, :, cols]
            halves.append(_causal_conv(u, prev, cw_ref[:, cols], nb, lt) + cb_ref[:, cols])
            fbuf_ref[:, :, cols] = _last_rows(u, nb, lt, FFN_CONV - 1)
        act = _silu(halves[0]) * halves[1]
        acc_scr[sub] += jnp.dot(_bf(act), wdn_ref[c * FFN_CHUNK:(c + 1) * FFN_CHUNK, :],
                                preferred_element_type=F32)

    def epilogue(sub):
        x2 = x1_scr[sub] + _rows(mod_ref[:, 5:6, :], nb, lt) * acc_scr[sub]
        y_ref[:, sub * lt:(sub + 1) * lt, :] = _rms(x2, fnw_ref[...]).reshape(nb, lt, D_MODEL)

    prologue(0)
    for sub in range(n_sub):
        u_next = up_proj(sub, 0)
        for c in range(n_chunks):
            u_cur = u_next
            if c + 1 < n_chunks:
                u_next = up_proj(sub, c + 1)
            chunk(sub, c, u_cur)
            if sub + 1 < n_sub and c == n_chunks - 1 - POST_LOOKAHEAD_CHUNKS:
                prologue(sub + 1)
        epilogue(sub)


def _post_call(o_dn, o_swa, x, mod3, w_out, norm_ffn_w, w_up, conv_w, conv_b, w_down, final_w, nb, lt, state=None):
    carry = state is None
    n_sub = POST_SUB_TILES if carry else 1
    bsz, seq, _ = x.shape
    nt = seq // (lt * n_sub)
    rows = nb * lt
    tile = rows * n_sub
    n_state = _state_rows(nb, lt, FFN_CONV - 1)
    row_map = lambda i, j: (i * nt + j, 0)
    const = lambda i, j: (0, 0)
    in_specs = [pl.BlockSpec((tile, DN_V), row_map),
                pl.BlockSpec((tile, SWA_Q), row_map),
                pl.BlockSpec((nb, lt * n_sub, D_MODEL), lambda i, j: (i, j, 0)),
                pl.BlockSpec((nb, 6, D_MODEL), lambda i, j: (i, 0, 0)),
                pl.BlockSpec((D_MODEL, D_MODEL), const, pipeline_mode=RESIDENT),
                pl.BlockSpec((1, D_MODEL), const),
                pl.BlockSpec(w_up.shape, const, pipeline_mode=RESIDENT),
                pl.BlockSpec((FFN_CONV, 2 * D_FF), const),
                pl.BlockSpec((1, 2 * D_FF), const),
                pl.BlockSpec((D_FF, D_MODEL), const, pipeline_mode=RESIDENT),
                pl.BlockSpec((1, D_MODEL), const)]
    args = [o_dn, o_swa, x, mod3, w_out, norm_ffn_w, w_up, conv_w, conv_b, w_down, final_w]
    if not carry:
        in_specs.append(pl.BlockSpec((nb, FFN_CONV - 1, 2 * D_FF), lambda i, j: (i, 0, 0)))
        args.append(state)
    return pl.pallas_call(
        functools.partial(_post_kernel, nb=nb, lt=lt, carry=carry, n_sub=n_sub),
        grid=(bsz // nb, nt),
        in_specs=in_specs,
        out_specs=[pl.BlockSpec((nb, lt * n_sub, D_MODEL), lambda i, j: (i, j, 0)),
                   pl.BlockSpec((nb, n_state, 2 * D_FF), lambda i, j: (i, 0, 0))],
        out_shape=[jax.ShapeDtypeStruct((bsz, seq, D_MODEL), F32),
                   jax.ShapeDtypeStruct((bsz, n_state, 2 * D_FF), F32)],
        scratch_shapes=[pltpu.VMEM((n_sub, rows, D_MODEL), BF16),
                        pltpu.VMEM((n_sub, rows, D_MODEL), F32),
                        pltpu.VMEM((n_sub, rows, D_MODEL), F32)],
        compiler_params=_cparams(("arbitrary", "arbitrary")),
        name="out_proj_convffn_carry" if carry else "out_proj_convffn_state",
    )(*args)


def _pad_row(vec, offset):
    return jnp.zeros((1, BA_PAD), F32).at[0, offset:offset + vec.shape[0]].set(vec)


def kernel(x_prompt, x_sample, state_dn_conv, state_dn_ssm, cache_swa_k, cache_swa_v, state_ffn_conv, c_prompt, c_sample, rel_bias, final_norm_w, w_ada, b_ada, norm_mix_w, w_in, dn_conv_w, dn_A_log, dn_dt_bias, dn_norm_w, swa_sinks, w_out, norm_ffn_w, ffn_w_up, ffn_conv_w, ffn_conv_b, ffn_w_down):
    bp, lp, _ = x_prompt.shape
    bs, ls, _ = x_sample.shape
    layer = 0

    w_in_t = jnp.transpose(w_in[layer])
    w_out_b = w_out[layer].astype(BF16)
    w_up_b = jnp.pad(ffn_w_up[layer].astype(BF16), ((0, 0), (0, ROW_PITCH_PAD)))
    w_dn_b = ffn_w_down[layer].astype(BF16)
    alog_row = _pad_row(dn_A_log[layer], DN_HEADS)
    dt_row = _pad_row(dn_dt_bias[layer], DN_HEADS)
    row = lambda v: v.reshape(1, -1)

    n_c = bp + bs
    n_c_pad = -(-n_c // SUBLANES) * SUBLANES
    c_all = jnp.pad(jnp.concatenate([c_sample, c_prompt], axis=0), ((0, n_c_pad - n_c), (0, 0)))
    mod_s, mod_p = _mod_call(c_all, w_ada[layer], row(b_ada[layer]), bs, bp)

    bias_t, bias_s = _bias_call(rel_bias)
    sinks = swa_sinks[layer]

    def post(o_dn, o_swa, x, mod3, nb, lt, state=None):
        return _post_call(o_dn, o_swa, x, mod3, w_out_b, row(norm_ffn_w[layer]), w_up_b, ffn_conv_w[layer],
                          row(ffn_conv_b[layer]), w_dn_b, row(final_norm_w), nb, lt, state)

    dn_args = (alog_row, dt_row, row(dn_norm_w[layer]))

    sq, sk, sv, p_tail, o_dn, p_ssm = _mix_call(x_prompt, mod_p, row(norm_mix_w[layer]), w_in_t,
                                                dn_conv_w[layer], *dn_args)
    o_swa = _swa_prompt_call(sinks, sq, sk, sv, bias_t, bp, lp)
    y_prompt, p_ffn_tail = post(o_dn, o_swa, x_prompt, mod_p, 1, SUB_ROWS)
    p_dn_conv = p_tail[:, SUBLANES - (DN_CONV - 1):]
    p_fbuf = p_ffn_tail[:, SUBLANES - (FFN_CONV - 1):]
    last_window = lambda t: t.reshape(bp, lp, SWA_KV)[:, lp - WINDOW:].reshape(bp, WINDOW, SWA_KV_HEADS, SWA_HD)
    p_swa_k = last_window(sk)
    p_swa_v = last_window(sv)

    cache_t = lambda c: jnp.transpose(c[layer].reshape(bs, WINDOW, SWA_KV), (0, 2, 1))
    cache_from_t = lambda c: jnp.transpose(c, (0, 2, 1)).reshape(bs, WINDOW, SWA_KV_HEADS, SWA_HD)
    qkv_s, z_s, ba_s, sq_s, sk_s, sv_s, s_dn_conv = _in_call(
        x_sample, mod_s, row(norm_mix_w[layer]), w_in_t, dn_conv_w[layer], SAMPLE_IN_SEQS, state_dn_conv[layer])
    o_dn_s, s_ssm = _dn_call(qkv_s, z_s, ba_s, *dn_args, bs, ls, state=state_dn_ssm[layer])
    o_swa_s, s_k, s_v = _swa_sample_call(sinks, sq_s, sk_s, sv_s,
                                         cache_t(cache_swa_k), cache_t(cache_swa_v), bias_s, bs, ls)
    y_sample, s_fbuf = post(o_dn_s, o_swa_s, x_sample, mod_s, SAMPLE_POST_SEQS, ls, state=state_ffn_conv[layer])

    return (y_prompt, y_sample, p_dn_conv[None], s_dn_conv[None], p_ssm[None], s_ssm[None],
            p_swa_k[None], cache_from_t(s_k)[None],
            p_swa_v[None], cache_from_t(s_v)[None],
            p_fbuf[None], s_fbuf[None])
```

```python
import functools
import math

import numpy as np
import jax
import jax.numpy as jnp
from jax import lax
from jax.experimental import pallas as pl
from jax.experimental.pallas import tpu as pltpu

F32 = jnp.float32
BF16 = jnp.bfloat16

D_MODEL = 1024
DN_HEADS = 4
DN_DK = 128
DN_DV = 128
DN_CONV = 4
SWA_HEADS = 8
SWA_KV_HEADS = 2
SWA_GROUP = SWA_HEADS // SWA_KV_HEADS
SWA_HD = 64
WINDOW = 128
N_BUCKETS = 32
MAX_DISTANCE = 128
D_FF = 2816
FFN_CONV = 3
EPS = 1e-6
NEG_INF = -1e30
LOG2_E = 1.0 / math.log(2.0)

DN_QK = DN_HEADS * DN_DK
DN_V = DN_HEADS * DN_DV
DN_CONV_CH = 2 * DN_QK + DN_V
SWA_Q = SWA_HEADS * SWA_HD
SWA_KV = SWA_KV_HEADS * SWA_HD
BA_PAD = 128
SUBLANES = 8
UNIT = 128
POST_SUB_TILES = 4
POST_LOOKAHEAD_CHUNKS = 3
FFN_CHUNK = 256
ROW_PITCH_PAD = 128
VMEM_LIMIT = 56 * 1024 * 1024
RESIDENT = pl.Buffered(1)


def _cparams(sem):
    return pltpu.CompilerParams(dimension_semantics=sem, vmem_limit_bytes=VMEM_LIMIT)


def _bf(x):
    return x.astype(BF16)


def _dot(a, b):
    return jnp.dot(_bf(a), _bf(b), preferred_element_type=F32)


def _dot_nt(a, b):
    return lax.dot_general(_bf(a), _bf(b), (((1,), (1,)), ((), ())), preferred_element_type=F32)


def _dot_mask(m, x):
    hi = _bf(x)
    r = x - hi.astype(F32)
    mid = _bf(r)
    lo = _bf(r - mid.astype(F32))
    d = functools.partial(jnp.dot, preferred_element_type=F32)
    return d(m, hi) + (d(m, mid) + d(m, lo))


def _sigmoid(x):
    return 1.0 / (1.0 + jnp.exp(-x))


def _silu(x):
    return x * _sigmoid(x)


def _softplus(x):
    return jnp.maximum(x, 0.0) + jnp.log1p(jnp.exp(-jnp.abs(x)))


def _rms(x, w):
    ms = jnp.mean(x * x, axis=-1, keepdims=True)
    return x * lax.rsqrt(ms + EPS) * w


def _l2norm(t):
    return t * lax.rsqrt(jnp.sum(t * t, axis=-1, keepdims=True) + EPS)


def _rows(m3, nb, lt):
    return jnp.broadcast_to(m3, (nb, lt, m3.shape[-1])).reshape(nb * lt, m3.shape[-1])


def _causal_conv(x, prev, w, nb, lt):
    width = w.shape[0]
    rows, ch = x.shape
    if nb == 1 and lt > SUBLANES:
        tiles = jnp.concatenate([prev, x.reshape(lt // SUBLANES, SUBLANES, ch)], axis=0)
        sub = lax.broadcasted_iota(jnp.int32, (1, SUBLANES, 1), 1)
        out = tiles[1:] * w[width - 1:width, :]
        for j in range(1, width):
            rot = pltpu.roll(tiles, j, axis=1)
            out = out + jnp.where(sub >= j, rot[1:], rot[:-1]) * w[width - 1 - j:width - j, :]
        return out.reshape(rows, ch)
    tmod = lax.broadcasted_iota(jnp.int32, (rows, 1), 0) & (lt - 1)
    out = x * w[width - 1:width, :]
    for j in range(1, width):
        sh = pltpu.roll(x, j, axis=0)
        for t in range(j):
            p = width - 1 - j + t
            sh = jnp.where(tmod == t, _rows(prev[:, p:p + 1, :], nb, lt), sh)
        out = out + sh * w[width - 1 - j:width - j, :]
    return out


def _last_rows(x, nb, lt, n_state):
    ch = x.shape[-1]
    if nb == 1 and lt > SUBLANES:
        return x[lt - SUBLANES:].reshape(1, SUBLANES, ch)
    return x.reshape(nb, lt, ch)[:, lt - n_state:, :]


def _state_rows(nb, lt, n_state):
    return SUBLANES if (nb == 1 and lt > SUBLANES) else n_state


MOD_PARTS = 6


def _mod_kernel(c_ref, w_ref, b_ref, os_ref, op_ref, *, n_s, n_p):
    part = pl.program_id(0)
    res = _dot(_silu(c_ref[...]), w_ref[...]) + b_ref[...]
    for k in range(MOD_PARTS):
        @pl.when(part == k)
        def _():
            os_ref[:, k, :] = res[:n_s]
            op_ref[:, k, :] = res[n_s:n_s + n_p]


def _mod_call(c_all, w_ada, b_ada, n_s, n_p):
    rows = c_all.shape[0]
    return pl.pallas_call(
        functools.partial(_mod_kernel, n_s=n_s, n_p=n_p),
        grid=(MOD_PARTS,),
        in_specs=[pl.BlockSpec((rows, D_MODEL), lambda k: (0, 0)),
                  pl.BlockSpec((D_MODEL, D_MODEL), lambda k: (0, k)),
                  pl.BlockSpec((1, D_MODEL), lambda k: (0, k))],
        out_specs=[pl.BlockSpec((n_s, MOD_PARTS, D_MODEL), lambda k: (0, 0, 0)),
                   pl.BlockSpec((n_p, MOD_PARTS, D_MODEL), lambda k: (0, 0, 0))],
        out_shape=[jax.ShapeDtypeStruct((n_s, MOD_PARTS, D_MODEL), F32),
                   jax.ShapeDtypeStruct((n_p, MOD_PARTS, D_MODEL), F32)],
        compiler_params=_cparams(("arbitrary",)),
        name="adaln_mod",
    )(c_all, w_ada, b_ada)


IN_SPLIT = (DN_CONV_CH, DN_V, BA_PAD, SWA_Q, SWA_KV, SWA_KV)


IN_LOOKAHEAD_SLABS = 1
W_PREP_ROWS = 256
IN_SLAB = 2 * DN_DK


def _in_kernel(*refs, nb, lt, carry, n_sub):
    if carry:
        (x_ref, mod_ref, nw_ref, w_ref, cw_ref,
         qkv_ref, z_ref, ba_ref, sq_ref, sk_ref, sv_ref, tail_ref, h_scr, wdn_ref, wba_ref, wsw_ref) = refs
        prev_ref = tail_ref

        @pl.when(pl.program_id(1) == 0)
        def _():
            tail_ref[...] = jnp.zeros_like(tail_ref)
    else:
        (x_ref, mod_ref, nw_ref, w_ref, cw_ref, prev_ref,
         qkv_ref, z_ref, ba_ref, sq_ref, sk_ref, sv_ref, tail_ref, h_scr, wdn_ref, wba_ref, wsw_ref) = refs

    @pl.when((pl.program_id(0) == 0) & (pl.program_id(1) == 0))
    def _():
        ba_lo = DN_CONV_CH + DN_V
        n_ba = 2 * DN_HEADS
        for r in range(0, ba_lo, W_PREP_ROWS):
            wdn_ref[:, r:r + W_PREP_ROWS] = _bf(w_ref[r:r + W_PREP_ROWS, :].T)
        row = lax.broadcasted_iota(jnp.int32, (BA_PAD, 1), 0)
        wba_ref[...] = _bf(jnp.where(row < n_ba, w_ref[ba_lo:ba_lo + BA_PAD, :], 0.0).T)
        for r in range(0, SWA_Q + 2 * SWA_KV, W_PREP_ROWS):
            wsw_ref[:, r:r + W_PREP_ROWS] = _bf(w_ref[ba_lo + n_ba + r:ba_lo + n_ba + r + W_PREP_ROWS, :].T)

    rows = nb * lt

    def norm(sub):
        x = x_ref[:, sub * lt:(sub + 1) * lt, :]
        ms = jnp.mean(x * x, axis=-1, keepdims=True)
        y = x * lax.rsqrt(ms + EPS) * nw_ref[...]
        h = y * (1.0 + mod_ref[:, 1:2, :]) + mod_ref[:, 0:1, :]
        h_scr[sub, :, :D_MODEL] = _bf(h.reshape(rows, D_MODEL))

    def proj(sub, w_bf_ref, lo, n):
        return jnp.dot(h_scr[sub, :, :D_MODEL], w_bf_ref[:, lo:lo + n], preferred_element_type=F32)

    def plain_proj(sub, i):
        rr = slice(sub * rows, (sub + 1) * rows)
        if i < 2:
            z_ref[rr, i * IN_SLAB:(i + 1) * IN_SLAB] = proj(sub, wdn_ref, DN_CONV_CH + i * IN_SLAB, IN_SLAB)
        elif i < 4:
            sq_ref[rr, (i - 2) * IN_SLAB:(i - 1) * IN_SLAB] = proj(sub, wsw_ref, (i - 2) * IN_SLAB, IN_SLAB)
        elif i == 4:
            kv = proj(sub, wsw_ref, SWA_Q, 2 * SWA_KV)
            sk_ref[rr, :] = kv[:, :SWA_KV]
            sv_ref[rr, :] = kv[:, SWA_KV:]
        else:
            ba_ref[rr, :] = proj(sub, wba_ref, 0, BA_PAD)

    def conv_slab(sub, slab, raw):
        rr = slice(sub * rows, (sub + 1) * rows)
        cols = slice(slab * IN_SLAB, (slab + 1) * IN_SLAB)
        c = _silu(_causal_conv(raw, prev_ref[:, :, cols], cw_ref[:, cols], nb, lt))
        tail_ref[:, :, cols] = _last_rows(raw, nb, lt, DN_CONV - 1)
        if slab * IN_SLAB < 2 * DN_QK:
            scale = DN_DK ** -0.5 if slab * IN_SLAB < DN_QK else 1.0
            c = jnp.concatenate([_l2norm(c[:, i * DN_DK:(i + 1) * DN_DK]) * scale
                                 for i in range(IN_SLAB // DN_DK)], axis=1)
        qkv_ref[rr, cols] = c

    n_slabs = DN_CONV_CH // IN_SLAB
    norm(0)
    for sub in range(n_sub):
        raw_next = proj(sub, wdn_ref, 0, IN_SLAB)
        for slab in range(n_slabs):
            raw = raw_next
            plain_proj(sub, slab)
            if slab + 1 < n_slabs:
                raw_next = proj(sub, wdn_ref, (slab + 1) * IN_SLAB, IN_SLAB)
            if sub + 1 < n_sub and slab == n_slabs - 1 - IN_LOOKAHEAD_SLABS:
                norm(sub + 1)
            conv_slab(sub, slab, raw)


def _in_call(x, mod3, norm_w, w_in_t, conv_w, nb, state):
    bsz, lt, _ = x.shape
    rows = nb * lt
    row_map = lambda i, j: (i, 0)
    const = lambda i, j: (0, 0)
    per_seq = lambda i, j: (i, 0, 0)
    return pl.pallas_call(
        functools.partial(_in_kernel, nb=nb, lt=lt, carry=False, n_sub=1),
        grid=(bsz // nb, 1),
        in_specs=[pl.BlockSpec((nb, lt, D_MODEL), per_seq),
                  pl.BlockSpec((nb, 6, D_MODEL), per_seq),
                  pl.BlockSpec((1, D_MODEL), const),
                  pl.BlockSpec(w_in_t.shape, const, pipeline_mode=RESIDENT),
                  pl.BlockSpec((DN_CONV, DN_CONV_CH), const),
                  pl.BlockSpec((nb, DN_CONV - 1, DN_CONV_CH), per_seq)],
        out_specs=[pl.BlockSpec((rows, n), row_map) for n in IN_SPLIT]
        + [pl.BlockSpec((nb, DN_CONV - 1, DN_CONV_CH), per_seq)],
        out_shape=[jax.ShapeDtypeStruct((bsz * lt, n), F32) for n in IN_SPLIT]
        + [jax.ShapeDtypeStruct((bsz, DN_CONV - 1, DN_CONV_CH), F32)],
        scratch_shapes=[pltpu.VMEM((1, rows, D_MODEL + ROW_PITCH_PAD), BF16),
                        pltpu.VMEM((D_MODEL, DN_CONV_CH + DN_V + ROW_PITCH_PAD), BF16),
                        pltpu.VMEM((D_MODEL, BA_PAD), BF16),
                        pltpu.VMEM((D_MODEL, SWA_Q + 2 * SWA_KV), BF16)],
        compiler_params=_cparams(("arbitrary", "arbitrary")),
        name="norm_in_proj_state",
    )(x, mod3, norm_w, w_in_t, conv_w, state)


INV_BASE_SHIFT = 2
SWA_BLOCKS_PER_STEP = 8
DN_UNITS_STATE = 2


def _unit_lower_inverses(a_mats, ri, ci, chunk_shift):
    def blocks(s):
        return (ri >> s) == (ci >> s)

    base = min(INV_BASE_SHIFT, chunk_shift)
    xs = [jnp.where(blocks(base), -a, 0.0) for a in a_mats]
    ts = [jnp.where(ri == ci, 1.0, n) for n in xs]
    for lvl in range(1, base):
        xs = [_dot(x, x) for x in xs]
        ts = [t + _dot(t, x) for t, x in zip(ts, xs)]
    for s in range(base, chunk_shift):
        sel = blocks(s + 1) & jnp.logical_not(blocks(s))
        ets = [_dot(jnp.where(sel, a, 0.0), t) for a, t in zip(a_mats, ts)]
        ts = [t - _dot(t, et) for t, et in zip(ts, ets)]
    return ts


def _dn_kernel(*refs, nb, chunk, carry, units):
    if carry:
        (qkv_ref, z_ref, ba_ref, alog_ref, dt_ref, nw_ref, o_ref, s_ref) = refs
        s0_ref = s_ref

        @pl.when(pl.program_id(1) == 0)
        def _():
            s_ref[...] = jnp.zeros_like(s_ref)
    else:
        (qkv_ref, z_ref, ba_ref, alog_ref, dt_ref, nw_ref, s0_ref, o_ref, s_ref) = refs

    ri = lax.broadcasted_iota(jnp.int32, (UNIT, UNIT), 0)
    ci = lax.broadcasted_iota(jnp.int32, (UNIT, UNIT), 1)
    shift = int(math.log2(chunk))
    same = (ri >> shift) == (ci >> shift)
    incl = same & (ri >= ci)
    strict = same & (ri > ci)

    ba = ba_ref[...]
    beta_full = _sigmoid(ba)
    g_full = -jnp.exp(alog_ref[...]) * _softplus(ba + dt_ref[...])
    masks = jnp.concatenate([jnp.where(incl, 1.0, 0.0), jnp.where(same, 1.0, 0.0)], axis=0).astype(BF16)
    gsums = [_dot_mask(masks, g_full[u * UNIT:(u + 1) * UNIT]) for u in range(units)]
    g_cum = [g[:UNIT] for g in gsums]
    g_tot = [g[UNIT:] for g in gsums]
    g_cum_t = [g.T for g in g_cum]

    probs = [(u, h) for u in range(units) for h in range(DN_HEADS)]
    n_p = len(probs)
    rs = lambda u: slice(u * UNIT, (u + 1) * UNIT)

    def head_cols(base):
        return [qkv_ref[rs(u), base + h * DN_DK:base + (h + 1) * DN_DK] for u, h in probs]

    q = head_cols(0)
    k = head_cols(DN_QK)
    v = head_cols(2 * DN_QK)
    gc = [g_cum[u][:, DN_HEADS + h:DN_HEADS + h + 1] for u, h in probs]
    gr = [g_cum_t[u][DN_HEADS + h:DN_HEADS + h + 1, :] for u, h in probs]
    gt = [g_tot[u][:, DN_HEADS + h:DN_HEADS + h + 1] for u, h in probs]
    bc = [beta_full[rs(u), h:h + 1] for u, h in probs]
    decay = [jnp.where(incl, jnp.exp(jnp.where(incl, gc[p] - gr[p], 0.0)), 0.0) for p in range(n_p)]
    e_g = [jnp.exp(gc[p]) for p in range(n_p)]
    kq = [_dot_nt(jnp.concatenate([k[p], q[p]], axis=0), k[p]) for p in range(n_p)]
    qk = [kq[p][UNIT:] * decay[p] for p in range(n_p)]
    a_mats = [jnp.where(strict, bc[p] * kq[p][:UNIT] * decay[p], 0.0) for p in range(n_p)]
    t_inv = _unit_lower_inverses(a_mats, ri, ci, shift)
    wvk = [_dot(t_inv[p], jnp.concatenate([v[p] * bc[p], k[p] * (bc[p] * e_g[p])], axis=1)) for p in range(n_p)]
    w_v = [w[:, :DN_DV] for w in wvk]
    w_k = [w[:, DN_DV:] for w in wvk]
    q_dec = [q[p] * e_g[p] for p in range(n_p)]
    k_tail_t = [(k[p] * jnp.exp(gt[p] - gc[p])).T for p in range(n_p)]
    c_dec = [jnp.exp(gt[p]) for p in range(n_p)]

    outs = {}
    if carry:
        state = [s_ref[0, h] for h in range(DN_HEADS)]
        for u in range(units):
            ps = [u * DN_HEADS + h for h in range(DN_HEADS)]
            r = [_dot(jnp.concatenate([w_k[p], q_dec[p]], axis=0), state[h]) for h, p in enumerate(ps)]
            uu = [w_v[p] - r[h][:UNIT] for h, p in enumerate(ps)]
            for h, p in enumerate(ps):
                outs[p] = r[h][UNIT:] + _dot(qk[p], uu[h])
            state = [state[h] * c_dec[p][0:1, :] + _dot(k_tail_t[p], uu[h]) for h, p in enumerate(ps)]
        for h in range(DN_HEADS):
            s_ref[0, h] = state[h]
    else:
        per_unit = nb // units
        for p, (u, h) in enumerate(probs):
            us, qs = [], []
            for s in range(per_unit):
                lo = s * chunk
                lhs = jnp.concatenate([w_k[p][lo:lo + chunk], q_dec[p][lo:lo + chunk]], axis=0)
                r = jnp.dot(lhs, s0_ref[u * per_unit + s, h], preferred_element_type=F32)
                us.append(w_v[p][lo:lo + chunk] - r[:chunk])
                qs.append(r[chunk:])
            uu = jnp.concatenate(us, axis=0)
            outs[p] = jnp.concatenate(qs, axis=0) + _dot(qk[p], uu)
            for s in range(per_unit):
                lo = s * chunk
                upd = jnp.dot(k_tail_t[p][:, lo:lo + chunk], uu[lo:lo + chunk], preferred_element_type=F32)
                s_ref[u * per_unit + s, h] = s0_ref[u * per_unit + s, h] * c_dec[p][lo:lo + 1, :] + upd

    for p, (u, h) in enumerate(probs):
        zz = z_ref[rs(u), h * DN_DV:(h + 1) * DN_DV]
        o_ref[rs(u), h * DN_DV:(h + 1) * DN_DV] = _rms(outs[p], nw_ref[...]) * _silu(zz)


def _dn_call(qkv, z, ba, alog_row, dt_row, norm_w, bsz, seq, state):
    rows = DN_UNITS_STATE * UNIT
    nb = rows // seq
    row_map = lambda i, j: (i, 0)
    const = lambda i, j: (0, 0)
    per_seq = lambda i, j: (i, 0, 0, 0)
    return pl.pallas_call(
        functools.partial(_dn_kernel, nb=nb, chunk=seq, carry=False, units=DN_UNITS_STATE),
        grid=(bsz // nb, 1),
        in_specs=[pl.BlockSpec((rows, DN_CONV_CH), row_map),
                  pl.BlockSpec((rows, DN_V), row_map),
                  pl.BlockSpec((rows, BA_PAD), row_map),
                  pl.BlockSpec((1, BA_PAD), const),
                  pl.BlockSpec((1, BA_PAD), const),
                  pl.BlockSpec((1, DN_DV), const),
                  pl.BlockSpec((nb, DN_HEADS, DN_DK, DN_DV), per_seq)],
        out_specs=[pl.BlockSpec((rows, DN_V), row_map),
                   pl.BlockSpec((nb, DN_HEADS, DN_DK, DN_DV), per_seq)],
        out_shape=[jax.ShapeDtypeStruct((bsz * seq, DN_V), F32),
                   jax.ShapeDtypeStruct((bsz, DN_HEADS, DN_DK, DN_DV), F32)],
        compiler_params=_cparams(("arbitrary", "arbitrary")),
        name="gated_deltanet_state",
    )(qkv, z, ba, alog_row, dt_row, norm_w, state)


MIX_SUB_TILES = 2
SUB_ROWS = 2 * UNIT
MIX_ROWS = MIX_SUB_TILES * SUB_ROWS
SAMPLE_IN_SEQS = 64
SAMPLE_POST_SEQS = 32


def _mix_kernel(x_ref, mod_ref, nw_ref, w_ref, cw_ref, alog_ref, dt_ref, dnw_ref,
                sq_ref, sk_ref, sv_ref, tail_ref, o_ref, s_ref,
                h_scr, wdn_ref, wba_ref, wsw_ref, qkv_scr, z_scr, ba_scr):
    _in_kernel(x_ref, mod_ref, nw_ref, w_ref, cw_ref,
               qkv_scr, z_scr, ba_scr, sq_ref, sk_ref, sv_ref, tail_ref, h_scr, wdn_ref, wba_ref, wsw_ref,
               nb=1, lt=SUB_ROWS, carry=True, n_sub=MIX_SUB_TILES)
    _dn_kernel(qkv_scr, z_scr, ba_scr, alog_ref, dt_ref, dnw_ref, o_ref, s_ref,
               nb=1, chunk=UNIT, carry=True, units=MIX_ROWS // UNIT)


def _mix_call(x, mod3, norm_w, w_in_t, conv_w, alog_row, dt_row, dn_norm_w):
    bsz, seq, _ = x.shape
    nt = seq // MIX_ROWS
    n_tok = bsz * seq
    row_map = lambda i, j: (i * nt + j, 0)
    const = lambda i, j: (0, 0)
    per_seq = lambda i, j: (i, 0, 0)
    swa_cols = (SWA_Q, SWA_KV, SWA_KV)
    return pl.pallas_call(
        _mix_kernel,
        grid=(bsz, nt),
        in_specs=[pl.BlockSpec((1, MIX_ROWS, D_MODEL), lambda i, j: (i, j, 0)),
                  pl.BlockSpec((1, 6, D_MODEL), per_seq),
                  pl.BlockSpec((1, D_MODEL), const),
                  pl.BlockSpec(w_in_t.shape, const, pipeline_mode=RESIDENT),
                  pl.BlockSpec((DN_CONV, DN_CONV_CH), const),
                  pl.BlockSpec((1, BA_PAD), const),
                  pl.BlockSpec((1, BA_PAD), const),
                  pl.BlockSpec((1, DN_DV), const)],
        out_specs=[pl.BlockSpec((MIX_ROWS, n), row_map) for n in swa_cols]
        + [pl.BlockSpec((1, SUBLANES, DN_CONV_CH), per_seq),
           pl.BlockSpec((MIX_ROWS, DN_V), row_map),
           pl.BlockSpec((1, DN_HEADS, DN_DK, DN_DV), lambda i, j: (i, 0, 0, 0))],
        out_shape=[jax.ShapeDtypeStruct((n_tok, n), F32) for n in swa_cols]
        + [jax.ShapeDtypeStruct((bsz, SUBLANES, DN_CONV_CH), F32),
           jax.ShapeDtypeStruct((n_tok, DN_V), F32),
           jax.ShapeDtypeStruct((bsz, DN_HEADS, DN_DK, DN_DV), F32)],
        scratch_shapes=[pltpu.VMEM((MIX_SUB_TILES, SUB_ROWS, D_MODEL + ROW_PITCH_PAD), BF16),
                        pltpu.VMEM((D_MODEL, DN_CONV_CH + DN_V + ROW_PITCH_PAD), BF16),
                        pltpu.VMEM((D_MODEL, BA_PAD), BF16),
                        pltpu.VMEM((D_MODEL, SWA_Q + 2 * SWA_KV), BF16),
                        pltpu.VMEM((MIX_ROWS, DN_CONV_CH), F32),
                        pltpu.VMEM((MIX_ROWS, DN_V + ROW_PITCH_PAD), F32),
                        pltpu.VMEM((MIX_ROWS, BA_PAD), F32)],
        compiler_params=_cparams(("arbitrary", "arbitrary")),
        name="in_proj_deltanet_carry",
    )(x, mod3, norm_w, w_in_t, conv_w, alog_row, dt_row, dn_norm_w)


def _bucket_table():
    i = np.arange(WINDOW, dtype=np.int64)[:, None]
    j = np.arange(2 * WINDOW, dtype=np.int64)[None, :]
    d = np.maximum(i + WINDOW - j, 0)
    exact = N_BUCKETS // 2
    logv = (np.log(np.maximum(d, 1).astype(np.float32) / np.float32(exact)).astype(np.float32)
            / np.float32(math.log(MAX_DISTANCE / exact)))
    large = np.minimum(exact + (logv * np.float32(N_BUCKETS - exact)).astype(np.int32), N_BUCKETS - 1)
    return np.where(d < exact, d, large).astype(np.int32)


def _sample_bucket_table(bucket):
    out = bucket[:SUBLANES].copy()
    out[:, 2 * WINDOW - SUBLANES:] = bucket[:SUBLANES, WINDOW:WINDOW + SUBLANES]
    return out


def _bias_lookup(rb_ref, bucket, head):
    acc = jnp.zeros(bucket.shape, F32)
    for b in range(N_BUCKETS):
        acc = jnp.where(bucket == b, rb_ref[b, head], acc)
    return acc


def _bias_kernel(rb_ref, bucket_t_ref, bucket_ref, ot_ref, os_ref):
    bucket_t = bucket_t_ref[...]
    kj = lax.broadcasted_iota(jnp.int32, bucket_t.shape, 0)
    qi = lax.broadcasted_iota(jnp.int32, bucket_t.shape, 1)
    dist = qi + WINDOW - kj
    valid = (dist >= 0) & (dist < WINDOW)
    for hk in range(SWA_KV_HEADS):
        for par in range(2):
            for st in range(SWA_GROUP // 2):
                head = hk * SWA_GROUP + 2 * st + par
                gen = jnp.where(valid, _bias_lookup(rb_ref, bucket_t, head) * LOG2_E, NEG_INF)
                ot_ref[1, hk, par, :, st * WINDOW:(st + 1) * WINDOW] = gen
                ot_ref[0, hk, par, :, st * WINDOW:(st + 1) * WINDOW] = jnp.where(kj >= WINDOW, gen, NEG_INF)
    bucket = bucket_ref[...]
    qi = lax.broadcasted_iota(jnp.int32, bucket.shape, 0)
    col = lax.broadcasted_iota(jnp.int32, bucket.shape, 1)
    kj = jnp.where(col < WINDOW, col, col - (WINDOW - SUBLANES))
    dist = qi + WINDOW - kj
    valid = (dist >= 0) & (dist < WINDOW) & ((col < WINDOW) | (col >= 2 * WINDOW - SUBLANES))
    for head in range(SWA_HEADS):
        os_ref[head] = jnp.where(valid, _bias_lookup(rb_ref, bucket, head), NEG_INF)


BIAS_T_SHAPE = (2, SWA_KV_HEADS, 2, 2 * WINDOW, (SWA_GROUP // 2) * WINDOW)
BIAS_S_SHAPE = (SWA_HEADS, SUBLANES, 2 * WINDOW)


def _bias_call(rel_bias):
    bucket = _bucket_table()
    return pl.pallas_call(
        _bias_kernel,
        in_specs=[pl.BlockSpec(memory_space=pltpu.SMEM),
                  pl.BlockSpec((2 * WINDOW, WINDOW), lambda: (0, 0)),
                  pl.BlockSpec((SUBLANES, 2 * WINDOW), lambda: (0, 0))],
        out_specs=[pl.BlockSpec(BIAS_T_SHAPE, lambda: (0,) * len(BIAS_T_SHAPE)),
                   pl.BlockSpec(BIAS_S_SHAPE, lambda: (0,) * len(BIAS_S_SHAPE))],
        out_shape=[jax.ShapeDtypeStruct(BIAS_T_SHAPE, F32), jax.ShapeDtypeStruct(BIAS_S_SHAPE, F32)],
        name="swa_rel_bias_table",
    )(rel_bias, jnp.asarray(np.ascontiguousarray(bucket.T)), jnp.asarray(_sample_bucket_table(bucket)))


def _softmax_sink_parts(s, sink):
    m = jnp.maximum(jnp.max(s, axis=0, keepdims=True), sink)
    p = jnp.exp2(s - m)
    return p, 1.0 / (jnp.sum(p, axis=0, keepdims=True) + jnp.exp2(sink - m))


def _half_lane_variants(full, hk, lo_half):
    rolled = pltpu.roll(full, SWA_HD, axis=1)
    low_src, high_src = (full, rolled) if hk == 0 else (rolled, full)
    return jnp.where(lo_half, low_src, 0.0), jnp.where(lo_half, 0.0, high_src)


def _swa_prompt_kernel(sink_ref, q_ref, kp_ref, kc_ref, vp_ref, vc_ref, bias_ref, o_ref, *, n_blk):
    step = pl.program_id(1)
    lo_half = lax.broadcasted_iota(jnp.int32, (1, 2 * SWA_HD), 1) < SWA_HD
    lo_rows = lax.broadcasted_iota(jnp.int32, (2 * SWA_HD, 1), 0) < SWA_HD
    q = _bf(q_ref[...] * (SWA_HD ** -0.5 * LOG2_E))
    keys = jnp.concatenate([kp_ref[...], kc_ref[...]], axis=0)
    vals = jnp.concatenate([vp_ref[...], vc_ref[...]], axis=0)
    k_var = [[_bf(t) for t in _half_lane_variants(keys, hk, lo_half)] for hk in range(SWA_KV_HEADS)]
    v_var_t = [[_bf(t.T) for t in _half_lane_variants(vals, hk, lo_half)] for hk in range(SWA_KV_HEADS)]
    n_stack = SWA_GROUP // 2
    sinks = [[jnp.concatenate([jnp.full((1, WINDOW), sink_ref[hk * SWA_GROUP + 2 * st + par] * LOG2_E, F32)
                               for st in range(n_stack)], axis=1) for par in range(2)]
             for hk in range(SWA_KV_HEADS)]
    nt_dims = (((1,), (1,)), ((), ()))

    def scores(b):
        rows = slice(b * WINDOW, (b + 1) * WINDOW)
        win = slice(b * WINDOW, (b + 2) * WINDOW)
        variant = jnp.where(step == 0, 0, 1) if b == 0 else 1
        out = []
        for hk in range(SWA_KV_HEADS):
            q2 = jnp.concatenate([q[rows, (hk * n_stack + st) * 2 * SWA_HD:(hk * n_stack + st + 1) * 2 * SWA_HD]
                                  for st in range(n_stack)], axis=0)
            out.append([lax.dot_general(k_var[hk][par][win], q2, nt_dims, preferred_element_type=F32)
                        + bias_ref[variant, hk, par] for par in range(2)])
        return out

    s_next = scores(0)
    for b in range(n_blk):
        s_cur = s_next
        if b + 1 < n_blk:
            s_next = scores(b + 1)
        rows = slice(b * WINDOW, (b + 1) * WINDOW)
        win = slice(b * WINDOW, (b + 2) * WINDOW)
        for hk in range(SWA_KV_HEADS):
            parts = [_softmax_sink_parts(s_cur[hk][par], sinks[hk][par]) for par in range(2)]
            o_t = (jnp.dot(v_var_t[hk][0][:, win], _bf(parts[0][0]), preferred_element_type=F32)
                   + jnp.dot(v_var_t[hk][1][:, win], _bf(parts[1][0]), preferred_element_type=F32))
            o_t = o_t * jnp.where(lo_rows, parts[0][1], parts[1][1])
            for st in range(n_stack):
                lo = (hk * n_stack + st) * 2 * SWA_HD
                o_ref[rows, lo:lo + 2 * SWA_HD] = o_t[:, st * WINDOW:(st + 1) * WINDOW].T


def _swa_prompt_call(sinks, sq, sk, sv, bias, bsz, seq):
    n_blk = SWA_BLOCKS_PER_STEP
    tile = n_blk * WINDOW
    nt = seq // tile
    cur = lambda b, i: (b * nt + i, 0)
    prv = lambda b, i: (b * nt * n_blk + jnp.maximum(i * n_blk - 1, 0), 0)
    return pl.pallas_call(
        functools.partial(_swa_prompt_kernel, n_blk=n_blk),
        grid=(bsz, nt),
        in_specs=[pl.BlockSpec(memory_space=pltpu.SMEM),
                  pl.BlockSpec((tile, SWA_Q), cur),
                  pl.BlockSpec((WINDOW, SWA_KV), prv),
                  pl.BlockSpec((tile, SWA_KV), cur),
                  pl.BlockSpec((WINDOW, SWA_KV), prv),
                  pl.BlockSpec((tile, SWA_KV), cur),
                  pl.BlockSpec(BIAS_T_SHAPE, lambda b, i: (0,) * len(BIAS_T_SHAPE))],
        out_specs=pl.BlockSpec((tile, SWA_Q), cur),
        out_shape=jax.ShapeDtypeStruct((bsz * seq, SWA_Q), F32),
        compiler_params=_cparams(("arbitrary", "arbitrary")),
        name="swa_banded",
    )(sinks, sq, sk, sk, sv, sv, bias)


def _swa_sample_kernel(sink_ref, q_ref, kn_ref, vn_ref, kc_ref, vc_ref, bias_ref, o_ref, ko_ref, vo_ref, *, nb, lt):
    lane = lax.broadcasted_iota(jnp.int32, (1, 2 * SWA_HD), 1)
    lo_half = lane < SWA_HD
    new_lanes = lane >= WINDOW - lt
    q_all = q_ref[...] * (SWA_HD ** -0.5)
    kn_all_t = kn_ref[...].T
    vn_all_t = vn_ref[...].T
    bias = jnp.concatenate([bias_ref[h, 0:lt, :] for h in range(SWA_HEADS)], axis=0)
    sink = jnp.concatenate([jnp.full((lt, 1), sink_ref[h], F32) for h in range(SWA_HEADS)], axis=0)
    nt_dims = (((1,), (1,)), ((), ()))

    def to_half(tile, src_half, dst_half):
        return tile if src_half == dst_half else pltpu.roll(tile, SWA_HD, axis=1)

    lhs, keys, vals = [], [], []
    for s in range(nb):
        rows = slice(s * lt, (s + 1) * lt)
        to_end = (WINDOW - lt - s * lt) % WINDOW
        for c_ref, n_all_t, out_ref, acc in ((kc_ref, kn_all_t, ko_ref, keys), (vc_ref, vn_all_t, vo_ref, vals)):
            cached = c_ref[s]
            new = n_all_t if to_end == 0 else pltpu.roll(n_all_t, to_end, axis=1)
            out_ref[s] = jnp.where(new_lanes, new, pltpu.roll(cached, WINDOW - lt, axis=1))
            acc.append(_bf(jnp.concatenate([cached, jnp.where(new_lanes, new, 0.0)], axis=1)))
        q = q_all[rows]
        tiles = []
        for h in range(SWA_HEADS):
            hk = h // SWA_GROUP
            t = to_half(q[:, (h // 2) * 2 * SWA_HD:(h // 2 + 1) * 2 * SWA_HD], h % 2, hk)
            tiles.append(jnp.where(lo_half if hk == 0 else jnp.logical_not(lo_half), t, 0.0))
        lhs.append(_bf(jnp.concatenate(tiles, axis=0)))
    scores = [jnp.dot(lhs[s], keys[s], preferred_element_type=F32) + bias for s in range(nb)]
    m = [jnp.maximum(jnp.max(sc, axis=-1, keepdims=True), sink) for sc in scores]
    p = [jnp.exp(sc - mm) for sc, mm in zip(scores, m)]
    rinv = [1.0 / (jnp.sum(pp, axis=-1, keepdims=True) + jnp.exp(sink - mm)) for pp, mm in zip(p, m)]
    res = [lax.dot_general(_bf(p[s]), vals[s], nt_dims, preferred_element_type=F32) * rinv[s] for s in range(nb)]
    rows_out = []
    for s in range(nb):
        tiles = []
        for pair in range(SWA_HEADS // 2):
            hk = (2 * pair) // SWA_GROUP
            low = to_half(res[s][(2 * pair) * lt:(2 * pair + 1) * lt], hk, 0)
            high = to_half(res[s][(2 * pair + 1) * lt:(2 * pair + 2) * lt], hk, 1)
            tiles.append(jnp.where(lo_half, low, high))
        rows_out.append(jnp.concatenate(tiles, axis=1))
    o_ref[...] = jnp.concatenate(rows_out, axis=0)


def _swa_sample_call(sinks, sq, sk, sv, cache_k, cache_v, bias, bsz, seq):
    assert seq == SUBLANES, "the sample bias table places one sublane tile of new keys"
    nb = UNIT // seq
    rows = lambda i: (i, 0)
    seqs = lambda i: (i, 0, 0)
    return pl.pallas_call(
        functools.partial(_swa_sample_kernel, nb=nb, lt=seq),
        grid=(bsz // nb,),
        in_specs=[pl.BlockSpec(memory_space=pltpu.SMEM),
                  pl.BlockSpec((UNIT, SWA_Q), rows),
                  pl.BlockSpec((UNIT, SWA_KV), rows),
                  pl.BlockSpec((UNIT, SWA_KV), rows),
                  pl.BlockSpec((nb, SWA_KV, WINDOW), seqs),
                  pl.BlockSpec((nb, SWA_KV, WINDOW), seqs),
                  pl.BlockSpec(BIAS_S_SHAPE, lambda i: (0,) * len(BIAS_S_SHAPE))],
        out_specs=[pl.BlockSpec((UNIT, SWA_Q), rows),
                   pl.BlockSpec((nb, SWA_KV, WINDOW), seqs),
                   pl.BlockSpec((nb, SWA_KV, WINDOW), seqs)],
        out_shape=[jax.ShapeDtypeStruct((bsz * seq, SWA_Q), F32),
                   jax.ShapeDtypeStruct((bsz, SWA_KV, WINDOW), F32),
                   jax.ShapeDtypeStruct((bsz, SWA_KV, WINDOW), F32)],
        compiler_params=_cparams(("arbitrary",)),
        name="swa_cached",
    )(sinks, sq, sk, sv, cache_k, cache_v, bias)


def _post_kernel(*refs, nb, lt, carry, n_sub):
    if carry:
        (odn_ref, oswa_ref, x_ref, mod_ref, wout_ref, nfw_ref, wup_ref, cw_ref, cb_ref, wdn_ref, fnw_ref,
         y_ref, fbuf_ref, h_scr, x1_scr, acc_scr) = refs
        prev_ref = fbuf_ref

        @pl.when(pl.program_id(1) == 0)
        def _():
            fbuf_ref[...] = jnp.zeros_like(fbuf_ref)
    else:
        (odn_ref, oswa_ref, x_ref, mod_ref, wout_ref, nfw_ref, wup_ref, cw_ref, cb_ref, wdn_ref, fnw_ref,
         prev_ref, y_ref, fbuf_ref, h_scr, x1_scr, acc_scr) = refs
    rows = nb * lt
    n_chunks = D_FF // FFN_CHUNK

    def col_slices(c):
        return [slice(base + c * FFN_CHUNK, base + (c + 1) * FFN_CHUNK) for base in (0, D_FF)]

    def prologue(sub):
        rr = slice(sub * rows, (sub + 1) * rows)
        attn = (jnp.dot(_bf(odn_ref[rr, :]), wout_ref[0:DN_V, :], preferred_element_type=F32)
                + jnp.dot(_bf(oswa_ref[rr, :]), wout_ref[DN_V:, :], preferred_element_type=F32))
        x = x_ref[:, sub * lt:(sub + 1) * lt, :].reshape(rows, D_MODEL)
        x1 = x + _rows(mod_ref[:, 2:3, :], nb, lt) * attn
        x1_scr[sub] = x1
        h = (_rms(x1, nfw_ref[...]) * (1.0 + _rows(mod_ref[:, 4:5, :], nb, lt))
             + _rows(mod_ref[:, 3:4, :], nb, lt))
        h_scr[sub] = _bf(h)
        acc_scr[sub] = jnp.zeros((rows, D_MODEL), F32)

    def up_proj(sub, c):
        return [jnp.dot(h_scr[sub], wup_ref[:, cols], preferred_element_type=F32) for cols in col_slices(c)]

    def chunk(sub, c, u_cur):
        halves = []
        for u, cols in zip(u_cur, col_slices(c)):
            prev = prev_ref[:, :, cols]
            halves.append(_causal_conv(u, prev, cw_ref[:, cols], nb, lt) + cb_ref[:, cols])
            fbuf_ref[:, :, cols] = _last_rows(u, nb, lt, FFN_CONV - 1)
        act = _silu(halves[0]) * halves[1]
        acc_scr[sub] += jnp.dot(_bf(act), wdn_ref[c * FFN_CHUNK:(c + 1) * FFN_CHUNK, :],
                                preferred_element_type=F32)

    def epilogue(sub):
        x2 = x1_scr[sub] + _rows(mod_ref[:, 5:6, :], nb, lt) * acc_scr[sub]
        y_ref[:, sub * lt:(sub + 1) * lt, :] = _rms(x2, fnw_ref[...]).reshape(nb, lt, D_MODEL)

    prologue(0)
    for sub in range(n_sub):
        u_next = up_proj(sub, 0)
        for c in range(n_chunks):
            u_cur = u_next
            if c + 1 < n_chunks:
                u_next = up_proj(sub, c + 1)
            chunk(sub, c, u_cur)
            if sub + 1 < n_sub and c == n_chunks - 1 - POST_LOOKAHEAD_CHUNKS:
                prologue(sub + 1)
        epilogue(sub)


def _post_call(o_dn, o_swa, x, mod3, w_out, norm_ffn_w, w_up, conv_w, conv_b, w_down, final_w, nb, lt, state=None):
    carry = state is None
    n_sub = POST_SUB_TILES if carry else 1
    bsz, seq, _ = x.shape
    nt = seq // (lt * n_sub)
    rows = nb * lt
    tile = rows * n_sub
    n_state = _state_rows(nb, lt, FFN_CONV - 1)
    row_map = lambda i, j: (i * nt + j, 0)
    const = lambda i, j: (0, 0)
    in_specs = [pl.BlockSpec((tile, DN_V), row_map),
                pl.BlockSpec((tile, SWA_Q), row_map),
                pl.BlockSpec((nb, lt * n_sub, D_MODEL), lambda i, j: (i, j, 0)),
                pl.BlockSpec((nb, 6, D_MODEL), lambda i, j: (i, 0, 0)),
                pl.BlockSpec((D_MODEL, D_MODEL), const, pipeline_mode=RESIDENT),
                pl.BlockSpec((1, D_MODEL), const),
                pl.BlockSpec(w_up.shape, const, pipeline_mode=RESIDENT),
                pl.BlockSpec((FFN_CONV, 2 * D_FF), const),
                pl.BlockSpec((1, 2 * D_FF), const),
                pl.BlockSpec((D_FF, D_MODEL), const, pipeline_mode=RESIDENT),
                pl.BlockSpec((1, D_MODEL), const)]
    args = [o_dn, o_swa, x, mod3, w_out, norm_ffn_w, w_up, conv_w, conv_b, w_down, final_w]
    if not carry:
        in_specs.append(pl.BlockSpec((nb, FFN_CONV - 1, 2 * D_FF), lambda i, j: (i, 0, 0)))
        args.append(state)
    return pl.pallas_call(
        functools.partial(_post_kernel, nb=nb, lt=lt, carry=carry, n_sub=n_sub),
        grid=(bsz // nb, nt),
        in_specs=in_specs,
        out_specs=[pl.BlockSpec((nb, lt * n_sub, D_MODEL), lambda i, j: (i, j, 0)),
                   pl.BlockSpec((nb, n_state, 2 * D_FF), lambda i, j: (i, 0, 0))],
        out_shape=[jax.ShapeDtypeStruct((bsz, seq, D_MODEL), F32),
                   jax.ShapeDtypeStruct((bsz, n_state, 2 * D_FF), F32)],
        scratch_shapes=[pltpu.VMEM((n_sub, rows, D_MODEL), BF16),
                        pltpu.VMEM((n_sub, rows, D_MODEL), F32),
                        pltpu.VMEM((n_sub, rows, D_MODEL), F32)],
        compiler_params=_cparams(("arbitrary", "arbitrary")),
        name="out_proj_convffn_carry" if carry else "out_proj_convffn_state",
    )(*args)


def _pad_row(vec, offset):
    return jnp.zeros((1, BA_PAD), F32).at[0, offset:offset + vec.shape[0]].set(vec)


def kernel(x_prompt, x_sample, state_dn_conv, state_dn_ssm, cache_swa_k, cache_swa_v, state_ffn_conv, c_prompt, c_sample, rel_bias, final_norm_w, w_ada, b_ada, norm_mix_w, w_in, dn_conv_w, dn_A_log, dn_dt_bias, dn_norm_w, swa_sinks, w_out, norm_ffn_w, ffn_w_up, ffn_conv_w, ffn_conv_b, ffn_w_down):
    bp, lp, _ = x_prompt.shape
    bs, ls, _ = x_sample.shape
    layer = 0

    w_in_t = jnp.transpose(w_in[layer])
    w_out_b = w_out[layer].astype(BF16)
    w_up_b = ffn_w_up[layer].astype(BF16)
    w_dn_b = ffn_w_down[layer].astype(BF16)
    alog_row = _pad_row(dn_A_log[layer], DN_HEADS)
    dt_row = _pad_row(dn_dt_bias[layer], DN_HEADS)
    row = lambda v: v.reshape(1, -1)

    n_c = bp + bs
    n_c_pad = -(-n_c // SUBLANES) * SUBLANES
    c_all = jnp.pad(jnp.concatenate([c_sample, c_prompt], axis=0), ((0, n_c_pad - n_c), (0, 0)))
    mod_s, mod_p = _mod_call(c_all, w_ada[layer], row(b_ada[layer]), bs, bp)

    bias_t, bias_s = _bias_call(rel_bias)
    sinks = swa_sinks[layer]

    def post(o_dn, o_swa, x, mod3, nb, lt, state=None):
        return _post_call(o_dn, o_swa, x, mod3, w_out_b, row(norm_ffn_w[layer]), w_up_b, ffn_conv_w[layer],
                          row(ffn_conv_b[layer]), w_dn_b, row(final_norm_w), nb, lt, state)

    dn_args = (alog_row, dt_row, row(dn_norm_w[layer]))

    sq, sk, sv, p_tail, o_dn, p_ssm = _mix_call(x_prompt, mod_p, row(norm_mix_w[layer]), w_in_t,
                                                dn_conv_w[layer], *dn_args)
    o_swa = _swa_prompt_call(sinks, sq, sk, sv, bias_t, bp, lp)
    y_prompt, p_ffn_tail = post(o_dn, o_swa, x_prompt, mod_p, 1, SUB_ROWS)
    p_dn_conv = p_tail[:, SUBLANES - (DN_CONV - 1):]
    p_fbuf = p_ffn_tail[:, SUBLANES - (FFN_CONV - 1):]
    last_window = lambda t: t.reshape(bp, lp, SWA_KV)[:, lp - WINDOW:].reshape(bp, WINDOW, SWA_KV_HEADS, SWA_HD)
    p_swa_k = last_window(sk)
    p_swa_v = last_window(sv)

    cache_t = lambda c: jnp.transpose(c[layer].reshape(bs, WINDOW, SWA_KV), (0, 2, 1))
    cache_from_t = lambda c: jnp.transpose(c, (0, 2, 1)).reshape(bs, WINDOW, SWA_KV_HEADS, SWA_HD)
    qkv_s, z_s, ba_s, sq_s, sk_s, sv_s, s_dn_conv = _in_call(
        x_sample, mod_s, row(norm_mix_w[layer]), w_in_t, dn_conv_w[layer], SAMPLE_IN_SEQS, state_dn_conv[layer])
    o_dn_s, s_ssm = _dn_call(qkv_s, z_s, ba_s, *dn_args, bs, ls, state=state_dn_ssm[layer])
    o_swa_s, s_k, s_v = _swa_sample_call(sinks, sq_s, sk_s, sv_s,
                                         cache_t(cache_swa_k), cache_t(cache_swa_v), bias_s, bs, ls)
    y_sample, s_fbuf = post(o_dn_s, o_swa_s, x_sample, mod_s, SAMPLE_POST_SEQS, ls, state=state_ffn_conv[layer])

    return (y_prompt, y_sample, p_dn_conv[None], s_dn_conv[None], p_ssm[None], s_ssm[None],
            p_swa_k[None], cache_from_t(s_k)[None],
            p_swa_v[None], cache_from_t(s_v)[None],
            p_fbuf[None], s_fbuf[None])
```

```python
import functools
import math

import numpy as np
import jax
import jax.numpy as jnp
from jax import lax
from jax.experimental import pallas as pl
from jax.experimental.pallas import tpu as pltpu

F32 = jnp.float32
BF16 = jnp.bfloat16

D_MODEL = 1024
DN_HEADS = 4
DN_DK = 128
DN_DV = 128
DN_CONV = 4
SWA_HEADS = 8
SWA_KV_HEADS = 2
SWA_GROUP = SWA_HEADS // SWA_KV_HEADS
SWA_HD = 64
WINDOW = 128
N_BUCKETS = 32
MAX_DISTANCE = 128
D_FF = 2816
FFN_CONV = 3
EPS = 1e-6
NEG_INF = -1e30
LOG2_E = 1.0 / math.log(2.0)

DN_QK = DN_HEADS * DN_DK
DN_V = DN_HEADS * DN_DV
DN_CONV_CH = 2 * DN_QK + DN_V
SWA_Q = SWA_HEADS * SWA_HD
SWA_KV = SWA_KV_HEADS * SWA_HD
BA_PAD = 128
SUBLANES = 8
UNIT = 128
POST_SUB_TILES = 2
POST_LOOKAHEAD_CHUNKS = 3
FFN_CHUNK = 256
ROW_PITCH_PAD = 128
VMEM_LIMIT = 56 * 1024 * 1024
RESIDENT = pl.Buffered(1)


def _cparams(sem):
    return pltpu.CompilerParams(dimension_semantics=sem, vmem_limit_bytes=VMEM_LIMIT)


def _bf(x):
    return x.astype(BF16)


def _dot(a, b):
    return jnp.dot(_bf(a), _bf(b), preferred_element_type=F32)


def _dot_nt(a, b):
    return lax.dot_general(_bf(a), _bf(b), (((1,), (1,)), ((), ())), preferred_element_type=F32)


def _dot_mask(m, x):
    hi = _bf(x)
    r = x - hi.astype(F32)
    mid = _bf(r)
    lo = _bf(r - mid.astype(F32))
    d = functools.partial(jnp.dot, preferred_element_type=F32)
    return d(m, hi) + (d(m, mid) + d(m, lo))


def _sigmoid(x):
    return 1.0 / (1.0 + jnp.exp(-x))


def _silu(x):
    return x * _sigmoid(x)


def _softplus(x):
    return jnp.maximum(x, 0.0) + jnp.log1p(jnp.exp(-jnp.abs(x)))


def _rms(x, w):
    ms = jnp.mean(x * x, axis=-1, keepdims=True)
    return x * lax.rsqrt(ms + EPS) * w


def _l2norm(t):
    return t * lax.rsqrt(jnp.sum(t * t, axis=-1, keepdims=True) + EPS)


def _rows(m3, nb, lt):
    return jnp.broadcast_to(m3, (nb, lt, m3.shape[-1])).reshape(nb * lt, m3.shape[-1])


def _causal_conv(x, prev, w, nb, lt):
    width = w.shape[0]
    rows, ch = x.shape
    if nb == 1 and lt > SUBLANES:
        tiles = jnp.concatenate([prev, x.reshape(lt // SUBLANES, SUBLANES, ch)], axis=0)
        sub = lax.broadcasted_iota(jnp.int32, (1, SUBLANES, 1), 1)
        out = tiles[1:] * w[width - 1:width, :]
        for j in range(1, width):
            rot = pltpu.roll(tiles, j, axis=1)
            out = out + jnp.where(sub >= j, rot[1:], rot[:-1]) * w[width - 1 - j:width - j, :]
        return out.reshape(rows, ch)
    tmod = lax.broadcasted_iota(jnp.int32, (rows, 1), 0) & (lt - 1)
    out = x * w[width - 1:width, :]
    for j in range(1, width):
        sh = pltpu.roll(x, j, axis=0)
        for t in range(j):
            p = width - 1 - j + t
            sh = jnp.where(tmod == t, _rows(prev[:, p:p + 1, :], nb, lt), sh)
        out = out + sh * w[width - 1 - j:width - j, :]
    return out


def _last_rows(x, nb, lt, n_state):
    ch = x.shape[-1]
    if nb == 1 and lt > SUBLANES:
        return x[lt - SUBLANES:].reshape(1, SUBLANES, ch)
    return x.reshape(nb, lt, ch)[:, lt - n_state:, :]


def _state_rows(nb, lt, n_state):
    return SUBLANES if (nb == 1 and lt > SUBLANES) else n_state


MOD_PARTS = 6


def _mod_kernel(c_ref, w_ref, b_ref, os_ref, op_ref, *, n_s, n_p):
    part = pl.program_id(0)
    res = _dot(_silu(c_ref[...]), w_ref[...]) + b_ref[...]
    for k in range(MOD_PARTS):
        @pl.when(part == k)
        def _():
            os_ref[:, k, :] = res[:n_s]
            op_ref[:, k, :] = res[n_s:n_s + n_p]


def _mod_call(c_all, w_ada, b_ada, n_s, n_p):
    rows = c_all.shape[0]
    return pl.pallas_call(
        functools.partial(_mod_kernel, n_s=n_s, n_p=n_p),
        grid=(MOD_PARTS,),
        in_specs=[pl.BlockSpec((rows, D_MODEL), lambda k: (0, 0)),
                  pl.BlockSpec((D_MODEL, D_MODEL), lambda k: (0, k)),
                  pl.BlockSpec((1, D_MODEL), lambda k: (0, k))],
        out_specs=[pl.BlockSpec((n_s, MOD_PARTS, D_MODEL), lambda k: (0, 0, 0)),
                   pl.BlockSpec((n_p, MOD_PARTS, D_MODEL), lambda k: (0, 0, 0))],
        out_shape=[jax.ShapeDtypeStruct((n_s, MOD_PARTS, D_MODEL), F32),
                   jax.ShapeDtypeStruct((n_p, MOD_PARTS, D_MODEL), F32)],
        compiler_params=_cparams(("arbitrary",)),
        name="adaln_mod",
    )(c_all, w_ada, b_ada)


IN_SPLIT = (DN_CONV_CH, DN_V, BA_PAD, SWA_Q, SWA_KV, SWA_KV)


IN_LOOKAHEAD_SLABS = 1
W_PREP_ROWS = 256
IN_SLAB = 2 * DN_DK


def _in_kernel(*refs, nb, lt, carry, n_sub):
    if carry:
        (x_ref, mod_ref, nw_ref, w_ref, cw_ref,
         qkv_ref, z_ref, ba_ref, sq_ref, sk_ref, sv_ref, tail_ref, h_scr, wdn_ref, wba_ref, wsw_ref) = refs
        prev_ref = tail_ref

        @pl.when(pl.program_id(1) == 0)
        def _():
            tail_ref[...] = jnp.zeros_like(tail_ref)
    else:
        (x_ref, mod_ref, nw_ref, w_ref, cw_ref, prev_ref,
         qkv_ref, z_ref, ba_ref, sq_ref, sk_ref, sv_ref, tail_ref, h_scr, wdn_ref, wba_ref, wsw_ref) = refs

    @pl.when((pl.program_id(0) == 0) & (pl.program_id(1) == 0))
    def _():
        ba_lo = DN_CONV_CH + DN_V
        n_ba = 2 * DN_HEADS
        for r in range(0, ba_lo, W_PREP_ROWS):
            wdn_ref[:, r:r + W_PREP_ROWS] = _bf(w_ref[r:r + W_PREP_ROWS, :].T)
        row = lax.broadcasted_iota(jnp.int32, (BA_PAD, 1), 0)
        wba_ref[...] = _bf(jnp.where(row < n_ba, w_ref[ba_lo:ba_lo + BA_PAD, :], 0.0).T)
        for r in range(0, SWA_Q + 2 * SWA_KV, W_PREP_ROWS):
            wsw_ref[:, r:r + W_PREP_ROWS] = _bf(w_ref[ba_lo + n_ba + r:ba_lo + n_ba + r + W_PREP_ROWS, :].T)

    rows = nb * lt

    def norm(sub):
        x = x_ref[:, sub * lt:(sub + 1) * lt, :]
        ms = jnp.mean(x * x, axis=-1, keepdims=True)
        y = x * lax.rsqrt(ms + EPS) * nw_ref[...]
        h = y * (1.0 + mod_ref[:, 1:2, :]) + mod_ref[:, 0:1, :]
        h_scr[sub, :, :D_MODEL] = _bf(h.reshape(rows, D_MODEL))

    def proj(sub, w_bf_ref, lo, n):
        return jnp.dot(h_scr[sub, :, :D_MODEL], w_bf_ref[:, lo:lo + n], preferred_element_type=F32)

    def plain_proj(sub, i):
        rr = slice(sub * rows, (sub + 1) * rows)
        if i < 2:
            z_ref[rr, i * IN_SLAB:(i + 1) * IN_SLAB] = proj(sub, wdn_ref, DN_CONV_CH + i * IN_SLAB, IN_SLAB)
        elif i < 4:
            sq_ref[rr, (i - 2) * IN_SLAB:(i - 1) * IN_SLAB] = proj(sub, wsw_ref, (i - 2) * IN_SLAB, IN_SLAB)
        elif i == 4:
            kv = proj(sub, wsw_ref, SWA_Q, 2 * SWA_KV)
            sk_ref[rr, :] = kv[:, :SWA_KV]
            sv_ref[rr, :] = kv[:, SWA_KV:]
        else:
            ba_ref[rr, :] = proj(sub, wba_ref, 0, BA_PAD)

    def conv_slab(sub, slab, raw):
        rr = slice(sub * rows, (sub + 1) * rows)
        cols = slice(slab * IN_SLAB, (slab + 1) * IN_SLAB)
        c = _silu(_causal_conv(raw, prev_ref[:, :, cols], cw_ref[:, cols], nb, lt))
        tail_ref[:, :, cols] = _last_rows(raw, nb, lt, DN_CONV - 1)
        if slab * IN_SLAB < 2 * DN_QK:
            scale = DN_DK ** -0.5 if slab * IN_SLAB < DN_QK else 1.0
            c = jnp.concatenate([_l2norm(c[:, i * DN_DK:(i + 1) * DN_DK]) * scale
                                 for i in range(IN_SLAB // DN_DK)], axis=1)
        qkv_ref[rr, cols] = c

    n_slabs = DN_CONV_CH // IN_SLAB
    norm(0)
    for sub in range(n_sub):
        raw_next = proj(sub, wdn_ref, 0, IN_SLAB)
        for slab in range(n_slabs):
            raw = raw_next
            plain_proj(sub, slab)
            if slab + 1 < n_slabs:
                raw_next = proj(sub, wdn_ref, (slab + 1) * IN_SLAB, IN_SLAB)
            if sub + 1 < n_sub and slab == n_slabs - 1 - IN_LOOKAHEAD_SLABS:
                norm(sub + 1)
            conv_slab(sub, slab, raw)


def _in_call(x, mod3, norm_w, w_in_t, conv_w, nb, state):
    bsz, lt, _ = x.shape
    rows = nb * lt
    row_map = lambda i, j: (i, 0)
    const = lambda i, j: (0, 0)
    per_seq = lambda i, j: (i, 0, 0)
    return pl.pallas_call(
        functools.partial(_in_kernel, nb=nb, lt=lt, carry=False, n_sub=1),
        grid=(bsz // nb, 1),
        in_specs=[pl.BlockSpec((nb, lt, D_MODEL), per_seq),
                  pl.BlockSpec((nb, 6, D_MODEL), per_seq),
                  pl.BlockSpec((1, D_MODEL), const),
                  pl.BlockSpec(w_in_t.shape, const, pipeline_mode=RESIDENT),
                  pl.BlockSpec((DN_CONV, DN_CONV_CH), const),
                  pl.BlockSpec((nb, DN_CONV - 1, DN_CONV_CH), per_seq)],
        out_specs=[pl.BlockSpec((rows, n), row_map) for n in IN_SPLIT]
        + [pl.BlockSpec((nb, DN_CONV - 1, DN_CONV_CH), per_seq)],
        out_shape=[jax.ShapeDtypeStruct((bsz * lt, n), F32) for n in IN_SPLIT]
        + [jax.ShapeDtypeStruct((bsz, DN_CONV - 1, DN_CONV_CH), F32)],
        scratch_shapes=[pltpu.VMEM((1, rows, D_MODEL + ROW_PITCH_PAD), BF16),
                        pltpu.VMEM((D_MODEL, DN_CONV_CH + DN_V + ROW_PITCH_PAD), BF16),
                        pltpu.VMEM((D_MODEL, BA_PAD), BF16),
                        pltpu.VMEM((D_MODEL, SWA_Q + 2 * SWA_KV), BF16)],
        compiler_params=_cparams(("arbitrary", "arbitrary")),
        name="norm_in_proj_state",
    )(x, mod3, norm_w, w_in_t, conv_w, state)


INV_BASE_SHIFT = 2
SWA_BLOCKS_PER_STEP = 16
DN_UNITS_STATE = 2


def _unit_lower_inverses(a_mats, ri, ci, chunk_shift):
    def blocks(s):
        return (ri >> s) == (ci >> s)

    base = min(INV_BASE_SHIFT, chunk_shift)
    xs = [jnp.where(blocks(base), -a, 0.0) for a in a_mats]
    ts = [jnp.where(ri == ci, 1.0, n) for n in xs]
    for lvl in range(1, base):
        xs = [_dot(x, x) for x in xs]
        ts = [t + _dot(t, x) for t, x in zip(ts, xs)]
    for s in range(base, chunk_shift):
        sel = blocks(s + 1) & jnp.logical_not(blocks(s))
        ets = [_dot(jnp.where(sel, a, 0.0), t) for a, t in zip(a_mats, ts)]
        ts = [t - _dot(t, et) for t, et in zip(ts, ets)]
    return ts


def _dn_kernel(*refs, nb, chunk, carry, units):
    if carry:
        (qkv_ref, z_ref, ba_ref, alog_ref, dt_ref, nw_ref, o_ref, s_ref) = refs
        s0_ref = s_ref

        @pl.when(pl.program_id(1) == 0)
        def _():
            s_ref[...] = jnp.zeros_like(s_ref)
    else:
        (qkv_ref, z_ref, ba_ref, alog_ref, dt_ref, nw_ref, s0_ref, o_ref, s_ref) = refs

    ri = lax.broadcasted_iota(jnp.int32, (UNIT, UNIT), 0)
    ci = lax.broadcasted_iota(jnp.int32, (UNIT, UNIT), 1)
    shift = int(math.log2(chunk))
    same = (ri >> shift) == (ci >> shift)
    incl = same & (ri >= ci)
    strict = same & (ri > ci)

    ba = ba_ref[...]
    beta_full = _sigmoid(ba)
    g_full = -jnp.exp(alog_ref[...]) * _softplus(ba + dt_ref[...])
    masks = jnp.concatenate([jnp.where(incl, 1.0, 0.0), jnp.where(same, 1.0, 0.0)], axis=0).astype(BF16)
    gsums = [_dot_mask(masks, g_full[u * UNIT:(u + 1) * UNIT]) for u in range(units)]
    g_cum = [g[:UNIT] for g in gsums]
    g_tot = [g[UNIT:] for g in gsums]
    g_cum_t = [g.T for g in g_cum]

    probs = [(u, h) for u in range(units) for h in range(DN_HEADS)]
    n_p = len(probs)
    rs = lambda u: slice(u * UNIT, (u + 1) * UNIT)

    def head_cols(base):
        return [qkv_ref[rs(u), base + h * DN_DK:base + (h + 1) * DN_DK] for u, h in probs]

    q = head_cols(0)
    k = head_cols(DN_QK)
    v = head_cols(2 * DN_QK)
    gc = [g_cum[u][:, DN_HEADS + h:DN_HEADS + h + 1] for u, h in probs]
    gr = [g_cum_t[u][DN_HEADS + h:DN_HEADS + h + 1, :] for u, h in probs]
    gt = [g_tot[u][:, DN_HEADS + h:DN_HEADS + h + 1] for u, h in probs]
    bc = [beta_full[rs(u), h:h + 1] for u, h in probs]
    decay = [jnp.where(incl, jnp.exp(jnp.where(incl, gc[p] - gr[p], 0.0)), 0.0) for p in range(n_p)]
    e_g = [jnp.exp(gc[p]) for p in range(n_p)]
    kq = [_dot_nt(jnp.concatenate([k[p], q[p]], axis=0), k[p]) for p in range(n_p)]
    qk = [kq[p][UNIT:] * decay[p] for p in range(n_p)]
    a_mats = [jnp.where(strict, bc[p] * kq[p][:UNIT] * decay[p], 0.0) for p in range(n_p)]
    t_inv = _unit_lower_inverses(a_mats, ri, ci, shift)
    wvk = [_dot(t_inv[p], jnp.concatenate([v[p] * bc[p], k[p] * (bc[p] * e_g[p])], axis=1)) for p in range(n_p)]
    w_v = [w[:, :DN_DV] for w in wvk]
    w_k = [w[:, DN_DV:] for w in wvk]
    q_dec = [q[p] * e_g[p] for p in range(n_p)]
    k_tail_t = [(k[p] * jnp.exp(gt[p] - gc[p])).T for p in range(n_p)]
    c_dec = [jnp.exp(gt[p]) for p in range(n_p)]

    outs = {}
    if carry:
        state = [s_ref[0, h] for h in range(DN_HEADS)]
        for u in range(units):
            ps = [u * DN_HEADS + h for h in range(DN_HEADS)]
            r = [_dot(jnp.concatenate([w_k[p], q_dec[p]], axis=0), state[h]) for h, p in enumerate(ps)]
            uu = [w_v[p] - r[h][:UNIT] for h, p in enumerate(ps)]
            for h, p in enumerate(ps):
                outs[p] = r[h][UNIT:] + _dot(qk[p], uu[h])
            state = [state[h] * c_dec[p][0:1, :] + _dot(k_tail_t[p], uu[h]) for h, p in enumerate(ps)]
        for h in range(DN_HEADS):
            s_ref[0, h] = state[h]
    else:
        per_unit = nb // units
        for p, (u, h) in enumerate(probs):
            us, qs = [], []
            for s in range(per_unit):
                lo = s * chunk
                lhs = jnp.concatenate([w_k[p][lo:lo + chunk], q_dec[p][lo:lo + chunk]], axis=0)
                r = jnp.dot(lhs, s0_ref[u * per_unit + s, h], preferred_element_type=F32)
                us.append(w_v[p][lo:lo + chunk] - r[:chunk])
                qs.append(r[chunk:])
            uu = jnp.concatenate(us, axis=0)
            outs[p] = jnp.concatenate(qs, axis=0) + _dot(qk[p], uu)
            for s in range(per_unit):
                lo = s * chunk
                upd = jnp.dot(k_tail_t[p][:, lo:lo + chunk], uu[lo:lo + chunk], preferred_element_type=F32)
                s_ref[u * per_unit + s, h] = s0_ref[u * per_unit + s, h] * c_dec[p][lo:lo + 1, :] + upd

    for p, (u, h) in enumerate(probs):
        zz = z_ref[rs(u), h * DN_DV:(h + 1) * DN_DV]
        o_ref[rs(u), h * DN_DV:(h + 1) * DN_DV] = _rms(outs[p], nw_ref[...]) * _silu(zz)


def _dn_call(qkv, z, ba, alog_row, dt_row, norm_w, bsz, seq, state):
    rows = DN_UNITS_STATE * UNIT
    nb = rows // seq
    row_map = lambda i, j: (i, 0)
    const = lambda i, j: (0, 0)
    per_seq = lambda i, j: (i, 0, 0, 0)
    return pl.pallas_call(
        functools.partial(_dn_kernel, nb=nb, chunk=seq, carry=False, units=DN_UNITS_STATE),
        grid=(bsz // nb, 1),
        in_specs=[pl.BlockSpec((rows, DN_CONV_CH), row_map),
                  pl.BlockSpec((rows, DN_V), row_map),
                  pl.BlockSpec((rows, BA_PAD), row_map),
                  pl.BlockSpec((1, BA_PAD), const),
                  pl.BlockSpec((1, BA_PAD), const),
                  pl.BlockSpec((1, DN_DV), const),
                  pl.BlockSpec((nb, DN_HEADS, DN_DK, DN_DV), per_seq)],
        out_specs=[pl.BlockSpec((rows, DN_V), row_map),
                   pl.BlockSpec((nb, DN_HEADS, DN_DK, DN_DV), per_seq)],
        out_shape=[jax.ShapeDtypeStruct((bsz * seq, DN_V), F32),
                   jax.ShapeDtypeStruct((bsz, DN_HEADS, DN_DK, DN_DV), F32)],
        compiler_params=_cparams(("arbitrary", "arbitrary")),
        name="gated_deltanet_state",
    )(qkv, z, ba, alog_row, dt_row, norm_w, state)


MIX_SUB_TILES = 2
SUB_ROWS = 2 * UNIT
MIX_ROWS = MIX_SUB_TILES * SUB_ROWS
SAMPLE_IN_SEQS = 64
SAMPLE_POST_SEQS = 32


def _mix_kernel(x_ref, mod_ref, nw_ref, w_ref, cw_ref, alog_ref, dt_ref, dnw_ref,
                sq_ref, sk_ref, sv_ref, tail_ref, o_ref, s_ref,
                h_scr, wdn_ref, wba_ref, wsw_ref, qkv_scr, z_scr, ba_scr):
    _in_kernel(x_ref, mod_ref, nw_ref, w_ref, cw_ref,
               qkv_scr, z_scr, ba_scr, sq_ref, sk_ref, sv_ref, tail_ref, h_scr, wdn_ref, wba_ref, wsw_ref,
               nb=1, lt=SUB_ROWS, carry=True, n_sub=MIX_SUB_TILES)
    _dn_kernel(qkv_scr, z_scr, ba_scr, alog_ref, dt_ref, dnw_ref, o_ref, s_ref,
               nb=1, chunk=UNIT, carry=True, units=MIX_ROWS // UNIT)


def _mix_call(x, mod3, norm_w, w_in_t, conv_w, alog_row, dt_row, dn_norm_w):
    bsz, seq, _ = x.shape
    nt = seq // MIX_ROWS
    n_tok = bsz * seq
    row_map = lambda i, j: (i * nt + j, 0)
    const = lambda i, j: (0, 0)
    per_seq = lambda i, j: (i, 0, 0)
    swa_cols = (SWA_Q, SWA_KV, SWA_KV)
    return pl.pallas_call(
        _mix_kernel,
        grid=(bsz, nt),
        in_specs=[pl.BlockSpec((1, MIX_ROWS, D_MODEL), lambda i, j: (i, j, 0)),
                  pl.BlockSpec((1, 6, D_MODEL), per_seq),
                  pl.BlockSpec((1, D_MODEL), const),
                  pl.BlockSpec(w_in_t.shape, const, pipeline_mode=RESIDENT),
                  pl.BlockSpec((DN_CONV, DN_CONV_CH), const),
                  pl.BlockSpec((1, BA_PAD), const),
                  pl.BlockSpec((1, BA_PAD), const),
                  pl.BlockSpec((1, DN_DV), const)],
        out_specs=[pl.BlockSpec((MIX_ROWS, n), row_map) for n in swa_cols]
        + [pl.BlockSpec((1, SUBLANES, DN_CONV_CH), per_seq),
           pl.BlockSpec((MIX_ROWS, DN_V), row_map),
           pl.BlockSpec((1, DN_HEADS, DN_DK, DN_DV), lambda i, j: (i, 0, 0, 0))],
        out_shape=[jax.ShapeDtypeStruct((n_tok, n), F32) for n in swa_cols]
        + [jax.ShapeDtypeStruct((bsz, SUBLANES, DN_CONV_CH), F32),
           jax.ShapeDtypeStruct((n_tok, DN_V), F32),
           jax.ShapeDtypeStruct((bsz, DN_HEADS, DN_DK, DN_DV), F32)],
        scratch_shapes=[pltpu.VMEM((MIX_SUB_TILES, SUB_ROWS, D_MODEL + ROW_PITCH_PAD), BF16),
                        pltpu.VMEM((D_MODEL, DN_CONV_CH + DN_V + ROW_PITCH_PAD), BF16),
                        pltpu.VMEM((D_MODEL, BA_PAD), BF16),
                        pltpu.VMEM((D_MODEL, SWA_Q + 2 * SWA_KV), BF16),
                        pltpu.VMEM((MIX_ROWS, DN_CONV_CH), F32),
                        pltpu.VMEM((MIX_ROWS, DN_V + ROW_PITCH_PAD), F32),
                        pltpu.VMEM((MIX_ROWS, BA_PAD), F32)],
        compiler_params=_cparams(("arbitrary", "arbitrary")),
        name="in_proj_deltanet_carry",
    )(x, mod3, norm_w, w_in_t, conv_w, alog_row, dt_row, dn_norm_w)


def _bucket_table():
    i = np.arange(WINDOW, dtype=np.int64)[:, None]
    j = np.arange(2 * WINDOW, dtype=np.int64)[None, :]
    d = np.maximum(i + WINDOW - j, 0)
    exact = N_BUCKETS // 2
    logv = (np.log(np.maximum(d, 1).astype(np.float32) / np.float32(exact)).astype(np.float32)
            / np.float32(math.log(MAX_DISTANCE / exact)))
    large = np.minimum(exact + (logv * np.float32(N_BUCKETS - exact)).astype(np.int32), N_BUCKETS - 1)
    return np.where(d < exact, d, large).astype(np.int32)


def _sample_bucket_table(bucket):
    out = bucket[:SUBLANES].copy()
    out[:, 2 * WINDOW - SUBLANES:] = bucket[:SUBLANES, WINDOW:WINDOW + SUBLANES]
    return out


def _bias_lookup(rb_ref, bucket, head):
    acc = jnp.zeros(bucket.shape, F32)
    for b in range(N_BUCKETS):
        acc = jnp.where(bucket == b, rb_ref[b, head], acc)
    return acc


def _bias_kernel(rb_ref, bucket_t_ref, bucket_ref, ot_ref, os_ref):
    bucket_t = bucket_t_ref[...]
    kj = lax.broadcasted_iota(jnp.int32, bucket_t.shape, 0)
    qi = lax.broadcasted_iota(jnp.int32, bucket_t.shape, 1)
    dist = qi + WINDOW - kj
    valid = (dist >= 0) & (dist < WINDOW)
    for hk in range(SWA_KV_HEADS):
        for par in range(2):
            for st in range(SWA_GROUP // 2):
                head = hk * SWA_GROUP + 2 * st + par
                gen = jnp.where(valid, _bias_lookup(rb_ref, bucket_t, head) * LOG2_E, NEG_INF)
                ot_ref[1, hk, par, :, st * WINDOW:(st + 1) * WINDOW] = gen
                ot_ref[0, hk, par, :, st * WINDOW:(st + 1) * WINDOW] = jnp.where(kj >= WINDOW, gen, NEG_INF)
    bucket = bucket_ref[...]
    qi = lax.broadcasted_iota(jnp.int32, bucket.shape, 0)
    col = lax.broadcasted_iota(jnp.int32, bucket.shape, 1)
    kj = jnp.where(col < WINDOW, col, col - (WINDOW - SUBLANES))
    dist = qi + WINDOW - kj
    valid = (dist >= 0) & (dist < WINDOW) & ((col < WINDOW) | (col >= 2 * WINDOW - SUBLANES))
    for head in range(SWA_HEADS):
        os_ref[head] = jnp.where(valid, _bias_lookup(rb_ref, bucket, head), NEG_INF)


BIAS_T_SHAPE = (2, SWA_KV_HEADS, 2, 2 * WINDOW, (SWA_GROUP // 2) * WINDOW)
BIAS_S_SHAPE = (SWA_HEADS, SUBLANES, 2 * WINDOW)


def _bias_call(rel_bias):
    bucket = _bucket_table()
    return pl.pallas_call(
        _bias_kernel,
        in_specs=[pl.BlockSpec(memory_space=pltpu.SMEM),
                  pl.BlockSpec((2 * WINDOW, WINDOW), lambda: (0, 0)),
                  pl.BlockSpec((SUBLANES, 2 * WINDOW), lambda: (0, 0))],
        out_specs=[pl.BlockSpec(BIAS_T_SHAPE, lambda: (0,) * len(BIAS_T_SHAPE)),
                   pl.BlockSpec(BIAS_S_SHAPE, lambda: (0,) * len(BIAS_S_SHAPE))],
        out_shape=[jax.ShapeDtypeStruct(BIAS_T_SHAPE, F32), jax.ShapeDtypeStruct(BIAS_S_SHAPE, F32)],
        name="swa_rel_bias_table",
    )(rel_bias, jnp.asarray(np.ascontiguousarray(bucket.T)), jnp.asarray(_sample_bucket_table(bucket)))


def _softmax_sink_parts(s, sink):
    m = jnp.maximum(jnp.max(s, axis=0, keepdims=True), sink)
    p = jnp.exp2(s - m)
    return p, 1.0 / (jnp.sum(p, axis=0, keepdims=True) + jnp.exp2(sink - m))


def _half_lane_variants(full, hk, lo_half):
    rolled = pltpu.roll(full, SWA_HD, axis=1)
    low_src, high_src = (full, rolled) if hk == 0 else (rolled, full)
    return jnp.where(lo_half, low_src, 0.0), jnp.where(lo_half, 0.0, high_src)


def _swa_prompt_kernel(sink_ref, q_ref, kp_ref, kc_ref, vp_ref, vc_ref, bias_ref, o_ref, *, n_blk):
    step = pl.program_id(1)
    lo_half = lax.broadcasted_iota(jnp.int32, (1, 2 * SWA_HD), 1) < SWA_HD
    lo_rows = lax.broadcasted_iota(jnp.int32, (2 * SWA_HD, 1), 0) < SWA_HD
    q = _bf(q_ref[...] * (SWA_HD ** -0.5 * LOG2_E))
    keys = jnp.concatenate([kp_ref[...], kc_ref[...]], axis=0)
    vals = jnp.concatenate([vp_ref[...], vc_ref[...]], axis=0)
    k_var = [[_bf(t) for t in _half_lane_variants(keys, hk, lo_half)] for hk in range(SWA_KV_HEADS)]
    v_var_t = [[_bf(t.T) for t in _half_lane_variants(vals, hk, lo_half)] for hk in range(SWA_KV_HEADS)]
    n_stack = SWA_GROUP // 2
    sinks = [[jnp.concatenate([jnp.full((1, WINDOW), sink_ref[hk * SWA_GROUP + 2 * st + par] * LOG2_E, F32)
                               for st in range(n_stack)], axis=1) for par in range(2)]
             for hk in range(SWA_KV_HEADS)]
    nt_dims = (((1,), (1,)), ((), ()))

    def scores(b):
        rows = slice(b * WINDOW, (b + 1) * WINDOW)
        win = slice(b * WINDOW, (b + 2) * WINDOW)
        variant = jnp.where(step == 0, 0, 1) if b == 0 else 1
        out = []
        for hk in range(SWA_KV_HEADS):
            q2 = jnp.concatenate([q[rows, (hk * n_stack + st) * 2 * SWA_HD:(hk * n_stack + st + 1) * 2 * SWA_HD]
                                  for st in range(n_stack)], axis=0)
            out.append([lax.dot_general(k_var[hk][par][win], q2, nt_dims, preferred_element_type=F32)
                        + bias_ref[variant, hk, par] for par in range(2)])
        return out

    s_next = scores(0)
    for b in range(n_blk):
        s_cur = s_next
        if b + 1 < n_blk:
            s_next = scores(b + 1)
        rows = slice(b * WINDOW, (b + 1) * WINDOW)
        win = slice(b * WINDOW, (b + 2) * WINDOW)
        for hk in range(SWA_KV_HEADS):
            parts = [_softmax_sink_parts(s_cur[hk][par], sinks[hk][par]) for par in range(2)]
            o_t = (jnp.dot(v_var_t[hk][0][:, win], _bf(parts[0][0]), preferred_element_type=F32)
                   + jnp.dot(v_var_t[hk][1][:, win], _bf(parts[1][0]), preferred_element_type=F32))
            o_t = o_t * jnp.where(lo_rows, parts[0][1], parts[1][1])
            for st in range(n_stack):
                lo = (hk * n_stack + st) * 2 * SWA_HD
                o_ref[rows, lo:lo + 2 * SWA_HD] = o_t[:, st * WINDOW:(st + 1) * WINDOW].T


def _swa_prompt_call(sinks, sq, sk, sv, bias, bsz, seq):
    n_blk = SWA_BLOCKS_PER_STEP
    tile = n_blk * WINDOW
    nt = seq // tile
    cur = lambda b, i: (b * nt + i, 0)
    prv = lambda b, i: (b * nt * n_blk + jnp.maximum(i * n_blk - 1, 0), 0)
    return pl.pallas_call(
        functools.partial(_swa_prompt_kernel, n_blk=n_blk),
        grid=(bsz, nt),
        in_specs=[pl.BlockSpec(memory_space=pltpu.SMEM),
                  pl.BlockSpec((tile, SWA_Q), cur),
                  pl.BlockSpec((WINDOW, SWA_KV), prv),
                  pl.BlockSpec((tile, SWA_KV), cur),
                  pl.BlockSpec((WINDOW, SWA_KV), prv),
                  pl.BlockSpec((tile, SWA_KV), cur),
                  pl.BlockSpec(BIAS_T_SHAPE, lambda b, i: (0,) * len(BIAS_T_SHAPE))],
        out_specs=pl.BlockSpec((tile, SWA_Q), cur),
        out_shape=jax.ShapeDtypeStruct((bsz * seq, SWA_Q), F32),
        compiler_params=_cparams(("arbitrary", "arbitrary")),
        name="swa_banded",
    )(sinks, sq, sk, sk, sv, sv, bias)


def _swa_sample_kernel(sink_ref, q_ref, kn_ref, vn_ref, kc_ref, vc_ref, bias_ref, o_ref, ko_ref, vo_ref, *, nb, lt):
    lane = lax.broadcasted_iota(jnp.int32, (1, 2 * SWA_HD), 1)
    lo_half = lane < SWA_HD
    new_lanes = lane >= WINDOW - lt
    q_all = q_ref[...] * (SWA_HD ** -0.5)
    kn_all_t = kn_ref[...].T
    vn_all_t = vn_ref[...].T
    bias = jnp.concatenate([bias_ref[h, 0:lt, :] for h in range(SWA_HEADS)], axis=0)
    sink = jnp.concatenate([jnp.full((lt, 1), sink_ref[h], F32) for h in range(SWA_HEADS)], axis=0)
    nt_dims = (((1,), (1,)), ((), ()))

    def to_half(tile, src_half, dst_half):
        return tile if src_half == dst_half else pltpu.roll(tile, SWA_HD, axis=1)

    lhs, keys, vals = [], [], []
    for s in range(nb):
        rows = slice(s * lt, (s + 1) * lt)
        to_end = (WINDOW - lt - s * lt) % WINDOW
        for c_ref, n_all_t, out_ref, acc in ((kc_ref, kn_all_t, ko_ref, keys), (vc_ref, vn_all_t, vo_ref, vals)):
            cached = c_ref[s]
            new = n_all_t if to_end == 0 else pltpu.roll(n_all_t, to_end, axis=1)
            out_ref[s] = jnp.where(new_lanes, new, pltpu.roll(cached, WINDOW - lt, axis=1))
            acc.append(_bf(jnp.concatenate([cached, jnp.where(new_lanes, new, 0.0)], axis=1)))
        q = q_all[rows]
        tiles = []
        for h in range(SWA_HEADS):
            hk = h // SWA_GROUP
            t = to_half(q[:, (h // 2) * 2 * SWA_HD:(h // 2 + 1) * 2 * SWA_HD], h % 2, hk)
            tiles.append(jnp.where(lo_half if hk == 0 else jnp.logical_not(lo_half), t, 0.0))
        lhs.append(_bf(jnp.concatenate(tiles, axis=0)))
    scores = [jnp.dot(lhs[s], keys[s], preferred_element_type=F32) + bias for s in range(nb)]
    m = [jnp.maximum(jnp.max(sc, axis=-1, keepdims=True), sink) for sc in scores]
    p = [jnp.exp(sc - mm) for sc, mm in zip(scores, m)]
    rinv = [1.0 / (jnp.sum(pp, axis=-1, keepdims=True) + jnp.exp(sink - mm)) for pp, mm in zip(p, m)]
    res = [lax.dot_general(_bf(p[s]), vals[s], nt_dims, preferred_element_type=F32) * rinv[s] for s in range(nb)]
    rows_out = []
    for s in range(nb):
        tiles = []
        for pair in range(SWA_HEADS // 2):
            hk = (2 * pair) // SWA_GROUP
            low = to_half(res[s][(2 * pair) * lt:(2 * pair + 1) * lt], hk, 0)
            high = to_half(res[s][(2 * pair + 1) * lt:(2 * pair + 2) * lt], hk, 1)
            tiles.append(jnp.where(lo_half, low, high))
        rows_out.append(jnp.concatenate(tiles, axis=1))
    o_ref[...] = jnp.concatenate(rows_out, axis=0)


def _swa_sample_call(sinks, sq, sk, sv, cache_k, cache_v, bias, bsz, seq):
    assert seq == SUBLANES, "the sample bias table places one sublane tile of new keys"
    nb = UNIT // seq
    rows = lambda i: (i, 0)
    seqs = lambda i: (i, 0, 0)
    return pl.pallas_call(
        functools.partial(_swa_sample_kernel, nb=nb, lt=seq),
        grid=(bsz // nb,),
        in_specs=[pl.BlockSpec(memory_space=pltpu.SMEM),
                  pl.BlockSpec((UNIT, SWA_Q), rows),
                  pl.BlockSpec((UNIT, SWA_KV), rows),
                  pl.BlockSpec((UNIT, SWA_KV), rows),
                  pl.BlockSpec((nb, SWA_KV, WINDOW), seqs),
                  pl.BlockSpec((nb, SWA_KV, WINDOW), seqs),
                  pl.BlockSpec(BIAS_S_SHAPE, lambda i: (0,) * len(BIAS_S_SHAPE))],
        out_specs=[pl.BlockSpec((UNIT, SWA_Q), rows),
                   pl.BlockSpec((nb, SWA_KV, WINDOW), seqs),
                   pl.BlockSpec((nb, SWA_KV, WINDOW), seqs)],
        out_shape=[jax.ShapeDtypeStruct((bsz * seq, SWA_Q), F32),
                   jax.ShapeDtypeStruct((bsz, SWA_KV, WINDOW), F32),
                   jax.ShapeDtypeStruct((bsz, SWA_KV, WINDOW), F32)],
        compiler_params=_cparams(("arbitrary",)),
        name="swa_cached",
    )(sinks, sq, sk, sv, cache_k, cache_v, bias)


def _post_kernel(*refs, nb, lt, carry, n_sub):
    if carry:
        (odn_ref, oswa_ref, x_ref, mod_ref, wout_ref, nfw_ref, wup_ref, cw_ref, cb_ref, wdn_ref, fnw_ref,
         y_ref, fbuf_ref, h_scr, x1_scr, acc_scr) = refs
        prev_ref = fbuf_ref

        @pl.when(pl.program_id(1) == 0)
        def _():
            fbuf_ref[...] = jnp.zeros_like(fbuf_ref)
    else:
        (odn_ref, oswa_ref, x_ref, mod_ref, wout_ref, nfw_ref, wup_ref, cw_ref, cb_ref, wdn_ref, fnw_ref,
         prev_ref, y_ref, fbuf_ref, h_scr, x1_scr, acc_scr) = refs
    rows = nb * lt
    n_chunks = D_FF // FFN_CHUNK

    def col_slices(c):
        return [slice(base + c * FFN_CHUNK, base + (c + 1) * FFN_CHUNK) for base in (0, D_FF)]

    def prologue(sub):
        rr = slice(sub * rows, (sub + 1) * rows)
        attn = (jnp.dot(_bf(odn_ref[rr, :]), wout_ref[0:DN_V, :], preferred_element_type=F32)
                + jnp.dot(_bf(oswa_ref[rr, :]), wout_ref[DN_V:, :], preferred_element_type=F32))
        x = x_ref[:, sub * lt:(sub + 1) * lt, :].reshape(rows, D_MODEL)
        x1 = x + _rows(mod_ref[:, 2:3, :], nb, lt) * attn
        x1_scr[sub] = x1
        h = (_rms(x1, nfw_ref[...]) * (1.0 + _rows(mod_ref[:, 4:5, :], nb, lt))
             + _rows(mod_ref[:, 3:4, :], nb, lt))
        h_scr[sub] = _bf(h)
        acc_scr[sub] = jnp.zeros((rows, D_MODEL), F32)

    def up_proj(sub, c):
        return [jnp.dot(h_scr[sub], wup_ref[:, cols], preferred_element_type=F32) for cols in col_slices(c)]

    def chunk(sub, c, u_cur):
        halves = []
        for u, cols in zip(u_cur, col_slices(c)):
            prev = prev_ref[:, :, cols]
            halves.append(_causal_conv(u, prev, cw_ref[:, cols], nb, lt) + cb_ref[:, cols])
            fbuf_ref[:, :, cols] = _last_rows(u, nb, lt, FFN_CONV - 1)
        act = _silu(halves[0]) * halves[1]
        acc_scr[sub] += jnp.dot(_bf(act), wdn_ref[c * FFN_CHUNK:(c + 1) * FFN_CHUNK, :],
                                preferred_element_type=F32)

    def epilogue(sub):
        x2 = x1_scr[sub] + _rows(mod_ref[:, 5:6, :], nb, lt) * acc_scr[sub]
        y_ref[:, sub * lt:(sub + 1) * lt, :] = _rms(x2, fnw_ref[...]).reshape(nb, lt, D_MODEL)

    prologue(0)
    for sub in range(n_sub):
        u_next = up_proj(sub, 0)
        for c in range(n_chunks):
            u_cur = u_next
            if c + 1 < n_chunks:
                u_next = up_proj(sub, c + 1)
            chunk(sub, c, u_cur)
            if sub + 1 < n_sub and c == n_chunks - 1 - POST_LOOKAHEAD_CHUNKS:
                prologue(sub + 1)
        epilogue(sub)


def _post_call(o_dn, o_swa, x, mod3, w_out, norm_ffn_w, w_up, conv_w, conv_b, w_down, final_w, nb, lt, state=None):
    carry = state is None
    n_sub = POST_SUB_TILES if carry else 1
    bsz, seq, _ = x.shape
    nt = seq // (lt * n_sub)
    rows = nb * lt
    tile = rows * n_sub
    n_state = _state_rows(nb, lt, FFN_CONV - 1)
    row_map = lambda i, j: (i * nt + j, 0)
    const = lambda i, j: (0, 0)
    in_specs = [pl.BlockSpec((tile, DN_V), row_map),
                pl.BlockSpec((tile, SWA_Q), row_map),
                pl.BlockSpec((nb, lt * n_sub, D_MODEL), lambda i, j: (i, j, 0)),
                pl.BlockSpec((nb, 6, D_MODEL), lambda i, j: (i, 0, 0)),
                pl.BlockSpec((D_MODEL, D_MODEL), const, pipeline_mode=RESIDENT),
                pl.BlockSpec((1, D_MODEL), const),
                pl.BlockSpec(w_up.shape, const, pipeline_mode=RESIDENT),
                pl.BlockSpec((FFN_CONV, 2 * D_FF), const),
                pl.BlockSpec((1, 2 * D_FF), const),
                pl.BlockSpec((D_FF, D_MODEL), const, pipeline_mode=RESIDENT),
                pl.BlockSpec((1, D_MODEL), const)]
    args = [o_dn, o_swa, x, mod3, w_out, norm_ffn_w, w_up, conv_w, conv_b, w_down, final_w]
    if not carry:
        in_specs.append(pl.BlockSpec((nb, FFN_CONV - 1, 2 * D_FF), lambda i, j: (i, 0, 0)))
        args.append(state)
    return pl.pallas_call(
        functools.partial(_post_kernel, nb=nb, lt=lt, carry=carry, n_sub=n_sub),
        grid=(bsz // nb, nt),
        in_specs=in_specs,
        out_specs=[pl.BlockSpec((nb, lt * n_sub, D_MODEL), lambda i, j: (i, j, 0)),
                   pl.BlockSpec((nb, n_state, 2 * D_FF), lambda i, j: (i, 0, 0))],
        out_shape=[jax.ShapeDtypeStruct((bsz, seq, D_MODEL), F32),
                   jax.ShapeDtypeStruct((bsz, n_state, 2 * D_FF), F32)],
        scratch_shapes=[pltpu.VMEM((n_sub, rows, D_MODEL), BF16),
                        pltpu.VMEM((n_sub, rows, D_MODEL), F32),
                        pltpu.VMEM((n_sub, rows, D_MODEL), F32)],
        compiler_params=_cparams(("arbitrary", "arbitrary")),
        name="out_proj_convffn_carry" if carry else "out_proj_convffn_state",
    )(*args)


def _pad_row(vec, offset):
    return jnp.zeros((1, BA_PAD), F32).at[0, offset:offset + vec.shape[0]].set(vec)


def kernel(x_prompt, x_sample, state_dn_conv, state_dn_ssm, cache_swa_k, cache_swa_v, state_ffn_conv, c_prompt, c_sample, rel_bias, final_norm_w, w_ada, b_ada, norm_mix_w, w_in, dn_conv_w, dn_A_log, dn_dt_bias, dn_norm_w, swa_sinks, w_out, norm_ffn_w, ffn_w_up, ffn_conv_w, ffn_conv_b, ffn_w_down):
    bp, lp, _ = x_prompt.shape
    bs, ls, _ = x_sample.shape
    layer = 0

    w_in_t = jnp.transpose(w_in[layer])
    w_out_b = w_out[layer].astype(BF16)
    w_up_b = ffn_w_up[layer].astype(BF16)
    w_dn_b = ffn_w_down[layer].astype(BF16)
    alog_row = _pad_row(dn_A_log[layer], DN_HEADS)
    dt_row = _pad_row(dn_dt_bias[layer], DN_HEADS)
    row = lambda v: v.reshape(1, -1)

    n_c = bp + bs
    n_c_pad = -(-n_c // SUBLANES) * SUBLANES
    c_all = jnp.pad(jnp.concatenate([c_sample, c_prompt], axis=0), ((0, n_c_pad - n_c), (0, 0)))
    mod_s, mod_p = _mod_call(c_all, w_ada[layer], row(b_ada[layer]), bs, bp)

    bias_t, bias_s = _bias_call(rel_bias)
    sinks = swa_sinks[layer]

    def post(o_dn, o_swa, x, mod3, nb, lt, state=None):
        return _post_call(o_dn, o_swa, x, mod3, w_out_b, row(norm_ffn_w[layer]), w_up_b, ffn_conv_w[layer],
                          row(ffn_conv_b[layer]), w_dn_b, row(final_norm_w), nb, lt, state)

    dn_args = (alog_row, dt_row, row(dn_norm_w[layer]))

    sq, sk, sv, p_tail, o_dn, p_ssm = _mix_call(x_prompt, mod_p, row(norm_mix_w[layer]), w_in_t,
                                                dn_conv_w[layer], *dn_args)
    o_swa = _swa_prompt_call(sinks, sq, sk, sv, bias_t, bp, lp)
    y_prompt, p_ffn_tail = post(o_dn, o_swa, x_prompt, mod_p, 1, SUB_ROWS)
    p_dn_conv = p_tail[:, SUBLANES - (DN_CONV - 1):]
    p_fbuf = p_ffn_tail[:, SUBLANES - (FFN_CONV - 1):]
    last_window = lambda t: t.reshape(bp, lp, SWA_KV)[:, lp - WINDOW:].reshape(bp, WINDOW, SWA_KV_HEADS, SWA_HD)
    p_swa_k = last_window(sk)
    p_swa_v = last_window(sv)

    cache_t = lambda c: jnp.transpose(c[layer].reshape(bs, WINDOW, SWA_KV), (0, 2, 1))
    cache_from_t = lambda c: jnp.transpose(c, (0, 2, 1)).reshape(bs, WINDOW, SWA_KV_HEADS, SWA_HD)
    qkv_s, z_s, ba_s, sq_s, sk_s, sv_s, s_dn_conv = _in_call(
        x_sample, mod_s, row(norm_mix_w[layer]), w_in_t, dn_conv_w[layer], SAMPLE_IN_SEQS, state_dn_conv[layer])
    o_dn_s, s_ssm = _dn_call(qkv_s, z_s, ba_s, *dn_args, bs, ls, state=state_dn_ssm[layer])
    o_swa_s, s_k, s_v = _swa_sample_call(sinks, sq_s, sk_s, sv_s,
                                         cache_t(cache_swa_k), cache_t(cache_swa_v), bias_s, bs, ls)
    y_sample, s_fbuf = post(o_dn_s, o_swa_s, x_sample, mod_s, SAMPLE_POST_SEQS, ls, state=state_ffn_conv[layer])

    return (y_prompt, y_sample, p_dn_conv[None], s_dn_conv[None], p_ssm[None], s_ssm[None],
            p_swa_k[None], cache_from_t(s_k)[None],
            p_swa_v[None], cache_from_t(s_v)[None],
            p_fbuf[None], s_fbuf[None])
```

```python
import functools
import math

import numpy as np
import jax
import jax.numpy as jnp
from jax import lax
from jax.experimental import pallas as pl
from jax.experimental.pallas import tpu as pltpu

F32 = jnp.float32
BF16 = jnp.bfloat16

D_MODEL = 1024
DN_HEADS = 4
DN_DK = 128
DN_DV = 128
DN_CONV = 4
SWA_HEADS = 8
SWA_KV_HEADS = 2
SWA_GROUP = SWA_HEADS // SWA_KV_HEADS
SWA_HD = 64
WINDOW = 128
N_BUCKETS = 32
MAX_DISTANCE = 128
D_FF = 2816
FFN_CONV = 3
EPS = 1e-6
NEG_INF = -1e30
LOG2_E = 1.0 / math.log(2.0)

DN_QK = DN_HEADS * DN_DK
DN_V = DN_HEADS * DN_DV
DN_CONV_CH = 2 * DN_QK + DN_V
SWA_Q = SWA_HEADS * SWA_HD
SWA_KV = SWA_KV_HEADS * SWA_HD
BA_PAD = 128
SUBLANES = 8
UNIT = 128
POST_SUB_TILES = 2
POST_LOOKAHEAD_CHUNKS = 3
UP_PROJ_AHEAD = 2
FFN_CHUNK = 256
ROW_PITCH_PAD = 128
VMEM_LIMIT = 56 * 1024 * 1024
RESIDENT = pl.Buffered(1)


def _cparams(sem):
    return pltpu.CompilerParams(dimension_semantics=sem, vmem_limit_bytes=VMEM_LIMIT)


def _bf(x):
    return x.astype(BF16)


def _dot(a, b):
    return jnp.dot(_bf(a), _bf(b), preferred_element_type=F32)


def _dot_nt(a, b):
    return lax.dot_general(_bf(a), _bf(b), (((1,), (1,)), ((), ())), preferred_element_type=F32)


def _dot_mask(m, x):
    hi = _bf(x)
    r = x - hi.astype(F32)
    mid = _bf(r)
    lo = _bf(r - mid.astype(F32))
    d = functools.partial(jnp.dot, preferred_element_type=F32)
    return d(m, hi) + (d(m, mid) + d(m, lo))


def _sigmoid(x):
    return 1.0 / (1.0 + jnp.exp(-x))


def _silu(x):
    return x * _sigmoid(x)


def _softplus(x):
    return jnp.maximum(x, 0.0) + jnp.log1p(jnp.exp(-jnp.abs(x)))


def _rms(x, w):
    ms = jnp.mean(x * x, axis=-1, keepdims=True)
    return x * lax.rsqrt(ms + EPS) * w


def _l2norm(t):
    return t * lax.rsqrt(jnp.sum(t * t, axis=-1, keepdims=True) + EPS)


def _rows(m3, nb, lt):
    return jnp.broadcast_to(m3, (nb, lt, m3.shape[-1])).reshape(nb * lt, m3.shape[-1])


def _causal_conv(x, prev, w, nb, lt):
    width = w.shape[0]
    rows, ch = x.shape
    if nb == 1 and lt > SUBLANES:
        tiles = jnp.concatenate([prev, x.reshape(lt // SUBLANES, SUBLANES, ch)], axis=0)
        sub = lax.broadcasted_iota(jnp.int32, (1, SUBLANES, 1), 1)
        out = tiles[1:] * w[width - 1:width, :]
        for j in range(1, width):
            rot = pltpu.roll(tiles, j, axis=1)
            out = out + jnp.where(sub >= j, rot[1:], rot[:-1]) * w[width - 1 - j:width - j, :]
        return out.reshape(rows, ch)
    assert lt == SUBLANES
    x3 = x.reshape(nb, lt, ch)
    st = jnp.concatenate([jnp.zeros((nb, lt - (width - 1), ch), F32), prev], axis=1)
    sub = lax.broadcasted_iota(jnp.int32, (1, SUBLANES, 1), 1)
    out = x3 * w[width - 1:width, :]
    for j in range(1, width):
        sh = jnp.where(sub >= j, pltpu.roll(x3, j, axis=1), pltpu.roll(st, j, axis=1))
        out = out + sh * w[width - 1 - j:width - j, :]
    return out.reshape(rows, ch)


def _last_rows(x, nb, lt, n_state):
    ch = x.shape[-1]
    if nb == 1 and lt > SUBLANES:
        return x[lt - SUBLANES:].reshape(1, SUBLANES, ch)
    return x.reshape(nb, lt, ch)[:, lt - n_state:, :]


def _state_rows(nb, lt, n_state):
    return SUBLANES if (nb == 1 and lt > SUBLANES) else n_state


MOD_PARTS = 6


def _mod_kernel(c_ref, w_ref, b_ref, os_ref, op_ref, *, n_s, n_p):
    part = pl.program_id(0)
    res = _dot(_silu(c_ref[...]), w_ref[...]) + b_ref[...]
    for k in range(MOD_PARTS):
        @pl.when(part == k)
        def _():
            os_ref[:, k, :] = res[:n_s]
            op_ref[:, k, :] = res[n_s:n_s + n_p]


def _mod_call(c_all, w_ada, b_ada, n_s, n_p):
    rows = c_all.shape[0]
    return pl.pallas_call(
        functools.partial(_mod_kernel, n_s=n_s, n_p=n_p),
        grid=(MOD_PARTS,),
        in_specs=[pl.BlockSpec((rows, D_MODEL), lambda k: (0, 0)),
                  pl.BlockSpec((D_MODEL, D_MODEL), lambda k: (0, k)),
                  pl.BlockSpec((1, D_MODEL), lambda k: (0, k))],
        out_specs=[pl.BlockSpec((n_s, MOD_PARTS, D_MODEL), lambda k: (0, 0, 0)),
                   pl.BlockSpec((n_p, MOD_PARTS, D_MODEL), lambda k: (0, 0, 0))],
        out_shape=[jax.ShapeDtypeStruct((n_s, MOD_PARTS, D_MODEL), F32),
                   jax.ShapeDtypeStruct((n_p, MOD_PARTS, D_MODEL), F32)],
        compiler_params=_cparams(("arbitrary",)),
        name="adaln_mod",
    )(c_all, w_ada, b_ada)


IN_SPLIT = (DN_CONV_CH, DN_V, BA_PAD, SWA_Q, SWA_KV, SWA_KV)


IN_LOOKAHEAD_SLABS = 1
W_PREP_ROWS = 256
IN_SLAB = 2 * DN_DK


def _in_kernel(*refs, nb, lt, carry, n_sub):
    if carry:
        (x_ref, mod_ref, nw_ref, w_ref, cw_ref,
         qkv_ref, z_ref, ba_ref, sq_ref, sk_ref, sv_ref, tail_ref, h_scr, wdn_ref, wba_ref, wsw_ref) = refs
        prev_ref = tail_ref

        @pl.when(pl.program_id(1) == 0)
        def _():
            tail_ref[...] = jnp.zeros_like(tail_ref)
    else:
        (x_ref, mod_ref, nw_ref, w_ref, cw_ref, prev_ref,
         qkv_ref, z_ref, ba_ref, sq_ref, sk_ref, sv_ref, tail_ref, h_scr, wdn_ref, wba_ref, wsw_ref) = refs

    @pl.when((pl.program_id(0) == 0) & (pl.program_id(1) == 0))
    def _():
        ba_lo = DN_CONV_CH + DN_V
        n_ba = 2 * DN_HEADS
        for r in range(0, ba_lo, W_PREP_ROWS):
            wdn_ref[:, r:r + W_PREP_ROWS] = _bf(w_ref[r:r + W_PREP_ROWS, :].T)
        row = lax.broadcasted_iota(jnp.int32, (BA_PAD, 1), 0)
        wba_ref[...] = _bf(jnp.where(row < n_ba, w_ref[ba_lo:ba_lo + BA_PAD, :], 0.0).T)
        for r in range(0, SWA_Q + 2 * SWA_KV, W_PREP_ROWS):
            wsw_ref[:, r:r + W_PREP_ROWS] = _bf(w_ref[ba_lo + n_ba + r:ba_lo + n_ba + r + W_PREP_ROWS, :].T)

    rows = nb * lt

    def norm(sub):
        x = x_ref[:, sub * lt:(sub + 1) * lt, :]
        ms = jnp.mean(x * x, axis=-1, keepdims=True)
        y = x * lax.rsqrt(ms + EPS) * nw_ref[...]
        h = y * (1.0 + mod_ref[:, 1:2, :]) + mod_ref[:, 0:1, :]
        h_scr[sub, :, :D_MODEL] = _bf(h.reshape(rows, D_MODEL))

    def proj(sub, w_bf_ref, lo, n):
        return jnp.dot(h_scr[sub, :, :D_MODEL], w_bf_ref[:, lo:lo + n], preferred_element_type=F32)

    def plain_proj(sub, i):
        rr = slice(sub * rows, (sub + 1) * rows)
        if i < 2:
            z_ref[rr, i * IN_SLAB:(i + 1) * IN_SLAB] = proj(sub, wdn_ref, DN_CONV_CH + i * IN_SLAB, IN_SLAB)
        elif i < 4:
            sq_ref[rr, (i - 2) * IN_SLAB:(i - 1) * IN_SLAB] = proj(sub, wsw_ref, (i - 2) * IN_SLAB, IN_SLAB)
        elif i == 4:
            kv = proj(sub, wsw_ref, SWA_Q, 2 * SWA_KV)
            sk_ref[rr, :] = kv[:, :SWA_KV]
            sv_ref[rr, :] = kv[:, SWA_KV:]
        else:
            ba_ref[rr, :] = proj(sub, wba_ref, 0, BA_PAD)

    def conv_slab(sub, slab, raw):
        rr = slice(sub * rows, (sub + 1) * rows)
        cols = slice(slab * IN_SLAB, (slab + 1) * IN_SLAB)
        c = _silu(_causal_conv(raw, prev_ref[:, :, cols], cw_ref[:, cols], nb, lt))
        tail_ref[:, :, cols] = _last_rows(raw, nb, lt, DN_CONV - 1)
        if slab * IN_SLAB < 2 * DN_QK:
            scale = DN_DK ** -0.5 if slab * IN_SLAB < DN_QK else 1.0
            c = jnp.concatenate([_l2norm(c[:, i * DN_DK:(i + 1) * DN_DK]) * scale
                                 for i in range(IN_SLAB // DN_DK)], axis=1)
        qkv_ref[rr, cols] = c

    n_slabs = DN_CONV_CH // IN_SLAB
    norm(0)
    for sub in range(n_sub):
        raw_next = proj(sub, wdn_ref, 0, IN_SLAB)
        for slab in range(n_slabs):
            raw = raw_next
            plain_proj(sub, slab)
            if slab + 1 < n_slabs:
                raw_next = proj(sub, wdn_ref, (slab + 1) * IN_SLAB, IN_SLAB)
            if sub + 1 < n_sub and slab == n_slabs - 1 - IN_LOOKAHEAD_SLABS:
                norm(sub + 1)
            conv_slab(sub, slab, raw)


def _in_call(x, mod3, norm_w, w_in_t, conv_w, nb, state):
    bsz, lt, _ = x.shape
    rows = nb * lt
    row_map = lambda i, j: (i, 0)
    const = lambda i, j: (0, 0)
    per_seq = lambda i, j: (i, 0, 0)
    return pl.pallas_call(
        functools.partial(_in_kernel, nb=nb, lt=lt, carry=False, n_sub=1),
        grid=(bsz // nb, 1),
        in_specs=[pl.BlockSpec((nb, lt, D_MODEL), per_seq),
                  pl.BlockSpec((nb, 6, D_MODEL), per_seq),
                  pl.BlockSpec((1, D_MODEL), const),
                  pl.BlockSpec(w_in_t.shape, const, pipeline_mode=RESIDENT),
                  pl.BlockSpec((DN_CONV, DN_CONV_CH), const),
                  pl.BlockSpec((nb, DN_CONV - 1, DN_CONV_CH), per_seq)],
        out_specs=[pl.BlockSpec((rows, n), row_map) for n in IN_SPLIT]
        + [pl.BlockSpec((nb, DN_CONV - 1, DN_CONV_CH), per_seq)],
        out_shape=[jax.ShapeDtypeStruct((bsz * lt, n), F32) for n in IN_SPLIT]
        + [jax.ShapeDtypeStruct((bsz, DN_CONV - 1, DN_CONV_CH), F32)],
        scratch_shapes=[pltpu.VMEM((1, rows, D_MODEL + ROW_PITCH_PAD), BF16),
                        pltpu.VMEM((D_MODEL, DN_CONV_CH + DN_V + ROW_PITCH_PAD), BF16),
                        pltpu.VMEM((D_MODEL, BA_PAD), BF16),
                        pltpu.VMEM((D_MODEL, SWA_Q + 2 * SWA_KV), BF16)],
        compiler_params=_cparams(("arbitrary", "arbitrary")),
        name="norm_in_proj_state",
    )(x, mod3, norm_w, w_in_t, conv_w, state)


INV_BASE_SHIFT = 2
SWA_BLOCKS_PER_STEP = 16
DN_UNITS_STATE = 2


def _unit_lower_inverses(a_mats, ri, ci, chunk_shift):
    def blocks(s):
        return (ri >> s) == (ci >> s)

    base = min(INV_BASE_SHIFT, chunk_shift)
    xs = [jnp.where(blocks(base), -a, 0.0) for a in a_mats]
    ts = [jnp.where(ri == ci, 1.0, n) for n in xs]
    for lvl in range(1, base):
        xs = [_dot(x, x) for x in xs]
        ts = [t + _dot(t, x) for t, x in zip(ts, xs)]
    for s in range(base, chunk_shift):
        sel = blocks(s + 1) & jnp.logical_not(blocks(s))
        ets = [_dot(jnp.where(sel, a, 0.0), t) for a, t in zip(a_mats, ts)]
        ts = [t - _dot(t, et) for t, et in zip(ts, ets)]
    return ts


def _dn_kernel(*refs, nb, chunk, carry, units):
    if carry:
        (qkv_ref, z_ref, ba_ref, alog_ref, dt_ref, nw_ref, o_ref, s_ref) = refs
        s0_ref = s_ref

        @pl.when(pl.program_id(1) == 0)
        def _():
            s_ref[...] = jnp.zeros_like(s_ref)
    else:
        (qkv_ref, z_ref, ba_ref, alog_ref, dt_ref, nw_ref, s0_ref, o_ref, s_ref) = refs

    ri = lax.broadcasted_iota(jnp.int32, (UNIT, UNIT), 0)
    ci = lax.broadcasted_iota(jnp.int32, (UNIT, UNIT), 1)
    shift = int(math.log2(chunk))
    same = (ri >> shift) == (ci >> shift)
    incl = same & (ri >= ci)
    strict = same & (ri > ci)

    ba = ba_ref[...]
    beta_full = _sigmoid(ba)
    g_full = -jnp.exp(alog_ref[...]) * _softplus(ba + dt_ref[...])
    masks = jnp.concatenate([jnp.where(incl, 1.0, 0.0), jnp.where(same, 1.0, 0.0)], axis=0).astype(BF16)
    gsums = [_dot_mask(masks, g_full[u * UNIT:(u + 1) * UNIT]) for u in range(units)]
    g_cum = [g[:UNIT] for g in gsums]
    g_tot = [g[UNIT:] for g in gsums]
    g_cum_t = [g.T for g in g_cum]

    probs = [(u, h) for u in range(units) for h in range(DN_HEADS)]
    n_p = len(probs)
    rs = lambda u: slice(u * UNIT, (u + 1) * UNIT)

    def head_cols(base):
        return [qkv_ref[rs(u), base + h * DN_DK:base + (h + 1) * DN_DK] for u, h in probs]

    q = head_cols(0)
    k = head_cols(DN_QK)
    v = head_cols(2 * DN_QK)
    gc = [g_cum[u][:, DN_HEADS + h:DN_HEADS + h + 1] for u, h in probs]
    gr = [g_cum_t[u][DN_HEADS + h:DN_HEADS + h + 1, :] for u, h in probs]
    gt = [g_tot[u][:, DN_HEADS + h:DN_HEADS + h + 1] for u, h in probs]
    bc = [beta_full[rs(u), h:h + 1] for u, h in probs]
    decay = [jnp.where(incl, jnp.exp(jnp.where(incl, gc[p] - gr[p], 0.0)), 0.0) for p in range(n_p)]
    e_g = [jnp.exp(gc[p]) for p in range(n_p)]
    kq = [_dot_nt(jnp.concatenate([k[p], q[p]], axis=0), k[p]) for p in range(n_p)]
    qk = [kq[p][UNIT:] * decay[p] for p in range(n_p)]
    a_mats = [jnp.where(strict, bc[p] * kq[p][:UNIT] * decay[p], 0.0) for p in range(n_p)]
    t_inv = _unit_lower_inverses(a_mats, ri, ci, shift)
    wvk = [_dot(t_inv[p], jnp.concatenate([v[p] * bc[p], k[p] * (bc[p] * e_g[p])], axis=1)) for p in range(n_p)]
    w_v = [w[:, :DN_DV] for w in wvk]
    w_k = [w[:, DN_DV:] for w in wvk]
    q_dec = [q[p] * e_g[p] for p in range(n_p)]
    k_tail_t = [(k[p] * jnp.exp(gt[p] - gc[p])).T for p in range(n_p)]
    c_dec = [jnp.exp(gt[p]) for p in range(n_p)]

    outs = {}
    if carry:
        state = [s_ref[0, h] for h in range(DN_HEADS)]
        for u in range(units):
            ps = [u * DN_HEADS + h for h in range(DN_HEADS)]
            r = [_dot(jnp.concatenate([w_k[p], q_dec[p]], axis=0), state[h]) for h, p in enumerate(ps)]
            uu = [w_v[p] - r[h][:UNIT] for h, p in enumerate(ps)]
            for h, p in enumerate(ps):
                outs[p] = r[h][UNIT:] + _dot(qk[p], uu[h])
            state = [state[h] * c_dec[p][0:1, :] + _dot(k_tail_t[p], uu[h]) for h, p in enumerate(ps)]
        for h in range(DN_HEADS):
            s_ref[0, h] = state[h]
    else:
        per_unit = nb // units
        for p, (u, h) in enumerate(probs):
            us, qs = [], []
            for s in range(per_unit):
                lo = s * chunk
                lhs = jnp.concatenate([w_k[p][lo:lo + chunk], q_dec[p][lo:lo + chunk]], axis=0)
                r = jnp.dot(lhs, s0_ref[u * per_unit + s, h], preferred_element_type=F32)
                us.append(w_v[p][lo:lo + chunk] - r[:chunk])
                qs.append(r[chunk:])
            uu = jnp.concatenate(us, axis=0)
            outs[p] = jnp.concatenate(qs, axis=0) + _dot(qk[p], uu)
            for s in range(per_unit):
                lo = s * chunk
                upd = jnp.dot(k_tail_t[p][:, lo:lo + chunk], uu[lo:lo + chunk], preferred_element_type=F32)
                s_ref[u * per_unit + s, h] = s0_ref[u * per_unit + s, h] * c_dec[p][lo:lo + 1, :] + upd

    for p, (u, h) in enumerate(probs):
        zz = z_ref[rs(u), h * DN_DV:(h + 1) * DN_DV]
        o_ref[rs(u), h * DN_DV:(h + 1) * DN_DV] = _rms(outs[p], nw_ref[...]) * _silu(zz)


def _dn_call(qkv, z, ba, alog_row, dt_row, norm_w, bsz, seq, state):
    rows = DN_UNITS_STATE * UNIT
    nb = rows // seq
    row_map = lambda i, j: (i, 0)
    const = lambda i, j: (0, 0)
    per_seq = lambda i, j: (i, 0, 0, 0)
    return pl.pallas_call(
        functools.partial(_dn_kernel, nb=nb, chunk=seq, carry=False, units=DN_UNITS_STATE),
        grid=(bsz // nb, 1),
        in_specs=[pl.BlockSpec((rows, DN_CONV_CH), row_map),
                  pl.BlockSpec((rows, DN_V), row_map),
                  pl.BlockSpec((rows, BA_PAD), row_map),
                  pl.BlockSpec((1, BA_PAD), const),
                  pl.BlockSpec((1, BA_PAD), const),
                  pl.BlockSpec((1, DN_DV), const),
                  pl.BlockSpec((nb, DN_HEADS, DN_DK, DN_DV), per_seq)],
        out_specs=[pl.BlockSpec((rows, DN_V), row_map),
                   pl.BlockSpec((nb, DN_HEADS, DN_DK, DN_DV), per_seq)],
        out_shape=[jax.ShapeDtypeStruct((bsz * seq, DN_V), F32),
                   jax.ShapeDtypeStruct((bsz, DN_HEADS, DN_DK, DN_DV), F32)],
        compiler_params=_cparams(("arbitrary", "arbitrary")),
        name="gated_deltanet_state",
    )(qkv, z, ba, alog_row, dt_row, norm_w, state)


MIX_SUB_TILES = 2
SUB_ROWS = 2 * UNIT
MIX_ROWS = MIX_SUB_TILES * SUB_ROWS
SAMPLE_IN_SEQS = 64
SAMPLE_POST_SEQS = 32


def _mix_kernel(x_ref, mod_ref, nw_ref, w_ref, cw_ref, alog_ref, dt_ref, dnw_ref,
                sq_ref, sk_ref, sv_ref, tail_ref, o_ref, s_ref,
                h_scr, wdn_ref, wba_ref, wsw_ref, qkv_scr, z_scr, ba_scr):
    _in_kernel(x_ref, mod_ref, nw_ref, w_ref, cw_ref,
               qkv_scr, z_scr, ba_scr, sq_ref, sk_ref, sv_ref, tail_ref, h_scr, wdn_ref, wba_ref, wsw_ref,
               nb=1, lt=SUB_ROWS, carry=True, n_sub=MIX_SUB_TILES)
    _dn_kernel(qkv_scr, z_scr, ba_scr, alog_ref, dt_ref, dnw_ref, o_ref, s_ref,
               nb=1, chunk=UNIT, carry=True, units=MIX_ROWS // UNIT)


def _mix_call(x, mod3, norm_w, w_in_t, conv_w, alog_row, dt_row, dn_norm_w):
    bsz, seq, _ = x.shape
    nt = seq // MIX_ROWS
    n_tok = bsz * seq
    row_map = lambda i, j: (i * nt + j, 0)
    const = lambda i, j: (0, 0)
    per_seq = lambda i, j: (i, 0, 0)
    swa_cols = (SWA_Q, SWA_KV, SWA_KV)
    return pl.pallas_call(
        _mix_kernel,
        grid=(bsz, nt),
        in_specs=[pl.BlockSpec((1, MIX_ROWS, D_MODEL), lambda i, j: (i, j, 0)),
                  pl.BlockSpec((1, 6, D_MODEL), per_seq),
                  pl.BlockSpec((1, D_MODEL), const),
                  pl.BlockSpec(w_in_t.shape, const, pipeline_mode=RESIDENT),
                  pl.BlockSpec((DN_CONV, DN_CONV_CH), const),
                  pl.BlockSpec((1, BA_PAD), const),
                  pl.BlockSpec((1, BA_PAD), const),
                  pl.BlockSpec((1, DN_DV), const)],
        out_specs=[pl.BlockSpec((MIX_ROWS, n), row_map) for n in swa_cols]
        + [pl.BlockSpec((1, SUBLANES, DN_CONV_CH), per_seq),
           pl.BlockSpec((MIX_ROWS, DN_V), row_map),
           pl.BlockSpec((1, DN_HEADS, DN_DK, DN_DV), lambda i, j: (i, 0, 0, 0))],
        out_shape=[jax.ShapeDtypeStruct((n_tok, n), F32) for n in swa_cols]
        + [jax.ShapeDtypeStruct((bsz, SUBLANES, DN_CONV_CH), F32),
           jax.ShapeDtypeStruct((n_tok, DN_V), F32),
           jax.ShapeDtypeStruct((bsz, DN_HEADS, DN_DK, DN_DV), F32)],
        scratch_shapes=[pltpu.VMEM((MIX_SUB_TILES, SUB_ROWS, D_MODEL + ROW_PITCH_PAD), BF16),
                        pltpu.VMEM((D_MODEL, DN_CONV_CH + DN_V + ROW_PITCH_PAD), BF16),
                        pltpu.VMEM((D_MODEL, BA_PAD), BF16),
                        pltpu.VMEM((D_MODEL, SWA_Q + 2 * SWA_KV), BF16),
                        pltpu.VMEM((MIX_ROWS, DN_CONV_CH), F32),
                        pltpu.VMEM((MIX_ROWS, DN_V + ROW_PITCH_PAD), F32),
                        pltpu.VMEM((MIX_ROWS, BA_PAD), F32)],
        compiler_params=_cparams(("arbitrary", "arbitrary")),
        name="in_proj_deltanet_carry",
    )(x, mod3, norm_w, w_in_t, conv_w, alog_row, dt_row, dn_norm_w)


def _bucket_table():
    i = np.arange(WINDOW, dtype=np.int64)[:, None]
    j = np.arange(2 * WINDOW, dtype=np.int64)[None, :]
    d = np.maximum(i + WINDOW - j, 0)
    exact = N_BUCKETS // 2
    logv = (np.log(np.maximum(d, 1).astype(np.float32) / np.float32(exact)).astype(np.float32)
            / np.float32(math.log(MAX_DISTANCE / exact)))
    large = np.minimum(exact + (logv * np.float32(N_BUCKETS - exact)).astype(np.int32), N_BUCKETS - 1)
    return np.where(d < exact, d, large).astype(np.int32)


def _sample_bucket_table(bucket):
    out = bucket[:SUBLANES].copy()
    out[:, 2 * WINDOW - SUBLANES:] = bucket[:SUBLANES, WINDOW:WINDOW + SUBLANES]
    return out


def _bias_lookup(rb_ref, bucket, head):
    acc = jnp.zeros(bucket.shape, F32)
    for b in range(N_BUCKETS):
        acc = jnp.where(bucket == b, rb_ref[b, head], acc)
    return acc


def _bias_kernel(rb_ref, bucket_t_ref, bucket_ref, ot_ref, os_ref):
    bucket_t = bucket_t_ref[...]
    kj = lax.broadcasted_iota(jnp.int32, bucket_t.shape, 0)
    qi = lax.broadcasted_iota(jnp.int32, bucket_t.shape, 1)
    dist = qi + WINDOW - kj
    valid = (dist >= 0) & (dist < WINDOW)
    for hk in range(SWA_KV_HEADS):
        for par in range(2):
            for st in range(SWA_GROUP // 2):
                head = hk * SWA_GROUP + 2 * st + par
                gen = jnp.where(valid, _bias_lookup(rb_ref, bucket_t, head) * LOG2_E, NEG_INF)
                ot_ref[1, hk, par, :, st * WINDOW:(st + 1) * WINDOW] = gen
                ot_ref[0, hk, par, :, st * WINDOW:(st + 1) * WINDOW] = jnp.where(kj >= WINDOW, gen, NEG_INF)
    bucket = bucket_ref[...]
    qi = lax.broadcasted_iota(jnp.int32, bucket.shape, 0)
    col = lax.broadcasted_iota(jnp.int32, bucket.shape, 1)
    kj = jnp.where(col < WINDOW, col, col - (WINDOW - SUBLANES))
    dist = qi + WINDOW - kj
    valid = (dist >= 0) & (dist < WINDOW) & ((col < WINDOW) | (col >= 2 * WINDOW - SUBLANES))
    for head in range(SWA_HEADS):
        os_ref[head] = jnp.where(valid, _bias_lookup(rb_ref, bucket, head), NEG_INF)


BIAS_T_SHAPE = (2, SWA_KV_HEADS, 2, 2 * WINDOW, (SWA_GROUP // 2) * WINDOW)
BIAS_S_SHAPE = (SWA_HEADS, SUBLANES, 2 * WINDOW)


def _bias_call(rel_bias):
    bucket = _bucket_table()
    return pl.pallas_call(
        _bias_kernel,
        in_specs=[pl.BlockSpec(memory_space=pltpu.SMEM),
                  pl.BlockSpec((2 * WINDOW, WINDOW), lambda: (0, 0)),
                  pl.BlockSpec((SUBLANES, 2 * WINDOW), lambda: (0, 0))],
        out_specs=[pl.BlockSpec(BIAS_T_SHAPE, lambda: (0,) * len(BIAS_T_SHAPE)),
                   pl.BlockSpec(BIAS_S_SHAPE, lambda: (0,) * len(BIAS_S_SHAPE))],
        out_shape=[jax.ShapeDtypeStruct(BIAS_T_SHAPE, F32), jax.ShapeDtypeStruct(BIAS_S_SHAPE, F32)],
        name="swa_rel_bias_table",
    )(rel_bias, jnp.asarray(np.ascontiguousarray(bucket.T)), jnp.asarray(_sample_bucket_table(bucket)))


def _softmax_sink_parts(s, sink):
    m = jnp.maximum(jnp.max(s, axis=0, keepdims=True), sink)
    p = jnp.exp2(s - m)
    return p, 1.0 / (jnp.sum(p, axis=0, keepdims=True) + jnp.exp2(sink - m))


def _half_lane_variants(full, hk, lo_half):
    rolled = pltpu.roll(full, SWA_HD, axis=1)
    low_src, high_src = (full, rolled) if hk == 0 else (rolled, full)
    return jnp.where(lo_half, low_src, 0.0), jnp.where(lo_half, 0.0, high_src)


def _swa_prompt_kernel(sink_ref, q_ref, kp_ref, kc_ref, vp_ref, vc_ref, bias_ref, o_ref, *, n_blk):
    step = pl.program_id(1)
    lo_half = lax.broadcasted_iota(jnp.int32, (1, 2 * SWA_HD), 1) < SWA_HD
    lo_rows = lax.broadcasted_iota(jnp.int32, (2 * SWA_HD, 1), 0) < SWA_HD
    q = _bf(q_ref[...] * (SWA_HD ** -0.5 * LOG2_E))
    keys = jnp.concatenate([kp_ref[...], kc_ref[...]], axis=0)
    vals = jnp.concatenate([vp_ref[...], vc_ref[...]], axis=0)
    k_var = [[_bf(t) for t in _half_lane_variants(keys, hk, lo_half)] for hk in range(SWA_KV_HEADS)]
    v_var_t = [[_bf(t.T) for t in _half_lane_variants(vals, hk, lo_half)] for hk in range(SWA_KV_HEADS)]
    n_stack = SWA_GROUP // 2
    sinks = [[jnp.concatenate([jnp.full((1, WINDOW), sink_ref[hk * SWA_GROUP + 2 * st + par] * LOG2_E, F32)
                               for st in range(n_stack)], axis=1) for par in range(2)]
             for hk in range(SWA_KV_HEADS)]
    nt_dims = (((1,), (1,)), ((), ()))

    def scores(b):
        rows = slice(b * WINDOW, (b + 1) * WINDOW)
        win = slice(b * WINDOW, (b + 2) * WINDOW)
        variant = jnp.where(step == 0, 0, 1) if b == 0 else 1
        out = []
        for hk in range(SWA_KV_HEADS):
            q2 = jnp.concatenate([q[rows, (hk * n_stack + st) * 2 * SWA_HD:(hk * n_stack + st + 1) * 2 * SWA_HD]
                                  for st in range(n_stack)], axis=0)
            out.append([lax.dot_general(k_var[hk][par][win], q2, nt_dims, preferred_element_type=F32)
                        + bias_ref[variant, hk, par] for par in range(2)])
        return out

    s_next = scores(0)
    for b in range(n_blk):
        s_cur = s_next
        if b + 1 < n_blk:
            s_next = scores(b + 1)
        rows = slice(b * WINDOW, (b + 1) * WINDOW)
        win = slice(b * WINDOW, (b + 2) * WINDOW)
        for hk in range(SWA_KV_HEADS):
            parts = [_softmax_sink_parts(s_cur[hk][par], sinks[hk][par]) for par in range(2)]
            o_t = (jnp.dot(v_var_t[hk][0][:, win], _bf(parts[0][0]), preferred_element_type=F32)
                   + jnp.dot(v_var_t[hk][1][:, win], _bf(parts[1][0]), preferred_element_type=F32))
            o_t = o_t * jnp.where(lo_rows, parts[0][1], parts[1][1])
            for st in range(n_stack):
                lo = (hk * n_stack + st) * 2 * SWA_HD
                o_ref[rows, lo:lo + 2 * SWA_HD] = o_t[:, st * WINDOW:(st + 1) * WINDOW].T


def _swa_prompt_call(sinks, sq, sk, sv, bias, bsz, seq):
    n_blk = SWA_BLOCKS_PER_STEP
    tile = n_blk * WINDOW
    nt = seq // tile
    cur = lambda b, i: (b * nt + i, 0)
    prv = lambda b, i: (b * nt * n_blk + jnp.maximum(i * n_blk - 1, 0), 0)
    return pl.pallas_call(
        functools.partial(_swa_prompt_kernel, n_blk=n_blk),
        grid=(bsz, nt),
        in_specs=[pl.BlockSpec(memory_space=pltpu.SMEM),
                  pl.BlockSpec((tile, SWA_Q), cur),
                  pl.BlockSpec((WINDOW, SWA_KV), prv),
                  pl.BlockSpec((tile, SWA_KV), cur),
                  pl.BlockSpec((WINDOW, SWA_KV), prv),
                  pl.BlockSpec((tile, SWA_KV), cur),
                  pl.BlockSpec(BIAS_T_SHAPE, lambda b, i: (0,) * len(BIAS_T_SHAPE))],
        out_specs=pl.BlockSpec((tile, SWA_Q), cur),
        out_shape=jax.ShapeDtypeStruct((bsz * seq, SWA_Q), F32),
        compiler_params=_cparams(("arbitrary", "arbitrary")),
        name="swa_banded",
    )(sinks, sq, sk, sk, sv, sv, bias)


def _swa_sample_kernel(sink_ref, q_ref, kn_ref, vn_ref, kc_ref, vc_ref, bias_ref, o_ref, ko_ref, vo_ref, *, nb, lt):
    lane = lax.broadcasted_iota(jnp.int32, (1, 2 * SWA_HD), 1)
    lo_half = lane < SWA_HD
    new_lanes = lane >= WINDOW - lt
    q_all = q_ref[...] * (SWA_HD ** -0.5)
    kn_all_t = kn_ref[...].T
    vn_all_t = vn_ref[...].T
    bias = jnp.concatenate([bias_ref[h, 0:lt, :] for h in range(SWA_HEADS)], axis=0)
    sink = jnp.concatenate([jnp.full((lt, 1), sink_ref[h], F32) for h in range(SWA_HEADS)], axis=0)
    nt_dims = (((1,), (1,)), ((), ()))

    def to_half(tile, src_half, dst_half):
        return tile if src_half == dst_half else pltpu.roll(tile, SWA_HD, axis=1)

    lhs, keys, vals = [], [], []
    for s in range(nb):
        rows = slice(s * lt, (s + 1) * lt)
        to_end = (WINDOW - lt - s * lt) % WINDOW
        for c_ref, n_all_t, out_ref, acc in ((kc_ref, kn_all_t, ko_ref, keys), (vc_ref, vn_all_t, vo_ref, vals)):
            cached = c_ref[s]
            new = n_all_t if to_end == 0 else pltpu.roll(n_all_t, to_end, axis=1)
            out_ref[s] = jnp.where(new_lanes, new, pltpu.roll(cached, WINDOW - lt, axis=1))
            acc.append(_bf(jnp.concatenate([cached, jnp.where(new_lanes, new, 0.0)], axis=1)))
        q = q_all[rows]
        tiles = []
        for h in range(SWA_HEADS):
            hk = h // SWA_GROUP
            t = to_half(q[:, (h // 2) * 2 * SWA_HD:(h // 2 + 1) * 2 * SWA_HD], h % 2, hk)
            tiles.append(jnp.where(lo_half if hk == 0 else jnp.logical_not(lo_half), t, 0.0))
        lhs.append(_bf(jnp.concatenate(tiles, axis=0)))
    scores = [jnp.dot(lhs[s], keys[s], preferred_element_type=F32) + bias for s in range(nb)]
    m = [jnp.maximum(jnp.max(sc, axis=-1, keepdims=True), sink) for sc in scores]
    p = [jnp.exp(sc - mm) for sc, mm in zip(scores, m)]
    rinv = [1.0 / (jnp.sum(pp, axis=-1, keepdims=True) + jnp.exp(sink - mm)) for pp, mm in zip(p, m)]
    res = [lax.dot_general(_bf(p[s]), vals[s], nt_dims, preferred_element_type=F32) * rinv[s] for s in range(nb)]
    rows_out = []
    for s in range(nb):
        tiles = []
        for pair in range(SWA_HEADS // 2):
            hk = (2 * pair) // SWA_GROUP
            low = to_half(res[s][(2 * pair) * lt:(2 * pair + 1) * lt], hk, 0)
            high = to_half(res[s][(2 * pair + 1) * lt:(2 * pair + 2) * lt], hk, 1)
            tiles.append(jnp.where(lo_half, low, high))
        rows_out.append(jnp.concatenate(tiles, axis=1))
    o_ref[...] = jnp.concatenate(rows_out, axis=0)


def _swa_sample_call(sinks, sq, sk, sv, cache_k, cache_v, bias, bsz, seq):
    assert seq == SUBLANES, "the sample bias table places one sublane tile of new keys"
    nb = UNIT // seq
    rows = lambda i: (i, 0)
    seqs = lambda i: (i, 0, 0)
    return pl.pallas_call(
        functools.partial(_swa_sample_kernel, nb=nb, lt=seq),
        grid=(bsz // nb,),
        in_specs=[pl.BlockSpec(memory_space=pltpu.SMEM),
                  pl.BlockSpec((UNIT, SWA_Q), rows),
                  pl.BlockSpec((UNIT, SWA_KV), rows),
                  pl.BlockSpec((UNIT, SWA_KV), rows),
                  pl.BlockSpec((nb, SWA_KV, WINDOW), seqs),
                  pl.BlockSpec((nb, SWA_KV, WINDOW), seqs),
                  pl.BlockSpec(BIAS_S_SHAPE, lambda i: (0,) * len(BIAS_S_SHAPE))],
        out_specs=[pl.BlockSpec((UNIT, SWA_Q), rows),
                   pl.BlockSpec((nb, SWA_KV, WINDOW), seqs),
                   pl.BlockSpec((nb, SWA_KV, WINDOW), seqs)],
        out_shape=[jax.ShapeDtypeStruct((bsz * seq, SWA_Q), F32),
                   jax.ShapeDtypeStruct((bsz, SWA_KV, WINDOW), F32),
                   jax.ShapeDtypeStruct((bsz, SWA_KV, WINDOW), F32)],
        compiler_params=_cparams(("arbitrary",)),
        name="swa_cached",
    )(sinks, sq, sk, sv, cache_k, cache_v, bias)


def _post_kernel(*refs, nb, lt, carry, n_sub):
    if carry:
        (odn_ref, oswa_ref, x_ref, mod_ref, wout_ref, nfw_ref, wup_ref, cw_ref, cb_ref, wdn_ref, fnw_ref,
         y_ref, fbuf_ref, h_scr, x1_scr, acc_scr) = refs
        prev_ref = fbuf_ref

        @pl.when(pl.program_id(1) == 0)
        def _():
            fbuf_ref[...] = jnp.zeros_like(fbuf_ref)
    else:
        (odn_ref, oswa_ref, x_ref, mod_ref, wout_ref, nfw_ref, wup_ref, cw_ref, cb_ref, wdn_ref, fnw_ref,
         prev_ref, y_ref, fbuf_ref, h_scr, x1_scr, acc_scr) = refs
    rows = nb * lt
    n_chunks = D_FF // FFN_CHUNK

    def col_slices(c):
        return [slice(base + c * FFN_CHUNK, base + (c + 1) * FFN_CHUNK) for base in (0, D_FF)]

    def prologue(sub):
        rr = slice(sub * rows, (sub + 1) * rows)
        attn = (jnp.dot(_bf(odn_ref[rr, :]), wout_ref[0:DN_V, :], preferred_element_type=F32)
                + jnp.dot(_bf(oswa_ref[rr, :]), wout_ref[DN_V:, :], preferred_element_type=F32))
        x = x_ref[:, sub * lt:(sub + 1) * lt, :].reshape(rows, D_MODEL)
        x1 = x + _rows(mod_ref[:, 2:3, :], nb, lt) * attn
        x1_scr[sub] = x1
        h = (_rms(x1, nfw_ref[...]) * (1.0 + _rows(mod_ref[:, 4:5, :], nb, lt))
             + _rows(mod_ref[:, 3:4, :], nb, lt))
        h_scr[sub] = _bf(h)
        acc_scr[sub] = jnp.zeros((rows, D_MODEL), F32)

    def up_proj(sub, c):
        return [jnp.dot(h_scr[sub], wup_ref[:, cols], preferred_element_type=F32) for cols in col_slices(c)]

    def chunk(sub, c, u_cur):
        halves = []
        for u, cols in zip(u_cur, col_slices(c)):
            prev = prev_ref[:, :, cols]
            halves.append(_causal_conv(u, prev, cw_ref[:, cols], nb, lt) + cb_ref[:, cols])
            fbuf_ref[:, :, cols] = _last_rows(u, nb, lt, FFN_CONV - 1)
        act = _silu(halves[0]) * halves[1]
        acc_scr[sub] += jnp.dot(_bf(act), wdn_ref[c * FFN_CHUNK:(c + 1) * FFN_CHUNK, :],
                                preferred_element_type=F32)

    def epilogue(sub):
        x2 = x1_scr[sub] + _rows(mod_ref[:, 5:6, :], nb, lt) * acc_scr[sub]
        y_ref[:, sub * lt:(sub + 1) * lt, :] = _rms(x2, fnw_ref[...]).reshape(nb, lt, D_MODEL)

    prologue(0)
    for sub in range(n_sub):
        pending = [up_proj(sub, c) for c in range(min(UP_PROJ_AHEAD, n_chunks))]
        for c in range(n_chunks):
            u_cur = pending.pop(0)
            if c + UP_PROJ_AHEAD < n_chunks:
                pending.append(up_proj(sub, c + UP_PROJ_AHEAD))
            chunk(sub, c, u_cur)
            if sub + 1 < n_sub and c == n_chunks - 1 - POST_LOOKAHEAD_CHUNKS:
                prologue(sub + 1)
        epilogue(sub)


def _post_call(o_dn, o_swa, x, mod3, w_out, norm_ffn_w, w_up, conv_w, conv_b, w_down, final_w, nb, lt, state=None):
    carry = state is None
    n_sub = POST_SUB_TILES if carry else 1
    bsz, seq, _ = x.shape
    nt = seq // (lt * n_sub)
    rows = nb * lt
    tile = rows * n_sub
    n_state = _state_rows(nb, lt, FFN_CONV - 1)
    row_map = lambda i, j: (i * nt + j, 0)
    const = lambda i, j: (0, 0)
    in_specs = [pl.BlockSpec((tile, DN_V), row_map),
                pl.BlockSpec((tile, SWA_Q), row_map),
                pl.BlockSpec((nb, lt * n_sub, D_MODEL), lambda i, j: (i, j, 0)),
                pl.BlockSpec((nb, 6, D_MODEL), lambda i, j: (i, 0, 0)),
                pl.BlockSpec((D_MODEL, D_MODEL), const, pipeline_mode=RESIDENT),
                pl.BlockSpec((1, D_MODEL), const),
                pl.BlockSpec(w_up.shape, const, pipeline_mode=RESIDENT),
                pl.BlockSpec((FFN_CONV, 2 * D_FF), const),
                pl.BlockSpec((1, 2 * D_FF), const),
                pl.BlockSpec((D_FF, D_MODEL), const, pipeline_mode=RESIDENT),
                pl.BlockSpec((1, D_MODEL), const)]
    args = [o_dn, o_swa, x, mod3, w_out, norm_ffn_w, w_up, conv_w, conv_b, w_down, final_w]
    if not carry:
        in_specs.append(pl.BlockSpec((nb, FFN_CONV - 1, 2 * D_FF), lambda i, j: (i, 0, 0)))
        args.append(state)
    return pl.pallas_call(
        functools.partial(_post_kernel, nb=nb, lt=lt, carry=carry, n_sub=n_sub),
        grid=(bsz // nb, nt),
        in_specs=in_specs,
        out_specs=[pl.BlockSpec((nb, lt * n_sub, D_MODEL), lambda i, j: (i, j, 0)),
                   pl.BlockSpec((nb, n_state, 2 * D_FF), lambda i, j: (i, 0, 0))],
        out_shape=[jax.ShapeDtypeStruct((bsz, seq, D_MODEL), F32),
                   jax.ShapeDtypeStruct((bsz, n_state, 2 * D_FF), F32)],
        scratch_shapes=[pltpu.VMEM((n_sub, rows, D_MODEL), BF16),
                        pltpu.VMEM((n_sub, rows, D_MODEL), F32),
                        pltpu.VMEM((n_sub, rows, D_MODEL), F32)],
        compiler_params=_cparams(("arbitrary", "arbitrary")),
        name="out_proj_convffn_carry" if carry else "out_proj_convffn_state",
    )(*args)


def _pad_row(vec, offset):
    return jnp.zeros((1, BA_PAD), F32).at[0, offset:offset + vec.shape[0]].set(vec)


def kernel(x_prompt, x_sample, state_dn_conv, state_dn_ssm, cache_swa_k, cache_swa_v, state_ffn_conv, c_prompt, c_sample, rel_bias, final_norm_w, w_ada, b_ada, norm_mix_w, w_in, dn_conv_w, dn_A_log, dn_dt_bias, dn_norm_w, swa_sinks, w_out, norm_ffn_w, ffn_w_up, ffn_conv_w, ffn_conv_b, ffn_w_down):
    bp, lp, _ = x_prompt.shape
    bs, ls, _ = x_sample.shape
    layer = 0

    w_in_t = jnp.transpose(w_in[layer])
    w_out_b = w_out[layer].astype(BF16)
    w_up_b = ffn_w_up[layer].astype(BF16)
    w_dn_b = ffn_w_down[layer].astype(BF16)
    alog_row = _pad_row(dn_A_log[layer], DN_HEADS)
    dt_row = _pad_row(dn_dt_bias[layer], DN_HEADS)
    row = lambda v: v.reshape(1, -1)

    n_c = bp + bs
    n_c_pad = -(-n_c // SUBLANES) * SUBLANES
    c_all = jnp.pad(jnp.concatenate([c_sample, c_prompt], axis=0), ((0, n_c_pad - n_c), (0, 0)))
    mod_s, mod_p = _mod_call(c_all, w_ada[layer], row(b_ada[layer]), bs, bp)

    bias_t, bias_s = _bias_call(rel_bias)
    sinks = swa_sinks[layer]

    def post(o_dn, o_swa, x, mod3, nb, lt, state=None):
        return _post_call(o_dn, o_swa, x, mod3, w_out_b, row(norm_ffn_w[layer]), w_up_b, ffn_conv_w[layer],
                          row(ffn_conv_b[layer]), w_dn_b, row(final_norm_w), nb, lt, state)

    dn_args = (alog_row, dt_row, row(dn_norm_w[layer]))

    sq, sk, sv, p_tail, o_dn, p_ssm = _mix_call(x_prompt, mod_p, row(norm_mix_w[layer]), w_in_t,
                                                dn_conv_w[layer], *dn_args)
    o_swa = _swa_prompt_call(sinks, sq, sk, sv, bias_t, bp, lp)
    y_prompt, p_ffn_tail = post(o_dn, o_swa, x_prompt, mod_p, 1, SUB_ROWS)
    p_dn_conv = p_tail[:, SUBLANES - (DN_CONV - 1):]
    p_fbuf = p_ffn_tail[:, SUBLANES - (FFN_CONV - 1):]
    last_window = lambda t: t.reshape(bp, lp, SWA_KV)[:, lp - WINDOW:].reshape(bp, WINDOW, SWA_KV_HEADS, SWA_HD)
    p_swa_k = last_window(sk)
    p_swa_v = last_window(sv)

    cache_t = lambda c: jnp.transpose(c[layer].reshape(bs, WINDOW, SWA_KV), (0, 2, 1))
    cache_from_t = lambda c: jnp.transpose(c, (0, 2, 1)).reshape(bs, WINDOW, SWA_KV_HEADS, SWA_HD)
    qkv_s, z_s, ba_s, sq_s, sk_s, sv_s, s_dn_conv = _in_call(
        x_sample, mod_s, row(norm_mix_w[layer]), w_in_t, dn_conv_w[layer], SAMPLE_IN_SEQS, state_dn_conv[layer])
    o_dn_s, s_ssm = _dn_call(qkv_s, z_s, ba_s, *dn_args, bs, ls, state=state_dn_ssm[layer])
    o_swa_s, s_k, s_v = _swa_sample_call(sinks, sq_s, sk_s, sv_s,
                                         cache_t(cache_swa_k), cache_t(cache_swa_v), bias_s, bs, ls)
    y_sample, s_fbuf = post(o_dn_s, o_swa_s, x_sample, mod_s, SAMPLE_POST_SEQS, ls, state=state_ffn_conv[layer])

    return (y_prompt, y_sample, p_dn_conv[None], s_dn_conv[None], p_ssm[None], s_ssm[None],
            p_swa_k[None], cache_from_t(s_k)[None],
            p_swa_v[None], cache_from_t(s_v)[None],
            p_fbuf[None], s_fbuf[None])
```

```python
import functools
import math

import numpy as np
import jax
import jax.numpy as jnp
from jax import lax
from jax.experimental import pallas as pl
from jax.experimental.pallas import tpu as pltpu

F32 = jnp.float32
BF16 = jnp.bfloat16

D_MODEL = 1024
DN_HEADS = 4
DN_DK = 128
DN_DV = 128
DN_CONV = 4
SWA_HEADS = 8
SWA_KV_HEADS = 2
SWA_GROUP = SWA_HEADS // SWA_KV_HEADS
SWA_HD = 64
WINDOW = 128
N_BUCKETS = 32
MAX_DISTANCE = 128
D_FF = 2816
FFN_CONV = 3
EPS = 1e-6
NEG_INF = -1e30
LOG2_E = 1.0 / math.log(2.0)

DN_QK = DN_HEADS * DN_DK
DN_V = DN_HEADS * DN_DV
DN_CONV_CH = 2 * DN_QK + DN_V
SWA_Q = SWA_HEADS * SWA_HD
SWA_KV = SWA_KV_HEADS * SWA_HD
BA_PAD = 128
SUBLANES = 8
UNIT = 128
POST_SUB_TILES = 2
POST_LOOKAHEAD_CHUNKS = 3
UP_PROJ_AHEAD = 2
FFN_CHUNK = 256
ROW_PITCH_PAD = 128
VMEM_LIMIT = 56 * 1024 * 1024
RESIDENT = pl.Buffered(1)


def _cparams(sem):
    return pltpu.CompilerParams(dimension_semantics=sem, vmem_limit_bytes=VMEM_LIMIT)


def _bf(x):
    return x.astype(BF16)


def _dot(a, b):
    return jnp.dot(_bf(a), _bf(b), preferred_element_type=F32)


def _dot_nt(a, b):
    return lax.dot_general(_bf(a), _bf(b), (((1,), (1,)), ((), ())), preferred_element_type=F32)


def _dot_mask(m, x):
    hi = _bf(x)
    r = x - hi.astype(F32)
    mid = _bf(r)
    lo = _bf(r - mid.astype(F32))
    d = functools.partial(jnp.dot, preferred_element_type=F32)
    return d(m, hi) + (d(m, mid) + d(m, lo))


def _sigmoid(x):
    return 1.0 / (1.0 + jnp.exp(-x))


def _silu(x):
    return x * _sigmoid(x)


def _softplus(x):
    return jnp.maximum(x, 0.0) + jnp.log1p(jnp.exp(-jnp.abs(x)))


def _rms(x, w):
    ms = jnp.mean(x * x, axis=-1, keepdims=True)
    return x * lax.rsqrt(ms + EPS) * w


def _l2norm(t):
    return t * lax.rsqrt(jnp.sum(t * t, axis=-1, keepdims=True) + EPS)


def _rows(m3, nb, lt):
    return jnp.broadcast_to(m3, (nb, lt, m3.shape[-1])).reshape(nb * lt, m3.shape[-1])


def _causal_conv(x, prev, w, nb, lt):
    width = w.shape[0]
    rows, ch = x.shape
    if nb == 1 and lt > SUBLANES:
        tiles = jnp.concatenate([prev, x.reshape(lt // SUBLANES, SUBLANES, ch)], axis=0)
        sub = lax.broadcasted_iota(jnp.int32, (1, SUBLANES, 1), 1)
        out = tiles[1:] * w[width - 1:width, :]
        for j in range(1, width):
            rot = pltpu.roll(tiles, j, axis=1)
            out = out + jnp.where(sub >= j, rot[1:], rot[:-1]) * w[width - 1 - j:width - j, :]
        return out.reshape(rows, ch)
    assert lt == SUBLANES
    x3 = x.reshape(nb, lt, ch)
    st = jnp.concatenate([jnp.zeros((nb, lt - (width - 1), ch), F32), prev], axis=1)
    sub = lax.broadcasted_iota(jnp.int32, (1, SUBLANES, 1), 1)
    out = x3 * w[width - 1:width, :]
    for j in range(1, width):
        sh = jnp.where(sub >= j, pltpu.roll(x3, j, axis=1), pltpu.roll(st, j, axis=1))
        out = out + sh * w[width - 1 - j:width - j, :]
    return out.reshape(rows, ch)


def _last_rows(x, nb, lt, n_state):
    ch = x.shape[-1]
    if nb == 1 and lt > SUBLANES:
        return x[lt - SUBLANES:].reshape(1, SUBLANES, ch)
    return x.reshape(nb, lt, ch)[:, lt - n_state:, :]


def _state_rows(nb, lt, n_state):
    return SUBLANES if (nb == 1 and lt > SUBLANES) else n_state


MOD_PARTS = 6


def _mod_kernel(c_ref, w_ref, b_ref, os_ref, op_ref, *, n_s, n_p):
    part = pl.program_id(0)
    res = _dot(_silu(c_ref[...]), w_ref[...]) + b_ref[...]
    for k in range(MOD_PARTS):
        @pl.when(part == k)
        def _():
            os_ref[:, k, :] = res[:n_s]
            op_ref[:, k, :] = res[n_s:n_s + n_p]


def _mod_call(c_all, w_ada, b_ada, n_s, n_p):
    rows = c_all.shape[0]
    return pl.pallas_call(
        functools.partial(_mod_kernel, n_s=n_s, n_p=n_p),
        grid=(MOD_PARTS,),
        in_specs=[pl.BlockSpec((rows, D_MODEL), lambda k: (0, 0)),
                  pl.BlockSpec((D_MODEL, D_MODEL), lambda k: (0, k)),
                  pl.BlockSpec((1, D_MODEL), lambda k: (0, k))],
        out_specs=[pl.BlockSpec((n_s, MOD_PARTS, D_MODEL), lambda k: (0, 0, 0)),
                   pl.BlockSpec((n_p, MOD_PARTS, D_MODEL), lambda k: (0, 0, 0))],
        out_shape=[jax.ShapeDtypeStruct((n_s, MOD_PARTS, D_MODEL), F32),
                   jax.ShapeDtypeStruct((n_p, MOD_PARTS, D_MODEL), F32)],
        compiler_params=_cparams(("arbitrary",)),
        name="adaln_mod",
    )(c_all, w_ada, b_ada)


IN_SPLIT = (DN_CONV_CH, DN_V, BA_PAD, SWA_Q, SWA_KV, SWA_KV)


IN_LOOKAHEAD_SLABS = 1
W_PREP_ROWS = 256
IN_SLAB = 2 * DN_DK


def _in_kernel(*refs, nb, lt, carry, n_sub):
    if carry:
        (x_ref, mod_ref, nw_ref, w_ref, cw_ref,
         qkv_ref, z_ref, ba_ref, sq_ref, sk_ref, sv_ref, tail_ref, h_scr, wdn_ref, wba_ref, wsw_ref) = refs
        prev_ref = tail_ref

        @pl.when(pl.program_id(1) == 0)
        def _():
            tail_ref[...] = jnp.zeros_like(tail_ref)
    else:
        (x_ref, mod_ref, nw_ref, w_ref, cw_ref, prev_ref,
         qkv_ref, z_ref, ba_ref, sq_ref, sk_ref, sv_ref, tail_ref, h_scr, wdn_ref, wba_ref, wsw_ref) = refs

    @pl.when((pl.program_id(0) == 0) & (pl.program_id(1) == 0))
    def _():
        ba_lo = DN_CONV_CH + DN_V
        n_ba = 2 * DN_HEADS
        for r in range(0, ba_lo, W_PREP_ROWS):
            wdn_ref[:, r:r + W_PREP_ROWS] = _bf(w_ref[r:r + W_PREP_ROWS, :].T)
        row = lax.broadcasted_iota(jnp.int32, (BA_PAD, 1), 0)
        wba_ref[...] = _bf(jnp.where(row < n_ba, w_ref[ba_lo:ba_lo + BA_PAD, :], 0.0).T)
        for r in range(0, SWA_Q + 2 * SWA_KV, W_PREP_ROWS):
            wsw_ref[:, r:r + W_PREP_ROWS] = _bf(w_ref[ba_lo + n_ba + r:ba_lo + n_ba + r + W_PREP_ROWS, :].T)

    rows = nb * lt

    def norm(sub):
        x = x_ref[:, sub * lt:(sub + 1) * lt, :]
        ms = jnp.mean(x * x, axis=-1, keepdims=True)
        y = x * lax.rsqrt(ms + EPS) * nw_ref[...]
        h = y * (1.0 + mod_ref[:, 1:2, :]) + mod_ref[:, 0:1, :]
        h_scr[sub, :, :D_MODEL] = _bf(h.reshape(rows, D_MODEL))

    def proj(sub, w_bf_ref, lo, n):
        return jnp.dot(h_scr[sub, :, :D_MODEL], w_bf_ref[:, lo:lo + n], preferred_element_type=F32)

    def plain_proj(sub, i):
        rr = slice(sub * rows, (sub + 1) * rows)
        if i < 2:
            z_ref[rr, i * IN_SLAB:(i + 1) * IN_SLAB] = proj(sub, wdn_ref, DN_CONV_CH + i * IN_SLAB, IN_SLAB)
        elif i < 4:
            sq_ref[rr, (i - 2) * IN_SLAB:(i - 1) * IN_SLAB] = proj(sub, wsw_ref, (i - 2) * IN_SLAB, IN_SLAB)
        elif i == 4:
            kv = proj(sub, wsw_ref, SWA_Q, 2 * SWA_KV)
            sk_ref[rr, :] = kv[:, :SWA_KV]
            sv_ref[rr, :] = kv[:, SWA_KV:]
        else:
            ba_ref[rr, :] = proj(sub, wba_ref, 0, BA_PAD)

    def conv_slab(sub, slab, raw):
        rr = slice(sub * rows, (sub + 1) * rows)
        cols = slice(slab * IN_SLAB, (slab + 1) * IN_SLAB)
        c = _silu(_causal_conv(raw, prev_ref[:, :, cols], cw_ref[:, cols], nb, lt))
        tail_ref[:, :, cols] = _last_rows(raw, nb, lt, DN_CONV - 1)
        if slab * IN_SLAB < 2 * DN_QK:
            scale = DN_DK ** -0.5 if slab * IN_SLAB < DN_QK else 1.0
            c = jnp.concatenate([_l2norm(c[:, i * DN_DK:(i + 1) * DN_DK]) * scale
                                 for i in range(IN_SLAB // DN_DK)], axis=1)
        qkv_ref[rr, cols] = c

    n_slabs = DN_CONV_CH // IN_SLAB
    norm(0)
    for sub in range(n_sub):
        raw_next = proj(sub, wdn_ref, 0, IN_SLAB)
        for slab in range(n_slabs):
            raw = raw_next
            plain_proj(sub, slab)
            if slab + 1 < n_slabs:
                raw_next = proj(sub, wdn_ref, (slab + 1) * IN_SLAB, IN_SLAB)
            if sub + 1 < n_sub and slab == n_slabs - 1 - IN_LOOKAHEAD_SLABS:
                norm(sub + 1)
            conv_slab(sub, slab, raw)


def _in_call(x, mod3, norm_w, w_in_t, conv_w, nb, state):
    bsz, lt, _ = x.shape
    rows = nb * lt
    row_map = lambda i, j: (i, 0)
    const = lambda i, j: (0, 0)
    per_seq = lambda i, j: (i, 0, 0)
    return pl.pallas_call(
        functools.partial(_in_kernel, nb=nb, lt=lt, carry=False, n_sub=1),
        grid=(bsz // nb, 1),
        in_specs=[pl.BlockSpec((nb, lt, D_MODEL), per_seq),
                  pl.BlockSpec((nb, 6, D_MODEL), per_seq),
                  pl.BlockSpec((1, D_MODEL), const),
                  pl.BlockSpec(w_in_t.shape, const, pipeline_mode=RESIDENT),
                  pl.BlockSpec((DN_CONV, DN_CONV_CH), const),
                  pl.BlockSpec((nb, DN_CONV - 1, DN_CONV_CH), per_seq)],
        out_specs=[pl.BlockSpec((rows, n), row_map) for n in IN_SPLIT]
        + [pl.BlockSpec((nb, DN_CONV - 1, DN_CONV_CH), per_seq)],
        out_shape=[jax.ShapeDtypeStruct((bsz * lt, n), F32) for n in IN_SPLIT]
        + [jax.ShapeDtypeStruct((bsz, DN_CONV - 1, DN_CONV_CH), F32)],
        scratch_shapes=[pltpu.VMEM((1, rows, D_MODEL + ROW_PITCH_PAD), BF16),
                        pltpu.VMEM((D_MODEL, DN_CONV_CH + DN_V + ROW_PITCH_PAD), BF16),
                        pltpu.VMEM((D_MODEL, BA_PAD), BF16),
                        pltpu.VMEM((D_MODEL, SWA_Q + 2 * SWA_KV), BF16)],
        compiler_params=_cparams(("arbitrary", "arbitrary")),
        name="norm_in_proj_state",
    )(x, mod3, norm_w, w_in_t, conv_w, state)


INV_BASE_SHIFT = 2
SWA_BLOCKS_PER_STEP = 16
DN_UNITS_STATE = 2


def _unit_lower_inverses(a_mats, ri, ci, chunk_shift):
    def blocks(s):
        return (ri >> s) == (ci >> s)

    base = min(INV_BASE_SHIFT, chunk_shift)
    xs = [jnp.where(blocks(base), -a, 0.0) for a in a_mats]
    ts = [jnp.where(ri == ci, 1.0, n) for n in xs]
    for lvl in range(1, base):
        xs = [_dot(x, x) for x in xs]
        ts = [t + _dot(t, x) for t, x in zip(ts, xs)]
    for s in range(base, chunk_shift):
        sel = blocks(s + 1) & jnp.logical_not(blocks(s))
        ets = [_dot(jnp.where(sel, a, 0.0), t) for a, t in zip(a_mats, ts)]
        ts = [t - _dot(t, et) for t, et in zip(ts, ets)]
    return ts


def _dn_kernel(*refs, nb, chunk, carry, units):
    if carry:
        (qkv_ref, z_ref, ba_ref, alog_ref, dt_ref, nw_ref, o_ref, s_ref) = refs
        s0_ref = s_ref

        @pl.when(pl.program_id(1) == 0)
        def _():
            s_ref[...] = jnp.zeros_like(s_ref)
    else:
        (qkv_ref, z_ref, ba_ref, alog_ref, dt_ref, nw_ref, s0_ref, o_ref, s_ref) = refs

    ri = lax.broadcasted_iota(jnp.int32, (UNIT, UNIT), 0)
    ci = lax.broadcasted_iota(jnp.int32, (UNIT, UNIT), 1)
    shift = int(math.log2(chunk))
    same = (ri >> shift) == (ci >> shift)
    incl = same & (ri >= ci)
    strict = same & (ri > ci)

    ba = ba_ref[...]
    beta_full = _sigmoid(ba)
    g_full = -jnp.exp(alog_ref[...]) * _softplus(ba + dt_ref[...])
    masks = jnp.concatenate([jnp.where(incl, 1.0, 0.0), jnp.where(same, 1.0, 0.0)], axis=0).astype(BF16)
    gsums = [_dot_mask(masks, g_full[u * UNIT:(u + 1) * UNIT]) for u in range(units)]
    g_cum = [g[:UNIT] for g in gsums]
    g_tot = [g[UNIT:] for g in gsums]
    g_cum_t = [g.T for g in g_cum]

    probs = [(u, h) for u in range(units) for h in range(DN_HEADS)]
    n_p = len(probs)
    rs = lambda u: slice(u * UNIT, (u + 1) * UNIT)

    def head_cols(base):
        return [qkv_ref[rs(u), base + h * DN_DK:base + (h + 1) * DN_DK] for u, h in probs]

    q = head_cols(0)
    k = head_cols(DN_QK)
    v = head_cols(2 * DN_QK)
    gc = [g_cum[u][:, DN_HEADS + h:DN_HEADS + h + 1] for u, h in probs]
    gr = [g_cum_t[u][DN_HEADS + h:DN_HEADS + h + 1, :] for u, h in probs]
    gt = [g_tot[u][:, DN_HEADS + h:DN_HEADS + h + 1] for u, h in probs]
    bc = [beta_full[rs(u), h:h + 1] for u, h in probs]
    decay = [jnp.where(incl, jnp.exp(jnp.where(incl, gc[p] - gr[p], 0.0)), 0.0) for p in range(n_p)]
    e_g = [jnp.exp(gc[p]) for p in range(n_p)]
    kq = [_dot_nt(jnp.concatenate([k[p], q[p]], axis=0), k[p]) for p in range(n_p)]
    qk = [kq[p][UNIT:] * decay[p] for p in range(n_p)]
    a_mats = [jnp.where(strict, bc[p] * kq[p][:UNIT] * decay[p], 0.0) for p in range(n_p)]
    t_inv = _unit_lower_inverses(a_mats, ri, ci, shift)
    wvk = [_dot(t_inv[p], jnp.concatenate([v[p] * bc[p], k[p] * (bc[p] * e_g[p])], axis=1)) for p in range(n_p)]
    w_v = [w[:, :DN_DV] for w in wvk]
    w_k = [w[:, DN_DV:] for w in wvk]
    q_dec = [q[p] * e_g[p] for p in range(n_p)]
    k_tail_t = [(k[p] * jnp.exp(gt[p] - gc[p])).T for p in range(n_p)]
    c_dec = [jnp.exp(gt[p]) for p in range(n_p)]

    outs = {}
    if carry:
        state = [s_ref[0, h] for h in range(DN_HEADS)]
        for u in range(units):
            ps = [u * DN_HEADS + h for h in range(DN_HEADS)]
            r = [_dot(jnp.concatenate([w_k[p], q_dec[p]], axis=0), state[h]) for h, p in enumerate(ps)]
            uu = [w_v[p] - r[h][:UNIT] for h, p in enumerate(ps)]
            for h, p in enumerate(ps):
                outs[p] = r[h][UNIT:] + _dot(qk[p], uu[h])
            state = [state[h] * c_dec[p][0:1, :] + _dot(k_tail_t[p], uu[h]) for h, p in enumerate(ps)]
        for h in range(DN_HEADS):
            s_ref[0, h] = state[h]
    else:
        per_unit = nb // units
        for p, (u, h) in enumerate(probs):
            us, qs = [], []
            for s in range(per_unit):
                lo = s * chunk
                lhs = jnp.concatenate([w_k[p][lo:lo + chunk], q_dec[p][lo:lo + chunk]], axis=0)
                r = jnp.dot(lhs, s0_ref[u * per_unit + s, h], preferred_element_type=F32)
                us.append(w_v[p][lo:lo + chunk] - r[:chunk])
                qs.append(r[chunk:])
            uu = jnp.concatenate(us, axis=0)
            outs[p] = jnp.concatenate(qs, axis=0) + _dot(qk[p], uu)
            for s in range(per_unit):
                lo = s * chunk
                upd = jnp.dot(k_tail_t[p][:, lo:lo + chunk], uu[lo:lo + chunk], preferred_element_type=F32)
                s_ref[u * per_unit + s, h] = s0_ref[u * per_unit + s, h] * c_dec[p][lo:lo + 1, :] + upd

    for p, (u, h) in enumerate(probs):
        zz = z_ref[rs(u), h * DN_DV:(h + 1) * DN_DV]
        o_ref[rs(u), h * DN_DV:(h + 1) * DN_DV] = _rms(outs[p], nw_ref[...]) * _silu(zz)


def _dn_call(qkv, z, ba, alog_row, dt_row, norm_w, bsz, seq, state):
    rows = DN_UNITS_STATE * UNIT
    nb = rows // seq
    row_map = lambda i, j: (i, 0)
    const = lambda i, j: (0, 0)
    per_seq = lambda i, j: (i, 0, 0, 0)
    return pl.pallas_call(
        functools.partial(_dn_kernel, nb=nb, chunk=seq, carry=False, units=DN_UNITS_STATE),
        grid=(bsz // nb, 1),
        in_specs=[pl.BlockSpec((rows, DN_CONV_CH), row_map),
                  pl.BlockSpec((rows, DN_V), row_map),
                  pl.BlockSpec((rows, BA_PAD), row_map),
                  pl.BlockSpec((1, BA_PAD), const),
                  pl.BlockSpec((1, BA_PAD), const),
                  pl.BlockSpec((1, DN_DV), const),
                  pl.BlockSpec((nb, DN_HEADS, DN_DK, DN_DV), per_seq)],
        out_specs=[pl.BlockSpec((rows, DN_V), row_map),
                   pl.BlockSpec((nb, DN_HEADS, DN_DK, DN_DV), per_seq)],
        out_shape=[jax.ShapeDtypeStruct((bsz * seq, DN_V), F32),
                   jax.ShapeDtypeStruct((bsz, DN_HEADS, DN_DK, DN_DV), F32)],
        compiler_params=_cparams(("arbitrary", "arbitrary")),
        name="gated_deltanet_state",
    )(qkv, z, ba, alog_row, dt_row, norm_w, state)


MIX_SUB_TILES = 2
SUB_ROWS = 2 * UNIT
MIX_ROWS = MIX_SUB_TILES * SUB_ROWS
SAMPLE_IN_SEQS = 64
SAMPLE_POST_SEQS = 32


def _mix_kernel(x_ref, mod_ref, nw_ref, w_ref, cw_ref, alog_ref, dt_ref, dnw_ref,
                sq_ref, sk_ref, sv_ref, tail_ref, o_ref, s_ref,
                h_scr, wdn_ref, wba_ref, wsw_ref, qkv_scr, z_scr, ba_scr):
    _in_kernel(x_ref, mod_ref, nw_ref, w_ref, cw_ref,
               qkv_scr, z_scr, ba_scr, sq_ref, sk_ref, sv_ref, tail_ref, h_scr, wdn_ref, wba_ref, wsw_ref,
               nb=1, lt=SUB_ROWS, carry=True, n_sub=MIX_SUB_TILES)
    _dn_kernel(qkv_scr, z_scr, ba_scr, alog_ref, dt_ref, dnw_ref, o_ref, s_ref,
               nb=1, chunk=UNIT, carry=True, units=MIX_ROWS // UNIT)


def _mix_call(x, mod3, norm_w, w_in_t, conv_w, alog_row, dt_row, dn_norm_w):
    bsz, seq, _ = x.shape
    nt = seq // MIX_ROWS
    n_tok = bsz * seq
    row_map = lambda i, j: (i * nt + j, 0)
    const = lambda i, j: (0, 0)
    per_seq = lambda i, j: (i, 0, 0)
    swa_cols = (SWA_Q, SWA_KV, SWA_KV)
    return pl.pallas_call(
        _mix_kernel,
        grid=(bsz, nt),
        in_specs=[pl.BlockSpec((1, MIX_ROWS, D_MODEL), lambda i, j: (i, j, 0)),
                  pl.BlockSpec((1, 6, D_MODEL), per_seq),
                  pl.BlockSpec((1, D_MODEL), const),
                  pl.BlockSpec(w_in_t.shape, const, pipeline_mode=RESIDENT),
                  pl.BlockSpec((DN_CONV, DN_CONV_CH), const),
                  pl.BlockSpec((1, BA_PAD), const),
                  pl.BlockSpec((1, BA_PAD), const),
                  pl.BlockSpec((1, DN_DV), const)],
        out_specs=[pl.BlockSpec((MIX_ROWS, n), row_map) for n in swa_cols]
        + [pl.BlockSpec((1, SUBLANES, DN_CONV_CH), per_seq),
           pl.BlockSpec((MIX_ROWS, DN_V), row_map),
           pl.BlockSpec((1, DN_HEADS, DN_DK, DN_DV), lambda i, j: (i, 0, 0, 0))],
        out_shape=[jax.ShapeDtypeStruct((n_tok, n), F32) for n in swa_cols]
        + [jax.ShapeDtypeStruct((bsz, SUBLANES, DN_CONV_CH), F32),
           jax.ShapeDtypeStruct((n_tok, DN_V), F32),
           jax.ShapeDtypeStruct((bsz, DN_HEADS, DN_DK, DN_DV), F32)],
        scratch_shapes=[pltpu.VMEM((MIX_SUB_TILES, SUB_ROWS, D_MODEL + ROW_PITCH_PAD), BF16),
                        pltpu.VMEM((D_MODEL, DN_CONV_CH + DN_V + ROW_PITCH_PAD), BF16),
                        pltpu.VMEM((D_MODEL, BA_PAD), BF16),
                        pltpu.VMEM((D_MODEL, SWA_Q + 2 * SWA_KV), BF16),
                        pltpu.VMEM((MIX_ROWS, DN_CONV_CH), F32),
                        pltpu.VMEM((MIX_ROWS, DN_V + ROW_PITCH_PAD), F32),
                        pltpu.VMEM((MIX_ROWS, BA_PAD), F32)],
        compiler_params=_cparams(("arbitrary", "arbitrary")),
        name="in_proj_deltanet_carry",
    )(x, mod3, norm_w, w_in_t, conv_w, alog_row, dt_row, dn_norm_w)


def _bucket_table():
    i = np.arange(WINDOW, dtype=np.int64)[:, None]
    j = np.arange(2 * WINDOW, dtype=np.int64)[None, :]
    d = np.maximum(i + WINDOW - j, 0)
    exact = N_BUCKETS // 2
    logv = (np.log(np.maximum(d, 1).astype(np.float32) / np.float32(exact)).astype(np.float32)
            / np.float32(math.log(MAX_DISTANCE / exact)))
    large = np.minimum(exact + (logv * np.float32(N_BUCKETS - exact)).astype(np.int32), N_BUCKETS - 1)
    return np.where(d < exact, d, large).astype(np.int32)


def _sample_bucket_table(bucket):
    out = bucket[:SUBLANES].copy()
    out[:, 2 * WINDOW - SUBLANES:] = bucket[:SUBLANES, WINDOW:WINDOW + SUBLANES]
    return out


def _bias_lookup(rb_ref, bucket, head):
    acc = jnp.zeros(bucket.shape, F32)
    for b in range(N_BUCKETS):
        acc = jnp.where(bucket == b, rb_ref[b, head], acc)
    return acc


def _bias_kernel(rb_ref, bucket_t_ref, bucket_ref, ot_ref, os_ref):
    bucket_t = bucket_t_ref[...]
    kj = lax.broadcasted_iota(jnp.int32, bucket_t.shape, 0)
    qi = lax.broadcasted_iota(jnp.int32, bucket_t.shape, 1)
    dist = qi + WINDOW - kj
    valid = (dist >= 0) & (dist < WINDOW)
    for hk in range(SWA_KV_HEADS):
        for par in range(2):
            for st in range(SWA_GROUP // 2):
                head = hk * SWA_GROUP + 2 * st + par
                gen = jnp.where(valid, _bias_lookup(rb_ref, bucket_t, head) * LOG2_E, NEG_INF)
                ot_ref[1, hk, par, :, st * WINDOW:(st + 1) * WINDOW] = gen
                ot_ref[0, hk, par, :, st * WINDOW:(st + 1) * WINDOW] = jnp.where(kj >= WINDOW, gen, NEG_INF)
    bucket = bucket_ref[...]
    qi = lax.broadcasted_iota(jnp.int32, bucket.shape, 0)
    col = lax.broadcasted_iota(jnp.int32, bucket.shape, 1)
    kj = jnp.where(col < WINDOW, col, col - (WINDOW - SUBLANES))
    dist = qi + WINDOW - kj
    valid = (dist >= 0) & (dist < WINDOW) & ((col < WINDOW) | (col >= 2 * WINDOW - SUBLANES))
    for head in range(SWA_HEADS):
        os_ref[head] = jnp.where(valid, _bias_lookup(rb_ref, bucket, head), NEG_INF)


BIAS_T_SHAPE = (2, SWA_KV_HEADS, 2, 2 * WINDOW, (SWA_GROUP // 2) * WINDOW)
BIAS_S_SHAPE = (SWA_HEADS, SUBLANES, 2 * WINDOW)


def _bias_call(rel_bias):
    bucket = _bucket_table()
    return pl.pallas_call(
        _bias_kernel,
        in_specs=[pl.BlockSpec(memory_space=pltpu.SMEM),
                  pl.BlockSpec((2 * WINDOW, WINDOW), lambda: (0, 0)),
                  pl.BlockSpec((SUBLANES, 2 * WINDOW), lambda: (0, 0))],
        out_specs=[pl.BlockSpec(BIAS_T_SHAPE, lambda: (0,) * len(BIAS_T_SHAPE)),
                   pl.BlockSpec(BIAS_S_SHAPE, lambda: (0,) * len(BIAS_S_SHAPE))],
        out_shape=[jax.ShapeDtypeStruct(BIAS_T_SHAPE, F32), jax.ShapeDtypeStruct(BIAS_S_SHAPE, F32)],
        name="swa_rel_bias_table",
    )(rel_bias, jnp.asarray(np.ascontiguousarray(bucket.T)), jnp.asarray(_sample_bucket_table(bucket)))


def _softmax_sink_parts(s, sink):
    m = jnp.maximum(jnp.max(s, axis=0, keepdims=True), sink)
    p = jnp.exp2(s - m)
    return p, 1.0 / (jnp.sum(p, axis=0, keepdims=True) + jnp.exp2(sink - m))


def _half_lane_variants(full, hk, lo_half):
    rolled = pltpu.roll(full, SWA_HD, axis=1)
    low_src, high_src = (full, rolled) if hk == 0 else (rolled, full)
    return jnp.where(lo_half, low_src, 0.0), jnp.where(lo_half, 0.0, high_src)


def _swa_prompt_kernel(sink_ref, q_ref, kp_ref, kc_ref, vp_ref, vc_ref, bias_ref, o_ref, *, n_blk):
    step = pl.program_id(1)
    lo_half = lax.broadcasted_iota(jnp.int32, (1, 2 * SWA_HD), 1) < SWA_HD
    lo_rows = lax.broadcasted_iota(jnp.int32, (2 * SWA_HD, 1), 0) < SWA_HD
    q = _bf(q_ref[...] * (SWA_HD ** -0.5 * LOG2_E))
    keys = jnp.concatenate([kp_ref[...], kc_ref[...]], axis=0)
    vals = jnp.concatenate([vp_ref[...], vc_ref[...]], axis=0)
    k_var = [[_bf(t) for t in _half_lane_variants(keys, hk, lo_half)] for hk in range(SWA_KV_HEADS)]
    v_var_t = [[_bf(t.T) for t in _half_lane_variants(vals, hk, lo_half)] for hk in range(SWA_KV_HEADS)]
    n_stack = SWA_GROUP // 2
    sinks = [[jnp.concatenate([jnp.full((1, WINDOW), sink_ref[hk * SWA_GROUP + 2 * st + par] * LOG2_E, F32)
                               for st in range(n_stack)], axis=1) for par in range(2)]
             for hk in range(SWA_KV_HEADS)]
    nt_dims = (((1,), (1,)), ((), ()))

    def scores(b):
        rows = slice(b * WINDOW, (b + 1) * WINDOW)
        win = slice(b * WINDOW, (b + 2) * WINDOW)
        variant = jnp.where(step == 0, 0, 1) if b == 0 else 1
        out = []
        for hk in range(SWA_KV_HEADS):
            q2 = jnp.concatenate([q[rows, (hk * n_stack + st) * 2 * SWA_HD:(hk * n_stack + st + 1) * 2 * SWA_HD]
                                  for st in range(n_stack)], axis=0)
            out.append([lax.dot_general(k_var[hk][par][win], q2, nt_dims, preferred_element_type=F32)
                        + bias_ref[variant, hk, par] for par in range(2)])
        return out

    s_next = scores(0)
    for b in range(n_blk):
        s_cur = s_next
        if b + 1 < n_blk:
            s_next = scores(b + 1)
        rows = slice(b * WINDOW, (b + 1) * WINDOW)
        win = slice(b * WINDOW, (b + 2) * WINDOW)
        for hk in range(SWA_KV_HEADS):
            parts = [_softmax_sink_parts(s_cur[hk][par], sinks[hk][par]) for par in range(2)]
            o_t = (jnp.dot(v_var_t[hk][0][:, win], _bf(parts[0][0]), preferred_element_type=F32)
                   + jnp.dot(v_var_t[hk][1][:, win], _bf(parts[1][0]), preferred_element_type=F32))
            o_t = o_t * jnp.where(lo_rows, parts[0][1], parts[1][1])
            for st in range(n_stack):
                lo = (hk * n_stack + st) * 2 * SWA_HD
                o_ref[rows, lo:lo + 2 * SWA_HD] = o_t[:, st * WINDOW:(st + 1) * WINDOW].T


def _swa_prompt_call(sinks, sq, sk, sv, bias, bsz, seq):
    n_blk = SWA_BLOCKS_PER_STEP
    tile = n_blk * WINDOW
    nt = seq // tile
    cur = lambda b, i: (b * nt + i, 0)
    prv = lambda b, i: (b * nt * n_blk + jnp.maximum(i * n_blk - 1, 0), 0)
    return pl.pallas_call(
        functools.partial(_swa_prompt_kernel, n_blk=n_blk),
        grid=(bsz, nt),
        in_specs=[pl.BlockSpec(memory_space=pltpu.SMEM),
                  pl.BlockSpec((tile, SWA_Q), cur),
                  pl.BlockSpec((WINDOW, SWA_KV), prv),
                  pl.BlockSpec((tile, SWA_KV), cur),
                  pl.BlockSpec((WINDOW, SWA_KV), prv),
                  pl.BlockSpec((tile, SWA_KV), cur),
                  pl.BlockSpec(BIAS_T_SHAPE, lambda b, i: (0,) * len(BIAS_T_SHAPE))],
        out_specs=pl.BlockSpec((tile, SWA_Q), cur),
        out_shape=jax.ShapeDtypeStruct((bsz * seq, SWA_Q), F32),
        compiler_params=_cparams(("arbitrary", "arbitrary")),
        name="swa_banded",
    )(sinks, sq, sk, sk, sv, sv, bias)


def _swa_sample_kernel(sink_ref, q_ref, kn_ref, vn_ref, kc_ref, vc_ref, bias_ref, o_ref, ko_ref, vo_ref, *, nb, lt):
    lane = lax.broadcasted_iota(jnp.int32, (1, 2 * SWA_HD), 1)
    lo_half = lane < SWA_HD
    new_lanes = lane >= WINDOW - lt
    q_all = q_ref[...] * (SWA_HD ** -0.5)
    kn_all_t = kn_ref[...].T
    vn_all_t = vn_ref[...].T
    bias = jnp.concatenate([bias_ref[h, 0:lt, :] for h in range(SWA_HEADS)], axis=0)
    sink = jnp.concatenate([jnp.full((lt, 1), sink_ref[h], F32) for h in range(SWA_HEADS)], axis=0)
    nt_dims = (((1,), (1,)), ((), ()))

    def to_half(tile, src_half, dst_half):
        return tile if src_half == dst_half else pltpu.roll(tile, SWA_HD, axis=1)

    lhs, keys, vals = [], [], []
    for s in range(nb):
        rows = slice(s * lt, (s + 1) * lt)
        to_end = (WINDOW - lt - s * lt) % WINDOW
        for c_ref, n_all_t, out_ref, acc in ((kc_ref, kn_all_t, ko_ref, keys), (vc_ref, vn_all_t, vo_ref, vals)):
            cached = c_ref[s]
            new = n_all_t if to_end == 0 else pltpu.roll(n_all_t, to_end, axis=1)
            out_ref[s] = jnp.where(new_lanes, new, pltpu.roll(cached, WINDOW - lt, axis=1))
            acc.append(_bf(jnp.concatenate([cached, jnp.where(new_lanes, new, 0.0)], axis=1)))
        q = q_all[rows]
        tiles = []
        for h in range(SWA_HEADS):
            hk = h // SWA_GROUP
            t = to_half(q[:, (h // 2) * 2 * SWA_HD:(h // 2 + 1) * 2 * SWA_HD], h % 2, hk)
            tiles.append(jnp.where(lo_half if hk == 0 else jnp.logical_not(lo_half), t, 0.0))
        lhs.append(_bf(jnp.concatenate(tiles, axis=0)))
    scores = [jnp.dot(lhs[s], keys[s], preferred_element_type=F32) + bias for s in range(nb)]
    m = [jnp.maximum(jnp.max(sc, axis=-1, keepdims=True), sink) for sc in scores]
    p = [jnp.exp(sc - mm) for sc, mm in zip(scores, m)]
    rinv = [1.0 / (jnp.sum(pp, axis=-1, keepdims=True) + jnp.exp(sink - mm)) for pp, mm in zip(p, m)]
    res = [lax.dot_general(_bf(p[s]), vals[s], nt_dims, preferred_element_type=F32) * rinv[s] for s in range(nb)]
    rows_out = []
    for s in range(nb):
        tiles = []
        for pair in range(SWA_HEADS // 2):
            hk = (2 * pair) // SWA_GROUP
            low = to_half(res[s][(2 * pair) * lt:(2 * pair + 1) * lt], hk, 0)
            high = to_half(res[s][(2 * pair + 1) * lt:(2 * pair + 2) * lt], hk, 1)
            tiles.append(jnp.where(lo_half, low, high))
        rows_out.append(jnp.concatenate(tiles, axis=1))
    o_ref[...] = jnp.concatenate(rows_out, axis=0)


def _swa_sample_call(sinks, sq, sk, sv, cache_k, cache_v, bias, bsz, seq):
    assert seq == SUBLANES, "the sample bias table places one sublane tile of new keys"
    nb = UNIT // seq
    rows = lambda i: (i, 0)
    seqs = lambda i: (i, 0, 0)
    return pl.pallas_call(
        functools.partial(_swa_sample_kernel, nb=nb, lt=seq),
        grid=(bsz // nb,),
        in_specs=[pl.BlockSpec(memory_space=pltpu.SMEM),
                  pl.BlockSpec((UNIT, SWA_Q), rows),
                  pl.BlockSpec((UNIT, SWA_KV), rows),
                  pl.BlockSpec((UNIT, SWA_KV), rows),
                  pl.BlockSpec((nb, SWA_KV, WINDOW), seqs),
                  pl.BlockSpec((nb, SWA_KV, WINDOW), seqs),
                  pl.BlockSpec(BIAS_S_SHAPE, lambda i: (0,) * len(BIAS_S_SHAPE))],
        out_specs=[pl.BlockSpec((UNIT, SWA_Q), rows),
                   pl.BlockSpec((nb, SWA_KV, WINDOW), seqs),
                   pl.BlockSpec((nb, SWA_KV, WINDOW), seqs)],
        out_shape=[jax.ShapeDtypeStruct((bsz * seq, SWA_Q), F32),
                   jax.ShapeDtypeStruct((bsz, SWA_KV, WINDOW), F32),
                   jax.ShapeDtypeStruct((bsz, SWA_KV, WINDOW), F32)],
        compiler_params=_cparams(("arbitrary",)),
        name="swa_cached",
    )(sinks, sq, sk, sv, cache_k, cache_v, bias)


def _post_kernel(*refs, nb, lt, carry, n_sub):
    if carry:
        (odn_ref, oswa_ref, x_ref, mod_ref, wout_ref, nfw_ref, wup_ref, cw_ref, cb_ref, wdn_ref, fnw_ref,
         y_ref, fbuf_ref, h_scr, x1_scr, acc_scr) = refs
        prev_ref = fbuf_ref

        @pl.when(pl.program_id(1) == 0)
        def _():
            fbuf_ref[...] = jnp.zeros_like(fbuf_ref)
    else:
        (odn_ref, oswa_ref, x_ref, mod_ref, wout_ref, nfw_ref, wup_ref, cw_ref, cb_ref, wdn_ref, fnw_ref,
         prev_ref, y_ref, fbuf_ref, h_scr, x1_scr, acc_scr) = refs
    rows = nb * lt
    n_chunks = D_FF // FFN_CHUNK

    def col_slices(c):
        return [slice(base + c * FFN_CHUNK, base + (c + 1) * FFN_CHUNK) for base in (0, D_FF)]

    def prologue(sub):
        rr = slice(sub * rows, (sub + 1) * rows)
        attn = (jnp.dot(_bf(odn_ref[rr, :]), wout_ref[0:DN_V, :], preferred_element_type=F32)
                + jnp.dot(_bf(oswa_ref[rr, :]), wout_ref[DN_V:, :], preferred_element_type=F32))
        x = x_ref[:, sub * lt:(sub + 1) * lt, :].reshape(rows, D_MODEL)
        x1 = x + _rows(mod_ref[:, 2:3, :], nb, lt) * attn
        x1_scr[sub] = x1
        h = (_rms(x1, nfw_ref[...]) * (1.0 + _rows(mod_ref[:, 4:5, :], nb, lt))
             + _rows(mod_ref[:, 3:4, :], nb, lt))
        h_scr[sub, :, :D_MODEL] = _bf(h)
        acc_scr[sub] = jnp.zeros((rows, D_MODEL), F32)

    def up_proj(sub, c):
        return [jnp.dot(h_scr[sub, :, :D_MODEL], wup_ref[:, cols], preferred_element_type=F32)
                for cols in col_slices(c)]

    def chunk(sub, c, u_cur):
        halves = []
        for u, cols in zip(u_cur, col_slices(c)):
            prev = prev_ref[:, :, cols]
            halves.append(_causal_conv(u, prev, cw_ref[:, cols], nb, lt) + cb_ref[:, cols])
            fbuf_ref[:, :, cols] = _last_rows(u, nb, lt, FFN_CONV - 1)
        act = _silu(halves[0]) * halves[1]
        acc_scr[sub] += jnp.dot(_bf(act), wdn_ref[c * FFN_CHUNK:(c + 1) * FFN_CHUNK, :],
                                preferred_element_type=F32)

    def epilogue(sub):
        x2 = x1_scr[sub] + _rows(mod_ref[:, 5:6, :], nb, lt) * acc_scr[sub]
        y_ref[:, sub * lt:(sub + 1) * lt, :] = _rms(x2, fnw_ref[...]).reshape(nb, lt, D_MODEL)

    prologue(0)
    for sub in range(n_sub):
        pending = [up_proj(sub, c) for c in range(min(UP_PROJ_AHEAD, n_chunks))]
        for c in range(n_chunks):
            u_cur = pending.pop(0)
            if c + UP_PROJ_AHEAD < n_chunks:
                pending.append(up_proj(sub, c + UP_PROJ_AHEAD))
            chunk(sub, c, u_cur)
            if sub + 1 < n_sub and c == n_chunks - 1 - POST_LOOKAHEAD_CHUNKS:
                prologue(sub + 1)
        epilogue(sub)


def _post_call(o_dn, o_swa, x, mod3, w_out, norm_ffn_w, w_up, conv_w, conv_b, w_down, final_w, nb, lt, state=None):
    carry = state is None
    n_sub = POST_SUB_TILES if carry else 1
    bsz, seq, _ = x.shape
    nt = seq // (lt * n_sub)
    rows = nb * lt
    tile = rows * n_sub
    n_state = _state_rows(nb, lt, FFN_CONV - 1)
    row_map = lambda i, j: (i * nt + j, 0)
    const = lambda i, j: (0, 0)
    in_specs = [pl.BlockSpec((tile, DN_V), row_map),
                pl.BlockSpec((tile, SWA_Q), row_map),
                pl.BlockSpec((nb, lt * n_sub, D_MODEL), lambda i, j: (i, j, 0)),
                pl.BlockSpec((nb, 6, D_MODEL), lambda i, j: (i, 0, 0)),
                pl.BlockSpec((D_MODEL, D_MODEL), const, pipeline_mode=RESIDENT),
                pl.BlockSpec((1, D_MODEL), const),
                pl.BlockSpec(w_up.shape, const, pipeline_mode=RESIDENT),
                pl.BlockSpec((FFN_CONV, 2 * D_FF), const),
                pl.BlockSpec((1, 2 * D_FF), const),
                pl.BlockSpec((D_FF, D_MODEL), const, pipeline_mode=RESIDENT),
                pl.BlockSpec((1, D_MODEL), const)]
    args = [o_dn, o_swa, x, mod3, w_out, norm_ffn_w, w_up, conv_w, conv_b, w_down, final_w]
    if not carry:
        in_specs.append(pl.BlockSpec((nb, FFN_CONV - 1, 2 * D_FF), lambda i, j: (i, 0, 0)))
        args.append(state)
    return pl.pallas_call(
        functools.partial(_post_kernel, nb=nb, lt=lt, carry=carry, n_sub=n_sub),
        grid=(bsz // nb, nt),
        in_specs=in_specs,
        out_specs=[pl.BlockSpec((nb, lt * n_sub, D_MODEL), lambda i, j: (i, j, 0)),
                   pl.BlockSpec((nb, n_state, 2 * D_FF), lambda i, j: (i, 0, 0))],
        out_shape=[jax.ShapeDtypeStruct((bsz, seq, D_MODEL), F32),
                   jax.ShapeDtypeStruct((bsz, n_state, 2 * D_FF), F32)],
        scratch_shapes=[pltpu.VMEM((n_sub, rows, D_MODEL + ROW_PITCH_PAD), BF16),
                        pltpu.VMEM((n_sub, rows, D_MODEL), F32),
                        pltpu.VMEM((n_sub, rows, D_MODEL), F32)],
        compiler_params=_cparams(("arbitrary", "arbitrary")),
        name="out_proj_convffn_carry" if carry else "out_proj_convffn_state",
    )(*args)


def _pad_row(vec, offset):
    return jnp.zeros((1, BA_PAD), F32).at[0, offset:offset + vec.shape[0]].set(vec)


def kernel(x_prompt, x_sample, state_dn_conv, state_dn_ssm, cache_swa_k, cache_swa_v, state_ffn_conv, c_prompt, c_sample, rel_bias, final_norm_w, w_ada, b_ada, norm_mix_w, w_in, dn_conv_w, dn_A_log, dn_dt_bias, dn_norm_w, swa_sinks, w_out, norm_ffn_w, ffn_w_up, ffn_conv_w, ffn_conv_b, ffn_w_down):
    bp, lp, _ = x_prompt.shape
    bs, ls, _ = x_sample.shape
    layer = 0

    w_in_t = jnp.transpose(w_in[layer])
    w_out_b = w_out[layer].astype(BF16)
    w_up_b = ffn_w_up[layer].astype(BF16)
    w_dn_b = ffn_w_down[layer].astype(BF16)
    alog_row = _pad_row(dn_A_log[layer], DN_HEADS)
    dt_row = _pad_row(dn_dt_bias[layer], DN_HEADS)
    row = lambda v: v.reshape(1, -1)

    n_c = bp + bs
    n_c_pad = -(-n_c // SUBLANES) * SUBLANES
    c_all = jnp.pad(jnp.concatenate([c_sample, c_prompt], axis=0), ((0, n_c_pad - n_c), (0, 0)))
    mod_s, mod_p = _mod_call(c_all, w_ada[layer], row(b_ada[layer]), bs, bp)

    bias_t, bias_s = _bias_call(rel_bias)
    sinks = swa_sinks[layer]

    def post(o_dn, o_swa, x, mod3, nb, lt, state=None):
        return _post_call(o_dn, o_swa, x, mod3, w_out_b, row(norm_ffn_w[layer]), w_up_b, ffn_conv_w[layer],
                          row(ffn_conv_b[layer]), w_dn_b, row(final_norm_w), nb, lt, state)

    dn_args = (alog_row, dt_row, row(dn_norm_w[layer]))

    sq, sk, sv, p_tail, o_dn, p_ssm = _mix_call(x_prompt, mod_p, row(norm_mix_w[layer]), w_in_t,
                                                dn_conv_w[layer], *dn_args)
    o_swa = _swa_prompt_call(sinks, sq, sk, sv, bias_t, bp, lp)
    y_prompt, p_ffn_tail = post(o_dn, o_swa, x_prompt, mod_p, 1, SUB_ROWS)
    p_dn_conv = p_tail[:, SUBLANES - (DN_CONV - 1):]
    p_fbuf = p_ffn_tail[:, SUBLANES - (FFN_CONV - 1):]
    last_window = lambda t: t.reshape(bp, lp, SWA_KV)[:, lp - WINDOW:].reshape(bp, WINDOW, SWA_KV_HEADS, SWA_HD)
    p_swa_k = last_window(sk)
    p_swa_v = last_window(sv)

    cache_t = lambda c: jnp.transpose(c[layer].reshape(bs, WINDOW, SWA_KV), (0, 2, 1))
    cache_from_t = lambda c: jnp.transpose(c, (0, 2, 1)).reshape(bs, WINDOW, SWA_KV_HEADS, SWA_HD)
    qkv_s, z_s, ba_s, sq_s, sk_s, sv_s, s_dn_conv = _in_call(
        x_sample, mod_s, row(norm_mix_w[layer]), w_in_t, dn_conv_w[layer], SAMPLE_IN_SEQS, state_dn_conv[layer])
    o_dn_s, s_ssm = _dn_call(qkv_s, z_s, ba_s, *dn_args, bs, ls, state=state_dn_ssm[layer])
    o_swa_s, s_k, s_v = _swa_sample_call(sinks, sq_s, sk_s, sv_s,
                                         cache_t(cache_swa_k), cache_t(cache_swa_v), bias_s, bs, ls)
    y_sample, s_fbuf = post(o_dn_s, o_swa_s, x_sample, mod_s, SAMPLE_POST_SEQS, ls, state=state_ffn_conv[layer])

    return (y_prompt, y_sample, p_dn_conv[None], s_dn_conv[None], p_ssm[None], s_ssm[None],
            p_swa_k[None], cache_from_t(s_k)[None],
            p_swa_v[None], cache_from_t(s_v)[None],
            p_fbuf[None], s_fbuf[None])
```

```python
import functools
import math

import numpy as np
import jax
import jax.numpy as jnp
from jax import lax
from jax.experimental import pallas as pl
from jax.experimental.pallas import tpu as pltpu

F32 = jnp.float32
BF16 = jnp.bfloat16

D_MODEL = 1024
DN_HEADS = 4
DN_DK = 128
DN_DV = 128
DN_CONV = 4
SWA_HEADS = 8
SWA_KV_HEADS = 2
SWA_GROUP = SWA_HEADS // SWA_KV_HEADS
SWA_HD = 64
WINDOW = 128
N_BUCKETS = 32
MAX_DISTANCE = 128
D_FF = 2816
FFN_CONV = 3
EPS = 1e-6
NEG_INF = -1e30
LOG2_E = 1.0 / math.log(2.0)

DN_QK = DN_HEADS * DN_DK
DN_V = DN_HEADS * DN_DV
DN_CONV_CH = 2 * DN_QK + DN_V
SWA_Q = SWA_HEADS * SWA_HD
SWA_KV = SWA_KV_HEADS * SWA_HD
BA_PAD = 128
SUBLANES = 8
UNIT = 128
POST_SUB_TILES = 2
POST_LOOKAHEAD_CHUNKS = 3
UP_PROJ_AHEAD = 2
FFN_CHUNK = 256
ROW_PITCH_PAD = 128
VMEM_LIMIT = 56 * 1024 * 1024
RESIDENT = pl.Buffered(1)


def _cparams(sem):
    return pltpu.CompilerParams(dimension_semantics=sem, vmem_limit_bytes=VMEM_LIMIT)


def _bf(x):
    return x.astype(BF16)


def _dot(a, b):
    return jnp.dot(_bf(a), _bf(b), preferred_element_type=F32)


def _dot_nt(a, b):
    return lax.dot_general(_bf(a), _bf(b), (((1,), (1,)), ((), ())), preferred_element_type=F32)


def _dot_mask(m, x):
    hi = _bf(x)
    r = x - hi.astype(F32)
    mid = _bf(r)
    lo = _bf(r - mid.astype(F32))
    d = functools.partial(jnp.dot, preferred_element_type=F32)
    return d(m, hi) + (d(m, mid) + d(m, lo))


def _sigmoid(x):
    return 1.0 / (1.0 + jnp.exp(-x))


def _silu(x):
    return x * _sigmoid(x)


def _softplus(x):
    return jnp.maximum(x, 0.0) + jnp.log1p(jnp.exp(-jnp.abs(x)))


def _rms(x, w):
    ms = jnp.mean(x * x, axis=-1, keepdims=True)
    return x * lax.rsqrt(ms + EPS) * w


def _l2norm(t):
    return t * lax.rsqrt(jnp.sum(t * t, axis=-1, keepdims=True) + EPS)


def _rows(m3, nb, lt):
    return jnp.broadcast_to(m3, (nb, lt, m3.shape[-1])).reshape(nb * lt, m3.shape[-1])


def _causal_conv(x, prev, w, nb, lt):
    width = w.shape[0]
    rows, ch = x.shape
    if nb == 1 and lt > SUBLANES:
        tiles = jnp.concatenate([prev, x.reshape(lt // SUBLANES, SUBLANES, ch)], axis=0)
        sub = lax.broadcasted_iota(jnp.int32, (1, SUBLANES, 1), 1)
        out = tiles[1:] * w[width - 1:width, :]
        for j in range(1, width):
            rot = pltpu.roll(tiles, j, axis=1)
            out = out + jnp.where(sub >= j, rot[1:], rot[:-1]) * w[width - 1 - j:width - j, :]
        return out.reshape(rows, ch)
    assert lt == SUBLANES
    x3 = x.reshape(nb, lt, ch)
    st = jnp.concatenate([jnp.zeros((nb, lt - (width - 1), ch), F32), prev], axis=1)
    sub = lax.broadcasted_iota(jnp.int32, (1, SUBLANES, 1), 1)
    out = x3 * w[width - 1:width, :]
    for j in range(1, width):
        sh = jnp.where(sub >= j, pltpu.roll(x3, j, axis=1), pltpu.roll(st, j, axis=1))
        out = out + sh * w[width - 1 - j:width - j, :]
    return out.reshape(rows, ch)


def _last_rows(x, nb, lt, n_state):
    ch = x.shape[-1]
    if nb == 1 and lt > SUBLANES:
        return x[lt - SUBLANES:].reshape(1, SUBLANES, ch)
    return x.reshape(nb, lt, ch)[:, lt - n_state:, :]


def _state_rows(nb, lt, n_state):
    return SUBLANES if (nb == 1 and lt > SUBLANES) else n_state


MOD_PARTS = 6


def _mod_kernel(c_ref, w_ref, b_ref, os_ref, op_ref, *, n_s, n_p):
    part = pl.program_id(0)
    res = _dot(_silu(c_ref[...]), w_ref[...]) + b_ref[...]
    for k in range(MOD_PARTS):
        @pl.when(part == k)
        def _():
            os_ref[:, k, :] = res[:n_s]
            op_ref[:, k, :] = res[n_s:n_s + n_p]


def _mod_call(c_all, w_ada, b_ada, n_s, n_p):
    rows = c_all.shape[0]
    return pl.pallas_call(
        functools.partial(_mod_kernel, n_s=n_s, n_p=n_p),
        grid=(MOD_PARTS,),
        in_specs=[pl.BlockSpec((rows, D_MODEL), lambda k: (0, 0)),
                  pl.BlockSpec((D_MODEL, D_MODEL), lambda k: (0, k)),
                  pl.BlockSpec((1, D_MODEL), lambda k: (0, k))],
        out_specs=[pl.BlockSpec((n_s, MOD_PARTS, D_MODEL), lambda k: (0, 0, 0)),
                   pl.BlockSpec((n_p, MOD_PARTS, D_MODEL), lambda k: (0, 0, 0))],
        out_shape=[jax.ShapeDtypeStruct((n_s, MOD_PARTS, D_MODEL), F32),
                   jax.ShapeDtypeStruct((n_p, MOD_PARTS, D_MODEL), F32)],
        compiler_params=_cparams(("arbitrary",)),
        name="adaln_mod",
    )(c_all, w_ada, b_ada)


IN_SPLIT = (DN_CONV_CH, DN_V, BA_PAD, SWA_Q, SWA_KV, SWA_KV)


IN_LOOKAHEAD_SLABS = 1
W_PREP_ROWS = 256
IN_SLAB = 2 * DN_DK


def _in_kernel(*refs, nb, lt, carry, n_sub):
    if carry:
        (x_ref, mod_ref, nw_ref, w_ref, cw_ref,
         qkv_ref, z_ref, ba_ref, sq_ref, sk_ref, sv_ref, tail_ref, h_scr, wdn_ref, wba_ref, wsw_ref) = refs
        prev_ref = tail_ref

        @pl.when(pl.program_id(1) == 0)
        def _():
            tail_ref[...] = jnp.zeros_like(tail_ref)
    else:
        (x_ref, mod_ref, nw_ref, w_ref, cw_ref, prev_ref,
         qkv_ref, z_ref, ba_ref, sq_ref, sk_ref, sv_ref, tail_ref, h_scr, wdn_ref, wba_ref, wsw_ref) = refs

    @pl.when((pl.program_id(0) == 0) & (pl.program_id(1) == 0))
    def _():
        ba_lo = DN_CONV_CH + DN_V
        n_ba = 2 * DN_HEADS
        for r in range(0, ba_lo, W_PREP_ROWS):
            wdn_ref[:, r:r + W_PREP_ROWS] = _bf(w_ref[r:r + W_PREP_ROWS, :].T)
        row = lax.broadcasted_iota(jnp.int32, (BA_PAD, 1), 0)
        wba_ref[...] = _bf(jnp.where(row < n_ba, w_ref[ba_lo:ba_lo + BA_PAD, :], 0.0).T)
        for r in range(0, SWA_Q + 2 * SWA_KV, W_PREP_ROWS):
            wsw_ref[:, r:r + W_PREP_ROWS] = _bf(w_ref[ba_lo + n_ba + r:ba_lo + n_ba + r + W_PREP_ROWS, :].T)

    rows = nb * lt

    def norm(sub):
        x = x_ref[:, sub * lt:(sub + 1) * lt, :]
        ms = jnp.mean(x * x, axis=-1, keepdims=True)
        y = x * lax.rsqrt(ms + EPS) * nw_ref[...]
        h = y * (1.0 + mod_ref[:, 1:2, :]) + mod_ref[:, 0:1, :]
        h_scr[sub] = _bf(h.reshape(rows, D_MODEL))

    def proj(sub, w_bf_ref, lo, n):
        return jnp.dot(h_scr[sub], w_bf_ref[:, lo:lo + n], preferred_element_type=F32)

    def plain_proj(sub, i):
        rr = slice(sub * rows, (sub + 1) * rows)
        if i < 2:
            z_ref[rr, i * IN_SLAB:(i + 1) * IN_SLAB] = proj(sub, wdn_ref, DN_CONV_CH + i * IN_SLAB, IN_SLAB)
        elif i < 4:
            sq_ref[rr, (i - 2) * IN_SLAB:(i - 1) * IN_SLAB] = proj(sub, wsw_ref, (i - 2) * IN_SLAB, IN_SLAB)
        elif i == 4:
            kv = proj(sub, wsw_ref, SWA_Q, 2 * SWA_KV)
            sk_ref[rr, :] = kv[:, :SWA_KV]
            sv_ref[rr, :] = kv[:, SWA_KV:]
        else:
            ba_ref[rr, :] = proj(sub, wba_ref, 0, BA_PAD)

    def conv_slab(sub, slab, raw):
        rr = slice(sub * rows, (sub + 1) * rows)
        cols = slice(slab * IN_SLAB, (slab + 1) * IN_SLAB)
        c = _silu(_causal_conv(raw, prev_ref[:, :, cols], cw_ref[:, cols], nb, lt))
        tail_ref[:, :, cols] = _last_rows(raw, nb, lt, DN_CONV - 1)
        if slab * IN_SLAB < 2 * DN_QK:
            scale = DN_DK ** -0.5 if slab * IN_SLAB < DN_QK else 1.0
            c = jnp.concatenate([_l2norm(c[:, i * DN_DK:(i + 1) * DN_DK]) * scale
                                 for i in range(IN_SLAB // DN_DK)], axis=1)
        qkv_ref[rr, cols] = c

    n_slabs = DN_CONV_CH // IN_SLAB
    norm(0)
    for sub in range(n_sub):
        raw_next = proj(sub, wdn_ref, 0, IN_SLAB)
        for slab in range(n_slabs):
            raw = raw_next
            plain_proj(sub, slab)
            if slab + 1 < n_slabs:
                raw_next = proj(sub, wdn_ref, (slab + 1) * IN_SLAB, IN_SLAB)
            if sub + 1 < n_sub and slab == n_slabs - 1 - IN_LOOKAHEAD_SLABS:
                norm(sub + 1)
            conv_slab(sub, slab, raw)


def _in_call(x, mod3, norm_w, w_in_t, conv_w, nb, state):
    bsz, lt, _ = x.shape
    rows = nb * lt
    row_map = lambda i, j: (i, 0)
    const = lambda i, j: (0, 0)
    per_seq = lambda i, j: (i, 0, 0)
    return pl.pallas_call(
        functools.partial(_in_kernel, nb=nb, lt=lt, carry=False, n_sub=1),
        grid=(bsz // nb, 1),
        in_specs=[pl.BlockSpec((nb, lt, D_MODEL), per_seq),
                  pl.BlockSpec((nb, 6, D_MODEL), per_seq),
                  pl.BlockSpec((1, D_MODEL), const),
                  pl.BlockSpec(w_in_t.shape, const, pipeline_mode=RESIDENT),
                  pl.BlockSpec((DN_CONV, DN_CONV_CH), const),
                  pl.BlockSpec((nb, DN_CONV - 1, DN_CONV_CH), per_seq)],
        out_specs=[pl.BlockSpec((rows, n), row_map) for n in IN_SPLIT]
        + [pl.BlockSpec((nb, DN_CONV - 1, DN_CONV_CH), per_seq)],
        out_shape=[jax.ShapeDtypeStruct((bsz * lt, n), F32) for n in IN_SPLIT]
        + [jax.ShapeDtypeStruct((bsz, DN_CONV - 1, DN_CONV_CH), F32)],
        scratch_shapes=[pltpu.VMEM((1, rows, D_MODEL), BF16),
                        pltpu.VMEM((D_MODEL, DN_CONV_CH + DN_V + ROW_PITCH_PAD), BF16),
                        pltpu.VMEM((D_MODEL, BA_PAD), BF16),
                        pltpu.VMEM((D_MODEL, SWA_Q + 2 * SWA_KV), BF16)],
        compiler_params=_cparams(("arbitrary", "arbitrary")),
        name="norm_in_proj_state",
    )(x, mod3, norm_w, w_in_t, conv_w, state)


INV_BASE_SHIFT = 2
SWA_BLOCKS_PER_STEP = 16
DN_UNITS_STATE = 2


def _unit_lower_inverses(a_mats, ri, ci, chunk_shift):
    def blocks(s):
        return (ri >> s) == (ci >> s)

    base = min(INV_BASE_SHIFT, chunk_shift)
    xs = [jnp.where(blocks(base), -a, 0.0) for a in a_mats]
    ts = [jnp.where(ri == ci, 1.0, n) for n in xs]
    for lvl in range(1, base):
        xs = [_dot(x, x) for x in xs]
        ts = [t + _dot(t, x) for t, x in zip(ts, xs)]
    for s in range(base, chunk_shift):
        sel = blocks(s + 1) & jnp.logical_not(blocks(s))
        ets = [_dot(jnp.where(sel, a, 0.0), t) for a, t in zip(a_mats, ts)]
        ts = [t - _dot(t, et) for t, et in zip(ts, ets)]
    return ts


def _dn_kernel(*refs, nb, chunk, carry, units):
    if carry:
        (qkv_ref, z_ref, ba_ref, alog_ref, dt_ref, nw_ref, o_ref, s_ref) = refs
        s0_ref = s_ref

        @pl.when(pl.program_id(1) == 0)
        def _():
            s_ref[...] = jnp.zeros_like(s_ref)
    else:
        (qkv_ref, z_ref, ba_ref, alog_ref, dt_ref, nw_ref, s0_ref, o_ref, s_ref) = refs

    ri = lax.broadcasted_iota(jnp.int32, (UNIT, UNIT), 0)
    ci = lax.broadcasted_iota(jnp.int32, (UNIT, UNIT), 1)
    shift = int(math.log2(chunk))
    same = (ri >> shift) == (ci >> shift)
    incl = same & (ri >= ci)
    strict = same & (ri > ci)

    ba = ba_ref[...]
    beta_full = _sigmoid(ba)
    g_full = -jnp.exp(alog_ref[...]) * _softplus(ba + dt_ref[...])
    masks = jnp.concatenate([jnp.where(incl, 1.0, 0.0), jnp.where(same, 1.0, 0.0)], axis=0).astype(BF16)
    gsums = [_dot_mask(masks, g_full[u * UNIT:(u + 1) * UNIT]) for u in range(units)]
    g_cum = [g[:UNIT] for g in gsums]
    g_tot = [g[UNIT:] for g in gsums]
    g_cum_t = [g.T for g in g_cum]

    probs = [(u, h) for u in range(units) for h in range(DN_HEADS)]
    n_p = len(probs)
    rs = lambda u: slice(u * UNIT, (u + 1) * UNIT)

    def head_cols(base):
        return [qkv_ref[rs(u), base + h * DN_DK:base + (h + 1) * DN_DK] for u, h in probs]

    q = head_cols(0)
    k = head_cols(DN_QK)
    v = head_cols(2 * DN_QK)
    gc = [g_cum[u][:, DN_HEADS + h:DN_HEADS + h + 1] for u, h in probs]
    gr = [g_cum_t[u][DN_HEADS + h:DN_HEADS + h + 1, :] for u, h in probs]
    gt = [g_tot[u][:, DN_HEADS + h:DN_HEADS + h + 1] for u, h in probs]
    bc = [beta_full[rs(u), h:h + 1] for u, h in probs]
    decay = [jnp.where(incl, jnp.exp(jnp.where(incl, gc[p] - gr[p], 0.0)), 0.0) for p in range(n_p)]
    e_g = [jnp.exp(gc[p]) for p in range(n_p)]
    kq = [_dot_nt(jnp.concatenate([k[p], q[p]], axis=0), k[p]) for p in range(n_p)]
    qk = [kq[p][UNIT:] * decay[p] for p in range(n_p)]
    a_mats = [jnp.where(strict, bc[p] * kq[p][:UNIT] * decay[p], 0.0) for p in range(n_p)]
    t_inv = _unit_lower_inverses(a_mats, ri, ci, shift)
    wvk = [_dot(t_inv[p], jnp.concatenate([v[p] * bc[p], k[p] * (bc[p] * e_g[p])], axis=1)) for p in range(n_p)]
    w_v = [w[:, :DN_DV] for w in wvk]
    w_k = [w[:, DN_DV:] for w in wvk]
    q_dec = [q[p] * e_g[p] for p in range(n_p)]
    k_tail_t = [(k[p] * jnp.exp(gt[p] - gc[p])).T for p in range(n_p)]
    c_dec = [jnp.exp(gt[p]) for p in range(n_p)]

    outs = {}
    if carry:
        state = [s_ref[0, h] for h in range(DN_HEADS)]
        for u in range(units):
            ps = [u * DN_HEADS + h for h in range(DN_HEADS)]
            r = [_dot(jnp.concatenate([w_k[p], q_dec[p]], axis=0), state[h]) for h, p in enumerate(ps)]
            uu = [w_v[p] - r[h][:UNIT] for h, p in enumerate(ps)]
            for h, p in enumerate(ps):
                outs[p] = r[h][UNIT:] + _dot(qk[p], uu[h])
            state = [state[h] * c_dec[p][0:1, :] + _dot(k_tail_t[p], uu[h]) for h, p in enumerate(ps)]
        for h in range(DN_HEADS):
            s_ref[0, h] = state[h]
    else:
        per_unit = nb // units
        for p, (u, h) in enumerate(probs):
            us, qs = [], []
            for s in range(per_unit):
                lo = s * chunk
                lhs = jnp.concatenate([w_k[p][lo:lo + chunk], q_dec[p][lo:lo + chunk]], axis=0)
                r = jnp.dot(lhs, s0_ref[u * per_unit + s, h], preferred_element_type=F32)
                us.append(w_v[p][lo:lo + chunk] - r[:chunk])
                qs.append(r[chunk:])
            uu = jnp.concatenate(us, axis=0)
            outs[p] = jnp.concatenate(qs, axis=0) + _dot(qk[p], uu)
            for s in range(per_unit):
                lo = s * chunk
                upd = jnp.dot(k_tail_t[p][:, lo:lo + chunk], uu[lo:lo + chunk], preferred_element_type=F32)
                s_ref[u * per_unit + s, h] = s0_ref[u * per_unit + s, h] * c_dec[p][lo:lo + 1, :] + upd

    for p, (u, h) in enumerate(probs):
        zz = z_ref[rs(u), h * DN_DV:(h + 1) * DN_DV]
        o_ref[rs(u), h * DN_DV:(h + 1) * DN_DV] = _rms(outs[p], nw_ref[...]) * _silu(zz)


def _dn_call(qkv, z, ba, alog_row, dt_row, norm_w, bsz, seq, state):
    rows = DN_UNITS_STATE * UNIT
    nb = rows // seq
    row_map = lambda i, j: (i, 0)
    const = lambda i, j: (0, 0)
    per_seq = lambda i, j: (i, 0, 0, 0)
    return pl.pallas_call(
        functools.partial(_dn_kernel, nb=nb, chunk=seq, carry=False, units=DN_UNITS_STATE),
        grid=(bsz // nb, 1),
        in_specs=[pl.BlockSpec((rows, DN_CONV_CH), row_map),
                  pl.BlockSpec((rows, DN_V), row_map),
                  pl.BlockSpec((rows, BA_PAD), row_map),
                  pl.BlockSpec((1, BA_PAD), const),
                  pl.BlockSpec((1, BA_PAD), const),
                  pl.BlockSpec((1, DN_DV), const),
                  pl.BlockSpec((nb, DN_HEADS, DN_DK, DN_DV), per_seq)],
        out_specs=[pl.BlockSpec((rows, DN_V), row_map),
                   pl.BlockSpec((nb, DN_HEADS, DN_DK, DN_DV), per_seq)],
        out_shape=[jax.ShapeDtypeStruct((bsz * seq, DN_V), F32),
                   jax.ShapeDtypeStruct((bsz, DN_HEADS, DN_DK, DN_DV), F32)],
        compiler_params=_cparams(("arbitrary", "arbitrary")),
        name="gated_deltanet_state",
    )(qkv, z, ba, alog_row, dt_row, norm_w, state)


MIX_SUB_TILES = 2
SUB_ROWS = 2 * UNIT
MIX_ROWS = MIX_SUB_TILES * SUB_ROWS
SAMPLE_IN_SEQS = 64
SAMPLE_POST_SEQS = 32


def _mix_kernel(x_ref, mod_ref, nw_ref, w_ref, cw_ref, alog_ref, dt_ref, dnw_ref,
                sq_ref, sk_ref, sv_ref, tail_ref, o_ref, s_ref,
                h_scr, wdn_ref, wba_ref, wsw_ref, qkv_scr, z_scr, ba_scr):
    _in_kernel(x_ref, mod_ref, nw_ref, w_ref, cw_ref,
               qkv_scr, z_scr, ba_scr, sq_ref, sk_ref, sv_ref, tail_ref, h_scr, wdn_ref, wba_ref, wsw_ref,
               nb=1, lt=SUB_ROWS, carry=True, n_sub=MIX_SUB_TILES)
    _dn_kernel(qkv_scr, z_scr, ba_scr, alog_ref, dt_ref, dnw_ref, o_ref, s_ref,
               nb=1, chunk=UNIT, carry=True, units=MIX_ROWS // UNIT)


def _mix_call(x, mod3, norm_w, w_in_t, conv_w, alog_row, dt_row, dn_norm_w):
    bsz, seq, _ = x.shape
    nt = seq // MIX_ROWS
    n_tok = bsz * seq
    row_map = lambda i, j: (i * nt + j, 0)
    const = lambda i, j: (0, 0)
    per_seq = lambda i, j: (i, 0, 0)
    swa_cols = (SWA_Q, SWA_KV, SWA_KV)
    return pl.pallas_call(
        _mix_kernel,
        grid=(bsz, nt),
        in_specs=[pl.BlockSpec((1, MIX_ROWS, D_MODEL), lambda i, j: (i, j, 0)),
                  pl.BlockSpec((1, 6, D_MODEL), per_seq),
                  pl.BlockSpec((1, D_MODEL), const),
                  pl.BlockSpec(w_in_t.shape, const, pipeline_mode=RESIDENT),
                  pl.BlockSpec((DN_CONV, DN_CONV_CH), const),
                  pl.BlockSpec((1, BA_PAD), const),
                  pl.BlockSpec((1, BA_PAD), const),
                  pl.BlockSpec((1, DN_DV), const)],
        out_specs=[pl.BlockSpec((MIX_ROWS, n), row_map) for n in swa_cols]
        + [pl.BlockSpec((1, SUBLANES, DN_CONV_CH), per_seq),
           pl.BlockSpec((MIX_ROWS, DN_V), row_map),
           pl.BlockSpec((1, DN_HEADS, DN_DK, DN_DV), lambda i, j: (i, 0, 0, 0))],
        out_shape=[jax.ShapeDtypeStruct((n_tok, n), F32) for n in swa_cols]
        + [jax.ShapeDtypeStruct((bsz, SUBLANES, DN_CONV_CH), F32),
           jax.ShapeDtypeStruct((n_tok, DN_V), F32),
           jax.ShapeDtypeStruct((bsz, DN_HEADS, DN_DK, DN_DV), F32)],
        scratch_shapes=[pltpu.VMEM((MIX_SUB_TILES, SUB_ROWS, D_MODEL), BF16),
                        pltpu.VMEM((D_MODEL, DN_CONV_CH + DN_V + ROW_PITCH_PAD), BF16),
                        pltpu.VMEM((D_MODEL, BA_PAD), BF16),
                        pltpu.VMEM((D_MODEL, SWA_Q + 2 * SWA_KV), BF16),
                        pltpu.VMEM((MIX_ROWS, DN_CONV_CH), F32),
                        pltpu.VMEM((MIX_ROWS, DN_V + ROW_PITCH_PAD), F32),
                        pltpu.VMEM((MIX_ROWS, BA_PAD), F32)],
        compiler_params=_cparams(("arbitrary", "arbitrary")),
        name="in_proj_deltanet_carry",
    )(x, mod3, norm_w, w_in_t, conv_w, alog_row, dt_row, dn_norm_w)


def _bucket_table():
    i = np.arange(WINDOW, dtype=np.int64)[:, None]
    j = np.arange(2 * WINDOW, dtype=np.int64)[None, :]
    d = np.maximum(i + WINDOW - j, 0)
    exact = N_BUCKETS // 2
    logv = (np.log(np.maximum(d, 1).astype(np.float32) / np.float32(exact)).astype(np.float32)
            / np.float32(math.log(MAX_DISTANCE / exact)))
    large = np.minimum(exact + (logv * np.float32(N_BUCKETS - exact)).astype(np.int32), N_BUCKETS - 1)
    return np.where(d < exact, d, large).astype(np.int32)


def _sample_bucket_table(bucket):
    out = bucket[:SUBLANES].copy()
    out[:, 2 * WINDOW - SUBLANES:] = bucket[:SUBLANES, WINDOW:WINDOW + SUBLANES]
    return out


def _bias_lookup(rb_ref, bucket, head):
    acc = jnp.zeros(bucket.shape, F32)
    for b in range(N_BUCKETS):
        acc = jnp.where(bucket == b, rb_ref[b, head], acc)
    return acc


def _bias_kernel(rb_ref, bucket_t_ref, bucket_ref, ot_ref, os_ref):
    bucket_t = bucket_t_ref[...]
    kj = lax.broadcasted_iota(jnp.int32, bucket_t.shape, 0)
    qi = lax.broadcasted_iota(jnp.int32, bucket_t.shape, 1)
    dist = qi + WINDOW - kj
    valid = (dist >= 0) & (dist < WINDOW)
    for hk in range(SWA_KV_HEADS):
        for par in range(2):
            for st in range(SWA_GROUP // 2):
                head = hk * SWA_GROUP + 2 * st + par
                gen = jnp.where(valid, _bias_lookup(rb_ref, bucket_t, head) * LOG2_E, NEG_INF)
                ot_ref[1, hk, par, :, st * WINDOW:(st + 1) * WINDOW] = gen
                ot_ref[0, hk, par, :, st * WINDOW:(st + 1) * WINDOW] = jnp.where(kj >= WINDOW, gen, NEG_INF)
    bucket = bucket_ref[...]
    qi = lax.broadcasted_iota(jnp.int32, bucket.shape, 0)
    col = lax.broadcasted_iota(jnp.int32, bucket.shape, 1)
    kj = jnp.where(col < WINDOW, col, col - (WINDOW - SUBLANES))
    dist = qi + WINDOW - kj
    valid = (dist >= 0) & (dist < WINDOW) & ((col < WINDOW) | (col >= 2 * WINDOW - SUBLANES))
    for head in range(SWA_HEADS):
        os_ref[head] = jnp.where(valid, _bias_lookup(rb_ref, bucket, head), NEG_INF)


BIAS_T_SHAPE = (2, SWA_KV_HEADS, 2, 2 * WINDOW, (SWA_GROUP // 2) * WINDOW)
BIAS_S_SHAPE = (SWA_HEADS, SUBLANES, 2 * WINDOW)


def _bias_call(rel_bias):
    bucket = _bucket_table()
    return pl.pallas_call(
        _bias_kernel,
        in_specs=[pl.BlockSpec(memory_space=pltpu.SMEM),
                  pl.BlockSpec((2 * WINDOW, WINDOW), lambda: (0, 0)),
                  pl.BlockSpec((SUBLANES, 2 * WINDOW), lambda: (0, 0))],
        out_specs=[pl.BlockSpec(BIAS_T_SHAPE, lambda: (0,) * len(BIAS_T_SHAPE)),
                   pl.BlockSpec(BIAS_S_SHAPE, lambda: (0,) * len(BIAS_S_SHAPE))],
        out_shape=[jax.ShapeDtypeStruct(BIAS_T_SHAPE, F32), jax.ShapeDtypeStruct(BIAS_S_SHAPE, F32)],
        name="swa_rel_bias_table",
    )(rel_bias, jnp.asarray(np.ascontiguousarray(bucket.T)), jnp.asarray(_sample_bucket_table(bucket)))


def _softmax_sink_parts(s, sink):
    m = jnp.maximum(jnp.max(s, axis=0, keepdims=True), sink)
    p = jnp.exp2(s - m)
    return p, 1.0 / (jnp.sum(p, axis=0, keepdims=True) + jnp.exp2(sink - m))


def _half_lane_variants(full, hk, lo_half):
    rolled = pltpu.roll(full, SWA_HD, axis=1)
    low_src, high_src = (full, rolled) if hk == 0 else (rolled, full)
    return jnp.where(lo_half, low_src, 0.0), jnp.where(lo_half, 0.0, high_src)


def _swa_prompt_kernel(sink_ref, q_ref, kp_ref, kc_ref, vp_ref, vc_ref, bias_ref, o_ref, *, n_blk):
    step = pl.program_id(1)
    lo_half = lax.broadcasted_iota(jnp.int32, (1, 2 * SWA_HD), 1) < SWA_HD
    lo_rows = lax.broadcasted_iota(jnp.int32, (2 * SWA_HD, 1), 0) < SWA_HD
    q = _bf(q_ref[...] * (SWA_HD ** -0.5 * LOG2_E))
    keys = jnp.concatenate([kp_ref[...], kc_ref[...]], axis=0)
    vals = jnp.concatenate([vp_ref[...], vc_ref[...]], axis=0)
    k_var = [[_bf(t) for t in _half_lane_variants(keys, hk, lo_half)] for hk in range(SWA_KV_HEADS)]
    v_var_t = [[_bf(t.T) for t in _half_lane_variants(vals, hk, lo_half)] for hk in range(SWA_KV_HEADS)]
    n_stack = SWA_GROUP // 2
    sinks = [[jnp.concatenate([jnp.full((1, WINDOW), sink_ref[hk * SWA_GROUP + 2 * st + par] * LOG2_E, F32)
                               for st in range(n_stack)], axis=1) for par in range(2)]
             for hk in range(SWA_KV_HEADS)]
    nt_dims = (((1,), (1,)), ((), ()))

    def scores(b):
        rows = slice(b * WINDOW, (b + 1) * WINDOW)
        win = slice(b * WINDOW, (b + 2) * WINDOW)
        variant = jnp.where(step == 0, 0, 1) if b == 0 else 1
        out = []
        for hk in range(SWA_KV_HEADS):
            q2 = jnp.concatenate([q[rows, (hk * n_stack + st) * 2 * SWA_HD:(hk * n_stack + st + 1) * 2 * SWA_HD]
                                  for st in range(n_stack)], axis=0)
            out.append([lax.dot_general(k_var[hk][par][win], q2, nt_dims, preferred_element_type=F32)
                        + bias_ref[variant, hk, par] for par in range(2)])
        return out

    s_next = scores(0)
    for b in range(n_blk):
        s_cur = s_next
        if b + 1 < n_blk:
            s_next = scores(b + 1)
        rows = slice(b * WINDOW, (b + 1) * WINDOW)
        win = slice(b * WINDOW, (b + 2) * WINDOW)
        for hk in range(SWA_KV_HEADS):
            parts = [_softmax_sink_parts(s_cur[hk][par], sinks[hk][par]) for par in range(2)]
            o_t = (jnp.dot(v_var_t[hk][0][:, win], _bf(parts[0][0]), preferred_element_type=F32)
                   + jnp.dot(v_var_t[hk][1][:, win], _bf(parts[1][0]), preferred_element_type=F32))
            o_t = o_t * jnp.where(lo_rows, parts[0][1], parts[1][1])
            for st in range(n_stack):
                lo = (hk * n_stack + st) * 2 * SWA_HD
                o_ref[rows, lo:lo + 2 * SWA_HD] = o_t[:, st * WINDOW:(st + 1) * WINDOW].T


def _swa_prompt_call(sinks, sq, sk, sv, bias, bsz, seq):
    n_blk = SWA_BLOCKS_PER_STEP
    tile = n_blk * WINDOW
    nt = seq // tile
    cur = lambda b, i: (b * nt + i, 0)
    prv = lambda b, i: (b * nt * n_blk + jnp.maximum(i * n_blk - 1, 0), 0)
    return pl.pallas_call(
        functools.partial(_swa_prompt_kernel, n_blk=n_blk),
        grid=(bsz, nt),
        in_specs=[pl.BlockSpec(memory_space=pltpu.SMEM),
                  pl.BlockSpec((tile, SWA_Q), cur),
                  pl.BlockSpec((WINDOW, SWA_KV), prv),
                  pl.BlockSpec((tile, SWA_KV), cur),
                  pl.BlockSpec((WINDOW, SWA_KV), prv),
                  pl.BlockSpec((tile, SWA_KV), cur),
                  pl.BlockSpec(BIAS_T_SHAPE, lambda b, i: (0,) * len(BIAS_T_SHAPE))],
        out_specs=pl.BlockSpec((tile, SWA_Q), cur),
        out_shape=jax.ShapeDtypeStruct((bsz * seq, SWA_Q), F32),
        compiler_params=_cparams(("arbitrary", "arbitrary")),
        name="swa_banded",
    )(sinks, sq, sk, sk, sv, sv, bias)


def _swa_sample_kernel(sink_ref, q_ref, kn_ref, vn_ref, kc_ref, vc_ref, bias_ref, o_ref, ko_ref, vo_ref, *, nb, lt):
    lane = lax.broadcasted_iota(jnp.int32, (1, 2 * SWA_HD), 1)
    lo_half = lane < SWA_HD
    new_lanes = lane >= WINDOW - lt
    q_all = q_ref[...] * (SWA_HD ** -0.5)
    kn_all_t = kn_ref[...].T
    vn_all_t = vn_ref[...].T
    bias = jnp.concatenate([bias_ref[h, 0:lt, :] for h in range(SWA_HEADS)], axis=0)
    sink = jnp.concatenate([jnp.full((lt, 1), sink_ref[h], F32) for h in range(SWA_HEADS)], axis=0)
    nt_dims = (((1,), (1,)), ((), ()))

    def to_half(tile, src_half, dst_half):
        return tile if src_half == dst_half else pltpu.roll(tile, SWA_HD, axis=1)

    lhs, keys, vals = [], [], []
    for s in range(nb):
        rows = slice(s * lt, (s + 1) * lt)
        to_end = (WINDOW - lt - s * lt) % WINDOW
        for c_ref, n_all_t, out_ref, acc in ((kc_ref, kn_all_t, ko_ref, keys), (vc_ref, vn_all_t, vo_ref, vals)):
            cached = c_ref[s]
            new = n_all_t if to_end == 0 else pltpu.roll(n_all_t, to_end, axis=1)
            out_ref[s] = jnp.where(new_lanes, new, pltpu.roll(cached, WINDOW - lt, axis=1))
            acc.append(_bf(jnp.concatenate([cached, jnp.where(new_lanes, new, 0.0)], axis=1)))
        q = q_all[rows]
        tiles = []
        for h in range(SWA_HEADS):
            hk = h // SWA_GROUP
            t = to_half(q[:, (h // 2) * 2 * SWA_HD:(h // 2 + 1) * 2 * SWA_HD], h % 2, hk)
            tiles.append(jnp.where(lo_half if hk == 0 else jnp.logical_not(lo_half), t, 0.0))
        lhs.append(_bf(jnp.concatenate(tiles, axis=0)))
    scores = [jnp.dot(lhs[s], keys[s], preferred_element_type=F32) + bias for s in range(nb)]
    m = [jnp.maximum(jnp.max(sc, axis=-1, keepdims=True), sink) for sc in scores]
    p = [jnp.exp(sc - mm) for sc, mm in zip(scores, m)]
    rinv = [1.0 / (jnp.sum(pp, axis=-1, keepdims=True) + jnp.exp(sink - mm)) for pp, mm in zip(p, m)]
    res = [lax.dot_general(_bf(p[s]), vals[s], nt_dims, preferred_element_type=F32) * rinv[s] for s in range(nb)]
    rows_out = []
    for s in range(nb):
        tiles = []
        for pair in range(SWA_HEADS // 2):
            hk = (2 * pair) // SWA_GROUP
            low = to_half(res[s][(2 * pair) * lt:(2 * pair + 1) * lt], hk, 0)
            high = to_half(res[s][(2 * pair + 1) * lt:(2 * pair + 2) * lt], hk, 1)
            tiles.append(jnp.where(lo_half, low, high))
        rows_out.append(jnp.concatenate(tiles, axis=1))
    o_ref[...] = jnp.concatenate(rows_out, axis=0)


def _swa_sample_call(sinks, sq, sk, sv, cache_k, cache_v, bias, bsz, seq):
    assert seq == SUBLANES, "the sample bias table places one sublane tile of new keys"
    nb = UNIT // seq
    rows = lambda i: (i, 0)
    seqs = lambda i: (i, 0, 0)
    return pl.pallas_call(
        functools.partial(_swa_sample_kernel, nb=nb, lt=seq),
        grid=(bsz // nb,),
        in_specs=[pl.BlockSpec(memory_space=pltpu.SMEM),
                  pl.BlockSpec((UNIT, SWA_Q), rows),
                  pl.BlockSpec((UNIT, SWA_KV), rows),
                  pl.BlockSpec((UNIT, SWA_KV), rows),
                  pl.BlockSpec((nb, SWA_KV, WINDOW), seqs),
                  pl.BlockSpec((nb, SWA_KV, WINDOW), seqs),
                  pl.BlockSpec(BIAS_S_SHAPE, lambda i: (0,) * len(BIAS_S_SHAPE))],
        out_specs=[pl.BlockSpec((UNIT, SWA_Q), rows),
                   pl.BlockSpec((nb, SWA_KV, WINDOW), seqs),
                   pl.BlockSpec((nb, SWA_KV, WINDOW), seqs)],
        out_shape=[jax.ShapeDtypeStruct((bsz * seq, SWA_Q), F32),
                   jax.ShapeDtypeStruct((bsz, SWA_KV, WINDOW), F32),
                   jax.ShapeDtypeStruct((bsz, SWA_KV, WINDOW), F32)],
        compiler_params=_cparams(("arbitrary",)),
        name="swa_cached",
    )(sinks, sq, sk, sv, cache_k, cache_v, bias)


def _post_kernel(*refs, nb, lt, carry, n_sub):
    if carry:
        (odn_ref, oswa_ref, x_ref, mod_ref, wout_ref, nfw_ref, wup_ref, cw_ref, cb_ref, wdn_ref, fnw_ref,
         y_ref, fbuf_ref, h_scr, x1_scr, acc_scr) = refs
        prev_ref = fbuf_ref

        @pl.when(pl.program_id(1) == 0)
        def _():
            fbuf_ref[...] = jnp.zeros_like(fbuf_ref)
    else:
        (odn_ref, oswa_ref, x_ref, mod_ref, wout_ref, nfw_ref, wup_ref, cw_ref, cb_ref, wdn_ref, fnw_ref,
         prev_ref, y_ref, fbuf_ref, h_scr, x1_scr, acc_scr) = refs
    rows = nb * lt
    n_chunks = D_FF // FFN_CHUNK

    def col_slices(c):
        return [slice(base + c * FFN_CHUNK, base + (c + 1) * FFN_CHUNK) for base in (0, D_FF)]

    def prologue(sub):
        rr = slice(sub * rows, (sub + 1) * rows)
        attn = (jnp.dot(_bf(odn_ref[rr, :]), wout_ref[0:DN_V, :], preferred_element_type=F32)
                + jnp.dot(_bf(oswa_ref[rr, :]), wout_ref[DN_V:, :], preferred_element_type=F32))
        x = x_ref[:, sub * lt:(sub + 1) * lt, :].reshape(rows, D_MODEL)
        x1 = x + _rows(mod_ref[:, 2:3, :], nb, lt) * attn
        x1_scr[sub] = x1
        h = (_rms(x1, nfw_ref[...]) * (1.0 + _rows(mod_ref[:, 4:5, :], nb, lt))
             + _rows(mod_ref[:, 3:4, :], nb, lt))
        h_scr[sub] = _bf(h)
        acc_scr[sub] = jnp.zeros((rows, D_MODEL), F32)

    def up_proj(sub, c):
        return [jnp.dot(h_scr[sub], wup_ref[:, cols], preferred_element_type=F32) for cols in col_slices(c)]

    def chunk(sub, c, u_cur):
        halves = []
        for u, cols in zip(u_cur, col_slices(c)):
            prev = prev_ref[:, :, cols]
            halves.append(_causal_conv(u, prev, cw_ref[:, cols], nb, lt) + cb_ref[:, cols])
            fbuf_ref[:, :, cols] = _last_rows(u, nb, lt, FFN_CONV - 1)
        act = _silu(halves[0]) * halves[1]
        acc_scr[sub] += jnp.dot(_bf(act), wdn_ref[c * FFN_CHUNK:(c + 1) * FFN_CHUNK, :],
                                preferred_element_type=F32)

    def epilogue(sub):
        x2 = x1_scr[sub] + _rows(mod_ref[:, 5:6, :], nb, lt) * acc_scr[sub]
        y_ref[:, sub * lt:(sub + 1) * lt, :] = _rms(x2, fnw_ref[...]).reshape(nb, lt, D_MODEL)

    prologue(0)
    for sub in range(n_sub):
        pending = [up_proj(sub, c) for c in range(min(UP_PROJ_AHEAD, n_chunks))]
        for c in range(n_chunks):
            u_cur = pending.pop(0)
            if c + UP_PROJ_AHEAD < n_chunks:
                pending.append(up_proj(sub, c + UP_PROJ_AHEAD))
            chunk(sub, c, u_cur)
            if sub + 1 < n_sub and c == n_chunks - 1 - POST_LOOKAHEAD_CHUNKS:
                prologue(sub + 1)
        epilogue(sub)


def _post_call(o_dn, o_swa, x, mod3, w_out, norm_ffn_w, w_up, conv_w, conv_b, w_down, final_w, nb, lt, state=None):
    carry = state is None
    n_sub = POST_SUB_TILES if carry else 1
    bsz, seq, _ = x.shape
    nt = seq // (lt * n_sub)
    rows = nb * lt
    tile = rows * n_sub
    n_state = _state_rows(nb, lt, FFN_CONV - 1)
    row_map = lambda i, j: (i * nt + j, 0)
    const = lambda i, j: (0, 0)
    in_specs = [pl.BlockSpec((tile, DN_V), row_map),
                pl.BlockSpec((tile, SWA_Q), row_map),
                pl.BlockSpec((nb, lt * n_sub, D_MODEL), lambda i, j: (i, j, 0)),
                pl.BlockSpec((nb, 6, D_MODEL), lambda i, j: (i, 0, 0)),
                pl.BlockSpec((D_MODEL, D_MODEL), const, pipeline_mode=RESIDENT),
                pl.BlockSpec((1, D_MODEL), const),
                pl.BlockSpec(w_up.shape, const, pipeline_mode=RESIDENT),
                pl.BlockSpec((FFN_CONV, 2 * D_FF), const),
                pl.BlockSpec((1, 2 * D_FF), const),
                pl.BlockSpec((D_FF, D_MODEL), const, pipeline_mode=RESIDENT),
                pl.BlockSpec((1, D_MODEL), const)]
    args = [o_dn, o_swa, x, mod3, w_out, norm_ffn_w, w_up, conv_w, conv_b, w_down, final_w]
    if not carry:
        in_specs.append(pl.BlockSpec((nb, FFN_CONV - 1, 2 * D_FF), lambda i, j: (i, 0, 0)))
        args.append(state)
    return pl.pallas_call(
        functools.partial(_post_kernel, nb=nb, lt=lt, carry=carry, n_sub=n_sub),
        grid=(bsz // nb, nt),
        in_specs=in_specs,
        out_specs=[pl.BlockSpec((nb, lt * n_sub, D_MODEL), lambda i, j: (i, j, 0)),
                   pl.BlockSpec((nb, n_state, 2 * D_FF), lambda i, j: (i, 0, 0))],
        out_shape=[jax.ShapeDtypeStruct((bsz, seq, D_MODEL), F32),
                   jax.ShapeDtypeStruct((bsz, n_state, 2 * D_FF), F32)],
        scratch_shapes=[pltpu.VMEM((n_sub, rows, D_MODEL), BF16),
                        pltpu.VMEM((n_sub, rows, D_MODEL), F32),
                        pltpu.VMEM((n_sub, rows, D_MODEL), F32)],
        compiler_params=_cparams(("arbitrary", "arbitrary")),
        name="out_proj_convffn_carry" if carry else "out_proj_convffn_state",
    )(*args)


def _pad_row(vec, offset):
    return jnp.zeros((1, BA_PAD), F32).at[0, offset:offset + vec.shape[0]].set(vec)


def kernel(x_prompt, x_sample, state_dn_conv, state_dn_ssm, cache_swa_k, cache_swa_v, state_ffn_conv, c_prompt, c_sample, rel_bias, final_norm_w, w_ada, b_ada, norm_mix_w, w_in, dn_conv_w, dn_A_log, dn_dt_bias, dn_norm_w, swa_sinks, w_out, norm_ffn_w, ffn_w_up, ffn_conv_w, ffn_conv_b, ffn_w_down):
    bp, lp, _ = x_prompt.shape
    bs, ls, _ = x_sample.shape
    layer = 0

    w_in_t = jnp.transpose(w_in[layer])
    w_out_b = w_out[layer].astype(BF16)
    w_up_b = ffn_w_up[layer].astype(BF16)
    w_dn_b = ffn_w_down[layer].astype(BF16)
    alog_row = _pad_row(dn_A_log[layer], DN_HEADS)
    dt_row = _pad_row(dn_dt_bias[layer], DN_HEADS)
    row = lambda v: v.reshape(1, -1)

    n_c = bp + bs
    n_c_pad = -(-n_c // SUBLANES) * SUBLANES
    c_all = jnp.pad(jnp.concatenate([c_sample, c_prompt], axis=0), ((0, n_c_pad - n_c), (0, 0)))
    mod_s, mod_p = _mod_call(c_all, w_ada[layer], row(b_ada[layer]), bs, bp)

    bias_t, bias_s = _bias_call(rel_bias)
    sinks = swa_sinks[layer]

    def post(o_dn, o_swa, x, mod3, nb, lt, state=None):
        return _post_call(o_dn, o_swa, x, mod3, w_out_b, row(norm_ffn_w[layer]), w_up_b, ffn_conv_w[layer],
                          row(ffn_conv_b[layer]), w_dn_b, row(final_norm_w), nb, lt, state)

    dn_args = (alog_row, dt_row, row(dn_norm_w[layer]))

    sq, sk, sv, p_tail, o_dn, p_ssm = _mix_call(x_prompt, mod_p, row(norm_mix_w[layer]), w_in_t,
                                                dn_conv_w[layer], *dn_args)
    o_swa = _swa_prompt_call(sinks, sq, sk, sv, bias_t, bp, lp)
    y_prompt, p_ffn_tail = post(o_dn, o_swa, x_prompt, mod_p, 1, SUB_ROWS)
    p_dn_conv = p_tail[:, SUBLANES - (DN_CONV - 1):]
    p_fbuf = p_ffn_tail[:, SUBLANES - (FFN_CONV - 1):]
    last_window = lambda t: t.reshape(bp, lp, SWA_KV)[:, lp - WINDOW:].reshape(bp, WINDOW, SWA_KV_HEADS, SWA_HD)
    p_swa_k = last_window(sk)
    p_swa_v = last_window(sv)

    cache_t = lambda c: jnp.transpose(c[layer].reshape(bs, WINDOW, SWA_KV), (0, 2, 1))
    cache_from_t = lambda c: jnp.transpose(c, (0, 2, 1)).reshape(bs, WINDOW, SWA_KV_HEADS, SWA_HD)
    qkv_s, z_s, ba_s, sq_s, sk_s, sv_s, s_dn_conv = _in_call(
        x_sample, mod_s, row(norm_mix_w[layer]), w_in_t, dn_conv_w[layer], SAMPLE_IN_SEQS, state_dn_conv[layer])
    o_dn_s, s_ssm = _dn_call(qkv_s, z_s, ba_s, *dn_args, bs, ls, state=state_dn_ssm[layer])
    o_swa_s, s_k, s_v = _swa_sample_call(sinks, sq_s, sk_s, sv_s,
                                         cache_t(cache_swa_k), cache_t(cache_swa_v), bias_s, bs, ls)
    y_sample, s_fbuf = post(o_dn_s, o_swa_s, x_sample, mod_s, SAMPLE_POST_SEQS, ls, state=state_ffn_conv[layer])

    return (y_prompt, y_sample, p_dn_conv[None], s_dn_conv[None], p_ssm[None], s_ssm[None],
            p_swa_k[None], cache_from_t(s_k)[None],
            p_swa_v[None], cache_from_t(s_v)[None],
            p_fbuf[None], s_fbuf[None])
```
